```python
import math
import jax, jax.numpy as jnp
from jax import lax
import numpy as np

D_MODEL = 1024
BATCH = 16
SEQ = 2048
DEPTH = 2

N_META = 16
BLOCK = 128
WINDOW = 128
A_HEADS = 8
A_KV_HEADS = 2
A_HEAD_DIM = 64
A_WIDTH = A_HEADS * A_HEAD_DIM
B_HEADS = 8
B_NOPE_DIM = 64
B_ROPE_DIM = 32
B_V_DIM = 64
B_WIDTH = B_HEADS * B_V_DIM
Q_LORA_RANK = 256
KV_LORA_RANK = 128
MIX_WIDTH = A_WIDTH + B_WIDTH
IN_SIZES = (A_WIDTH, A_KV_HEADS * A_HEAD_DIM, A_KV_HEADS * A_HEAD_DIM, A_WIDTH,
            Q_LORA_RANK, KV_LORA_RANK, B_ROPE_DIM, B_WIDTH)
IN_WIDTH = 2 * A_WIDTH + 2 * A_KV_HEADS * A_HEAD_DIM + Q_LORA_RANK + KV_LORA_RANK + B_ROPE_DIM + B_WIDTH
N_BUCKETS = 32
MAX_DISTANCE = 128
ROPE_THETA = 10000.0
EPS = 1e-6

kernel_name = "hymba_swa_mla_hybrid_encoder"


def rms_norm(x, g):
    xf = x.astype(jnp.float32)
    y = xf * lax.rsqrt(jnp.mean(xf * xf, axis=-1, keepdims=True) + EPS)
    return (y * g.astype(jnp.float32)).astype(x.dtype)


def t5_bucket(rel):
    nb = N_BUCKETS // 2
    max_exact = nb // 2
    ret = jnp.where(rel > 0, nb, 0)
    n = jnp.abs(rel)
    nf = jnp.maximum(n, 1).astype(jnp.float32)
    large = max_exact + (jnp.log(nf / max_exact) / math.log(MAX_DISTANCE / max_exact)
                         * (nb - max_exact)).astype(jnp.int32)
    large = jnp.minimum(large, nb - 1)
    return ret + jnp.where(n < max_exact, n, large)


def rel_bias(table, q_pos, k_pos):
    b = t5_bucket(k_pos[..., None, :] - q_pos[..., :, None])
    return jnp.moveaxis(table.astype(jnp.float32)[b], -1, -3)


def softmax_with_sink(logits, sink, mask):
    logits = jnp.where(mask, logits, -jnp.inf)
    s = sink.astype(jnp.float32)[:, None, None]
    m = jnp.maximum(jnp.max(logits, axis=-1, keepdims=True), s)
    e = jnp.exp(logits - m)
    return e / (jnp.sum(e, axis=-1, keepdims=True) + jnp.exp(s - m))


def apply_rope(x, cos, sin):
    x1, x2 = jnp.split(x.astype(jnp.float32), 2, axis=-1)
    return jnp.concatenate([x1 * cos - x2 * sin, x2 * cos + x1 * sin], axis=-1).astype(x.dtype)


def window_attention(q, k, v, sink, table):
    B, L = q.shape[0], q.shape[1]
    S = L - N_META
    nb = S // BLOCK
    g = A_HEADS // A_KV_HEADS
    dh = A_HEAD_DIM
    scale = dh ** -0.5
    q = q.reshape(B, L, A_KV_HEADS, g, dh)
    qm, qr = q[:, :N_META], q[:, N_META:]
    km, kr = k[:, :N_META], k[:, N_META:]
    vm, vr = v[:, :N_META], v[:, N_META:]

    qb = qr.reshape(B, nb, BLOCK, A_KV_HEADS, g, dh)
    pad = ((0, 0), (BLOCK, BLOCK), (0, 0), (0, 0))

    def band(t, tm):
        tp = jnp.pad(t, pad).reshape(B, nb + 2, BLOCK, A_KV_HEADS, dh)
        tb = jnp.concatenate([tp[:, :-2], tp[:, 1:-1], tp[:, 2:]], axis=2)
        tmb = jnp.broadcast_to(tm[:, None], (B, nb, N_META, A_KV_HEADS, dh))
        return jnp.concatenate([tmb, tb], axis=2)

    kb, vb = band(kr, km), band(vr, vm)
    blk = jnp.arange(nb)[:, None]
    q_idx = blk * BLOCK + jnp.arange(BLOCK)[None, :]
    k_idx = (blk - 1) * BLOCK + jnp.arange(3 * BLOCK)[None, :]
    valid = ((k_idx[:, None, :] >= 0) & (k_idx[:, None, :] < S)
             & (jnp.abs(q_idx[:, :, None] - k_idx[:, None, :]) <= WINDOW))
    mask = jnp.concatenate([jnp.ones((nb, BLOCK, N_META), bool), valid], axis=-1)
    q_pos = N_META + q_idx
    k_pos = jnp.concatenate([jnp.broadcast_to(jnp.arange(N_META)[None], (nb, N_META)), N_META + k_idx], axis=-1)
    bias = rel_bias(table, q_pos, k_pos)
    logits = jnp.einsum('bnqhgd,bnkhd->bnhgqk', qb, kb).astype(jnp.float32) * scale
    logits = logits.reshape(B, nb, A_HEADS, BLOCK, N_META + 3 * BLOCK) + bias[None]
    p = softmax_with_sink(logits, sink, mask[None, :, None]).astype(v.dtype)
    p = p.reshape(B, nb, A_KV_HEADS, g, BLOCK, N_META + 3 * BLOCK)
    out_r = jnp.einsum('bnhgqk,bnkhd->bnqhgd', p, vb).reshape(B, S, A_WIDTH)

    km2 = jnp.concatenate([km, kr[:, :BLOCK]], axis=1)
    vm2 = jnp.concatenate([vm, vr[:, :BLOCK]], axis=1)
    qp = jnp.arange(N_META)
    kp = jnp.arange(N_META + BLOCK)
    mmask = jnp.abs(kp[None, :] - qp[:, None]) <= WINDOW
    mbias = rel_bias(table, qp, kp)
    ml = jnp.einsum('bqhgd,bkhd->bhgqk', qm, km2).astype(jnp.float32) * scale
    ml = ml.reshape(B, A_HEADS, N_META, N_META + BLOCK) + mbias[None]
    mp = softmax_with_sink(ml, sink, mmask).astype(v.dtype).reshape(B, A_KV_HEADS, g, N_META, N_META + BLOCK)
    out_m = jnp.einsum('bhgqk,bkhd->bqhgd', mp, vm2).reshape(B, N_META, A_WIDTH)
    return jnp.concatenate([out_m, out_r], axis=1)


def mla_attention(q_nope, q_rope, k_nope, k_rope, v):
    B, L = q_nope.shape[0], q_nope.shape[1]
    S = L - N_META
    nb = S // BLOCK
    scale = (B_NOPE_DIM + B_ROPE_DIM) ** -0.5

    def attend(qn, qr):
        logits = (jnp.einsum('bqhd,bkhd->bhqk', qn, k_nope)
                  + jnp.einsum('bqhd,bkd->bhqk', qr, k_rope)).astype(jnp.float32) * scale
        p = jax.nn.softmax(logits, axis=-1).astype(v.dtype)
        return jnp.einsum('bhqk,bkhd->bqhd', p, v)

    out_m = attend(q_nope[:, :N_META], q_rope[:, :N_META]).reshape(B, N_META, B_WIDTH)
    qn_b = jnp.moveaxis(q_nope[:, N_META:].reshape(B, nb, BLOCK, B_HEADS, B_NOPE_DIM), 1, 0)
    qr_b = jnp.moveaxis(q_rope[:, N_META:].reshape(B, nb, BLOCK, B_HEADS, B_ROPE_DIM), 1, 0)
    out_r = lax.map(lambda a: attend(a[0], a[1]), (qn_b, qr_b))
    out_r = jnp.moveaxis(out_r, 0, 1).reshape(B, S, B_WIDTH)
    return jnp.concatenate([out_m, out_r], axis=1)


def hybrid_layer(h, norm_in, w_in, sink_a, norm_q_lat, w_uq, norm_kv_lat, w_ukv,
                 norm_out_a, norm_out_b, w_out, rel_table, cos, sin):
    B, L = h.shape[0], h.shape[1]
    u = rms_norm(h, norm_in)
    proj = jnp.einsum('bld,de->ble', u, w_in)
    offsets = np.cumsum(IN_SIZES)[:-1].tolist()
    qa, ka, va, ga, cq, ckv, kr, gb = jnp.split(proj, offsets, axis=-1)

    ya = window_attention(qa.reshape(B, L, A_HEADS, A_HEAD_DIM),
                          ka.reshape(B, L, A_KV_HEADS, A_HEAD_DIM),
                          va.reshape(B, L, A_KV_HEADS, A_HEAD_DIM), sink_a, rel_table)

    q = jnp.einsum('blr,re->ble', rms_norm(cq, norm_q_lat), w_uq).reshape(B, L, B_HEADS, B_NOPE_DIM + B_ROPE_DIM)
    q_nope, q_rope = q[..., :B_NOPE_DIM], q[..., B_NOPE_DIM:]
    q_rope = apply_rope(q_rope, cos[:, None], sin[:, None])
    kv = jnp.einsum('blr,re->ble', rms_norm(ckv, norm_kv_lat), w_ukv).reshape(B, L, B_HEADS, B_NOPE_DIM + B_V_DIM)
    k_nope, vb = kv[..., :B_NOPE_DIM], kv[..., B_NOPE_DIM:]
    k_rope = apply_rope(kr, cos, sin)
    yb = mla_attention(q_nope, q_rope, k_nope, k_rope, vb)

    y = jnp.concatenate([rms_norm(ya, norm_out_a) * jax.nn.silu(ga),
                         rms_norm(yb, norm_out_b) * jax.nn.silu(gb)], axis=-1)
    return h + jnp.einsum('ble,ed->bld', y, w_out)


def _fwd_setup_inputs(seed: int = 0) -> dict:
    key = jax.random.key(seed)
    ks = jax.random.split(key, 16)
    f32 = jnp.float32
    nrm = lambda k, s, sc: jax.random.normal(k, s, f32) * sc
    gain = lambda k, s: 1.0 + 0.1 * jax.random.normal(k, s, f32)
    return {
        "x": nrm(ks[0], (BATCH, SEQ, D_MODEL), 1.0),
        "meta_tokens": nrm(ks[1], (N_META, D_MODEL), 1.0),
        "rel_bias_table": nrm(ks[2], (N_BUCKETS, A_HEADS), 0.5),
        "norm_in": gain(ks[3], (DEPTH, D_MODEL)),
        "w_in": nrm(ks[4], (DEPTH, D_MODEL, IN_WIDTH), D_MODEL ** -0.5),
        "sink_a": nrm(ks[5], (DEPTH, A_HEADS), 0.5),
        "norm_q_lat": gain(ks[6], (DEPTH, Q_LORA_RANK)),
        "w_uq": nrm(ks[7], (DEPTH, Q_LORA_RANK, B_HEADS * (B_NOPE_DIM + B_ROPE_DIM)), Q_LORA_RANK ** -0.5),
        "norm_kv_lat": gain(ks[8], (DEPTH, KV_LORA_RANK)),
        "w_ukv": nrm(ks[9], (DEPTH, KV_LORA_RANK, B_HEADS * (B_NOPE_DIM + B_V_DIM)), KV_LORA_RANK ** -0.5),
        "norm_out_a": gain(ks[10], (DEPTH, A_WIDTH)),
        "norm_out_b": gain(ks[11], (DEPTH, B_WIDTH)),
        "w_out": nrm(ks[12], (DEPTH, MIX_WIDTH, D_MODEL), MIX_WIDTH ** -0.5),
        "norm_final": gain(ks[13], (D_MODEL,)),
    }


def _fwd_reference(x, meta_tokens, rel_bias_table, norm_in, w_in, sink_a, norm_q_lat, w_uq,
              norm_kv_lat, w_ukv, norm_out_a, norm_out_b, w_out, norm_final):
    B = x.shape[0]
    meta = jnp.broadcast_to(meta_tokens.astype(x.dtype)[None], (B, N_META, D_MODEL))
    h = jnp.concatenate([meta, x], axis=1)
    L = h.shape[1]
    half = B_ROPE_DIM // 2
    freqs = ROPE_THETA ** (-jnp.arange(half, dtype=jnp.float32) / half)
    ang = jnp.arange(L, dtype=jnp.float32)[:, None] * freqs[None, :]
    cos, sin = jnp.cos(ang), jnp.sin(ang)
    for i in range(DEPTH):
        h = hybrid_layer(h, norm_in[i], w_in[i], sink_a[i], norm_q_lat[i], w_uq[i], norm_kv_lat[i],
                         w_ukv[i], norm_out_a[i], norm_out_b[i], w_out[i], rel_bias_table, cos, sin)
    return rms_norm(h[:, N_META:], norm_final)


import jax as _jax
import jax.numpy as _jnp

TWIN_FORMAT = 'train_step'
FWD_PARAMS = ['x', 'meta_tokens', 'rel_bias_table', 'norm_in', 'w_in', 'sink_a', 'norm_q_lat', 'w_uq', 'norm_kv_lat', 'w_ukv', 'norm_out_a', 'norm_out_b', 'w_out', 'norm_final']
TWIN_WEIGHTS = ['meta_tokens', 'rel_bias_table', 'norm_in', 'w_in', 'sink_a', 'norm_q_lat', 'w_uq', 'norm_kv_lat', 'w_ukv', 'norm_out_a', 'norm_out_b', 'w_out', 'norm_final']
TWIN_DIFF_INPUT = 'x'
TWIN_INPUTS = ['x', 'meta_tokens', 'rel_bias_table', 'norm_in', 'w_in', 'sink_a', 'norm_q_lat', 'w_uq', 'norm_kv_lat', 'w_ukv', 'norm_out_a', 'norm_out_b', 'w_out', 'norm_final', 'loss_target', 'm_meta_tokens', 'm_rel_bias_table', 'm_norm_in', 'm_w_in', 'm_sink_a', 'm_norm_q_lat', 'm_w_uq', 'm_norm_kv_lat', 'm_w_ukv', 'm_norm_out_a', 'm_norm_out_b', 'm_w_out', 'm_norm_final', 'v_meta_tokens', 'v_rel_bias_table', 'v_norm_in', 'v_w_in', 'v_sink_a', 'v_norm_q_lat', 'v_w_uq', 'v_norm_kv_lat', 'v_w_ukv', 'v_norm_out_a', 'v_norm_out_b', 'v_w_out', 'v_norm_final']
TWIN_OUTPUTS = ['loss', 'grad_x', 'grad_meta_tokens', 'grad_rel_bias_table', 'grad_norm_in', 'grad_w_in', 'grad_sink_a', 'grad_norm_q_lat', 'grad_w_uq', 'grad_norm_kv_lat', 'grad_w_ukv', 'grad_norm_out_a', 'grad_norm_out_b', 'grad_w_out', 'grad_norm_final', 'delta_meta_tokens', 'delta_rel_bias_table', 'delta_norm_in', 'delta_w_in', 'delta_sink_a', 'delta_norm_q_lat', 'delta_w_uq', 'delta_norm_kv_lat', 'delta_w_ukv', 'delta_norm_out_a', 'delta_norm_out_b', 'delta_w_out', 'delta_norm_final', 'new_m_meta_tokens', 'new_m_rel_bias_table', 'new_m_norm_in', 'new_m_w_in', 'new_m_sink_a', 'new_m_norm_q_lat', 'new_m_w_uq', 'new_m_norm_kv_lat', 'new_m_w_ukv', 'new_m_norm_out_a', 'new_m_norm_out_b', 'new_m_w_out', 'new_m_norm_final', 'new_v_meta_tokens', 'new_v_rel_bias_table', 'new_v_norm_in', 'new_v_w_in', 'new_v_sink_a', 'new_v_norm_q_lat', 'new_v_w_uq', 'new_v_norm_kv_lat', 'new_v_w_ukv', 'new_v_norm_out_a', 'new_v_norm_out_b', 'new_v_w_out', 'new_v_norm_final']
TWIN_LEAF_KINDS = {'loss': 'loss', 'grad_x': 'grad_x', 'grad_meta_tokens': 'grad_w', 'grad_rel_bias_table': 'grad_w', 'grad_norm_in': 'grad_w', 'grad_w_in': 'grad_w', 'grad_sink_a': 'grad_w', 'grad_norm_q_lat': 'grad_w', 'grad_w_uq': 'grad_w', 'grad_norm_kv_lat': 'grad_w', 'grad_w_ukv': 'grad_w', 'grad_norm_out_a': 'grad_w', 'grad_norm_out_b': 'grad_w', 'grad_w_out': 'grad_w', 'grad_norm_final': 'grad_w', 'delta_meta_tokens': 'delta_w', 'delta_rel_bias_table': 'delta_w', 'delta_norm_in': 'delta_w', 'delta_w_in': 'delta_w', 'delta_sink_a': 'delta_w', 'delta_norm_q_lat': 'delta_w', 'delta_w_uq': 'delta_w', 'delta_norm_kv_lat': 'delta_w', 'delta_w_ukv': 'delta_w', 'delta_norm_out_a': 'delta_w', 'delta_norm_out_b': 'delta_w', 'delta_w_out': 'delta_w', 'delta_norm_final': 'delta_w', 'new_m_meta_tokens': 'new_m', 'new_m_rel_bias_table': 'new_m', 'new_m_norm_in': 'new_m', 'new_m_w_in': 'new_m', 'new_m_sink_a': 'new_m', 'new_m_norm_q_lat': 'new_m', 'new_m_w_uq': 'new_m', 'new_m_norm_kv_lat': 'new_m', 'new_m_w_ukv': 'new_m', 'new_m_norm_out_a': 'new_m', 'new_m_norm_out_b': 'new_m', 'new_m_w_out': 'new_m', 'new_m_norm_final': 'new_m', 'new_v_meta_tokens': 'new_v', 'new_v_rel_bias_table': 'new_v', 'new_v_norm_in': 'new_v', 'new_v_w_in': 'new_v', 'new_v_sink_a': 'new_v', 'new_v_norm_q_lat': 'new_v', 'new_v_w_uq': 'new_v', 'new_v_norm_kv_lat': 'new_v', 'new_v_w_ukv': 'new_v', 'new_v_norm_out_a': 'new_v', 'new_v_norm_out_b': 'new_v', 'new_v_w_out': 'new_v', 'new_v_norm_final': 'new_v'}


def _forward(args):
    return _fwd_reference(*[args[k] for k in FWD_PARAMS])


def _output_shape():
    out = _jax.eval_shape(lambda: _forward(_fwd_setup_inputs(0)))
    return out.shape, out.dtype

N_MICROBATCH = 1
ADAM_LR = 0.001
ADAM_B1 = 0.9
ADAM_B2 = 0.999
ADAM_EPS = 1e-08
ADAM_WD = 0.01
ADAM_STEP = 10
PER_EXAMPLE_BATCH_AXIS = {'x': 0, 'loss_target': 0}
SHARED_INPUTS = []
_WEIGHT_DTYPES = {'meta_tokens': _jnp.float32, 'rel_bias_table': _jnp.float32, 'norm_in': _jnp.float32, 'w_in': _jnp.float32, 'sink_a': _jnp.float32, 'norm_q_lat': _jnp.float32, 'w_uq': _jnp.float32, 'norm_kv_lat': _jnp.float32, 'w_ukv': _jnp.float32, 'norm_out_a': _jnp.float32, 'norm_out_b': _jnp.float32, 'w_out': _jnp.float32, 'norm_final': _jnp.float32}
MOMENT_SCALE = {'meta_tokens': 1.242712e-02, 'rel_bias_table': 1.691199e-01, 'norm_in': 1.854253e-01, 'w_in': 1.249310e-01, 'sink_a': 4.049659e-03, 'norm_q_lat': 1.440510e-01, 'w_uq': 7.780456e-02, 'norm_kv_lat': 3.408130e-01, 'w_ukv': 1.004968e-01, 'norm_out_a': 1.107650e-01, 'norm_out_b': 1.075859e-01, 'w_out': 1.064238e-01, 'norm_final': 3.215852e+01}


def _to_microbatches(a, axis):
    t = _jnp.moveaxis(a, axis, 0)
    t = t.reshape((N_MICROBATCH, t.shape[0] // N_MICROBATCH) + t.shape[1:])
    return _jnp.moveaxis(t, 1, axis + 1)


def setup_inputs(seed: int = 0) -> dict:
    inp = _fwd_setup_inputs(seed)
    key = _jax.random.fold_in(_jax.random.key(seed), 7919)
    shape, _ = _output_shape()
    out = dict(inp)
    out["loss_target"] = _jax.random.normal(_jax.random.fold_in(key, 0), shape, _jnp.float32)
    for i, name in enumerate(TWIN_WEIGHTS):
        w = inp[name].astype(_jnp.float32)
        if MOMENT_SCALE is None:
            s = _jnp.sqrt(_jnp.mean(_jnp.square(w)) + 1e-30)
        else:
            s = MOMENT_SCALE[name]
        km, kv = _jax.random.split(_jax.random.fold_in(key, i + 1))
        out[name] = w
        out["m_" + name] = s * _jax.random.normal(km, w.shape, _jnp.float32)
        out["v_" + name] = (s * s) * _jax.random.uniform(kv, w.shape, _jnp.float32, 0.5, 1.5)
    if N_MICROBATCH > 1:
        for name, axis in PER_EXAMPLE_BATCH_AXIS.items():
            out[name] = _to_microbatches(out[name], axis)
    return {'x': out['x'], 'meta_tokens': out['meta_tokens'], 'rel_bias_table': out['rel_bias_table'], 'norm_in': out['norm_in'], 'w_in': out['w_in'], 'sink_a': out['sink_a'], 'norm_q_lat': out['norm_q_lat'], 'w_uq': out['w_uq'], 'norm_kv_lat': out['norm_kv_lat'], 'w_ukv': out['w_ukv'], 'norm_out_a': out['norm_out_a'], 'norm_out_b': out['norm_out_b'], 'w_out': out['w_out'], 'norm_final': out['norm_final'], 'loss_target': out['loss_target'], 'm_meta_tokens': out['m_meta_tokens'], 'm_rel_bias_table': out['m_rel_bias_table'], 'm_norm_in': out['m_norm_in'], 'm_w_in': out['m_w_in'], 'm_sink_a': out['m_sink_a'], 'm_norm_q_lat': out['m_norm_q_lat'], 'm_w_uq': out['m_w_uq'], 'm_norm_kv_lat': out['m_norm_kv_lat'], 'm_w_ukv': out['m_w_ukv'], 'm_norm_out_a': out['m_norm_out_a'], 'm_norm_out_b': out['m_norm_out_b'], 'm_w_out': out['m_w_out'], 'm_norm_final': out['m_norm_final'], 'v_meta_tokens': out['v_meta_tokens'], 'v_rel_bias_table': out['v_rel_bias_table'], 'v_norm_in': out['v_norm_in'], 'v_w_in': out['v_w_in'], 'v_sink_a': out['v_sink_a'], 'v_norm_q_lat': out['v_norm_q_lat'], 'v_w_uq': out['v_w_uq'], 'v_norm_kv_lat': out['v_norm_kv_lat'], 'v_w_ukv': out['v_w_ukv'], 'v_norm_out_a': out['v_norm_out_a'], 'v_norm_out_b': out['v_norm_out_b'], 'v_w_out': out['v_w_out'], 'v_norm_final': out['v_norm_final']}


def _loss(weights, diff, rest, loss_target):
    with _jax.named_scope("forward"):
        args = {**rest, TWIN_DIFF_INPUT: diff, **{k: w.astype(_WEIGHT_DTYPES[k]) for k, w in weights.items()}}
        y = _forward(args)
    with _jax.named_scope("loss_head"):
        err = _jnp.square(y.astype(_jnp.float32) - loss_target)
        return 0.5 * _jnp.sum(_jnp.mean(err, axis=-1)) if err.ndim else 0.5 * err


def _adamw(w, g, m, v):
    m = ADAM_B1 * m + (1.0 - ADAM_B1) * g
    v = ADAM_B2 * v + (1.0 - ADAM_B2) * _jnp.square(g)
    m_hat = m / (1.0 - ADAM_B1 ** ADAM_STEP)
    v_hat = v / (1.0 - ADAM_B2 ** ADAM_STEP)
    delta = -ADAM_LR * (m_hat / (_jnp.sqrt(v_hat) + ADAM_EPS) + ADAM_WD * w)
    return delta, m, v


def reference(x, meta_tokens, rel_bias_table, norm_in, w_in, sink_a, norm_q_lat, w_uq, norm_kv_lat, w_ukv, norm_out_a, norm_out_b, w_out, norm_final, loss_target, m_meta_tokens, m_rel_bias_table, m_norm_in, m_w_in, m_sink_a, m_norm_q_lat, m_w_uq, m_norm_kv_lat, m_w_ukv, m_norm_out_a, m_norm_out_b, m_w_out, m_norm_final, v_meta_tokens, v_rel_bias_table, v_norm_in, v_w_in, v_sink_a, v_norm_q_lat, v_w_uq, v_norm_kv_lat, v_w_ukv, v_norm_out_a, v_norm_out_b, v_w_out, v_norm_final):
    given = dict(x=x, meta_tokens=meta_tokens, rel_bias_table=rel_bias_table, norm_in=norm_in, w_in=w_in, sink_a=sink_a, norm_q_lat=norm_q_lat, w_uq=w_uq, norm_kv_lat=norm_kv_lat, w_ukv=w_ukv, norm_out_a=norm_out_a, norm_out_b=norm_out_b, w_out=w_out, norm_final=norm_final, loss_target=loss_target, m_meta_tokens=m_meta_tokens, m_rel_bias_table=m_rel_bias_table, m_norm_in=m_norm_in, m_w_in=m_w_in, m_sink_a=m_sink_a, m_norm_q_lat=m_norm_q_lat, m_w_uq=m_w_uq, m_norm_kv_lat=m_norm_kv_lat, m_w_ukv=m_w_ukv, m_norm_out_a=m_norm_out_a, m_norm_out_b=m_norm_out_b, m_w_out=m_w_out, m_norm_final=m_norm_final, v_meta_tokens=v_meta_tokens, v_rel_bias_table=v_rel_bias_table, v_norm_in=v_norm_in, v_w_in=v_w_in, v_sink_a=v_sink_a, v_norm_q_lat=v_norm_q_lat, v_w_uq=v_w_uq, v_norm_kv_lat=v_norm_kv_lat, v_w_ukv=v_w_ukv, v_norm_out_a=v_norm_out_a, v_norm_out_b=v_norm_out_b, v_w_out=v_w_out, v_norm_final=v_norm_final)
    weights = {n: given[n] for n in TWIN_WEIGHTS}
    shared = {n: given[n] for n in SHARED_INPUTS}
    per_example = {n: given[n] for n in ['x']}
    grad_fn = _jax.value_and_grad(_loss, argnums=(0, 1))

    def one_microbatch(ex, loss_target):
        ex = dict(ex)
        diff = ex.pop(TWIN_DIFF_INPUT)
        return grad_fn(weights, diff, {**shared, **ex}, loss_target)

    if N_MICROBATCH == 1:
        loss, (grad_w, grad_x) = one_microbatch(per_example, given["loss_target"])
    else:
        def body(carry, xs):
            loss_sum, grad_sum = carry
            l_k, (gw_k, gx_k) = one_microbatch(xs[0], xs[1])
            with _jax.named_scope("update"):
                return (loss_sum + l_k, _jax.tree.map(_jnp.add, grad_sum, gw_k)), gx_k

        init = (_jnp.zeros((), _jnp.float32), _jax.tree.map(_jnp.zeros_like, weights))
        (loss, grad_w), grad_x = _jax.lax.scan(body, init, (per_example, given["loss_target"]))
    with _jax.named_scope("update"):
        delta_w, new_m, new_v = {}, {}, {}
        for n in TWIN_WEIGHTS:
            delta_w[n], new_m[n], new_v[n] = _adamw(weights[n], grad_w[n], given["m_" + n], given["v_" + n])
    return (loss, grad_x, *[grad_w[n] for n in TWIN_WEIGHTS], *[delta_w[n] for n in TWIN_WEIGHTS],
            *[new_m[n] for n in TWIN_WEIGHTS], *[new_v[n] for n in TWIN_WEIGHTS])
```

```python
import collections
import functools
import math

import jax
import jax.numpy as jnp
import numpy as np
from jax import lax
from jax.experimental import pallas as pl
from jax.experimental.pallas import tpu as pltpu

F32 = jnp.float32
BF16 = jnp.bfloat16

BLK = 128
N_META = 16
D_MODEL = 1024
A_HEADS, A_KV, A_DH = 8, 2, 64
B_HEADS, B_NOPE, B_ROPE, B_DV = 8, 64, 32, 64
Q_RANK, KV_RANK = 256, 128
N_BUCKETS, MAX_DIST = 32, 128
ROPE_THETA = 10000.0
EPS = 1e-6
IN_WIDTH = 2208
W_IN_P = 2304
NEG = -1e30
VMEM_LIMIT = 48 * 1024 * 1024

ADAM_LR, ADAM_B1, ADAM_B2, ADAM_EPS, ADAM_WD, ADAM_STEP = 0.001, 0.9, 0.999, 1e-08, 0.01, 10

Cfg = collections.namedtuple("Cfg", "B S NB NJ LP TP")


def make_cfg(batch, seq):
    nb = seq // BLK
    nj = nb + 1
    return Cfg(batch, seq, nb, nj, nj * BLK, batch * nj * BLK)


def _cp(*sem):
    return pltpu.CompilerParams(dimension_semantics=sem, vmem_limit_bytes=VMEM_LIMIT)


def _dot(a, b):
    return jnp.dot(a, b, preferred_element_type=F32)


def _dot_nt(a, b):
    return lax.dot_general(a, b, (((1,), (1,)), ((), ())), preferred_element_type=F32)


def _dot_tn(a, b):
    return lax.dot_general(a, b, (((0,), (0,)), ((), ())), preferred_element_type=F32)


def _rms(x, width=None):
    n = x.shape[-1] if width is None else width
    r = lax.rsqrt(jnp.sum(x * x, axis=-1, keepdims=True) * (1.0 / n) + EPS)
    return x * r, r


def _rms_bwd(xhat, r, t):
    n = xhat.shape[-1]
    return r * (t - xhat * (jnp.sum(t * xhat, axis=-1, keepdims=True) * (1.0 / n)))


def _sigmoid(x):
    return 1.0 / (1.0 + jnp.exp(-x))


def _lane(shape):
    return lax.broadcasted_iota(jnp.int32, shape, len(shape) - 1)


def _swap_rope(x):
    n = x.shape[-1]
    lane = _lane(x.shape) % BLK
    up = pltpu.roll(x, n - 16, axis=x.ndim - 1)
    dn = pltpu.roll(x, 16, axis=x.ndim - 1)
    return jnp.where((lane >= 64) & (lane < 80), up, jnp.where((lane >= 80) & (lane < 96), dn, 0.0))


A_ORDER = (0, 4, 1, 5, 2, 6, 3, 7)


def _jtype(j, nj):
    return 0 if j == 0 else 1 if j == 1 else 3 if j == nj - 1 else 2


def _window_structure(nj):
    def pos(blk, r):
        return np.where(blk == 0, r, N_META + (blk - 1) * BLK + r)

    def valid(blk, r):
        return np.where(blk == 0, r < N_META, True)

    r = np.arange(BLK)
    rels, viss = [], []
    for j in (0, 1, 2, nj - 1):
        qpos = pos(j, r)[:, None]
        rel_t, vis_t = [], []
        for s, kb in enumerate((0, j - 1, j, j + 1)):
            slot_ok = (s == 0) or (1 <= kb <= nj - 1)
            kbc = min(max(kb, 0), nj - 1)
            kpos = pos(kbc, r)[None, :]
            rel = kpos - qpos
            v = valid(kbc, r)[None, :] & np.ones((BLK, 1), bool)
            if s > 0:
                v = v & (np.abs(rel) <= BLK)
            rel_t.append(rel)
            vis_t.append(v & slot_ok)
        rels.append(np.concatenate(rel_t, axis=1))
        viss.append(np.concatenate(vis_t, axis=1))
    return np.stack(rels).astype(np.int32), np.stack(viss)


def _t5_bucket(rel):
    nb = N_BUCKETS // 2
    max_exact = nb // 2
    ret = jnp.where(rel > 0, nb, 0)
    n = jnp.abs(rel)
    nf = jnp.maximum(n, 1).astype(jnp.float32)
    large = max_exact + (jnp.log(nf / max_exact) / math.log(MAX_DIST / max_exact) * (nb - max_exact)).astype(jnp.int32)
    large = jnp.minimum(large, nb - 1)
    return ret + jnp.where(n < max_exact, n, large)


def _perm_heads64(a, axis):
    parts = [lax.slice_in_dim(a, h * 64, (h + 1) * 64, axis=axis) for h in A_ORDER]
    return jnp.concatenate(parts, axis=axis)


def _unperm_heads64(a, axis):
    inv = [A_ORDER.index(h) for h in range(8)]
    parts = [lax.slice_in_dim(a, p * 64, (p + 1) * 64, axis=axis) for p in inv]
    return jnp.concatenate(parts, axis=axis)


def _w_in_to_p(w):
    sl = lambda a, b: lax.slice_in_dim(w, a, b, axis=1)
    z = lambda n: jnp.zeros((w.shape[0], n), w.dtype)
    return jnp.concatenate([_perm_heads64(sl(0, 512), 1), sl(512, 768), _perm_heads64(sl(768, 1280), 1), sl(1696, 2208),
                            sl(1280, 1536), sl(1536, 1664), z(64), sl(1664, 1696), z(32)], axis=1)


def _w_in_from_p(g):
    sl = lambda a, b: lax.slice_in_dim(g, a, b, axis=1)
    return jnp.concatenate([_unperm_heads64(sl(0, 512), 1), sl(512, 768), _unperm_heads64(sl(768, 1280), 1),
                            sl(1792, 2048), sl(2048, 2176), sl(2240, 2272), sl(1280, 1792)], axis=1)


def _w_uq_to_p(w):
    z = jnp.zeros((w.shape[0], 32), w.dtype)
    return jnp.concatenate([p for h in range(8) for p in (lax.slice_in_dim(w, h * 96, (h + 1) * 96, axis=1), z)], axis=1)


def _w_uq_from_p(g):
    return jnp.concatenate([lax.slice_in_dim(g, h * 128, h * 128 + 96, axis=1) for h in range(8)], axis=1)


def _w_ukv_to_p(w):
    z = jnp.zeros((w.shape[0], 64), w.dtype)
    ks = [p for h in range(8) for p in (lax.slice_in_dim(w, h * 128, h * 128 + 64, axis=1), z)]
    vs = [lax.slice_in_dim(w, h * 128 + 64, (h + 1) * 128, axis=1) for h in range(8)]
    return jnp.concatenate(ks + vs, axis=1)


def _w_ukv_from_p(g):
    parts = []
    for h in range(8):
        parts.append(lax.slice_in_dim(g, h * 128, h * 128 + 64, axis=1))
        parts.append(lax.slice_in_dim(g, 1024 + h * 64, 1024 + (h + 1) * 64, axis=1))
    return jnp.concatenate(parts, axis=1)


def _w_out_to_p(w):
    return jnp.concatenate([_perm_heads64(lax.slice_in_dim(w, 0, 512, axis=0), 0), lax.slice_in_dim(w, 512, 1024, axis=0)], axis=0)


def _w_out_from_p(g):
    return jnp.concatenate([_unperm_heads64(lax.slice_in_dim(g, 0, 512, axis=0), 0), lax.slice_in_dim(g, 512, 1024, axis=0)], axis=0)


def _rope_tables(cfg):
    half = B_ROPE // 2
    length = N_META + cfg.S
    freqs = ROPE_THETA ** (-jnp.arange(half, dtype=jnp.float32) / half)
    ang = jnp.arange(length, dtype=jnp.float32)[:, None] * freqs[None, :]
    cos, sin = jnp.cos(ang), jnp.sin(ang)

    def rows(t):
        return jnp.concatenate([t[:N_META], jnp.zeros((BLK - N_META, t.shape[1]), t.dtype), t[N_META:]], axis=0)

    ones = jnp.ones((length, 64), F32)
    zer = jnp.zeros((length, 32), F32)
    c_tab = rows(jnp.concatenate([ones, cos, cos, zer], axis=1))
    s_tab = rows(jnp.concatenate([zer, zer, -sin, sin, zer], axis=1))
    return c_tab, s_tab


def _inproj_fwd(cfg, h, g, w_p):
    tm = 256

    def body(h_ref, g_ref, w_ref, pa_ref, pf_ref):
        xh, _ = _rms(h_ref[...])
        u = (xh * g_ref[...]).astype(BF16)
        acc = _dot(u, w_ref[...])
        pa_ref[...] = acc[:, :768].astype(BF16)
        pf_ref[...] = acc[:, 768:]

    return pl.pallas_call(
        body, grid=(cfg.TP // tm,), name="inproj_fwd",
        in_specs=[pl.BlockSpec((tm, D_MODEL), lambda i: (i, 0)), pl.BlockSpec((1, D_MODEL), lambda i: (0, 0)),
                  pl.BlockSpec((D_MODEL, W_IN_P), lambda i: (0, 0))],
        out_specs=[pl.BlockSpec((tm, 768), lambda i: (i, 0)), pl.BlockSpec((tm, 1536), lambda i: (i, 0))],
        out_shape=[jax.ShapeDtypeStruct((cfg.TP, 768), BF16), jax.ShapeDtypeStruct((cfg.TP, 1536), F32)],
        compiler_params=_cp("parallel"),
    )(h, g, w_p)


def _lat_fwd(cfg, pf, gq, gkv, wq_p, wkv_p, c_tab, s_tab):
    nj = cfg.NJ

    def body(cq_ref, ckv_ref, kr_ref, gq_ref, gkv_ref, wq_ref, wkv_ref, c_ref, s_ref, q_ref, k_ref, v_ref):
        c1, s1 = c_ref[...], s_ref[...]
        c8, s8 = jnp.tile(c1, (1, 8)), jnp.tile(s1, (1, 8))
        xq, _ = _rms(cq_ref[...])
        qp = _dot((xq * gq_ref[...]).astype(BF16), wq_ref[...])
        q_ref[...] = (qp * c8 + _swap_rope(qp) * s8).astype(BF16)
        xk, _ = _rms(ckv_ref[...])
        kvp = _dot((xk * gkv_ref[...]).astype(BF16), wkv_ref[...])
        kr = kr_ref[...]
        krr = kr * c1 + _swap_rope(kr) * s1
        k_ref[...] = (kvp[:, :1024] + jnp.tile(krr, (1, 8))).astype(BF16)
        v_ref[...] = kvp[:, 1024:].astype(BF16)

    row = lambda b, j: b * nj + j
    return pl.pallas_call(
        body, grid=(cfg.B, nj), name="lat_fwd",
        in_specs=[pl.BlockSpec((BLK, 256), lambda b, j: (row(b, j), 4)), pl.BlockSpec((BLK, 128), lambda b, j: (row(b, j), 10)),
                  pl.BlockSpec((BLK, 128), lambda b, j: (row(b, j), 11)),
                  pl.BlockSpec((1, 256), lambda b, j: (0, 0)), pl.BlockSpec((1, 128), lambda b, j: (0, 0)),
                  pl.BlockSpec((256, 1024), lambda b, j: (0, 0)), pl.BlockSpec((128, 1536), lambda b, j: (0, 0)),
                  pl.BlockSpec((BLK, 128), lambda b, j: (j, 0)), pl.BlockSpec((BLK, 128), lambda b, j: (j, 0))],
        out_specs=[pl.BlockSpec((BLK, 1024), lambda b, j: (row(b, j), 0)), pl.BlockSpec((BLK, 1024), lambda b, j: (row(b, j), 0)),
                   pl.BlockSpec((BLK, 512), lambda b, j: (row(b, j), 0))],
        out_shape=[jax.ShapeDtypeStruct((cfg.TP, 1024), BF16), jax.ShapeDtypeStruct((cfg.TP, 1024), BF16),
                   jax.ShapeDtypeStruct((cfg.TP, 512), BF16)],
        compiler_params=_cp("parallel", "parallel"),
    )(pf, pf, pf, gq, gkv, wq_p, wkv_p, c_tab, s_tab)


def _gate_halves(ya, yb, ga, gb, goa, gob):
    xa, ra = _rms(ya)
    xb, rb = _rms(yb)
    sga, sgb = _sigmoid(ga), _sigmoid(gb)
    return xa, ra, xb, rb, sga, sgb, xa * goa * (ga * sga), xb * gob * (gb * sgb)


def _out_fwd(cfg, ya, yb, pf, goa, gob, wo_p, h):
    tm = 256

    def body(ya_ref, yb_ref, ga_ref, gb_ref, goa_ref, gob_ref, w_ref, h_ref, o_ref):
        *_, y_a, y_b = _gate_halves(ya_ref[...], yb_ref[...], ga_ref[...], gb_ref[...], goa_ref[...], gob_ref[...])
        y = jnp.concatenate([y_a, y_b], axis=1).astype(BF16)
        o_ref[...] = h_ref[...] + _dot(y, w_ref[...])

    return pl.pallas_call(
        body, grid=(cfg.TP // tm,), name="out_fwd",
        in_specs=[pl.BlockSpec((tm, 512), lambda i: (i, 0)), pl.BlockSpec((tm, 512), lambda i: (i, 0)),
                  pl.BlockSpec((tm, 512), lambda i: (i, 0)), pl.BlockSpec((tm, 512), lambda i: (i, 1)),
                  pl.BlockSpec((1, 512), lambda i: (0, 0)), pl.BlockSpec((1, 512), lambda i: (0, 0)),
                  pl.BlockSpec((D_MODEL, D_MODEL), lambda i: (0, 0)), pl.BlockSpec((tm, D_MODEL), lambda i: (i, 0))],
        out_specs=pl.BlockSpec((tm, D_MODEL), lambda i: (i, 0)),
        out_shape=jax.ShapeDtypeStruct((cfg.TP, D_MODEL), F32),
        compiler_params=_cp("parallel"),
    )(ya, yb, pf, pf, goa, gob, wo_p, h)


def _bias_build(table, bucket, maskadd):
    def body(tab_ref, bk_ref, ma_ref, o_ref):
        h = pl.program_id(1)
        bk = bk_ref[0]

        def step(b, acc):
            return jnp.where(bk == b, tab_ref[b, h], acc)

        o_ref[0, 0] = lax.fori_loop(0, N_BUCKETS, step, jnp.zeros(bk.shape, F32)) + ma_ref[0]

    return pl.pallas_call(
        body, grid=(4, A_HEADS), name="bias_build",
        in_specs=[pl.BlockSpec(memory_space=pltpu.SMEM), pl.BlockSpec((1, BLK, 512), lambda t, h: (t, 0, 0)),
                  pl.BlockSpec((1, BLK, 512), lambda t, h: (t, 0, 0))],
        out_specs=pl.BlockSpec((1, 1, BLK, 512), lambda t, h: (t, h, 0, 0)),
        out_shape=jax.ShapeDtypeStruct((4, A_HEADS, BLK, 512), F32),
        compiler_params=_cp("parallel", "parallel"),
    )(table, bucket, maskadd)


def _bias_grad(cfg, s_acc, bucket):
    nstep = cfg.B * 4 * A_HEADS

    def body(s_ref, bk_ref, o_ref):
        i = pl.program_id(0)
        h = i % A_HEADS

        @pl.when(i == 0)
        def _():
            o_ref[...] = jnp.zeros_like(o_ref)

        s = s_ref[0, 0, 0]
        bk = bk_ref[0]
        row = lax.broadcasted_iota(jnp.int32, (N_BUCKETS, 128), 0)
        lane = lax.broadcasted_iota(jnp.int32, (N_BUCKETS, 128), 1)

        def step(b, acc):
            tot = jnp.sum(jnp.where(bk == b, s, 0.0))
            return jnp.where((row == b) & (lane == h), acc + tot, acc)

        o_ref[...] = lax.fori_loop(0, N_BUCKETS, step, o_ref[...])

    return pl.pallas_call(
        body, grid=(nstep,), name="bias_grad",
        in_specs=[pl.BlockSpec((1, 1, 1, BLK, 512), lambda i: (i // 32, (i // 8) % 4, i % 8, 0, 0)),
                  pl.BlockSpec((1, BLK, 512), lambda i: ((i // 8) % 4, 0, 0))],
        out_specs=pl.BlockSpec((N_BUCKETS, 128), lambda i: (0, 0)),
        out_shape=jax.ShapeDtypeStruct((N_BUCKETS, 128), F32),
        compiler_params=_cp("arbitrary"),
    )(s_acc, bucket)


def _win_specs(cfg):
    nj = cfg.NJ
    row = lambda b, j: b * nj + j
    jt = lambda j: jnp.where(j == 0, 0, jnp.where(j == 1, 1, jnp.where(j == nj - 1, 3, 2)))
    slot_rows = [lambda b, j: row(b, 0), lambda b, j: row(b, jnp.maximum(j - 1, 0)), lambda b, j: row(b, j),
                 lambda b, j: row(b, jnp.minimum(j + 1, nj - 1))]
    k_specs = [pl.BlockSpec((BLK, 128), functools.partial(lambda b, j, f: (f(b, j), 4), f=f)) for f in slot_rows]
    v_specs = [pl.BlockSpec((BLK, 128), functools.partial(lambda b, j, f: (f(b, j), 5), f=f)) for f in slot_rows]
    q_spec = pl.BlockSpec((BLK, 512), lambda b, j: (row(b, j), 0))
    bias_spec = pl.BlockSpec((1, A_HEADS, BLK, 512), lambda b, j: (jt(j), 0, 0, 0))
    return row, jt, q_spec, k_specs, v_specs, bias_spec


def _win_fwd(cfg, pa, bias, sink):
    row, jt, q_spec, k_specs, v_specs, bias_spec = _win_specs(cfg)
    scale = A_DH ** -0.5

    def body(sink_ref, q_ref, k0, k1, k2, k3, v0, v1, v2, v3, b_ref, o_ref, lse_ref):
        k4 = jnp.concatenate([k0[...], k1[...], k2[...], k3[...]], axis=0)
        v4 = jnp.concatenate([v0[...], v1[...], v2[...], v3[...]], axis=0)
        lane_k = _lane(k4.shape)
        kk = (jnp.where(lane_k < 64, k4, jnp.zeros_like(k4)), jnp.where(lane_k >= 64, k4, jnp.zeros_like(k4)))
        lane_o = _lane((BLK, 128))
        lse_all = jnp.zeros((BLK, 128), F32)
        for c in range(4):
            qc = q_ref[:, c * 128:(c + 1) * 128]
            outs = []
            for hf in range(2):
                h = c + 4 * hf
                s = _dot_nt(qc, kk[hf]) * scale + b_ref[0, h]
                m = jnp.maximum(jnp.max(s, axis=1, keepdims=True), sink_ref[h])
                e = jnp.exp(s - m)
                den = jnp.sum(e, axis=1, keepdims=True) + jnp.exp(sink_ref[h] - m)
                outs.append(_dot(e.astype(BF16), v4) / den)
                lse_all = jnp.where(lane_o == h, m + jnp.log(den), lse_all)
            o_ref[:, c * 128:(c + 1) * 128] = jnp.where(lane_o < 64, outs[0], outs[1])
        lse_ref[...] = lse_all

    return pl.pallas_call(
        body, grid=(cfg.B, cfg.NJ), name="win_fwd",
        in_specs=[pl.BlockSpec(memory_space=pltpu.SMEM), q_spec, *k_specs, *v_specs, bias_spec],
        out_specs=[pl.BlockSpec((BLK, 512), lambda b, j: (row(b, j), 0)), pl.BlockSpec((BLK, 128), lambda b, j: (row(b, j), 0))],
        out_shape=[jax.ShapeDtypeStruct((cfg.TP, 512), F32), jax.ShapeDtypeStruct((cfg.TP, 128), F32)],
        compiler_params=_cp("parallel", "parallel"),
    )(sink, pa, *([pa] * 8), bias)


def _win_bwd(cfg, pa, bias, sink, dya, ya, lse):
    row, jt, q_spec, k_specs, v_specs, bias_spec = _win_specs(cfg)
    nj = cfg.NJ
    scale = A_DH ** -0.5

    def body(sink_ref, q_ref, k0, k1, k2, k3, v0, v1, v2, v3, b_ref, dy_ref, y_ref, lse_ref,
             dq_ref, dkp_ref, dvp_ref, dkm_ref, dvm_ref, s_ref, dsink_ref):
        j = pl.program_id(1)
        first_of_kind = (j == 0) | (j == 1) | (j == 2) | (j == nj - 1)
        k4 = jnp.concatenate([k0[...], k1[...], k2[...], k3[...]], axis=0)
        v4 = jnp.concatenate([v0[...], v1[...], v2[...], v3[...]], axis=0)
        lane_k = _lane(k4.shape)
        kk = (jnp.where(lane_k < 64, k4, jnp.zeros_like(k4)), jnp.where(lane_k >= 64, k4, jnp.zeros_like(k4)))
        lane_o = _lane((BLK, 128))
        half = (lane_o < 64, lane_o >= 64)
        lse_blk = lse_ref[...]
        dk4 = jnp.zeros((512, 128), F32)
        dv4 = jnp.zeros((512, 128), F32)
        dsink = jnp.zeros((8, 128), F32)
        lane_s = _lane((8, 128))
        row_s = lax.broadcasted_iota(jnp.int32, (8, 128), 0)
        for c in range(4):
            qc = q_ref[:, c * 128:(c + 1) * 128]
            dyc = dy_ref[:, c * 128:(c + 1) * 128]
            yc = y_ref[:, c * 128:(c + 1) * 128]
            dqc = jnp.zeros((BLK, 128), F32)
            for hf in range(2):
                h = c + 4 * hf
                lse_h = jnp.sum(jnp.where(lane_o == h, lse_blk, 0.0), axis=1, keepdims=True)
                s = _dot_nt(qc, kk[hf]) * scale + b_ref[0, h]
                p = jnp.exp(s - lse_h)
                do_h = jnp.where(half[hf], dyc, 0.0)
                delta = jnp.sum(do_h * yc, axis=1, keepdims=True)
                do_b = do_h.astype(BF16)
                dp = _dot_nt(do_b, v4)
                ds = p * (dp - delta)

                @pl.when(first_of_kind)
                def _():
                    s_ref[0, 0, h] = ds

                @pl.when(jnp.logical_not(first_of_kind))
                def _():
                    s_ref[0, 0, h] += ds

                psink = jnp.exp(sink_ref[h] - lse_h)
                tot = -jnp.sum(psink * delta)
                dsink = jnp.where((row_s == 0) & (lane_s == h), tot, dsink)
                dsb = (ds * scale).astype(BF16)
                dqc = dqc + _dot(dsb, kk[hf])
                qm = jnp.where(half[hf], qc, jnp.zeros_like(qc))
                dk4 = dk4 + _dot_tn(dsb, qm)
                dv4 = dv4 + _dot_tn(p.astype(BF16), do_b)
            dq_ref[:, c * 128:(c + 1) * 128] = dqc.astype(BF16)
        dkp_ref[0] = dk4
        dvp_ref[0] = dv4

        @pl.when(j == 0)
        def _():
            dkm_ref[...] = dk4[:BLK]
            dvm_ref[...] = dv4[:BLK]

        @pl.when(j > 0)
        def _():
            dkm_ref[...] += dk4[:BLK]
            dvm_ref[...] += dv4[:BLK]

        @pl.when((pl.program_id(0) == 0) & (j == 0))
        def _():
            dsink_ref[...] = dsink

        @pl.when((pl.program_id(0) > 0) | (j > 0))
        def _():
            dsink_ref[...] += dsink

    blk_row = pl.BlockSpec((BLK, 512), lambda b, j: (row(b, j), 0))
    return pl.pallas_call(
        body, grid=(cfg.B, nj), name="win_bwd",
        in_specs=[pl.BlockSpec(memory_space=pltpu.SMEM), q_spec, *k_specs, *v_specs, bias_spec, blk_row, blk_row,
                  pl.BlockSpec((BLK, 128), lambda b, j: (row(b, j), 0))],
        out_specs=[blk_row,
                   pl.BlockSpec((1, 512, 128), lambda b, j: (row(b, j), 0, 0)), pl.BlockSpec((1, 512, 128), lambda b, j: (row(b, j), 0, 0)),
                   pl.BlockSpec((BLK, 128), lambda b, j: (b, 0)), pl.BlockSpec((BLK, 128), lambda b, j: (b, 0)),
                   pl.BlockSpec((1, 1, A_HEADS, BLK, 512), lambda b, j: (b, jt(j), 0, 0, 0)),
                   pl.BlockSpec((8, 128), lambda b, j: (0, 0))],
        out_shape=[jax.ShapeDtypeStruct((cfg.TP, 512), BF16),
                   jax.ShapeDtypeStruct((cfg.B * nj, 512, 128), F32), jax.ShapeDtypeStruct((cfg.B * nj, 512, 128), F32),
                   jax.ShapeDtypeStruct((cfg.B * BLK, 128), F32), jax.ShapeDtypeStruct((cfg.B * BLK, 128), F32),
                   jax.ShapeDtypeStruct((cfg.B, 4, A_HEADS, BLK, 512), F32),
                   jax.ShapeDtypeStruct((8, 128), F32)],
        compiler_params=_cp("arbitrary", "arbitrary"),
    )(sink, pa, *([pa] * 8), bias, dya, ya, lse)


def _win_dkv_combine(cfg, dkp, dvp, dkm, dvm):
    nj = cfg.NJ
    row = lambda b, j: b * nj + j

    def body(k1, k2, k3, v1, v2, v3, km, vm, o_ref):
        j = pl.program_id(1)

        @pl.when(j == 0)
        def _():
            o_ref[:, :128] = km[...].astype(BF16)
            o_ref[:, 128:] = vm[...].astype(BF16)

        @pl.when(j > 0)
        def _():
            up = jnp.where(j + 1 <= nj - 1, 1.0, 0.0)
            o_ref[:, :128] = (k1[0] * up + k2[0] + k3[0]).astype(BF16)
            o_ref[:, 128:] = (v1[0] * up + v2[0] + v3[0]).astype(BF16)

    def part(slot, dj):
        return pl.BlockSpec((1, BLK, 128), lambda b, j: (row(b, jnp.clip(j + dj, 0, nj - 1)), slot, 0))

    return pl.pallas_call(
        body, grid=(cfg.B, nj), name="win_dkv_combine",
        in_specs=[part(1, 1), part(2, 0), part(3, -1), part(1, 1), part(2, 0), part(3, -1),
                  pl.BlockSpec((BLK, 128), lambda b, j: (b, 0)), pl.BlockSpec((BLK, 128), lambda b, j: (b, 0))],
        out_specs=pl.BlockSpec((BLK, 256), lambda b, j: (row(b, j), 0)),
        out_shape=jax.ShapeDtypeStruct((cfg.TP, 256), BF16),
        compiler_params=_cp("parallel", "parallel"),
    )(dkp, dkp, dkp, dvp, dvp, dvp, dkm, dvm)


def _mla_fwd(cfg, q, k, v, kmask):
    nj, lp = cfg.NJ, cfg.LP
    scale = (B_NOPE + B_ROPE) ** -0.5

    def body(q_ref, k_ref, v_ref, km_ref, o_ref, lse_ref):
        lane_o = _lane((BLK, 128))
        v2 = v_ref[...]
        km = km_ref[...]
        outs = []
        lse_all = jnp.zeros((BLK, 128), F32)
        for hh in range(2):
            s = _dot_nt(q_ref[:, hh * 128:(hh + 1) * 128], k_ref[:, hh * 128:(hh + 1) * 128]) * scale + km
            m = jnp.max(s, axis=1, keepdims=True)
            e = jnp.exp(s - m)
            den = jnp.sum(e, axis=1, keepdims=True)
            outs.append(_dot(e.astype(BF16), v2) / den)
            lse_all = jnp.where(lane_o == hh, m + jnp.log(den), lse_all)
        o_ref[...] = jnp.where(lane_o < 64, outs[0], outs[1])
        lse_ref[0] = lse_all

    return pl.pallas_call(
        body, grid=(cfg.B, 4, nj), name="mla_fwd",
        in_specs=[pl.BlockSpec((BLK, 256), lambda b, p, i: (b * nj + i, p)), pl.BlockSpec((lp, 256), lambda b, p, i: (b, p)),
                  pl.BlockSpec((lp, 128), lambda b, p, i: (b, p)), pl.BlockSpec((1, lp), lambda b, p, i: (0, 0))],
        out_specs=[pl.BlockSpec((BLK, 128), lambda b, p, i: (b * nj + i, p)),
                   pl.BlockSpec((1, BLK, 128), lambda b, p, i: (p, b * nj + i, 0))],
        out_shape=[jax.ShapeDtypeStruct((cfg.TP, 512), F32), jax.ShapeDtypeStruct((4, cfg.TP, 128), F32)],
        compiler_params=_cp("parallel", "parallel", "parallel"),
    )(q, k, v, kmask)


def _mla_bwd(cfg, q, k, v, kmask, dyb, yb, lse):
    nj, lp = cfg.NJ, cfg.LP
    scale = (B_NOPE + B_ROPE) ** -0.5

    def body(q_ref, k_ref, v_ref, km_ref, dy_ref, y_ref, lse_ref, dq_ref, dk_ref, dv_ref):
        i = pl.program_id(2)

        @pl.when(i == 0)
        def _():
            dk_ref[...] = jnp.zeros_like(dk_ref)
            dv_ref[...] = jnp.zeros_like(dv_ref)

        lane_o = _lane((BLK, 128))
        half = (lane_o < 64, lane_o >= 64)
        v2 = v_ref[...]
        km = km_ref[...]
        dy = dy_ref[...]
        y = y_ref[...]
        lse_blk = lse_ref[0]
        for hh in range(2):
            qh = q_ref[:, hh * 128:(hh + 1) * 128]
            kh = k_ref[:, hh * 128:(hh + 1) * 128]
            lse_h = jnp.sum(jnp.where(lane_o == hh, lse_blk, 0.0), axis=1, keepdims=True)
            p = jnp.exp(_dot_nt(qh, kh) * scale + km - lse_h)
            do_h = jnp.where(half[hh], dy, 0.0)
            delta = jnp.sum(do_h * y, axis=1, keepdims=True)
            do_b = do_h.astype(BF16)
            ds = p * (_dot_nt(do_b, v2) - delta)
            dsb = (ds * scale).astype(BF16)
            dq_ref[:, hh * 128:(hh + 1) * 128] = _dot(dsb, kh)
            dk_ref[:, hh * 128:(hh + 1) * 128] += _dot_tn(dsb, qh)
            dv_ref[...] += _dot_tn(p.astype(BF16), do_b)

    return pl.pallas_call(
        body, grid=(cfg.B, 4, nj), name="mla_bwd",
        in_specs=[pl.BlockSpec((BLK, 256), lambda b, p, i: (b * nj + i, p)), pl.BlockSpec((lp, 256), lambda b, p, i: (b, p)),
                  pl.BlockSpec((lp, 128), lambda b, p, i: (b, p)), pl.BlockSpec((1, lp), lambda b, p, i: (0, 0)),
                  pl.BlockSpec((BLK, 128), lambda b, p, i: (b * nj + i, p)), pl.BlockSpec((BLK, 128), lambda b, p, i: (b * nj + i, p)),
                  pl.BlockSpec((1, BLK, 128), lambda b, p, i: (p, b * nj + i, 0))],
        out_specs=[pl.BlockSpec((BLK, 256), lambda b, p, i: (b * nj + i, p)), pl.BlockSpec((lp, 256), lambda b, p, i: (b, p)),
                   pl.BlockSpec((lp, 128), lambda b, p, i: (b, p))],
        out_shape=[jax.ShapeDtypeStruct((cfg.TP, 1024), F32), jax.ShapeDtypeStruct((cfg.TP, 1024), F32),
                   jax.ShapeDtypeStruct((cfg.TP, 512), F32)],
        compiler_params=_cp("parallel", "parallel", "arbitrary"),
    )(q, k, v, kmask, dyb, yb, lse)


def _loss_bwd(cfg, h, target, gf):
    nj, nb = cfg.NJ, cfg.NB

    def body(h_ref, t_ref, g_ref, dh_ref, loss_ref, dg_ref):
        b, j = pl.program_id(0), pl.program_id(1)

        @pl.when((b == 0) & (j == 0))
        def _():
            loss_ref[...] = jnp.zeros_like(loss_ref)
            dg_ref[...] = jnp.zeros_like(dg_ref)

        @pl.when(j == 0)
        def _():
            dh_ref[...] = jnp.zeros_like(dh_ref)

        @pl.when(j > 0)
        def _():
            g = g_ref[...]
            xh, r = _rms(h_ref[...])
            err = xh * g - t_ref[...]
            loss_ref[...] += jnp.where((lax.broadcasted_iota(jnp.int32, (8, 128), 0) == 0) & (_lane((8, 128)) == 0),
                                       (0.5 / D_MODEL) * jnp.sum(err * err), 0.0)
            dy = err * (1.0 / D_MODEL)
            dg_ref[...] += jnp.sum(dy * xh, axis=0, keepdims=True)
            dh_ref[...] = _rms_bwd(xh, r, dy * g)

    return pl.pallas_call(
        body, grid=(cfg.B, nj), name="loss_bwd",
        in_specs=[pl.BlockSpec((BLK, D_MODEL), lambda b, j: (b * nj + j, 0)),
                  pl.BlockSpec((BLK, D_MODEL), lambda b, j: (b * nb + jnp.maximum(j - 1, 0), 0)),
                  pl.BlockSpec((1, D_MODEL), lambda b, j: (0, 0))],
        out_specs=[pl.BlockSpec((BLK, D_MODEL), lambda b, j: (b * nj + j, 0)), pl.BlockSpec((8, 128), lambda b, j: (0, 0)),
                   pl.BlockSpec((1, D_MODEL), lambda b, j: (0, 0))],
        out_shape=[jax.ShapeDtypeStruct((cfg.TP, D_MODEL), F32), jax.ShapeDtypeStruct((8, 128), F32),
                   jax.ShapeDtypeStruct((1, D_MODEL), F32)],
        compiler_params=_cp("arbitrary", "arbitrary"),
    )(h, target, gf)


def _out_bwd(cfg, dh, ya, yb, pf, goa, gob, wo_p):
    tm = 256

    def body(dh_ref, ya_ref, yb_ref, ga_ref, gb_ref, goa_ref, gob_ref, w_ref,
             dya_ref, dyb_ref, dg_ref, dw_ref, dgoa_ref, dgob_ref):
        @pl.when(pl.program_id(0) == 0)
        def _():
            dw_ref[...] = jnp.zeros_like(dw_ref)
            dgoa_ref[...] = jnp.zeros_like(dgoa_ref)
            dgob_ref[...] = jnp.zeros_like(dgob_ref)

        ga, gb, goa, gob = ga_ref[...], gb_ref[...], goa_ref[...], gob_ref[...]
        xa, ra, xb, rb, sga, sgb, y_a, y_b = _gate_halves(ya_ref[...], yb_ref[...], ga, gb, goa, gob)
        dhb = dh_ref[...].astype(BF16)
        dw_ref[...] += _dot_tn(jnp.concatenate([y_a, y_b], axis=1).astype(BF16), dhb)
        dy = _dot_nt(dhb, w_ref[...])
        for (dyh, x, r, g, sg, go, dy_out, dgo_ref, col) in (
                (dy[:, :512], xa, ra, ga, sga, goa, dya_ref, dgoa_ref, 0), (dy[:, 512:], xb, rb, gb, sgb, gob, dyb_ref, dgob_ref, 512)):
            dn = dyh * (g * sg)
            dg_ref[:, col:col + 512] = (dyh * (x * go) * (sg * (1.0 + g * (1.0 - sg)))).astype(BF16)
            dgo_ref[...] += jnp.sum(dn * x, axis=0, keepdims=True)
            dy_out[...] = _rms_bwd(x, r, dn * go)

    half = lambda c: pl.BlockSpec((tm, 512), lambda i: (i, c))
    vec = pl.BlockSpec((1, 512), lambda i: (0, 0))
    return pl.pallas_call(
        body, grid=(cfg.TP // tm,), name="out_bwd",
        in_specs=[pl.BlockSpec((tm, D_MODEL), lambda i: (i, 0)), half(0), half(0), half(0), half(1), vec, vec,
                  pl.BlockSpec((D_MODEL, D_MODEL), lambda i: (0, 0))],
        out_specs=[half(0), half(0), pl.BlockSpec((tm, D_MODEL), lambda i: (i, 0)),
                   pl.BlockSpec((D_MODEL, D_MODEL), lambda i: (0, 0)), vec, vec],
        out_shape=[jax.ShapeDtypeStruct((cfg.TP, 512), F32), jax.ShapeDtypeStruct((cfg.TP, 512), F32),
                   jax.ShapeDtypeStruct((cfg.TP, D_MODEL), BF16), jax.ShapeDtypeStruct((D_MODEL, D_MODEL), F32),
                   jax.ShapeDtypeStruct((1, 512), F32), jax.ShapeDtypeStruct((1, 512), F32)],
        compiler_params=_cp("arbitrary"),
    )(dh, ya, yb, pf, pf, goa, gob, wo_p)


def _lat_bwd(cfg, dq, dk, dv, pf, gq, gkv, wq_p, wkv_p, c_tab, s_tab):
    nj = cfg.NJ

    def body(dq_ref, dk_ref, dv_ref, cq_ref, ckv_ref, gq_ref, gkv_ref, wq_ref, wkv_ref, c_ref, s_ref,
             dl_ref, dwq_ref, dwkv_ref, dgq_ref, dgkv_ref):
        @pl.when((pl.program_id(0) == 0) & (pl.program_id(1) == 0))
        def _():
            dwq_ref[...] = jnp.zeros_like(dwq_ref)
            dwkv_ref[...] = jnp.zeros_like(dwkv_ref)
            dgq_ref[...] = jnp.zeros_like(dgq_ref)
            dgkv_ref[...] = jnp.zeros_like(dgkv_ref)

        c1, s1 = c_ref[...], s_ref[...]
        c8, s8 = jnp.tile(c1, (1, 8)), jnp.tile(s1, (1, 8))
        dq_r = dq_ref[...]
        dqp = (dq_r * c8 + _swap_rope(dq_r * s8)).astype(BF16)
        gq = gq_ref[...]
        xq, rq = _rms(cq_ref[...])
        dwq_ref[...] += _dot_tn((xq * gq).astype(BF16), dqp)
        dn = _dot_nt(dqp, wq_ref[...])
        dgq_ref[...] += jnp.sum(dn * xq, axis=0, keepdims=True)
        dl_ref[:, :256] = _rms_bwd(xq, rq, dn * gq).astype(BF16)

        dk_r = dk_ref[...]
        dkr = dk_r[:, :128]
        for hd in range(1, 8):
            dkr = dkr + dk_r[:, hd * 128:(hd + 1) * 128]
        lane1 = _lane(dkr.shape)
        dkr = jnp.where((lane1 >= 64) & (lane1 < 96), dkr, 0.0)
        dl_ref[:, 384:] = (dkr * c1 + _swap_rope(dkr * s1)).astype(BF16)
        dkv = jnp.concatenate([dk_r, dv_ref[...]], axis=1).astype(BF16)
        gkv = gkv_ref[...]
        xk, rk = _rms(ckv_ref[...])
        dwkv_ref[...] += _dot_tn((xk * gkv).astype(BF16), dkv)
        dn2 = _dot_nt(dkv, wkv_ref[...])
        dgkv_ref[...] += jnp.sum(dn2 * xk, axis=0, keepdims=True)
        dl_ref[:, 256:384] = _rms_bwd(xk, rk, dn2 * gkv).astype(BF16)

    row = lambda b, j: b * nj + j
    const = lambda shape: pl.BlockSpec(shape, lambda b, j: (0, 0))
    return pl.pallas_call(
        body, grid=(cfg.B, nj), name="lat_bwd",
        in_specs=[pl.BlockSpec((BLK, 1024), lambda b, j: (row(b, j), 0)), pl.BlockSpec((BLK, 1024), lambda b, j: (row(b, j), 0)),
                  pl.BlockSpec((BLK, 512), lambda b, j: (row(b, j), 0)),
                  pl.BlockSpec((BLK, 256), lambda b, j: (row(b, j), 4)), pl.BlockSpec((BLK, 128), lambda b, j: (row(b, j), 10)),
                  const((1, 256)), const((1, 128)), const((256, 1024)), const((128, 1536)),
                  pl.BlockSpec((BLK, 128), lambda b, j: (j, 0)), pl.BlockSpec((BLK, 128), lambda b, j: (j, 0))],
        out_specs=[pl.BlockSpec((BLK, 512), lambda b, j: (row(b, j), 0)), const((256, 1024)), const((128, 1536)),
                   const((1, 256)), const((1, 128))],
        out_shape=[jax.ShapeDtypeStruct((cfg.TP, 512), BF16), jax.ShapeDtypeStruct((256, 1024), F32),
                   jax.ShapeDtypeStruct((128, 1536), F32), jax.ShapeDtypeStruct((1, 256), F32), jax.ShapeDtypeStruct((1, 128), F32)],
        compiler_params=_cp("arbitrary", "arbitrary"),
    )(dq, dk, dv, pf, pf, gq, gkv, wq_p, wkv_p, c_tab, s_tab)


def _inproj_bwd(cfg, h, g, w_p, dqa, dkva, dgate, dlat, dh):
    tm = 256

    def body(h_ref, g_ref, w_ref, dqa_ref, dkva_ref, dg_ref, dl_ref, dh_ref, o_ref, dw_ref, dgn_ref):
        @pl.when(pl.program_id(0) == 0)
        def _():
            dw_ref[...] = jnp.zeros_like(dw_ref)
            dgn_ref[...] = jnp.zeros_like(dgn_ref)

        g = g_ref[...]
        xh, r = _rms(h_ref[...])
        dproj = jnp.concatenate([dqa_ref[...], dkva_ref[...], dg_ref[...], dl_ref[...]], axis=1)
        dw_ref[...] += _dot_tn((xh * g).astype(BF16), dproj)
        du = _dot_nt(dproj, w_ref[...])
        dgn_ref[...] += jnp.sum(du * xh, axis=0, keepdims=True)
        o_ref[...] = dh_ref[...] + _rms_bwd(xh, r, du * g)

    rows = lambda w: pl.BlockSpec((tm, w), lambda i: (i, 0))
    return pl.pallas_call(
        body, grid=(cfg.TP // tm,), name="inproj_bwd",
        in_specs=[rows(D_MODEL), pl.BlockSpec((1, D_MODEL), lambda i: (0, 0)), pl.BlockSpec((D_MODEL, W_IN_P), lambda i: (0, 0)),
                  rows(512), rows(256), rows(1024), rows(512), rows(D_MODEL)],
        out_specs=[rows(D_MODEL), pl.BlockSpec((D_MODEL, W_IN_P), lambda i: (0, 0)), pl.BlockSpec((1, D_MODEL), lambda i: (0, 0))],
        out_shape=[jax.ShapeDtypeStruct((cfg.TP, D_MODEL), F32), jax.ShapeDtypeStruct((D_MODEL, W_IN_P), F32),
                   jax.ShapeDtypeStruct((1, D_MODEL), F32)],
        compiler_params=_cp("arbitrary"),
    )(h, g, w_p, dqa, dkva, dgate, dlat, dh)


def _meta_grad(cfg, dh):
    def body(d_ref, o_ref):
        @pl.when(pl.program_id(0) == 0)
        def _():
            o_ref[...] = d_ref[...]

        @pl.when(pl.program_id(0) > 0)
        def _():
            o_ref[...] += d_ref[...]

    return pl.pallas_call(
        body, grid=(cfg.B,), name="meta_grad",
        in_specs=[pl.BlockSpec((BLK, D_MODEL), lambda b: (b * cfg.NJ, 0))],
        out_specs=pl.BlockSpec((BLK, D_MODEL), lambda b: (0, 0)),
        out_shape=jax.ShapeDtypeStruct((BLK, D_MODEL), F32),
        compiler_params=_cp("arbitrary"),
    )(dh)


def _local_grads(cfg, x, target, meta, table, norm_in, w_in, sink_a, norm_q_lat, w_uq, norm_kv_lat, w_ukv,
                 norm_out_a, norm_out_b, w_out, norm_final):
    depth = w_in.shape[0]
    rel, vis = _window_structure(cfg.NJ)
    bucket = _t5_bucket(jnp.asarray(rel))
    maskadd = jnp.asarray(np.where(vis, 0.0, NEG).astype(np.float32))
    kvalid = np.concatenate([np.arange(BLK) < N_META] + [np.ones(BLK, bool)] * cfg.NB)
    kmask = jnp.asarray(np.where(kvalid, 0.0, NEG).astype(np.float32))[None, :]
    c_tab, s_tab = _rope_tables(cfg)
    bias = _bias_build(table, bucket, maskadd)

    meta_blk = jnp.concatenate([meta, jnp.zeros((BLK - N_META, D_MODEL), F32)], axis=0)
    h = jnp.concatenate([jnp.broadcast_to(meta_blk[None], (cfg.B, BLK, D_MODEL)), x], axis=1).reshape(cfg.TP, D_MODEL)

    wp = []
    for i in range(depth):
        wp.append(dict(
            w_in=_w_in_to_p(w_in[i]), w_uq=_w_uq_to_p(w_uq[i]), w_ukv=_w_ukv_to_p(w_ukv[i]), w_out=_w_out_to_p(w_out[i]),
            g_in=norm_in[i][None], gq=norm_q_lat[i][None], gkv=norm_kv_lat[i][None],
            goa=_perm_heads64(norm_out_a[i], 0)[None], gob=norm_out_b[i][None], sink=sink_a[i]))

    saved = []
    for i in range(depth):
        w = wp[i]
        pa, pf = _inproj_fwd(cfg, h, w["g_in"], w["w_in"])
        q, k, v = _lat_fwd(cfg, pf, w["gq"], w["gkv"], w["w_uq"], w["w_ukv"], c_tab, s_tab)
        ya, lse_a = _win_fwd(cfg, pa, bias, w["sink"])
        yb, lse_b = _mla_fwd(cfg, q, k, v, kmask)
        h_next = _out_fwd(cfg, ya, yb, pf, w["goa"], w["gob"], w["w_out"], h)
        saved.append(dict(h=h, pa=pa, pf=pf, q=q, k=k, v=v, ya=ya, lse_a=lse_a, yb=yb, lse_b=lse_b))
        h = h_next

    dh, loss_tile, d_norm_final = _loss_bwd(cfg, h, target.reshape(cfg.B * cfg.S, D_MODEL), norm_final[None])

    grads = {k_: [] for k_ in ("norm_in", "w_in", "sink_a", "norm_q_lat", "w_uq", "norm_kv_lat", "w_ukv", "norm_out_a", "norm_out_b", "w_out")}
    s_total = None
    for i in reversed(range(depth)):
        w, sv = wp[i], saved[i]
        dya, dyb, dgate, dwo, dgoa, dgob = _out_bwd(cfg, dh, sv["ya"], sv["yb"], sv["pf"], w["goa"], w["gob"], w["w_out"])
        dqa, dkp, dvp, dkm, dvm, s_acc, dsink = _win_bwd(cfg, sv["pa"], bias, w["sink"], dya, sv["ya"], sv["lse_a"])
        dkva = _win_dkv_combine(cfg, dkp, dvp, dkm, dvm)
        dq, dk, dv = _mla_bwd(cfg, sv["q"], sv["k"], sv["v"], kmask, dyb, sv["yb"], sv["lse_b"])
        dlat, dwq, dwkv, dgq, dgkv = _lat_bwd(cfg, dq, dk, dv, sv["pf"], w["gq"], w["gkv"], w["w_uq"], w["w_ukv"], c_tab, s_tab)
        dh, dwin, dgin = _inproj_bwd(cfg, sv["h"], w["g_in"], w["w_in"], dqa, dkva, dgate, dlat, dh)
        s_total = s_acc if s_total is None else jnp.concatenate([s_total, s_acc], axis=0)
        grads["norm_in"].append(dgin[0])
        grads["w_in"].append(_w_in_from_p(dwin))
        grads["sink_a"].append(dsink[0, :A_HEADS])
        grads["norm_q_lat"].append(dgq[0])
        grads["w_uq"].append(_w_uq_from_p(dwq))
        grads["norm_kv_lat"].append(dgkv[0])
        grads["w_ukv"].append(_w_ukv_from_p(dwkv))
        grads["norm_out_a"].append(_unperm_heads64(dgoa[0], 0))
        grads["norm_out_b"].append(dgob[0])
        grads["w_out"].append(_w_out_from_p(dwo))

    out = {k_: jnp.stack(v_[::-1]) for k_, v_ in grads.items()}
    dtab = _bias_grad(cfg._replace(B=cfg.B * depth), s_total, bucket)
    out["rel_bias_table"] = dtab[:, :A_HEADS]
    out["norm_final"] = d_norm_final[0]
    out["meta_tokens"] = _meta_grad(cfg, dh)[:N_META]
    return loss_tile[0, 0], dh.reshape(cfg.B, cfg.LP, D_MODEL)[:, BLK:], out


MESH = pl.DeviceIdType.MESH
ANY = pl.BlockSpec(memory_space=pl.ANY)


def _place():
    x, y, c = lax.axis_index("x"), lax.axis_index("y"), lax.axis_index("c")
    others = [(1 - x, y), (x, 1 - y), (1 - x, 1 - y)]
    return x, y, c, others


def _gather_shards(shards):
    n = len(shards)

    def body(*refs):
        ins, outs = refs[:n], refs[n:2 * n]
        send_sems, recv_sems, local_sems = refs[2 * n:]
        x, y, c, others = _place()
        k_me = 2 * x + y
        local = [pltpu.make_async_copy(ins[a], outs[a].at[k_me], local_sems.at[a]) for a in range(n)]
        for cp in local:
            cp.start()

        def copy(a, j, k_dst):
            return pltpu.make_async_remote_copy(src_ref=ins[a], dst_ref=outs[a].at[k_dst], send_sem=send_sems.at[3 * a + j],
                                                recv_sem=recv_sems.at[3 * a + j], device_id=(*others[j], c), device_id_type=MESH)

        sends = [copy(a, j, k_me) for a in range(n) for j in range(3)]
        for cp in sends:
            cp.start()
        for a in range(n):
            for j, (ox, oy) in enumerate(others):
                copy(a, j, 2 * ox + oy).wait_recv()
        for cp in sends:
            cp.wait_send()
        for cp in local:
            cp.wait()

    return pl.pallas_call(
        body, name="gather_shards", in_specs=[ANY] * n, out_specs=[ANY] * n,
        out_shape=[jax.ShapeDtypeStruct((4, *s.shape), s.dtype) for s in shards],
        scratch_shapes=[pltpu.SemaphoreType.DMA((3 * n,)), pltpu.SemaphoreType.DMA((3 * n,)), pltpu.SemaphoreType.DMA((n,))],
    )(*shards)


def _scatter_parts(parts):
    n = len(parts)

    def body(*refs):
        ins, outs = refs[:n], refs[n:2 * n]
        send_sems, recv_sems = refs[2 * n:]
        x, y, c, others = _place()

        def copy(a, j):
            ox, oy = others[j]
            return pltpu.make_async_remote_copy(src_ref=ins[a].at[2 * ox + oy], dst_ref=outs[a].at[j], send_sem=send_sems.at[3 * a + j],
                                                recv_sem=recv_sems.at[3 * a + j], device_id=(ox, oy, c), device_id_type=MESH)

        copies = [copy(a, j) for a in range(n) for j in range(3)]
        for cp in copies:
            cp.start()
        for cp in copies:
            cp.wait_recv()
        for cp in copies:
            cp.wait_send()

    return pl.pallas_call(
        body, name="scatter_parts", in_specs=[ANY] * n, out_specs=[ANY] * n,
        out_shape=[jax.ShapeDtypeStruct((3, *p.shape[1:]), p.dtype) for p in parts],
        scratch_shapes=[pltpu.SemaphoreType.DMA((3 * n,)), pltpu.SemaphoreType.DMA((3 * n,))],
    )(*parts)


def _swap_sibling(arrs):
    n = len(arrs)

    def body(*refs):
        ins, outs = refs[:n], refs[n:2 * n]
        send_sems, recv_sems = refs[2 * n:]
        x, y, c, _ = _place()
        copies = [pltpu.make_async_remote_copy(src_ref=ins[a], dst_ref=outs[a], send_sem=send_sems.at[a], recv_sem=recv_sems.at[a],
                                               device_id=(x, y, 1 - c), device_id_type=MESH) for a in range(n)]
        for cp in copies:
            cp.start()
        for cp in copies:
            cp.wait_recv()
        for cp in copies:
            cp.wait_send()

    return pl.pallas_call(
        body, name="swap_sibling", in_specs=[ANY] * n, out_specs=[ANY] * n,
        out_shape=[jax.ShapeDtypeStruct(a.shape, a.dtype) for a in arrs],
        scratch_shapes=[pltpu.SemaphoreType.DMA((n,)), pltpu.SemaphoreType.DMA((n,))],
    )(*arrs)


def _allreduce_small(v):
    def body(v_ref, o_ref, buf, send_sems, recv_sems):
        x, y, c, _ = _place()
        me = 4 * x + 2 * y + c
        buf[me] = v_ref[...]

        def copy(r):
            tx, ty, tc = (x + (r >> 2)) % 2, (y + ((r >> 1) & 1)) % 2, (c + (r & 1)) % 2
            return tx, ty, tc

        sends = []
        for r in range(1, 8):
            tx, ty, tc = copy(r)
            sends.append(pltpu.make_async_remote_copy(src_ref=v_ref, dst_ref=buf.at[me], send_sem=send_sems.at[r - 1],
                                                      recv_sem=recv_sems.at[r - 1], device_id=(tx, ty, tc), device_id_type=MESH))
        for cp in sends:
            cp.start()
        for r in range(1, 8):
            tx, ty, tc = copy(r)
            pltpu.make_async_remote_copy(src_ref=v_ref, dst_ref=buf.at[4 * tx + 2 * ty + tc], send_sem=send_sems.at[r - 1],
                                         recv_sem=recv_sems.at[r - 1], device_id=(tx, ty, tc), device_id_type=MESH).wait_recv()
        for cp in sends:
            cp.wait_send()
        acc = buf[0]
        for d in range(1, 8):
            acc = acc + buf[d]
        o_ref[...] = acc

    return pl.pallas_call(
        body, name="allreduce_small", in_specs=[pl.BlockSpec(memory_space=pltpu.VMEM)], out_specs=pl.BlockSpec(memory_space=pltpu.VMEM),
        out_shape=jax.ShapeDtypeStruct(v.shape, F32),
        scratch_shapes=[pltpu.VMEM((8, *v.shape), F32), pltpu.SemaphoreType.DMA((7,)), pltpu.SemaphoreType.DMA((7,))],
    )(v)


def _rows_view(a):
    return a.reshape(-1, a.shape[-1])


def _elementwise(name, fn, ins, n_out):
    rows, cols = ins[0].shape
    tm = min(rows, 256)
    spec = pl.BlockSpec((tm, cols), lambda i: (i, 0))

    def body(*refs):
        outs = fn(*[r[...] for r in refs[:len(ins)]])
        for o_ref, o in zip(refs[len(ins):], outs):
            o_ref[...] = o

    return pl.pallas_call(
        body, grid=(rows // tm,), name=name, in_specs=[spec] * len(ins), out_specs=[spec] * n_out,
        out_shape=[jax.ShapeDtypeStruct((rows, cols), F32)] * n_out, compiler_params=_cp("parallel"),
    )(*ins)


def _sum_parts(name, own, recv):
    def fn(o, r0, r1, r2):
        return (o + r0.astype(F32) + r1.astype(F32) + r2.astype(F32),)

    return _elementwise("sum_parts_" + name, fn, [own, recv[0], recv[1], recv[2]], 1)[0]


def _adamw(name, w, m, v, g_parts):
    def fn(w_, m_, v_, *gs):
        g = gs[0]
        for extra in gs[1:]:
            g = g + extra
        m_new = ADAM_B1 * m_ + (1.0 - ADAM_B1) * g
        v_new = ADAM_B2 * v_ + (1.0 - ADAM_B2) * (g * g)
        m_hat = m_new / (1.0 - ADAM_B1 ** ADAM_STEP)
        v_hat = v_new / (1.0 - ADAM_B2 ** ADAM_STEP)
        delta = -ADAM_LR * (m_hat / (jnp.sqrt(v_hat) + ADAM_EPS) + ADAM_WD * w_)
        return g, delta, m_new, v_new

    return _elementwise("adamw_" + name, fn, [w, m, v, *g_parts], 4)


SHARDED = ("meta_tokens", "w_in", "w_uq", "w_ukv", "w_out")
SHARD_AXIS = {"meta_tokens": 1, "w_in": 2, "w_uq": 2, "w_ukv": 2, "w_out": 1}
SMALL = ("rel_bias_table", "norm_in", "sink_a", "norm_q_lat", "norm_kv_lat", "norm_out_a", "norm_out_b", "norm_final")
WEIGHTS = ("meta_tokens", "rel_bias_table", "norm_in", "w_in", "sink_a", "norm_q_lat", "w_uq", "norm_kv_lat", "w_ukv",
           "norm_out_a", "norm_out_b", "w_out", "norm_final")
SMALL_ROWS, SMALL_COLS = 8, 1024


def _pack_small(d, loss=None):
    flat = [d[n].reshape(-1) for n in SMALL]
    if loss is not None:
        flat.append(loss.reshape(1))
    used = sum(f.shape[0] for f in flat)
    flat.append(jnp.zeros((SMALL_ROWS * SMALL_COLS - used,), F32))
    return jnp.concatenate(flat).reshape(SMALL_ROWS, SMALL_COLS)


def _unpack_small(p, like):
    flat, out, off = p.reshape(-1), {}, 0
    for n in SMALL:
        size = int(np.prod(like[n].shape))
        out[n] = flat[off:off + size].reshape(like[n].shape)
        off += size
    return out, flat[off]


def _split4(a, axis):
    size = a.shape[axis] // 4
    return jnp.stack([lax.slice_in_dim(a, k * size, (k + 1) * size, axis=axis) for k in range(4)])


def _train_step(cfg, x, target, w, m, v):
    shards = [w[n] if n == "meta_tokens" else w[n].astype(BF16) for n in SHARDED]
    gathered = _gather_shards(shards)
    full = {}
    for n, g4 in zip(SHARDED, gathered):
        full[n] = jnp.concatenate([g4[k] for k in range(4)], axis=SHARD_AXIS[n])

    loss_local, grad_x, g = _local_grads(
        cfg, x, target, full["meta_tokens"].astype(F32), w["rel_bias_table"], w["norm_in"], full["w_in"], w["sink_a"],
        w["norm_q_lat"], full["w_uq"], w["norm_kv_lat"], full["w_ukv"], w["norm_out_a"], w["norm_out_b"], full["w_out"],
        w["norm_final"])

    small_sum = _allreduce_small(_pack_small(g, loss_local))
    g_small, loss = _unpack_small(small_sum, {n: w[n] for n in SMALL})

    k_me = 2 * lax.axis_index("x") + lax.axis_index("y")
    split = [_split4(g[n], SHARD_AXIS[n]) for n in SHARDED]
    recv = _scatter_parts([s.astype(BF16) for s in split])
    partial = []
    for n, s, r in zip(SHARDED, split, recv):
        own = lax.dynamic_index_in_dim(s, k_me, 0, keepdims=False)
        partial.append(_sum_parts(n, _rows_view(own), r.reshape(3, -1, r.shape[-1])))
    sibling = _swap_sibling(partial)

    outs = {}
    for n, p_me, p_sib in zip(SHARDED, partial, sibling):
        res = _adamw(n, _rows_view(w[n]), _rows_view(m[n]), _rows_view(v[n]), [p_me, p_sib])
        outs[n] = [r.reshape(w[n].shape) for r in res]
    res = _adamw("small", _pack_small(w), _pack_small(m), _pack_small(v), [_pack_small(g_small)])
    unpacked = [_unpack_small(r, {n: w[n] for n in SMALL})[0] for r in res]
    for n in SMALL:
        outs[n] = [u[n] for u in unpacked]

    result = [loss, grad_x]
    for field in range(4):
        result.extend(outs[n][field] for n in WEIGHTS)
    return tuple(result)


def kernel(x, meta_tokens, rel_bias_table, norm_in, w_in, sink_a, norm_q_lat, w_uq, norm_kv_lat, w_ukv, norm_out_a, norm_out_b, w_out, norm_final, loss_target, m_meta_tokens, m_rel_bias_table, m_norm_in, m_w_in, m_sink_a, m_norm_q_lat, m_w_uq, m_norm_kv_lat, m_w_ukv, m_norm_out_a, m_norm_out_b, m_w_out, m_norm_final, v_meta_tokens, v_rel_bias_table, v_norm_in, v_w_in, v_sink_a, v_norm_q_lat, v_w_uq, v_norm_kv_lat, v_w_ukv, v_norm_out_a, v_norm_out_b, v_w_out, v_norm_final):
    w = dict(zip(WEIGHTS, (meta_tokens, rel_bias_table, norm_in, w_in, sink_a, norm_q_lat, w_uq, norm_kv_lat, w_ukv, norm_out_a, norm_out_b, w_out, norm_final)))
    m = dict(zip(WEIGHTS, (m_meta_tokens, m_rel_bias_table, m_norm_in, m_w_in, m_sink_a, m_norm_q_lat, m_w_uq, m_norm_kv_lat, m_w_ukv, m_norm_out_a, m_norm_out_b, m_w_out, m_norm_final)))
    v = dict(zip(WEIGHTS, (v_meta_tokens, v_rel_bias_table, v_norm_in, v_w_in, v_sink_a, v_norm_q_lat, v_w_uq, v_norm_kv_lat, v_w_ukv, v_norm_out_a, v_norm_out_b, v_w_out, v_norm_final)))
    cfg = make_cfg(x.shape[0], x.shape[1])
    return _train_step(cfg, x, loss_target, w, m, v)
```

```python
import collections
import functools
import math

import jax
import jax.numpy as jnp
import numpy as np
from jax import lax
from jax.experimental import pallas as pl
from jax.experimental.pallas import tpu as pltpu

F32 = jnp.float32
BF16 = jnp.bfloat16

BLK = 128
N_META = 16
D_MODEL = 1024
A_HEADS, A_KV, A_DH = 8, 2, 64
B_HEADS, B_NOPE, B_ROPE, B_DV = 8, 64, 32, 64
Q_RANK, KV_RANK = 256, 128
N_BUCKETS, MAX_DIST = 32, 128
ROPE_THETA = 10000.0
EPS = 1e-6
IN_WIDTH = 2208
W_IN_P = 2304
NEG = -1e30
VMEM_LIMIT = 48 * 1024 * 1024

ADAM_LR, ADAM_B1, ADAM_B2, ADAM_EPS, ADAM_WD, ADAM_STEP = 0.001, 0.9, 0.999, 1e-08, 0.01, 10

Cfg = collections.namedtuple("Cfg", "B S NB NJ LP TP")


def make_cfg(batch, seq):
    nb = seq // BLK
    nj = nb + 1
    return Cfg(batch, seq, nb, nj, nj * BLK, batch * nj * BLK)


def _cp(*sem):
    return pltpu.CompilerParams(dimension_semantics=sem, vmem_limit_bytes=VMEM_LIMIT)


def _dot(a, b):
    return jnp.dot(a, b, preferred_element_type=F32)


def _dot_nt(a, b):
    return lax.dot_general(a, b, (((1,), (1,)), ((), ())), preferred_element_type=F32)


def _dot_tn(a, b):
    return lax.dot_general(a, b, (((0,), (0,)), ((), ())), preferred_element_type=F32)


def _rms(x, width=None):
    n = x.shape[-1] if width is None else width
    r = lax.rsqrt(jnp.sum(x * x, axis=-1, keepdims=True) * (1.0 / n) + EPS)
    return x * r, r


def _rms_bwd(xhat, r, t):
    n = xhat.shape[-1]
    return r * (t - xhat * (jnp.sum(t * xhat, axis=-1, keepdims=True) * (1.0 / n)))


def _sigmoid(x):
    return 1.0 / (1.0 + jnp.exp(-x))


def _lane(shape):
    return lax.broadcasted_iota(jnp.int32, shape, len(shape) - 1)


def _swap_rope(x):
    n = x.shape[-1]
    lane = _lane(x.shape) % BLK
    up = pltpu.roll(x, n - 16, axis=x.ndim - 1)
    dn = pltpu.roll(x, 16, axis=x.ndim - 1)
    return jnp.where((lane >= 64) & (lane < 80), up, jnp.where((lane >= 80) & (lane < 96), dn, 0.0))


A_ORDER = (0, 4, 1, 5, 2, 6, 3, 7)


def _jtype(j, nj):
    return 0 if j == 0 else 1 if j == 1 else 3 if j == nj - 1 else 2


def _window_structure(nj):
    def pos(blk, r):
        return np.where(blk == 0, r, N_META + (blk - 1) * BLK + r)

    def valid(blk, r):
        return np.where(blk == 0, r < N_META, True)

    r = np.arange(BLK)
    rels, viss = [], []
    for j in (0, 1, 2, nj - 1):
        qpos = pos(j, r)[:, None]
        rel_t, vis_t = [], []
        for s, kb in enumerate((0, j - 1, j, j + 1)):
            slot_ok = (s == 0) or (1 <= kb <= nj - 1)
            kbc = min(max(kb, 0), nj - 1)
            kpos = pos(kbc, r)[None, :]
            rel = kpos - qpos
            v = valid(kbc, r)[None, :] & np.ones((BLK, 1), bool)
            if s > 0:
                v = v & (np.abs(rel) <= BLK)
            rel_t.append(rel)
            vis_t.append(v & slot_ok)
        rels.append(np.concatenate(rel_t, axis=1))
        viss.append(np.concatenate(vis_t, axis=1))
    return np.stack(rels).astype(np.int32), np.stack(viss)


def _t5_bucket(rel):
    nb = N_BUCKETS // 2
    max_exact = nb // 2
    ret = jnp.where(rel > 0, nb, 0)
    n = jnp.abs(rel)
    nf = jnp.maximum(n, 1).astype(jnp.float32)
    large = max_exact + (jnp.log(nf / max_exact) / math.log(MAX_DIST / max_exact) * (nb - max_exact)).astype(jnp.int32)
    large = jnp.minimum(large, nb - 1)
    return ret + jnp.where(n < max_exact, n, large)


def _perm_heads64(a, axis):
    parts = [lax.slice_in_dim(a, h * 64, (h + 1) * 64, axis=axis) for h in A_ORDER]
    return jnp.concatenate(parts, axis=axis)


def _unperm_heads64(a, axis):
    inv = [A_ORDER.index(h) for h in range(8)]
    parts = [lax.slice_in_dim(a, p * 64, (p + 1) * 64, axis=axis) for p in inv]
    return jnp.concatenate(parts, axis=axis)


def _w_in_to_p(w):
    sl = lambda a, b: lax.slice_in_dim(w, a, b, axis=1)
    z = lambda n: jnp.zeros((w.shape[0], n), w.dtype)
    return jnp.concatenate([_perm_heads64(sl(0, 512), 1), sl(512, 768), _perm_heads64(sl(768, 1280), 1), sl(1696, 2208),
                            sl(1280, 1536), sl(1536, 1664), z(64), sl(1664, 1696), z(32)], axis=1)


def _w_in_from_p(g):
    sl = lambda a, b: lax.slice_in_dim(g, a, b, axis=1)
    return jnp.concatenate([_unperm_heads64(sl(0, 512), 1), sl(512, 768), _unperm_heads64(sl(768, 1280), 1),
                            sl(1792, 2048), sl(2048, 2176), sl(2240, 2272), sl(1280, 1792)], axis=1)


def _w_uq_to_p(w):
    z = jnp.zeros((w.shape[0], 32), w.dtype)
    return jnp.concatenate([p for h in range(8) for p in (lax.slice_in_dim(w, h * 96, (h + 1) * 96, axis=1), z)], axis=1)


def _w_uq_from_p(g):
    return jnp.concatenate([lax.slice_in_dim(g, h * 128, h * 128 + 96, axis=1) for h in range(8)], axis=1)


def _w_ukv_to_p(w):
    z = jnp.zeros((w.shape[0], 64), w.dtype)
    ks = [p for h in range(8) for p in (lax.slice_in_dim(w, h * 128, h * 128 + 64, axis=1), z)]
    vs = [lax.slice_in_dim(w, h * 128 + 64, (h + 1) * 128, axis=1) for h in range(8)]
    return jnp.concatenate(ks + vs, axis=1)


def _w_ukv_from_p(g):
    parts = []
    for h in range(8):
        parts.append(lax.slice_in_dim(g, h * 128, h * 128 + 64, axis=1))
        parts.append(lax.slice_in_dim(g, 1024 + h * 64, 1024 + (h + 1) * 64, axis=1))
    return jnp.concatenate(parts, axis=1)


def _w_out_to_p(w):
    return jnp.concatenate([_perm_heads64(lax.slice_in_dim(w, 0, 512, axis=0), 0), lax.slice_in_dim(w, 512, 1024, axis=0)], axis=0)


def _w_out_from_p(g):
    return jnp.concatenate([_unperm_heads64(lax.slice_in_dim(g, 0, 512, axis=0), 0), lax.slice_in_dim(g, 512, 1024, axis=0)], axis=0)


def _rope_tables(cfg):
    half = B_ROPE // 2
    length = N_META + cfg.S
    freqs = ROPE_THETA ** (-jnp.arange(half, dtype=jnp.float32) / half)
    ang = jnp.arange(length, dtype=jnp.float32)[:, None] * freqs[None, :]
    cos, sin = jnp.cos(ang), jnp.sin(ang)

    def rows(t):
        return jnp.concatenate([t[:N_META], jnp.zeros((BLK - N_META, t.shape[1]), t.dtype), t[N_META:]], axis=0)

    ones = jnp.ones((length, 64), F32)
    zer = jnp.zeros((length, 32), F32)
    c_tab = rows(jnp.concatenate([ones, cos, cos, zer], axis=1))
    s_tab = rows(jnp.concatenate([zer, zer, -sin, sin, zer], axis=1))
    return c_tab, s_tab


def _inproj_fwd(cfg, h, g, w_p):
    tm = 256

    def body(h_ref, g_ref, w_ref, pa_ref, pf_ref):
        xh, _ = _rms(h_ref[...])
        u = (xh * g_ref[...]).astype(BF16)
        acc = _dot(u, w_ref[...])
        pa_ref[...] = acc[:, :768].astype(BF16)
        pf_ref[...] = acc[:, 768:]

    return pl.pallas_call(
        body, grid=(cfg.TP // tm,), name="inproj_fwd",
        in_specs=[pl.BlockSpec((tm, D_MODEL), lambda i: (i, 0)), pl.BlockSpec((1, D_MODEL), lambda i: (0, 0)),
                  pl.BlockSpec((D_MODEL, W_IN_P), lambda i: (0, 0))],
        out_specs=[pl.BlockSpec((tm, 768), lambda i: (i, 0)), pl.BlockSpec((tm, 1536), lambda i: (i, 0))],
        out_shape=[jax.ShapeDtypeStruct((cfg.TP, 768), BF16), jax.ShapeDtypeStruct((cfg.TP, 1536), F32)],
        compiler_params=_cp("parallel"),
    )(h, g, w_p)


def _lat_fwd(cfg, pf, gq, gkv, wq_p, wkv_p, c_tab, s_tab):
    nj = cfg.NJ

    def body(cq_ref, ckv_ref, kr_ref, gq_ref, gkv_ref, wq_ref, wkv_ref, c_ref, s_ref, q_ref, k_ref, v_ref):
        c1, s1 = c_ref[...], s_ref[...]
        c8, s8 = jnp.tile(c1, (1, 8)), jnp.tile(s1, (1, 8))
        xq, _ = _rms(cq_ref[...])
        qp = _dot((xq * gq_ref[...]).astype(BF16), wq_ref[...])
        q_ref[...] = (qp * c8 + _swap_rope(qp) * s8).astype(BF16)
        xk, _ = _rms(ckv_ref[...])
        kvp = _dot((xk * gkv_ref[...]).astype(BF16), wkv_ref[...])
        kr = kr_ref[...]
        krr = kr * c1 + _swap_rope(kr) * s1
        k_ref[...] = (kvp[:, :1024] + jnp.tile(krr, (1, 8))).astype(BF16)
        v_ref[...] = kvp[:, 1024:].astype(BF16)

    row = lambda b, j: b * nj + j
    return pl.pallas_call(
        body, grid=(cfg.B, nj), name="lat_fwd",
        in_specs=[pl.BlockSpec((BLK, 256), lambda b, j: (row(b, j), 4)), pl.BlockSpec((BLK, 128), lambda b, j: (row(b, j), 10)),
                  pl.BlockSpec((BLK, 128), lambda b, j: (row(b, j), 11)),
                  pl.BlockSpec((1, 256), lambda b, j: (0, 0)), pl.BlockSpec((1, 128), lambda b, j: (0, 0)),
                  pl.BlockSpec((256, 1024), lambda b, j: (0, 0)), pl.BlockSpec((128, 1536), lambda b, j: (0, 0)),
                  pl.BlockSpec((BLK, 128), lambda b, j: (j, 0)), pl.BlockSpec((BLK, 128), lambda b, j: (j, 0))],
        out_specs=[pl.BlockSpec((BLK, 1024), lambda b, j: (row(b, j), 0)), pl.BlockSpec((BLK, 1024), lambda b, j: (row(b, j), 0)),
                   pl.BlockSpec((BLK, 512), lambda b, j: (row(b, j), 0))],
        out_shape=[jax.ShapeDtypeStruct((cfg.TP, 1024), BF16), jax.ShapeDtypeStruct((cfg.TP, 1024), BF16),
                   jax.ShapeDtypeStruct((cfg.TP, 512), BF16)],
        compiler_params=_cp("parallel", "parallel"),
    )(pf, pf, pf, gq, gkv, wq_p, wkv_p, c_tab, s_tab)


def _gate_halves(ya, yb, ga, gb, goa, gob):
    xa, ra = _rms(ya)
    xb, rb = _rms(yb)
    sga, sgb = _sigmoid(ga), _sigmoid(gb)
    return xa, ra, xb, rb, sga, sgb, xa * goa * (ga * sga), xb * gob * (gb * sgb)


def _out_fwd(cfg, ya, yb, pf, goa, gob, wo_p, h):
    tm = 256

    def body(ya_ref, yb_ref, ga_ref, gb_ref, goa_ref, gob_ref, w_ref, h_ref, o_ref):
        *_, y_a, y_b = _gate_halves(ya_ref[...], yb_ref[...], ga_ref[...], gb_ref[...], goa_ref[...], gob_ref[...])
        y = jnp.concatenate([y_a, y_b], axis=1).astype(BF16)
        o_ref[...] = h_ref[...] + _dot(y, w_ref[...])

    return pl.pallas_call(
        body, grid=(cfg.TP // tm,), name="out_fwd",
        in_specs=[pl.BlockSpec((tm, 512), lambda i: (i, 0)), pl.BlockSpec((tm, 512), lambda i: (i, 0)),
                  pl.BlockSpec((tm, 512), lambda i: (i, 0)), pl.BlockSpec((tm, 512), lambda i: (i, 1)),
                  pl.BlockSpec((1, 512), lambda i: (0, 0)), pl.BlockSpec((1, 512), lambda i: (0, 0)),
                  pl.BlockSpec((D_MODEL, D_MODEL), lambda i: (0, 0)), pl.BlockSpec((tm, D_MODEL), lambda i: (i, 0))],
        out_specs=pl.BlockSpec((tm, D_MODEL), lambda i: (i, 0)),
        out_shape=jax.ShapeDtypeStruct((cfg.TP, D_MODEL), F32),
        compiler_params=_cp("parallel"),
    )(ya, yb, pf, pf, goa, gob, wo_p, h)


def _bias_build(table, bucket, maskadd):
    def body(tab_ref, bk_ref, ma_ref, o_ref):
        h = pl.program_id(1)
        bk = bk_ref[0]

        def step(b, acc):
            return jnp.where(bk == b, tab_ref[b, h], acc)

        o_ref[0, 0] = lax.fori_loop(0, N_BUCKETS, step, jnp.zeros(bk.shape, F32)) + ma_ref[0]

    return pl.pallas_call(
        body, grid=(4, A_HEADS), name="bias_build",
        in_specs=[pl.BlockSpec(memory_space=pltpu.SMEM), pl.BlockSpec((1, BLK, 512), lambda t, h: (t, 0, 0)),
                  pl.BlockSpec((1, BLK, 512), lambda t, h: (t, 0, 0))],
        out_specs=pl.BlockSpec((1, 1, BLK, 512), lambda t, h: (t, h, 0, 0)),
        out_shape=jax.ShapeDtypeStruct((4, A_HEADS, BLK, 512), F32),
        compiler_params=_cp("parallel", "parallel"),
    )(table, bucket, maskadd)


def _bias_grad(s_accs, bucket):
    depth = len(s_accs)

    def body(*refs):
        s_refs, bk_ref, o_ref, sum_ref, part_ref = refs[:depth], refs[depth], refs[depth + 1], refs[depth + 2], refs[depth + 3]
        t = pl.program_id(0)

        @pl.when(t == 0)
        def _():
            o_ref[...] = jnp.zeros_like(o_ref)

        total = s_refs[0][0]
        for extra in s_refs[1:]:
            total = total + extra[0]
        sum_ref[...] = total

        def step(b, carry):
            accs = [jnp.zeros((8, 512), F32) for _ in range(A_HEADS)]
            for g in range(BLK // 8):
                rows = pl.ds(g * 8, 8)
                hit = bk_ref[0, rows, :] == b
                for h in range(A_HEADS):
                    accs[h] = accs[h] + jnp.where(hit, sum_ref[h, rows, :], 0.0)
            rows8 = jnp.concatenate([jnp.sum(a, axis=0, keepdims=True) for a in accs], axis=0)
            part_ref[pl.ds(pl.multiple_of(b * A_HEADS, 8), A_HEADS), :] = rows8
            return carry

        lax.fori_loop(0, N_BUCKETS, step, 0)
        o_ref[...] += jnp.broadcast_to(jnp.sum(part_ref[...], axis=1, keepdims=True), o_ref.shape)

    s_spec = pl.BlockSpec((1, A_HEADS, BLK, 512), lambda t: (t, 0, 0, 0))
    return pl.pallas_call(
        body, grid=(4,), name="bias_grad",
        in_specs=[s_spec] * depth + [pl.BlockSpec((1, BLK, 512), lambda t: (t, 0, 0))],
        out_specs=pl.BlockSpec((N_BUCKETS * A_HEADS, 128), lambda t: (0, 0)),
        out_shape=jax.ShapeDtypeStruct((N_BUCKETS * A_HEADS, 128), F32),
        scratch_shapes=[pltpu.VMEM((A_HEADS, BLK, 512), F32), pltpu.VMEM((N_BUCKETS * A_HEADS, 512), F32)],
        compiler_params=_cp("arbitrary"),
    )(*s_accs, bucket)


def _win_specs(cfg):
    nj = cfg.NJ
    row = lambda b, j: b * nj + j
    jt = lambda j: jnp.where(j == 0, 0, jnp.where(j == 1, 1, jnp.where(j == nj - 1, 3, 2)))
    slot_rows = [lambda b, j: row(b, 0), lambda b, j: row(b, jnp.maximum(j - 1, 0)), lambda b, j: row(b, j),
                 lambda b, j: row(b, jnp.minimum(j + 1, nj - 1))]
    k_specs = [pl.BlockSpec((BLK, 128), functools.partial(lambda b, j, f: (f(b, j), 4), f=f)) for f in slot_rows]
    v_specs = [pl.BlockSpec((BLK, 128), functools.partial(lambda b, j, f: (f(b, j), 5), f=f)) for f in slot_rows]
    q_spec = pl.BlockSpec((BLK, 512), lambda b, j: (row(b, j), 0))
    bias_spec = pl.BlockSpec((1, A_HEADS, BLK, 512), lambda b, j: (jt(j), 0, 0, 0))
    return row, jt, q_spec, k_specs, v_specs, bias_spec


def _stack4(ref):
    return jnp.concatenate([ref[:, c * 128:(c + 1) * 128] for c in range(4)], axis=0)


def _win_keys(k_refs, v_refs):
    k4 = jnp.concatenate([r[...] for r in k_refs], axis=0)
    v4 = jnp.concatenate([r[...] for r in v_refs], axis=0)
    lane_k = _lane(k4.shape)
    return (jnp.where(lane_k < 64, k4, jnp.zeros_like(k4)), jnp.where(lane_k >= 64, k4, jnp.zeros_like(k4))), v4


def _sink_col(sink_ref, hf):
    rowi = lax.broadcasted_iota(jnp.int32, (4 * BLK, 1), 0)
    col = jnp.full((4 * BLK, 1), sink_ref[4 * hf + 3], F32)
    for c in (2, 1, 0):
        col = jnp.where(rowi < (c + 1) * BLK, sink_ref[4 * hf + c], col)
    return col


def _win_fwd(cfg, pa, bias, sink):
    row, jt, q_spec, k_specs, v_specs, bias_spec = _win_specs(cfg)
    scale = A_DH ** -0.5

    def body(sink_ref, q_ref, k0, k1, k2, k3, v0, v1, v2, v3, b_ref, o_ref, lse_ref):
        kk, v4 = _win_keys((k0, k1, k2, k3), (v0, v1, v2, v3))
        qs = _stack4(q_ref)
        lane_o = _lane((4 * BLK, 128))
        outs, lses = [], []
        for hf in range(2):
            s = _dot_nt(qs, kk[hf]) * scale + b_ref[0, 4 * hf:4 * hf + 4].reshape(4 * BLK, 512)
            sink_col = _sink_col(sink_ref, hf)
            m = jnp.maximum(jnp.max(s, axis=1, keepdims=True), sink_col)
            e = jnp.exp(s - m)
            den = jnp.sum(e, axis=1, keepdims=True) + jnp.exp(sink_col - m)
            outs.append(_dot(e.astype(BF16), v4) / den)
            lses.append(m + jnp.log(den))
        o = jnp.where(lane_o < 64, outs[0], outs[1])
        for c in range(4):
            o_ref[:, c * 128:(c + 1) * 128] = o[c * BLK:(c + 1) * BLK]
        lse_ref[...] = jnp.where(lane_o == 0, lses[0], jnp.where(lane_o == 1, lses[1], 0.0))

    return pl.pallas_call(
        body, grid=(cfg.B, cfg.NJ), name="win_fwd",
        in_specs=[pl.BlockSpec(memory_space=pltpu.SMEM), q_spec, *k_specs, *v_specs, bias_spec],
        out_specs=[pl.BlockSpec((BLK, 512), lambda b, j: (row(b, j), 0)), pl.BlockSpec((4 * BLK, 128), lambda b, j: (row(b, j), 0))],
        out_shape=[jax.ShapeDtypeStruct((cfg.TP, 512), F32), jax.ShapeDtypeStruct((4 * cfg.TP, 128), F32)],
        compiler_params=_cp("parallel", "parallel"),
    )(sink, pa, *([pa] * 8), bias)


def _win_bwd(cfg, pa, bias, sink, dya, ya, lse):
    row, jt, q_spec, k_specs, v_specs, bias_spec = _win_specs(cfg)
    nj = cfg.NJ
    scale = A_DH ** -0.5

    def body(sink_ref, q_ref, k0, k1, k2, k3, v0, v1, v2, v3, b_ref, dy_ref, y_ref, lse_ref,
             dq_ref, dkp_ref, dvp_ref, dkm_ref, dvm_ref, s_ref, dsink_ref):
        j = pl.program_id(1)
        kind = jt(j)

        @pl.when((pl.program_id(0) == 0) & (j == 0))
        def _():
            s_ref[...] = jnp.zeros_like(s_ref)

        kk, v4 = _win_keys((k0, k1, k2, k3), (v0, v1, v2, v3))
        qs, dys, ys = _stack4(q_ref), _stack4(dy_ref), _stack4(y_ref)
        lane_o = _lane((4 * BLK, 128))
        half = (lane_o < 64, lane_o >= 64)
        lse_blk = lse_ref[...]
        dq = jnp.zeros((4 * BLK, 128), F32)
        dk4 = jnp.zeros((512, 128), F32)
        dv4 = jnp.zeros((512, 128), F32)
        dsink = jnp.zeros((8, 128), F32)
        lane_s = _lane((8, 128))
        row_s = lax.broadcasted_iota(jnp.int32, (8, 128), 0)
        for hf in range(2):
            lse_h = jnp.sum(jnp.where(lane_o == hf, lse_blk, 0.0), axis=1, keepdims=True)
            s = _dot_nt(qs, kk[hf]) * scale + b_ref[0, 4 * hf:4 * hf + 4].reshape(4 * BLK, 512)
            p = jnp.exp(s - lse_h)
            do_h = jnp.where(half[hf], dys, 0.0)
            delta = jnp.sum(do_h * ys, axis=1, keepdims=True)
            do_b = do_h.astype(BF16)
            ds = p * (_dot_nt(do_b, v4) - delta)
            s_ref[kind, 4 * hf:4 * hf + 4] += ds.reshape(4, BLK, 512)
            sink_grad = jnp.exp(_sink_col(sink_ref, hf) - lse_h) * delta
            for c in range(4):
                tot = -jnp.sum(sink_grad[c * BLK:(c + 1) * BLK])
                dsink = jnp.where((row_s == 0) & (lane_s == 4 * hf + c), tot, dsink)
            dsb = (ds * scale).astype(BF16)
            dq = dq + _dot(dsb, kk[hf])
            dk4 = dk4 + _dot_tn(dsb, jnp.where(half[hf], qs, jnp.zeros_like(qs)))
            dv4 = dv4 + _dot_tn(p.astype(BF16), do_b)
        for c in range(4):
            dq_ref[:, c * 128:(c + 1) * 128] = dq[c * BLK:(c + 1) * BLK].astype(BF16)
        dkp_ref[0] = dk4
        dvp_ref[0] = dv4

        @pl.when(j == 0)
        def _():
            dkm_ref[...] = dk4[:BLK]
            dvm_ref[...] = dv4[:BLK]

        @pl.when(j > 0)
        def _():
            dkm_ref[...] += dk4[:BLK]
            dvm_ref[...] += dv4[:BLK]

        @pl.when((pl.program_id(0) == 0) & (j == 0))
        def _():
            dsink_ref[...] = dsink

        @pl.when((pl.program_id(0) > 0) | (j > 0))
        def _():
            dsink_ref[...] += dsink

    blk_row = pl.BlockSpec((BLK, 512), lambda b, j: (row(b, j), 0))
    return pl.pallas_call(
        body, grid=(cfg.B, nj), name="win_bwd",
        in_specs=[pl.BlockSpec(memory_space=pltpu.SMEM), q_spec, *k_specs, *v_specs, bias_spec, blk_row, blk_row,
                  pl.BlockSpec((4 * BLK, 128), lambda b, j: (row(b, j), 0))],
        out_specs=[blk_row,
                   pl.BlockSpec((1, 512, 128), lambda b, j: (row(b, j), 0, 0)), pl.BlockSpec((1, 512, 128), lambda b, j: (row(b, j), 0, 0)),
                   pl.BlockSpec((BLK, 128), lambda b, j: (b, 0)), pl.BlockSpec((BLK, 128), lambda b, j: (b, 0)),
                   pl.BlockSpec((4, A_HEADS, BLK, 512), lambda b, j: (0, 0, 0, 0)),
                   pl.BlockSpec((8, 128), lambda b, j: (0, 0))],
        out_shape=[jax.ShapeDtypeStruct((cfg.TP, 512), BF16),
                   jax.ShapeDtypeStruct((cfg.B * nj, 512, 128), F32), jax.ShapeDtypeStruct((cfg.B * nj, 512, 128), F32),
                   jax.ShapeDtypeStruct((cfg.B * BLK, 128), F32), jax.ShapeDtypeStruct((cfg.B * BLK, 128), F32),
                   jax.ShapeDtypeStruct((4, A_HEADS, BLK, 512), F32),
                   jax.ShapeDtypeStruct((8, 128), F32)],
        compiler_params=_cp("arbitrary", "arbitrary"),
    )(sink, pa, *([pa] * 8), bias, dya, ya, lse)


def _win_dkv_combine(cfg, dkp, dvp, dkm, dvm):
    nj = cfg.NJ
    row = lambda b, j: b * nj + j

    def body(k1, k2, k3, v1, v2, v3, km, vm, o_ref):
        j = pl.program_id(1)

        @pl.when(j == 0)
        def _():
            o_ref[:, :128] = km[...].astype(BF16)
            o_ref[:, 128:] = vm[...].astype(BF16)

        @pl.when(j > 0)
        def _():
            up = jnp.where(j + 1 <= nj - 1, 1.0, 0.0)
            o_ref[:, :128] = (k1[0] * up + k2[0] + k3[0]).astype(BF16)
            o_ref[:, 128:] = (v1[0] * up + v2[0] + v3[0]).astype(BF16)

    def part(slot, dj):
        return pl.BlockSpec((1, BLK, 128), lambda b, j: (row(b, jnp.clip(j + dj, 0, nj - 1)), slot, 0))

    return pl.pallas_call(
        body, grid=(cfg.B, nj), name="win_dkv_combine",
        in_specs=[part(1, 1), part(2, 0), part(3, -1), part(1, 1), part(2, 0), part(3, -1),
                  pl.BlockSpec((BLK, 128), lambda b, j: (b, 0)), pl.BlockSpec((BLK, 128), lambda b, j: (b, 0))],
        out_specs=pl.BlockSpec((BLK, 256), lambda b, j: (row(b, j), 0)),
        out_shape=jax.ShapeDtypeStruct((cfg.TP, 256), BF16),
        compiler_params=_cp("parallel", "parallel"),
    )(dkp, dkp, dkp, dvp, dvp, dvp, dkm, dvm)


def _mla_fwd(cfg, q, k, v, kmask):
    nj, lp = cfg.NJ, cfg.LP
    scale = (B_NOPE + B_ROPE) ** -0.5

    def body(q_ref, k_ref, v_ref, km_ref, o_ref, lse_ref):
        lane_o = _lane((BLK, 128))
        km = km_ref[...]
        outs = []
        lse_all = jnp.zeros((BLK, 128), F32)
        for hh in range(2):
            cols = slice(hh * 128, (hh + 1) * 128)
            qh = q_ref[:, cols]
            s0 = _dot_nt(qh, k_ref[:BLK, cols]) * scale + km
            s1 = _dot_nt(qh, k_ref[BLK:, cols]) * scale
            m = jnp.maximum(jnp.max(s0, axis=1, keepdims=True), jnp.max(s1, axis=1, keepdims=True))
            e0, e1 = jnp.exp(s0 - m), jnp.exp(s1 - m)
            den = jnp.sum(e0, axis=1, keepdims=True) + jnp.sum(e1, axis=1, keepdims=True)
            outs.append((_dot(e0.astype(BF16), v_ref[:BLK]) + _dot(e1.astype(BF16), v_ref[BLK:])) / den)
            lse_all = jnp.where(lane_o == hh, m + jnp.log(den), lse_all)
        o_ref[...] = jnp.where(lane_o < 64, outs[0], outs[1])
        lse_ref[0] = lse_all

    return pl.pallas_call(
        body, grid=(cfg.B, 4, nj), name="mla_fwd",
        in_specs=[pl.BlockSpec((BLK, 256), lambda b, p, i: (b * nj + i, p)), pl.BlockSpec((lp, 256), lambda b, p, i: (b, p)),
                  pl.BlockSpec((lp, 128), lambda b, p, i: (b, p)), pl.BlockSpec((1, BLK), lambda b, p, i: (0, 0))],
        out_specs=[pl.BlockSpec((BLK, 128), lambda b, p, i: (b * nj + i, p)),
                   pl.BlockSpec((1, BLK, 128), lambda b, p, i: (p, b * nj + i, 0))],
        out_shape=[jax.ShapeDtypeStruct((cfg.TP, 512), F32), jax.ShapeDtypeStruct((4, cfg.TP, 128), F32)],
        compiler_params=_cp("parallel", "parallel", "parallel"),
    )(q, k, v, kmask)


def _mla_bwd(cfg, q, k, v, kmask, dyb, yb, lse):
    nj, lp = cfg.NJ, cfg.LP
    scale = (B_NOPE + B_ROPE) ** -0.5

    def body(q_ref, k_ref, v_ref, km_ref, dy_ref, y_ref, lse_ref, dq_ref, dk_ref, dv_ref):
        i = pl.program_id(2)

        @pl.when(i == 0)
        def _():
            dk_ref[...] = jnp.zeros_like(dk_ref)
            dv_ref[...] = jnp.zeros_like(dv_ref)

        lane_o = _lane((BLK, 128))
        half = (lane_o < 64, lane_o >= 64)
        km = km_ref[...]
        dy = dy_ref[...]
        y = y_ref[...]
        lse_blk = lse_ref[0]
        for hh in range(2):
            cols = slice(hh * 128, (hh + 1) * 128)
            qh = q_ref[:, cols]
            lse_h = jnp.sum(jnp.where(lane_o == hh, lse_blk, 0.0), axis=1, keepdims=True)
            do_h = jnp.where(half[hh], dy, 0.0)
            delta = jnp.sum(do_h * y, axis=1, keepdims=True)
            do_b = do_h.astype(BF16)
            dq = jnp.zeros((BLK, 128), F32)
            for rows, mask in ((slice(0, BLK), km), (slice(BLK, lp), None)):
                kh = k_ref[rows, cols]
                s = _dot_nt(qh, kh) * scale
                p = jnp.exp((s if mask is None else s + mask) - lse_h)
                ds = p * (_dot_nt(do_b, v_ref[rows]) - delta)
                dsb = (ds * scale).astype(BF16)
                dq = dq + _dot(dsb, kh)
                dk_ref[rows, cols] += _dot_tn(dsb, qh)
                dv_ref[rows] += _dot_tn(p.astype(BF16), do_b)
            dq_ref[:, cols] = dq

    return pl.pallas_call(
        body, grid=(cfg.B, 4, nj), name="mla_bwd",
        in_specs=[pl.BlockSpec((BLK, 256), lambda b, p, i: (b * nj + i, p)), pl.BlockSpec((lp, 256), lambda b, p, i: (b, p)),
                  pl.BlockSpec((lp, 128), lambda b, p, i: (b, p)), pl.BlockSpec((1, BLK), lambda b, p, i: (0, 0)),
                  pl.BlockSpec((BLK, 128), lambda b, p, i: (b * nj + i, p)), pl.BlockSpec((BLK, 128), lambda b, p, i: (b * nj + i, p)),
                  pl.BlockSpec((1, BLK, 128), lambda b, p, i: (p, b * nj + i, 0))],
        out_specs=[pl.BlockSpec((BLK, 256), lambda b, p, i: (b * nj + i, p)), pl.BlockSpec((lp, 256), lambda b, p, i: (b, p)),
                   pl.BlockSpec((lp, 128), lambda b, p, i: (b, p))],
        out_shape=[jax.ShapeDtypeStruct((cfg.TP, 1024), F32), jax.ShapeDtypeStruct((cfg.TP, 1024), F32),
                   jax.ShapeDtypeStruct((cfg.TP, 512), F32)],
        compiler_params=_cp("parallel", "parallel", "arbitrary"),
    )(q, k, v, kmask, dyb, yb, lse)


def _loss_bwd(cfg, h, target, gf):
    nj, nb = cfg.NJ, cfg.NB

    def body(h_ref, t_ref, g_ref, dh_ref, loss_ref, dg_ref):
        b, j = pl.program_id(0), pl.program_id(1)

        @pl.when((b == 0) & (j == 0))
        def _():
            loss_ref[...] = jnp.zeros_like(loss_ref)
            dg_ref[...] = jnp.zeros_like(dg_ref)

        @pl.when(j == 0)
        def _():
            dh_ref[...] = jnp.zeros_like(dh_ref)

        @pl.when(j > 0)
        def _():
            g = g_ref[...]
            xh, r = _rms(h_ref[...])
            err = xh * g - t_ref[...]
            loss_ref[...] += jnp.where((lax.broadcasted_iota(jnp.int32, (8, 128), 0) == 0) & (_lane((8, 128)) == 0),
                                       (0.5 / D_MODEL) * jnp.sum(err * err), 0.0)
            dy = err * (1.0 / D_MODEL)
            dg_ref[...] += jnp.sum(dy * xh, axis=0, keepdims=True)
            dh_ref[...] = _rms_bwd(xh, r, dy * g)

    return pl.pallas_call(
        body, grid=(cfg.B, nj), name="loss_bwd",
        in_specs=[pl.BlockSpec((BLK, D_MODEL), lambda b, j: (b * nj + j, 0)),
                  pl.BlockSpec((BLK, D_MODEL), lambda b, j: (b * nb + jnp.maximum(j - 1, 0), 0)),
                  pl.BlockSpec((1, D_MODEL), lambda b, j: (0, 0))],
        out_specs=[pl.BlockSpec((BLK, D_MODEL), lambda b, j: (b * nj + j, 0)), pl.BlockSpec((8, 128), lambda b, j: (0, 0)),
                   pl.BlockSpec((1, D_MODEL), lambda b, j: (0, 0))],
        out_shape=[jax.ShapeDtypeStruct((cfg.TP, D_MODEL), F32), jax.ShapeDtypeStruct((8, 128), F32),
                   jax.ShapeDtypeStruct((1, D_MODEL), F32)],
        compiler_params=_cp("arbitrary", "arbitrary"),
    )(h, target, gf)


def _out_bwd(cfg, dh, ya, yb, pf, goa, gob, wo_p):
    tm = 256

    def body(dh_ref, ya_ref, yb_ref, ga_ref, gb_ref, goa_ref, gob_ref, w_ref,
             dya_ref, dyb_ref, dg_ref, dw_ref, dgoa_ref, dgob_ref):
        @pl.when(pl.program_id(0) == 0)
        def _():
            dw_ref[...] = jnp.zeros_like(dw_ref)
            dgoa_ref[...] = jnp.zeros_like(dgoa_ref)
            dgob_ref[...] = jnp.zeros_like(dgob_ref)

        ga, gb, goa, gob = ga_ref[...], gb_ref[...], goa_ref[...], gob_ref[...]
        xa, ra, xb, rb, sga, sgb, y_a, y_b = _gate_halves(ya_ref[...], yb_ref[...], ga, gb, goa, gob)
        dhb = dh_ref[...].astype(BF16)
        dw_ref[...] += _dot_tn(jnp.concatenate([y_a, y_b], axis=1).astype(BF16), dhb)
        dy = _dot_nt(dhb, w_ref[...])
        for (dyh, x, r, g, sg, go, dy_out, dgo_ref, col) in (
                (dy[:, :512], xa, ra, ga, sga, goa, dya_ref, dgoa_ref, 0), (dy[:, 512:], xb, rb, gb, sgb, gob, dyb_ref, dgob_ref, 512)):
            dn = dyh * (g * sg)
            dg_ref[:, col:col + 512] = (dyh * (x * go) * (sg * (1.0 + g * (1.0 - sg)))).astype(BF16)
            dgo_ref[...] += jnp.sum(dn * x, axis=0, keepdims=True)
            dy_out[...] = _rms_bwd(x, r, dn * go)

    half = lambda c: pl.BlockSpec((tm, 512), lambda i: (i, c))
    vec = pl.BlockSpec((1, 512), lambda i: (0, 0))
    return pl.pallas_call(
        body, grid=(cfg.TP // tm,), name="out_bwd",
        in_specs=[pl.BlockSpec((tm, D_MODEL), lambda i: (i, 0)), half(0), half(0), half(0), half(1), vec, vec,
                  pl.BlockSpec((D_MODEL, D_MODEL), lambda i: (0, 0))],
        out_specs=[half(0), half(0), pl.BlockSpec((tm, D_MODEL), lambda i: (i, 0)),
                   pl.BlockSpec((D_MODEL, D_MODEL), lambda i: (0, 0)), vec, vec],
        out_shape=[jax.ShapeDtypeStruct((cfg.TP, 512), F32), jax.ShapeDtypeStruct((cfg.TP, 512), F32),
                   jax.ShapeDtypeStruct((cfg.TP, D_MODEL), BF16), jax.ShapeDtypeStruct((D_MODEL, D_MODEL), F32),
                   jax.ShapeDtypeStruct((1, 512), F32), jax.ShapeDtypeStruct((1, 512), F32)],
        compiler_params=_cp("arbitrary"),
    )(dh, ya, yb, pf, pf, goa, gob, wo_p)


def _lat_bwd(cfg, dq, dk, dv, pf, gq, gkv, wq_p, wkv_p, c_tab, s_tab):
    nj = cfg.NJ

    def body(dq_ref, dk_ref, dv_ref, cq_ref, ckv_ref, gq_ref, gkv_ref, wq_ref, wkv_ref, c_ref, s_ref,
             dl_ref, dwq_ref, dwkv_ref, dgq_ref, dgkv_ref):
        @pl.when((pl.program_id(0) == 0) & (pl.program_id(1) == 0))
        def _():
            dwq_ref[...] = jnp.zeros_like(dwq_ref)
            dwkv_ref[...] = jnp.zeros_like(dwkv_ref)
            dgq_ref[...] = jnp.zeros_like(dgq_ref)
            dgkv_ref[...] = jnp.zeros_like(dgkv_ref)

        c1, s1 = c_ref[...], s_ref[...]
        c8, s8 = jnp.tile(c1, (1, 8)), jnp.tile(s1, (1, 8))
        dq_r = dq_ref[...]
        dqp = (dq_r * c8 + _swap_rope(dq_r * s8)).astype(BF16)
        gq = gq_ref[...]
        xq, rq = _rms(cq_ref[...])
        dwq_ref[...] += _dot_tn((xq * gq).astype(BF16), dqp)
        dn = _dot_nt(dqp, wq_ref[...])
        dgq_ref[...] += jnp.sum(dn * xq, axis=0, keepdims=True)
        dl_ref[:, :256] = _rms_bwd(xq, rq, dn * gq).astype(BF16)

        dk_r = dk_ref[...]
        dkr = dk_r[:, :128]
        for hd in range(1, 8):
            dkr = dkr + dk_r[:, hd * 128:(hd + 1) * 128]
        lane1 = _lane(dkr.shape)
        dkr = jnp.where((lane1 >= 64) & (lane1 < 96), dkr, 0.0)
        dl_ref[:, 384:] = (dkr * c1 + _swap_rope(dkr * s1)).astype(BF16)
        dkv = jnp.concatenate([dk_r, dv_ref[...]], axis=1).astype(BF16)
        gkv = gkv_ref[...]
        xk, rk = _rms(ckv_ref[...])
        dwkv_ref[...] += _dot_tn((xk * gkv).astype(BF16), dkv)
        dn2 = _dot_nt(dkv, wkv_ref[...])
        dgkv_ref[...] += jnp.sum(dn2 * xk, axis=0, keepdims=True)
        dl_ref[:, 256:384] = _rms_bwd(xk, rk, dn2 * gkv).astype(BF16)

    row = lambda b, j: b * nj + j
    const = lambda shape: pl.BlockSpec(shape, lambda b, j: (0, 0))
    return pl.pallas_call(
        body, grid=(cfg.B, nj), name="lat_bwd",
        in_specs=[pl.BlockSpec((BLK, 1024), lambda b, j: (row(b, j), 0)), pl.BlockSpec((BLK, 1024), lambda b, j: (row(b, j), 0)),
                  pl.BlockSpec((BLK, 512), lambda b, j: (row(b, j), 0)),
                  pl.BlockSpec((BLK, 256), lambda b, j: (row(b, j), 4)), pl.BlockSpec((BLK, 128), lambda b, j: (row(b, j), 10)),
                  const((1, 256)), const((1, 128)), const((256, 1024)), const((128, 1536)),
                  pl.BlockSpec((BLK, 128), lambda b, j: (j, 0)), pl.BlockSpec((BLK, 128), lambda b, j: (j, 0))],
        out_specs=[pl.BlockSpec((BLK, 512), lambda b, j: (row(b, j), 0)), const((256, 1024)), const((128, 1536)),
                   const((1, 256)), const((1, 128))],
        out_shape=[jax.ShapeDtypeStruct((cfg.TP, 512), BF16), jax.ShapeDtypeStruct((256, 1024), F32),
                   jax.ShapeDtypeStruct((128, 1536), F32), jax.ShapeDtypeStruct((1, 256), F32), jax.ShapeDtypeStruct((1, 128), F32)],
        compiler_params=_cp("arbitrary", "arbitrary"),
    )(dq, dk, dv, pf, pf, gq, gkv, wq_p, wkv_p, c_tab, s_tab)


def _inproj_bwd(cfg, h, g, w_p, dqa, dkva, dgate, dlat, dh):
    tm = 256

    def body(h_ref, g_ref, w_ref, dqa_ref, dkva_ref, dg_ref, dl_ref, dh_ref, o_ref, dw_ref, dgn_ref):
        @pl.when(pl.program_id(0) == 0)
        def _():
            dw_ref[...] = jnp.zeros_like(dw_ref)
            dgn_ref[...] = jnp.zeros_like(dgn_ref)

        g = g_ref[...]
        xh, r = _rms(h_ref[...])
        dproj = jnp.concatenate([dqa_ref[...], dkva_ref[...], dg_ref[...], dl_ref[...]], axis=1)
        dw_ref[...] += _dot_tn((xh * g).astype(BF16), dproj)
        du = _dot_nt(dproj, w_ref[...])
        dgn_ref[...] += jnp.sum(du * xh, axis=0, keepdims=True)
        o_ref[...] = dh_ref[...] + _rms_bwd(xh, r, du * g)

    rows = lambda w: pl.BlockSpec((tm, w), lambda i: (i, 0))
    return pl.pallas_call(
        body, grid=(cfg.TP // tm,), name="inproj_bwd",
        in_specs=[rows(D_MODEL), pl.BlockSpec((1, D_MODEL), lambda i: (0, 0)), pl.BlockSpec((D_MODEL, W_IN_P), lambda i: (0, 0)),
                  rows(512), rows(256), rows(1024), rows(512), rows(D_MODEL)],
        out_specs=[rows(D_MODEL), pl.BlockSpec((D_MODEL, W_IN_P), lambda i: (0, 0)), pl.BlockSpec((1, D_MODEL), lambda i: (0, 0))],
        out_shape=[jax.ShapeDtypeStruct((cfg.TP, D_MODEL), F32), jax.ShapeDtypeStruct((D_MODEL, W_IN_P), F32),
                   jax.ShapeDtypeStruct((1, D_MODEL), F32)],
        compiler_params=_cp("arbitrary"),
    )(h, g, w_p, dqa, dkva, dgate, dlat, dh)


def _meta_grad(cfg, dh):
    def body(d_ref, o_ref):
        @pl.when(pl.program_id(0) == 0)
        def _():
            o_ref[...] = d_ref[...]

        @pl.when(pl.program_id(0) > 0)
        def _():
            o_ref[...] += d_ref[...]

    return pl.pallas_call(
        body, grid=(cfg.B,), name="meta_grad",
        in_specs=[pl.BlockSpec((BLK, D_MODEL), lambda b: (b * cfg.NJ, 0))],
        out_specs=pl.BlockSpec((BLK, D_MODEL), lambda b: (0, 0)),
        out_shape=jax.ShapeDtypeStruct((BLK, D_MODEL), F32),
        compiler_params=_cp("arbitrary"),
    )(dh)


def _local_grads(cfg, x, target, meta, table, norm_in, w_in, sink_a, norm_q_lat, w_uq, norm_kv_lat, w_ukv,
                 norm_out_a, norm_out_b, w_out, norm_final):
    depth = w_in.shape[0]
    rel, vis = _window_structure(cfg.NJ)
    bucket = _t5_bucket(jnp.asarray(rel))
    maskadd = jnp.asarray(np.where(vis, 0.0, NEG).astype(np.float32))
    kvalid = np.concatenate([np.arange(BLK) < N_META] + [np.ones(BLK, bool)] * cfg.NB)
    kmask = jnp.asarray(np.where(kvalid, 0.0, NEG).astype(np.float32))[None, :]
    c_tab, s_tab = _rope_tables(cfg)
    bias = _bias_build(table, bucket, maskadd)

    meta_blk = jnp.concatenate([meta, jnp.zeros((BLK - N_META, D_MODEL), F32)], axis=0)
    h = jnp.concatenate([jnp.broadcast_to(meta_blk[None], (cfg.B, BLK, D_MODEL)), x], axis=1).reshape(cfg.TP, D_MODEL)

    wp = []
    for i in range(depth):
        wp.append(dict(
            w_in=_w_in_to_p(w_in[i]), w_uq=_w_uq_to_p(w_uq[i]), w_ukv=_w_ukv_to_p(w_ukv[i]), w_out=_w_out_to_p(w_out[i]),
            g_in=norm_in[i][None], gq=norm_q_lat[i][None], gkv=norm_kv_lat[i][None],
            goa=_perm_heads64(norm_out_a[i], 0)[None], gob=norm_out_b[i][None], sink=sink_a[i]))

    saved = []
    for i in range(depth):
        w = wp[i]
        pa, pf = _inproj_fwd(cfg, h, w["g_in"], w["w_in"])
        q, k, v = _lat_fwd(cfg, pf, w["gq"], w["gkv"], w["w_uq"], w["w_ukv"], c_tab, s_tab)
        ya, lse_a = _win_fwd(cfg, pa, bias, w["sink"])
        yb, lse_b = _mla_fwd(cfg, q, k, v, kmask)
        h_next = _out_fwd(cfg, ya, yb, pf, w["goa"], w["gob"], w["w_out"], h)
        saved.append(dict(h=h, pa=pa, pf=pf, q=q, k=k, v=v, ya=ya, lse_a=lse_a, yb=yb, lse_b=lse_b))
        h = h_next

    dh, loss_tile, d_norm_final = _loss_bwd(cfg, h, target.reshape(cfg.B * cfg.S, D_MODEL), norm_final[None])

    grads = {k_: [] for k_ in ("norm_in", "w_in", "sink_a", "norm_q_lat", "w_uq", "norm_kv_lat", "w_ukv", "norm_out_a", "norm_out_b", "w_out")}
    s_accs = []
    for i in reversed(range(depth)):
        w, sv = wp[i], saved[i]
        dya, dyb, dgate, dwo, dgoa, dgob = _out_bwd(cfg, dh, sv["ya"], sv["yb"], sv["pf"], w["goa"], w["gob"], w["w_out"])
        dqa, dkp, dvp, dkm, dvm, s_acc, dsink = _win_bwd(cfg, sv["pa"], bias, w["sink"], dya, sv["ya"], sv["lse_a"])
        dkva = _win_dkv_combine(cfg, dkp, dvp, dkm, dvm)
        dq, dk, dv = _mla_bwd(cfg, sv["q"], sv["k"], sv["v"], kmask, dyb, sv["yb"], sv["lse_b"])
        dlat, dwq, dwkv, dgq, dgkv = _lat_bwd(cfg, dq, dk, dv, sv["pf"], w["gq"], w["gkv"], w["w_uq"], w["w_ukv"], c_tab, s_tab)
        dh, dwin, dgin = _inproj_bwd(cfg, sv["h"], w["g_in"], w["w_in"], dqa, dkva, dgate, dlat, dh)
        s_accs.append(s_acc)
        grads["norm_in"].append(dgin[0])
        grads["w_in"].append(_w_in_from_p(dwin))
        grads["sink_a"].append(dsink[0, :A_HEADS])
        grads["norm_q_lat"].append(dgq[0])
        grads["w_uq"].append(_w_uq_from_p(dwq))
        grads["norm_kv_lat"].append(dgkv[0])
        grads["w_ukv"].append(_w_ukv_from_p(dwkv))
        grads["norm_out_a"].append(_unperm_heads64(dgoa[0], 0))
        grads["norm_out_b"].append(dgob[0])
        grads["w_out"].append(_w_out_from_p(dwo))

    out = {k_: jnp.stack(v_[::-1]) for k_, v_ in grads.items()}
    out["rel_bias_table"] = _bias_grad(s_accs, bucket)[:, 0].reshape(N_BUCKETS, A_HEADS)
    out["norm_final"] = d_norm_final[0]
    out["meta_tokens"] = _meta_grad(cfg, dh)[:N_META]
    return loss_tile[0, 0], dh.reshape(cfg.B, cfg.LP, D_MODEL)[:, BLK:], out


MESH = pl.DeviceIdType.MESH
ANY = pl.BlockSpec(memory_space=pl.ANY)


def _place():
    x, y, c = lax.axis_index("x"), lax.axis_index("y"), lax.axis_index("c")
    others = [(1 - x, y), (x, 1 - y), (1 - x, 1 - y)]
    return x, y, c, others


def _gather_shards(shards):
    n = len(shards)

    def body(*refs):
        ins, outs = refs[:n], refs[n:2 * n]
        send_sems, recv_sems, local_sems = refs[2 * n:]
        x, y, c, others = _place()
        k_me = 2 * x + y
        local = [pltpu.make_async_copy(ins[a], outs[a].at[k_me], local_sems.at[a]) for a in range(n)]
        for cp in local:
            cp.start()

        def copy(a, j, k_dst):
            return pltpu.make_async_remote_copy(src_ref=ins[a], dst_ref=outs[a].at[k_dst], send_sem=send_sems.at[3 * a + j],
                                                recv_sem=recv_sems.at[3 * a + j], device_id=(*others[j], c), device_id_type=MESH)

        sends = [copy(a, j, k_me) for a in range(n) for j in range(3)]
        for cp in sends:
            cp.start()
        for a in range(n):
            for j, (ox, oy) in enumerate(others):
                copy(a, j, 2 * ox + oy).wait_recv()
        for cp in sends:
            cp.wait_send()
        for cp in local:
            cp.wait()

    return pl.pallas_call(
        body, name="gather_shards", in_specs=[ANY] * n, out_specs=[ANY] * n,
        out_shape=[jax.ShapeDtypeStruct((4, *s.shape), s.dtype) for s in shards],
        scratch_shapes=[pltpu.SemaphoreType.DMA((3 * n,)), pltpu.SemaphoreType.DMA((3 * n,)), pltpu.SemaphoreType.DMA((n,))],
    )(*shards)


def _scatter_parts(parts):
    n = len(parts)

    def body(*refs):
        ins, outs = refs[:n], refs[n:2 * n]
        send_sems, recv_sems = refs[2 * n:]
        x, y, c, others = _place()

        def copy(a, j):
            ox, oy = others[j]
            return pltpu.make_async_remote_copy(src_ref=ins[a].at[2 * ox + oy], dst_ref=outs[a].at[j], send_sem=send_sems.at[3 * a + j],
                                                recv_sem=recv_sems.at[3 * a + j], device_id=(ox, oy, c), device_id_type=MESH)

        copies = [copy(a, j) for a in range(n) for j in range(3)]
        for cp in copies:
            cp.start()
        for cp in copies:
            cp.wait_recv()
        for cp in copies:
            cp.wait_send()

    return pl.pallas_call(
        body, name="scatter_parts", in_specs=[ANY] * n, out_specs=[ANY] * n,
        out_shape=[jax.ShapeDtypeStruct((3, *p.shape[1:]), p.dtype) for p in parts],
        scratch_shapes=[pltpu.SemaphoreType.DMA((3 * n,)), pltpu.SemaphoreType.DMA((3 * n,))],
    )(*parts)


def _swap_sibling(arrs):
    n = len(arrs)

    def body(*refs):
        ins, outs = refs[:n], refs[n:2 * n]
        send_sems, recv_sems = refs[2 * n:]
        x, y, c, _ = _place()
        copies = [pltpu.make_async_remote_copy(src_ref=ins[a], dst_ref=outs[a], send_sem=send_sems.at[a], recv_sem=recv_sems.at[a],
                                               device_id=(x, y, 1 - c), device_id_type=MESH) for a in range(n)]
        for cp in copies:
            cp.start()
        for cp in copies:
            cp.wait_recv()
        for cp in copies:
            cp.wait_send()

    return pl.pallas_call(
        body, name="swap_sibling", in_specs=[ANY] * n, out_specs=[ANY] * n,
        out_shape=[jax.ShapeDtypeStruct(a.shape, a.dtype) for a in arrs],
        scratch_shapes=[pltpu.SemaphoreType.DMA((n,)), pltpu.SemaphoreType.DMA((n,))],
    )(*arrs)


def _allreduce_small(v):
    def body(v_ref, o_ref, buf, send_sems, recv_sems):
        x, y, c, _ = _place()
        me = 4 * x + 2 * y + c
        buf[me] = v_ref[...]

        def copy(r):
            tx, ty, tc = (x + (r >> 2)) % 2, (y + ((r >> 1) & 1)) % 2, (c + (r & 1)) % 2
            return tx, ty, tc

        sends = []
        for r in range(1, 8):
            tx, ty, tc = copy(r)
            sends.append(pltpu.make_async_remote_copy(src_ref=v_ref, dst_ref=buf.at[me], send_sem=send_sems.at[r - 1],
                                                      recv_sem=recv_sems.at[r - 1], device_id=(tx, ty, tc), device_id_type=MESH))
        for cp in sends:
            cp.start()
        for r in range(1, 8):
            tx, ty, tc = copy(r)
            pltpu.make_async_remote_copy(src_ref=v_ref, dst_ref=buf.at[4 * tx + 2 * ty + tc], send_sem=send_sems.at[r - 1],
                                         recv_sem=recv_sems.at[r - 1], device_id=(tx, ty, tc), device_id_type=MESH).wait_recv()
        for cp in sends:
            cp.wait_send()
        acc = buf[0]
        for d in range(1, 8):
            acc = acc + buf[d]
        o_ref[...] = acc

    return pl.pallas_call(
        body, name="allreduce_small", in_specs=[pl.BlockSpec(memory_space=pltpu.VMEM)], out_specs=pl.BlockSpec(memory_space=pltpu.VMEM),
        out_shape=jax.ShapeDtypeStruct(v.shape, F32),
        scratch_shapes=[pltpu.VMEM((8, *v.shape), F32), pltpu.SemaphoreType.DMA((7,)), pltpu.SemaphoreType.DMA((7,))],
    )(v)


def _rows_view(a):
    return a.reshape(-1, a.shape[-1])


def _elementwise(name, fn, ins, n_out):
    rows, cols = ins[0].shape
    tm = min(rows, 256)
    spec = pl.BlockSpec((tm, cols), lambda i: (i, 0))

    def body(*refs):
        outs = fn(*[r[...] for r in refs[:len(ins)]])
        for o_ref, o in zip(refs[len(ins):], outs):
            o_ref[...] = o

    return pl.pallas_call(
        body, grid=(rows // tm,), name=name, in_specs=[spec] * len(ins), out_specs=[spec] * n_out,
        out_shape=[jax.ShapeDtypeStruct((rows, cols), F32)] * n_out, compiler_params=_cp("parallel"),
    )(*ins)


def _sum_parts(name, own, recv):
    def fn(o, r0, r1, r2):
        return (o + r0.astype(F32) + r1.astype(F32) + r2.astype(F32),)

    return _elementwise("sum_parts_" + name, fn, [own, recv[0], recv[1], recv[2]], 1)[0]


def _adamw(name, w, m, v, g_parts):
    def fn(w_, m_, v_, *gs):
        g = gs[0]
        for extra in gs[1:]:
            g = g + extra
        m_new = ADAM_B1 * m_ + (1.0 - ADAM_B1) * g
        v_new = ADAM_B2 * v_ + (1.0 - ADAM_B2) * (g * g)
        m_hat = m_new / (1.0 - ADAM_B1 ** ADAM_STEP)
        v_hat = v_new / (1.0 - ADAM_B2 ** ADAM_STEP)
        delta = -ADAM_LR * (m_hat / (jnp.sqrt(v_hat) + ADAM_EPS) + ADAM_WD * w_)
        return g, delta, m_new, v_new

    return _elementwise("adamw_" + name, fn, [w, m, v, *g_parts], 4)


SHARDED = ("meta_tokens", "w_in", "w_uq", "w_ukv", "w_out")
SHARD_AXIS = {"meta_tokens": 1, "w_in": 2, "w_uq": 2, "w_ukv": 2, "w_out": 1}
SMALL = ("rel_bias_table", "norm_in", "sink_a", "norm_q_lat", "norm_kv_lat", "norm_out_a", "norm_out_b", "norm_final")
WEIGHTS = ("meta_tokens", "rel_bias_table", "norm_in", "w_in", "sink_a", "norm_q_lat", "w_uq", "norm_kv_lat", "w_ukv",
           "norm_out_a", "norm_out_b", "w_out", "norm_final")
SMALL_ROWS, SMALL_COLS = 8, 1024


def _pack_small(d, loss=None):
    flat = [d[n].reshape(-1) for n in SMALL]
    if loss is not None:
        flat.append(loss.reshape(1))
    used = sum(f.shape[0] for f in flat)
    flat.append(jnp.zeros((SMALL_ROWS * SMALL_COLS - used,), F32))
    return jnp.concatenate(flat).reshape(SMALL_ROWS, SMALL_COLS)


def _unpack_small(p, like):
    flat, out, off = p.reshape(-1), {}, 0
    for n in SMALL:
        size = int(np.prod(like[n].shape))
        out[n] = flat[off:off + size].reshape(like[n].shape)
        off += size
    return out, flat[off]


def _split4(a, axis):
    size = a.shape[axis] // 4
    return jnp.stack([lax.slice_in_dim(a, k * size, (k + 1) * size, axis=axis) for k in range(4)])


def _train_step(cfg, x, target, w, m, v):
    shards = [w[n] if n == "meta_tokens" else w[n].astype(BF16) for n in SHARDED]
    gathered = _gather_shards(shards)
    full = {}
    for n, g4 in zip(SHARDED, gathered):
        full[n] = jnp.concatenate([g4[k] for k in range(4)], axis=SHARD_AXIS[n])

    loss_local, grad_x, g = _local_grads(
        cfg, x, target, full["meta_tokens"].astype(F32), w["rel_bias_table"], w["norm_in"], full["w_in"], w["sink_a"],
        w["norm_q_lat"], full["w_uq"], w["norm_kv_lat"], full["w_ukv"], w["norm_out_a"], w["norm_out_b"], full["w_out"],
        w["norm_final"])

    small_sum = _allreduce_small(_pack_small(g, loss_local))
    g_small, loss = _unpack_small(small_sum, {n: w[n] for n in SMALL})

    k_me = 2 * lax.axis_index("x") + lax.axis_index("y")
    split = [_split4(g[n], SHARD_AXIS[n]) for n in SHARDED]
    recv = _scatter_parts([s.astype(BF16) for s in split])
    partial = []
    for n, s, r in zip(SHARDED, split, recv):
        own = lax.dynamic_index_in_dim(s, k_me, 0, keepdims=False)
        partial.append(_sum_parts(n, _rows_view(own), r.reshape(3, -1, r.shape[-1])))
    sibling = _swap_sibling(partial)

    outs = {}
    for n, p_me, p_sib in zip(SHARDED, partial, sibling):
        res = _adamw(n, _rows_view(w[n]), _rows_view(m[n]), _rows_view(v[n]), [p_me, p_sib])
        outs[n] = [r.reshape(w[n].shape) for r in res]
    res = _adamw("small", _pack_small(w), _pack_small(m), _pack_small(v), [_pack_small(g_small)])
    unpacked = [_unpack_small(r, {n: w[n] for n in SMALL})[0] for r in res]
    for n in SMALL:
        outs[n] = [u[n] for u in unpacked]

    result = [loss, grad_x]
    for field in range(4):
        result.extend(outs[n][field] for n in WEIGHTS)
    return tuple(result)


def kernel(x, meta_tokens, rel_bias_table, norm_in, w_in, sink_a, norm_q_lat, w_uq, norm_kv_lat, w_ukv, norm_out_a, norm_out_b, w_out, norm_final, loss_target, m_meta_tokens, m_rel_bias_table, m_norm_in, m_w_in, m_sink_a, m_norm_q_lat, m_w_uq, m_norm_kv_lat, m_w_ukv, m_norm_out_a, m_norm_out_b, m_w_out, m_norm_final, v_meta_tokens, v_rel_bias_table, v_norm_in, v_w_in, v_sink_a, v_norm_q_lat, v_w_uq, v_norm_kv_lat, v_w_ukv, v_norm_out_a, v_norm_out_b, v_w_out, v_norm_final):
    w = dict(zip(WEIGHTS, (meta_tokens, rel_bias_table, norm_in, w_in, sink_a, norm_q_lat, w_uq, norm_kv_lat, w_ukv, norm_out_a, norm_out_b, w_out, norm_final)))
    m = dict(zip(WEIGHTS, (m_meta_tokens, m_rel_bias_table, m_norm_in, m_w_in, m_sink_a, m_norm_q_lat, m_w_uq, m_norm_kv_lat, m_w_ukv, m_norm_out_a, m_norm_out_b, m_w_out, m_norm_final)))
    v = dict(zip(WEIGHTS, (v_meta_tokens, v_rel_bias_table, v_norm_in, v_w_in, v_sink_a, v_norm_q_lat, v_w_uq, v_norm_kv_lat, v_w_ukv, v_norm_out_a, v_norm_out_b, v_w_out, v_norm_final)))
    cfg = make_cfg(x.shape[0], x.shape[1])
    return _train_step(cfg, x, loss_target, w, m, v)
```

```python
import collections
import functools
import math

import jax
import jax.numpy as jnp
import numpy as np
from jax import lax
from jax.experimental import pallas as pl
from jax.experimental.pallas import tpu as pltpu

F32 = jnp.float32
BF16 = jnp.bfloat16

BLK = 128
N_META = 16
D_MODEL = 1024
A_HEADS, A_KV, A_DH = 8, 2, 64
B_HEADS, B_NOPE, B_ROPE, B_DV = 8, 64, 32, 64
Q_RANK, KV_RANK = 256, 128
N_BUCKETS, MAX_DIST = 32, 128
ROPE_THETA = 10000.0
EPS = 1e-6
IN_WIDTH = 2208
W_IN_P = 2304
NEG = -1e30
MASK_LANE = 96
VMEM_LIMIT = 48 * 1024 * 1024

ADAM_LR, ADAM_B1, ADAM_B2, ADAM_EPS, ADAM_WD, ADAM_STEP = 0.001, 0.9, 0.999, 1e-08, 0.01, 10

Cfg = collections.namedtuple("Cfg", "B S NB NJ LP TP")


def make_cfg(batch, seq):
    nb = seq // BLK
    nj = nb + 1
    return Cfg(batch, seq, nb, nj, nj * BLK, batch * nj * BLK)


def _cp(*sem):
    return pltpu.CompilerParams(dimension_semantics=sem, vmem_limit_bytes=VMEM_LIMIT)


def _dot(a, b):
    return jnp.dot(a, b, preferred_element_type=F32)


def _dot_nt(a, b):
    return lax.dot_general(a, b, (((1,), (1,)), ((), ())), preferred_element_type=F32)


def _dot_tn(a, b):
    return lax.dot_general(a, b, (((0,), (0,)), ((), ())), preferred_element_type=F32)


def _rms(x, width=None):
    n = x.shape[-1] if width is None else width
    r = lax.rsqrt(jnp.sum(x * x, axis=-1, keepdims=True) * (1.0 / n) + EPS)
    return x * r, r


def _rms_bwd(xhat, r, t):
    n = xhat.shape[-1]
    return r * (t - xhat * (jnp.sum(t * xhat, axis=-1, keepdims=True) * (1.0 / n)))


def _sigmoid(x):
    return 1.0 / (1.0 + jnp.exp(-x))


def _lane(shape):
    return lax.broadcasted_iota(jnp.int32, shape, len(shape) - 1)


def _swap_rope(x):
    n = x.shape[-1]
    lane = _lane(x.shape) % BLK
    up = pltpu.roll(x, n - 16, axis=x.ndim - 1)
    dn = pltpu.roll(x, 16, axis=x.ndim - 1)
    return jnp.where((lane >= 64) & (lane < 80), up, jnp.where((lane >= 80) & (lane < 96), dn, 0.0))


A_ORDER = (0, 4, 1, 5, 2, 6, 3, 7)


def _jtype(j, nj):
    return 0 if j == 0 else 1 if j == 1 else 3 if j == nj - 1 else 2


def _window_structure(nj):
    def pos(blk, r):
        return np.where(blk == 0, r, N_META + (blk - 1) * BLK + r)

    def valid(blk, r):
        return np.where(blk == 0, r < N_META, True)

    r = np.arange(BLK)
    rels, viss = [], []
    for j in (0, 1, 2, nj - 1):
        qpos = pos(j, r)[:, None]
        rel_t, vis_t = [], []
        for s, kb in enumerate((0, j - 1, j, j + 1)):
            slot_ok = (s == 0) or (1 <= kb <= nj - 1)
            kbc = min(max(kb, 0), nj - 1)
            kpos = pos(kbc, r)[None, :]
            rel = kpos - qpos
            v = valid(kbc, r)[None, :] & np.ones((BLK, 1), bool)
            if s > 0:
                v = v & (np.abs(rel) <= BLK)
            rel_t.append(rel)
            vis_t.append(v & slot_ok)
        rels.append(np.concatenate(rel_t, axis=1))
        viss.append(np.concatenate(vis_t, axis=1))
    return np.stack(rels).astype(np.int32), np.stack(viss)


def _t5_bucket(rel):
    nb = N_BUCKETS // 2
    max_exact = nb // 2
    ret = jnp.where(rel > 0, nb, 0)
    n = jnp.abs(rel)
    nf = jnp.maximum(n, 1).astype(jnp.float32)
    large = max_exact + (jnp.log(nf / max_exact) / math.log(MAX_DIST / max_exact) * (nb - max_exact)).astype(jnp.int32)
    large = jnp.minimum(large, nb - 1)
    return ret + jnp.where(n < max_exact, n, large)


def _perm_heads64(a, axis):
    parts = [lax.slice_in_dim(a, h * 64, (h + 1) * 64, axis=axis) for h in A_ORDER]
    return jnp.concatenate(parts, axis=axis)


def _unperm_heads64(a, axis):
    inv = [A_ORDER.index(h) for h in range(8)]
    parts = [lax.slice_in_dim(a, p * 64, (p + 1) * 64, axis=axis) for p in inv]
    return jnp.concatenate(parts, axis=axis)


def _w_in_to_p(w):
    sl = lambda a, b: lax.slice_in_dim(w, a, b, axis=1)
    z = lambda n: jnp.zeros((w.shape[0], n), w.dtype)
    return jnp.concatenate([_perm_heads64(sl(0, 512), 1), sl(512, 768), _perm_heads64(sl(768, 1280), 1), sl(1696, 2208),
                            sl(1280, 1536), sl(1536, 1664), z(64), sl(1664, 1696), z(32)], axis=1)


def _w_in_from_p(g):
    sl = lambda a, b: lax.slice_in_dim(g, a, b, axis=1)
    return jnp.concatenate([_unperm_heads64(sl(0, 512), 1), sl(512, 768), _unperm_heads64(sl(768, 1280), 1),
                            sl(1792, 2048), sl(2048, 2176), sl(2240, 2272), sl(1280, 1792)], axis=1)


def _w_uq_to_p(w):
    z = jnp.zeros((w.shape[0], 32), w.dtype)
    return jnp.concatenate([p for h in range(8) for p in (lax.slice_in_dim(w, h * 96, (h + 1) * 96, axis=1), z)], axis=1)


def _w_uq_from_p(g):
    return jnp.concatenate([lax.slice_in_dim(g, h * 128, h * 128 + 96, axis=1) for h in range(8)], axis=1)


def _w_ukv_to_p(w):
    z = jnp.zeros((w.shape[0], 64), w.dtype)
    ks = [p for h in range(8) for p in (lax.slice_in_dim(w, h * 128, h * 128 + 64, axis=1), z)]
    vs = [lax.slice_in_dim(w, h * 128 + 64, (h + 1) * 128, axis=1) for h in range(8)]
    return jnp.concatenate(ks + vs, axis=1)


def _w_ukv_from_p(g):
    parts = []
    for h in range(8):
        parts.append(lax.slice_in_dim(g, h * 128, h * 128 + 64, axis=1))
        parts.append(lax.slice_in_dim(g, 1024 + h * 64, 1024 + (h + 1) * 64, axis=1))
    return jnp.concatenate(parts, axis=1)


def _w_out_to_p(w):
    return jnp.concatenate([_perm_heads64(lax.slice_in_dim(w, 0, 512, axis=0), 0), lax.slice_in_dim(w, 512, 1024, axis=0)], axis=0)


def _w_out_from_p(g):
    return jnp.concatenate([_unperm_heads64(lax.slice_in_dim(g, 0, 512, axis=0), 0), lax.slice_in_dim(g, 512, 1024, axis=0)], axis=0)


def _rope_tables(cfg):
    half = B_ROPE // 2
    length = N_META + cfg.S
    freqs = ROPE_THETA ** (-jnp.arange(half, dtype=jnp.float32) / half)
    ang = jnp.arange(length, dtype=jnp.float32)[:, None] * freqs[None, :]
    cos, sin = jnp.cos(ang), jnp.sin(ang)

    def rows(t):
        return jnp.concatenate([t[:N_META], jnp.zeros((BLK - N_META, t.shape[1]), t.dtype), t[N_META:]], axis=0)

    ones = jnp.ones((length, 64), F32)
    zer = jnp.zeros((length, 32), F32)
    c_tab = rows(jnp.concatenate([ones, cos, cos, zer], axis=1))
    s_tab = rows(jnp.concatenate([zer, zer, -sin, sin, zer], axis=1))
    return c_tab, s_tab


def _inproj_fwd(cfg, h, g, w_p):
    tm = 256

    def body(h_ref, g_ref, w_ref, pa_ref, pf_ref):
        xh, _ = _rms(h_ref[...])
        u = (xh * g_ref[...]).astype(BF16)
        acc = _dot(u, w_ref[...])
        pa_ref[...] = acc[:, :768].astype(BF16)
        pf_ref[...] = acc[:, 768:]

    return pl.pallas_call(
        body, grid=(cfg.TP // tm,), name="inproj_fwd",
        in_specs=[pl.BlockSpec((tm, D_MODEL), lambda i: (i, 0)), pl.BlockSpec((1, D_MODEL), lambda i: (0, 0)),
                  pl.BlockSpec((D_MODEL, W_IN_P), lambda i: (0, 0))],
        out_specs=[pl.BlockSpec((tm, 768), lambda i: (i, 0)), pl.BlockSpec((tm, 1536), lambda i: (i, 0))],
        out_shape=[jax.ShapeDtypeStruct((cfg.TP, 768), BF16), jax.ShapeDtypeStruct((cfg.TP, 1536), F32)],
        compiler_params=_cp("parallel"),
    )(h, g, w_p)


def _lat_fwd(cfg, pf, gq, gkv, wq_p, wkv_p, c_tab, s_tab):
    nj = cfg.NJ

    def body(cq_ref, ckv_ref, kr_ref, gq_ref, gkv_ref, wq_ref, wkv_ref, c_ref, s_ref, q_ref, k_ref, v_ref, kt_ref, vt_ref):
        c1, s1 = c_ref[...], s_ref[...]
        c8, s8 = jnp.tile(c1, (1, 8)), jnp.tile(s1, (1, 8))
        mask_lane = _lane((BLK, 1024)) % BLK == MASK_LANE
        zero_row = (pl.program_id(1) == 0) & (lax.broadcasted_iota(jnp.int32, (BLK, 1024), 0) >= N_META)
        xq, _ = _rms(cq_ref[...])
        qp = _dot((xq * gq_ref[...]).astype(BF16), wq_ref[...])
        q_ref[...] = jnp.where(mask_lane, 1.0, qp * c8 + _swap_rope(qp) * s8).astype(BF16)
        xk, _ = _rms(ckv_ref[...])
        kvp = _dot((xk * gkv_ref[...]).astype(BF16), wkv_ref[...])
        kr = kr_ref[...]
        krr = kr * c1 + _swap_rope(kr) * s1
        k = jnp.where(mask_lane & zero_row, NEG, kvp[:, :1024] + jnp.tile(krr, (1, 8)))
        k_ref[...] = k.astype(BF16)
        v_ref[...] = kvp[:, 1024:].astype(BF16)
        kt_ref[...] = k.T.astype(BF16)
        vt_ref[...] = kvp[:, 1024:].T.astype(BF16)

    row = lambda b, j: b * nj + j
    return pl.pallas_call(
        body, grid=(cfg.B, nj), name="lat_fwd",
        in_specs=[pl.BlockSpec((BLK, 256), lambda b, j: (row(b, j), 4)), pl.BlockSpec((BLK, 128), lambda b, j: (row(b, j), 10)),
                  pl.BlockSpec((BLK, 128), lambda b, j: (row(b, j), 11)),
                  pl.BlockSpec((1, 256), lambda b, j: (0, 0)), pl.BlockSpec((1, 128), lambda b, j: (0, 0)),
                  pl.BlockSpec((256, 1024), lambda b, j: (0, 0)), pl.BlockSpec((128, 1536), lambda b, j: (0, 0)),
                  pl.BlockSpec((BLK, 128), lambda b, j: (j, 0)), pl.BlockSpec((BLK, 128), lambda b, j: (j, 0))],
        out_specs=[pl.BlockSpec((BLK, 1024), lambda b, j: (row(b, j), 0)), pl.BlockSpec((BLK, 1024), lambda b, j: (row(b, j), 0)),
                   pl.BlockSpec((BLK, 512), lambda b, j: (row(b, j), 0)),
                   pl.BlockSpec((1024, BLK), lambda b, j: (b, j)), pl.BlockSpec((512, BLK), lambda b, j: (b, j))],
        out_shape=[jax.ShapeDtypeStruct((cfg.TP, 1024), BF16), jax.ShapeDtypeStruct((cfg.TP, 1024), BF16),
                   jax.ShapeDtypeStruct((cfg.TP, 512), BF16),
                   jax.ShapeDtypeStruct((cfg.B * 1024, cfg.LP), BF16), jax.ShapeDtypeStruct((cfg.B * 512, cfg.LP), BF16)],
        compiler_params=_cp("parallel", "parallel"),
    )(pf, pf, pf, gq, gkv, wq_p, wkv_p, c_tab, s_tab)


def _gate_halves(ya, yb, ga, gb, goa, gob):
    xa, ra = _rms(ya)
    xb, rb = _rms(yb)
    sga, sgb = _sigmoid(ga), _sigmoid(gb)
    return xa, ra, xb, rb, sga, sgb, xa * goa * (ga * sga), xb * gob * (gb * sgb)


def _out_fwd(cfg, ya, yb, pf, goa, gob, wo_p, h):
    tm = 256

    def body(ya_ref, yb_ref, ga_ref, gb_ref, goa_ref, gob_ref, w_ref, h_ref, o_ref):
        *_, y_a, y_b = _gate_halves(ya_ref[...], yb_ref[...], ga_ref[...], gb_ref[...], goa_ref[...], gob_ref[...])
        y = jnp.concatenate([y_a, y_b], axis=1).astype(BF16)
        o_ref[...] = h_ref[...] + _dot(y, w_ref[...])

    return pl.pallas_call(
        body, grid=(cfg.TP // tm,), name="out_fwd",
        in_specs=[pl.BlockSpec((tm, 512), lambda i: (i, 0)), pl.BlockSpec((tm, 512), lambda i: (i, 0)),
                  pl.BlockSpec((tm, 512), lambda i: (i, 0)), pl.BlockSpec((tm, 512), lambda i: (i, 1)),
                  pl.BlockSpec((1, 512), lambda i: (0, 0)), pl.BlockSpec((1, 512), lambda i: (0, 0)),
                  pl.BlockSpec((D_MODEL, D_MODEL), lambda i: (0, 0)), pl.BlockSpec((tm, D_MODEL), lambda i: (i, 0))],
        out_specs=pl.BlockSpec((tm, D_MODEL), lambda i: (i, 0)),
        out_shape=jax.ShapeDtypeStruct((cfg.TP, D_MODEL), F32),
        compiler_params=_cp("parallel"),
    )(ya, yb, pf, pf, goa, gob, wo_p, h)


def _bias_build(table, bucket, maskadd):
    def body(tab_ref, bk_ref, ma_ref, o_ref):
        h = pl.program_id(1)
        bk = bk_ref[0]

        def step(b, acc):
            return jnp.where(bk == b, tab_ref[b, h], acc)

        o_ref[0, 0] = lax.fori_loop(0, N_BUCKETS, step, jnp.zeros(bk.shape, F32)) + ma_ref[0]

    return pl.pallas_call(
        body, grid=(4, A_HEADS), name="bias_build",
        in_specs=[pl.BlockSpec(memory_space=pltpu.SMEM), pl.BlockSpec((1, BLK, 512), lambda t, h: (t, 0, 0)),
                  pl.BlockSpec((1, BLK, 512), lambda t, h: (t, 0, 0))],
        out_specs=pl.BlockSpec((1, 1, BLK, 512), lambda t, h: (t, h, 0, 0)),
        out_shape=jax.ShapeDtypeStruct((4, A_HEADS, BLK, 512), F32),
        compiler_params=_cp("parallel", "parallel"),
    )(table, bucket, maskadd)


def _bias_grad(s_accs, bucket):
    depth = len(s_accs)

    def body(*refs):
        s_refs, bk_ref, o_ref, sum_ref, part_ref = refs[:depth], refs[depth], refs[depth + 1], refs[depth + 2], refs[depth + 3]
        t = pl.program_id(0)

        @pl.when(t == 0)
        def _():
            o_ref[...] = jnp.zeros_like(o_ref)

        total = s_refs[0][0]
        for extra in s_refs[1:]:
            total = total + extra[0]
        sum_ref[...] = total

        def step(b, carry):
            accs = [jnp.zeros((8, 512), F32) for _ in range(A_HEADS)]
            for g in range(BLK // 8):
                rows = pl.ds(g * 8, 8)
                hit = bk_ref[0, rows, :] == b
                for h in range(A_HEADS):
                    accs[h] = accs[h] + jnp.where(hit, sum_ref[h, rows, :], 0.0)
            rows8 = jnp.concatenate([jnp.sum(a, axis=0, keepdims=True) for a in accs], axis=0)
            part_ref[pl.ds(pl.multiple_of(b * A_HEADS, 8), A_HEADS), :] = rows8
            return carry

        lax.fori_loop(0, N_BUCKETS, step, 0)
        o_ref[...] += jnp.broadcast_to(jnp.sum(part_ref[...], axis=1, keepdims=True), o_ref.shape)

    s_spec = pl.BlockSpec((1, A_HEADS, BLK, 512), lambda t: (t, 0, 0, 0))
    return pl.pallas_call(
        body, grid=(4,), name="bias_grad",
        in_specs=[s_spec] * depth + [pl.BlockSpec((1, BLK, 512), lambda t: (t, 0, 0))],
        out_specs=pl.BlockSpec((N_BUCKETS * A_HEADS, 128), lambda t: (0, 0)),
        out_shape=jax.ShapeDtypeStruct((N_BUCKETS * A_HEADS, 128), F32),
        scratch_shapes=[pltpu.VMEM((A_HEADS, BLK, 512), F32), pltpu.VMEM((N_BUCKETS * A_HEADS, 512), F32)],
        compiler_params=_cp("arbitrary"),
    )(*s_accs, bucket)


def _win_specs(cfg):
    nj = cfg.NJ
    row = lambda b, j: b * nj + j
    jt = lambda j: jnp.where(j == 0, 0, jnp.where(j == 1, 1, jnp.where(j == nj - 1, 3, 2)))
    slot_rows = [lambda b, j: row(b, 0), lambda b, j: row(b, jnp.maximum(j - 1, 0)), lambda b, j: row(b, j),
                 lambda b, j: row(b, jnp.minimum(j + 1, nj - 1))]
    k_specs = [pl.BlockSpec((BLK, 128), functools.partial(lambda b, j, f: (f(b, j), 4), f=f)) for f in slot_rows]
    v_specs = [pl.BlockSpec((BLK, 128), functools.partial(lambda b, j, f: (f(b, j), 5), f=f)) for f in slot_rows]
    q_spec = pl.BlockSpec((BLK, 512), lambda b, j: (row(b, j), 0))
    bias_spec = pl.BlockSpec((1, A_HEADS, BLK, 512), lambda b, j: (jt(j), 0, 0, 0))
    return row, jt, q_spec, k_specs, v_specs, bias_spec


def _stack4(ref):
    return jnp.concatenate([ref[:, c * 128:(c + 1) * 128] for c in range(4)], axis=0)


def _win_keys(k_refs, v_refs):
    k4 = jnp.concatenate([r[...] for r in k_refs], axis=0)
    v4 = jnp.concatenate([r[...] for r in v_refs], axis=0)
    lane_k = _lane(k4.shape)
    return (jnp.where(lane_k < 64, k4, jnp.zeros_like(k4)), jnp.where(lane_k >= 64, k4, jnp.zeros_like(k4))), v4


def _sink_col(sink_ref, hf):
    rowi = lax.broadcasted_iota(jnp.int32, (4 * BLK, 1), 0)
    col = jnp.full((4 * BLK, 1), sink_ref[4 * hf + 3], F32)
    for c in (2, 1, 0):
        col = jnp.where(rowi < (c + 1) * BLK, sink_ref[4 * hf + c], col)
    return col


def _win_fwd(cfg, pa, bias, sink):
    row, jt, q_spec, k_specs, v_specs, bias_spec = _win_specs(cfg)
    scale = A_DH ** -0.5

    def body(sink_ref, q_ref, k0, k1, k2, k3, v0, v1, v2, v3, b_ref, o_ref, lse_ref):
        kk, v4 = _win_keys((k0, k1, k2, k3), (v0, v1, v2, v3))
        qs = _stack4(q_ref)
        lane_o = _lane((4 * BLK, 128))
        outs, lses = [], []
        for hf in range(2):
            s = _dot_nt(qs, kk[hf]) * scale + b_ref[0, 4 * hf:4 * hf + 4].reshape(4 * BLK, 512)
            sink_col = _sink_col(sink_ref, hf)
            m = jnp.maximum(jnp.max(s, axis=1, keepdims=True), sink_col)
            e = jnp.exp(s - m)
            den = jnp.sum(e, axis=1, keepdims=True) + jnp.exp(sink_col - m)
            outs.append(_dot(e.astype(BF16), v4) / den)
            lses.append(m + jnp.log(den))
        o = jnp.where(lane_o < 64, outs[0], outs[1])
        for c in range(4):
            o_ref[:, c * 128:(c + 1) * 128] = o[c * BLK:(c + 1) * BLK]
        lse_ref[...] = jnp.where(lane_o == 0, lses[0], jnp.where(lane_o == 1, lses[1], 0.0))

    return pl.pallas_call(
        body, grid=(cfg.B, cfg.NJ), name="win_fwd",
        in_specs=[pl.BlockSpec(memory_space=pltpu.SMEM), q_spec, *k_specs, *v_specs, bias_spec],
        out_specs=[pl.BlockSpec((BLK, 512), lambda b, j: (row(b, j), 0)), pl.BlockSpec((4 * BLK, 128), lambda b, j: (row(b, j), 0))],
        out_shape=[jax.ShapeDtypeStruct((cfg.TP, 512), F32), jax.ShapeDtypeStruct((4 * cfg.TP, 128), F32)],
        compiler_params=_cp("parallel", "parallel"),
    )(sink, pa, *([pa] * 8), bias)


def _win_bwd(cfg, pa, bias, sink, dya, ya, lse):
    row, jt, q_spec, k_specs, v_specs, bias_spec = _win_specs(cfg)
    nj = cfg.NJ
    scale = A_DH ** -0.5

    def body(sink_ref, q_ref, k0, k1, k2, k3, v0, v1, v2, v3, b_ref, dy_ref, y_ref, lse_ref,
             dq_ref, dkp_ref, dvp_ref, dkm_ref, dvm_ref, s_ref, dsink_ref):
        j = pl.program_id(1)
        kind = jt(j)

        @pl.when((pl.program_id(0) == 0) & (j == 0))
        def _():
            s_ref[...] = jnp.zeros_like(s_ref)

        kk, v4 = _win_keys((k0, k1, k2, k3), (v0, v1, v2, v3))
        qs, dys, ys = _stack4(q_ref), _stack4(dy_ref), _stack4(y_ref)
        lane_o = _lane((4 * BLK, 128))
        half = (lane_o < 64, lane_o >= 64)
        lse_blk = lse_ref[...]
        dq = jnp.zeros((4 * BLK, 128), F32)
        dk4 = jnp.zeros((512, 128), F32)
        dv4 = jnp.zeros((512, 128), F32)
        dsink = jnp.zeros((8, 128), F32)
        lane_s = _lane((8, 128))
        row_s = lax.broadcasted_iota(jnp.int32, (8, 128), 0)
        for hf in range(2):
            lse_h = jnp.sum(jnp.where(lane_o == hf, lse_blk, 0.0), axis=1, keepdims=True)
            s = _dot_nt(qs, kk[hf]) * scale + b_ref[0, 4 * hf:4 * hf + 4].reshape(4 * BLK, 512)
            p = jnp.exp(s - lse_h)
            do_h = jnp.where(half[hf], dys, 0.0)
            delta = jnp.sum(do_h * ys, axis=1, keepdims=True)
            do_b = do_h.astype(BF16)
            ds = p * (_dot_nt(do_b, v4) - delta)
            s_ref[kind, 4 * hf:4 * hf + 4] += ds.reshape(4, BLK, 512)
            sink_grad = jnp.exp(_sink_col(sink_ref, hf) - lse_h) * delta
            for c in range(4):
                tot = -jnp.sum(sink_grad[c * BLK:(c + 1) * BLK])
                dsink = jnp.where((row_s == 0) & (lane_s == 4 * hf + c), tot, dsink)
            dsb = (ds * scale).astype(BF16)
            dq = dq + _dot(dsb, kk[hf])
            dk4 = dk4 + _dot_tn(dsb, jnp.where(half[hf], qs, jnp.zeros_like(qs)))
            dv4 = dv4 + _dot_tn(p.astype(BF16), do_b)
        for c in range(4):
            dq_ref[:, c * 128:(c + 1) * 128] = dq[c * BLK:(c + 1) * BLK].astype(BF16)
        dkp_ref[0] = dk4
        dvp_ref[0] = dv4

        @pl.when(j == 0)
        def _():
            dkm_ref[...] = dk4[:BLK]
            dvm_ref[...] = dv4[:BLK]

        @pl.when(j > 0)
        def _():
            dkm_ref[...] += dk4[:BLK]
            dvm_ref[...] += dv4[:BLK]

        @pl.when((pl.program_id(0) == 0) & (j == 0))
        def _():
            dsink_ref[...] = dsink

        @pl.when((pl.program_id(0) > 0) | (j > 0))
        def _():
            dsink_ref[...] += dsink

    blk_row = pl.BlockSpec((BLK, 512), lambda b, j: (row(b, j), 0))
    return pl.pallas_call(
        body, grid=(cfg.B, nj), name="win_bwd",
        in_specs=[pl.BlockSpec(memory_space=pltpu.SMEM), q_spec, *k_specs, *v_specs, bias_spec, blk_row, blk_row,
                  pl.BlockSpec((4 * BLK, 128), lambda b, j: (row(b, j), 0))],
        out_specs=[blk_row,
                   pl.BlockSpec((1, 512, 128), lambda b, j: (row(b, j), 0, 0)), pl.BlockSpec((1, 512, 128), lambda b, j: (row(b, j), 0, 0)),
                   pl.BlockSpec((BLK, 128), lambda b, j: (b, 0)), pl.BlockSpec((BLK, 128), lambda b, j: (b, 0)),
                   pl.BlockSpec((4, A_HEADS, BLK, 512), lambda b, j: (0, 0, 0, 0)),
                   pl.BlockSpec((8, 128), lambda b, j: (0, 0))],
        out_shape=[jax.ShapeDtypeStruct((cfg.TP, 512), BF16),
                   jax.ShapeDtypeStruct((cfg.B * nj, 512, 128), F32), jax.ShapeDtypeStruct((cfg.B * nj, 512, 128), F32),
                   jax.ShapeDtypeStruct((cfg.B * BLK, 128), F32), jax.ShapeDtypeStruct((cfg.B * BLK, 128), F32),
                   jax.ShapeDtypeStruct((4, A_HEADS, BLK, 512), F32),
                   jax.ShapeDtypeStruct((8, 128), F32)],
        compiler_params=_cp("arbitrary", "arbitrary"),
    )(sink, pa, *([pa] * 8), bias, dya, ya, lse)


def _win_dkv_combine(cfg, dkp, dvp, dkm, dvm):
    nj = cfg.NJ
    row = lambda b, j: b * nj + j

    def body(k1, k2, k3, v1, v2, v3, km, vm, o_ref):
        j = pl.program_id(1)

        @pl.when(j == 0)
        def _():
            o_ref[:, :128] = km[...].astype(BF16)
            o_ref[:, 128:] = vm[...].astype(BF16)

        @pl.when(j > 0)
        def _():
            up = jnp.where(j + 1 <= nj - 1, 1.0, 0.0)
            o_ref[:, :128] = (k1[0] * up + k2[0] + k3[0]).astype(BF16)
            o_ref[:, 128:] = (v1[0] * up + v2[0] + v3[0]).astype(BF16)

    def part(slot, dj):
        return pl.BlockSpec((1, BLK, 128), lambda b, j: (row(b, jnp.clip(j + dj, 0, nj - 1)), slot, 0))

    return pl.pallas_call(
        body, grid=(cfg.B, nj), name="win_dkv_combine",
        in_specs=[part(1, 1), part(2, 0), part(3, -1), part(1, 1), part(2, 0), part(3, -1),
                  pl.BlockSpec((BLK, 128), lambda b, j: (b, 0)), pl.BlockSpec((BLK, 128), lambda b, j: (b, 0))],
        out_specs=pl.BlockSpec((BLK, 256), lambda b, j: (row(b, j), 0)),
        out_shape=jax.ShapeDtypeStruct((cfg.TP, 256), BF16),
        compiler_params=_cp("parallel", "parallel"),
    )(dkp, dkp, dkp, dvp, dvp, dvp, dkm, dvm)


def _pair_blockdiag(q):
    lane = _lane(q.shape)
    return jnp.concatenate([jnp.where(lane < 128, q, jnp.zeros_like(q)), jnp.where(lane >= 128, q, jnp.zeros_like(q))], axis=0)


def _mla_fwd(cfg, q, kt, v, comm=None):
    nj, lp = cfg.NJ, cfg.LP
    scale = (B_NOPE + B_ROPE) ** -0.5

    def body(q_ref, kt_ref, v_ref, o_ref, lse_ref):
        lane_o = _lane((BLK, 128))
        s = _dot(_pair_blockdiag(q_ref[...]), kt_ref[...]) * scale
        m = jnp.max(s, axis=1, keepdims=True)
        e = jnp.exp(s - m)
        den = jnp.sum(e, axis=1, keepdims=True)
        pv = _dot(e.astype(BF16), v_ref[...]) / den
        o_ref[...] = jnp.where(lane_o < 64, pv[:BLK], pv[BLK:])
        lse_ref[0] = jnp.broadcast_to(m + jnp.log(den), (2 * BLK, 128))

    return _call_with_comm(
        body, comm, grid=(cfg.B, 4, nj), name="mla_fwd",
        in_specs=[pl.BlockSpec((BLK, 256), lambda b, p, i: (b * nj + i, p)), pl.BlockSpec((256, lp), lambda b, p, i: (b * 4 + p, 0)),
                  pl.BlockSpec((lp, 128), lambda b, p, i: (b, p))],
        out_specs=[pl.BlockSpec((BLK, 128), lambda b, p, i: (b * nj + i, p)),
                   pl.BlockSpec((1, 2 * BLK, 128), lambda b, p, i: (p, b * nj + i, 0))],
        out_shape=[jax.ShapeDtypeStruct((cfg.TP, 512), F32), jax.ShapeDtypeStruct((4, 2 * cfg.TP, 128), F32)],
        args=(q, kt, v))


def _mla_bwd(cfg, q, k, kt, vt, dyb, yb, lse, comm=None):
    nj, lp = cfg.NJ, cfg.LP
    scale = (B_NOPE + B_ROPE) ** -0.5

    def body(q_ref, k_ref, kt_ref, vt_ref, dy_ref, y_ref, lse_ref, dq_ref, dk_ref, dv_ref):
        i = pl.program_id(2)

        @pl.when(i == 0)
        def _():
            dk_ref[...] = jnp.zeros_like(dk_ref)
            dv_ref[...] = jnp.zeros_like(dv_ref)

        lane_o = _lane((BLK, 128))
        qbd = _pair_blockdiag(q_ref[...])
        dy, y = dy_ref[...], y_ref[...]
        do_s = jnp.concatenate([jnp.where(lane_o < 64, dy, 0.0), jnp.where(lane_o >= 64, dy, 0.0)], axis=0)
        delta = jnp.sum(do_s * jnp.concatenate([y, y], axis=0), axis=1, keepdims=True)
        do_b = do_s.astype(BF16)
        p = jnp.exp(_dot(qbd, kt_ref[...]) * scale - lse_ref[0][:, :1])
        ds = p * (_dot(do_b, vt_ref[...]) - delta)
        dsb = (ds * scale).astype(BF16)
        dq2 = _dot(dsb, k_ref[...])
        dq_ref[...] = jnp.where(_lane((BLK, 256)) < 128, dq2[:BLK], dq2[BLK:])
        dk_ref[...] += _dot_tn(dsb, qbd)
        dv_ref[...] += _dot_tn(p.astype(BF16), do_b)

    return _call_with_comm(
        body, comm, grid=(cfg.B, 4, nj), name="mla_bwd",
        in_specs=[pl.BlockSpec((BLK, 256), lambda b, p, i: (b * nj + i, p)), pl.BlockSpec((lp, 256), lambda b, p, i: (b, p)),
                  pl.BlockSpec((256, lp), lambda b, p, i: (b * 4 + p, 0)), pl.BlockSpec((128, lp), lambda b, p, i: (b * 4 + p, 0)),
                  pl.BlockSpec((BLK, 128), lambda b, p, i: (b * nj + i, p)), pl.BlockSpec((BLK, 128), lambda b, p, i: (b * nj + i, p)),
                  pl.BlockSpec((1, 2 * BLK, 128), lambda b, p, i: (p, b * nj + i, 0))],
        out_specs=[pl.BlockSpec((BLK, 256), lambda b, p, i: (b * nj + i, p)), pl.BlockSpec((lp, 256), lambda b, p, i: (b, p)),
                   pl.BlockSpec((lp, 128), lambda b, p, i: (b, p))],
        out_shape=[jax.ShapeDtypeStruct((cfg.TP, 1024), F32), jax.ShapeDtypeStruct((cfg.TP, 1024), F32),
                   jax.ShapeDtypeStruct((cfg.TP, 512), F32)],
        args=(q, k, kt, vt, dyb, yb, lse))


def _loss_bwd(cfg, h, target, gf):
    nj, nb = cfg.NJ, cfg.NB

    def body(h_ref, t_ref, g_ref, dh_ref, loss_ref, dg_ref):
        b, j = pl.program_id(0), pl.program_id(1)

        @pl.when((b == 0) & (j == 0))
        def _():
            loss_ref[...] = jnp.zeros_like(loss_ref)
            dg_ref[...] = jnp.zeros_like(dg_ref)

        @pl.when(j == 0)
        def _():
            dh_ref[...] = jnp.zeros_like(dh_ref)

        @pl.when(j > 0)
        def _():
            g = g_ref[...]
            xh, r = _rms(h_ref[...])
            err = xh * g - t_ref[...]
            loss_ref[...] += jnp.where((lax.broadcasted_iota(jnp.int32, (8, 128), 0) == 0) & (_lane((8, 128)) == 0),
                                       (0.5 / D_MODEL) * jnp.sum(err * err), 0.0)
            dy = err * (1.0 / D_MODEL)
            dg_ref[...] += jnp.sum(dy * xh, axis=0, keepdims=True)
            dh_ref[...] = _rms_bwd(xh, r, dy * g)

    return pl.pallas_call(
        body, grid=(cfg.B, nj), name="loss_bwd",
        in_specs=[pl.BlockSpec((BLK, D_MODEL), lambda b, j: (b * nj + j, 0)),
                  pl.BlockSpec((BLK, D_MODEL), lambda b, j: (b * nb + jnp.maximum(j - 1, 0), 0)),
                  pl.BlockSpec((1, D_MODEL), lambda b, j: (0, 0))],
        out_specs=[pl.BlockSpec((BLK, D_MODEL), lambda b, j: (b * nj + j, 0)), pl.BlockSpec((8, 128), lambda b, j: (0, 0)),
                   pl.BlockSpec((1, D_MODEL), lambda b, j: (0, 0))],
        out_shape=[jax.ShapeDtypeStruct((cfg.TP, D_MODEL), F32), jax.ShapeDtypeStruct((8, 128), F32),
                   jax.ShapeDtypeStruct((1, D_MODEL), F32)],
        compiler_params=_cp("arbitrary", "arbitrary"),
    )(h, target, gf)


def _out_bwd(cfg, dh, ya, yb, pf, goa, gob, wo_p):
    tm = 256

    def body(dh_ref, ya_ref, yb_ref, ga_ref, gb_ref, goa_ref, gob_ref, w_ref,
             dya_ref, dyb_ref, dg_ref, dw_ref, dgoa_ref, dgob_ref):
        @pl.when(pl.program_id(0) == 0)
        def _():
            dw_ref[...] = jnp.zeros_like(dw_ref)
            dgoa_ref[...] = jnp.zeros_like(dgoa_ref)
            dgob_ref[...] = jnp.zeros_like(dgob_ref)

        ga, gb, goa, gob = ga_ref[...], gb_ref[...], goa_ref[...], gob_ref[...]
        xa, ra, xb, rb, sga, sgb, y_a, y_b = _gate_halves(ya_ref[...], yb_ref[...], ga, gb, goa, gob)
        dhb = dh_ref[...].astype(BF16)
        dw_ref[...] += _dot_tn(jnp.concatenate([y_a, y_b], axis=1).astype(BF16), dhb)
        dy = _dot_nt(dhb, w_ref[...])
        for (dyh, x, r, g, sg, go, dy_out, dgo_ref, col) in (
                (dy[:, :512], xa, ra, ga, sga, goa, dya_ref, dgoa_ref, 0), (dy[:, 512:], xb, rb, gb, sgb, gob, dyb_ref, dgob_ref, 512)):
            dn = dyh * (g * sg)
            dg_ref[:, col:col + 512] = (dyh * (x * go) * (sg * (1.0 + g * (1.0 - sg)))).astype(BF16)
            dgo_ref[...] += jnp.sum(dn * x, axis=0, keepdims=True)
            dy_out[...] = _rms_bwd(x, r, dn * go)

    half = lambda c: pl.BlockSpec((tm, 512), lambda i: (i, c))
    vec = pl.BlockSpec((1, 512), lambda i: (0, 0))
    return pl.pallas_call(
        body, grid=(cfg.TP // tm,), name="out_bwd",
        in_specs=[pl.BlockSpec((tm, D_MODEL), lambda i: (i, 0)), half(0), half(0), half(0), half(1), vec, vec,
                  pl.BlockSpec((D_MODEL, D_MODEL), lambda i: (0, 0))],
        out_specs=[half(0), half(0), pl.BlockSpec((tm, D_MODEL), lambda i: (i, 0)),
                   pl.BlockSpec((D_MODEL, D_MODEL), lambda i: (0, 0)), vec, vec],
        out_shape=[jax.ShapeDtypeStruct((cfg.TP, 512), F32), jax.ShapeDtypeStruct((cfg.TP, 512), F32),
                   jax.ShapeDtypeStruct((cfg.TP, D_MODEL), BF16), jax.ShapeDtypeStruct((D_MODEL, D_MODEL), F32),
                   jax.ShapeDtypeStruct((1, 512), F32), jax.ShapeDtypeStruct((1, 512), F32)],
        compiler_params=_cp("arbitrary"),
    )(dh, ya, yb, pf, pf, goa, gob, wo_p)


def _lat_bwd(cfg, dq, dk, dv, pf, gq, gkv, wq_p, wkv_p, c_tab, s_tab):
    nj = cfg.NJ

    def body(dq_ref, dk_ref, dv_ref, cq_ref, ckv_ref, gq_ref, gkv_ref, wq_ref, wkv_ref, c_ref, s_ref,
             dl_ref, dwq_ref, dwkv_ref, dgq_ref, dgkv_ref):
        @pl.when((pl.program_id(0) == 0) & (pl.program_id(1) == 0))
        def _():
            dwq_ref[...] = jnp.zeros_like(dwq_ref)
            dwkv_ref[...] = jnp.zeros_like(dwkv_ref)
            dgq_ref[...] = jnp.zeros_like(dgq_ref)
            dgkv_ref[...] = jnp.zeros_like(dgkv_ref)

        c1, s1 = c_ref[...], s_ref[...]
        c8, s8 = jnp.tile(c1, (1, 8)), jnp.tile(s1, (1, 8))
        dq_r = dq_ref[...]
        dqp = (dq_r * c8 + _swap_rope(dq_r * s8)).astype(BF16)
        gq = gq_ref[...]
        xq, rq = _rms(cq_ref[...])
        dwq_ref[...] += _dot_tn((xq * gq).astype(BF16), dqp)
        dn = _dot_nt(dqp, wq_ref[...])
        dgq_ref[...] += jnp.sum(dn * xq, axis=0, keepdims=True)
        dl_ref[:, :256] = _rms_bwd(xq, rq, dn * gq).astype(BF16)

        dk_r = dk_ref[...]
        dkr = dk_r[:, :128]
        for hd in range(1, 8):
            dkr = dkr + dk_r[:, hd * 128:(hd + 1) * 128]
        lane1 = _lane(dkr.shape)
        dkr = jnp.where((lane1 >= 64) & (lane1 < 96), dkr, 0.0)
        dl_ref[:, 384:] = (dkr * c1 + _swap_rope(dkr * s1)).astype(BF16)
        dkv = jnp.concatenate([dk_r, dv_ref[...]], axis=1).astype(BF16)
        gkv = gkv_ref[...]
        xk, rk = _rms(ckv_ref[...])
        dwkv_ref[...] += _dot_tn((xk * gkv).astype(BF16), dkv)
        dn2 = _dot_nt(dkv, wkv_ref[...])
        dgkv_ref[...] += jnp.sum(dn2 * xk, axis=0, keepdims=True)
        dl_ref[:, 256:384] = _rms_bwd(xk, rk, dn2 * gkv).astype(BF16)

    row = lambda b, j: b * nj + j
    const = lambda shape: pl.BlockSpec(shape, lambda b, j: (0, 0))
    return pl.pallas_call(
        body, grid=(cfg.B, nj), name="lat_bwd",
        in_specs=[pl.BlockSpec((BLK, 1024), lambda b, j: (row(b, j), 0)), pl.BlockSpec((BLK, 1024), lambda b, j: (row(b, j), 0)),
                  pl.BlockSpec((BLK, 512), lambda b, j: (row(b, j), 0)),
                  pl.BlockSpec((BLK, 256), lambda b, j: (row(b, j), 4)), pl.BlockSpec((BLK, 128), lambda b, j: (row(b, j), 10)),
                  const((1, 256)), const((1, 128)), const((256, 1024)), const((128, 1536)),
                  pl.BlockSpec((BLK, 128), lambda b, j: (j, 0)), pl.BlockSpec((BLK, 128), lambda b, j: (j, 0))],
        out_specs=[pl.BlockSpec((BLK, 512), lambda b, j: (row(b, j), 0)), const((256, 1024)), const((128, 1536)),
                   const((1, 256)), const((1, 128))],
        out_shape=[jax.ShapeDtypeStruct((cfg.TP, 512), BF16), jax.ShapeDtypeStruct((256, 1024), F32),
                   jax.ShapeDtypeStruct((128, 1536), F32), jax.ShapeDtypeStruct((1, 256), F32), jax.ShapeDtypeStruct((1, 128), F32)],
        compiler_params=_cp("arbitrary", "arbitrary"),
    )(dq, dk, dv, pf, pf, gq, gkv, wq_p, wkv_p, c_tab, s_tab)


def _inproj_bwd(cfg, h, g, w_p, dqa, dkva, dgate, dlat, dh):
    tm = 256

    def body(h_ref, g_ref, w_ref, dqa_ref, dkva_ref, dg_ref, dl_ref, dh_ref, o_ref, dw_ref, dgn_ref):
        @pl.when(pl.program_id(0) == 0)
        def _():
            dw_ref[...] = jnp.zeros_like(dw_ref)
            dgn_ref[...] = jnp.zeros_like(dgn_ref)

        g = g_ref[...]
        xh, r = _rms(h_ref[...])
        dproj = jnp.concatenate([dqa_ref[...], dkva_ref[...], dg_ref[...], dl_ref[...]], axis=1)
        dw_ref[...] += _dot_tn((xh * g).astype(BF16), dproj)
        du = _dot_nt(dproj, w_ref[...])
        dgn_ref[...] += jnp.sum(du * xh, axis=0, keepdims=True)
        o_ref[...] = dh_ref[...] + _rms_bwd(xh, r, du * g)

    rows = lambda w: pl.BlockSpec((tm, w), lambda i: (i, 0))
    return pl.pallas_call(
        body, grid=(cfg.TP // tm,), name="inproj_bwd",
        in_specs=[rows(D_MODEL), pl.BlockSpec((1, D_MODEL), lambda i: (0, 0)), pl.BlockSpec((D_MODEL, W_IN_P), lambda i: (0, 0)),
                  rows(512), rows(256), rows(1024), rows(512), rows(D_MODEL)],
        out_specs=[rows(D_MODEL), pl.BlockSpec((D_MODEL, W_IN_P), lambda i: (0, 0)), pl.BlockSpec((1, D_MODEL), lambda i: (0, 0))],
        out_shape=[jax.ShapeDtypeStruct((cfg.TP, D_MODEL), F32), jax.ShapeDtypeStruct((D_MODEL, W_IN_P), F32),
                   jax.ShapeDtypeStruct((1, D_MODEL), F32)],
        compiler_params=_cp("arbitrary"),
    )(h, g, w_p, dqa, dkva, dgate, dlat, dh)


def _meta_grad(cfg, dh):
    def body(d_ref, o_ref):
        @pl.when(pl.program_id(0) == 0)
        def _():
            o_ref[...] = d_ref[...]

        @pl.when(pl.program_id(0) > 0)
        def _():
            o_ref[...] += d_ref[...]

    return pl.pallas_call(
        body, grid=(cfg.B,), name="meta_grad",
        in_specs=[pl.BlockSpec((BLK, D_MODEL), lambda b: (b * cfg.NJ, 0))],
        out_specs=pl.BlockSpec((BLK, D_MODEL), lambda b: (0, 0)),
        out_shape=jax.ShapeDtypeStruct((BLK, D_MODEL), F32),
        compiler_params=_cp("arbitrary"),
    )(dh)


MATRICES = ("w_in", "w_uq", "w_ukv", "w_out")


def _local_grads(cfg, x, target, meta, table, small, weights_of, fwd_comm=None, bwd_comm=None):
    depth = small["norm_in"].shape[0]
    rel, vis = _window_structure(cfg.NJ)
    bucket = _t5_bucket(jnp.asarray(rel))
    maskadd = jnp.asarray(np.where(vis, 0.0, NEG).astype(np.float32))
    c_tab, s_tab = _rope_tables(cfg)
    bias = _bias_build(table, bucket, maskadd)

    meta_blk = jnp.concatenate([meta, jnp.zeros((BLK - N_META, D_MODEL), F32)], axis=0)
    h = jnp.concatenate([jnp.broadcast_to(meta_blk[None], (cfg.B, BLK, D_MODEL)), x], axis=1).reshape(cfg.TP, D_MODEL)

    wp, saved = [], []
    for i in range(depth):
        w_in, w_uq, w_ukv, w_out = weights_of(i)
        w = dict(w_in=_w_in_to_p(w_in), w_uq=_w_uq_to_p(w_uq), w_ukv=_w_ukv_to_p(w_ukv), w_out=_w_out_to_p(w_out),
                 g_in=small["norm_in"][i][None], gq=small["norm_q_lat"][i][None], gkv=small["norm_kv_lat"][i][None],
                 goa=_perm_heads64(small["norm_out_a"][i], 0)[None], gob=small["norm_out_b"][i][None], sink=small["sink_a"][i])
        wp.append(w)
        hook = fwd_comm(i) if fwd_comm else None
        pa, pf = _inproj_fwd(cfg, h, w["g_in"], w["w_in"])
        q, k, v, kt, vt = _lat_fwd(cfg, pf, w["gq"], w["gkv"], w["w_uq"], w["w_ukv"], c_tab, s_tab)
        ya, lse_a = _win_fwd(cfg, pa, bias, w["sink"])
        yb, lse_b, *travelled = _mla_fwd(cfg, q, kt, v, hook[0] if hook else None)
        if hook:
            hook[1](travelled)
        h_next = _out_fwd(cfg, ya, yb, pf, w["goa"], w["gob"], w["w_out"], h)
        saved.append(dict(h=h, pa=pa, pf=pf, q=q, k=k, kt=kt, vt=vt, ya=ya, lse_a=lse_a, yb=yb, lse_b=lse_b))
        h = h_next

    dh, loss_tile, d_norm_final = _loss_bwd(cfg, h, target.reshape(cfg.B * cfg.S, D_MODEL), small["norm_final"][None])

    grads = {k_: [] for k_ in ("norm_in", "sink_a", "norm_q_lat", "norm_kv_lat", "norm_out_a", "norm_out_b")}
    mats, s_accs = {}, []
    for i in reversed(range(depth)):
        w, sv = wp[i], saved[i]
        hook = bwd_comm(i, mats) if bwd_comm else None
        dya, dyb, dgate, dwo, dgoa, dgob = _out_bwd(cfg, dh, sv["ya"], sv["yb"], sv["pf"], w["goa"], w["gob"], w["w_out"])
        dqa, dkp, dvp, dkm, dvm, s_acc, dsink = _win_bwd(cfg, sv["pa"], bias, w["sink"], dya, sv["ya"], sv["lse_a"])
        dkva = _win_dkv_combine(cfg, dkp, dvp, dkm, dvm)
        dq, dk, dv, *travelled = _mla_bwd(cfg, sv["q"], sv["k"], sv["kt"], sv["vt"], dyb, sv["yb"], sv["lse_b"], hook[0] if hook else None)
        if hook:
            hook[1](travelled)
        dlat, dwq, dwkv, dgq, dgkv = _lat_bwd(cfg, dq, dk, dv, sv["pf"], w["gq"], w["gkv"], w["w_uq"], w["w_ukv"], c_tab, s_tab)
        dh, dwin, dgin = _inproj_bwd(cfg, sv["h"], w["g_in"], w["w_in"], dqa, dkva, dgate, dlat, dh)
        s_accs.append(s_acc)
        mats[i] = dict(w_in=_w_in_from_p(dwin), w_uq=_w_uq_from_p(dwq), w_ukv=_w_ukv_from_p(dwkv), w_out=_w_out_from_p(dwo))
        grads["norm_in"].append(dgin[0])
        grads["sink_a"].append(dsink[0, :A_HEADS])
        grads["norm_q_lat"].append(dgq[0])
        grads["norm_kv_lat"].append(dgkv[0])
        grads["norm_out_a"].append(_unperm_heads64(dgoa[0], 0))
        grads["norm_out_b"].append(dgob[0])

    out = {k_: jnp.stack(v_[::-1]) for k_, v_ in grads.items()}
    out["rel_bias_table"] = _bias_grad(s_accs, bucket)[:, 0].reshape(N_BUCKETS, A_HEADS)
    out["norm_final"] = d_norm_final[0]
    return loss_tile[0, 0], dh.reshape(cfg.B, cfg.LP, D_MODEL)[:, BLK:], out, mats, _meta_grad(cfg, dh)[:N_META]


MESH = pl.DeviceIdType.MESH
ANY = pl.BlockSpec(memory_space=pl.ANY)


def _place():
    x, y, c = lax.axis_index("x"), lax.axis_index("y"), lax.axis_index("c")
    others = [(1 - x, y), (x, 1 - y), (1 - x, 1 - y)]
    return x, y, c, others


Comm = collections.namedtuple("Comm", "inputs out_shapes scratch start wait")


def _gather_comm(shards):
    n = len(shards)

    def copies(ins, outs, sems, arriving):
        send_sems, recv_sems, local_sems = sems
        x, y, c, others = _place()
        k_me = 2 * x + y
        local = [pltpu.make_async_copy(ins[a], outs[a].at[k_me], local_sems.at[a]) for a in range(n)]
        remote = [pltpu.make_async_remote_copy(src_ref=ins[a], dst_ref=outs[a].at[2 * ox + oy if arriving else k_me],
                                               send_sem=send_sems.at[3 * a + j], recv_sem=recv_sems.at[3 * a + j],
                                               device_id=(ox, oy, c), device_id_type=MESH)
                  for a in range(n) for j, (ox, oy) in enumerate(others)]
        return local, remote

    def start(ins, outs, sems):
        local, sends = copies(ins, outs, sems, arriving=False)
        for cp in local + sends:
            cp.start()

    def wait(ins, outs, sems):
        local, recvs = copies(ins, outs, sems, arriving=True)
        for cp in recvs:
            cp.wait_recv()
        for cp in recvs:
            cp.wait_send()
        for cp in local:
            cp.wait()

    return Comm(list(shards), [jax.ShapeDtypeStruct((4, *s.shape), s.dtype) for s in shards],
                [pltpu.SemaphoreType.DMA((3 * n,)), pltpu.SemaphoreType.DMA((3 * n,)), pltpu.SemaphoreType.DMA((n,))], start, wait)


def _scatter_comm(parts):
    n = len(parts)

    def copies(ins, outs, sems):
        send_sems, recv_sems = sems
        x, y, c, others = _place()
        return [pltpu.make_async_remote_copy(src_ref=ins[a].at[2 * ox + oy], dst_ref=outs[a].at[j], send_sem=send_sems.at[3 * a + j],
                                             recv_sem=recv_sems.at[3 * a + j], device_id=(ox, oy, c), device_id_type=MESH)
                for a in range(n) for j, (ox, oy) in enumerate(others)]

    def start(ins, outs, sems):
        for cp in copies(ins, outs, sems):
            cp.start()

    def wait(ins, outs, sems):
        cps = copies(ins, outs, sems)
        for cp in cps:
            cp.wait_recv()
        for cp in cps:
            cp.wait_send()

    return Comm(list(parts), [jax.ShapeDtypeStruct((3, *p.shape[1:]), p.dtype) for p in parts],
                [pltpu.SemaphoreType.DMA((3 * n,)), pltpu.SemaphoreType.DMA((3 * n,))], start, wait)


def _run_comm(comm, name):
    ni, no = len(comm.inputs), len(comm.out_shapes)

    def body(*refs):
        ins, outs, sems = refs[:ni], refs[ni:ni + no], refs[ni + no:]
        comm.start(ins, outs, sems)
        comm.wait(ins, outs, sems)

    return pl.pallas_call(body, name=name, in_specs=[ANY] * ni, out_specs=[ANY] * no, out_shape=comm.out_shapes,
                          scratch_shapes=comm.scratch)(*comm.inputs)


def _call_with_comm(body, comm, *, grid, name, in_specs, out_specs, out_shape, args):
    if comm is None:
        return pl.pallas_call(body, grid=grid, name=name, in_specs=in_specs, out_specs=out_specs, out_shape=out_shape,
                              compiler_params=_cp(*["arbitrary"] * len(grid)))(*args)
    n_in, n_out, ci, co = len(in_specs), len(out_specs), len(comm.inputs), len(comm.out_shapes)

    def wrapped(*refs):
        ins, cins = refs[:n_in], refs[n_in:n_in + ci]
        outs, couts = refs[n_in + ci:n_in + ci + n_out], refs[n_in + ci + n_out:n_in + ci + n_out + co]
        sems = refs[n_in + ci + n_out + co:]
        ids = [pl.program_id(a) for a in range(len(grid))]
        first = functools.reduce(jnp.logical_and, [i == 0 for i in ids])
        last = functools.reduce(jnp.logical_and, [i == g - 1 for i, g in zip(ids, grid)])

        @pl.when(first)
        def _():
            comm.start(cins, couts, sems)

        body(*ins, *outs)

        @pl.when(last)
        def _():
            comm.wait(cins, couts, sems)

    return pl.pallas_call(
        wrapped, grid=grid, name=name + "_comm", in_specs=[*in_specs, *[ANY] * ci], out_specs=[*out_specs, *[ANY] * co],
        out_shape=[*out_shape, *comm.out_shapes], scratch_shapes=comm.scratch,
        compiler_params=_cp(*["arbitrary"] * len(grid)))(*args, *comm.inputs)


def _swap_sibling(arrs):
    n = len(arrs)

    def body(*refs):
        ins, outs = refs[:n], refs[n:2 * n]
        send_sems, recv_sems = refs[2 * n:]
        x, y, c, _ = _place()
        copies = [pltpu.make_async_remote_copy(src_ref=ins[a], dst_ref=outs[a], send_sem=send_sems.at[a], recv_sem=recv_sems.at[a],
                                               device_id=(x, y, 1 - c), device_id_type=MESH) for a in range(n)]
        for cp in copies:
            cp.start()
        for cp in copies:
            cp.wait_recv()
        for cp in copies:
            cp.wait_send()

    return pl.pallas_call(
        body, name="swap_sibling", in_specs=[ANY] * n, out_specs=[ANY] * n,
        out_shape=[jax.ShapeDtypeStruct(a.shape, a.dtype) for a in arrs],
        scratch_shapes=[pltpu.SemaphoreType.DMA((n,)), pltpu.SemaphoreType.DMA((n,))],
    )(*arrs)


def _allreduce_small(v):
    def body(v_ref, o_ref, buf, send_sems, recv_sems):
        x, y, c, _ = _place()
        me = 4 * x + 2 * y + c
        buf[me] = v_ref[...]

        def copy(r):
            tx, ty, tc = (x + (r >> 2)) % 2, (y + ((r >> 1) & 1)) % 2, (c + (r & 1)) % 2
            return tx, ty, tc

        sends = []
        for r in range(1, 8):
            tx, ty, tc = copy(r)
            sends.append(pltpu.make_async_remote_copy(src_ref=v_ref, dst_ref=buf.at[me], send_sem=send_sems.at[r - 1],
                                                      recv_sem=recv_sems.at[r - 1], device_id=(tx, ty, tc), device_id_type=MESH))
        for cp in sends:
            cp.start()
        for r in range(1, 8):
            tx, ty, tc = copy(r)
            pltpu.make_async_remote_copy(src_ref=v_ref, dst_ref=buf.at[4 * tx + 2 * ty + tc], send_sem=send_sems.at[r - 1],
                                         recv_sem=recv_sems.at[r - 1], device_id=(tx, ty, tc), device_id_type=MESH).wait_recv()
        for cp in sends:
            cp.wait_send()
        acc = buf[0]
        for d in range(1, 8):
            acc = acc + buf[d]
        o_ref[...] = acc

    return pl.pallas_call(
        body, name="allreduce_small", in_specs=[pl.BlockSpec(memory_space=pltpu.VMEM)], out_specs=pl.BlockSpec(memory_space=pltpu.VMEM),
        out_shape=jax.ShapeDtypeStruct(v.shape, F32),
        scratch_shapes=[pltpu.VMEM((8, *v.shape), F32), pltpu.SemaphoreType.DMA((7,)), pltpu.SemaphoreType.DMA((7,))],
    )(v)


def _rows_view(a):
    return a.reshape(-1, a.shape[-1])


def _elementwise(name, fn, ins, n_out):
    rows, cols = ins[0].shape
    tm = min(rows, 256)
    spec = pl.BlockSpec((tm, cols), lambda i: (i, 0))

    def body(*refs):
        outs = fn(*[r[...] for r in refs[:len(ins)]])
        for o_ref, o in zip(refs[len(ins):], outs):
            o_ref[...] = o

    return pl.pallas_call(
        body, grid=(rows // tm,), name=name, in_specs=[spec] * len(ins), out_specs=[spec] * n_out,
        out_shape=[jax.ShapeDtypeStruct((rows, cols), F32)] * n_out, compiler_params=_cp("parallel"),
    )(*ins)


def _sum_parts(name, own, recv):
    def fn(o, r0, r1, r2):
        return (o + r0.astype(F32) + r1.astype(F32) + r2.astype(F32),)

    return _elementwise("sum_parts_" + name, fn, [own, recv[0], recv[1], recv[2]], 1)[0]


def _adamw(name, w, m, v, g_parts):
    def fn(w_, m_, v_, *gs):
        g = gs[0]
        for extra in gs[1:]:
            g = g + extra
        m_new = ADAM_B1 * m_ + (1.0 - ADAM_B1) * g
        v_new = ADAM_B2 * v_ + (1.0 - ADAM_B2) * (g * g)
        m_hat = m_new / (1.0 - ADAM_B1 ** ADAM_STEP)
        v_hat = v_new / (1.0 - ADAM_B2 ** ADAM_STEP)
        delta = -ADAM_LR * (m_hat / (jnp.sqrt(v_hat) + ADAM_EPS) + ADAM_WD * w_)
        return g, delta, m_new, v_new

    return _elementwise("adamw_" + name, fn, [w, m, v, *g_parts], 4)


MAT_AXIS = {"w_in": 1, "w_uq": 1, "w_ukv": 1, "w_out": 0}
SMALL = ("rel_bias_table", "norm_in", "sink_a", "norm_q_lat", "norm_kv_lat", "norm_out_a", "norm_out_b", "norm_final")
WEIGHTS = ("meta_tokens", "rel_bias_table", "norm_in", "w_in", "sink_a", "norm_q_lat", "w_uq", "norm_kv_lat", "w_ukv",
           "norm_out_a", "norm_out_b", "w_out", "norm_final")
SMALL_ROWS, SMALL_COLS = 8, 1024


def _pack_small(d, loss=None):
    flat = [d[n].reshape(-1) for n in SMALL]
    if loss is not None:
        flat.append(loss.reshape(1))
    used = sum(f.shape[0] for f in flat)
    flat.append(jnp.zeros((SMALL_ROWS * SMALL_COLS - used,), F32))
    return jnp.concatenate(flat).reshape(SMALL_ROWS, SMALL_COLS)


def _unpack_small(p, like):
    flat, out, off = p.reshape(-1), {}, 0
    for n in SMALL:
        size = int(np.prod(like[n].shape))
        out[n] = flat[off:off + size].reshape(like[n].shape)
        off += size
    return out, flat[off]


def _split4(a, axis):
    size = a.shape[axis] // 4
    return jnp.stack([lax.slice_in_dim(a, k * size, (k + 1) * size, axis=axis) for k in range(4)])


def _train_step(cfg, x, target, w, m, v):
    depth = w["w_in"].shape[0]
    mat_axes = [MAT_AXIS[n] for n in MATRICES]

    def layer_shards(i):
        return [w[n][i].astype(BF16) for n in MATRICES]

    def full_of(gathered, axes):
        return [jnp.concatenate([g[k] for k in range(4)], axis=ax) for g, ax in zip(gathered, axes)]

    first = _run_comm(_gather_comm(layer_shards(0) + [w["meta_tokens"]]), "gather_first")
    weights = {0: full_of(first[:4], mat_axes)}
    meta = full_of(first[4:], [1])[0]

    def fwd_comm(i):
        if i + 1 >= depth:
            return None

        def deliver(res):
            weights[i + 1] = full_of(res, mat_axes)

        return _gather_comm(layer_shards(i + 1)), deliver

    splits, received = {}, {}

    def parts_of(i, mats):
        splits[i] = [_split4(mats[i][n], MAT_AXIS[n]) for n in MATRICES]
        return [s.astype(BF16) for s in splits[i]]

    def bwd_comm(i, mats):
        if i + 1 >= depth:
            return None

        def deliver(res):
            received[i + 1] = res

        return _scatter_comm(parts_of(i + 1, mats)), deliver

    loss_local, grad_x, g, mats, g_meta = _local_grads(cfg, x, target, meta, w["rel_bias_table"], {n: w[n] for n in SMALL},
                                                      lambda i: weights[i], fwd_comm, bwd_comm)
    meta_split = _split4(g_meta, 1)
    last = _run_comm(_scatter_comm(parts_of(0, mats) + [meta_split.astype(BF16)]), "scatter_last")
    received[0] = last[:4]

    small_sum = _allreduce_small(_pack_small(g, loss_local))
    g_small, loss = _unpack_small(small_sum, {n: w[n] for n in SMALL})

    k_me = 2 * lax.axis_index("x") + lax.axis_index("y")

    def core_sum(name, split, recv):
        own = lax.dynamic_index_in_dim(split, k_me, 0, keepdims=False)
        return _sum_parts(name, _rows_view(own), recv.reshape(3, -1, recv.shape[-1]))

    partial = [core_sum("meta_tokens", meta_split, last[4])]
    for idx, n in enumerate(MATRICES):
        partial.append(jnp.concatenate([core_sum(f"{n}_{i}", splits[i][idx], received[i][idx]) for i in range(depth)], axis=0))
    sibling = _swap_sibling(partial)

    outs = {}
    for n, p_me, p_sib in zip(("meta_tokens", *MATRICES), partial, sibling):
        res = _adamw(n, _rows_view(w[n]), _rows_view(m[n]), _rows_view(v[n]), [p_me, p_sib])
        outs[n] = [r.reshape(w[n].shape) for r in res]
    res = _adamw("small", _pack_small(w), _pack_small(m), _pack_small(v), [_pack_small(g_small)])
    unpacked = [_unpack_small(r, {n: w[n] for n in SMALL})[0] for r in res]
    for n in SMALL:
        outs[n] = [u[n] for u in unpacked]

    result = [loss, grad_x]
    for field in range(4):
        result.extend(outs[n][field] for n in WEIGHTS)
    return tuple(result)


def kernel(x, meta_tokens, rel_bias_table, norm_in, w_in, sink_a, norm_q_lat, w_uq, norm_kv_lat, w_ukv, norm_out_a, norm_out_b, w_out, norm_final, loss_target, m_meta_tokens, m_rel_bias_table, m_norm_in, m_w_in, m_sink_a, m_norm_q_lat, m_w_uq, m_norm_kv_lat, m_w_ukv, m_norm_out_a, m_norm_out_b, m_w_out, m_norm_final, v_meta_tokens, v_rel_bias_table, v_norm_in, v_w_in, v_sink_a, v_norm_q_lat, v_w_uq, v_norm_kv_lat, v_w_ukv, v_norm_out_a, v_norm_out_b, v_w_out, v_norm_final):
    w = dict(zip(WEIGHTS, (meta_tokens, rel_bias_table, norm_in, w_in, sink_a, norm_q_lat, w_uq, norm_kv_lat, w_ukv, norm_out_a, norm_out_b, w_out, norm_final)))
    m = dict(zip(WEIGHTS, (m_meta_tokens, m_rel_bias_table, m_norm_in, m_w_in, m_sink_a, m_norm_q_lat, m_w_uq, m_norm_kv_lat, m_w_ukv, m_norm_out_a, m_norm_out_b, m_w_out, m_norm_final)))
    v = dict(zip(WEIGHTS, (v_meta_tokens, v_rel_bias_table, v_norm_in, v_w_in, v_sink_a, v_norm_q_lat, v_w_uq, v_norm_kv_lat, v_w_ukv, v_norm_out_a, v_norm_out_b, v_w_out, v_norm_final)))
    cfg = make_cfg(x.shape[0], x.shape[1])
    return _train_step(cfg, x, loss_target, w, m, v)
```

```python
import collections
import functools
import math

import jax
import jax.numpy as jnp
import numpy as np
from jax import lax
from jax.experimental import pallas as pl
from jax.experimental.pallas import tpu as pltpu

F32 = jnp.float32
BF16 = jnp.bfloat16

BLK = 128
N_META = 16
D_MODEL = 1024
A_HEADS, A_KV, A_DH = 8, 2, 64
B_HEADS, B_NOPE, B_ROPE, B_DV = 8, 64, 32, 64
Q_RANK, KV_RANK = 256, 128
N_BUCKETS, MAX_DIST = 32, 128
ROPE_THETA = 10000.0
EPS = 1e-6
IN_WIDTH = 2208
W_IN_P = 2304
NEG = -1e30
MASK_LANE = 96
Q_SCALE = (B_NOPE + B_ROPE) ** -0.5 * math.log2(math.e)
LN2 = math.log(2.0)
VMEM_LIMIT = 48 * 1024 * 1024

ADAM_LR, ADAM_B1, ADAM_B2, ADAM_EPS, ADAM_WD, ADAM_STEP = 0.001, 0.9, 0.999, 1e-08, 0.01, 10

Cfg = collections.namedtuple("Cfg", "B S NB NJ LP TP")


def make_cfg(batch, seq):
    nb = seq // BLK
    nj = nb + 1
    return Cfg(batch, seq, nb, nj, nj * BLK, batch * nj * BLK)


def _cp(*sem):
    return pltpu.CompilerParams(dimension_semantics=sem, vmem_limit_bytes=VMEM_LIMIT)


def _dot(a, b):
    return jnp.dot(a, b, preferred_element_type=F32)


def _dot_nt(a, b):
    return lax.dot_general(a, b, (((1,), (1,)), ((), ())), preferred_element_type=F32)


def _dot_tn(a, b):
    return lax.dot_general(a, b, (((0,), (0,)), ((), ())), preferred_element_type=F32)


def _rms(x, width=None):
    n = x.shape[-1] if width is None else width
    r = lax.rsqrt(jnp.sum(x * x, axis=-1, keepdims=True) * (1.0 / n) + EPS)
    return x * r, r


def _rms_bwd(xhat, r, t):
    n = xhat.shape[-1]
    return r * (t - xhat * (jnp.sum(t * xhat, axis=-1, keepdims=True) * (1.0 / n)))


def _sigmoid(x):
    return 1.0 / (1.0 + jnp.exp(-x))


def _lane(shape):
    return lax.broadcasted_iota(jnp.int32, shape, len(shape) - 1)


def _swap_rope(x):
    n = x.shape[-1]
    lane = _lane(x.shape) % BLK
    up = pltpu.roll(x, n - 16, axis=x.ndim - 1)
    dn = pltpu.roll(x, 16, axis=x.ndim - 1)
    return jnp.where((lane >= 64) & (lane < 80), up, jnp.where((lane >= 80) & (lane < 96), dn, 0.0))


A_ORDER = (0, 4, 1, 5, 2, 6, 3, 7)


def _jtype(j, nj):
    return 0 if j == 0 else 1 if j == 1 else 3 if j == nj - 1 else 2


def _window_structure(nj):
    def pos(blk, r):
        return np.where(blk == 0, r, N_META + (blk - 1) * BLK + r)

    def valid(blk, r):
        return np.where(blk == 0, r < N_META, True)

    r = np.arange(BLK)
    rels, viss = [], []
    for j in (0, 1, 2, nj - 1):
        qpos = pos(j, r)[:, None]
        rel_t, vis_t = [], []
        for s, kb in enumerate((0, j - 1, j, j + 1)):
            slot_ok = (s == 0) or (1 <= kb <= nj - 1)
            kbc = min(max(kb, 0), nj - 1)
            kpos = pos(kbc, r)[None, :]
            rel = kpos - qpos
            v = valid(kbc, r)[None, :] & np.ones((BLK, 1), bool)
            if s > 0:
                v = v & (np.abs(rel) <= BLK)
            rel_t.append(rel)
            vis_t.append(v & slot_ok)
        rels.append(np.concatenate(rel_t, axis=1))
        viss.append(np.concatenate(vis_t, axis=1))
    return np.stack(rels).astype(np.int32), np.stack(viss)


def _t5_bucket(rel):
    nb = N_BUCKETS // 2
    max_exact = nb // 2
    ret = jnp.where(rel > 0, nb, 0)
    n = jnp.abs(rel)
    nf = jnp.maximum(n, 1).astype(jnp.float32)
    large = max_exact + (jnp.log(nf / max_exact) / math.log(MAX_DIST / max_exact) * (nb - max_exact)).astype(jnp.int32)
    large = jnp.minimum(large, nb - 1)
    return ret + jnp.where(n < max_exact, n, large)


def _perm_heads64(a, axis):
    parts = [lax.slice_in_dim(a, h * 64, (h + 1) * 64, axis=axis) for h in A_ORDER]
    return jnp.concatenate(parts, axis=axis)


def _unperm_heads64(a, axis):
    inv = [A_ORDER.index(h) for h in range(8)]
    parts = [lax.slice_in_dim(a, p * 64, (p + 1) * 64, axis=axis) for p in inv]
    return jnp.concatenate(parts, axis=axis)


def _w_in_to_p(w):
    sl = lambda a, b: lax.slice_in_dim(w, a, b, axis=1)
    z = lambda n: jnp.zeros((w.shape[0], n), w.dtype)
    return jnp.concatenate([_perm_heads64(sl(0, 512), 1), sl(512, 768), _perm_heads64(sl(768, 1280), 1), sl(1696, 2208),
                            sl(1280, 1536), sl(1536, 1664), z(64), sl(1664, 1696), z(32)], axis=1)


def _w_in_from_p(g):
    sl = lambda a, b: lax.slice_in_dim(g, a, b, axis=1)
    return jnp.concatenate([_unperm_heads64(sl(0, 512), 1), sl(512, 768), _unperm_heads64(sl(768, 1280), 1),
                            sl(1792, 2048), sl(2048, 2176), sl(2240, 2272), sl(1280, 1792)], axis=1)


def _w_uq_to_p(w):
    z = jnp.zeros((w.shape[0], 32), w.dtype)
    return jnp.concatenate([p for h in range(8) for p in (lax.slice_in_dim(w, h * 96, (h + 1) * 96, axis=1), z)], axis=1)


def _w_uq_from_p(g):
    return jnp.concatenate([lax.slice_in_dim(g, h * 128, h * 128 + 96, axis=1) for h in range(8)], axis=1)


def _w_ukv_to_p(w):
    z = jnp.zeros((w.shape[0], 64), w.dtype)
    ks = [p for h in range(8) for p in (lax.slice_in_dim(w, h * 128, h * 128 + 64, axis=1), z)]
    vs = [lax.slice_in_dim(w, h * 128 + 64, (h + 1) * 128, axis=1) for h in range(8)]
    return jnp.concatenate(ks + vs, axis=1)


def _w_ukv_from_p(g):
    parts = []
    for h in range(8):
        parts.append(lax.slice_in_dim(g, h * 128, h * 128 + 64, axis=1))
        parts.append(lax.slice_in_dim(g, 1024 + h * 64, 1024 + (h + 1) * 64, axis=1))
    return jnp.concatenate(parts, axis=1)


def _w_out_to_p(w):
    return jnp.concatenate([_perm_heads64(lax.slice_in_dim(w, 0, 512, axis=0), 0), lax.slice_in_dim(w, 512, 1024, axis=0)], axis=0)


def _w_out_from_p(g):
    return jnp.concatenate([_unperm_heads64(lax.slice_in_dim(g, 0, 512, axis=0), 0), lax.slice_in_dim(g, 512, 1024, axis=0)], axis=0)


def _rope_tables(cfg):
    half = B_ROPE // 2
    length = N_META + cfg.S
    freqs = ROPE_THETA ** (-jnp.arange(half, dtype=jnp.float32) / half)
    ang = jnp.arange(length, dtype=jnp.float32)[:, None] * freqs[None, :]
    cos, sin = jnp.cos(ang), jnp.sin(ang)

    def rows(t):
        return jnp.concatenate([t[:N_META], jnp.zeros((BLK - N_META, t.shape[1]), t.dtype), t[N_META:]], axis=0)

    ones = jnp.ones((length, 64), F32)
    zer = jnp.zeros((length, 32), F32)
    c_tab = rows(jnp.concatenate([ones, cos, cos, zer], axis=1))
    s_tab = rows(jnp.concatenate([zer, zer, -sin, sin, zer], axis=1))
    return c_tab, s_tab


def _inproj_fwd(cfg, h, g, w_p, comm=None):
    tm = 256

    def body(h_ref, g_ref, w_ref, pa_ref, pf_ref):
        xh, _ = _rms(h_ref[...])
        u = (xh * g_ref[...]).astype(BF16)
        acc = _dot(u, w_ref[...])
        pa_ref[...] = acc[:, :768].astype(BF16)
        pf_ref[...] = acc[:, 768:]

    return _call_with_comm(
        body, comm, grid=(cfg.TP // tm,), name="inproj_fwd",
        in_specs=[pl.BlockSpec((tm, D_MODEL), lambda i: (i, 0)), pl.BlockSpec((1, D_MODEL), lambda i: (0, 0)),
                  pl.BlockSpec((D_MODEL, W_IN_P), lambda i: (0, 0))],
        out_specs=[pl.BlockSpec((tm, 768), lambda i: (i, 0)), pl.BlockSpec((tm, 1536), lambda i: (i, 0))],
        out_shape=[jax.ShapeDtypeStruct((cfg.TP, 768), BF16), jax.ShapeDtypeStruct((cfg.TP, 1536), F32)],
        args=(h, g, w_p))


def _lat_fwd(cfg, pf, gq, gkv, wq_p, wkv_p, c_tab, s_tab):
    nj = cfg.NJ

    def body(cq_ref, ckv_ref, kr_ref, gq_ref, gkv_ref, wq_ref, wkv_ref, c_ref, s_ref, q_ref, k_ref, v_ref, kt_ref, vt_ref):
        c1, s1 = c_ref[...], s_ref[...]
        c8, s8 = jnp.tile(c1, (1, 8)), jnp.tile(s1, (1, 8))
        mask_lane = _lane((BLK, 1024)) % BLK == MASK_LANE
        zero_row = (pl.program_id(1) == 0) & (lax.broadcasted_iota(jnp.int32, (BLK, 1024), 0) >= N_META)
        xq, _ = _rms(cq_ref[...])
        qp = _dot((xq * gq_ref[...]).astype(BF16), wq_ref[...])
        q_ref[...] = jnp.where(mask_lane, 1.0, (qp * c8 + _swap_rope(qp) * s8) * Q_SCALE).astype(BF16)
        xk, _ = _rms(ckv_ref[...])
        kvp = _dot((xk * gkv_ref[...]).astype(BF16), wkv_ref[...])
        kr = kr_ref[...]
        krr = kr * c1 + _swap_rope(kr) * s1
        k = jnp.where(mask_lane & zero_row, NEG, kvp[:, :1024] + jnp.tile(krr, (1, 8)))
        k_ref[...] = k.astype(BF16)
        v_ref[...] = kvp[:, 1024:].astype(BF16)
        kt_ref[...] = k.T.astype(BF16)
        vt_ref[...] = kvp[:, 1024:].T.astype(BF16)

    row = lambda b, j: b * nj + j
    return pl.pallas_call(
        body, grid=(cfg.B, nj), name="lat_fwd",
        in_specs=[pl.BlockSpec((BLK, 256), lambda b, j: (row(b, j), 4)), pl.BlockSpec((BLK, 128), lambda b, j: (row(b, j), 10)),
                  pl.BlockSpec((BLK, 128), lambda b, j: (row(b, j), 11)),
                  pl.BlockSpec((1, 256), lambda b, j: (0, 0)), pl.BlockSpec((1, 128), lambda b, j: (0, 0)),
                  pl.BlockSpec((256, 1024), lambda b, j: (0, 0)), pl.BlockSpec((128, 1536), lambda b, j: (0, 0)),
                  pl.BlockSpec((BLK, 128), lambda b, j: (j, 0)), pl.BlockSpec((BLK, 128), lambda b, j: (j, 0))],
        out_specs=[pl.BlockSpec((BLK, 1024), lambda b, j: (row(b, j), 0)), pl.BlockSpec((BLK, 1024), lambda b, j: (row(b, j), 0)),
                   pl.BlockSpec((BLK, 512), lambda b, j: (row(b, j), 0)),
                   pl.BlockSpec((1024, BLK), lambda b, j: (b, j)), pl.BlockSpec((512, BLK), lambda b, j: (b, j))],
        out_shape=[jax.ShapeDtypeStruct((cfg.TP, 1024), BF16), jax.ShapeDtypeStruct((cfg.TP, 1024), BF16),
                   jax.ShapeDtypeStruct((cfg.TP, 512), BF16),
                   jax.ShapeDtypeStruct((cfg.B * 1024, cfg.LP), BF16), jax.ShapeDtypeStruct((cfg.B * 512, cfg.LP), BF16)],
        compiler_params=_cp("parallel", "parallel"),
    )(pf, pf, pf, gq, gkv, wq_p, wkv_p, c_tab, s_tab)


def _gate_halves(ya, yb, ga, gb, goa, gob):
    xa, ra = _rms(ya)
    xb, rb = _rms(yb)
    sga, sgb = _sigmoid(ga), _sigmoid(gb)
    return xa, ra, xb, rb, sga, sgb, xa * goa * (ga * sga), xb * gob * (gb * sgb)


def _out_fwd(cfg, ya, yb, pf, goa, gob, wo_p, h):
    tm = 256

    def body(ya_ref, yb_ref, ga_ref, gb_ref, goa_ref, gob_ref, w_ref, h_ref, o_ref):
        *_, y_a, y_b = _gate_halves(ya_ref[...], yb_ref[...], ga_ref[...], gb_ref[...], goa_ref[...], gob_ref[...])
        y = jnp.concatenate([y_a, y_b], axis=1).astype(BF16)
        o_ref[...] = h_ref[...] + _dot(y, w_ref[...])

    return pl.pallas_call(
        body, grid=(cfg.TP // tm,), name="out_fwd",
        in_specs=[pl.BlockSpec((tm, 512), lambda i: (i, 0)), pl.BlockSpec((tm, 512), lambda i: (i, 0)),
                  pl.BlockSpec((tm, 512), lambda i: (i, 0)), pl.BlockSpec((tm, 512), lambda i: (i, 1)),
                  pl.BlockSpec((1, 512), lambda i: (0, 0)), pl.BlockSpec((1, 512), lambda i: (0, 0)),
                  pl.BlockSpec((D_MODEL, D_MODEL), lambda i: (0, 0)), pl.BlockSpec((tm, D_MODEL), lambda i: (i, 0))],
        out_specs=pl.BlockSpec((tm, D_MODEL), lambda i: (i, 0)),
        out_shape=jax.ShapeDtypeStruct((cfg.TP, D_MODEL), F32),
        compiler_params=_cp("parallel"),
    )(ya, yb, pf, pf, goa, gob, wo_p, h)


def _bias_build(table, bucket, maskadd):
    def body(tab_ref, bk_ref, ma_ref, o_ref):
        h = pl.program_id(1)
        bk = bk_ref[0]

        def step(b, acc):
            return jnp.where(bk == b, tab_ref[b, h], acc)

        o_ref[0, 0] = lax.fori_loop(0, N_BUCKETS, step, jnp.zeros(bk.shape, F32)) + ma_ref[0]

    return pl.pallas_call(
        body, grid=(4, A_HEADS), name="bias_build",
        in_specs=[pl.BlockSpec(memory_space=pltpu.SMEM), pl.BlockSpec((1, BLK, 512), lambda t, h: (t, 0, 0)),
                  pl.BlockSpec((1, BLK, 512), lambda t, h: (t, 0, 0))],
        out_specs=pl.BlockSpec((1, 1, BLK, 512), lambda t, h: (t, h, 0, 0)),
        out_shape=jax.ShapeDtypeStruct((4, A_HEADS, BLK, 512), F32),
        compiler_params=_cp("parallel", "parallel"),
    )(table, bucket, maskadd)


def _bias_grad(s_accs, bucket):
    depth = len(s_accs)

    def body(*refs):
        s_refs, bk_ref, o_ref, sum_ref, part_ref = refs[:depth], refs[depth], refs[depth + 1], refs[depth + 2], refs[depth + 3]
        t = pl.program_id(0)

        @pl.when(t == 0)
        def _():
            o_ref[...] = jnp.zeros_like(o_ref)

        total = s_refs[0][0]
        for extra in s_refs[1:]:
            total = total + extra[0]
        sum_ref[...] = total

        def step(b, carry):
            accs = [jnp.zeros((8, 512), F32) for _ in range(A_HEADS)]
            for g in range(BLK // 8):
                rows = pl.ds(g * 8, 8)
                hit = bk_ref[0, rows, :] == b
                for h in range(A_HEADS):
                    accs[h] = accs[h] + jnp.where(hit, sum_ref[h, rows, :], 0.0)
            rows8 = jnp.concatenate([jnp.sum(a, axis=0, keepdims=True) for a in accs], axis=0)
            part_ref[pl.ds(pl.multiple_of(b * A_HEADS, 8), A_HEADS), :] = rows8
            return carry

        lax.fori_loop(0, N_BUCKETS, step, 0)
        o_ref[...] += jnp.broadcast_to(jnp.sum(part_ref[...], axis=1, keepdims=True), o_ref.shape)

    s_spec = pl.BlockSpec((1, A_HEADS, BLK, 512), lambda t: (t, 0, 0, 0))
    return pl.pallas_call(
        body, grid=(4,), name="bias_grad",
        in_specs=[s_spec] * depth + [pl.BlockSpec((1, BLK, 512), lambda t: (t, 0, 0))],
        out_specs=pl.BlockSpec((N_BUCKETS * A_HEADS, 128), lambda t: (0, 0)),
        out_shape=jax.ShapeDtypeStruct((N_BUCKETS * A_HEADS, 128), F32),
        scratch_shapes=[pltpu.VMEM((A_HEADS, BLK, 512), F32), pltpu.VMEM((N_BUCKETS * A_HEADS, 512), F32)],
        compiler_params=_cp("arbitrary"),
    )(*s_accs, bucket)


def _win_specs(cfg):
    nj = cfg.NJ
    row = lambda b, j: b * nj + j
    jt = lambda j: jnp.where(j == 0, 0, jnp.where(j == 1, 1, jnp.where(j == nj - 1, 3, 2)))
    slot_rows = [lambda b, j: row(b, 0), lambda b, j: row(b, jnp.maximum(j - 1, 0)), lambda b, j: row(b, j),
                 lambda b, j: row(b, jnp.minimum(j + 1, nj - 1))]
    k_specs = [pl.BlockSpec((BLK, 128), functools.partial(lambda b, j, f: (f(b, j), 4), f=f)) for f in slot_rows]
    v_specs = [pl.BlockSpec((BLK, 128), functools.partial(lambda b, j, f: (f(b, j), 5), f=f)) for f in slot_rows]
    q_spec = pl.BlockSpec((BLK, 512), lambda b, j: (row(b, j), 0))
    bias_spec = pl.BlockSpec((1, A_HEADS, BLK, 512), lambda b, j: (jt(j), 0, 0, 0))
    return row, jt, q_spec, k_specs, v_specs, bias_spec


def _stack4(ref):
    return jnp.concatenate([ref[:, c * 128:(c + 1) * 128] for c in range(4)], axis=0)


def _win_keys(k_refs, v_refs):
    k4 = jnp.concatenate([r[...] for r in k_refs], axis=0)
    v4 = jnp.concatenate([r[...] for r in v_refs], axis=0)
    lane_k = _lane(k4.shape)
    return (jnp.where(lane_k < 64, k4, jnp.zeros_like(k4)), jnp.where(lane_k >= 64, k4, jnp.zeros_like(k4))), v4


def _sink_col(sink_ref, hf):
    rowi = lax.broadcasted_iota(jnp.int32, (4 * BLK, 1), 0)
    col = jnp.full((4 * BLK, 1), sink_ref[4 * hf + 3], F32)
    for c in (2, 1, 0):
        col = jnp.where(rowi < (c + 1) * BLK, sink_ref[4 * hf + c], col)
    return col


def _win_fwd(cfg, pa, bias, sink):
    row, jt, q_spec, k_specs, v_specs, bias_spec = _win_specs(cfg)
    scale = A_DH ** -0.5

    def body(sink_ref, q_ref, k0, k1, k2, k3, v0, v1, v2, v3, b_ref, o_ref, lse_ref):
        kk, v4 = _win_keys((k0, k1, k2, k3), (v0, v1, v2, v3))
        qs = _stack4(q_ref)
        lane_o = _lane((4 * BLK, 128))
        outs, lses = [], []
        for hf in range(2):
            s = _dot_nt(qs, kk[hf]) * scale + b_ref[0, 4 * hf:4 * hf + 4].reshape(4 * BLK, 512)
            sink_col = _sink_col(sink_ref, hf)
            m = jnp.maximum(jnp.max(s, axis=1, keepdims=True), sink_col)
            e = jnp.exp(s - m)
            den = jnp.sum(e, axis=1, keepdims=True) + jnp.exp(sink_col - m)
            outs.append(_dot(e.astype(BF16), v4) / den)
            lses.append(m + jnp.log(den))
        o = jnp.where(lane_o < 64, outs[0], outs[1])
        for c in range(4):
            o_ref[:, c * 128:(c + 1) * 128] = o[c * BLK:(c + 1) * BLK]
        lse_ref[...] = jnp.where(lane_o == 0, lses[0], jnp.where(lane_o == 1, lses[1], 0.0))

    return pl.pallas_call(
        body, grid=(cfg.B, cfg.NJ), name="win_fwd",
        in_specs=[pl.BlockSpec(memory_space=pltpu.SMEM), q_spec, *k_specs, *v_specs, bias_spec],
        out_specs=[pl.BlockSpec((BLK, 512), lambda b, j: (row(b, j), 0)), pl.BlockSpec((4 * BLK, 128), lambda b, j: (row(b, j), 0))],
        out_shape=[jax.ShapeDtypeStruct((cfg.TP, 512), F32), jax.ShapeDtypeStruct((4 * cfg.TP, 128), F32)],
        compiler_params=_cp("parallel", "parallel"),
    )(sink, pa, *([pa] * 8), bias)


def _win_bwd(cfg, pa, bias, sink, dya, ya, lse):
    row, jt, q_spec, k_specs, v_specs, bias_spec = _win_specs(cfg)
    nj = cfg.NJ
    scale = A_DH ** -0.5

    def body(sink_ref, q_ref, k0, k1, k2, k3, v0, v1, v2, v3, b_ref, dy_ref, y_ref, lse_ref,
             dq_ref, dkp_ref, dvp_ref, dkm_ref, dvm_ref, s_ref, dsink_ref):
        j = pl.program_id(1)
        kind = jt(j)

        @pl.when((pl.program_id(0) == 0) & (j == 0))
        def _():
            s_ref[...] = jnp.zeros_like(s_ref)

        kk, v4 = _win_keys((k0, k1, k2, k3), (v0, v1, v2, v3))
        qs, dys, ys = _stack4(q_ref), _stack4(dy_ref), _stack4(y_ref)
        lane_o = _lane((4 * BLK, 128))
        half = (lane_o < 64, lane_o >= 64)
        lse_blk = lse_ref[...]
        dq = jnp.zeros((4 * BLK, 128), F32)
        dk4 = jnp.zeros((512, 128), F32)
        dv4 = jnp.zeros((512, 128), F32)
        dsink = jnp.zeros((8, 128), F32)
        lane_s = _lane((8, 128))
        row_s = lax.broadcasted_iota(jnp.int32, (8, 128), 0)
        for hf in range(2):
            lse_h = jnp.sum(jnp.where(lane_o == hf, lse_blk, 0.0), axis=1, keepdims=True)
            s = _dot_nt(qs, kk[hf]) * scale + b_ref[0, 4 * hf:4 * hf + 4].reshape(4 * BLK, 512)
            p = jnp.exp(s - lse_h)
            do_h = jnp.where(half[hf], dys, 0.0)
            delta = jnp.sum(do_h * ys, axis=1, keepdims=True)
            do_b = do_h.astype(BF16)
            ds = p * (_dot_nt(do_b, v4) - delta)
            s_ref[kind, 4 * hf:4 * hf + 4] += ds.reshape(4, BLK, 512)
            sink_grad = jnp.exp(_sink_col(sink_ref, hf) - lse_h) * delta
            for c in range(4):
                tot = -jnp.sum(sink_grad[c * BLK:(c + 1) * BLK])
                dsink = jnp.where((row_s == 0) & (lane_s == 4 * hf + c), tot, dsink)
            dsb = (ds * scale).astype(BF16)
            dq = dq + _dot(dsb, kk[hf])
            dk4 = dk4 + _dot_tn(dsb, jnp.where(half[hf], qs, jnp.zeros_like(qs)))
            dv4 = dv4 + _dot_tn(p.astype(BF16), do_b)
        for c in range(4):
            dq_ref[:, c * 128:(c + 1) * 128] = dq[c * BLK:(c + 1) * BLK].astype(BF16)
        dkp_ref[0] = dk4
        dvp_ref[0] = dv4

        @pl.when(j == 0)
        def _():
            dkm_ref[...] = dk4[:BLK]
            dvm_ref[...] = dv4[:BLK]

        @pl.when(j > 0)
        def _():
            dkm_ref[...] += dk4[:BLK]
            dvm_ref[...] += dv4[:BLK]

        @pl.when((pl.program_id(0) == 0) & (j == 0))
        def _():
            dsink_ref[...] = dsink

        @pl.when((pl.program_id(0) > 0) | (j > 0))
        def _():
            dsink_ref[...] += dsink

    blk_row = pl.BlockSpec((BLK, 512), lambda b, j: (row(b, j), 0))
    return pl.pallas_call(
        body, grid=(cfg.B, nj), name="win_bwd",
        in_specs=[pl.BlockSpec(memory_space=pltpu.SMEM), q_spec, *k_specs, *v_specs, bias_spec, blk_row, blk_row,
                  pl.BlockSpec((4 * BLK, 128), lambda b, j: (row(b, j), 0))],
        out_specs=[blk_row,
                   pl.BlockSpec((1, 512, 128), lambda b, j: (row(b, j), 0, 0)), pl.BlockSpec((1, 512, 128), lambda b, j: (row(b, j), 0, 0)),
                   pl.BlockSpec((BLK, 128), lambda b, j: (b, 0)), pl.BlockSpec((BLK, 128), lambda b, j: (b, 0)),
                   pl.BlockSpec((4, A_HEADS, BLK, 512), lambda b, j: (0, 0, 0, 0)),
                   pl.BlockSpec((8, 128), lambda b, j: (0, 0))],
        out_shape=[jax.ShapeDtypeStruct((cfg.TP, 512), BF16),
                   jax.ShapeDtypeStruct((cfg.B * nj, 512, 128), F32), jax.ShapeDtypeStruct((cfg.B * nj, 512, 128), F32),
                   jax.ShapeDtypeStruct((cfg.B * BLK, 128), F32), jax.ShapeDtypeStruct((cfg.B * BLK, 128), F32),
                   jax.ShapeDtypeStruct((4, A_HEADS, BLK, 512), F32),
                   jax.ShapeDtypeStruct((8, 128), F32)],
        compiler_params=_cp("arbitrary", "arbitrary"),
    )(sink, pa, *([pa] * 8), bias, dya, ya, lse)


def _win_dkv_combine(cfg, dkp, dvp, dkm, dvm):
    nj = cfg.NJ

    def body(kp, vp, km, vm, o_ref):
        o_ref[:BLK, :128] = km[...].astype(BF16)
        o_ref[:BLK, 128:] = vm[...].astype(BF16)
        for kb in range(1, nj):
            for col, part in ((0, kp), (128, vp)):
                tot = part[kb, 2 * BLK:3 * BLK] + part[kb - 1, 3 * BLK:4 * BLK]
                if kb + 1 < nj:
                    tot = tot + part[kb + 1, BLK:2 * BLK]
                o_ref[kb * BLK:(kb + 1) * BLK, col:col + 128] = tot.astype(BF16)

    return pl.pallas_call(
        body, grid=(cfg.B,), name="win_dkv_combine",
        in_specs=[pl.BlockSpec((nj, 512, 128), lambda b: (b, 0, 0)), pl.BlockSpec((nj, 512, 128), lambda b: (b, 0, 0)),
                  pl.BlockSpec((BLK, 128), lambda b: (b, 0)), pl.BlockSpec((BLK, 128), lambda b: (b, 0))],
        out_specs=pl.BlockSpec((cfg.LP, 256), lambda b: (b, 0)),
        out_shape=jax.ShapeDtypeStruct((cfg.TP, 256), BF16),
        compiler_params=_cp("parallel"),
    )(dkp, dvp, dkm, dvm)


def _pair_blockdiag(q):
    lane = _lane(q.shape)
    return jnp.concatenate([jnp.where(lane < 128, q, jnp.zeros_like(q)), jnp.where(lane >= 128, q, jnp.zeros_like(q))], axis=0)


def _mla_fwd(cfg, q, kt, v, comm=None):
    nj, lp = cfg.NJ, cfg.LP

    def body(q_ref, kt_ref, v_ref, o_ref, lse_ref):
        lane_o = _lane((BLK, 128))
        s = _dot(_pair_blockdiag(q_ref[...]), kt_ref[...])
        m = jnp.max(s, axis=1, keepdims=True)
        e = jnp.exp2(s - m)
        den = jnp.sum(e, axis=1, keepdims=True)
        pv = _dot(e.astype(BF16), v_ref[...]) / den
        o_ref[...] = jnp.where(lane_o < 64, pv[:BLK], pv[BLK:])
        lse_ref[0] = jnp.broadcast_to(m + jnp.log2(den), (2 * BLK, 128))

    return _call_with_comm(
        body, comm, grid=(cfg.B, 4, nj), name="mla_fwd",
        in_specs=[pl.BlockSpec((BLK, 256), lambda b, p, i: (b * nj + i, p)), pl.BlockSpec((256, lp), lambda b, p, i: (b * 4 + p, 0)),
                  pl.BlockSpec((lp, 128), lambda b, p, i: (b, p))],
        out_specs=[pl.BlockSpec((BLK, 128), lambda b, p, i: (b * nj + i, p)),
                   pl.BlockSpec((1, 2 * BLK, 128), lambda b, p, i: (p, b * nj + i, 0))],
        out_shape=[jax.ShapeDtypeStruct((cfg.TP, 512), F32), jax.ShapeDtypeStruct((4, 2 * cfg.TP, 128), F32)],
        args=(q, kt, v))


def _mla_bwd(cfg, q, k, kt, vt, dyb, yb, lse, comm=None):
    nj, lp = cfg.NJ, cfg.LP

    def body(q_ref, k_ref, kt_ref, vt_ref, dy_ref, y_ref, lse_ref, dq_ref, dk_ref, dv_ref):
        i = pl.program_id(2)

        @pl.when(i == 0)
        def _():
            dk_ref[...] = jnp.zeros_like(dk_ref)
            dv_ref[...] = jnp.zeros_like(dv_ref)

        lane_o = _lane((BLK, 128))
        qbd = _pair_blockdiag(q_ref[...])
        dy, y = dy_ref[...], y_ref[...]
        do_s = jnp.concatenate([jnp.where(lane_o < 64, dy, 0.0), jnp.where(lane_o >= 64, dy, 0.0)], axis=0)
        delta = jnp.sum(do_s * jnp.concatenate([y, y], axis=0), axis=1, keepdims=True)
        do_b = do_s.astype(BF16)
        p = jnp.exp2(_dot(qbd, kt_ref[...]) - lse_ref[0][:, :1])
        ds = p * (_dot(do_b, vt_ref[...]) - delta)
        dsb = (ds * LN2).astype(BF16)
        dq2 = _dot(dsb, k_ref[...])
        dq_ref[...] = jnp.where(_lane((BLK, 256)) < 128, dq2[:BLK], dq2[BLK:]) * Q_SCALE
        dk_ref[...] += _dot_tn(dsb, qbd)
        dv_ref[...] += _dot_tn(p.astype(BF16), do_b)

    return _call_with_comm(
        body, comm, grid=(cfg.B, 4, nj), name="mla_bwd",
        in_specs=[pl.BlockSpec((BLK, 256), lambda b, p, i: (b * nj + i, p)), pl.BlockSpec((lp, 256), lambda b, p, i: (b, p)),
                  pl.BlockSpec((256, lp), lambda b, p, i: (b * 4 + p, 0)), pl.BlockSpec((128, lp), lambda b, p, i: (b * 4 + p, 0)),
                  pl.BlockSpec((BLK, 128), lambda b, p, i: (b * nj + i, p)), pl.BlockSpec((BLK, 128), lambda b, p, i: (b * nj + i, p)),
                  pl.BlockSpec((1, 2 * BLK, 128), lambda b, p, i: (p, b * nj + i, 0))],
        out_specs=[pl.BlockSpec((BLK, 256), lambda b, p, i: (b * nj + i, p)), pl.BlockSpec((lp, 256), lambda b, p, i: (b, p)),
                   pl.BlockSpec((lp, 128), lambda b, p, i: (b, p))],
        out_shape=[jax.ShapeDtypeStruct((cfg.TP, 1024), F32), jax.ShapeDtypeStruct((cfg.TP, 1024), F32),
                   jax.ShapeDtypeStruct((cfg.TP, 512), F32)],
        args=(q, k, kt, vt, dyb, yb, lse))


def _loss_bwd(cfg, h, target, gf):
    nj, nb = cfg.NJ, cfg.NB

    def body(h_ref, t_ref, g_ref, dh_ref, loss_ref, dg_ref):
        b, j = pl.program_id(0), pl.program_id(1)

        @pl.when((b == 0) & (j == 0))
        def _():
            loss_ref[...] = jnp.zeros_like(loss_ref)
            dg_ref[...] = jnp.zeros_like(dg_ref)

        @pl.when(j == 0)
        def _():
            dh_ref[...] = jnp.zeros_like(dh_ref)

        @pl.when(j > 0)
        def _():
            g = g_ref[...]
            xh, r = _rms(h_ref[...])
            err = xh * g - t_ref[...]
            loss_ref[...] += jnp.where((lax.broadcasted_iota(jnp.int32, (8, 128), 0) == 0) & (_lane((8, 128)) == 0),
                                       (0.5 / D_MODEL) * jnp.sum(err * err), 0.0)
            dy = err * (1.0 / D_MODEL)
            dg_ref[...] += jnp.sum(dy * xh, axis=0, keepdims=True)
            dh_ref[...] = _rms_bwd(xh, r, dy * g)

    return pl.pallas_call(
        body, grid=(cfg.B, nj), name="loss_bwd",
        in_specs=[pl.BlockSpec((BLK, D_MODEL), lambda b, j: (b * nj + j, 0)),
                  pl.BlockSpec((BLK, D_MODEL), lambda b, j: (b * nb + jnp.maximum(j - 1, 0), 0)),
                  pl.BlockSpec((1, D_MODEL), lambda b, j: (0, 0))],
        out_specs=[pl.BlockSpec((BLK, D_MODEL), lambda b, j: (b * nj + j, 0)), pl.BlockSpec((8, 128), lambda b, j: (0, 0)),
                   pl.BlockSpec((1, D_MODEL), lambda b, j: (0, 0))],
        out_shape=[jax.ShapeDtypeStruct((cfg.TP, D_MODEL), F32), jax.ShapeDtypeStruct((8, 128), F32),
                   jax.ShapeDtypeStruct((1, D_MODEL), F32)],
        compiler_params=_cp("arbitrary", "arbitrary"),
    )(h, target, gf)


def _out_bwd(cfg, dh, ya, yb, pf, goa, gob, wo_p):
    tm = 256

    def body(dh_ref, ya_ref, yb_ref, ga_ref, gb_ref, goa_ref, gob_ref, w_ref,
             dya_ref, dyb_ref, dg_ref, dw_ref, dgoa_ref, dgob_ref):
        @pl.when(pl.program_id(0) == 0)
        def _():
            dw_ref[...] = jnp.zeros_like(dw_ref)
            dgoa_ref[...] = jnp.zeros_like(dgoa_ref)
            dgob_ref[...] = jnp.zeros_like(dgob_ref)

        ga, gb, goa, gob = ga_ref[...], gb_ref[...], goa_ref[...], gob_ref[...]
        xa, ra, xb, rb, sga, sgb, y_a, y_b = _gate_halves(ya_ref[...], yb_ref[...], ga, gb, goa, gob)
        dhb = dh_ref[...].astype(BF16)
        dw_ref[...] += _dot_tn(jnp.concatenate([y_a, y_b], axis=1).astype(BF16), dhb)
        dy = _dot_nt(dhb, w_ref[...])
        for (dyh, x, r, g, sg, go, dy_out, dgo_ref, col) in (
                (dy[:, :512], xa, ra, ga, sga, goa, dya_ref, dgoa_ref, 0), (dy[:, 512:], xb, rb, gb, sgb, gob, dyb_ref, dgob_ref, 512)):
            dn = dyh * (g * sg)
            dg_ref[:, col:col + 512] = (dyh * (x * go) * (sg * (1.0 + g * (1.0 - sg)))).astype(BF16)
            dgo_ref[...] += jnp.sum(dn * x, axis=0, keepdims=True)
            dy_out[...] = _rms_bwd(x, r, dn * go)

    half = lambda c: pl.BlockSpec((tm, 512), lambda i: (i, c))
    vec = pl.BlockSpec((1, 512), lambda i: (0, 0))
    return pl.pallas_call(
        body, grid=(cfg.TP // tm,), name="out_bwd",
        in_specs=[pl.BlockSpec((tm, D_MODEL), lambda i: (i, 0)), half(0), half(0), half(0), half(1), vec, vec,
                  pl.BlockSpec((D_MODEL, D_MODEL), lambda i: (0, 0))],
        out_specs=[half(0), half(0), pl.BlockSpec((tm, D_MODEL), lambda i: (i, 0)),
                   pl.BlockSpec((D_MODEL, D_MODEL), lambda i: (0, 0)), vec, vec],
        out_shape=[jax.ShapeDtypeStruct((cfg.TP, 512), F32), jax.ShapeDtypeStruct((cfg.TP, 512), F32),
                   jax.ShapeDtypeStruct((cfg.TP, D_MODEL), BF16), jax.ShapeDtypeStruct((D_MODEL, D_MODEL), F32),
                   jax.ShapeDtypeStruct((1, 512), F32), jax.ShapeDtypeStruct((1, 512), F32)],
        compiler_params=_cp("arbitrary"),
    )(dh, ya, yb, pf, pf, goa, gob, wo_p)


def _lat_bwd(cfg, dq, dk, dv, pf, gq, gkv, wq_p, wkv_p, c_tab, s_tab):
    nj = cfg.NJ

    def body(dq_ref, dk_ref, dv_ref, cq_ref, ckv_ref, gq_ref, gkv_ref, wq_ref, wkv_ref, c_ref, s_ref,
             dl_ref, dwq_ref, dwkv_ref, dgq_ref, dgkv_ref):
        @pl.when((pl.program_id(0) == 0) & (pl.program_id(1) == 0))
        def _():
            dwq_ref[...] = jnp.zeros_like(dwq_ref)
            dwkv_ref[...] = jnp.zeros_like(dwkv_ref)
            dgq_ref[...] = jnp.zeros_like(dgq_ref)
            dgkv_ref[...] = jnp.zeros_like(dgkv_ref)

        c1, s1 = c_ref[...], s_ref[...]
        c8, s8 = jnp.tile(c1, (1, 8)), jnp.tile(s1, (1, 8))
        dq_r = dq_ref[...]
        dqp = (dq_r * c8 + _swap_rope(dq_r * s8)).astype(BF16)
        gq = gq_ref[...]
        xq, rq = _rms(cq_ref[...])
        dwq_ref[...] += _dot_tn((xq * gq).astype(BF16), dqp)
        dn = _dot_nt(dqp, wq_ref[...])
        dgq_ref[...] += jnp.sum(dn * xq, axis=0, keepdims=True)
        dl_ref[:, :256] = _rms_bwd(xq, rq, dn * gq).astype(BF16)

        dk_r = dk_ref[...]
        dkr = dk_r[:, :128]
        for hd in range(1, 8):
            dkr = dkr + dk_r[:, hd * 128:(hd + 1) * 128]
        lane1 = _lane(dkr.shape)
        dkr = jnp.where((lane1 >= 64) & (lane1 < 96), dkr, 0.0)
        dl_ref[:, 384:] = (dkr * c1 + _swap_rope(dkr * s1)).astype(BF16)
        dkv = jnp.concatenate([dk_r, dv_ref[...]], axis=1).astype(BF16)
        gkv = gkv_ref[...]
        xk, rk = _rms(ckv_ref[...])
        dwkv_ref[...] += _dot_tn((xk * gkv).astype(BF16), dkv)
        dn2 = _dot_nt(dkv, wkv_ref[...])
        dgkv_ref[...] += jnp.sum(dn2 * xk, axis=0, keepdims=True)
        dl_ref[:, 256:384] = _rms_bwd(xk, rk, dn2 * gkv).astype(BF16)

    row = lambda b, j: b * nj + j
    const = lambda shape: pl.BlockSpec(shape, lambda b, j: (0, 0))
    return pl.pallas_call(
        body, grid=(cfg.B, nj), name="lat_bwd",
        in_specs=[pl.BlockSpec((BLK, 1024), lambda b, j: (row(b, j), 0)), pl.BlockSpec((BLK, 1024), lambda b, j: (row(b, j), 0)),
                  pl.BlockSpec((BLK, 512), lambda b, j: (row(b, j), 0)),
                  pl.BlockSpec((BLK, 256), lambda b, j: (row(b, j), 4)), pl.BlockSpec((BLK, 128), lambda b, j: (row(b, j), 10)),
                  const((1, 256)), const((1, 128)), const((256, 1024)), const((128, 1536)),
                  pl.BlockSpec((BLK, 128), lambda b, j: (j, 0)), pl.BlockSpec((BLK, 128), lambda b, j: (j, 0))],
        out_specs=[pl.BlockSpec((BLK, 512), lambda b, j: (row(b, j), 0)), const((256, 1024)), const((128, 1536)),
                   const((1, 256)), const((1, 128))],
        out_shape=[jax.ShapeDtypeStruct((cfg.TP, 512), BF16), jax.ShapeDtypeStruct((256, 1024), F32),
                   jax.ShapeDtypeStruct((128, 1536), F32), jax.ShapeDtypeStruct((1, 256), F32), jax.ShapeDtypeStruct((1, 128), F32)],
        compiler_params=_cp("arbitrary", "arbitrary"),
    )(dq, dk, dv, pf, pf, gq, gkv, wq_p, wkv_p, c_tab, s_tab)


def _inproj_bwd(cfg, h, g, w_p, dqa, dkva, dgate, dlat, dh, comm=None):
    tm = 256

    def body(h_ref, g_ref, w_ref, dqa_ref, dkva_ref, dg_ref, dl_ref, dh_ref, o_ref, dw_ref, dgn_ref):
        @pl.when(pl.program_id(0) == 0)
        def _():
            dw_ref[...] = jnp.zeros_like(dw_ref)
            dgn_ref[...] = jnp.zeros_like(dgn_ref)

        g = g_ref[...]
        xh, r = _rms(h_ref[...])
        dproj = jnp.concatenate([dqa_ref[...], dkva_ref[...], dg_ref[...], dl_ref[...]], axis=1)
        dw_ref[...] += _dot_tn((xh * g).astype(BF16), dproj)
        du = _dot_nt(dproj, w_ref[...])
        dgn_ref[...] += jnp.sum(du * xh, axis=0, keepdims=True)
        o_ref[...] = dh_ref[...] + _rms_bwd(xh, r, du * g)

    rows = lambda w: pl.BlockSpec((tm, w), lambda i: (i, 0))
    return _call_with_comm(
        body, comm, grid=(cfg.TP // tm,), name="inproj_bwd",
        in_specs=[rows(D_MODEL), pl.BlockSpec((1, D_MODEL), lambda i: (0, 0)), pl.BlockSpec((D_MODEL, W_IN_P), lambda i: (0, 0)),
                  rows(512), rows(256), rows(1024), rows(512), rows(D_MODEL)],
        out_specs=[rows(D_MODEL), pl.BlockSpec((D_MODEL, W_IN_P), lambda i: (0, 0)), pl.BlockSpec((1, D_MODEL), lambda i: (0, 0))],
        out_shape=[jax.ShapeDtypeStruct((cfg.TP, D_MODEL), F32), jax.ShapeDtypeStruct((D_MODEL, W_IN_P), F32),
                   jax.ShapeDtypeStruct((1, D_MODEL), F32)],
        args=(h, g, w_p, dqa, dkva, dgate, dlat, dh))


def _meta_grad(cfg, dh):
    def body(d_ref, o_ref):
        @pl.when(pl.program_id(0) == 0)
        def _():
            o_ref[...] = d_ref[...]

        @pl.when(pl.program_id(0) > 0)
        def _():
            o_ref[...] += d_ref[...]

    return pl.pallas_call(
        body, grid=(cfg.B,), name="meta_grad",
        in_specs=[pl.BlockSpec((BLK, D_MODEL), lambda b: (b * cfg.NJ, 0))],
        out_specs=pl.BlockSpec((BLK, D_MODEL), lambda b: (0, 0)),
        out_shape=jax.ShapeDtypeStruct((BLK, D_MODEL), F32),
        compiler_params=_cp("arbitrary"),
    )(dh)


MATRICES = ("w_in", "w_uq", "w_ukv", "w_out")


def _local_grads(cfg, x, target, meta, table, small, weight_of, rider=None):
    def ride(stage, i, mats):
        hook = rider(stage, i, mats) if rider else None
        return hook if hook else (None, lambda res: None)

    depth = small["norm_in"].shape[0]
    rel, vis = _window_structure(cfg.NJ)
    bucket = _t5_bucket(jnp.asarray(rel))
    maskadd = jnp.asarray(np.where(vis, 0.0, NEG).astype(np.float32))
    c_tab, s_tab = _rope_tables(cfg)
    bias = _bias_build(table, bucket, maskadd)

    meta_blk = jnp.concatenate([meta, jnp.zeros((BLK - N_META, D_MODEL), F32)], axis=0)
    h = jnp.concatenate([jnp.broadcast_to(meta_blk[None], (cfg.B, BLK, D_MODEL)), x], axis=1).reshape(cfg.TP, D_MODEL)

    wp, saved = [], []
    for i in range(depth):
        w = dict(w_in=_w_in_to_p(weight_of(i, "w_in")),
                 g_in=small["norm_in"][i][None], gq=small["norm_q_lat"][i][None], gkv=small["norm_kv_lat"][i][None],
                 goa=_perm_heads64(small["norm_out_a"][i], 0)[None], gob=small["norm_out_b"][i][None], sink=small["sink_a"][i])
        wp.append(w)
        comm, deliver = ride("inproj_fwd", i, {})
        pa, pf, *travelled = _inproj_fwd(cfg, h, w["g_in"], w["w_in"], comm)
        deliver(travelled)
        w.update(w_uq=_w_uq_to_p(weight_of(i, "w_uq")), w_ukv=_w_ukv_to_p(weight_of(i, "w_ukv")), w_out=_w_out_to_p(weight_of(i, "w_out")))
        q, k, v, kt, vt = _lat_fwd(cfg, pf, w["gq"], w["gkv"], w["w_uq"], w["w_ukv"], c_tab, s_tab)
        ya, lse_a = _win_fwd(cfg, pa, bias, w["sink"])
        comm, deliver = ride("mla_fwd", i, {})
        yb, lse_b, *travelled = _mla_fwd(cfg, q, kt, v, comm)
        deliver(travelled)
        h_next = _out_fwd(cfg, ya, yb, pf, w["goa"], w["gob"], w["w_out"], h)
        saved.append(dict(h=h, pa=pa, pf=pf, q=q, k=k, kt=kt, vt=vt, ya=ya, lse_a=lse_a, yb=yb, lse_b=lse_b))
        h = h_next

    dh, loss_tile, d_norm_final = _loss_bwd(cfg, h, target.reshape(cfg.B * cfg.S, D_MODEL), small["norm_final"][None])

    grads = {k_: [] for k_ in ("norm_in", "sink_a", "norm_q_lat", "norm_kv_lat", "norm_out_a", "norm_out_b")}
    mats, s_accs = {}, []
    for i in reversed(range(depth)):
        w, sv = wp[i], saved[i]
        dya, dyb, dgate, dwo, dgoa, dgob = _out_bwd(cfg, dh, sv["ya"], sv["yb"], sv["pf"], w["goa"], w["gob"], w["w_out"])
        dqa, dkp, dvp, dkm, dvm, s_acc, dsink = _win_bwd(cfg, sv["pa"], bias, w["sink"], dya, sv["ya"], sv["lse_a"])
        dkva = _win_dkv_combine(cfg, dkp, dvp, dkm, dvm)
        comm, deliver = ride("mla_bwd", i, mats)
        dq, dk, dv, *travelled = _mla_bwd(cfg, sv["q"], sv["k"], sv["kt"], sv["vt"], dyb, sv["yb"], sv["lse_b"], comm)
        deliver(travelled)
        dlat, dwq, dwkv, dgq, dgkv = _lat_bwd(cfg, dq, dk, dv, sv["pf"], w["gq"], w["gkv"], w["w_uq"], w["w_ukv"], c_tab, s_tab)
        mats[i] = dict(w_uq=_w_uq_from_p(dwq), w_ukv=_w_ukv_from_p(dwkv), w_out=_w_out_from_p(dwo))
        comm, deliver = ride("inproj_bwd", i, mats)
        dh, dwin, dgin, *travelled = _inproj_bwd(cfg, sv["h"], w["g_in"], w["w_in"], dqa, dkva, dgate, dlat, dh, comm)
        deliver(travelled)
        s_accs.append(s_acc)
        mats[i]["w_in"] = _w_in_from_p(dwin)
        grads["norm_in"].append(dgin[0])
        grads["sink_a"].append(dsink[0, :A_HEADS])
        grads["norm_q_lat"].append(dgq[0])
        grads["norm_kv_lat"].append(dgkv[0])
        grads["norm_out_a"].append(_unperm_heads64(dgoa[0], 0))
        grads["norm_out_b"].append(dgob[0])

    out = {k_: jnp.stack(v_[::-1]) for k_, v_ in grads.items()}
    out["rel_bias_table"] = _bias_grad(s_accs, bucket)[:, 0].reshape(N_BUCKETS, A_HEADS)
    out["norm_final"] = d_norm_final[0]
    return loss_tile[0, 0], dh.reshape(cfg.B, cfg.LP, D_MODEL)[:, BLK:], out, mats, _meta_grad(cfg, dh)[:N_META]


MESH = pl.DeviceIdType.MESH
ANY = pl.BlockSpec(memory_space=pl.ANY)


def _place():
    x, y, c = lax.axis_index("x"), lax.axis_index("y"), lax.axis_index("c")
    others = [(1 - x, y), (x, 1 - y), (1 - x, 1 - y)]
    return x, y, c, others


Comm = collections.namedtuple("Comm", "inputs out_shapes scratch start wait")


def _gather_comm(shards):
    n = len(shards)

    def copies(ins, outs, sems, arriving):
        send_sems, recv_sems, local_sems = sems
        x, y, c, others = _place()
        k_me = 2 * x + y
        local = [pltpu.make_async_copy(ins[a], outs[a].at[k_me], local_sems.at[a]) for a in range(n)]
        remote = [pltpu.make_async_remote_copy(src_ref=ins[a], dst_ref=outs[a].at[2 * ox + oy if arriving else k_me],
                                               send_sem=send_sems.at[3 * a + j], recv_sem=recv_sems.at[3 * a + j],
                                               device_id=(ox, oy, c), device_id_type=MESH)
                  for a in range(n) for j, (ox, oy) in enumerate(others)]
        return local, remote

    def start(ins, outs, sems):
        local, sends = copies(ins, outs, sems, arriving=False)
        for cp in local + sends:
            cp.start()

    def wait(ins, outs, sems):
        local, recvs = copies(ins, outs, sems, arriving=True)
        for cp in recvs:
            cp.wait_recv()
        for cp in recvs:
            cp.wait_send()
        for cp in local:
            cp.wait()

    return Comm(list(shards), [jax.ShapeDtypeStruct((4, *s.shape), s.dtype) for s in shards],
                [pltpu.SemaphoreType.DMA((3 * n,)), pltpu.SemaphoreType.DMA((3 * n,)), pltpu.SemaphoreType.DMA((n,))], start, wait)


def _scatter_comm(parts):
    n = len(parts)

    def copies(ins, outs, sems):
        send_sems, recv_sems = sems
        x, y, c, others = _place()
        return [pltpu.make_async_remote_copy(src_ref=ins[a].at[2 * ox + oy], dst_ref=outs[a].at[j], send_sem=send_sems.at[3 * a + j],
                                             recv_sem=recv_sems.at[3 * a + j], device_id=(ox, oy, c), device_id_type=MESH)
                for a in range(n) for j, (ox, oy) in enumerate(others)]

    def start(ins, outs, sems):
        for cp in copies(ins, outs, sems):
            cp.start()

    def wait(ins, outs, sems):
        cps = copies(ins, outs, sems)
        for cp in cps:
            cp.wait_recv()
        for cp in cps:
            cp.wait_send()

    return Comm(list(parts), [jax.ShapeDtypeStruct((3, *p.shape[1:]), p.dtype) for p in parts],
                [pltpu.SemaphoreType.DMA((3 * n,)), pltpu.SemaphoreType.DMA((3 * n,))], start, wait)


def _run_comm(comm, name):
    ni, no = len(comm.inputs), len(comm.out_shapes)

    def body(*refs):
        ins, outs, sems = refs[:ni], refs[ni:ni + no], refs[ni + no:]
        comm.start(ins, outs, sems)
        comm.wait(ins, outs, sems)

    return pl.pallas_call(body, name=name, in_specs=[ANY] * ni, out_specs=[ANY] * no, out_shape=comm.out_shapes,
                          scratch_shapes=comm.scratch)(*comm.inputs)


def _call_with_comm(body, comm, *, grid, name, in_specs, out_specs, out_shape, args):
    if comm is None:
        return pl.pallas_call(body, grid=grid, name=name, in_specs=in_specs, out_specs=out_specs, out_shape=out_shape,
                              compiler_params=_cp(*["arbitrary"] * len(grid)))(*args)
    n_in, n_out, ci, co = len(in_specs), len(out_specs), len(comm.inputs), len(comm.out_shapes)

    def wrapped(*refs):
        ins, cins = refs[:n_in], refs[n_in:n_in + ci]
        outs, couts = refs[n_in + ci:n_in + ci + n_out], refs[n_in + ci + n_out:n_in + ci + n_out + co]
        sems = refs[n_in + ci + n_out + co:]
        ids = [pl.program_id(a) for a in range(len(grid))]
        first = functools.reduce(jnp.logical_and, [i == 0 for i in ids])
        last = functools.reduce(jnp.logical_and, [i == g - 1 for i, g in zip(ids, grid)])

        @pl.when(first)
        def _():
            comm.start(cins, couts, sems)

        body(*ins, *outs)

        @pl.when(last)
        def _():
            comm.wait(cins, couts, sems)

    return pl.pallas_call(
        wrapped, grid=grid, name=name + "_comm", in_specs=[*in_specs, *[ANY] * ci], out_specs=[*out_specs, *[ANY] * co],
        out_shape=[*out_shape, *comm.out_shapes], scratch_shapes=comm.scratch,
        compiler_params=_cp(*["arbitrary"] * len(grid)))(*args, *comm.inputs)


def _swap_sibling(arrs):
    n = len(arrs)

    def body(*refs):
        ins, outs = refs[:n], refs[n:2 * n]
        send_sems, recv_sems = refs[2 * n:]
        x, y, c, _ = _place()
        copies = [pltpu.make_async_remote_copy(src_ref=ins[a], dst_ref=outs[a], send_sem=send_sems.at[a], recv_sem=recv_sems.at[a],
                                               device_id=(x, y, 1 - c), device_id_type=MESH) for a in range(n)]
        for cp in copies:
            cp.start()
        for cp in copies:
            cp.wait_recv()
        for cp in copies:
            cp.wait_send()

    return pl.pallas_call(
        body, name="swap_sibling", in_specs=[ANY] * n, out_specs=[ANY] * n,
        out_shape=[jax.ShapeDtypeStruct(a.shape, a.dtype) for a in arrs],
        scratch_shapes=[pltpu.SemaphoreType.DMA((n,)), pltpu.SemaphoreType.DMA((n,))],
    )(*arrs)


def _allreduce_small(v):
    def body(v_ref, o_ref, buf, send_sems, recv_sems):
        x, y, c, _ = _place()
        me = 4 * x + 2 * y + c
        buf[me] = v_ref[...]

        def copy(r):
            tx, ty, tc = (x + (r >> 2)) % 2, (y + ((r >> 1) & 1)) % 2, (c + (r & 1)) % 2
            return tx, ty, tc

        sends = []
        for r in range(1, 8):
            tx, ty, tc = copy(r)
            sends.append(pltpu.make_async_remote_copy(src_ref=v_ref, dst_ref=buf.at[me], send_sem=send_sems.at[r - 1],
                                                      recv_sem=recv_sems.at[r - 1], device_id=(tx, ty, tc), device_id_type=MESH))
        for cp in sends:
            cp.start()
        for r in range(1, 8):
            tx, ty, tc = copy(r)
            pltpu.make_async_remote_copy(src_ref=v_ref, dst_ref=buf.at[4 * tx + 2 * ty + tc], send_sem=send_sems.at[r - 1],
                                         recv_sem=recv_sems.at[r - 1], device_id=(tx, ty, tc), device_id_type=MESH).wait_recv()
        for cp in sends:
            cp.wait_send()
        acc = buf[0]
        for d in range(1, 8):
            acc = acc + buf[d]
        o_ref[...] = acc

    return pl.pallas_call(
        body, name="allreduce_small", in_specs=[pl.BlockSpec(memory_space=pltpu.VMEM)], out_specs=pl.BlockSpec(memory_space=pltpu.VMEM),
        out_shape=jax.ShapeDtypeStruct(v.shape, F32),
        scratch_shapes=[pltpu.VMEM((8, *v.shape), F32), pltpu.SemaphoreType.DMA((7,)), pltpu.SemaphoreType.DMA((7,))],
    )(v)


def _rows_view(a):
    return a.reshape(-1, a.shape[-1])


def _elementwise(name, fn, ins, n_out):
    rows, cols = ins[0].shape
    tm = min(rows, 256)
    spec = pl.BlockSpec((tm, cols), lambda i: (i, 0))

    def body(*refs):
        outs = fn(*[r[...] for r in refs[:len(ins)]])
        for o_ref, o in zip(refs[len(ins):], outs):
            o_ref[...] = o

    return pl.pallas_call(
        body, grid=(rows // tm,), name=name, in_specs=[spec] * len(ins), out_specs=[spec] * n_out,
        out_shape=[jax.ShapeDtypeStruct((rows, cols), F32)] * n_out, compiler_params=_cp("parallel"),
    )(*ins)


def _sum_parts(name, own, recv):
    def fn(o, r0, r1, r2):
        return (o + r0.astype(F32) + r1.astype(F32) + r2.astype(F32),)

    return _elementwise("sum_parts_" + name, fn, [own, recv[0], recv[1], recv[2]], 1)[0]


def _adamw(name, w, m, v, g_parts):
    def fn(w_, m_, v_, *gs):
        g = gs[0]
        for extra in gs[1:]:
            g = g + extra
        m_new = ADAM_B1 * m_ + (1.0 - ADAM_B1) * g
        v_new = ADAM_B2 * v_ + (1.0 - ADAM_B2) * (g * g)
        m_hat = m_new / (1.0 - ADAM_B1 ** ADAM_STEP)
        v_hat = v_new / (1.0 - ADAM_B2 ** ADAM_STEP)
        delta = -ADAM_LR * (m_hat / (jnp.sqrt(v_hat) + ADAM_EPS) + ADAM_WD * w_)
        return g, delta, m_new, v_new

    return _elementwise("adamw_" + name, fn, [w, m, v, *g_parts], 4)


MAT_AXIS = {"w_in": 1, "w_uq": 1, "w_ukv": 1, "w_out": 0}
SMALL = ("rel_bias_table", "norm_in", "sink_a", "norm_q_lat", "norm_kv_lat", "norm_out_a", "norm_out_b", "norm_final")
WEIGHTS = ("meta_tokens", "rel_bias_table", "norm_in", "w_in", "sink_a", "norm_q_lat", "w_uq", "norm_kv_lat", "w_ukv",
           "norm_out_a", "norm_out_b", "w_out", "norm_final")
SMALL_ROWS, SMALL_COLS = 8, 1024


def _pack_small(d, loss=None):
    flat = [d[n].reshape(-1) for n in SMALL]
    if loss is not None:
        flat.append(loss.reshape(1))
    used = sum(f.shape[0] for f in flat)
    flat.append(jnp.zeros((SMALL_ROWS * SMALL_COLS - used,), F32))
    return jnp.concatenate(flat).reshape(SMALL_ROWS, SMALL_COLS)


def _unpack_small(p, like):
    flat, out, off = p.reshape(-1), {}, 0
    for n in SMALL:
        size = int(np.prod(like[n].shape))
        out[n] = flat[off:off + size].reshape(like[n].shape)
        off += size
    return out, flat[off]


def _split4(a, axis):
    size = a.shape[axis] // 4
    return jnp.stack([lax.slice_in_dim(a, k * size, (k + 1) * size, axis=axis) for k in range(4)])


def _train_step(cfg, x, target, w, m, v):
    depth = w["w_in"].shape[0]
    rest = tuple(n for n in MATRICES if n != "w_in")
    weights, splits, received = {}, {}, {}

    def gather(i, names, also=()):
        def deliver(res):
            for n, g in zip(names, res):
                weights[i, n] = jnp.concatenate([g[k] for k in range(4)], axis=MAT_AXIS[n])

        return _gather_comm([w[n][i].astype(BF16) for n in names] + list(also)), deliver

    def scatter(i, names, mats, also=()):
        for n in names:
            splits[i, n] = _split4(mats[i][n], MAT_AXIS[n])

        def deliver(res):
            for n, r in zip(names, res):
                received[i, n] = r

        return _scatter_comm([splits[i, n].astype(BF16) for n in names] + list(also)), deliver

    def rider(stage, i, mats):
        if stage == "inproj_fwd" and i == 0:
            return gather(0, rest)
        if stage == "mla_fwd" and i + 1 < depth:
            return gather(i + 1, MATRICES)
        if stage == "mla_bwd" and i + 1 < depth:
            return scatter(i + 1, MATRICES, mats)
        if stage == "inproj_bwd" and i == 0:
            return scatter(0, rest, mats)
        return None

    comm, deliver = gather(0, ("w_in",), also=[w["meta_tokens"]])
    first = _run_comm(comm, "gather_first")
    deliver(first)
    meta = jnp.concatenate([first[1][k] for k in range(4)], axis=1)

    loss_local, grad_x, g, mats, g_meta = _local_grads(cfg, x, target, meta, w["rel_bias_table"], {n: w[n] for n in SMALL},
                                                      lambda i, n: weights[i, n], rider)
    meta_split = _split4(g_meta, 1)
    comm, deliver = scatter(0, ("w_in",), mats, also=[meta_split.astype(BF16)])
    last = _run_comm(comm, "scatter_last")
    deliver(last)

    small_sum = _allreduce_small(_pack_small(g, loss_local))
    g_small, loss = _unpack_small(small_sum, {n: w[n] for n in SMALL})

    k_me = 2 * lax.axis_index("x") + lax.axis_index("y")

    def core_sum(name, split, recv):
        own = lax.dynamic_index_in_dim(split, k_me, 0, keepdims=False)
        return _sum_parts(name, _rows_view(own), recv.reshape(3, -1, recv.shape[-1]))

    partial = [core_sum("meta_tokens", meta_split, last[1])]
    for n in MATRICES:
        partial.append(jnp.concatenate([core_sum(f"{n}_{i}", splits[i, n], received[i, n]) for i in range(depth)], axis=0))
    sibling = _swap_sibling(partial)

    outs = {}
    for n, p_me, p_sib in zip(("meta_tokens", *MATRICES), partial, sibling):
        res = _adamw(n, _rows_view(w[n]), _rows_view(m[n]), _rows_view(v[n]), [p_me, p_sib])
        outs[n] = [r.reshape(w[n].shape) for r in res]
    res = _adamw("small", _pack_small(w), _pack_small(m), _pack_small(v), [_pack_small(g_small)])
    unpacked = [_unpack_small(r, {n: w[n] for n in SMALL})[0] for r in res]
    for n in SMALL:
        outs[n] = [u[n] for u in unpacked]

    result = [loss, grad_x]
    for field in range(4):
        result.extend(outs[n][field] for n in WEIGHTS)
    return tuple(result)


def kernel(x, meta_tokens, rel_bias_table, norm_in, w_in, sink_a, norm_q_lat, w_uq, norm_kv_lat, w_ukv, norm_out_a, norm_out_b, w_out, norm_final, loss_target, m_meta_tokens, m_rel_bias_table, m_norm_in, m_w_in, m_sink_a, m_norm_q_lat, m_w_uq, m_norm_kv_lat, m_w_ukv, m_norm_out_a, m_norm_out_b, m_w_out, m_norm_final, v_meta_tokens, v_rel_bias_table, v_norm_in, v_w_in, v_sink_a, v_norm_q_lat, v_w_uq, v_norm_kv_lat, v_w_ukv, v_norm_out_a, v_norm_out_b, v_w_out, v_norm_final):
    w = dict(zip(WEIGHTS, (meta_tokens, rel_bias_table, norm_in, w_in, sink_a, norm_q_lat, w_uq, norm_kv_lat, w_ukv, norm_out_a, norm_out_b, w_out, norm_final)))
    m = dict(zip(WEIGHTS, (m_meta_tokens, m_rel_bias_table, m_norm_in, m_w_in, m_sink_a, m_norm_q_lat, m_w_uq, m_norm_kv_lat, m_w_ukv, m_norm_out_a, m_norm_out_b, m_w_out, m_norm_final)))
    v = dict(zip(WEIGHTS, (v_meta_tokens, v_rel_bias_table, v_norm_in, v_w_in, v_sink_a, v_norm_q_lat, v_w_uq, v_norm_kv_lat, v_w_ukv, v_norm_out_a, v_norm_out_b, v_w_out, v_norm_final)))
    cfg = make_cfg(x.shape[0], x.shape[1])
    return _train_step(cfg, x, loss_target, w, m, v)
```

```python
import collections
import functools
import math

import jax
import jax.numpy as jnp
import numpy as np
from jax import lax
from jax.experimental import pallas as pl
from jax.experimental.pallas import tpu as pltpu

F32 = jnp.float32
BF16 = jnp.bfloat16

BLK = 128
N_META = 16
D_MODEL = 1024
A_HEADS, A_KV, A_DH = 8, 2, 64
B_HEADS, B_NOPE, B_ROPE, B_DV = 8, 64, 32, 64
Q_RANK, KV_RANK = 256, 128
N_BUCKETS, MAX_DIST = 32, 128
ROPE_THETA = 10000.0
EPS = 1e-6
IN_WIDTH = 2208
W_IN_P = 2304
NEG = -1e30
MASK_LANE = 96
Q_SCALE = (B_NOPE + B_ROPE) ** -0.5 * math.log2(math.e)
LN2 = math.log(2.0)
VMEM_LIMIT = 48 * 1024 * 1024

ADAM_LR, ADAM_B1, ADAM_B2, ADAM_EPS, ADAM_WD, ADAM_STEP = 0.001, 0.9, 0.999, 1e-08, 0.01, 10

Cfg = collections.namedtuple("Cfg", "B S NB NJ LP TP")


def make_cfg(batch, seq):
    nb = seq // BLK
    nj = nb + 1
    return Cfg(batch, seq, nb, nj, nj * BLK, batch * nj * BLK)


def _cp(*sem):
    return pltpu.CompilerParams(dimension_semantics=sem, vmem_limit_bytes=VMEM_LIMIT)


def _dot(a, b):
    return jnp.dot(a, b, preferred_element_type=F32)


def _dot_nt(a, b):
    return lax.dot_general(a, b, (((1,), (1,)), ((), ())), preferred_element_type=F32)


def _dot_tn(a, b):
    return lax.dot_general(a, b, (((0,), (0,)), ((), ())), preferred_element_type=F32)


def _rms(x, width=None):
    n = x.shape[-1] if width is None else width
    r = lax.rsqrt(jnp.sum(x * x, axis=-1, keepdims=True) * (1.0 / n) + EPS)
    return x * r, r


def _rms_bwd(xhat, r, t):
    n = xhat.shape[-1]
    return r * (t - xhat * (jnp.sum(t * xhat, axis=-1, keepdims=True) * (1.0 / n)))


def _sigmoid(x):
    return 1.0 / (1.0 + jnp.exp(-x))


def _lane(shape):
    return lax.broadcasted_iota(jnp.int32, shape, len(shape) - 1)


def _swap_rope(x):
    n = x.shape[-1]
    lane = _lane(x.shape) % BLK
    up = pltpu.roll(x, n - 16, axis=x.ndim - 1)
    dn = pltpu.roll(x, 16, axis=x.ndim - 1)
    return jnp.where((lane >= 64) & (lane < 80), up, jnp.where((lane >= 80) & (lane < 96), dn, 0.0))


A_ORDER = (0, 4, 1, 5, 2, 6, 3, 7)


def _jtype(j, nj):
    return 0 if j == 0 else 1 if j == 1 else 3 if j == nj - 1 else 2


def _window_structure(nj):
    def pos(blk, r):
        return np.where(blk == 0, r, N_META + (blk - 1) * BLK + r)

    def valid(blk, r):
        return np.where(blk == 0, r < N_META, True)

    r = np.arange(BLK)
    rels, viss = [], []
    for j in (0, 1, 2, nj - 1):
        qpos = pos(j, r)[:, None]
        rel_t, vis_t = [], []
        for s, kb in enumerate((0, j - 1, j, j + 1)):
            slot_ok = (s == 0) or (1 <= kb <= nj - 1)
            kbc = min(max(kb, 0), nj - 1)
            kpos = pos(kbc, r)[None, :]
            rel = kpos - qpos
            v = valid(kbc, r)[None, :] & np.ones((BLK, 1), bool)
            if s > 0:
                v = v & (np.abs(rel) <= BLK)
            rel_t.append(rel)
            vis_t.append(v & slot_ok)
        rels.append(np.concatenate(rel_t, axis=1))
        viss.append(np.concatenate(vis_t, axis=1))
    return np.stack(rels).astype(np.int32), np.stack(viss)


def _t5_bucket(rel):
    nb = N_BUCKETS // 2
    max_exact = nb // 2
    ret = jnp.where(rel > 0, nb, 0)
    n = jnp.abs(rel)
    nf = jnp.maximum(n, 1).astype(jnp.float32)
    large = max_exact + (jnp.log(nf / max_exact) / math.log(MAX_DIST / max_exact) * (nb - max_exact)).astype(jnp.int32)
    large = jnp.minimum(large, nb - 1)
    return ret + jnp.where(n < max_exact, n, large)


def _perm_heads64(a, axis):
    parts = [lax.slice_in_dim(a, h * 64, (h + 1) * 64, axis=axis) for h in A_ORDER]
    return jnp.concatenate(parts, axis=axis)


def _unperm_heads64(a, axis):
    inv = [A_ORDER.index(h) for h in range(8)]
    parts = [lax.slice_in_dim(a, p * 64, (p + 1) * 64, axis=axis) for p in inv]
    return jnp.concatenate(parts, axis=axis)


def _w_in_to_p(w):
    sl = lambda a, b: lax.slice_in_dim(w, a, b, axis=1)
    z = lambda n: jnp.zeros((w.shape[0], n), w.dtype)
    return jnp.concatenate([_perm_heads64(sl(0, 512), 1), sl(512, 768), _perm_heads64(sl(768, 1280), 1), sl(1696, 2208),
                            sl(1280, 1536), sl(1536, 1664), z(64), sl(1664, 1696), z(32)], axis=1)


def _w_in_from_p(g):
    sl = lambda a, b: lax.slice_in_dim(g, a, b, axis=1)
    return jnp.concatenate([_unperm_heads64(sl(0, 512), 1), sl(512, 768), _unperm_heads64(sl(768, 1280), 1),
                            sl(1792, 2048), sl(2048, 2176), sl(2240, 2272), sl(1280, 1792)], axis=1)


def _w_uq_to_p(w):
    z = jnp.zeros((w.shape[0], 32), w.dtype)
    return jnp.concatenate([p for h in range(8) for p in (lax.slice_in_dim(w, h * 96, (h + 1) * 96, axis=1), z)], axis=1)


def _w_uq_from_p(g):
    return jnp.concatenate([lax.slice_in_dim(g, h * 128, h * 128 + 96, axis=1) for h in range(8)], axis=1)


def _w_ukv_to_p(w):
    z = jnp.zeros((w.shape[0], 64), w.dtype)
    ks = [p for h in range(8) for p in (lax.slice_in_dim(w, h * 128, h * 128 + 64, axis=1), z)]
    vs = [lax.slice_in_dim(w, h * 128 + 64, (h + 1) * 128, axis=1) for h in range(8)]
    return jnp.concatenate(ks + vs, axis=1)


def _w_ukv_from_p(g):
    parts = []
    for h in range(8):
        parts.append(lax.slice_in_dim(g, h * 128, h * 128 + 64, axis=1))
        parts.append(lax.slice_in_dim(g, 1024 + h * 64, 1024 + (h + 1) * 64, axis=1))
    return jnp.concatenate(parts, axis=1)


def _w_out_to_p(w):
    return jnp.concatenate([_perm_heads64(lax.slice_in_dim(w, 0, 512, axis=0), 0), lax.slice_in_dim(w, 512, 1024, axis=0)], axis=0)


def _w_out_from_p(g):
    return jnp.concatenate([_unperm_heads64(lax.slice_in_dim(g, 0, 512, axis=0), 0), lax.slice_in_dim(g, 512, 1024, axis=0)], axis=0)


def _rope_tables(cfg):
    half = B_ROPE // 2
    length = N_META + cfg.S
    freqs = ROPE_THETA ** (-jnp.arange(half, dtype=jnp.float32) / half)
    ang = jnp.arange(length, dtype=jnp.float32)[:, None] * freqs[None, :]
    cos, sin = jnp.cos(ang), jnp.sin(ang)

    def rows(t):
        return jnp.concatenate([t[:N_META], jnp.zeros((BLK - N_META, t.shape[1]), t.dtype), t[N_META:]], axis=0)

    ones = jnp.ones((length, 64), F32)
    zer = jnp.zeros((length, 32), F32)
    c_tab = rows(jnp.concatenate([ones, cos, cos, zer], axis=1))
    s_tab = rows(jnp.concatenate([zer, zer, -sin, sin, zer], axis=1))
    return c_tab, s_tab


def _inproj_fwd(cfg, h, g, w_p, comm=None):
    tm = 256

    def body(h_ref, g_ref, w_ref, pa_ref, pf_ref):
        xh, _ = _rms(h_ref[...])
        u = (xh * g_ref[...]).astype(BF16)
        acc = _dot(u, w_ref[...])
        pa_ref[...] = acc[:, :768].astype(BF16)
        pf_ref[...] = acc[:, 768:]

    return _call_with_comm(
        body, comm, grid=(cfg.TP // tm,), name="inproj_fwd",
        in_specs=[pl.BlockSpec((tm, D_MODEL), lambda i: (i, 0)), pl.BlockSpec((1, D_MODEL), lambda i: (0, 0)),
                  pl.BlockSpec((D_MODEL, W_IN_P), lambda i: (0, 0))],
        out_specs=[pl.BlockSpec((tm, 768), lambda i: (i, 0)), pl.BlockSpec((tm, 1536), lambda i: (i, 0))],
        out_shape=[jax.ShapeDtypeStruct((cfg.TP, 768), BF16), jax.ShapeDtypeStruct((cfg.TP, 1536), F32)],
        args=(h, g, w_p))


def _lat_fwd(cfg, pf, gq, gkv, wq_p, wkv_p, c_tab, s_tab):
    nj = cfg.NJ

    def body(cq_ref, ckv_ref, kr_ref, gq_ref, gkv_ref, wq_ref, wkv_ref, c_ref, s_ref, q_ref, k_ref, v_ref, kt_ref, vt_ref):
        c1, s1 = c_ref[...], s_ref[...]
        c8, s8 = jnp.tile(c1, (1, 8)), jnp.tile(s1, (1, 8))
        mask_lane = _lane((BLK, 1024)) % BLK == MASK_LANE
        zero_row = (pl.program_id(1) == 0) & (lax.broadcasted_iota(jnp.int32, (BLK, 1024), 0) >= N_META)
        xq, _ = _rms(cq_ref[...])
        qp = _dot((xq * gq_ref[...]).astype(BF16), wq_ref[...])
        q_ref[...] = jnp.where(mask_lane, 1.0, (qp * c8 + _swap_rope(qp) * s8) * Q_SCALE).astype(BF16)
        xk, _ = _rms(ckv_ref[...])
        kvp = _dot((xk * gkv_ref[...]).astype(BF16), wkv_ref[...])
        kr = kr_ref[...]
        krr = kr * c1 + _swap_rope(kr) * s1
        k = jnp.where(mask_lane & zero_row, NEG, kvp[:, :1024] + jnp.tile(krr, (1, 8)))
        k_ref[...] = k.astype(BF16)
        v_ref[...] = kvp[:, 1024:].astype(BF16)
        kt_ref[...] = k.T.astype(BF16)
        vt_ref[...] = kvp[:, 1024:].T.astype(BF16)

    row = lambda b, j: b * nj + j
    return pl.pallas_call(
        body, grid=(cfg.B, nj), name="lat_fwd",
        in_specs=[pl.BlockSpec((BLK, 256), lambda b, j: (row(b, j), 4)), pl.BlockSpec((BLK, 128), lambda b, j: (row(b, j), 10)),
                  pl.BlockSpec((BLK, 128), lambda b, j: (row(b, j), 11)),
                  pl.BlockSpec((1, 256), lambda b, j: (0, 0)), pl.BlockSpec((1, 128), lambda b, j: (0, 0)),
                  pl.BlockSpec((256, 1024), lambda b, j: (0, 0)), pl.BlockSpec((128, 1536), lambda b, j: (0, 0)),
                  pl.BlockSpec((BLK, 128), lambda b, j: (j, 0)), pl.BlockSpec((BLK, 128), lambda b, j: (j, 0))],
        out_specs=[pl.BlockSpec((BLK, 1024), lambda b, j: (row(b, j), 0)), pl.BlockSpec((BLK, 1024), lambda b, j: (row(b, j), 0)),
                   pl.BlockSpec((BLK, 512), lambda b, j: (row(b, j), 0)),
                   pl.BlockSpec((1024, BLK), lambda b, j: (b, j)), pl.BlockSpec((512, BLK), lambda b, j: (b, j))],
        out_shape=[jax.ShapeDtypeStruct((cfg.TP, 1024), BF16), jax.ShapeDtypeStruct((cfg.TP, 1024), BF16),
                   jax.ShapeDtypeStruct((cfg.TP, 512), BF16),
                   jax.ShapeDtypeStruct((cfg.B * 1024, cfg.LP), BF16), jax.ShapeDtypeStruct((cfg.B * 512, cfg.LP), BF16)],
        compiler_params=_cp("parallel", "parallel"),
    )(pf, pf, pf, gq, gkv, wq_p, wkv_p, c_tab, s_tab)


def _gate_halves(ya, yb, ga, gb, goa, gob):
    xa, ra = _rms(ya)
    xb, rb = _rms(yb)
    sga, sgb = _sigmoid(ga), _sigmoid(gb)
    return xa, ra, xb, rb, sga, sgb, xa * goa * (ga * sga), xb * gob * (gb * sgb)


def _out_fwd(cfg, ya, yb, pf, goa, gob, wo_p, h):
    tm = 256

    def body(ya_ref, yb_ref, ga_ref, gb_ref, goa_ref, gob_ref, w_ref, h_ref, o_ref):
        *_, y_a, y_b = _gate_halves(ya_ref[...], yb_ref[...], ga_ref[...], gb_ref[...], goa_ref[...], gob_ref[...])
        y = jnp.concatenate([y_a, y_b], axis=1).astype(BF16)
        o_ref[...] = h_ref[...] + _dot(y, w_ref[...])

    return pl.pallas_call(
        body, grid=(cfg.TP // tm,), name="out_fwd",
        in_specs=[pl.BlockSpec((tm, 512), lambda i: (i, 0)), pl.BlockSpec((tm, 512), lambda i: (i, 0)),
                  pl.BlockSpec((tm, 512), lambda i: (i, 0)), pl.BlockSpec((tm, 512), lambda i: (i, 1)),
                  pl.BlockSpec((1, 512), lambda i: (0, 0)), pl.BlockSpec((1, 512), lambda i: (0, 0)),
                  pl.BlockSpec((D_MODEL, D_MODEL), lambda i: (0, 0)), pl.BlockSpec((tm, D_MODEL), lambda i: (i, 0))],
        out_specs=pl.BlockSpec((tm, D_MODEL), lambda i: (i, 0)),
        out_shape=jax.ShapeDtypeStruct((cfg.TP, D_MODEL), F32),
        compiler_params=_cp("parallel"),
    )(ya, yb, pf, pf, goa, gob, wo_p, h)


BIAS_CHUNK = 8192
N_PAIRS = BLK * 512


def _one_hot_buckets(bk):
    return jnp.where(lax.broadcasted_iota(jnp.int32, (N_BUCKETS, bk.shape[1]), 0) == bk, 1.0, 0.0)


def _bias_build(table_t, bucket, maskadd):
    def body(tab_ref, bk_ref, ma_ref, o_ref):
        picked = jnp.dot(tab_ref[...], _one_hot_buckets(bk_ref[0]), preferred_element_type=F32, precision=lax.Precision.HIGHEST)
        o_ref[0] = picked + ma_ref[0]

    flat = pl.BlockSpec((1, 1, BIAS_CHUNK), lambda t, c: (t, 0, c))
    return pl.pallas_call(
        body, grid=(4, N_PAIRS // BIAS_CHUNK), name="bias_build",
        in_specs=[pl.BlockSpec((A_HEADS, N_BUCKETS), lambda t, c: (0, 0)), flat, flat],
        out_specs=pl.BlockSpec((1, A_HEADS, BIAS_CHUNK), lambda t, c: (t, 0, c)),
        out_shape=jax.ShapeDtypeStruct((4, A_HEADS, N_PAIRS), F32),
        compiler_params=_cp("parallel", "parallel"),
    )(table_t, bucket, maskadd)


def _bias_grad(s_flat, bucket):
    depth = len(s_flat)

    def body(*refs):
        s_refs, bk_ref, o_ref = refs[:depth], refs[depth], refs[depth + 1]

        @pl.when((pl.program_id(0) == 0) & (pl.program_id(1) == 0))
        def _():
            o_ref[...] = jnp.zeros_like(o_ref)

        total = s_refs[0][0]
        for extra in s_refs[1:]:
            total = total + extra[0]
        o_ref[...] += lax.dot_general(total, _one_hot_buckets(bk_ref[0]), (((1,), (1,)), ((), ())),
                                      preferred_element_type=F32, precision=lax.Precision.HIGHEST)

    s_spec = pl.BlockSpec((1, A_HEADS, BIAS_CHUNK), lambda t, c: (t, 0, c))
    return pl.pallas_call(
        body, grid=(4, N_PAIRS // BIAS_CHUNK), name="bias_grad",
        in_specs=[s_spec] * depth + [pl.BlockSpec((1, 1, BIAS_CHUNK), lambda t, c: (t, 0, c))],
        out_specs=pl.BlockSpec((A_HEADS, N_BUCKETS), lambda t, c: (0, 0)),
        out_shape=jax.ShapeDtypeStruct((A_HEADS, N_BUCKETS), F32),
        compiler_params=_cp("arbitrary", "arbitrary"),
    )(*s_flat, bucket)


def _win_specs(cfg):
    nj = cfg.NJ
    row = lambda b, j: b * nj + j
    jt = lambda j: jnp.where(j == 0, 0, jnp.where(j == 1, 1, jnp.where(j == nj - 1, 3, 2)))
    slot_rows = [lambda b, j: row(b, 0), lambda b, j: row(b, jnp.maximum(j - 1, 0)), lambda b, j: row(b, j),
                 lambda b, j: row(b, jnp.minimum(j + 1, nj - 1))]
    k_specs = [pl.BlockSpec((BLK, 128), functools.partial(lambda b, j, f: (f(b, j), 4), f=f)) for f in slot_rows]
    v_specs = [pl.BlockSpec((BLK, 128), functools.partial(lambda b, j, f: (f(b, j), 5), f=f)) for f in slot_rows]
    q_spec = pl.BlockSpec((BLK, 512), lambda b, j: (row(b, j), 0))
    bias_spec = pl.BlockSpec((1, A_HEADS, BLK, 512), lambda b, j: (jt(j), 0, 0, 0))
    return row, jt, q_spec, k_specs, v_specs, bias_spec


def _stack4(ref):
    return jnp.concatenate([ref[:, c * 128:(c + 1) * 128] for c in range(4)], axis=0)


def _win_keys(k_refs, v_refs):
    k4 = jnp.concatenate([r[...] for r in k_refs], axis=0)
    v4 = jnp.concatenate([r[...] for r in v_refs], axis=0)
    lane_k = _lane(k4.shape)
    return (jnp.where(lane_k < 64, k4, jnp.zeros_like(k4)), jnp.where(lane_k >= 64, k4, jnp.zeros_like(k4))), v4


def _sink_col(sink_ref, hf):
    rowi = lax.broadcasted_iota(jnp.int32, (4 * BLK, 1), 0)
    col = jnp.full((4 * BLK, 1), sink_ref[4 * hf + 3], F32)
    for c in (2, 1, 0):
        col = jnp.where(rowi < (c + 1) * BLK, sink_ref[4 * hf + c], col)
    return col


def _win_fwd(cfg, pa, bias, sink):
    row, jt, q_spec, k_specs, v_specs, bias_spec = _win_specs(cfg)
    scale = A_DH ** -0.5

    def body(sink_ref, q_ref, k0, k1, k2, k3, v0, v1, v2, v3, b_ref, o_ref, lse_ref):
        kk, v4 = _win_keys((k0, k1, k2, k3), (v0, v1, v2, v3))
        qs = _stack4(q_ref)
        lane_o = _lane((4 * BLK, 128))
        outs, lses = [], []
        for hf in range(2):
            s = _dot_nt(qs, kk[hf]) * scale + b_ref[0, 4 * hf:4 * hf + 4].reshape(4 * BLK, 512)
            sink_col = _sink_col(sink_ref, hf)
            m = jnp.maximum(jnp.max(s, axis=1, keepdims=True), sink_col)
            e = jnp.exp(s - m)
            den = jnp.sum(e, axis=1, keepdims=True) + jnp.exp(sink_col - m)
            outs.append(_dot(e.astype(BF16), v4) / den)
            lses.append(m + jnp.log(den))
        o = jnp.where(lane_o < 64, outs[0], outs[1])
        for c in range(4):
            o_ref[:, c * 128:(c + 1) * 128] = o[c * BLK:(c + 1) * BLK]
        lse_ref[...] = jnp.where(lane_o == 0, lses[0], jnp.where(lane_o == 1, lses[1], 0.0))

    return pl.pallas_call(
        body, grid=(cfg.B, cfg.NJ), name="win_fwd",
        in_specs=[pl.BlockSpec(memory_space=pltpu.SMEM), q_spec, *k_specs, *v_specs, bias_spec],
        out_specs=[pl.BlockSpec((BLK, 512), lambda b, j: (row(b, j), 0)), pl.BlockSpec((4 * BLK, 128), lambda b, j: (row(b, j), 0))],
        out_shape=[jax.ShapeDtypeStruct((cfg.TP, 512), F32), jax.ShapeDtypeStruct((4 * cfg.TP, 128), F32)],
        compiler_params=_cp("parallel", "parallel"),
    )(sink, pa, *([pa] * 8), bias)


def _win_bwd(cfg, pa, bias, sink, dya, ya, lse):
    row, jt, q_spec, k_specs, v_specs, bias_spec = _win_specs(cfg)
    nj = cfg.NJ
    scale = A_DH ** -0.5

    def body(sink_ref, q_ref, k0, k1, k2, k3, v0, v1, v2, v3, b_ref, dy_ref, y_ref, lse_ref,
             dq_ref, dkp_ref, dvp_ref, dkm_ref, dvm_ref, s_ref, dsink_ref):
        j = pl.program_id(1)
        kind = jt(j)

        @pl.when((pl.program_id(0) == 0) & (j == 0))
        def _():
            s_ref[...] = jnp.zeros_like(s_ref)

        kk, v4 = _win_keys((k0, k1, k2, k3), (v0, v1, v2, v3))
        qs, dys, ys = _stack4(q_ref), _stack4(dy_ref), _stack4(y_ref)
        lane_o = _lane((4 * BLK, 128))
        half = (lane_o < 64, lane_o >= 64)
        lse_blk = lse_ref[...]
        dq = jnp.zeros((4 * BLK, 128), F32)
        dk4 = jnp.zeros((512, 128), F32)
        dv4 = jnp.zeros((512, 128), F32)
        dsink = jnp.zeros((8, 128), F32)
        lane_s = _lane((8, 128))
        row_s = lax.broadcasted_iota(jnp.int32, (8, 128), 0)
        for hf in range(2):
            lse_h = jnp.sum(jnp.where(lane_o == hf, lse_blk, 0.0), axis=1, keepdims=True)
            s = _dot_nt(qs, kk[hf]) * scale + b_ref[0, 4 * hf:4 * hf + 4].reshape(4 * BLK, 512)
            p = jnp.exp(s - lse_h)
            do_h = jnp.where(half[hf], dys, 0.0)
            delta = jnp.sum(do_h * ys, axis=1, keepdims=True)
            do_b = do_h.astype(BF16)
            ds = p * (_dot_nt(do_b, v4) - delta)
            s_ref[kind, 4 * hf:4 * hf + 4] += ds.reshape(4, BLK, 512)
            sink_grad = jnp.exp(_sink_col(sink_ref, hf) - lse_h) * delta
            for c in range(4):
                tot = -jnp.sum(sink_grad[c * BLK:(c + 1) * BLK])
                dsink = jnp.where((row_s == 0) & (lane_s == 4 * hf + c), tot, dsink)
            dsb = (ds * scale).astype(BF16)
            dq = dq + _dot(dsb, kk[hf])
            dk4 = dk4 + _dot_tn(dsb, jnp.where(half[hf], qs, jnp.zeros_like(qs)))
            dv4 = dv4 + _dot_tn(p.astype(BF16), do_b)
        for c in range(4):
            dq_ref[:, c * 128:(c + 1) * 128] = dq[c * BLK:(c + 1) * BLK].astype(BF16)
        dkp_ref[0] = dk4
        dvp_ref[0] = dv4

        @pl.when(j == 0)
        def _():
            dkm_ref[...] = dk4[:BLK]
            dvm_ref[...] = dv4[:BLK]

        @pl.when(j > 0)
        def _():
            dkm_ref[...] += dk4[:BLK]
            dvm_ref[...] += dv4[:BLK]

        @pl.when((pl.program_id(0) == 0) & (j == 0))
        def _():
            dsink_ref[...] = dsink

        @pl.when((pl.program_id(0) > 0) | (j > 0))
        def _():
            dsink_ref[...] += dsink

    blk_row = pl.BlockSpec((BLK, 512), lambda b, j: (row(b, j), 0))
    return pl.pallas_call(
        body, grid=(cfg.B, nj), name="win_bwd",
        in_specs=[pl.BlockSpec(memory_space=pltpu.SMEM), q_spec, *k_specs, *v_specs, bias_spec, blk_row, blk_row,
                  pl.BlockSpec((4 * BLK, 128), lambda b, j: (row(b, j), 0))],
        out_specs=[blk_row,
                   pl.BlockSpec((1, 512, 128), lambda b, j: (row(b, j), 0, 0)), pl.BlockSpec((1, 512, 128), lambda b, j: (row(b, j), 0, 0)),
                   pl.BlockSpec((BLK, 128), lambda b, j: (b, 0)), pl.BlockSpec((BLK, 128), lambda b, j: (b, 0)),
                   pl.BlockSpec((4, A_HEADS, BLK, 512), lambda b, j: (0, 0, 0, 0)),
                   pl.BlockSpec((8, 128), lambda b, j: (0, 0))],
        out_shape=[jax.ShapeDtypeStruct((cfg.TP, 512), BF16),
                   jax.ShapeDtypeStruct((cfg.B * nj, 512, 128), F32), jax.ShapeDtypeStruct((cfg.B * nj, 512, 128), F32),
                   jax.ShapeDtypeStruct((cfg.B * BLK, 128), F32), jax.ShapeDtypeStruct((cfg.B * BLK, 128), F32),
                   jax.ShapeDtypeStruct((4, A_HEADS, BLK, 512), F32),
                   jax.ShapeDtypeStruct((8, 128), F32)],
        compiler_params=_cp("arbitrary", "arbitrary"),
    )(sink, pa, *([pa] * 8), bias, dya, ya, lse)


def _win_dkv_combine(cfg, dkp, dvp, dkm, dvm):
    nj = cfg.NJ

    def body(kp, vp, km, vm, o_ref):
        o_ref[:BLK, :128] = km[...].astype(BF16)
        o_ref[:BLK, 128:] = vm[...].astype(BF16)
        for kb in range(1, nj):
            for col, part in ((0, kp), (128, vp)):
                tot = part[kb, 2 * BLK:3 * BLK] + part[kb - 1, 3 * BLK:4 * BLK]
                if kb + 1 < nj:
                    tot = tot + part[kb + 1, BLK:2 * BLK]
                o_ref[kb * BLK:(kb + 1) * BLK, col:col + 128] = tot.astype(BF16)

    return pl.pallas_call(
        body, grid=(cfg.B,), name="win_dkv_combine",
        in_specs=[pl.BlockSpec((nj, 512, 128), lambda b: (b, 0, 0)), pl.BlockSpec((nj, 512, 128), lambda b: (b, 0, 0)),
                  pl.BlockSpec((BLK, 128), lambda b: (b, 0)), pl.BlockSpec((BLK, 128), lambda b: (b, 0))],
        out_specs=pl.BlockSpec((cfg.LP, 256), lambda b: (b, 0)),
        out_shape=jax.ShapeDtypeStruct((cfg.TP, 256), BF16),
        compiler_params=_cp("parallel"),
    )(dkp, dvp, dkm, dvm)


def _pair_blockdiag(q):
    lane = _lane(q.shape)
    return jnp.concatenate([jnp.where(lane < 128, q, jnp.zeros_like(q)), jnp.where(lane >= 128, q, jnp.zeros_like(q))], axis=0)


def _mla_fwd(cfg, q, kt, v, comm=None):
    nj, lp = cfg.NJ, cfg.LP

    def body(q_ref, kt_ref, v_ref, o_ref, lse_ref, s_even, s_odd):
        i = pl.program_id(2)
        lane_o = _lane((BLK, 128))

        def logits(s_write):
            s_write[...] = _dot(_pair_blockdiag(q_ref[...]), kt_ref[...])

        def finish(s_read):
            s = s_read[...]
            m = jnp.max(s, axis=1, keepdims=True)
            e = jnp.exp2(s - m)
            den = jnp.sum(e, axis=1, keepdims=True)
            pv = _dot(e.astype(BF16), v_ref[...]) / den
            o_ref[...] = jnp.where(lane_o < 64, pv[:BLK], pv[BLK:])
            lse_ref[0] = jnp.broadcast_to(m + jnp.log2(den), (2 * BLK, 128))

        odd = i % 2 == 1

        @pl.when(i == 0)
        def _():
            logits(s_even)

        @pl.when(odd & (i < nj))
        def _():
            logits(s_odd)
            finish(s_even)

        @pl.when(jnp.logical_not(odd) & (i > 0) & (i < nj))
        def _():
            logits(s_even)
            finish(s_odd)

        @pl.when(i == nj)
        def _():
            finish(s_even if nj % 2 == 1 else s_odd)

    cur = lambda b, i: b * nj + jnp.minimum(i, nj - 1)
    prev = lambda b, i: b * nj + jnp.maximum(i - 1, 0)
    return _call_with_comm(
        body, comm, grid=(cfg.B, 4, nj + 1), name="mla_fwd",
        in_specs=[pl.BlockSpec((BLK, 256), lambda b, p, i: (cur(b, i), p)), pl.BlockSpec((256, lp), lambda b, p, i: (b * 4 + p, 0)),
                  pl.BlockSpec((lp, 128), lambda b, p, i: (b, p))],
        out_specs=[pl.BlockSpec((BLK, 128), lambda b, p, i: (prev(b, i), p)),
                   pl.BlockSpec((1, 2 * BLK, 128), lambda b, p, i: (p, prev(b, i), 0))],
        out_shape=[jax.ShapeDtypeStruct((cfg.TP, 512), F32), jax.ShapeDtypeStruct((4, 2 * cfg.TP, 128), F32)],
        scratch_shapes=[pltpu.VMEM((2 * BLK, lp), F32), pltpu.VMEM((2 * BLK, lp), F32)],
        args=(q, kt, v))


def _mla_bwd(cfg, q, k, kt, vt, dyb, yb, lse, comm=None):
    nj, lp = cfg.NJ, cfg.LP

    def body(q_ref, k_ref, kt_ref, vt_ref, dy_ref, y_ref, lse_ref, dq_ref, dk_ref, dv_ref):
        i = pl.program_id(2)

        @pl.when(i == 0)
        def _():
            dk_ref[...] = jnp.zeros_like(dk_ref)
            dv_ref[...] = jnp.zeros_like(dv_ref)

        lane_o = _lane((BLK, 128))
        qbd = _pair_blockdiag(q_ref[...])
        dy, y = dy_ref[...], y_ref[...]
        do_s = jnp.concatenate([jnp.where(lane_o < 64, dy, 0.0), jnp.where(lane_o >= 64, dy, 0.0)], axis=0)
        delta = jnp.sum(do_s * jnp.concatenate([y, y], axis=0), axis=1, keepdims=True)
        do_b = do_s.astype(BF16)
        p = jnp.exp2(_dot(qbd, kt_ref[...]) - lse_ref[0][:, :1])
        ds = p * (_dot(do_b, vt_ref[...]) - delta)
        dsb = (ds * LN2).astype(BF16)
        dq2 = _dot(dsb, k_ref[...])
        dq_ref[...] = jnp.where(_lane((BLK, 256)) < 128, dq2[:BLK], dq2[BLK:]) * Q_SCALE
        dk_ref[...] += _dot_tn(dsb, qbd)
        dv_ref[...] += _dot_tn(p.astype(BF16), do_b)

    return _call_with_comm(
        body, comm, grid=(cfg.B, 4, nj), name="mla_bwd",
        in_specs=[pl.BlockSpec((BLK, 256), lambda b, p, i: (b * nj + i, p)), pl.BlockSpec((lp, 256), lambda b, p, i: (b, p)),
                  pl.BlockSpec((256, lp), lambda b, p, i: (b * 4 + p, 0)), pl.BlockSpec((128, lp), lambda b, p, i: (b * 4 + p, 0)),
                  pl.BlockSpec((BLK, 128), lambda b, p, i: (b * nj + i, p)), pl.BlockSpec((BLK, 128), lambda b, p, i: (b * nj + i, p)),
                  pl.BlockSpec((1, 2 * BLK, 128), lambda b, p, i: (p, b * nj + i, 0))],
        out_specs=[pl.BlockSpec((BLK, 256), lambda b, p, i: (b * nj + i, p)), pl.BlockSpec((lp, 256), lambda b, p, i: (b, p)),
                   pl.BlockSpec((lp, 128), lambda b, p, i: (b, p))],
        out_shape=[jax.ShapeDtypeStruct((cfg.TP, 1024), F32), jax.ShapeDtypeStruct((cfg.TP, 1024), F32),
                   jax.ShapeDtypeStruct((cfg.TP, 512), F32)],
        args=(q, k, kt, vt, dyb, yb, lse))


def _loss_bwd(cfg, h, target, gf):
    nj, nb = cfg.NJ, cfg.NB

    def body(h_ref, t_ref, g_ref, dh_ref, loss_ref, dg_ref):
        b, j = pl.program_id(0), pl.program_id(1)

        @pl.when((b == 0) & (j == 0))
        def _():
            loss_ref[...] = jnp.zeros_like(loss_ref)
            dg_ref[...] = jnp.zeros_like(dg_ref)

        @pl.when(j == 0)
        def _():
            dh_ref[...] = jnp.zeros_like(dh_ref)

        @pl.when(j > 0)
        def _():
            g = g_ref[...]
            xh, r = _rms(h_ref[...])
            err = xh * g - t_ref[...]
            loss_ref[...] += jnp.where((lax.broadcasted_iota(jnp.int32, (8, 128), 0) == 0) & (_lane((8, 128)) == 0),
                                       (0.5 / D_MODEL) * jnp.sum(err * err), 0.0)
            dy = err * (1.0 / D_MODEL)
            dg_ref[...] += jnp.sum(dy * xh, axis=0, keepdims=True)
            dh_ref[...] = _rms_bwd(xh, r, dy * g)

    return pl.pallas_call(
        body, grid=(cfg.B, nj), name="loss_bwd",
        in_specs=[pl.BlockSpec((BLK, D_MODEL), lambda b, j: (b * nj + j, 0)),
                  pl.BlockSpec((BLK, D_MODEL), lambda b, j: (b * nb + jnp.maximum(j - 1, 0), 0)),
                  pl.BlockSpec((1, D_MODEL), lambda b, j: (0, 0))],
        out_specs=[pl.BlockSpec((BLK, D_MODEL), lambda b, j: (b * nj + j, 0)), pl.BlockSpec((8, 128), lambda b, j: (0, 0)),
                   pl.BlockSpec((1, D_MODEL), lambda b, j: (0, 0))],
        out_shape=[jax.ShapeDtypeStruct((cfg.TP, D_MODEL), F32), jax.ShapeDtypeStruct((8, 128), F32),
                   jax.ShapeDtypeStruct((1, D_MODEL), F32)],
        compiler_params=_cp("arbitrary", "arbitrary"),
    )(h, target, gf)


def _out_bwd(cfg, dh, ya, yb, pf, goa, gob, wo_p):
    tm = 256

    def body(dh_ref, ya_ref, yb_ref, ga_ref, gb_ref, goa_ref, gob_ref, w_ref,
             dya_ref, dyb_ref, dg_ref, dw_ref, dgoa_ref, dgob_ref):
        @pl.when(pl.program_id(0) == 0)
        def _():
            dw_ref[...] = jnp.zeros_like(dw_ref)
            dgoa_ref[...] = jnp.zeros_like(dgoa_ref)
            dgob_ref[...] = jnp.zeros_like(dgob_ref)

        ga, gb, goa, gob = ga_ref[...], gb_ref[...], goa_ref[...], gob_ref[...]
        xa, ra, xb, rb, sga, sgb, y_a, y_b = _gate_halves(ya_ref[...], yb_ref[...], ga, gb, goa, gob)
        dhb = dh_ref[...].astype(BF16)
        dw_ref[...] += _dot_tn(jnp.concatenate([y_a, y_b], axis=1).astype(BF16), dhb)
        dy = _dot_nt(dhb, w_ref[...])
        for (dyh, x, r, g, sg, go, dy_out, dgo_ref, col) in (
                (dy[:, :512], xa, ra, ga, sga, goa, dya_ref, dgoa_ref, 0), (dy[:, 512:], xb, rb, gb, sgb, gob, dyb_ref, dgob_ref, 512)):
            dn = dyh * (g * sg)
            dg_ref[:, col:col + 512] = (dyh * (x * go) * (sg * (1.0 + g * (1.0 - sg)))).astype(BF16)
            dgo_ref[...] += jnp.sum(dn * x, axis=0, keepdims=True)
            dy_out[...] = _rms_bwd(x, r, dn * go)

    half = lambda c: pl.BlockSpec((tm, 512), lambda i: (i, c))
    vec = pl.BlockSpec((1, 512), lambda i: (0, 0))
    return pl.pallas_call(
        body, grid=(cfg.TP // tm,), name="out_bwd",
        in_specs=[pl.BlockSpec((tm, D_MODEL), lambda i: (i, 0)), half(0), half(0), half(0), half(1), vec, vec,
                  pl.BlockSpec((D_MODEL, D_MODEL), lambda i: (0, 0))],
        out_specs=[half(0), half(0), pl.BlockSpec((tm, D_MODEL), lambda i: (i, 0)),
                   pl.BlockSpec((D_MODEL, D_MODEL), lambda i: (0, 0)), vec, vec],
        out_shape=[jax.ShapeDtypeStruct((cfg.TP, 512), F32), jax.ShapeDtypeStruct((cfg.TP, 512), F32),
                   jax.ShapeDtypeStruct((cfg.TP, D_MODEL), BF16), jax.ShapeDtypeStruct((D_MODEL, D_MODEL), F32),
                   jax.ShapeDtypeStruct((1, 512), F32), jax.ShapeDtypeStruct((1, 512), F32)],
        compiler_params=_cp("arbitrary"),
    )(dh, ya, yb, pf, pf, goa, gob, wo_p)


def _lat_bwd(cfg, dq, dk, dv, pf, gq, gkv, wq_p, wkv_p, c_tab, s_tab):
    nj = cfg.NJ

    def body(dq_ref, dk_ref, dv_ref, cq_ref, ckv_ref, gq_ref, gkv_ref, wq_ref, wkv_ref, c_ref, s_ref,
             dl_ref, dwq_ref, dwkv_ref, dgq_ref, dgkv_ref):
        @pl.when((pl.program_id(0) == 0) & (pl.program_id(1) == 0))
        def _():
            dwq_ref[...] = jnp.zeros_like(dwq_ref)
            dwkv_ref[...] = jnp.zeros_like(dwkv_ref)
            dgq_ref[...] = jnp.zeros_like(dgq_ref)
            dgkv_ref[...] = jnp.zeros_like(dgkv_ref)

        c1, s1 = c_ref[...], s_ref[...]
        c8, s8 = jnp.tile(c1, (1, 8)), jnp.tile(s1, (1, 8))
        dq_r = dq_ref[...]
        dqp = (dq_r * c8 + _swap_rope(dq_r * s8)).astype(BF16)
        gq = gq_ref[...]
        xq, rq = _rms(cq_ref[...])
        dwq_ref[...] += _dot_tn((xq * gq).astype(BF16), dqp)
        dn = _dot_nt(dqp, wq_ref[...])
        dgq_ref[...] += jnp.sum(dn * xq, axis=0, keepdims=True)
        dl_ref[:, :256] = _rms_bwd(xq, rq, dn * gq).astype(BF16)

        dk_r = dk_ref[...]
        dkr = dk_r[:, :128]
        for hd in range(1, 8):
            dkr = dkr + dk_r[:, hd * 128:(hd + 1) * 128]
        lane1 = _lane(dkr.shape)
        dkr = jnp.where((lane1 >= 64) & (lane1 < 96), dkr, 0.0)
        dl_ref[:, 384:] = (dkr * c1 + _swap_rope(dkr * s1)).astype(BF16)
        dkv = jnp.concatenate([dk_r, dv_ref[...]], axis=1).astype(BF16)
        gkv = gkv_ref[...]
        xk, rk = _rms(ckv_ref[...])
        dwkv_ref[...] += _dot_tn((xk * gkv).astype(BF16), dkv)
        dn2 = _dot_nt(dkv, wkv_ref[...])
        dgkv_ref[...] += jnp.sum(dn2 * xk, axis=0, keepdims=True)
        dl_ref[:, 256:384] = _rms_bwd(xk, rk, dn2 * gkv).astype(BF16)

    row = lambda b, j: b * nj + j
    const = lambda shape: pl.BlockSpec(shape, lambda b, j: (0, 0))
    return pl.pallas_call(
        body, grid=(cfg.B, nj), name="lat_bwd",
        in_specs=[pl.BlockSpec((BLK, 1024), lambda b, j: (row(b, j), 0)), pl.BlockSpec((BLK, 1024), lambda b, j: (row(b, j), 0)),
                  pl.BlockSpec((BLK, 512), lambda b, j: (row(b, j), 0)),
                  pl.BlockSpec((BLK, 256), lambda b, j: (row(b, j), 4)), pl.BlockSpec((BLK, 128), lambda b, j: (row(b, j), 10)),
                  const((1, 256)), const((1, 128)), const((256, 1024)), const((128, 1536)),
                  pl.BlockSpec((BLK, 128), lambda b, j: (j, 0)), pl.BlockSpec((BLK, 128), lambda b, j: (j, 0))],
        out_specs=[pl.BlockSpec((BLK, 512), lambda b, j: (row(b, j), 0)), const((256, 1024)), const((128, 1536)),
                   const((1, 256)), const((1, 128))],
        out_shape=[jax.ShapeDtypeStruct((cfg.TP, 512), BF16), jax.ShapeDtypeStruct((256, 1024), F32),
                   jax.ShapeDtypeStruct((128, 1536), F32), jax.ShapeDtypeStruct((1, 256), F32), jax.ShapeDtypeStruct((1, 128), F32)],
        compiler_params=_cp("arbitrary", "arbitrary"),
    )(dq, dk, dv, pf, pf, gq, gkv, wq_p, wkv_p, c_tab, s_tab)


def _inproj_bwd(cfg, h, g, w_p, dqa, dkva, dgate, dlat, dh, comm=None):
    tm = 256

    def body(h_ref, g_ref, w_ref, dqa_ref, dkva_ref, dg_ref, dl_ref, dh_ref, o_ref, dw_ref, dgn_ref):
        @pl.when(pl.program_id(0) == 0)
        def _():
            dw_ref[...] = jnp.zeros_like(dw_ref)
            dgn_ref[...] = jnp.zeros_like(dgn_ref)

        g = g_ref[...]
        xh, r = _rms(h_ref[...])
        dproj = jnp.concatenate([dqa_ref[...], dkva_ref[...], dg_ref[...], dl_ref[...]], axis=1)
        dw_ref[...] += _dot_tn((xh * g).astype(BF16), dproj)
        du = _dot_nt(dproj, w_ref[...])
        dgn_ref[...] += jnp.sum(du * xh, axis=0, keepdims=True)
        o_ref[...] = dh_ref[...] + _rms_bwd(xh, r, du * g)

    rows = lambda w: pl.BlockSpec((tm, w), lambda i: (i, 0))
    return _call_with_comm(
        body, comm, grid=(cfg.TP // tm,), name="inproj_bwd",
        in_specs=[rows(D_MODEL), pl.BlockSpec((1, D_MODEL), lambda i: (0, 0)), pl.BlockSpec((D_MODEL, W_IN_P), lambda i: (0, 0)),
                  rows(512), rows(256), rows(1024), rows(512), rows(D_MODEL)],
        out_specs=[rows(D_MODEL), pl.BlockSpec((D_MODEL, W_IN_P), lambda i: (0, 0)), pl.BlockSpec((1, D_MODEL), lambda i: (0, 0))],
        out_shape=[jax.ShapeDtypeStruct((cfg.TP, D_MODEL), F32), jax.ShapeDtypeStruct((D_MODEL, W_IN_P), F32),
                   jax.ShapeDtypeStruct((1, D_MODEL), F32)],
        args=(h, g, w_p, dqa, dkva, dgate, dlat, dh))


def _meta_grad(cfg, dh):
    def body(d_ref, o_ref):
        @pl.when(pl.program_id(0) == 0)
        def _():
            o_ref[...] = d_ref[...]

        @pl.when(pl.program_id(0) > 0)
        def _():
            o_ref[...] += d_ref[...]

    return pl.pallas_call(
        body, grid=(cfg.B,), name="meta_grad",
        in_specs=[pl.BlockSpec((BLK, D_MODEL), lambda b: (b * cfg.NJ, 0))],
        out_specs=pl.BlockSpec((BLK, D_MODEL), lambda b: (0, 0)),
        out_shape=jax.ShapeDtypeStruct((BLK, D_MODEL), F32),
        compiler_params=_cp("arbitrary"),
    )(dh)


MATRICES = ("w_in", "w_uq", "w_ukv", "w_out")


def _local_grads(cfg, x, target, meta, table, small, weight_of, rider=None):
    def ride(stage, i, mats):
        hook = rider(stage, i, mats) if rider else None
        return hook if hook else (None, lambda res: None)

    depth = small["norm_in"].shape[0]
    rel, vis = _window_structure(cfg.NJ)
    bucket = _t5_bucket(jnp.asarray(rel)).reshape(4, 1, N_PAIRS)
    maskadd = jnp.asarray(np.where(vis, 0.0, NEG).astype(np.float32).reshape(4, 1, N_PAIRS))
    c_tab, s_tab = _rope_tables(cfg)
    bias = _bias_build(table.T, bucket, maskadd).reshape(4, A_HEADS, BLK, 512)

    meta_blk = jnp.concatenate([meta, jnp.zeros((BLK - N_META, D_MODEL), F32)], axis=0)
    h = jnp.concatenate([jnp.broadcast_to(meta_blk[None], (cfg.B, BLK, D_MODEL)), x], axis=1).reshape(cfg.TP, D_MODEL)

    wp, saved = [], []
    for i in range(depth):
        w = dict(w_in=_w_in_to_p(weight_of(i, "w_in")),
                 g_in=small["norm_in"][i][None], gq=small["norm_q_lat"][i][None], gkv=small["norm_kv_lat"][i][None],
                 goa=_perm_heads64(small["norm_out_a"][i], 0)[None], gob=small["norm_out_b"][i][None], sink=small["sink_a"][i])
        wp.append(w)
        comm, deliver = ride("inproj_fwd", i, {})
        pa, pf, *travelled = _inproj_fwd(cfg, h, w["g_in"], w["w_in"], comm)
        deliver(travelled)
        w.update(w_uq=_w_uq_to_p(weight_of(i, "w_uq")), w_ukv=_w_ukv_to_p(weight_of(i, "w_ukv")), w_out=_w_out_to_p(weight_of(i, "w_out")))
        q, k, v, kt, vt = _lat_fwd(cfg, pf, w["gq"], w["gkv"], w["w_uq"], w["w_ukv"], c_tab, s_tab)
        ya, lse_a = _win_fwd(cfg, pa, bias, w["sink"])
        comm, deliver = ride("mla_fwd", i, {})
        yb, lse_b, *travelled = _mla_fwd(cfg, q, kt, v, comm)
        deliver(travelled)
        h_next = _out_fwd(cfg, ya, yb, pf, w["goa"], w["gob"], w["w_out"], h)
        saved.append(dict(h=h, pa=pa, pf=pf, q=q, k=k, kt=kt, vt=vt, ya=ya, lse_a=lse_a, yb=yb, lse_b=lse_b))
        h = h_next

    dh, loss_tile, d_norm_final = _loss_bwd(cfg, h, target.reshape(cfg.B * cfg.S, D_MODEL), small["norm_final"][None])

    grads = {k_: [] for k_ in ("norm_in", "sink_a", "norm_q_lat", "norm_kv_lat", "norm_out_a", "norm_out_b")}
    mats, s_accs = {}, []
    for i in reversed(range(depth)):
        w, sv = wp[i], saved[i]
        dya, dyb, dgate, dwo, dgoa, dgob = _out_bwd(cfg, dh, sv["ya"], sv["yb"], sv["pf"], w["goa"], w["gob"], w["w_out"])
        dqa, dkp, dvp, dkm, dvm, s_acc, dsink = _win_bwd(cfg, sv["pa"], bias, w["sink"], dya, sv["ya"], sv["lse_a"])
        dkva = _win_dkv_combine(cfg, dkp, dvp, dkm, dvm)
        comm, deliver = ride("mla_bwd", i, mats)
        dq, dk, dv, *travelled = _mla_bwd(cfg, sv["q"], sv["k"], sv["kt"], sv["vt"], dyb, sv["yb"], sv["lse_b"], comm)
        deliver(travelled)
        dlat, dwq, dwkv, dgq, dgkv = _lat_bwd(cfg, dq, dk, dv, sv["pf"], w["gq"], w["gkv"], w["w_uq"], w["w_ukv"], c_tab, s_tab)
        mats[i] = dict(w_uq=_w_uq_from_p(dwq), w_ukv=_w_ukv_from_p(dwkv), w_out=_w_out_from_p(dwo))
        comm, deliver = ride("inproj_bwd", i, mats)
        dh, dwin, dgin, *travelled = _inproj_bwd(cfg, sv["h"], w["g_in"], w["w_in"], dqa, dkva, dgate, dlat, dh, comm)
        deliver(travelled)
        s_accs.append(s_acc)
        mats[i]["w_in"] = _w_in_from_p(dwin)
        grads["norm_in"].append(dgin[0])
        grads["sink_a"].append(dsink[0, :A_HEADS])
        grads["norm_q_lat"].append(dgq[0])
        grads["norm_kv_lat"].append(dgkv[0])
        grads["norm_out_a"].append(_unperm_heads64(dgoa[0], 0))
        grads["norm_out_b"].append(dgob[0])

    out = {k_: jnp.stack(v_[::-1]) for k_, v_ in grads.items()}
    out["rel_bias_table"] = _bias_grad([s.reshape(4, A_HEADS, N_PAIRS) for s in s_accs], bucket).T
    out["norm_final"] = d_norm_final[0]
    return loss_tile[0, 0], dh.reshape(cfg.B, cfg.LP, D_MODEL)[:, BLK:], out, mats, _meta_grad(cfg, dh)[:N_META]


MESH = pl.DeviceIdType.MESH
ANY = pl.BlockSpec(memory_space=pl.ANY)


def _place():
    x, y, c = lax.axis_index("x"), lax.axis_index("y"), lax.axis_index("c")
    others = [(1 - x, y), (x, 1 - y), (1 - x, 1 - y)]
    return x, y, c, others


Comm = collections.namedtuple("Comm", "inputs out_shapes scratch start wait")


def _gather_comm(shards):
    n = len(shards)

    def copies(ins, outs, sems, arriving):
        send_sems, recv_sems, local_sems = sems
        x, y, c, others = _place()
        k_me = 2 * x + y
        local = [pltpu.make_async_copy(ins[a], outs[a].at[k_me], local_sems.at[a]) for a in range(n)]
        remote = [pltpu.make_async_remote_copy(src_ref=ins[a], dst_ref=outs[a].at[2 * ox + oy if arriving else k_me],
                                               send_sem=send_sems.at[3 * a + j], recv_sem=recv_sems.at[3 * a + j],
                                               device_id=(ox, oy, c), device_id_type=MESH)
                  for a in range(n) for j, (ox, oy) in enumerate(others)]
        return local, remote

    def start(ins, outs, sems):
        local, sends = copies(ins, outs, sems, arriving=False)
        for cp in local + sends:
            cp.start()

    def wait(ins, outs, sems):
        local, recvs = copies(ins, outs, sems, arriving=True)
        for cp in recvs:
            cp.wait_recv()
        for cp in recvs:
            cp.wait_send()
        for cp in local:
            cp.wait()

    return Comm(list(shards), [jax.ShapeDtypeStruct((4, *s.shape), s.dtype) for s in shards],
                [pltpu.SemaphoreType.DMA((3 * n,)), pltpu.SemaphoreType.DMA((3 * n,)), pltpu.SemaphoreType.DMA((n,))], start, wait)


def _scatter_comm(parts):
    n = len(parts)

    def copies(ins, outs, sems):
        send_sems, recv_sems = sems
        x, y, c, others = _place()
        return [pltpu.make_async_remote_copy(src_ref=ins[a].at[2 * ox + oy], dst_ref=outs[a].at[j], send_sem=send_sems.at[3 * a + j],
                                             recv_sem=recv_sems.at[3 * a + j], device_id=(ox, oy, c), device_id_type=MESH)
                for a in range(n) for j, (ox, oy) in enumerate(others)]

    def start(ins, outs, sems):
        for cp in copies(ins, outs, sems):
            cp.start()

    def wait(ins, outs, sems):
        cps = copies(ins, outs, sems)
        for cp in cps:
            cp.wait_recv()
        for cp in cps:
            cp.wait_send()

    return Comm(list(parts), [jax.ShapeDtypeStruct((3, *p.shape[1:]), p.dtype) for p in parts],
                [pltpu.SemaphoreType.DMA((3 * n,)), pltpu.SemaphoreType.DMA((3 * n,))], start, wait)


def _run_comm(comm, name):
    ni, no = len(comm.inputs), len(comm.out_shapes)

    def body(*refs):
        ins, outs, sems = refs[:ni], refs[ni:ni + no], refs[ni + no:]
        comm.start(ins, outs, sems)
        comm.wait(ins, outs, sems)

    return pl.pallas_call(body, name=name, in_specs=[ANY] * ni, out_specs=[ANY] * no, out_shape=comm.out_shapes,
                          scratch_shapes=comm.scratch)(*comm.inputs)


def _call_with_comm(body, comm, *, grid, name, in_specs, out_specs, out_shape, args, scratch_shapes=()):
    if comm is None:
        return pl.pallas_call(body, grid=grid, name=name, in_specs=in_specs, out_specs=out_specs, out_shape=out_shape,
                              scratch_shapes=list(scratch_shapes), compiler_params=_cp(*["arbitrary"] * len(grid)))(*args)
    n_in, n_out, ci, co, ns = len(in_specs), len(out_specs), len(comm.inputs), len(comm.out_shapes), len(scratch_shapes)

    def wrapped(*refs):
        ins, cins = refs[:n_in], refs[n_in:n_in + ci]
        outs, couts = refs[n_in + ci:n_in + ci + n_out], refs[n_in + ci + n_out:n_in + ci + n_out + co]
        scratch, sems = refs[n_in + ci + n_out + co:n_in + ci + n_out + co + ns], refs[n_in + ci + n_out + co + ns:]
        ids = [pl.program_id(a) for a in range(len(grid))]
        first = functools.reduce(jnp.logical_and, [i == 0 for i in ids])
        last = functools.reduce(jnp.logical_and, [i == g - 1 for i, g in zip(ids, grid)])

        @pl.when(first)
        def _():
            comm.start(cins, couts, sems)

        body(*ins, *outs, *scratch)

        @pl.when(last)
        def _():
            comm.wait(cins, couts, sems)

    return pl.pallas_call(
        wrapped, grid=grid, name=name + "_comm", in_specs=[*in_specs, *[ANY] * ci], out_specs=[*out_specs, *[ANY] * co],
        out_shape=[*out_shape, *comm.out_shapes], scratch_shapes=[*scratch_shapes, *comm.scratch],
        compiler_params=_cp(*["arbitrary"] * len(grid)))(*args, *comm.inputs)


def _swap_sibling(arrs):
    n = len(arrs)

    def body(*refs):
        ins, outs = refs[:n], refs[n:2 * n]
        send_sems, recv_sems = refs[2 * n:]
        x, y, c, _ = _place()
        copies = [pltpu.make_async_remote_copy(src_ref=ins[a], dst_ref=outs[a], send_sem=send_sems.at[a], recv_sem=recv_sems.at[a],
                                               device_id=(x, y, 1 - c), device_id_type=MESH) for a in range(n)]
        for cp in copies:
            cp.start()
        for cp in copies:
            cp.wait_recv()
        for cp in copies:
            cp.wait_send()

    return pl.pallas_call(
        body, name="swap_sibling", in_specs=[ANY] * n, out_specs=[ANY] * n,
        out_shape=[jax.ShapeDtypeStruct(a.shape, a.dtype) for a in arrs],
        scratch_shapes=[pltpu.SemaphoreType.DMA((n,)), pltpu.SemaphoreType.DMA((n,))],
    )(*arrs)


def _allreduce_small(v):
    def body(v_ref, o_ref, buf, send_sems, recv_sems):
        x, y, c, _ = _place()
        me = 4 * x + 2 * y + c
        buf[me] = v_ref[...]

        def copy(r):
            tx, ty, tc = (x + (r >> 2)) % 2, (y + ((r >> 1) & 1)) % 2, (c + (r & 1)) % 2
            return tx, ty, tc

        sends = []
        for r in range(1, 8):
            tx, ty, tc = copy(r)
            sends.append(pltpu.make_async_remote_copy(src_ref=v_ref, dst_ref=buf.at[me], send_sem=send_sems.at[r - 1],
                                                      recv_sem=recv_sems.at[r - 1], device_id=(tx, ty, tc), device_id_type=MESH))
        for cp in sends:
            cp.start()
        for r in range(1, 8):
            tx, ty, tc = copy(r)
            pltpu.make_async_remote_copy(src_ref=v_ref, dst_ref=buf.at[4 * tx + 2 * ty + tc], send_sem=send_sems.at[r - 1],
                                         recv_sem=recv_sems.at[r - 1], device_id=(tx, ty, tc), device_id_type=MESH).wait_recv()
        for cp in sends:
            cp.wait_send()
        acc = buf[0]
        for d in range(1, 8):
            acc = acc + buf[d]
        o_ref[...] = acc

    return pl.pallas_call(
        body, name="allreduce_small", in_specs=[pl.BlockSpec(memory_space=pltpu.VMEM)], out_specs=pl.BlockSpec(memory_space=pltpu.VMEM),
        out_shape=jax.ShapeDtypeStruct(v.shape, F32),
        scratch_shapes=[pltpu.VMEM((8, *v.shape), F32), pltpu.SemaphoreType.DMA((7,)), pltpu.SemaphoreType.DMA((7,))],
    )(v)


def _rows_view(a):
    return a.reshape(-1, a.shape[-1])


def _elementwise(name, fn, ins, n_out):
    rows, cols = ins[0].shape
    tm = min(rows, 256)
    spec = pl.BlockSpec((tm, cols), lambda i: (i, 0))

    def body(*refs):
        outs = fn(*[r[...] for r in refs[:len(ins)]])
        for o_ref, o in zip(refs[len(ins):], outs):
            o_ref[...] = o

    return pl.pallas_call(
        body, grid=(rows // tm,), name=name, in_specs=[spec] * len(ins), out_specs=[spec] * n_out,
        out_shape=[jax.ShapeDtypeStruct((rows, cols), F32)] * n_out, compiler_params=_cp("parallel"),
    )(*ins)


def _sum_parts(name, own, recv):
    def fn(o, r0, r1, r2):
        return (o + r0.astype(F32) + r1.astype(F32) + r2.astype(F32),)

    return _elementwise("sum_parts_" + name, fn, [own, recv[0], recv[1], recv[2]], 1)[0]


def _adamw(name, w, m, v, g_parts):
    def fn(w_, m_, v_, *gs):
        g = gs[0]
        for extra in gs[1:]:
            g = g + extra
        m_new = ADAM_B1 * m_ + (1.0 - ADAM_B1) * g
        v_new = ADAM_B2 * v_ + (1.0 - ADAM_B2) * (g * g)
        m_hat = m_new / (1.0 - ADAM_B1 ** ADAM_STEP)
        v_hat = v_new / (1.0 - ADAM_B2 ** ADAM_STEP)
        delta = -ADAM_LR * (m_hat / (jnp.sqrt(v_hat) + ADAM_EPS) + ADAM_WD * w_)
        return g, delta, m_new, v_new

    return _elementwise("adamw_" + name, fn, [w, m, v, *g_parts], 4)


MAT_AXIS = {"w_in": 1, "w_uq": 1, "w_ukv": 1, "w_out": 0}
SMALL = ("rel_bias_table", "norm_in", "sink_a", "norm_q_lat", "norm_kv_lat", "norm_out_a", "norm_out_b", "norm_final")
WEIGHTS = ("meta_tokens", "rel_bias_table", "norm_in", "w_in", "sink_a", "norm_q_lat", "w_uq", "norm_kv_lat", "w_ukv",
           "norm_out_a", "norm_out_b", "w_out", "norm_final")
SMALL_ROWS, SMALL_COLS = 8, 1024


def _pack_small(d, loss=None):
    flat = [d[n].reshape(-1) for n in SMALL]
    if loss is not None:
        flat.append(loss.reshape(1))
    used = sum(f.shape[0] for f in flat)
    flat.append(jnp.zeros((SMALL_ROWS * SMALL_COLS - used,), F32))
    return jnp.concatenate(flat).reshape(SMALL_ROWS, SMALL_COLS)


def _unpack_small(p, like):
    flat, out, off = p.reshape(-1), {}, 0
    for n in SMALL:
        size = int(np.prod(like[n].shape))
        out[n] = flat[off:off + size].reshape(like[n].shape)
        off += size
    return out, flat[off]


def _split4(a, axis):
    size = a.shape[axis] // 4
    return jnp.stack([lax.slice_in_dim(a, k * size, (k + 1) * size, axis=axis) for k in range(4)])


def _train_step(cfg, x, target, w, m, v):
    depth = w["w_in"].shape[0]
    rest = tuple(n for n in MATRICES if n != "w_in")
    weights, splits, received = {}, {}, {}

    def gather(i, names, also=()):
        def deliver(res):
            for n, g in zip(names, res):
                weights[i, n] = jnp.concatenate([g[k] for k in range(4)], axis=MAT_AXIS[n])

        return _gather_comm([w[n][i].astype(BF16) for n in names] + list(also)), deliver

    def scatter(i, names, mats, also=()):
        for n in names:
            splits[i, n] = _split4(mats[i][n], MAT_AXIS[n])

        def deliver(res):
            for n, r in zip(names, res):
                received[i, n] = r

        return _scatter_comm([splits[i, n].astype(BF16) for n in names] + list(also)), deliver

    def rider(stage, i, mats):
        if stage == "inproj_fwd" and i == 0:
            return gather(0, rest)
        if stage == "mla_fwd" and i + 1 < depth:
            return gather(i + 1, MATRICES)
        if stage == "mla_bwd" and i + 1 < depth:
            return scatter(i + 1, MATRICES, mats)
        if stage == "inproj_bwd" and i == 0:
            return scatter(0, rest, mats)
        return None

    comm, deliver = gather(0, ("w_in",), also=[w["meta_tokens"]])
    first = _run_comm(comm, "gather_first")
    deliver(first)
    meta = jnp.concatenate([first[1][k] for k in range(4)], axis=1)

    loss_local, grad_x, g, mats, g_meta = _local_grads(cfg, x, target, meta, w["rel_bias_table"], {n: w[n] for n in SMALL},
                                                      lambda i, n: weights[i, n], rider)
    meta_split = _split4(g_meta, 1)
    comm, deliver = scatter(0, ("w_in",), mats, also=[meta_split.astype(BF16)])
    last = _run_comm(comm, "scatter_last")
    deliver(last)

    small_sum = _allreduce_small(_pack_small(g, loss_local))
    g_small, loss = _unpack_small(small_sum, {n: w[n] for n in SMALL})

    k_me = 2 * lax.axis_index("x") + lax.axis_index("y")

    def core_sum(name, split, recv):
        own = lax.dynamic_index_in_dim(split, k_me, 0, keepdims=False)
        return _sum_parts(name, _rows_view(own), recv.reshape(3, -1, recv.shape[-1]))

    partial = [core_sum("meta_tokens", meta_split, last[1])]
    for n in MATRICES:
        partial.append(jnp.concatenate([core_sum(f"{n}_{i}", splits[i, n], received[i, n]) for i in range(depth)], axis=0))
    sibling = _swap_sibling(partial)

    outs = {}
    for n, p_me, p_sib in zip(("meta_tokens", *MATRICES), partial, sibling):
        res = _adamw(n, _rows_view(w[n]), _rows_view(m[n]), _rows_view(v[n]), [p_me, p_sib])
        outs[n] = [r.reshape(w[n].shape) for r in res]
    res = _adamw("small", _pack_small(w), _pack_small(m), _pack_small(v), [_pack_small(g_small)])
    unpacked = [_unpack_small(r, {n: w[n] for n in SMALL})[0] for r in res]
    for n in SMALL:
        outs[n] = [u[n] for u in unpacked]

    result = [loss, grad_x]
    for field in range(4):
        result.extend(outs[n][field] for n in WEIGHTS)
    return tuple(result)


def kernel(x, meta_tokens, rel_bias_table, norm_in, w_in, sink_a, norm_q_lat, w_uq, norm_kv_lat, w_ukv, norm_out_a, norm_out_b, w_out, norm_final, loss_target, m_meta_tokens, m_rel_bias_table, m_norm_in, m_w_in, m_sink_a, m_norm_q_lat, m_w_uq, m_norm_kv_lat, m_w_ukv, m_norm_out_a, m_norm_out_b, m_w_out, m_norm_final, v_meta_tokens, v_rel_bias_table, v_norm_in, v_w_in, v_sink_a, v_norm_q_lat, v_w_uq, v_norm_kv_lat, v_w_ukv, v_norm_out_a, v_norm_out_b, v_w_out, v_norm_final):
    w = dict(zip(WEIGHTS, (meta_tokens, rel_bias_table, norm_in, w_in, sink_a, norm_q_lat, w_uq, norm_kv_lat, w_ukv, norm_out_a, norm_out_b, w_out, norm_final)))
    m = dict(zip(WEIGHTS, (m_meta_tokens, m_rel_bias_table, m_norm_in, m_w_in, m_sink_a, m_norm_q_lat, m_w_uq, m_norm_kv_lat, m_w_ukv, m_norm_out_a, m_norm_out_b, m_w_out, m_norm_final)))
    v = dict(zip(WEIGHTS, (v_meta_tokens, v_rel_bias_table, v_norm_in, v_w_in, v_sink_a, v_norm_q_lat, v_w_uq, v_norm_kv_lat, v_w_ukv, v_norm_out_a, v_norm_out_b, v_w_out, v_norm_final)))
    cfg = make_cfg(x.shape[0], x.shape[1])
    return _train_step(cfg, x, loss_target, w, m, v)
```

```python
import collections
import functools
import math

import jax
import jax.numpy as jnp
import numpy as np
from jax import lax
from jax.experimental import pallas as pl
from jax.experimental.pallas import tpu as pltpu

F32 = jnp.float32
BF16 = jnp.bfloat16

BLK = 128
N_META = 16
D_MODEL = 1024
A_HEADS, A_KV, A_DH = 8, 2, 64
B_HEADS, B_NOPE, B_ROPE, B_DV = 8, 64, 32, 64
Q_RANK, KV_RANK = 256, 128
N_BUCKETS, MAX_DIST = 32, 128
ROPE_THETA = 10000.0
EPS = 1e-6
IN_WIDTH = 2208
W_IN_P = 2304
NEG = -1e30
MASK_LANE = 96
Q_SCALE = (B_NOPE + B_ROPE) ** -0.5 * math.log2(math.e)
LN2 = math.log(2.0)
VMEM_LIMIT = 48 * 1024 * 1024

ADAM_LR, ADAM_B1, ADAM_B2, ADAM_EPS, ADAM_WD, ADAM_STEP = 0.001, 0.9, 0.999, 1e-08, 0.01, 10

Cfg = collections.namedtuple("Cfg", "B S NB NJ LP TP")


def make_cfg(batch, seq):
    nb = seq // BLK
    nj = nb + 1
    return Cfg(batch, seq, nb, nj, nj * BLK, batch * nj * BLK)


def _cp(*sem):
    return pltpu.CompilerParams(dimension_semantics=sem, vmem_limit_bytes=VMEM_LIMIT)


def _pallas(body, *, out_shape, **kw):
    pinned = jax.tree.map(lambda s: pltpu.HBM(s.shape, s.dtype), out_shape)
    call = pl.pallas_call(body, out_shape=pinned, **kw)
    return lambda *args: call(*[pltpu.with_memory_space_constraint(a, pltpu.HBM) for a in args])


def _dot(a, b):
    return jnp.dot(a, b, preferred_element_type=F32)


def _dot_nt(a, b):
    return lax.dot_general(a, b, (((1,), (1,)), ((), ())), preferred_element_type=F32)


def _dot_tn(a, b):
    return lax.dot_general(a, b, (((0,), (0,)), ((), ())), preferred_element_type=F32)


def _rms(x, width=None):
    n = x.shape[-1] if width is None else width
    r = lax.rsqrt(jnp.sum(x * x, axis=-1, keepdims=True) * (1.0 / n) + EPS)
    return x * r, r


def _rms_bwd(xhat, r, t):
    n = xhat.shape[-1]
    return r * (t - xhat * (jnp.sum(t * xhat, axis=-1, keepdims=True) * (1.0 / n)))


def _sigmoid(x):
    return 1.0 / (1.0 + jnp.exp(-x))


def _lane(shape):
    return lax.broadcasted_iota(jnp.int32, shape, len(shape) - 1)


def _swap_rope(x):
    n = x.shape[-1]
    lane = _lane(x.shape) % BLK
    up = pltpu.roll(x, n - 16, axis=x.ndim - 1)
    dn = pltpu.roll(x, 16, axis=x.ndim - 1)
    return jnp.where((lane >= 64) & (lane < 80), up, jnp.where((lane >= 80) & (lane < 96), dn, 0.0))


A_ORDER = (0, 4, 1, 5, 2, 6, 3, 7)


def _jtype(j, nj):
    return 0 if j == 0 else 1 if j == 1 else 3 if j == nj - 1 else 2


def _window_structure(nj):
    def pos(blk, r):
        return np.where(blk == 0, r, N_META + (blk - 1) * BLK + r)

    def valid(blk, r):
        return np.where(blk == 0, r < N_META, True)

    r = np.arange(BLK)
    rels, viss = [], []
    for j in (0, 1, 2, nj - 1):
        qpos = pos(j, r)[:, None]
        rel_t, vis_t = [], []
        for s, kb in enumerate((0, j - 1, j, j + 1)):
            slot_ok = (s == 0) or (1 <= kb <= nj - 1)
            kbc = min(max(kb, 0), nj - 1)
            kpos = pos(kbc, r)[None, :]
            rel = kpos - qpos
            v = valid(kbc, r)[None, :] & np.ones((BLK, 1), bool)
            if s > 0:
                v = v & (np.abs(rel) <= BLK)
            rel_t.append(rel)
            vis_t.append(v & slot_ok)
        rels.append(np.concatenate(rel_t, axis=1))
        viss.append(np.concatenate(vis_t, axis=1))
    return np.stack(rels).astype(np.int32), np.stack(viss)


def _t5_bucket(rel):
    nb = N_BUCKETS // 2
    max_exact = nb // 2
    ret = jnp.where(rel > 0, nb, 0)
    n = jnp.abs(rel)
    nf = jnp.maximum(n, 1).astype(jnp.float32)
    large = max_exact + (jnp.log(nf / max_exact) / math.log(MAX_DIST / max_exact) * (nb - max_exact)).astype(jnp.int32)
    large = jnp.minimum(large, nb - 1)
    return ret + jnp.where(n < max_exact, n, large)


def _perm_heads64(a, axis):
    parts = [lax.slice_in_dim(a, h * 64, (h + 1) * 64, axis=axis) for h in A_ORDER]
    return jnp.concatenate(parts, axis=axis)


def _unperm_heads64(a, axis):
    inv = [A_ORDER.index(h) for h in range(8)]
    parts = [lax.slice_in_dim(a, p * 64, (p + 1) * 64, axis=axis) for p in inv]
    return jnp.concatenate(parts, axis=axis)


def _w_in_to_p(w):
    sl = lambda a, b: lax.slice_in_dim(w, a, b, axis=1)
    z = lambda n: jnp.zeros((w.shape[0], n), w.dtype)
    return jnp.concatenate([_perm_heads64(sl(0, 512), 1), sl(512, 768), _perm_heads64(sl(768, 1280), 1), sl(1696, 2208),
                            sl(1280, 1536), sl(1536, 1664), z(64), sl(1664, 1696), z(32)], axis=1)


def _w_in_from_p(g):
    sl = lambda a, b: lax.slice_in_dim(g, a, b, axis=1)
    return jnp.concatenate([_unperm_heads64(sl(0, 512), 1), sl(512, 768), _unperm_heads64(sl(768, 1280), 1),
                            sl(1792, 2048), sl(2048, 2176), sl(2240, 2272), sl(1280, 1792)], axis=1)


def _w_uq_to_p(w):
    z = jnp.zeros((w.shape[0], 32), w.dtype)
    return jnp.concatenate([p for h in range(8) for p in (lax.slice_in_dim(w, h * 96, (h + 1) * 96, axis=1), z)], axis=1)


def _w_uq_from_p(g):
    return jnp.concatenate([lax.slice_in_dim(g, h * 128, h * 128 + 96, axis=1) for h in range(8)], axis=1)


def _w_ukv_to_p(w):
    z = jnp.zeros((w.shape[0], 64), w.dtype)
    ks = [p for h in range(8) for p in (lax.slice_in_dim(w, h * 128, h * 128 + 64, axis=1), z)]
    vs = [lax.slice_in_dim(w, h * 128 + 64, (h + 1) * 128, axis=1) for h in range(8)]
    return jnp.concatenate(ks + vs, axis=1)


def _w_ukv_from_p(g):
    parts = []
    for h in range(8):
        parts.append(lax.slice_in_dim(g, h * 128, h * 128 + 64, axis=1))
        parts.append(lax.slice_in_dim(g, 1024 + h * 64, 1024 + (h + 1) * 64, axis=1))
    return jnp.concatenate(parts, axis=1)


def _w_out_to_p(w):
    return jnp.concatenate([_perm_heads64(lax.slice_in_dim(w, 0, 512, axis=0), 0), lax.slice_in_dim(w, 512, 1024, axis=0)], axis=0)


def _w_out_from_p(g):
    return jnp.concatenate([_unperm_heads64(lax.slice_in_dim(g, 0, 512, axis=0), 0), lax.slice_in_dim(g, 512, 1024, axis=0)], axis=0)


def _rope_tables(cfg):
    half = B_ROPE // 2
    length = N_META + cfg.S
    freqs = ROPE_THETA ** (-jnp.arange(half, dtype=jnp.float32) / half)
    ang = jnp.arange(length, dtype=jnp.float32)[:, None] * freqs[None, :]
    cos, sin = jnp.cos(ang), jnp.sin(ang)

    def rows(t):
        return jnp.concatenate([t[:N_META], jnp.zeros((BLK - N_META, t.shape[1]), t.dtype), t[N_META:]], axis=0)

    ones = jnp.ones((length, 64), F32)
    zer = jnp.zeros((length, 32), F32)
    c_tab = rows(jnp.concatenate([ones, cos, cos, zer], axis=1))
    s_tab = rows(jnp.concatenate([zer, zer, -sin, sin, zer], axis=1))
    return c_tab, s_tab


def _inproj_fwd(cfg, h, g, w_p, comm=None):
    tm = 256

    def body(h_ref, g_ref, w_ref, pa_ref, pf_ref):
        xh, _ = _rms(h_ref[...])
        u = (xh * g_ref[...]).astype(BF16)
        acc = _dot(u, w_ref[...])
        pa_ref[...] = acc[:, :768].astype(BF16)
        pf_ref[...] = acc[:, 768:]

    return _call_with_comm(
        body, comm, grid=(cfg.TP // tm,), name="inproj_fwd",
        in_specs=[pl.BlockSpec((tm, D_MODEL), lambda i: (i, 0)), pl.BlockSpec((1, D_MODEL), lambda i: (0, 0)),
                  pl.BlockSpec((D_MODEL, W_IN_P), lambda i: (0, 0))],
        out_specs=[pl.BlockSpec((tm, 768), lambda i: (i, 0)), pl.BlockSpec((tm, 1536), lambda i: (i, 0))],
        out_shape=[jax.ShapeDtypeStruct((cfg.TP, 768), BF16), jax.ShapeDtypeStruct((cfg.TP, 1536), F32)],
        args=(h, g, w_p))


def _lat_fwd(cfg, pf, gq, gkv, wq_p, wkv_p, c_tab, s_tab):
    nj = cfg.NJ

    def body(cq_ref, ckv_ref, kr_ref, gq_ref, gkv_ref, wq_ref, wkv_ref, c_ref, s_ref, q_ref, k_ref, v_ref, kt_ref, vt_ref):
        c1, s1 = c_ref[...], s_ref[...]
        c8, s8 = jnp.tile(c1, (1, 8)), jnp.tile(s1, (1, 8))
        mask_lane = _lane((BLK, 1024)) % BLK == MASK_LANE
        zero_row = (pl.program_id(1) == 0) & (lax.broadcasted_iota(jnp.int32, (BLK, 1024), 0) >= N_META)
        xq, _ = _rms(cq_ref[...])
        qp = _dot((xq * gq_ref[...]).astype(BF16), wq_ref[...])
        q_ref[...] = jnp.where(mask_lane, 1.0, (qp * c8 + _swap_rope(qp) * s8) * Q_SCALE).astype(BF16)
        xk, _ = _rms(ckv_ref[...])
        kvp = _dot((xk * gkv_ref[...]).astype(BF16), wkv_ref[...])
        kr = kr_ref[...]
        krr = kr * c1 + _swap_rope(kr) * s1
        k = jnp.where(mask_lane & zero_row, NEG, kvp[:, :1024] + jnp.tile(krr, (1, 8)))
        k_ref[...] = k.astype(BF16)
        v_ref[...] = kvp[:, 1024:].astype(BF16)
        kt_ref[...] = k.T.astype(BF16)
        vt_ref[...] = kvp[:, 1024:].T.astype(BF16)

    row = lambda b, j: b * nj + j
    return _pallas(
        body, grid=(cfg.B, nj), name="lat_fwd",
        in_specs=[pl.BlockSpec((BLK, 256), lambda b, j: (row(b, j), 4)), pl.BlockSpec((BLK, 128), lambda b, j: (row(b, j), 10)),
                  pl.BlockSpec((BLK, 128), lambda b, j: (row(b, j), 11)),
                  pl.BlockSpec((1, 256), lambda b, j: (0, 0)), pl.BlockSpec((1, 128), lambda b, j: (0, 0)),
                  pl.BlockSpec((256, 1024), lambda b, j: (0, 0)), pl.BlockSpec((128, 1536), lambda b, j: (0, 0)),
                  pl.BlockSpec((BLK, 128), lambda b, j: (j, 0)), pl.BlockSpec((BLK, 128), lambda b, j: (j, 0))],
        out_specs=[pl.BlockSpec((BLK, 1024), lambda b, j: (row(b, j), 0)), pl.BlockSpec((BLK, 1024), lambda b, j: (row(b, j), 0)),
                   pl.BlockSpec((BLK, 512), lambda b, j: (row(b, j), 0)),
                   pl.BlockSpec((1024, BLK), lambda b, j: (b, j)), pl.BlockSpec((512, BLK), lambda b, j: (b, j))],
        out_shape=[jax.ShapeDtypeStruct((cfg.TP, 1024), BF16), jax.ShapeDtypeStruct((cfg.TP, 1024), BF16),
                   jax.ShapeDtypeStruct((cfg.TP, 512), BF16),
                   jax.ShapeDtypeStruct((cfg.B * 1024, cfg.LP), BF16), jax.ShapeDtypeStruct((cfg.B * 512, cfg.LP), BF16)],
        compiler_params=_cp("parallel", "parallel"),
    )(pf, pf, pf, gq, gkv, wq_p, wkv_p, c_tab, s_tab)


def _gate_halves(ya, yb, ga, gb, goa, gob):
    xa, ra = _rms(ya)
    xb, rb = _rms(yb)
    sga, sgb = _sigmoid(ga), _sigmoid(gb)
    return xa, ra, xb, rb, sga, sgb, xa * goa * (ga * sga), xb * gob * (gb * sgb)


def _out_fwd(cfg, ya, yb, pf, goa, gob, wo_p, h):
    tm = 256

    def body(ya_ref, yb_ref, ga_ref, gb_ref, goa_ref, gob_ref, w_ref, h_ref, o_ref):
        *_, y_a, y_b = _gate_halves(ya_ref[...], yb_ref[...], ga_ref[...], gb_ref[...], goa_ref[...], gob_ref[...])
        y = jnp.concatenate([y_a, y_b], axis=1).astype(BF16)
        o_ref[...] = h_ref[...] + _dot(y, w_ref[...])

    return _pallas(
        body, grid=(cfg.TP // tm,), name="out_fwd",
        in_specs=[pl.BlockSpec((tm, 512), lambda i: (i, 0)), pl.BlockSpec((tm, 512), lambda i: (i, 0)),
                  pl.BlockSpec((tm, 512), lambda i: (i, 0)), pl.BlockSpec((tm, 512), lambda i: (i, 1)),
                  pl.BlockSpec((1, 512), lambda i: (0, 0)), pl.BlockSpec((1, 512), lambda i: (0, 0)),
                  pl.BlockSpec((D_MODEL, D_MODEL), lambda i: (0, 0)), pl.BlockSpec((tm, D_MODEL), lambda i: (i, 0))],
        out_specs=pl.BlockSpec((tm, D_MODEL), lambda i: (i, 0)),
        out_shape=jax.ShapeDtypeStruct((cfg.TP, D_MODEL), F32),
        compiler_params=_cp("parallel"),
    )(ya, yb, pf, pf, goa, gob, wo_p, h)


def _bias_build(table, bucket, maskadd):
    def body(tab_ref, bk_ref, ma_ref, o_ref):
        def rows(g, carry):
            r = pl.ds(pl.multiple_of(g * 8, 8), 8)
            bk = bk_ref[0, r, :]
            accs = [jnp.zeros(bk.shape, F32)] * A_HEADS
            for b in range(N_BUCKETS):
                hit = bk == b
                accs = [jnp.where(hit, tab_ref[b, h], accs[h]) for h in range(A_HEADS)]
            ma = ma_ref[0, r, :]
            for h in range(A_HEADS):
                o_ref[0, h, r, :] = accs[h] + ma
            return carry

        lax.fori_loop(0, BLK // 8, rows, 0)

    return _pallas(
        body, grid=(4,), name="bias_build",
        in_specs=[pl.BlockSpec(memory_space=pltpu.SMEM), pl.BlockSpec((1, BLK, 512), lambda t: (t, 0, 0)),
                  pl.BlockSpec((1, BLK, 512), lambda t: (t, 0, 0))],
        out_specs=pl.BlockSpec((1, A_HEADS, BLK, 512), lambda t: (t, 0, 0, 0)),
        out_shape=jax.ShapeDtypeStruct((4, A_HEADS, BLK, 512), F32),
        compiler_params=_cp("parallel"),
    )(table, bucket, maskadd)


def _bias_grad(s_accs, bucket):
    depth = len(s_accs)

    def body(*refs):
        s_refs, bk_ref, o_ref, sum_ref, part_ref = refs[:depth], refs[depth], refs[depth + 1], refs[depth + 2], refs[depth + 3]
        t = pl.program_id(0)

        @pl.when(t == 0)
        def _():
            o_ref[...] = jnp.zeros_like(o_ref)

        total = s_refs[0][0]
        for extra in s_refs[1:]:
            total = total + extra[0]
        sum_ref[...] = total

        def step(b, carry):
            accs = [jnp.zeros((8, 512), F32) for _ in range(A_HEADS)]
            for g in range(BLK // 8):
                rows = pl.ds(g * 8, 8)
                hit = bk_ref[0, rows, :] == b
                for h in range(A_HEADS):
                    accs[h] = accs[h] + jnp.where(hit, sum_ref[h, rows, :], 0.0)
            rows8 = jnp.concatenate([jnp.sum(a, axis=0, keepdims=True) for a in accs], axis=0)
            part_ref[pl.ds(pl.multiple_of(b * A_HEADS, 8), A_HEADS), :] = rows8
            return carry

        lax.fori_loop(0, N_BUCKETS, step, 0)
        o_ref[...] += jnp.broadcast_to(jnp.sum(part_ref[...], axis=1, keepdims=True), o_ref.shape)

    s_spec = pl.BlockSpec((1, A_HEADS, BLK, 512), lambda t: (t, 0, 0, 0))
    return _pallas(
        body, grid=(4,), name="bias_grad",
        in_specs=[s_spec] * depth + [pl.BlockSpec((1, BLK, 512), lambda t: (t, 0, 0))],
        out_specs=pl.BlockSpec((N_BUCKETS * A_HEADS, 128), lambda t: (0, 0)),
        out_shape=jax.ShapeDtypeStruct((N_BUCKETS * A_HEADS, 128), F32),
        scratch_shapes=[pltpu.VMEM((A_HEADS, BLK, 512), F32), pltpu.VMEM((N_BUCKETS * A_HEADS, 512), F32)],
        compiler_params=_cp("arbitrary"),
    )(*s_accs, bucket)


def _win_specs(cfg):
    nj = cfg.NJ
    row = lambda b, j: b * nj + j
    jt = lambda j: jnp.where(j == 0, 0, jnp.where(j == 1, 1, jnp.where(j == nj - 1, 3, 2)))
    slot_rows = [lambda b, j: row(b, 0), lambda b, j: row(b, jnp.maximum(j - 1, 0)), lambda b, j: row(b, j),
                 lambda b, j: row(b, jnp.minimum(j + 1, nj - 1))]
    k_specs = [pl.BlockSpec((BLK, 128), functools.partial(lambda b, j, f: (f(b, j), 4), f=f)) for f in slot_rows]
    v_specs = [pl.BlockSpec((BLK, 128), functools.partial(lambda b, j, f: (f(b, j), 5), f=f)) for f in slot_rows]
    q_spec = pl.BlockSpec((BLK, 512), lambda b, j: (row(b, j), 0))
    bias_spec = pl.BlockSpec((1, A_HEADS, BLK, 512), lambda b, j: (jt(j), 0, 0, 0))
    return row, jt, q_spec, k_specs, v_specs, bias_spec


def _stack4(ref):
    return jnp.concatenate([ref[:, c * 128:(c + 1) * 128] for c in range(4)], axis=0)


def _win_keys(k_refs, v_refs):
    k4 = jnp.concatenate([r[...] for r in k_refs], axis=0)
    v4 = jnp.concatenate([r[...] for r in v_refs], axis=0)
    lane_k = _lane(k4.shape)
    return (jnp.where(lane_k < 64, k4, jnp.zeros_like(k4)), jnp.where(lane_k >= 64, k4, jnp.zeros_like(k4))), v4


def _sink_col(sink_ref, hf):
    rowi = lax.broadcasted_iota(jnp.int32, (4 * BLK, 1), 0)
    col = jnp.full((4 * BLK, 1), sink_ref[4 * hf + 3], F32)
    for c in (2, 1, 0):
        col = jnp.where(rowi < (c + 1) * BLK, sink_ref[4 * hf + c], col)
    return col


def _win_fwd(cfg, pa, bias, sink):
    row, jt, q_spec, k_specs, v_specs, bias_spec = _win_specs(cfg)
    scale = A_DH ** -0.5

    def body(sink_ref, q_ref, k0, k1, k2, k3, v0, v1, v2, v3, b_ref, o_ref, lse_ref):
        kk, v4 = _win_keys((k0, k1, k2, k3), (v0, v1, v2, v3))
        qs = _stack4(q_ref)
        lane_o = _lane((4 * BLK, 128))
        outs, lses = [], []
        for hf in range(2):
            s = _dot_nt(qs, kk[hf]) * scale + b_ref[0, 4 * hf:4 * hf + 4].reshape(4 * BLK, 512)
            sink_col = _sink_col(sink_ref, hf)
            m = jnp.maximum(jnp.max(s, axis=1, keepdims=True), sink_col)
            e = jnp.exp(s - m)
            den = jnp.sum(e, axis=1, keepdims=True) + jnp.exp(sink_col - m)
            outs.append(_dot(e.astype(BF16), v4) / den)
            lses.append(m + jnp.log(den))
        o = jnp.where(lane_o < 64, outs[0], outs[1])
        for c in range(4):
            o_ref[:, c * 128:(c + 1) * 128] = o[c * BLK:(c + 1) * BLK]
        lse_ref[...] = jnp.where(lane_o == 0, lses[0], jnp.where(lane_o == 1, lses[1], 0.0))

    return _pallas(
        body, grid=(cfg.B, cfg.NJ), name="win_fwd",
        in_specs=[pl.BlockSpec(memory_space=pltpu.SMEM), q_spec, *k_specs, *v_specs, bias_spec],
        out_specs=[pl.BlockSpec((BLK, 512), lambda b, j: (row(b, j), 0)), pl.BlockSpec((4 * BLK, 128), lambda b, j: (row(b, j), 0))],
        out_shape=[jax.ShapeDtypeStruct((cfg.TP, 512), F32), jax.ShapeDtypeStruct((4 * cfg.TP, 128), F32)],
        compiler_params=_cp("parallel", "parallel"),
    )(sink, pa, *([pa] * 8), bias)


def _win_bwd(cfg, pa, bias, sink, dya, ya, lse):
    row, jt, q_spec, k_specs, v_specs, bias_spec = _win_specs(cfg)
    nj = cfg.NJ
    scale = A_DH ** -0.5

    def body(sink_ref, q_ref, k0, k1, k2, k3, v0, v1, v2, v3, b_ref, dy_ref, y_ref, lse_ref,
             dq_ref, dkp_ref, dvp_ref, dkm_ref, dvm_ref, s_ref, dsink_ref):
        j = pl.program_id(1)
        kind = jt(j)

        @pl.when((pl.program_id(0) == 0) & (j == 0))
        def _():
            s_ref[...] = jnp.zeros_like(s_ref)

        kk, v4 = _win_keys((k0, k1, k2, k3), (v0, v1, v2, v3))
        qs, dys, ys = _stack4(q_ref), _stack4(dy_ref), _stack4(y_ref)
        lane_o = _lane((4 * BLK, 128))
        half = (lane_o < 64, lane_o >= 64)
        lse_blk = lse_ref[...]
        dq = jnp.zeros((4 * BLK, 128), F32)
        dk4 = jnp.zeros((512, 128), F32)
        dv4 = jnp.zeros((512, 128), F32)
        dsink = jnp.zeros((8, 128), F32)
        lane_s = _lane((8, 128))
        row_s = lax.broadcasted_iota(jnp.int32, (8, 128), 0)
        for hf in range(2):
            lse_h = jnp.sum(jnp.where(lane_o == hf, lse_blk, 0.0), axis=1, keepdims=True)
            s = _dot_nt(qs, kk[hf]) * scale + b_ref[0, 4 * hf:4 * hf + 4].reshape(4 * BLK, 512)
            p = jnp.exp(s - lse_h)
            do_h = jnp.where(half[hf], dys, 0.0)
            delta = jnp.sum(do_h * ys, axis=1, keepdims=True)
            do_b = do_h.astype(BF16)
            ds = p * (_dot_nt(do_b, v4) - delta)
            s_ref[kind, 4 * hf:4 * hf + 4] += ds.reshape(4, BLK, 512)
            sink_grad = jnp.exp(_sink_col(sink_ref, hf) - lse_h) * delta
            for c in range(4):
                tot = -jnp.sum(sink_grad[c * BLK:(c + 1) * BLK])
                dsink = jnp.where((row_s == 0) & (lane_s == 4 * hf + c), tot, dsink)
            dsb = (ds * scale).astype(BF16)
            dq = dq + _dot(dsb, kk[hf])
            dk4 = dk4 + _dot_tn(dsb, jnp.where(half[hf], qs, jnp.zeros_like(qs)))
            dv4 = dv4 + _dot_tn(p.astype(BF16), do_b)
        for c in range(4):
            dq_ref[:, c * 128:(c + 1) * 128] = dq[c * BLK:(c + 1) * BLK].astype(BF16)
        dkp_ref[0] = dk4
        dvp_ref[0] = dv4

        @pl.when(j == 0)
        def _():
            dkm_ref[...] = dk4[:BLK]
            dvm_ref[...] = dv4[:BLK]

        @pl.when(j > 0)
        def _():
            dkm_ref[...] += dk4[:BLK]
            dvm_ref[...] += dv4[:BLK]

        @pl.when((pl.program_id(0) == 0) & (j == 0))
        def _():
            dsink_ref[...] = dsink

        @pl.when((pl.program_id(0) > 0) | (j > 0))
        def _():
            dsink_ref[...] += dsink

    blk_row = pl.BlockSpec((BLK, 512), lambda b, j: (row(b, j), 0))
    return _pallas(
        body, grid=(cfg.B, nj), name="win_bwd",
        in_specs=[pl.BlockSpec(memory_space=pltpu.SMEM), q_spec, *k_specs, *v_specs, bias_spec, blk_row, blk_row,
                  pl.BlockSpec((4 * BLK, 128), lambda b, j: (row(b, j), 0))],
        out_specs=[blk_row,
                   pl.BlockSpec((1, 512, 128), lambda b, j: (row(b, j), 0, 0)), pl.BlockSpec((1, 512, 128), lambda b, j: (row(b, j), 0, 0)),
                   pl.BlockSpec((BLK, 128), lambda b, j: (b, 0)), pl.BlockSpec((BLK, 128), lambda b, j: (b, 0)),
                   pl.BlockSpec((4, A_HEADS, BLK, 512), lambda b, j: (0, 0, 0, 0)),
                   pl.BlockSpec((8, 128), lambda b, j: (0, 0))],
        out_shape=[jax.ShapeDtypeStruct((cfg.TP, 512), BF16),
                   jax.ShapeDtypeStruct((cfg.B * nj, 512, 128), F32), jax.ShapeDtypeStruct((cfg.B * nj, 512, 128), F32),
                   jax.ShapeDtypeStruct((cfg.B * BLK, 128), F32), jax.ShapeDtypeStruct((cfg.B * BLK, 128), F32),
                   jax.ShapeDtypeStruct((4, A_HEADS, BLK, 512), F32),
                   jax.ShapeDtypeStruct((8, 128), F32)],
        compiler_params=_cp("arbitrary", "arbitrary"),
    )(sink, pa, *([pa] * 8), bias, dya, ya, lse)


def _win_dkv_combine(cfg, dkp, dvp, dkm, dvm):
    nj = cfg.NJ

    def body(kp, vp, km, vm, o_ref):
        o_ref[:BLK, :128] = km[...].astype(BF16)
        o_ref[:BLK, 128:] = vm[...].astype(BF16)
        for kb in range(1, nj):
            for col, part in ((0, kp), (128, vp)):
                tot = part[kb, 2 * BLK:3 * BLK] + part[kb - 1, 3 * BLK:4 * BLK]
                if kb + 1 < nj:
                    tot = tot + part[kb + 1, BLK:2 * BLK]
                o_ref[kb * BLK:(kb + 1) * BLK, col:col + 128] = tot.astype(BF16)

    return _pallas(
        body, grid=(cfg.B,), name="win_dkv_combine",
        in_specs=[pl.BlockSpec((nj, 512, 128), lambda b: (b, 0, 0)), pl.BlockSpec((nj, 512, 128), lambda b: (b, 0, 0)),
                  pl.BlockSpec((BLK, 128), lambda b: (b, 0)), pl.BlockSpec((BLK, 128), lambda b: (b, 0))],
        out_specs=pl.BlockSpec((cfg.LP, 256), lambda b: (b, 0)),
        out_shape=jax.ShapeDtypeStruct((cfg.TP, 256), BF16),
        compiler_params=_cp("parallel"),
    )(dkp, dvp, dkm, dvm)


def _pair_blockdiag(q):
    lane = _lane(q.shape)
    return jnp.concatenate([jnp.where(lane < 128, q, jnp.zeros_like(q)), jnp.where(lane >= 128, q, jnp.zeros_like(q))], axis=0)


def _mla_fwd(cfg, q, kt, v, comm=None):
    nj, lp = cfg.NJ, cfg.LP

    def body(q_ref, kt_ref, v_ref, o_ref, lse_ref, s_even, s_odd):
        i = pl.program_id(2)
        lane_o = _lane((BLK, 128))

        def logits(s_write):
            s_write[...] = _dot(_pair_blockdiag(q_ref[...]), kt_ref[...])

        def finish(s_read):
            s = s_read[...]
            m = jnp.max(s, axis=1, keepdims=True)
            e = jnp.exp2(s - m)
            den = jnp.sum(e, axis=1, keepdims=True)
            pv = _dot(e.astype(BF16), v_ref[...]) / den
            o_ref[...] = jnp.where(lane_o < 64, pv[:BLK], pv[BLK:])
            lse_ref[0] = jnp.broadcast_to(m + jnp.log2(den), (2 * BLK, 128))

        odd = i % 2 == 1

        @pl.when(i == 0)
        def _():
            logits(s_even)

        @pl.when(odd & (i < nj))
        def _():
            logits(s_odd)
            finish(s_even)

        @pl.when(jnp.logical_not(odd) & (i > 0) & (i < nj))
        def _():
            logits(s_even)
            finish(s_odd)

        @pl.when(i == nj)
        def _():
            finish(s_even if nj % 2 == 1 else s_odd)

    cur = lambda b, i: b * nj + jnp.minimum(i, nj - 1)
    prev = lambda b, i: b * nj + jnp.maximum(i - 1, 0)
    return _call_with_comm(
        body, comm, grid=(cfg.B, 4, nj + 1), name="mla_fwd",
        in_specs=[pl.BlockSpec((BLK, 256), lambda b, p, i: (cur(b, i), p)), pl.BlockSpec((256, lp), lambda b, p, i: (b * 4 + p, 0)),
                  pl.BlockSpec((lp, 128), lambda b, p, i: (b, p))],
        out_specs=[pl.BlockSpec((BLK, 128), lambda b, p, i: (prev(b, i), p)),
                   pl.BlockSpec((1, 2 * BLK, 128), lambda b, p, i: (p, prev(b, i), 0))],
        out_shape=[jax.ShapeDtypeStruct((cfg.TP, 512), F32), jax.ShapeDtypeStruct((4, 2 * cfg.TP, 128), F32)],
        scratch_shapes=[pltpu.VMEM((2 * BLK, lp), F32), pltpu.VMEM((2 * BLK, lp), F32)],
        args=(q, kt, v))


def _mla_bwd(cfg, q, k, kt, vt, dyb, yb, lse, comm=None):
    nj, lp = cfg.NJ, cfg.LP

    def body(q_ref, k_ref, kt_ref, vt_ref, dy_ref, y_ref, lse_ref, dq_ref, dk_ref, dv_ref):
        i = pl.program_id(2)

        @pl.when(i == 0)
        def _():
            dk_ref[...] = jnp.zeros_like(dk_ref)
            dv_ref[...] = jnp.zeros_like(dv_ref)

        lane_o = _lane((BLK, 128))
        qbd = _pair_blockdiag(q_ref[...])
        dy, y = dy_ref[...], y_ref[...]
        do_s = jnp.concatenate([jnp.where(lane_o < 64, dy, 0.0), jnp.where(lane_o >= 64, dy, 0.0)], axis=0)
        delta = jnp.sum(do_s * jnp.concatenate([y, y], axis=0), axis=1, keepdims=True)
        do_b = do_s.astype(BF16)
        p = jnp.exp2(_dot(qbd, kt_ref[...]) - lse_ref[0][:, :1])
        ds = p * (_dot(do_b, vt_ref[...]) - delta)
        dsb = (ds * LN2).astype(BF16)
        dq2 = _dot(dsb, k_ref[...])
        dq_ref[...] = jnp.where(_lane((BLK, 256)) < 128, dq2[:BLK], dq2[BLK:]) * Q_SCALE
        dk_ref[...] += _dot_tn(dsb, qbd)
        dv_ref[...] += _dot_tn(p.astype(BF16), do_b)

    return _call_with_comm(
        body, comm, grid=(cfg.B, 4, nj), name="mla_bwd",
        in_specs=[pl.BlockSpec((BLK, 256), lambda b, p, i: (b * nj + i, p)), pl.BlockSpec((lp, 256), lambda b, p, i: (b, p)),
                  pl.BlockSpec((256, lp), lambda b, p, i: (b * 4 + p, 0)), pl.BlockSpec((128, lp), lambda b, p, i: (b * 4 + p, 0)),
                  pl.BlockSpec((BLK, 128), lambda b, p, i: (b * nj + i, p)), pl.BlockSpec((BLK, 128), lambda b, p, i: (b * nj + i, p)),
                  pl.BlockSpec((1, 2 * BLK, 128), lambda b, p, i: (p, b * nj + i, 0))],
        out_specs=[pl.BlockSpec((BLK, 256), lambda b, p, i: (b * nj + i, p)), pl.BlockSpec((lp, 256), lambda b, p, i: (b, p)),
                   pl.BlockSpec((lp, 128), lambda b, p, i: (b, p))],
        out_shape=[jax.ShapeDtypeStruct((cfg.TP, 1024), F32), jax.ShapeDtypeStruct((cfg.TP, 1024), F32),
                   jax.ShapeDtypeStruct((cfg.TP, 512), F32)],
        args=(q, k, kt, vt, dyb, yb, lse))


def _loss_bwd(cfg, h, target, gf):
    nj, nb = cfg.NJ, cfg.NB

    def body(h_ref, t_ref, g_ref, dh_ref, loss_ref, dg_ref):
        b, j = pl.program_id(0), pl.program_id(1)

        @pl.when((b == 0) & (j == 0))
        def _():
            loss_ref[...] = jnp.zeros_like(loss_ref)
            dg_ref[...] = jnp.zeros_like(dg_ref)

        @pl.when(j == 0)
        def _():
            dh_ref[...] = jnp.zeros_like(dh_ref)

        @pl.when(j > 0)
        def _():
            g = g_ref[...]
            xh, r = _rms(h_ref[...])
            err = xh * g - t_ref[...]
            loss_ref[...] += jnp.where((lax.broadcasted_iota(jnp.int32, (8, 128), 0) == 0) & (_lane((8, 128)) == 0),
                                       (0.5 / D_MODEL) * jnp.sum(err * err), 0.0)
            dy = err * (1.0 / D_MODEL)
            dg_ref[...] += jnp.sum(dy * xh, axis=0, keepdims=True)
            dh_ref[...] = _rms_bwd(xh, r, dy * g)

    return _pallas(
        body, grid=(cfg.B, nj), name="loss_bwd",
        in_specs=[pl.BlockSpec((BLK, D_MODEL), lambda b, j: (b * nj + j, 0)),
                  pl.BlockSpec((BLK, D_MODEL), lambda b, j: (b * nb + jnp.maximum(j - 1, 0), 0)),
                  pl.BlockSpec((1, D_MODEL), lambda b, j: (0, 0))],
        out_specs=[pl.BlockSpec((BLK, D_MODEL), lambda b, j: (b * nj + j, 0)), pl.BlockSpec((8, 128), lambda b, j: (0, 0)),
                   pl.BlockSpec((1, D_MODEL), lambda b, j: (0, 0))],
        out_shape=[jax.ShapeDtypeStruct((cfg.TP, D_MODEL), F32), jax.ShapeDtypeStruct((8, 128), F32),
                   jax.ShapeDtypeStruct((1, D_MODEL), F32)],
        compiler_params=_cp("arbitrary", "arbitrary"),
    )(h, target, gf)


def _out_bwd(cfg, dh, ya, yb, pf, goa, gob, wo_p):
    tm = 256

    def body(dh_ref, ya_ref, yb_ref, ga_ref, gb_ref, goa_ref, gob_ref, w_ref,
             dya_ref, dyb_ref, dg_ref, dw_ref, dgoa_ref, dgob_ref):
        @pl.when(pl.program_id(0) == 0)
        def _():
            dw_ref[...] = jnp.zeros_like(dw_ref)
            dgoa_ref[...] = jnp.zeros_like(dgoa_ref)
            dgob_ref[...] = jnp.zeros_like(dgob_ref)

        ga, gb, goa, gob = ga_ref[...], gb_ref[...], goa_ref[...], gob_ref[...]
        xa, ra, xb, rb, sga, sgb, y_a, y_b = _gate_halves(ya_ref[...], yb_ref[...], ga, gb, goa, gob)
        dhb = dh_ref[...].astype(BF16)
        dw_ref[...] += _dot_tn(jnp.concatenate([y_a, y_b], axis=1).astype(BF16), dhb)
        dy = _dot_nt(dhb, w_ref[...])
        for (dyh, x, r, g, sg, go, dy_out, dgo_ref, col) in (
                (dy[:, :512], xa, ra, ga, sga, goa, dya_ref, dgoa_ref, 0), (dy[:, 512:], xb, rb, gb, sgb, gob, dyb_ref, dgob_ref, 512)):
            dn = dyh * (g * sg)
            dg_ref[:, col:col + 512] = (dyh * (x * go) * (sg * (1.0 + g * (1.0 - sg)))).astype(BF16)
            dgo_ref[...] += jnp.sum(dn * x, axis=0, keepdims=True)
            dy_out[...] = _rms_bwd(x, r, dn * go)

    half = lambda c: pl.BlockSpec((tm, 512), lambda i: (i, c))
    vec = pl.BlockSpec((1, 512), lambda i: (0, 0))
    return _pallas(
        body, grid=(cfg.TP // tm,), name="out_bwd",
        in_specs=[pl.BlockSpec((tm, D_MODEL), lambda i: (i, 0)), half(0), half(0), half(0), half(1), vec, vec,
                  pl.BlockSpec((D_MODEL, D_MODEL), lambda i: (0, 0))],
        out_specs=[half(0), half(0), pl.BlockSpec((tm, D_MODEL), lambda i: (i, 0)),
                   pl.BlockSpec((D_MODEL, D_MODEL), lambda i: (0, 0)), vec, vec],
        out_shape=[jax.ShapeDtypeStruct((cfg.TP, 512), F32), jax.ShapeDtypeStruct((cfg.TP, 512), F32),
                   jax.ShapeDtypeStruct((cfg.TP, D_MODEL), BF16), jax.ShapeDtypeStruct((D_MODEL, D_MODEL), F32),
                   jax.ShapeDtypeStruct((1, 512), F32), jax.ShapeDtypeStruct((1, 512), F32)],
        compiler_params=_cp("arbitrary"),
    )(dh, ya, yb, pf, pf, goa, gob, wo_p)


def _lat_bwd(cfg, dq, dk, dv, pf, gq, gkv, wq_p, wkv_p, c_tab, s_tab):
    nj = cfg.NJ

    def body(dq_ref, dk_ref, dv_ref, cq_ref, ckv_ref, gq_ref, gkv_ref, wq_ref, wkv_ref, c_ref, s_ref,
             dl_ref, dwq_ref, dwkv_ref, dgq_ref, dgkv_ref):
        @pl.when((pl.program_id(0) == 0) & (pl.program_id(1) == 0))
        def _():
            dwq_ref[...] = jnp.zeros_like(dwq_ref)
            dwkv_ref[...] = jnp.zeros_like(dwkv_ref)
            dgq_ref[...] = jnp.zeros_like(dgq_ref)
            dgkv_ref[...] = jnp.zeros_like(dgkv_ref)

        c1, s1 = c_ref[...], s_ref[...]
        c8, s8 = jnp.tile(c1, (1, 8)), jnp.tile(s1, (1, 8))
        dq_r = dq_ref[...]
        dqp = (dq_r * c8 + _swap_rope(dq_r * s8)).astype(BF16)
        gq = gq_ref[...]
        xq, rq = _rms(cq_ref[...])
        dwq_ref[...] += _dot_tn((xq * gq).astype(BF16), dqp)
        dn = _dot_nt(dqp, wq_ref[...])
        dgq_ref[...] += jnp.sum(dn * xq, axis=0, keepdims=True)
        dl_ref[:, :256] = _rms_bwd(xq, rq, dn * gq).astype(BF16)

        dk_r = dk_ref[...]
        dkr = dk_r[:, :128]
        for hd in range(1, 8):
            dkr = dkr + dk_r[:, hd * 128:(hd + 1) * 128]
        lane1 = _lane(dkr.shape)
        dkr = jnp.where((lane1 >= 64) & (lane1 < 96), dkr, 0.0)
        dl_ref[:, 384:] = (dkr * c1 + _swap_rope(dkr * s1)).astype(BF16)
        dkv = jnp.concatenate([dk_r, dv_ref[...]], axis=1).astype(BF16)
        gkv = gkv_ref[...]
        xk, rk = _rms(ckv_ref[...])
        dwkv_ref[...] += _dot_tn((xk * gkv).astype(BF16), dkv)
        dn2 = _dot_nt(dkv, wkv_ref[...])
        dgkv_ref[...] += jnp.sum(dn2 * xk, axis=0, keepdims=True)
        dl_ref[:, 256:384] = _rms_bwd(xk, rk, dn2 * gkv).astype(BF16)

    row = lambda b, j: b * nj + j
    const = lambda shape: pl.BlockSpec(shape, lambda b, j: (0, 0))
    return _pallas(
        body, grid=(cfg.B, nj), name="lat_bwd",
        in_specs=[pl.BlockSpec((BLK, 1024), lambda b, j: (row(b, j), 0)), pl.BlockSpec((BLK, 1024), lambda b, j: (row(b, j), 0)),
                  pl.BlockSpec((BLK, 512), lambda b, j: (row(b, j), 0)),
                  pl.BlockSpec((BLK, 256), lambda b, j: (row(b, j), 4)), pl.BlockSpec((BLK, 128), lambda b, j: (row(b, j), 10)),
                  const((1, 256)), const((1, 128)), const((256, 1024)), const((128, 1536)),
                  pl.BlockSpec((BLK, 128), lambda b, j: (j, 0)), pl.BlockSpec((BLK, 128), lambda b, j: (j, 0))],
        out_specs=[pl.BlockSpec((BLK, 512), lambda b, j: (row(b, j), 0)), const((256, 1024)), const((128, 1536)),
                   const((1, 256)), const((1, 128))],
        out_shape=[jax.ShapeDtypeStruct((cfg.TP, 512), BF16), jax.ShapeDtypeStruct((256, 1024), F32),
                   jax.ShapeDtypeStruct((128, 1536), F32), jax.ShapeDtypeStruct((1, 256), F32), jax.ShapeDtypeStruct((1, 128), F32)],
        compiler_params=_cp("arbitrary", "arbitrary"),
    )(dq, dk, dv, pf, pf, gq, gkv, wq_p, wkv_p, c_tab, s_tab)


def _inproj_bwd(cfg, h, g, w_p, dqa, dkva, dgate, dlat, dh, comm=None):
    tm = 256

    def body(h_ref, g_ref, w_ref, dqa_ref, dkva_ref, dg_ref, dl_ref, dh_ref, o_ref, dw_ref, dgn_ref):
        @pl.when(pl.program_id(0) == 0)
        def _():
            dw_ref[...] = jnp.zeros_like(dw_ref)
            dgn_ref[...] = jnp.zeros_like(dgn_ref)

        g = g_ref[...]
        xh, r = _rms(h_ref[...])
        dproj = jnp.concatenate([dqa_ref[...], dkva_ref[...], dg_ref[...], dl_ref[...]], axis=1)
        dw_ref[...] += _dot_tn((xh * g).astype(BF16), dproj)
        du = _dot_nt(dproj, w_ref[...])
        dgn_ref[...] += jnp.sum(du * xh, axis=0, keepdims=True)
        o_ref[...] = dh_ref[...] + _rms_bwd(xh, r, du * g)

    rows = lambda w: pl.BlockSpec((tm, w), lambda i: (i, 0))
    return _call_with_comm(
        body, comm, grid=(cfg.TP // tm,), name="inproj_bwd",
        in_specs=[rows(D_MODEL), pl.BlockSpec((1, D_MODEL), lambda i: (0, 0)), pl.BlockSpec((D_MODEL, W_IN_P), lambda i: (0, 0)),
                  rows(512), rows(256), rows(1024), rows(512), rows(D_MODEL)],
        out_specs=[rows(D_MODEL), pl.BlockSpec((D_MODEL, W_IN_P), lambda i: (0, 0)), pl.BlockSpec((1, D_MODEL), lambda i: (0, 0))],
        out_shape=[jax.ShapeDtypeStruct((cfg.TP, D_MODEL), F32), jax.ShapeDtypeStruct((D_MODEL, W_IN_P), F32),
                   jax.ShapeDtypeStruct((1, D_MODEL), F32)],
        args=(h, g, w_p, dqa, dkva, dgate, dlat, dh))


def _meta_grad(cfg, dh):
    def body(d_ref, o_ref):
        @pl.when(pl.program_id(0) == 0)
        def _():
            o_ref[...] = d_ref[...]

        @pl.when(pl.program_id(0) > 0)
        def _():
            o_ref[...] += d_ref[...]

    return _pallas(
        body, grid=(cfg.B,), name="meta_grad",
        in_specs=[pl.BlockSpec((BLK, D_MODEL), lambda b: (b * cfg.NJ, 0))],
        out_specs=pl.BlockSpec((BLK, D_MODEL), lambda b: (0, 0)),
        out_shape=jax.ShapeDtypeStruct((BLK, D_MODEL), F32),
        compiler_params=_cp("arbitrary"),
    )(dh)


MATRICES = ("w_in", "w_uq", "w_ukv", "w_out")


def _local_grads(cfg, x, target, meta, table, small, weight_of, rider=None):
    def ride(stage, i, mats):
        hook = rider(stage, i, mats) if rider else None
        return hook if hook else (None, lambda res: None)

    depth = small["norm_in"].shape[0]
    rel, vis = _window_structure(cfg.NJ)
    bucket = _t5_bucket(jnp.asarray(rel))
    maskadd = jnp.asarray(np.where(vis, 0.0, NEG).astype(np.float32))
    c_tab, s_tab = _rope_tables(cfg)
    bias = _bias_build(table, bucket, maskadd)

    meta_blk = jnp.concatenate([meta, jnp.zeros((BLK - N_META, D_MODEL), F32)], axis=0)
    h = jnp.concatenate([jnp.broadcast_to(meta_blk[None], (cfg.B, BLK, D_MODEL)), x], axis=1).reshape(cfg.TP, D_MODEL)

    wp, saved = [], []
    for i in range(depth):
        w = dict(w_in=_w_in_to_p(weight_of(i, "w_in")),
                 g_in=small["norm_in"][i][None], gq=small["norm_q_lat"][i][None], gkv=small["norm_kv_lat"][i][None],
                 goa=_perm_heads64(small["norm_out_a"][i], 0)[None], gob=small["norm_out_b"][i][None], sink=small["sink_a"][i])
        wp.append(w)
        comm, deliver = ride("inproj_fwd", i, {})
        pa, pf, *travelled = _inproj_fwd(cfg, h, w["g_in"], w["w_in"], comm)
        deliver(travelled)
        w.update(w_uq=_w_uq_to_p(weight_of(i, "w_uq")), w_ukv=_w_ukv_to_p(weight_of(i, "w_ukv")), w_out=_w_out_to_p(weight_of(i, "w_out")))
        q, k, v, kt, vt = _lat_fwd(cfg, pf, w["gq"], w["gkv"], w["w_uq"], w["w_ukv"], c_tab, s_tab)
        ya, lse_a = _win_fwd(cfg, pa, bias, w["sink"])
        comm, deliver = ride("mla_fwd", i, {})
        yb, lse_b, *travelled = _mla_fwd(cfg, q, kt, v, comm)
        deliver(travelled)
        h_next = _out_fwd(cfg, ya, yb, pf, w["goa"], w["gob"], w["w_out"], h)
        saved.append(dict(h=h, pa=pa, pf=pf, q=q, k=k, kt=kt, vt=vt, ya=ya, lse_a=lse_a, yb=yb, lse_b=lse_b))
        h = h_next

    dh, loss_tile, d_norm_final = _loss_bwd(cfg, h, target.reshape(cfg.B * cfg.S, D_MODEL), small["norm_final"][None])

    grads = {k_: [] for k_ in ("norm_in", "sink_a", "norm_q_lat", "norm_kv_lat", "norm_out_a", "norm_out_b")}
    mats, s_accs = {}, []
    for i in reversed(range(depth)):
        w, sv = wp[i], saved[i]
        dya, dyb, dgate, dwo, dgoa, dgob = _out_bwd(cfg, dh, sv["ya"], sv["yb"], sv["pf"], w["goa"], w["gob"], w["w_out"])
        dqa, dkp, dvp, dkm, dvm, s_acc, dsink = _win_bwd(cfg, sv["pa"], bias, w["sink"], dya, sv["ya"], sv["lse_a"])
        dkva = _win_dkv_combine(cfg, dkp, dvp, dkm, dvm)
        comm, deliver = ride("mla_bwd", i, mats)
        dq, dk, dv, *travelled = _mla_bwd(cfg, sv["q"], sv["k"], sv["kt"], sv["vt"], dyb, sv["yb"], sv["lse_b"], comm)
        deliver(travelled)
        dlat, dwq, dwkv, dgq, dgkv = _lat_bwd(cfg, dq, dk, dv, sv["pf"], w["gq"], w["gkv"], w["w_uq"], w["w_ukv"], c_tab, s_tab)
        mats[i] = dict(w_uq=_w_uq_from_p(dwq), w_ukv=_w_ukv_from_p(dwkv), w_out=_w_out_from_p(dwo))
        comm, deliver = ride("inproj_bwd", i, mats)
        dh, dwin, dgin, *travelled = _inproj_bwd(cfg, sv["h"], w["g_in"], w["w_in"], dqa, dkva, dgate, dlat, dh, comm)
        deliver(travelled)
        s_accs.append(s_acc)
        mats[i]["w_in"] = _w_in_from_p(dwin)
        grads["norm_in"].append(dgin[0])
        grads["sink_a"].append(dsink[0, :A_HEADS])
        grads["norm_q_lat"].append(dgq[0])
        grads["norm_kv_lat"].append(dgkv[0])
        grads["norm_out_a"].append(_unperm_heads64(dgoa[0], 0))
        grads["norm_out_b"].append(dgob[0])

    out = {k_: jnp.stack(v_[::-1]) for k_, v_ in grads.items()}
    out["rel_bias_table"] = _bias_grad(s_accs, bucket)[:, 0].reshape(N_BUCKETS, A_HEADS)
    out["norm_final"] = d_norm_final[0]
    return loss_tile[0, 0], dh.reshape(cfg.B, cfg.LP, D_MODEL)[:, BLK:], out, mats, _meta_grad(cfg, dh)[:N_META]


MESH = pl.DeviceIdType.MESH
ANY = pl.BlockSpec(memory_space=pl.ANY)


def _place():
    x, y, c = lax.axis_index("x"), lax.axis_index("y"), lax.axis_index("c")
    others = [(1 - x, y), (x, 1 - y), (1 - x, 1 - y)]
    return x, y, c, others


Comm = collections.namedtuple("Comm", "inputs out_shapes scratch start wait")


def _gather_comm(shards):
    n = len(shards)

    def copies(ins, outs, sems, arriving):
        send_sems, recv_sems, local_sems = sems
        x, y, c, others = _place()
        k_me = 2 * x + y
        local = [pltpu.make_async_copy(ins[a], outs[a].at[k_me], local_sems.at[a]) for a in range(n)]
        remote = [pltpu.make_async_remote_copy(src_ref=ins[a], dst_ref=outs[a].at[2 * ox + oy if arriving else k_me],
                                               send_sem=send_sems.at[3 * a + j], recv_sem=recv_sems.at[3 * a + j],
                                               device_id=(ox, oy, c), device_id_type=MESH)
                  for a in range(n) for j, (ox, oy) in enumerate(others)]
        return local, remote

    def start(ins, outs, sems):
        local, sends = copies(ins, outs, sems, arriving=False)
        for cp in local + sends:
            cp.start()

    def wait(ins, outs, sems):
        local, recvs = copies(ins, outs, sems, arriving=True)
        for cp in recvs:
            cp.wait_recv()
        for cp in recvs:
            cp.wait_send()
        for cp in local:
            cp.wait()

    return Comm(list(shards), [jax.ShapeDtypeStruct((4, *s.shape), s.dtype) for s in shards],
                [pltpu.SemaphoreType.DMA((3 * n,)), pltpu.SemaphoreType.DMA((3 * n,)), pltpu.SemaphoreType.DMA((n,))], start, wait)


def _scatter_comm(parts):
    n = len(parts)

    def copies(ins, outs, sems):
        send_sems, recv_sems = sems
        x, y, c, others = _place()
        return [pltpu.make_async_remote_copy(src_ref=ins[a].at[2 * ox + oy], dst_ref=outs[a].at[j], send_sem=send_sems.at[3 * a + j],
                                             recv_sem=recv_sems.at[3 * a + j], device_id=(ox, oy, c), device_id_type=MESH)
                for a in range(n) for j, (ox, oy) in enumerate(others)]

    def start(ins, outs, sems):
        for cp in copies(ins, outs, sems):
            cp.start()

    def wait(ins, outs, sems):
        cps = copies(ins, outs, sems)
        for cp in cps:
            cp.wait_recv()
        for cp in cps:
            cp.wait_send()

    return Comm(list(parts), [jax.ShapeDtypeStruct((3, *p.shape[1:]), p.dtype) for p in parts],
                [pltpu.SemaphoreType.DMA((3 * n,)), pltpu.SemaphoreType.DMA((3 * n,))], start, wait)


def _run_comm(comm, name):
    ni, no = len(comm.inputs), len(comm.out_shapes)

    def body(*refs):
        ins, outs, sems = refs[:ni], refs[ni:ni + no], refs[ni + no:]
        comm.start(ins, outs, sems)
        comm.wait(ins, outs, sems)

    return _pallas(body, name=name, in_specs=[ANY] * ni, out_specs=[ANY] * no, out_shape=comm.out_shapes,
                          scratch_shapes=comm.scratch)(*comm.inputs)


def _call_with_comm(body, comm, *, grid, name, in_specs, out_specs, out_shape, args, scratch_shapes=()):
    if comm is None:
        return _pallas(body, grid=grid, name=name, in_specs=in_specs, out_specs=out_specs, out_shape=out_shape,
                              scratch_shapes=list(scratch_shapes), compiler_params=_cp(*["arbitrary"] * len(grid)))(*args)
    n_in, n_out, ci, co, ns = len(in_specs), len(out_specs), len(comm.inputs), len(comm.out_shapes), len(scratch_shapes)

    def wrapped(*refs):
        ins, cins = refs[:n_in], refs[n_in:n_in + ci]
        outs, couts = refs[n_in + ci:n_in + ci + n_out], refs[n_in + ci + n_out:n_in + ci + n_out + co]
        scratch, sems = refs[n_in + ci + n_out + co:n_in + ci + n_out + co + ns], refs[n_in + ci + n_out + co + ns:]
        ids = [pl.program_id(a) for a in range(len(grid))]
        first = functools.reduce(jnp.logical_and, [i == 0 for i in ids])
        last = functools.reduce(jnp.logical_and, [i == g - 1 for i, g in zip(ids, grid)])

        @pl.when(first)
        def _():
            comm.start(cins, couts, sems)

        body(*ins, *outs, *scratch)

        @pl.when(last)
        def _():
            comm.wait(cins, couts, sems)

    return _pallas(
        wrapped, grid=grid, name=name + "_comm", in_specs=[*in_specs, *[ANY] * ci], out_specs=[*out_specs, *[ANY] * co],
        out_shape=[*out_shape, *comm.out_shapes], scratch_shapes=[*scratch_shapes, *comm.scratch],
        compiler_params=_cp(*["arbitrary"] * len(grid)))(*args, *comm.inputs)


def _swap_sibling(arrs):
    n = len(arrs)

    def body(*refs):
        ins, outs = refs[:n], refs[n:2 * n]
        send_sems, recv_sems = refs[2 * n:]
        x, y, c, _ = _place()
        copies = [pltpu.make_async_remote_copy(src_ref=ins[a], dst_ref=outs[a], send_sem=send_sems.at[a], recv_sem=recv_sems.at[a],
                                               device_id=(x, y, 1 - c), device_id_type=MESH) for a in range(n)]
        for cp in copies:
            cp.start()
        for cp in copies:
            cp.wait_recv()
        for cp in copies:
            cp.wait_send()

    return _pallas(
        body, name="swap_sibling", in_specs=[ANY] * n, out_specs=[ANY] * n,
        out_shape=[jax.ShapeDtypeStruct(a.shape, a.dtype) for a in arrs],
        scratch_shapes=[pltpu.SemaphoreType.DMA((n,)), pltpu.SemaphoreType.DMA((n,))],
    )(*arrs)


def _allreduce_small(v):
    def body(v_ref, o_ref, buf, send_sems, recv_sems):
        x, y, c, _ = _place()
        me = 4 * x + 2 * y + c
        buf[me] = v_ref[...]

        def copy(r):
            tx, ty, tc = (x + (r >> 2)) % 2, (y + ((r >> 1) & 1)) % 2, (c + (r & 1)) % 2
            return tx, ty, tc

        sends = []
        for r in range(1, 8):
            tx, ty, tc = copy(r)
            sends.append(pltpu.make_async_remote_copy(src_ref=v_ref, dst_ref=buf.at[me], send_sem=send_sems.at[r - 1],
                                                      recv_sem=recv_sems.at[r - 1], device_id=(tx, ty, tc), device_id_type=MESH))
        for cp in sends:
            cp.start()
        for r in range(1, 8):
            tx, ty, tc = copy(r)
            pltpu.make_async_remote_copy(src_ref=v_ref, dst_ref=buf.at[4 * tx + 2 * ty + tc], send_sem=send_sems.at[r - 1],
                                         recv_sem=recv_sems.at[r - 1], device_id=(tx, ty, tc), device_id_type=MESH).wait_recv()
        for cp in sends:
            cp.wait_send()
        acc = buf[0]
        for d in range(1, 8):
            acc = acc + buf[d]
        o_ref[...] = acc

    return pl.pallas_call(
        body, name="allreduce_small", in_specs=[pl.BlockSpec(memory_space=pltpu.VMEM)], out_specs=pl.BlockSpec(memory_space=pltpu.VMEM),
        out_shape=jax.ShapeDtypeStruct(v.shape, F32),
        scratch_shapes=[pltpu.VMEM((8, *v.shape), F32), pltpu.SemaphoreType.DMA((7,)), pltpu.SemaphoreType.DMA((7,))],
    )(v)


def _rows_view(a):
    return a.reshape(-1, a.shape[-1])


def _elementwise(name, fn, ins, n_out):
    rows, cols = ins[0].shape
    tm = min(rows, 256)
    spec = pl.BlockSpec((tm, cols), lambda i: (i, 0))

    def body(*refs):
        outs = fn(*[r[...] for r in refs[:len(ins)]])
        for o_ref, o in zip(refs[len(ins):], outs):
            o_ref[...] = o

    return _pallas(
        body, grid=(rows // tm,), name=name, in_specs=[spec] * len(ins), out_specs=[spec] * n_out,
        out_shape=[jax.ShapeDtypeStruct((rows, cols), F32)] * n_out, compiler_params=_cp("parallel"),
    )(*ins)


def _sum_parts(name, own, recv):
    def fn(o, r0, r1, r2):
        return (o + r0.astype(F32) + r1.astype(F32) + r2.astype(F32),)

    return _elementwise("sum_parts_" + name, fn, [own, recv[0], recv[1], recv[2]], 1)[0]


def _adamw(name, w, m, v, g_parts):
    def fn(w_, m_, v_, *gs):
        g = gs[0]
        for extra in gs[1:]:
            g = g + extra
        m_new = ADAM_B1 * m_ + (1.0 - ADAM_B1) * g
        v_new = ADAM_B2 * v_ + (1.0 - ADAM_B2) * (g * g)
        m_hat = m_new / (1.0 - ADAM_B1 ** ADAM_STEP)
        v_hat = v_new / (1.0 - ADAM_B2 ** ADAM_STEP)
        delta = -ADAM_LR * (m_hat / (jnp.sqrt(v_hat) + ADAM_EPS) + ADAM_WD * w_)
        return g, delta, m_new, v_new

    return _elementwise("adamw_" + name, fn, [w, m, v, *g_parts], 4)


MAT_AXIS = {"w_in": 1, "w_uq": 1, "w_ukv": 1, "w_out": 0}
SMALL = ("rel_bias_table", "norm_in", "sink_a", "norm_q_lat", "norm_kv_lat", "norm_out_a", "norm_out_b", "norm_final")
WEIGHTS = ("meta_tokens", "rel_bias_table", "norm_in", "w_in", "sink_a", "norm_q_lat", "w_uq", "norm_kv_lat", "w_ukv",
           "norm_out_a", "norm_out_b", "w_out", "norm_final")
SMALL_ROWS, SMALL_COLS = 8, 1024


def _pack_small(d, loss=None):
    flat = [d[n].reshape(-1) for n in SMALL]
    if loss is not None:
        flat.append(loss.reshape(1))
    used = sum(f.shape[0] for f in flat)
    flat.append(jnp.zeros((SMALL_ROWS * SMALL_COLS - used,), F32))
    return jnp.concatenate(flat).reshape(SMALL_ROWS, SMALL_COLS)


def _unpack_small(p, like):
    flat, out, off = p.reshape(-1), {}, 0
    for n in SMALL:
        size = int(np.prod(like[n].shape))
        out[n] = flat[off:off + size].reshape(like[n].shape)
        off += size
    return out, flat[off]


def _split4(a, axis):
    size = a.shape[axis] // 4
    return jnp.stack([lax.slice_in_dim(a, k * size, (k + 1) * size, axis=axis) for k in range(4)])


def _train_step(cfg, x, target, w, m, v):
    depth = w["w_in"].shape[0]
    rest = tuple(n for n in MATRICES if n != "w_in")
    weights, splits, received = {}, {}, {}

    def gather(i, names, also=()):
        def deliver(res):
            for n, g in zip(names, res):
                weights[i, n] = jnp.concatenate([g[k] for k in range(4)], axis=MAT_AXIS[n])

        return _gather_comm([w[n][i].astype(BF16) for n in names] + list(also)), deliver

    def scatter(i, names, mats, also=()):
        for n in names:
            splits[i, n] = _split4(mats[i][n], MAT_AXIS[n])

        def deliver(res):
            for n, r in zip(names, res):
                received[i, n] = r

        return _scatter_comm([splits[i, n].astype(BF16) for n in names] + list(also)), deliver

    def rider(stage, i, mats):
        if stage == "inproj_fwd" and i == 0:
            return gather(0, rest)
        if stage == "mla_fwd" and i + 1 < depth:
            return gather(i + 1, MATRICES)
        if stage == "mla_bwd" and i + 1 < depth:
            return scatter(i + 1, MATRICES, mats)
        if stage == "inproj_bwd" and i == 0:
            return scatter(0, rest, mats)
        return None

    comm, deliver = gather(0, ("w_in",), also=[w["meta_tokens"]])
    first = _run_comm(comm, "gather_first")
    deliver(first)
    meta = jnp.concatenate([first[1][k] for k in range(4)], axis=1)

    loss_local, grad_x, g, mats, g_meta = _local_grads(cfg, x, target, meta, w["rel_bias_table"], {n: w[n] for n in SMALL},
                                                      lambda i, n: weights[i, n], rider)
    meta_split = _split4(g_meta, 1)
    comm, deliver = scatter(0, ("w_in",), mats, also=[meta_split.astype(BF16)])
    last = _run_comm(comm, "scatter_last")
    deliver(last)

    small_sum = _allreduce_small(_pack_small(g, loss_local))
    g_small, loss = _unpack_small(small_sum, {n: w[n] for n in SMALL})

    k_me = 2 * lax.axis_index("x") + lax.axis_index("y")

    def core_sum(name, split, recv):
        own = lax.dynamic_index_in_dim(split, k_me, 0, keepdims=False)
        return _sum_parts(name, _rows_view(own), recv.reshape(3, -1, recv.shape[-1]))

    partial = [core_sum("meta_tokens", meta_split, last[1])]
    for n in MATRICES:
        partial.append(jnp.concatenate([core_sum(f"{n}_{i}", splits[i, n], received[i, n]) for i in range(depth)], axis=0))
    sibling = _swap_sibling(partial)

    outs = {}
    for n, p_me, p_sib in zip(("meta_tokens", *MATRICES), partial, sibling):
        res = _adamw(n, _rows_view(w[n]), _rows_view(m[n]), _rows_view(v[n]), [p_me, p_sib])
        outs[n] = [r.reshape(w[n].shape) for r in res]
    res = _adamw("small", _pack_small(w), _pack_small(m), _pack_small(v), [_pack_small(g_small)])
    unpacked = [_unpack_small(r, {n: w[n] for n in SMALL})[0] for r in res]
    for n in SMALL:
        outs[n] = [u[n] for u in unpacked]

    result = [loss, grad_x]
    for field in range(4):
        result.extend(outs[n][field] for n in WEIGHTS)
    return tuple(result)


def kernel(x, meta_tokens, rel_bias_table, norm_in, w_in, sink_a, norm_q_lat, w_uq, norm_kv_lat, w_ukv, norm_out_a, norm_out_b, w_out, norm_final, loss_target, m_meta_tokens, m_rel_bias_table, m_norm_in, m_w_in, m_sink_a, m_norm_q_lat, m_w_uq, m_norm_kv_lat, m_w_ukv, m_norm_out_a, m_norm_out_b, m_w_out, m_norm_final, v_meta_tokens, v_rel_bias_table, v_norm_in, v_w_in, v_sink_a, v_norm_q_lat, v_w_uq, v_norm_kv_lat, v_w_ukv, v_norm_out_a, v_norm_out_b, v_w_out, v_norm_final):
    w = dict(zip(WEIGHTS, (meta_tokens, rel_bias_table, norm_in, w_in, sink_a, norm_q_lat, w_uq, norm_kv_lat, w_ukv, norm_out_a, norm_out_b, w_out, norm_final)))
    m = dict(zip(WEIGHTS, (m_meta_tokens, m_rel_bias_table, m_norm_in, m_w_in, m_sink_a, m_norm_q_lat, m_w_uq, m_norm_kv_lat, m_w_ukv, m_norm_out_a, m_norm_out_b, m_w_out, m_norm_final)))
    v = dict(zip(WEIGHTS, (v_meta_tokens, v_rel_bias_table, v_norm_in, v_w_in, v_sink_a, v_norm_q_lat, v_w_uq, v_norm_kv_lat, v_w_ukv, v_norm_out_a, v_norm_out_b, v_w_out, v_norm_final)))
    cfg = make_cfg(x.shape[0], x.shape[1])
    return _train_step(cfg, x, loss_target, w, m, v)
```

```python
import collections
import functools
import math

import jax
import jax.numpy as jnp
import numpy as np
from jax import lax
from jax.experimental import pallas as pl
from jax.experimental.pallas import tpu as pltpu

F32 = jnp.float32
BF16 = jnp.bfloat16

BLK = 128
N_META = 16
D_MODEL = 1024
A_HEADS, A_KV, A_DH = 8, 2, 64
B_HEADS, B_NOPE, B_ROPE, B_DV = 8, 64, 32, 64
Q_RANK, KV_RANK = 256, 128
N_BUCKETS, MAX_DIST = 32, 128
ROPE_THETA = 10000.0
EPS = 1e-6
IN_WIDTH = 2208
W_IN_P = 2304
NEG = -1e30
MASK_LANE = 96
Q_SCALE = (B_NOPE + B_ROPE) ** -0.5 * math.log2(math.e)
LN2 = math.log(2.0)
VMEM_LIMIT = 48 * 1024 * 1024

ADAM_LR, ADAM_B1, ADAM_B2, ADAM_EPS, ADAM_WD, ADAM_STEP = 0.001, 0.9, 0.999, 1e-08, 0.01, 10

Cfg = collections.namedtuple("Cfg", "B S NB NJ LP TP")


def make_cfg(batch, seq):
    nb = seq // BLK
    nj = nb + 1
    return Cfg(batch, seq, nb, nj, nj * BLK, batch * nj * BLK)


def _cp(*sem):
    return pltpu.CompilerParams(dimension_semantics=sem, vmem_limit_bytes=VMEM_LIMIT)


def _pallas(body, *, out_shape, **kw):
    pinned = jax.tree.map(lambda s: pltpu.HBM(s.shape, s.dtype), out_shape)
    call = pl.pallas_call(body, out_shape=pinned, **kw)
    return lambda *args: call(*[pltpu.with_memory_space_constraint(a, pltpu.HBM) for a in args])


def _dot(a, b):
    return jnp.dot(a, b, preferred_element_type=F32)


def _dot_nt(a, b):
    return lax.dot_general(a, b, (((1,), (1,)), ((), ())), preferred_element_type=F32)


def _dot_tn(a, b):
    return lax.dot_general(a, b, (((0,), (0,)), ((), ())), preferred_element_type=F32)


def _rms(x, width=None):
    n = x.shape[-1] if width is None else width
    r = lax.rsqrt(jnp.sum(x * x, axis=-1, keepdims=True) * (1.0 / n) + EPS)
    return x * r, r


def _rms_bwd(xhat, r, t):
    n = xhat.shape[-1]
    return r * (t - xhat * (jnp.sum(t * xhat, axis=-1, keepdims=True) * (1.0 / n)))


def _sigmoid(x):
    return 1.0 / (1.0 + jnp.exp(-x))


def _lane(shape):
    return lax.broadcasted_iota(jnp.int32, shape, len(shape) - 1)


def _swap_rope(x):
    n = x.shape[-1]
    lane = _lane(x.shape) % BLK
    up = pltpu.roll(x, n - 16, axis=x.ndim - 1)
    dn = pltpu.roll(x, 16, axis=x.ndim - 1)
    return jnp.where((lane >= 64) & (lane < 80), up, jnp.where((lane >= 80) & (lane < 96), dn, 0.0))


A_ORDER = (0, 4, 1, 5, 2, 6, 3, 7)


def _jtype(j, nj):
    return 0 if j == 0 else 1 if j == 1 else 3 if j == nj - 1 else 2


def _window_structure(nj):
    def pos(blk, r):
        return np.where(blk == 0, r, N_META + (blk - 1) * BLK + r)

    def valid(blk, r):
        return np.where(blk == 0, r < N_META, True)

    r = np.arange(BLK)
    rels, viss = [], []
    for j in (0, 1, 2, nj - 1):
        qpos = pos(j, r)[:, None]
        rel_t, vis_t = [], []
        for s, kb in enumerate((0, j - 1, j, j + 1)):
            slot_ok = (s == 0) or (1 <= kb <= nj - 1)
            kbc = min(max(kb, 0), nj - 1)
            kpos = pos(kbc, r)[None, :]
            rel = kpos - qpos
            v = valid(kbc, r)[None, :] & np.ones((BLK, 1), bool)
            if s > 0:
                v = v & (np.abs(rel) <= BLK)
            rel_t.append(rel)
            vis_t.append(v & slot_ok)
        rels.append(np.concatenate(rel_t, axis=1))
        viss.append(np.concatenate(vis_t, axis=1))
    return np.stack(rels).astype(np.int32), np.stack(viss)


def _t5_bucket(rel):
    nb = N_BUCKETS // 2
    max_exact = nb // 2
    ret = jnp.where(rel > 0, nb, 0)
    n = jnp.abs(rel)
    nf = jnp.maximum(n, 1).astype(jnp.float32)
    large = max_exact + (jnp.log(nf / max_exact) / math.log(MAX_DIST / max_exact) * (nb - max_exact)).astype(jnp.int32)
    large = jnp.minimum(large, nb - 1)
    return ret + jnp.where(n < max_exact, n, large)


def _perm_heads64(a, axis):
    parts = [lax.slice_in_dim(a, h * 64, (h + 1) * 64, axis=axis) for h in A_ORDER]
    return jnp.concatenate(parts, axis=axis)


def _unperm_heads64(a, axis):
    inv = [A_ORDER.index(h) for h in range(8)]
    parts = [lax.slice_in_dim(a, p * 64, (p + 1) * 64, axis=axis) for p in inv]
    return jnp.concatenate(parts, axis=axis)


def _w_in_to_p(w):
    sl = lambda a, b: lax.slice_in_dim(w, a, b, axis=1)
    z = lambda n: jnp.zeros((w.shape[0], n), w.dtype)
    return jnp.concatenate([_perm_heads64(sl(0, 512), 1), sl(512, 768), _perm_heads64(sl(768, 1280), 1), sl(1696, 2208),
                            sl(1280, 1536), sl(1536, 1664), z(64), sl(1664, 1696), z(32)], axis=1)


def _w_in_from_p(g):
    sl = lambda a, b: lax.slice_in_dim(g, a, b, axis=1)
    return jnp.concatenate([_unperm_heads64(sl(0, 512), 1), sl(512, 768), _unperm_heads64(sl(768, 1280), 1),
                            sl(1792, 2048), sl(2048, 2176), sl(2240, 2272), sl(1280, 1792)], axis=1)


def _w_uq_to_p(w):
    z = jnp.zeros((w.shape[0], 32), w.dtype)
    return jnp.concatenate([p for h in range(8) for p in (lax.slice_in_dim(w, h * 96, (h + 1) * 96, axis=1), z)], axis=1)


def _w_uq_from_p(g):
    return jnp.concatenate([lax.slice_in_dim(g, h * 128, h * 128 + 96, axis=1) for h in range(8)], axis=1)


def _w_ukv_to_p(w):
    z = jnp.zeros((w.shape[0], 64), w.dtype)
    ks = [p for h in range(8) for p in (lax.slice_in_dim(w, h * 128, h * 128 + 64, axis=1), z)]
    vs = [lax.slice_in_dim(w, h * 128 + 64, (h + 1) * 128, axis=1) for h in range(8)]
    return jnp.concatenate(ks + vs, axis=1)


def _w_ukv_from_p(g):
    parts = []
    for h in range(8):
        parts.append(lax.slice_in_dim(g, h * 128, h * 128 + 64, axis=1))
        parts.append(lax.slice_in_dim(g, 1024 + h * 64, 1024 + (h + 1) * 64, axis=1))
    return jnp.concatenate(parts, axis=1)


def _w_out_to_p(w):
    return jnp.concatenate([_perm_heads64(lax.slice_in_dim(w, 0, 512, axis=0), 0), lax.slice_in_dim(w, 512, 1024, axis=0)], axis=0)


def _w_out_from_p(g):
    return jnp.concatenate([_unperm_heads64(lax.slice_in_dim(g, 0, 512, axis=0), 0), lax.slice_in_dim(g, 512, 1024, axis=0)], axis=0)


def _rope_tables(cfg):
    half = B_ROPE // 2
    length = N_META + cfg.S
    freqs = ROPE_THETA ** (-jnp.arange(half, dtype=jnp.float32) / half)
    ang = jnp.arange(length, dtype=jnp.float32)[:, None] * freqs[None, :]
    cos, sin = jnp.cos(ang), jnp.sin(ang)

    def rows(t):
        return jnp.concatenate([t[:N_META], jnp.zeros((BLK - N_META, t.shape[1]), t.dtype), t[N_META:]], axis=0)

    ones = jnp.ones((length, 64), F32)
    zer = jnp.zeros((length, 32), F32)
    c_tab = rows(jnp.concatenate([ones, cos, cos, zer], axis=1))
    s_tab = rows(jnp.concatenate([zer, zer, -sin, sin, zer], axis=1))
    return c_tab, s_tab


def _inproj_fwd(cfg, h, g, w_p, comm=None):
    tm = 256

    def body(h_ref, g_ref, w_ref, pa_ref, pf_ref):
        xh, _ = _rms(h_ref[...])
        u = (xh * g_ref[...]).astype(BF16)
        acc = _dot(u, w_ref[...])
        pa_ref[...] = acc[:, :768].astype(BF16)
        pf_ref[...] = acc[:, 768:]

    return _call_with_comm(
        body, comm, grid=(cfg.TP // tm,), name="inproj_fwd",
        in_specs=[pl.BlockSpec((tm, D_MODEL), lambda i: (i, 0)), pl.BlockSpec((1, D_MODEL), lambda i: (0, 0)),
                  pl.BlockSpec((D_MODEL, W_IN_P), lambda i: (0, 0))],
        out_specs=[pl.BlockSpec((tm, 768), lambda i: (i, 0)), pl.BlockSpec((tm, 1536), lambda i: (i, 0))],
        out_shape=[jax.ShapeDtypeStruct((cfg.TP, 768), BF16), jax.ShapeDtypeStruct((cfg.TP, 1536), F32)],
        args=(h, g, w_p))


def _lat_fwd(cfg, pf, gq, gkv, wq_p, wkv_p, c_tab, s_tab):
    nj = cfg.NJ

    def body(cq_ref, ckv_ref, kr_ref, gq_ref, gkv_ref, wq_ref, wkv_ref, c_ref, s_ref, q_ref, k_ref, v_ref, kt_ref, vt_ref):
        c1, s1 = c_ref[...], s_ref[...]
        c8, s8 = jnp.tile(c1, (1, 8)), jnp.tile(s1, (1, 8))
        mask_lane = _lane((BLK, 1024)) % BLK == MASK_LANE
        zero_row = (pl.program_id(1) == 0) & (lax.broadcasted_iota(jnp.int32, (BLK, 1024), 0) >= N_META)
        xq, _ = _rms(cq_ref[...])
        qp = _dot((xq * gq_ref[...]).astype(BF16), wq_ref[...])
        q_ref[...] = jnp.where(mask_lane, 1.0, (qp * c8 + _swap_rope(qp) * s8) * Q_SCALE).astype(BF16)
        xk, _ = _rms(ckv_ref[...])
        kvp = _dot((xk * gkv_ref[...]).astype(BF16), wkv_ref[...])
        kr = kr_ref[...]
        krr = kr * c1 + _swap_rope(kr) * s1
        k = jnp.where(mask_lane & zero_row, NEG, kvp[:, :1024] + jnp.tile(krr, (1, 8)))
        k_ref[...] = k.astype(BF16)
        v_ref[...] = kvp[:, 1024:].astype(BF16)
        kt_ref[...] = k.T.astype(BF16)
        vt_ref[...] = kvp[:, 1024:].T.astype(BF16)

    row = lambda b, j: b * nj + j
    return _pallas(
        body, grid=(cfg.B, nj), name="lat_fwd",
        in_specs=[pl.BlockSpec((BLK, 256), lambda b, j: (row(b, j), 4)), pl.BlockSpec((BLK, 128), lambda b, j: (row(b, j), 10)),
                  pl.BlockSpec((BLK, 128), lambda b, j: (row(b, j), 11)),
                  pl.BlockSpec((1, 256), lambda b, j: (0, 0)), pl.BlockSpec((1, 128), lambda b, j: (0, 0)),
                  pl.BlockSpec((256, 1024), lambda b, j: (0, 0)), pl.BlockSpec((128, 1536), lambda b, j: (0, 0)),
                  pl.BlockSpec((BLK, 128), lambda b, j: (j, 0)), pl.BlockSpec((BLK, 128), lambda b, j: (j, 0))],
        out_specs=[pl.BlockSpec((BLK, 1024), lambda b, j: (row(b, j), 0)), pl.BlockSpec((BLK, 1024), lambda b, j: (row(b, j), 0)),
                   pl.BlockSpec((BLK, 512), lambda b, j: (row(b, j), 0)),
                   pl.BlockSpec((1024, BLK), lambda b, j: (b, j)), pl.BlockSpec((512, BLK), lambda b, j: (b, j))],
        out_shape=[jax.ShapeDtypeStruct((cfg.TP, 1024), BF16), jax.ShapeDtypeStruct((cfg.TP, 1024), BF16),
                   jax.ShapeDtypeStruct((cfg.TP, 512), BF16),
                   jax.ShapeDtypeStruct((cfg.B * 1024, cfg.LP), BF16), jax.ShapeDtypeStruct((cfg.B * 512, cfg.LP), BF16)],
        compiler_params=_cp("parallel", "parallel"),
    )(pf, pf, pf, gq, gkv, wq_p, wkv_p, c_tab, s_tab)


def _gate_halves(ya, yb, ga, gb, goa, gob):
    xa, ra = _rms(ya)
    xb, rb = _rms(yb)
    sga, sgb = _sigmoid(ga), _sigmoid(gb)
    return xa, ra, xb, rb, sga, sgb, xa * goa * (ga * sga), xb * gob * (gb * sgb)


def _out_fwd(cfg, ya, yb, pf, goa, gob, wo_p, h):
    tm = 256

    def body(ya_ref, yb_ref, ga_ref, gb_ref, goa_ref, gob_ref, w_ref, h_ref, o_ref):
        *_, y_a, y_b = _gate_halves(ya_ref[...], yb_ref[...], ga_ref[...], gb_ref[...], goa_ref[...], gob_ref[...])
        y = jnp.concatenate([y_a, y_b], axis=1).astype(BF16)
        o_ref[...] = h_ref[...] + _dot(y, w_ref[...])

    return _pallas(
        body, grid=(cfg.TP // tm,), name="out_fwd",
        in_specs=[pl.BlockSpec((tm, 512), lambda i: (i, 0)), pl.BlockSpec((tm, 512), lambda i: (i, 0)),
                  pl.BlockSpec((tm, 512), lambda i: (i, 0)), pl.BlockSpec((tm, 512), lambda i: (i, 1)),
                  pl.BlockSpec((1, 512), lambda i: (0, 0)), pl.BlockSpec((1, 512), lambda i: (0, 0)),
                  pl.BlockSpec((D_MODEL, D_MODEL), lambda i: (0, 0)), pl.BlockSpec((tm, D_MODEL), lambda i: (i, 0))],
        out_specs=pl.BlockSpec((tm, D_MODEL), lambda i: (i, 0)),
        out_shape=jax.ShapeDtypeStruct((cfg.TP, D_MODEL), F32),
        compiler_params=_cp("parallel"),
    )(ya, yb, pf, pf, goa, gob, wo_p, h)


def _bias_build(table, bucket, maskadd, comm=None):
    def body(tab_ref, bk_ref, ma_ref, o_ref):
        def rows(g, carry):
            r = pl.ds(pl.multiple_of(g * 8, 8), 8)
            bk = bk_ref[0, r, :]
            accs = [jnp.zeros(bk.shape, F32)] * A_HEADS
            for b in range(N_BUCKETS):
                hit = bk == b
                accs = [jnp.where(hit, tab_ref[b, h], accs[h]) for h in range(A_HEADS)]
            ma = ma_ref[0, r, :]
            for h in range(A_HEADS):
                o_ref[0, h, r, :] = accs[h] + ma
            return carry

        lax.fori_loop(0, BLK // 8, rows, 0)

    return _call_with_comm(
        body, comm, grid=(4,), name="bias_build",
        in_specs=[pl.BlockSpec(memory_space=pltpu.SMEM), pl.BlockSpec((1, BLK, 512), lambda t: (t, 0, 0)),
                  pl.BlockSpec((1, BLK, 512), lambda t: (t, 0, 0))],
        out_specs=[pl.BlockSpec((1, A_HEADS, BLK, 512), lambda t: (t, 0, 0, 0))],
        out_shape=[jax.ShapeDtypeStruct((4, A_HEADS, BLK, 512), F32)],
        args=(table, bucket, maskadd))


def _bias_grad(s_accs, bucket, comm=None):
    depth = len(s_accs)

    def body(*refs):
        s_refs, bk_ref, o_ref, sum_ref, part_ref = refs[:depth], refs[depth], refs[depth + 1], refs[depth + 2], refs[depth + 3]
        t = pl.program_id(0)

        @pl.when(t == 0)
        def _():
            o_ref[...] = jnp.zeros_like(o_ref)

        total = s_refs[0][0]
        for extra in s_refs[1:]:
            total = total + extra[0]
        sum_ref[...] = total

        def step(b, carry):
            accs = [jnp.zeros((8, 512), F32) for _ in range(A_HEADS)]
            for g in range(BLK // 8):
                rows = pl.ds(g * 8, 8)
                hit = bk_ref[0, rows, :] == b
                for h in range(A_HEADS):
                    accs[h] = accs[h] + jnp.where(hit, sum_ref[h, rows, :], 0.0)
            rows8 = jnp.concatenate([jnp.sum(a, axis=0, keepdims=True) for a in accs], axis=0)
            part_ref[pl.ds(pl.multiple_of(b * A_HEADS, 8), A_HEADS), :] = rows8
            return carry

        lax.fori_loop(0, N_BUCKETS, step, 0)
        o_ref[...] += jnp.broadcast_to(jnp.sum(part_ref[...], axis=1, keepdims=True), o_ref.shape)

    s_spec = pl.BlockSpec((1, A_HEADS, BLK, 512), lambda t: (t, 0, 0, 0))
    return _call_with_comm(
        body, comm, grid=(4,), name="bias_grad",
        in_specs=[s_spec] * depth + [pl.BlockSpec((1, BLK, 512), lambda t: (t, 0, 0))],
        out_specs=[pl.BlockSpec((N_BUCKETS * A_HEADS, 128), lambda t: (0, 0))],
        out_shape=[jax.ShapeDtypeStruct((N_BUCKETS * A_HEADS, 128), F32)],
        scratch_shapes=[pltpu.VMEM((A_HEADS, BLK, 512), F32), pltpu.VMEM((N_BUCKETS * A_HEADS, 512), F32)],
        args=(*s_accs, bucket))


def _win_specs(cfg):
    nj = cfg.NJ
    row = lambda b, j: b * nj + j
    jt = lambda j: jnp.where(j == 0, 0, jnp.where(j == 1, 1, jnp.where(j == nj - 1, 3, 2)))
    slot_rows = [lambda b, j: row(b, 0), lambda b, j: row(b, jnp.maximum(j - 1, 0)), lambda b, j: row(b, j),
                 lambda b, j: row(b, jnp.minimum(j + 1, nj - 1))]
    k_specs = [pl.BlockSpec((BLK, 128), functools.partial(lambda b, j, f: (f(b, j), 4), f=f)) for f in slot_rows]
    v_specs = [pl.BlockSpec((BLK, 128), functools.partial(lambda b, j, f: (f(b, j), 5), f=f)) for f in slot_rows]
    q_spec = pl.BlockSpec((BLK, 512), lambda b, j: (row(b, j), 0))
    bias_spec = pl.BlockSpec((1, A_HEADS, BLK, 512), lambda b, j: (jt(j), 0, 0, 0))
    return row, jt, q_spec, k_specs, v_specs, bias_spec


def _stack4(ref):
    return jnp.concatenate([ref[:, c * 128:(c + 1) * 128] for c in range(4)], axis=0)


def _win_keys(k_refs, v_refs):
    k4 = jnp.concatenate([r[...] for r in k_refs], axis=0)
    v4 = jnp.concatenate([r[...] for r in v_refs], axis=0)
    lane_k = _lane(k4.shape)
    return (jnp.where(lane_k < 64, k4, jnp.zeros_like(k4)), jnp.where(lane_k >= 64, k4, jnp.zeros_like(k4))), v4


def _sink_col(sink_ref, hf):
    rowi = lax.broadcasted_iota(jnp.int32, (4 * BLK, 1), 0)
    col = jnp.full((4 * BLK, 1), sink_ref[4 * hf + 3], F32)
    for c in (2, 1, 0):
        col = jnp.where(rowi < (c + 1) * BLK, sink_ref[4 * hf + c], col)
    return col


def _win_fwd(cfg, pa, bias, sink):
    row, jt, q_spec, k_specs, v_specs, bias_spec = _win_specs(cfg)
    scale = A_DH ** -0.5

    def body(sink_ref, q_ref, k0, k1, k2, k3, v0, v1, v2, v3, b_ref, o_ref, lse_ref):
        kk, v4 = _win_keys((k0, k1, k2, k3), (v0, v1, v2, v3))
        qs = _stack4(q_ref)
        lane_o = _lane((4 * BLK, 128))
        outs, lses = [], []
        for hf in range(2):
            s = _dot_nt(qs, kk[hf]) * scale + b_ref[0, 4 * hf:4 * hf + 4].reshape(4 * BLK, 512)
            sink_col = _sink_col(sink_ref, hf)
            m = jnp.maximum(jnp.max(s, axis=1, keepdims=True), sink_col)
            e = jnp.exp(s - m)
            den = jnp.sum(e, axis=1, keepdims=True) + jnp.exp(sink_col - m)
            outs.append(_dot(e.astype(BF16), v4) / den)
            lses.append(m + jnp.log(den))
        o = jnp.where(lane_o < 64, outs[0], outs[1])
        for c in range(4):
            o_ref[:, c * 128:(c + 1) * 128] = o[c * BLK:(c + 1) * BLK]
        lse_ref[...] = jnp.where(lane_o == 0, lses[0], jnp.where(lane_o == 1, lses[1], 0.0))

    return _pallas(
        body, grid=(cfg.B, cfg.NJ), name="win_fwd",
        in_specs=[pl.BlockSpec(memory_space=pltpu.SMEM), q_spec, *k_specs, *v_specs, bias_spec],
        out_specs=[pl.BlockSpec((BLK, 512), lambda b, j: (row(b, j), 0)), pl.BlockSpec((4 * BLK, 128), lambda b, j: (row(b, j), 0))],
        out_shape=[jax.ShapeDtypeStruct((cfg.TP, 512), F32), jax.ShapeDtypeStruct((4 * cfg.TP, 128), F32)],
        compiler_params=_cp("parallel", "parallel"),
    )(sink, pa, *([pa] * 8), bias)


def _win_bwd(cfg, pa, bias, sink, dya, ya, lse):
    row, jt, q_spec, k_specs, v_specs, bias_spec = _win_specs(cfg)
    nj = cfg.NJ
    scale = A_DH ** -0.5

    def body(sink_ref, q_ref, k0, k1, k2, k3, v0, v1, v2, v3, b_ref, dy_ref, y_ref, lse_ref,
             dq_ref, dkp_ref, dvp_ref, dkm_ref, dvm_ref, s_ref, dsink_ref):
        j = pl.program_id(1)
        kind = jt(j)

        @pl.when((pl.program_id(0) == 0) & (j == 0))
        def _():
            s_ref[...] = jnp.zeros_like(s_ref)

        kk, v4 = _win_keys((k0, k1, k2, k3), (v0, v1, v2, v3))
        qs, dys, ys = _stack4(q_ref), _stack4(dy_ref), _stack4(y_ref)
        lane_o = _lane((4 * BLK, 128))
        half = (lane_o < 64, lane_o >= 64)
        lse_blk = lse_ref[...]
        dq = jnp.zeros((4 * BLK, 128), F32)
        dk4 = jnp.zeros((512, 128), F32)
        dv4 = jnp.zeros((512, 128), F32)
        dsink = jnp.zeros((8, 128), F32)
        lane_s = _lane((8, 128))
        row_s = lax.broadcasted_iota(jnp.int32, (8, 128), 0)
        for hf in range(2):
            lse_h = jnp.sum(jnp.where(lane_o == hf, lse_blk, 0.0), axis=1, keepdims=True)
            s = _dot_nt(qs, kk[hf]) * scale + b_ref[0, 4 * hf:4 * hf + 4].reshape(4 * BLK, 512)
            p = jnp.exp(s - lse_h)
            do_h = jnp.where(half[hf], dys, 0.0)
            delta = jnp.sum(do_h * ys, axis=1, keepdims=True)
            do_b = do_h.astype(BF16)
            ds = p * (_dot_nt(do_b, v4) - delta)
            s_ref[kind, 4 * hf:4 * hf + 4] += ds.reshape(4, BLK, 512)
            sink_grad = jnp.exp(_sink_col(sink_ref, hf) - lse_h) * delta
            for c in range(4):
                tot = -jnp.sum(sink_grad[c * BLK:(c + 1) * BLK])
                dsink = jnp.where((row_s == 0) & (lane_s == 4 * hf + c), tot, dsink)
            dsb = (ds * scale).astype(BF16)
            dq = dq + _dot(dsb, kk[hf])
            dk4 = dk4 + _dot_tn(dsb, jnp.where(half[hf], qs, jnp.zeros_like(qs)))
            dv4 = dv4 + _dot_tn(p.astype(BF16), do_b)
        for c in range(4):
            dq_ref[:, c * 128:(c + 1) * 128] = dq[c * BLK:(c + 1) * BLK].astype(BF16)
        dkp_ref[0] = dk4
        dvp_ref[0] = dv4

        @pl.when(j == 0)
        def _():
            dkm_ref[...] = dk4[:BLK]
            dvm_ref[...] = dv4[:BLK]

        @pl.when(j > 0)
        def _():
            dkm_ref[...] += dk4[:BLK]
            dvm_ref[...] += dv4[:BLK]

        @pl.when((pl.program_id(0) == 0) & (j == 0))
        def _():
            dsink_ref[...] = dsink

        @pl.when((pl.program_id(0) > 0) | (j > 0))
        def _():
            dsink_ref[...] += dsink

    blk_row = pl.BlockSpec((BLK, 512), lambda b, j: (row(b, j), 0))
    return _pallas(
        body, grid=(cfg.B, nj), name="win_bwd",
        in_specs=[pl.BlockSpec(memory_space=pltpu.SMEM), q_spec, *k_specs, *v_specs, bias_spec, blk_row, blk_row,
                  pl.BlockSpec((4 * BLK, 128), lambda b, j: (row(b, j), 0))],
        out_specs=[blk_row,
                   pl.BlockSpec((1, 512, 128), lambda b, j: (row(b, j), 0, 0)), pl.BlockSpec((1, 512, 128), lambda b, j: (row(b, j), 0, 0)),
                   pl.BlockSpec((BLK, 128), lambda b, j: (b, 0)), pl.BlockSpec((BLK, 128), lambda b, j: (b, 0)),
                   pl.BlockSpec((4, A_HEADS, BLK, 512), lambda b, j: (0, 0, 0, 0)),
                   pl.BlockSpec((8, 128), lambda b, j: (0, 0))],
        out_shape=[jax.ShapeDtypeStruct((cfg.TP, 512), BF16),
                   jax.ShapeDtypeStruct((cfg.B * nj, 512, 128), F32), jax.ShapeDtypeStruct((cfg.B * nj, 512, 128), F32),
                   jax.ShapeDtypeStruct((cfg.B * BLK, 128), F32), jax.ShapeDtypeStruct((cfg.B * BLK, 128), F32),
                   jax.ShapeDtypeStruct((4, A_HEADS, BLK, 512), F32),
                   jax.ShapeDtypeStruct((8, 128), F32)],
        compiler_params=_cp("arbitrary", "arbitrary"),
    )(sink, pa, *([pa] * 8), bias, dya, ya, lse)


def _win_dkv_combine(cfg, dkp, dvp, dkm, dvm):
    nj = cfg.NJ

    def body(kp, vp, km, vm, o_ref):
        o_ref[:BLK, :128] = km[...].astype(BF16)
        o_ref[:BLK, 128:] = vm[...].astype(BF16)
        for kb in range(1, nj):
            for col, part in ((0, kp), (128, vp)):
                tot = part[kb, 2 * BLK:3 * BLK] + part[kb - 1, 3 * BLK:4 * BLK]
                if kb + 1 < nj:
                    tot = tot + part[kb + 1, BLK:2 * BLK]
                o_ref[kb * BLK:(kb + 1) * BLK, col:col + 128] = tot.astype(BF16)

    return _pallas(
        body, grid=(cfg.B,), name="win_dkv_combine",
        in_specs=[pl.BlockSpec((nj, 512, 128), lambda b: (b, 0, 0)), pl.BlockSpec((nj, 512, 128), lambda b: (b, 0, 0)),
                  pl.BlockSpec((BLK, 128), lambda b: (b, 0)), pl.BlockSpec((BLK, 128), lambda b: (b, 0))],
        out_specs=pl.BlockSpec((cfg.LP, 256), lambda b: (b, 0)),
        out_shape=jax.ShapeDtypeStruct((cfg.TP, 256), BF16),
        compiler_params=_cp("parallel"),
    )(dkp, dvp, dkm, dvm)


def _pair_blockdiag(q):
    lane = _lane(q.shape)
    return jnp.concatenate([jnp.where(lane < 128, q, jnp.zeros_like(q)), jnp.where(lane >= 128, q, jnp.zeros_like(q))], axis=0)


def _mla_fwd(cfg, q, kt, v, comm=None):
    nj, lp = cfg.NJ, cfg.LP

    def body(q_ref, kt_ref, v_ref, o_ref, lse_ref, s_even, s_odd):
        i = pl.program_id(2)
        lane_o = _lane((BLK, 128))

        def logits(s_write):
            s_write[...] = _dot(_pair_blockdiag(q_ref[...]), kt_ref[...])

        def finish(s_read):
            s = s_read[...]
            m = jnp.max(s, axis=1, keepdims=True)
            e = jnp.exp2(s - m)
            den = jnp.sum(e, axis=1, keepdims=True)
            pv = _dot(e.astype(BF16), v_ref[...]) / den
            o_ref[...] = jnp.where(lane_o < 64, pv[:BLK], pv[BLK:])
            lse_ref[0] = jnp.broadcast_to(m + jnp.log2(den), (2 * BLK, 128))

        odd = i % 2 == 1

        @pl.when(i == 0)
        def _():
            logits(s_even)

        @pl.when(odd & (i < nj))
        def _():
            logits(s_odd)
            finish(s_even)

        @pl.when(jnp.logical_not(odd) & (i > 0) & (i < nj))
        def _():
            logits(s_even)
            finish(s_odd)

        @pl.when(i == nj)
        def _():
            finish(s_even if nj % 2 == 1 else s_odd)

    cur = lambda b, i: b * nj + jnp.minimum(i, nj - 1)
    prev = lambda b, i: b * nj + jnp.maximum(i - 1, 0)
    return _call_with_comm(
        body, comm, grid=(cfg.B, 4, nj + 1), name="mla_fwd",
        in_specs=[pl.BlockSpec((BLK, 256), lambda b, p, i: (cur(b, i), p)), pl.BlockSpec((256, lp), lambda b, p, i: (b * 4 + p, 0)),
                  pl.BlockSpec((lp, 128), lambda b, p, i: (b, p))],
        out_specs=[pl.BlockSpec((BLK, 128), lambda b, p, i: (prev(b, i), p)),
                   pl.BlockSpec((1, 2 * BLK, 128), lambda b, p, i: (p, prev(b, i), 0))],
        out_shape=[jax.ShapeDtypeStruct((cfg.TP, 512), F32), jax.ShapeDtypeStruct((4, 2 * cfg.TP, 128), F32)],
        scratch_shapes=[pltpu.VMEM((2 * BLK, lp), F32), pltpu.VMEM((2 * BLK, lp), F32)],
        args=(q, kt, v))


def _mla_bwd(cfg, q, k, kt, vt, dyb, yb, lse, comm=None):
    nj, lp = cfg.NJ, cfg.LP

    def body(q_ref, k_ref, kt_ref, vt_ref, dy_ref, y_ref, lse_ref, dq_ref, dk_ref, dv_ref):
        i = pl.program_id(2)

        @pl.when(i == 0)
        def _():
            dk_ref[...] = jnp.zeros_like(dk_ref)
            dv_ref[...] = jnp.zeros_like(dv_ref)

        lane_o = _lane((BLK, 128))
        qbd = _pair_blockdiag(q_ref[...])
        dy, y = dy_ref[...], y_ref[...]
        do_s = jnp.concatenate([jnp.where(lane_o < 64, dy, 0.0), jnp.where(lane_o >= 64, dy, 0.0)], axis=0)
        delta = jnp.sum(do_s * jnp.concatenate([y, y], axis=0), axis=1, keepdims=True)
        do_b = do_s.astype(BF16)
        p = jnp.exp2(_dot(qbd, kt_ref[...]) - lse_ref[0][:, :1])
        ds = p * (_dot(do_b, vt_ref[...]) - delta)
        dsb = (ds * LN2).astype(BF16)
        dq2 = _dot(dsb, k_ref[...])
        dq_ref[...] = jnp.where(_lane((BLK, 256)) < 128, dq2[:BLK], dq2[BLK:]) * Q_SCALE
        dk_ref[...] += _dot_tn(dsb, qbd)
        dv_ref[...] += _dot_tn(p.astype(BF16), do_b)

    return _call_with_comm(
        body, comm, grid=(cfg.B, 4, nj), name="mla_bwd",
        in_specs=[pl.BlockSpec((BLK, 256), lambda b, p, i: (b * nj + i, p)), pl.BlockSpec((lp, 256), lambda b, p, i: (b, p)),
                  pl.BlockSpec((256, lp), lambda b, p, i: (b * 4 + p, 0)), pl.BlockSpec((128, lp), lambda b, p, i: (b * 4 + p, 0)),
                  pl.BlockSpec((BLK, 128), lambda b, p, i: (b * nj + i, p)), pl.BlockSpec((BLK, 128), lambda b, p, i: (b * nj + i, p)),
                  pl.BlockSpec((1, 2 * BLK, 128), lambda b, p, i: (p, b * nj + i, 0))],
        out_specs=[pl.BlockSpec((BLK, 256), lambda b, p, i: (b * nj + i, p)), pl.BlockSpec((lp, 256), lambda b, p, i: (b, p)),
                   pl.BlockSpec((lp, 128), lambda b, p, i: (b, p))],
        out_shape=[jax.ShapeDtypeStruct((cfg.TP, 1024), F32), jax.ShapeDtypeStruct((cfg.TP, 1024), F32),
                   jax.ShapeDtypeStruct((cfg.TP, 512), F32)],
        args=(q, k, kt, vt, dyb, yb, lse))


def _loss_bwd(cfg, h, target, gf):
    nj, nb = cfg.NJ, cfg.NB

    def body(h_ref, t_ref, g_ref, dh_ref, loss_ref, dg_ref):
        b, j = pl.program_id(0), pl.program_id(1)

        @pl.when((b == 0) & (j == 0))
        def _():
            loss_ref[...] = jnp.zeros_like(loss_ref)
            dg_ref[...] = jnp.zeros_like(dg_ref)

        @pl.when(j == 0)
        def _():
            dh_ref[...] = jnp.zeros_like(dh_ref)

        @pl.when(j > 0)
        def _():
            g = g_ref[...]
            xh, r = _rms(h_ref[...])
            err = xh * g - t_ref[...]
            loss_ref[...] += jnp.where((lax.broadcasted_iota(jnp.int32, (8, 128), 0) == 0) & (_lane((8, 128)) == 0),
                                       (0.5 / D_MODEL) * jnp.sum(err * err), 0.0)
            dy = err * (1.0 / D_MODEL)
            dg_ref[...] += jnp.sum(dy * xh, axis=0, keepdims=True)
            dh_ref[...] = _rms_bwd(xh, r, dy * g)

    return _pallas(
        body, grid=(cfg.B, nj), name="loss_bwd",
        in_specs=[pl.BlockSpec((BLK, D_MODEL), lambda b, j: (b * nj + j, 0)),
                  pl.BlockSpec((BLK, D_MODEL), lambda b, j: (b * nb + jnp.maximum(j - 1, 0), 0)),
                  pl.BlockSpec((1, D_MODEL), lambda b, j: (0, 0))],
        out_specs=[pl.BlockSpec((BLK, D_MODEL), lambda b, j: (b * nj + j, 0)), pl.BlockSpec((8, 128), lambda b, j: (0, 0)),
                   pl.BlockSpec((1, D_MODEL), lambda b, j: (0, 0))],
        out_shape=[jax.ShapeDtypeStruct((cfg.TP, D_MODEL), F32), jax.ShapeDtypeStruct((8, 128), F32),
                   jax.ShapeDtypeStruct((1, D_MODEL), F32)],
        compiler_params=_cp("arbitrary", "arbitrary"),
    )(h, target, gf)


def _out_bwd(cfg, dh, ya, yb, pf, goa, gob, wo_p):
    tm = 256

    def body(dh_ref, ya_ref, yb_ref, ga_ref, gb_ref, goa_ref, gob_ref, w_ref,
             dya_ref, dyb_ref, dg_ref, dw_ref, dgoa_ref, dgob_ref):
        @pl.when(pl.program_id(0) == 0)
        def _():
            dw_ref[...] = jnp.zeros_like(dw_ref)
            dgoa_ref[...] = jnp.zeros_like(dgoa_ref)
            dgob_ref[...] = jnp.zeros_like(dgob_ref)

        ga, gb, goa, gob = ga_ref[...], gb_ref[...], goa_ref[...], gob_ref[...]
        xa, ra, xb, rb, sga, sgb, y_a, y_b = _gate_halves(ya_ref[...], yb_ref[...], ga, gb, goa, gob)
        dhb = dh_ref[...].astype(BF16)
        dw_ref[...] += _dot_tn(jnp.concatenate([y_a, y_b], axis=1).astype(BF16), dhb)
        dy = _dot_nt(dhb, w_ref[...])
        for (dyh, x, r, g, sg, go, dy_out, dgo_ref, col) in (
                (dy[:, :512], xa, ra, ga, sga, goa, dya_ref, dgoa_ref, 0), (dy[:, 512:], xb, rb, gb, sgb, gob, dyb_ref, dgob_ref, 512)):
            dn = dyh * (g * sg)
            dg_ref[:, col:col + 512] = (dyh * (x * go) * (sg * (1.0 + g * (1.0 - sg)))).astype(BF16)
            dgo_ref[...] += jnp.sum(dn * x, axis=0, keepdims=True)
            dy_out[...] = _rms_bwd(x, r, dn * go)

    half = lambda c: pl.BlockSpec((tm, 512), lambda i: (i, c))
    vec = pl.BlockSpec((1, 512), lambda i: (0, 0))
    return _pallas(
        body, grid=(cfg.TP // tm,), name="out_bwd",
        in_specs=[pl.BlockSpec((tm, D_MODEL), lambda i: (i, 0)), half(0), half(0), half(0), half(1), vec, vec,
                  pl.BlockSpec((D_MODEL, D_MODEL), lambda i: (0, 0))],
        out_specs=[half(0), half(0), pl.BlockSpec((tm, D_MODEL), lambda i: (i, 0)),
                   pl.BlockSpec((D_MODEL, D_MODEL), lambda i: (0, 0)), vec, vec],
        out_shape=[jax.ShapeDtypeStruct((cfg.TP, 512), F32), jax.ShapeDtypeStruct((cfg.TP, 512), F32),
                   jax.ShapeDtypeStruct((cfg.TP, D_MODEL), BF16), jax.ShapeDtypeStruct((D_MODEL, D_MODEL), F32),
                   jax.ShapeDtypeStruct((1, 512), F32), jax.ShapeDtypeStruct((1, 512), F32)],
        compiler_params=_cp("arbitrary"),
    )(dh, ya, yb, pf, pf, goa, gob, wo_p)


def _lat_bwd(cfg, dq, dk, dv, pf, gq, gkv, wq_p, wkv_p, c_tab, s_tab):
    nj = cfg.NJ

    def body(dq_ref, dk_ref, dv_ref, cq_ref, ckv_ref, gq_ref, gkv_ref, wq_ref, wkv_ref, c_ref, s_ref,
             dl_ref, dwq_ref, dwkv_ref, dgq_ref, dgkv_ref):
        @pl.when((pl.program_id(0) == 0) & (pl.program_id(1) == 0))
        def _():
            dwq_ref[...] = jnp.zeros_like(dwq_ref)
            dwkv_ref[...] = jnp.zeros_like(dwkv_ref)
            dgq_ref[...] = jnp.zeros_like(dgq_ref)
            dgkv_ref[...] = jnp.zeros_like(dgkv_ref)

        c1, s1 = c_ref[...], s_ref[...]
        c8, s8 = jnp.tile(c1, (1, 8)), jnp.tile(s1, (1, 8))
        dq_r = dq_ref[...]
        dqp = (dq_r * c8 + _swap_rope(dq_r * s8)).astype(BF16)
        gq = gq_ref[...]
        xq, rq = _rms(cq_ref[...])
        dwq_ref[...] += _dot_tn((xq * gq).astype(BF16), dqp)
        dn = _dot_nt(dqp, wq_ref[...])
        dgq_ref[...] += jnp.sum(dn * xq, axis=0, keepdims=True)
        dl_ref[:, :256] = _rms_bwd(xq, rq, dn * gq).astype(BF16)

        dk_r = dk_ref[...]
        dkr = dk_r[:, :128]
        for hd in range(1, 8):
            dkr = dkr + dk_r[:, hd * 128:(hd + 1) * 128]
        lane1 = _lane(dkr.shape)
        dkr = jnp.where((lane1 >= 64) & (lane1 < 96), dkr, 0.0)
        dl_ref[:, 384:] = (dkr * c1 + _swap_rope(dkr * s1)).astype(BF16)
        dkv = jnp.concatenate([dk_r, dv_ref[...]], axis=1).astype(BF16)
        gkv = gkv_ref[...]
        xk, rk = _rms(ckv_ref[...])
        dwkv_ref[...] += _dot_tn((xk * gkv).astype(BF16), dkv)
        dn2 = _dot_nt(dkv, wkv_ref[...])
        dgkv_ref[...] += jnp.sum(dn2 * xk, axis=0, keepdims=True)
        dl_ref[:, 256:384] = _rms_bwd(xk, rk, dn2 * gkv).astype(BF16)

    row = lambda b, j: b * nj + j
    const = lambda shape: pl.BlockSpec(shape, lambda b, j: (0, 0))
    return _pallas(
        body, grid=(cfg.B, nj), name="lat_bwd",
        in_specs=[pl.BlockSpec((BLK, 1024), lambda b, j: (row(b, j), 0)), pl.BlockSpec((BLK, 1024), lambda b, j: (row(b, j), 0)),
                  pl.BlockSpec((BLK, 512), lambda b, j: (row(b, j), 0)),
                  pl.BlockSpec((BLK, 256), lambda b, j: (row(b, j), 4)), pl.BlockSpec((BLK, 128), lambda b, j: (row(b, j), 10)),
                  const((1, 256)), const((1, 128)), const((256, 1024)), const((128, 1536)),
                  pl.BlockSpec((BLK, 128), lambda b, j: (j, 0)), pl.BlockSpec((BLK, 128), lambda b, j: (j, 0))],
        out_specs=[pl.BlockSpec((BLK, 512), lambda b, j: (row(b, j), 0)), const((256, 1024)), const((128, 1536)),
                   const((1, 256)), const((1, 128))],
        out_shape=[jax.ShapeDtypeStruct((cfg.TP, 512), BF16), jax.ShapeDtypeStruct((256, 1024), F32),
                   jax.ShapeDtypeStruct((128, 1536), F32), jax.ShapeDtypeStruct((1, 256), F32), jax.ShapeDtypeStruct((1, 128), F32)],
        compiler_params=_cp("arbitrary", "arbitrary"),
    )(dq, dk, dv, pf, pf, gq, gkv, wq_p, wkv_p, c_tab, s_tab)


def _inproj_bwd(cfg, h, g, w_p, dqa, dkva, dgate, dlat, dh, comm=None):
    tm = 256

    def body(h_ref, g_ref, w_ref, dqa_ref, dkva_ref, dg_ref, dl_ref, dh_ref, o_ref, dw_ref, dgn_ref):
        @pl.when(pl.program_id(0) == 0)
        def _():
            dw_ref[...] = jnp.zeros_like(dw_ref)
            dgn_ref[...] = jnp.zeros_like(dgn_ref)

        g = g_ref[...]
        xh, r = _rms(h_ref[...])
        dproj = jnp.concatenate([dqa_ref[...], dkva_ref[...], dg_ref[...], dl_ref[...]], axis=1)
        dw_ref[...] += _dot_tn((xh * g).astype(BF16), dproj)
        du = _dot_nt(dproj, w_ref[...])
        dgn_ref[...] += jnp.sum(du * xh, axis=0, keepdims=True)
        o_ref[...] = dh_ref[...] + _rms_bwd(xh, r, du * g)

    rows = lambda w: pl.BlockSpec((tm, w), lambda i: (i, 0))
    return _call_with_comm(
        body, comm, grid=(cfg.TP // tm,), name="inproj_bwd",
        in_specs=[rows(D_MODEL), pl.BlockSpec((1, D_MODEL), lambda i: (0, 0)), pl.BlockSpec((D_MODEL, W_IN_P), lambda i: (0, 0)),
                  rows(512), rows(256), rows(1024), rows(512), rows(D_MODEL)],
        out_specs=[rows(D_MODEL), pl.BlockSpec((D_MODEL, W_IN_P), lambda i: (0, 0)), pl.BlockSpec((1, D_MODEL), lambda i: (0, 0))],
        out_shape=[jax.ShapeDtypeStruct((cfg.TP, D_MODEL), F32), jax.ShapeDtypeStruct((D_MODEL, W_IN_P), F32),
                   jax.ShapeDtypeStruct((1, D_MODEL), F32)],
        args=(h, g, w_p, dqa, dkva, dgate, dlat, dh))


def _meta_grad(cfg, dh):
    def body(d_ref, o_ref):
        @pl.when(pl.program_id(0) == 0)
        def _():
            o_ref[...] = d_ref[...]

        @pl.when(pl.program_id(0) > 0)
        def _():
            o_ref[...] += d_ref[...]

    return _pallas(
        body, grid=(cfg.B,), name="meta_grad",
        in_specs=[pl.BlockSpec((BLK, D_MODEL), lambda b: (b * cfg.NJ, 0))],
        out_specs=pl.BlockSpec((BLK, D_MODEL), lambda b: (0, 0)),
        out_shape=jax.ShapeDtypeStruct((BLK, D_MODEL), F32),
        compiler_params=_cp("arbitrary"),
    )(dh)


MATRICES = ("w_in", "w_uq", "w_ukv", "w_out")


def _local_grads(cfg, x, target, meta_of, table, small, weight_of, rider=None):
    def ride(stage, i, mats):
        hook = rider(stage, i, mats) if rider else None
        return hook if hook else (None, lambda res: None)

    depth = small["norm_in"].shape[0]
    rel, vis = _window_structure(cfg.NJ)
    bucket = _t5_bucket(jnp.asarray(rel))
    maskadd = jnp.asarray(np.where(vis, 0.0, NEG).astype(np.float32))
    c_tab, s_tab = _rope_tables(cfg)
    comm, deliver = ride("bias_build", 0, {})
    bias, *travelled = _bias_build(table, bucket, maskadd, comm)
    deliver(travelled)

    meta_blk = jnp.concatenate([meta_of(), jnp.zeros((BLK - N_META, D_MODEL), F32)], axis=0)
    h = jnp.concatenate([jnp.broadcast_to(meta_blk[None], (cfg.B, BLK, D_MODEL)), x], axis=1).reshape(cfg.TP, D_MODEL)

    wp, saved = [], []
    for i in range(depth):
        w = dict(w_in=_w_in_to_p(weight_of(i, "w_in")),
                 g_in=small["norm_in"][i][None], gq=small["norm_q_lat"][i][None], gkv=small["norm_kv_lat"][i][None],
                 goa=_perm_heads64(small["norm_out_a"][i], 0)[None], gob=small["norm_out_b"][i][None], sink=small["sink_a"][i])
        wp.append(w)
        comm, deliver = ride("inproj_fwd", i, {})
        pa, pf, *travelled = _inproj_fwd(cfg, h, w["g_in"], w["w_in"], comm)
        deliver(travelled)
        w.update(w_uq=_w_uq_to_p(weight_of(i, "w_uq")), w_ukv=_w_ukv_to_p(weight_of(i, "w_ukv")), w_out=_w_out_to_p(weight_of(i, "w_out")))
        q, k, v, kt, vt = _lat_fwd(cfg, pf, w["gq"], w["gkv"], w["w_uq"], w["w_ukv"], c_tab, s_tab)
        ya, lse_a = _win_fwd(cfg, pa, bias, w["sink"])
        comm, deliver = ride("mla_fwd", i, {})
        yb, lse_b, *travelled = _mla_fwd(cfg, q, kt, v, comm)
        deliver(travelled)
        h_next = _out_fwd(cfg, ya, yb, pf, w["goa"], w["gob"], w["w_out"], h)
        saved.append(dict(h=h, pa=pa, pf=pf, q=q, k=k, kt=kt, vt=vt, ya=ya, lse_a=lse_a, yb=yb, lse_b=lse_b))
        h = h_next

    dh, loss_tile, d_norm_final = _loss_bwd(cfg, h, target.reshape(cfg.B * cfg.S, D_MODEL), small["norm_final"][None])

    grads = {k_: [] for k_ in ("norm_in", "sink_a", "norm_q_lat", "norm_kv_lat", "norm_out_a", "norm_out_b")}
    mats, s_accs = {}, []
    for i in reversed(range(depth)):
        w, sv = wp[i], saved[i]
        dya, dyb, dgate, dwo, dgoa, dgob = _out_bwd(cfg, dh, sv["ya"], sv["yb"], sv["pf"], w["goa"], w["gob"], w["w_out"])
        dqa, dkp, dvp, dkm, dvm, s_acc, dsink = _win_bwd(cfg, sv["pa"], bias, w["sink"], dya, sv["ya"], sv["lse_a"])
        dkva = _win_dkv_combine(cfg, dkp, dvp, dkm, dvm)
        comm, deliver = ride("mla_bwd", i, mats)
        dq, dk, dv, *travelled = _mla_bwd(cfg, sv["q"], sv["k"], sv["kt"], sv["vt"], dyb, sv["yb"], sv["lse_b"], comm)
        deliver(travelled)
        dlat, dwq, dwkv, dgq, dgkv = _lat_bwd(cfg, dq, dk, dv, sv["pf"], w["gq"], w["gkv"], w["w_uq"], w["w_ukv"], c_tab, s_tab)
        mats[i] = dict(w_uq=_w_uq_from_p(dwq), w_ukv=_w_ukv_from_p(dwkv), w_out=_w_out_from_p(dwo))
        comm, deliver = ride("inproj_bwd", i, mats)
        dh, dwin, dgin, *travelled = _inproj_bwd(cfg, sv["h"], w["g_in"], w["w_in"], dqa, dkva, dgate, dlat, dh, comm)
        deliver(travelled)
        s_accs.append(s_acc)
        mats[i]["w_in"] = _w_in_from_p(dwin)
        grads["norm_in"].append(dgin[0])
        grads["sink_a"].append(dsink[0, :A_HEADS])
        grads["norm_q_lat"].append(dgq[0])
        grads["norm_kv_lat"].append(dgkv[0])
        grads["norm_out_a"].append(_unperm_heads64(dgoa[0], 0))
        grads["norm_out_b"].append(dgob[0])

    out = {k_: jnp.stack(v_[::-1]) for k_, v_ in grads.items()}
    mats["meta_tokens"] = _meta_grad(cfg, dh)[:N_META]
    comm, deliver = ride("bias_grad", 0, mats)
    dtable, *travelled = _bias_grad(s_accs, bucket, comm)
    deliver(travelled)
    out["rel_bias_table"] = dtable[:, 0].reshape(N_BUCKETS, A_HEADS)
    out["norm_final"] = d_norm_final[0]
    return loss_tile[0, 0], dh.reshape(cfg.B, cfg.LP, D_MODEL)[:, BLK:], out, mats


MESH = pl.DeviceIdType.MESH
ANY = pl.BlockSpec(memory_space=pl.ANY)


def _place():
    x, y, c = lax.axis_index("x"), lax.axis_index("y"), lax.axis_index("c")
    others = [(1 - x, y), (x, 1 - y), (1 - x, 1 - y)]
    return x, y, c, others


Comm = collections.namedtuple("Comm", "inputs out_shapes scratch start wait")


def _gather_comm(shards):
    n = len(shards)

    def copies(ins, outs, sems, arriving):
        send_sems, recv_sems, local_sems = sems
        x, y, c, others = _place()
        k_me = 2 * x + y
        local = [pltpu.make_async_copy(ins[a], outs[a].at[k_me], local_sems.at[a]) for a in range(n)]
        remote = [pltpu.make_async_remote_copy(src_ref=ins[a], dst_ref=outs[a].at[2 * ox + oy if arriving else k_me],
                                               send_sem=send_sems.at[3 * a + j], recv_sem=recv_sems.at[3 * a + j],
                                               device_id=(ox, oy, c), device_id_type=MESH)
                  for a in range(n) for j, (ox, oy) in enumerate(others)]
        return local, remote

    def start(ins, outs, sems):
        local, sends = copies(ins, outs, sems, arriving=False)
        for cp in local + sends:
            cp.start()

    def wait(ins, outs, sems):
        local, recvs = copies(ins, outs, sems, arriving=True)
        for cp in recvs:
            cp.wait_recv()
        for cp in recvs:
            cp.wait_send()
        for cp in local:
            cp.wait()

    return Comm(list(shards), [jax.ShapeDtypeStruct((4, *s.shape), s.dtype) for s in shards],
                [pltpu.SemaphoreType.DMA((3 * n,)), pltpu.SemaphoreType.DMA((3 * n,)), pltpu.SemaphoreType.DMA((n,))], start, wait)


def _scatter_comm(parts):
    n = len(parts)

    def copies(ins, outs, sems):
        send_sems, recv_sems = sems
        x, y, c, others = _place()
        return [pltpu.make_async_remote_copy(src_ref=ins[a].at[2 * ox + oy], dst_ref=outs[a].at[j], send_sem=send_sems.at[3 * a + j],
                                             recv_sem=recv_sems.at[3 * a + j], device_id=(ox, oy, c), device_id_type=MESH)
                for a in range(n) for j, (ox, oy) in enumerate(others)]

    def start(ins, outs, sems):
        for cp in copies(ins, outs, sems):
            cp.start()

    def wait(ins, outs, sems):
        cps = copies(ins, outs, sems)
        for cp in cps:
            cp.wait_recv()
        for cp in cps:
            cp.wait_send()

    return Comm(list(parts), [jax.ShapeDtypeStruct((3, *p.shape[1:]), p.dtype) for p in parts],
                [pltpu.SemaphoreType.DMA((3 * n,)), pltpu.SemaphoreType.DMA((3 * n,))], start, wait)


def _run_comm(comm, name):
    ni, no = len(comm.inputs), len(comm.out_shapes)

    def body(*refs):
        ins, outs, sems = refs[:ni], refs[ni:ni + no], refs[ni + no:]
        comm.start(ins, outs, sems)
        comm.wait(ins, outs, sems)

    return _pallas(body, name=name, in_specs=[ANY] * ni, out_specs=[ANY] * no, out_shape=comm.out_shapes,
                          scratch_shapes=comm.scratch)(*comm.inputs)


def _call_with_comm(body, comm, *, grid, name, in_specs, out_specs, out_shape, args, scratch_shapes=()):
    if comm is None:
        return _pallas(body, grid=grid, name=name, in_specs=in_specs, out_specs=out_specs, out_shape=out_shape,
                              scratch_shapes=list(scratch_shapes), compiler_params=_cp(*["arbitrary"] * len(grid)))(*args)
    n_in, n_out, ci, co, ns = len(in_specs), len(out_specs), len(comm.inputs), len(comm.out_shapes), len(scratch_shapes)

    def wrapped(*refs):
        ins, cins = refs[:n_in], refs[n_in:n_in + ci]
        outs, couts = refs[n_in + ci:n_in + ci + n_out], refs[n_in + ci + n_out:n_in + ci + n_out + co]
        scratch, sems = refs[n_in + ci + n_out + co:n_in + ci + n_out + co + ns], refs[n_in + ci + n_out + co + ns:]
        ids = [pl.program_id(a) for a in range(len(grid))]
        first = functools.reduce(jnp.logical_and, [i == 0 for i in ids])
        last = functools.reduce(jnp.logical_and, [i == g - 1 for i, g in zip(ids, grid)])

        @pl.when(first)
        def _():
            comm.start(cins, couts, sems)

        body(*ins, *outs, *scratch)

        @pl.when(last)
        def _():
            comm.wait(cins, couts, sems)

    return _pallas(
        wrapped, grid=grid, name=name + "_comm", in_specs=[*in_specs, *[ANY] * ci], out_specs=[*out_specs, *[ANY] * co],
        out_shape=[*out_shape, *comm.out_shapes], scratch_shapes=[*scratch_shapes, *comm.scratch],
        compiler_params=_cp(*["arbitrary"] * len(grid)))(*args, *comm.inputs)


def _swap_sibling(arrs):
    n = len(arrs)

    def body(*refs):
        ins, outs = refs[:n], refs[n:2 * n]
        send_sems, recv_sems = refs[2 * n:]
        x, y, c, _ = _place()
        copies = [pltpu.make_async_remote_copy(src_ref=ins[a], dst_ref=outs[a], send_sem=send_sems.at[a], recv_sem=recv_sems.at[a],
                                               device_id=(x, y, 1 - c), device_id_type=MESH) for a in range(n)]
        for cp in copies:
            cp.start()
        for cp in copies:
            cp.wait_recv()
        for cp in copies:
            cp.wait_send()

    return _pallas(
        body, name="swap_sibling", in_specs=[ANY] * n, out_specs=[ANY] * n,
        out_shape=[jax.ShapeDtypeStruct(a.shape, a.dtype) for a in arrs],
        scratch_shapes=[pltpu.SemaphoreType.DMA((n,)), pltpu.SemaphoreType.DMA((n,))],
    )(*arrs)


def _allreduce_small(v):
    def body(v_ref, o_ref, buf, send_sems, recv_sems):
        x, y, c, _ = _place()
        me = 4 * x + 2 * y + c
        buf[me] = v_ref[...]

        def copy(r):
            tx, ty, tc = (x + (r >> 2)) % 2, (y + ((r >> 1) & 1)) % 2, (c + (r & 1)) % 2
            return tx, ty, tc

        sends = []
        for r in range(1, 8):
            tx, ty, tc = copy(r)
            sends.append(pltpu.make_async_remote_copy(src_ref=v_ref, dst_ref=buf.at[me], send_sem=send_sems.at[r - 1],
                                                      recv_sem=recv_sems.at[r - 1], device_id=(tx, ty, tc), device_id_type=MESH))
        for cp in sends:
            cp.start()
        for r in range(1, 8):
            tx, ty, tc = copy(r)
            pltpu.make_async_remote_copy(src_ref=v_ref, dst_ref=buf.at[4 * tx + 2 * ty + tc], send_sem=send_sems.at[r - 1],
                                         recv_sem=recv_sems.at[r - 1], device_id=(tx, ty, tc), device_id_type=MESH).wait_recv()
        for cp in sends:
            cp.wait_send()
        acc = buf[0]
        for d in range(1, 8):
            acc = acc + buf[d]
        o_ref[...] = acc

    return pl.pallas_call(
        body, name="allreduce_small", in_specs=[pl.BlockSpec(memory_space=pltpu.VMEM)], out_specs=pl.BlockSpec(memory_space=pltpu.VMEM),
        out_shape=jax.ShapeDtypeStruct(v.shape, F32),
        scratch_shapes=[pltpu.VMEM((8, *v.shape), F32), pltpu.SemaphoreType.DMA((7,)), pltpu.SemaphoreType.DMA((7,))],
    )(v)


def _rows_view(a):
    return a.reshape(-1, a.shape[-1])


def _elementwise(name, fn, ins, n_out):
    rows, cols = ins[0].shape
    tm = min(rows, 256)
    spec = pl.BlockSpec((tm, cols), lambda i: (i, 0))

    def body(*refs):
        outs = fn(*[r[...] for r in refs[:len(ins)]])
        for o_ref, o in zip(refs[len(ins):], outs):
            o_ref[...] = o

    return _pallas(
        body, grid=(rows // tm,), name=name, in_specs=[spec] * len(ins), out_specs=[spec] * n_out,
        out_shape=[jax.ShapeDtypeStruct((rows, cols), F32)] * n_out, compiler_params=_cp("parallel"),
    )(*ins)


def _sum_parts(name, own, recv):
    def fn(o, r0, r1, r2):
        return (o + r0.astype(F32) + r1.astype(F32) + r2.astype(F32),)

    return _elementwise("sum_parts_" + name, fn, [own, recv[0], recv[1], recv[2]], 1)[0]


def _adamw(name, w, m, v, g_parts):
    def fn(w_, m_, v_, *gs):
        g = gs[0]
        for extra in gs[1:]:
            g = g + extra
        m_new = ADAM_B1 * m_ + (1.0 - ADAM_B1) * g
        v_new = ADAM_B2 * v_ + (1.0 - ADAM_B2) * (g * g)
        m_hat = m_new / (1.0 - ADAM_B1 ** ADAM_STEP)
        v_hat = v_new / (1.0 - ADAM_B2 ** ADAM_STEP)
        delta = -ADAM_LR * (m_hat / (jnp.sqrt(v_hat) + ADAM_EPS) + ADAM_WD * w_)
        return g, delta, m_new, v_new

    return _elementwise("adamw_" + name, fn, [w, m, v, *g_parts], 4)


MAT_AXIS = {"w_in": 1, "w_uq": 1, "w_ukv": 1, "w_out": 0}
SMALL = ("rel_bias_table", "norm_in", "sink_a", "norm_q_lat", "norm_kv_lat", "norm_out_a", "norm_out_b", "norm_final")
WEIGHTS = ("meta_tokens", "rel_bias_table", "norm_in", "w_in", "sink_a", "norm_q_lat", "w_uq", "norm_kv_lat", "w_ukv",
           "norm_out_a", "norm_out_b", "w_out", "norm_final")
SMALL_ROWS, SMALL_COLS = 8, 1024


def _pack_small(d, loss=None):
    flat = [d[n].reshape(-1) for n in SMALL]
    if loss is not None:
        flat.append(loss.reshape(1))
    used = sum(f.shape[0] for f in flat)
    flat.append(jnp.zeros((SMALL_ROWS * SMALL_COLS - used,), F32))
    return jnp.concatenate(flat).reshape(SMALL_ROWS, SMALL_COLS)


def _unpack_small(p, like):
    flat, out, off = p.reshape(-1), {}, 0
    for n in SMALL:
        size = int(np.prod(like[n].shape))
        out[n] = flat[off:off + size].reshape(like[n].shape)
        off += size
    return out, flat[off]


def _split4(a, axis):
    size = a.shape[axis] // 4
    return jnp.stack([lax.slice_in_dim(a, k * size, (k + 1) * size, axis=axis) for k in range(4)])


def _train_step(cfg, x, target, w, m, v):
    depth = w["w_in"].shape[0]
    rest = tuple(n for n in MATRICES if n != "w_in")
    weights, splits, received = {}, {}, {}

    def gather(i, names, also=()):
        def deliver(res):
            for n, g in zip(names, res):
                weights[i, n] = jnp.concatenate([g[k] for k in range(4)], axis=MAT_AXIS[n])

        return _gather_comm([w[n][i].astype(BF16) for n in names] + list(also)), deliver

    def scatter(i, names, mats, also=()):
        for n in names:
            splits[i, n] = _split4(mats[i][n], MAT_AXIS[n])

        def deliver(res):
            for n, r in zip(names, res):
                received[i, n] = r

        return _scatter_comm([splits[i, n].astype(BF16) for n in names] + list(also)), deliver

    def rider(stage, i, mats):
        if stage == "bias_build":
            comm, deliver = gather(0, ("w_in",), also=[w["meta_tokens"]])

            def deliver_first(res):
                deliver(res)
                weights["meta"] = jnp.concatenate([res[1][k] for k in range(4)], axis=1)

            return comm, deliver_first
        if stage == "inproj_fwd" and i == 0:
            return gather(0, rest)
        if stage == "mla_fwd" and i + 1 < depth:
            return gather(i + 1, MATRICES)
        if stage == "mla_bwd" and i + 1 < depth:
            return scatter(i + 1, MATRICES, mats)
        if stage == "inproj_bwd" and i == 0:
            return scatter(0, rest, mats)
        if stage == "bias_grad":
            splits["meta"] = _split4(mats["meta_tokens"], 1)
            comm, deliver = scatter(0, ("w_in",), mats, also=[splits["meta"].astype(BF16)])

            def deliver_last(res):
                deliver(res)
                received["meta"] = res[1]

            return comm, deliver_last
        return None

    loss_local, grad_x, g, mats = _local_grads(cfg, x, target, lambda: weights["meta"], w["rel_bias_table"], {n: w[n] for n in SMALL},
                                               lambda i, n: weights[i, n], rider)

    small_sum = _allreduce_small(_pack_small(g, loss_local))
    g_small, loss = _unpack_small(small_sum, {n: w[n] for n in SMALL})

    k_me = 2 * lax.axis_index("x") + lax.axis_index("y")

    def core_sum(name, split, recv):
        own = lax.dynamic_index_in_dim(split, k_me, 0, keepdims=False)
        return _sum_parts(name, _rows_view(own), recv.reshape(3, -1, recv.shape[-1]))

    partial = [core_sum("meta_tokens", splits["meta"], received["meta"])]
    for n in MATRICES:
        partial.append(jnp.concatenate([core_sum(f"{n}_{i}", splits[i, n], received[i, n]) for i in range(depth)], axis=0))
    sibling = _swap_sibling(partial)

    outs = {}
    for n, p_me, p_sib in zip(("meta_tokens", *MATRICES), partial, sibling):
        res = _adamw(n, _rows_view(w[n]), _rows_view(m[n]), _rows_view(v[n]), [p_me, p_sib])
        outs[n] = [r.reshape(w[n].shape) for r in res]
    res = _adamw("small", _pack_small(w), _pack_small(m), _pack_small(v), [_pack_small(g_small)])
    unpacked = [_unpack_small(r, {n: w[n] for n in SMALL})[0] for r in res]
    for n in SMALL:
        outs[n] = [u[n] for u in unpacked]

    result = [loss, grad_x]
    for field in range(4):
        result.extend(outs[n][field] for n in WEIGHTS)
    return tuple(result)


def kernel(x, meta_tokens, rel_bias_table, norm_in, w_in, sink_a, norm_q_lat, w_uq, norm_kv_lat, w_ukv, norm_out_a, norm_out_b, w_out, norm_final, loss_target, m_meta_tokens, m_rel_bias_table, m_norm_in, m_w_in, m_sink_a, m_norm_q_lat, m_w_uq, m_norm_kv_lat, m_w_ukv, m_norm_out_a, m_norm_out_b, m_w_out, m_norm_final, v_meta_tokens, v_rel_bias_table, v_norm_in, v_w_in, v_sink_a, v_norm_q_lat, v_w_uq, v_norm_kv_lat, v_w_ukv, v_norm_out_a, v_norm_out_b, v_w_out, v_norm_final):
    w = dict(zip(WEIGHTS, (meta_tokens, rel_bias_table, norm_in, w_in, sink_a, norm_q_lat, w_uq, norm_kv_lat, w_ukv, norm_out_a, norm_out_b, w_out, norm_final)))
    m = dict(zip(WEIGHTS, (m_meta_tokens, m_rel_bias_table, m_norm_in, m_w_in, m_sink_a, m_norm_q_lat, m_w_uq, m_norm_kv_lat, m_w_ukv, m_norm_out_a, m_norm_out_b, m_w_out, m_norm_final)))
    v = dict(zip(WEIGHTS, (v_meta_tokens, v_rel_bias_table, v_norm_in, v_w_in, v_sink_a, v_norm_q_lat, v_w_uq, v_norm_kv_lat, v_w_ukv, v_norm_out_a, v_norm_out_b, v_w_out, v_norm_final)))
    cfg = make_cfg(x.shape[0], x.shape[1])
    return _train_step(cfg, x, loss_target, w, m, v)
```

```python
import collections
import functools
import math

import jax
import jax.numpy as jnp
import numpy as np
from jax import lax
from jax.experimental import pallas as pl
from jax.experimental.pallas import tpu as pltpu

F32 = jnp.float32
BF16 = jnp.bfloat16

BLK = 128
N_META = 16
D_MODEL = 1024
A_HEADS, A_KV, A_DH = 8, 2, 64
B_HEADS, B_NOPE, B_ROPE, B_DV = 8, 64, 32, 64
Q_RANK, KV_RANK = 256, 128
N_BUCKETS, MAX_DIST = 32, 128
ROPE_THETA = 10000.0
EPS = 1e-6
IN_WIDTH = 2208
W_IN_P = 2304
NEG = -1e30
MASK_LANE = 96
LOG2E = math.log2(math.e)
Q_SCALE = (B_NOPE + B_ROPE) ** -0.5 * LOG2E
QA_SCALE = A_DH ** -0.5 * LOG2E
LN2 = math.log(2.0)
VMEM_LIMIT = 48 * 1024 * 1024

ADAM_LR, ADAM_B1, ADAM_B2, ADAM_EPS, ADAM_WD, ADAM_STEP = 0.001, 0.9, 0.999, 1e-08, 0.01, 10

Cfg = collections.namedtuple("Cfg", "B S NB NJ LP TP")


def make_cfg(batch, seq):
    nb = seq // BLK
    nj = nb + 1
    return Cfg(batch, seq, nb, nj, nj * BLK, batch * nj * BLK)


def _cp(*sem):
    return pltpu.CompilerParams(dimension_semantics=sem, vmem_limit_bytes=VMEM_LIMIT)


def _pallas(body, *, out_shape, **kw):
    pinned = jax.tree.map(lambda s: pltpu.HBM(s.shape, s.dtype), out_shape)
    call = pl.pallas_call(body, out_shape=pinned, **kw)
    return lambda *args: call(*[pltpu.with_memory_space_constraint(a, pltpu.HBM) for a in args])


def _dot(a, b):
    return jnp.dot(a, b, preferred_element_type=F32)


def _dot_nt(a, b):
    return lax.dot_general(a, b, (((1,), (1,)), ((), ())), preferred_element_type=F32)


def _dot_tn(a, b):
    return lax.dot_general(a, b, (((0,), (0,)), ((), ())), preferred_element_type=F32)


def _rms(x, width=None):
    n = x.shape[-1] if width is None else width
    r = lax.rsqrt(jnp.sum(x * x, axis=-1, keepdims=True) * (1.0 / n) + EPS)
    return x * r, r


def _rms_bwd(xhat, r, t):
    n = xhat.shape[-1]
    return r * (t - xhat * (jnp.sum(t * xhat, axis=-1, keepdims=True) * (1.0 / n)))


def _sigmoid(x):
    return 1.0 / (1.0 + jnp.exp(-x))


def _lane(shape):
    return lax.broadcasted_iota(jnp.int32, shape, len(shape) - 1)


def _swap_rope(x):
    n = x.shape[-1]
    lane = _lane(x.shape) % BLK
    up = pltpu.roll(x, n - 16, axis=x.ndim - 1)
    dn = pltpu.roll(x, 16, axis=x.ndim - 1)
    return jnp.where((lane >= 64) & (lane < 80), up, jnp.where((lane >= 80) & (lane < 96), dn, 0.0))


A_ORDER = (0, 4, 1, 5, 2, 6, 3, 7)


def _jtype(j, nj):
    return 0 if j == 0 else 1 if j == 1 else 3 if j == nj - 1 else 2


def _window_structure(nj):
    def pos(blk, r):
        return np.where(blk == 0, r, N_META + (blk - 1) * BLK + r)

    def valid(blk, r):
        return np.where(blk == 0, r < N_META, True)

    r = np.arange(BLK)
    rels, viss = [], []
    for j in (0, 1, 2, nj - 1):
        qpos = pos(j, r)[:, None]
        rel_t, vis_t = [], []
        for s, kb in enumerate((0, j - 1, j, j + 1)):
            slot_ok = (s == 0) or (1 <= kb <= nj - 1)
            kbc = min(max(kb, 0), nj - 1)
            kpos = pos(kbc, r)[None, :]
            rel = kpos - qpos
            v = valid(kbc, r)[None, :] & np.ones((BLK, 1), bool)
            if s > 0:
                v = v & (np.abs(rel) <= BLK)
            rel_t.append(rel)
            vis_t.append(v & slot_ok)
        rels.append(np.concatenate(rel_t, axis=1))
        viss.append(np.concatenate(vis_t, axis=1))
    return np.stack(rels).astype(np.int32), np.stack(viss)


def _t5_bucket(rel):
    nb = N_BUCKETS // 2
    max_exact = nb // 2
    ret = jnp.where(rel > 0, nb, 0)
    n = jnp.abs(rel)
    nf = jnp.maximum(n, 1).astype(jnp.float32)
    large = max_exact + (jnp.log(nf / max_exact) / math.log(MAX_DIST / max_exact) * (nb - max_exact)).astype(jnp.int32)
    large = jnp.minimum(large, nb - 1)
    return ret + jnp.where(n < max_exact, n, large)


def _perm_heads64(a, axis):
    parts = [lax.slice_in_dim(a, h * 64, (h + 1) * 64, axis=axis) for h in A_ORDER]
    return jnp.concatenate(parts, axis=axis)


def _unperm_heads64(a, axis):
    inv = [A_ORDER.index(h) for h in range(8)]
    parts = [lax.slice_in_dim(a, p * 64, (p + 1) * 64, axis=axis) for p in inv]
    return jnp.concatenate(parts, axis=axis)


def _w_in_to_p(w):
    sl = lambda a, b: lax.slice_in_dim(w, a, b, axis=1)
    z = lambda n: jnp.zeros((w.shape[0], n), w.dtype)
    return jnp.concatenate([_perm_heads64(sl(0, 512), 1), sl(512, 768), _perm_heads64(sl(768, 1280), 1), sl(1696, 2208),
                            sl(1280, 1536), sl(1536, 1664), z(64), sl(1664, 1696), z(32)], axis=1)


def _w_in_from_p(g):
    sl = lambda a, b: lax.slice_in_dim(g, a, b, axis=1)
    return jnp.concatenate([_unperm_heads64(sl(0, 512), 1), sl(512, 768), _unperm_heads64(sl(768, 1280), 1),
                            sl(1792, 2048), sl(2048, 2176), sl(2240, 2272), sl(1280, 1792)], axis=1)


def _w_uq_to_p(w):
    z = jnp.zeros((w.shape[0], 32), w.dtype)
    return jnp.concatenate([p for h in range(8) for p in (lax.slice_in_dim(w, h * 96, (h + 1) * 96, axis=1), z)], axis=1)


def _w_uq_from_p(g):
    return jnp.concatenate([lax.slice_in_dim(g, h * 128, h * 128 + 96, axis=1) for h in range(8)], axis=1)


def _w_ukv_to_p(w):
    z = jnp.zeros((w.shape[0], 64), w.dtype)
    ks = [p for h in range(8) for p in (lax.slice_in_dim(w, h * 128, h * 128 + 64, axis=1), z)]
    vs = [lax.slice_in_dim(w, h * 128 + 64, (h + 1) * 128, axis=1) for h in range(8)]
    return jnp.concatenate(ks + vs, axis=1)


def _w_ukv_from_p(g):
    parts = []
    for h in range(8):
        parts.append(lax.slice_in_dim(g, h * 128, h * 128 + 64, axis=1))
        parts.append(lax.slice_in_dim(g, 1024 + h * 64, 1024 + (h + 1) * 64, axis=1))
    return jnp.concatenate(parts, axis=1)


def _w_out_to_p(w):
    return jnp.concatenate([_perm_heads64(lax.slice_in_dim(w, 0, 512, axis=0), 0), lax.slice_in_dim(w, 512, 1024, axis=0)], axis=0)


def _w_out_from_p(g):
    return jnp.concatenate([_unperm_heads64(lax.slice_in_dim(g, 0, 512, axis=0), 0), lax.slice_in_dim(g, 512, 1024, axis=0)], axis=0)


def _rope_tables(cfg):
    half = B_ROPE // 2
    length = N_META + cfg.S
    freqs = ROPE_THETA ** (-jnp.arange(half, dtype=jnp.float32) / half)
    ang = jnp.arange(length, dtype=jnp.float32)[:, None] * freqs[None, :]
    cos, sin = jnp.cos(ang), jnp.sin(ang)

    def rows(t):
        return jnp.concatenate([t[:N_META], jnp.zeros((BLK - N_META, t.shape[1]), t.dtype), t[N_META:]], axis=0)

    ones = jnp.ones((length, 64), F32)
    zer = jnp.zeros((length, 32), F32)
    c_tab = rows(jnp.concatenate([ones, cos, cos, zer], axis=1))
    s_tab = rows(jnp.concatenate([zer, zer, -sin, sin, zer], axis=1))
    return c_tab, s_tab


def _inproj_fwd(cfg, h, g, w_p, comm=None):
    tm = 256

    def body(h_ref, g_ref, w_ref, pa_ref, pf_ref):
        xh, _ = _rms(h_ref[...])
        u = (xh * g_ref[...]).astype(BF16)
        acc = _dot(u, w_ref[...])
        pa_ref[:, :512] = (acc[:, :512] * QA_SCALE).astype(BF16)
        pa_ref[:, 512:] = acc[:, 512:768].astype(BF16)
        pf_ref[...] = acc[:, 768:]

    return _call_with_comm(
        body, comm, grid=(cfg.TP // tm,), name="inproj_fwd",
        in_specs=[pl.BlockSpec((tm, D_MODEL), lambda i: (i, 0)), pl.BlockSpec((1, D_MODEL), lambda i: (0, 0)),
                  pl.BlockSpec((D_MODEL, W_IN_P), lambda i: (0, 0))],
        out_specs=[pl.BlockSpec((tm, 768), lambda i: (i, 0)), pl.BlockSpec((tm, 1536), lambda i: (i, 0))],
        out_shape=[jax.ShapeDtypeStruct((cfg.TP, 768), BF16), jax.ShapeDtypeStruct((cfg.TP, 1536), F32)],
        args=(h, g, w_p))


def _lat_fwd(cfg, pf, gq, gkv, wq_p, wkv_p, c_tab, s_tab):
    nj = cfg.NJ

    def body(cq_ref, ckv_ref, kr_ref, gq_ref, gkv_ref, wq_ref, wkv_ref, c_ref, s_ref, q_ref, k_ref, v_ref, kt_ref, vt_ref):
        c1, s1 = c_ref[...], s_ref[...]
        c8, s8 = jnp.tile(c1, (1, 8)), jnp.tile(s1, (1, 8))
        mask_lane = _lane((BLK, 1024)) % BLK == MASK_LANE
        zero_row = (pl.program_id(1) == 0) & (lax.broadcasted_iota(jnp.int32, (BLK, 1024), 0) >= N_META)
        xq, _ = _rms(cq_ref[...])
        qp = _dot((xq * gq_ref[...]).astype(BF16), wq_ref[...])
        q_ref[...] = jnp.where(mask_lane, 1.0, (qp * c8 + _swap_rope(qp) * s8) * Q_SCALE).astype(BF16)
        xk, _ = _rms(ckv_ref[...])
        kvp = _dot((xk * gkv_ref[...]).astype(BF16), wkv_ref[...])
        kr = kr_ref[...]
        krr = kr * c1 + _swap_rope(kr) * s1
        k = jnp.where(mask_lane & zero_row, NEG, kvp[:, :1024] + jnp.tile(krr, (1, 8)))
        k_ref[...] = k.astype(BF16)
        v_ref[...] = kvp[:, 1024:].astype(BF16)
        kt_ref[...] = k.T.astype(BF16)
        vt_ref[...] = kvp[:, 1024:].T.astype(BF16)

    row = lambda b, j: b * nj + j
    return _pallas(
        body, grid=(cfg.B, nj), name="lat_fwd",
        in_specs=[pl.BlockSpec((BLK, 256), lambda b, j: (row(b, j), 4)), pl.BlockSpec((BLK, 128), lambda b, j: (row(b, j), 10)),
                  pl.BlockSpec((BLK, 128), lambda b, j: (row(b, j), 11)),
                  pl.BlockSpec((1, 256), lambda b, j: (0, 0)), pl.BlockSpec((1, 128), lambda b, j: (0, 0)),
                  pl.BlockSpec((256, 1024), lambda b, j: (0, 0)), pl.BlockSpec((128, 1536), lambda b, j: (0, 0)),
                  pl.BlockSpec((BLK, 128), lambda b, j: (j, 0)), pl.BlockSpec((BLK, 128), lambda b, j: (j, 0))],
        out_specs=[pl.BlockSpec((BLK, 1024), lambda b, j: (row(b, j), 0)), pl.BlockSpec((BLK, 1024), lambda b, j: (row(b, j), 0)),
                   pl.BlockSpec((BLK, 512), lambda b, j: (row(b, j), 0)),
                   pl.BlockSpec((1024, BLK), lambda b, j: (b, j)), pl.BlockSpec((512, BLK), lambda b, j: (b, j))],
        out_shape=[jax.ShapeDtypeStruct((cfg.TP, 1024), BF16), jax.ShapeDtypeStruct((cfg.TP, 1024), BF16),
                   jax.ShapeDtypeStruct((cfg.TP, 512), BF16),
                   jax.ShapeDtypeStruct((cfg.B * 1024, cfg.LP), BF16), jax.ShapeDtypeStruct((cfg.B * 512, cfg.LP), BF16)],
        compiler_params=_cp("parallel", "parallel"),
    )(pf, pf, pf, gq, gkv, wq_p, wkv_p, c_tab, s_tab)


def _gate_halves(ya, yb, ga, gb, goa, gob):
    xa, ra = _rms(ya)
    xb, rb = _rms(yb)
    sga, sgb = _sigmoid(ga), _sigmoid(gb)
    return xa, ra, xb, rb, sga, sgb, xa * goa * (ga * sga), xb * gob * (gb * sgb)


def _out_fwd(cfg, ya, yb, pf, goa, gob, wo_p, h):
    tm = 256

    def body(ya_ref, yb_ref, ga_ref, gb_ref, goa_ref, gob_ref, w_ref, h_ref, o_ref):
        *_, y_a, y_b = _gate_halves(ya_ref[...], yb_ref[...], ga_ref[...], gb_ref[...], goa_ref[...], gob_ref[...])
        y = jnp.concatenate([y_a, y_b], axis=1).astype(BF16)
        o_ref[...] = h_ref[...] + _dot(y, w_ref[...])

    return _pallas(
        body, grid=(cfg.TP // tm,), name="out_fwd",
        in_specs=[pl.BlockSpec((tm, 512), lambda i: (i, 0)), pl.BlockSpec((tm, 512), lambda i: (i, 0)),
                  pl.BlockSpec((tm, 512), lambda i: (i, 0)), pl.BlockSpec((tm, 512), lambda i: (i, 1)),
                  pl.BlockSpec((1, 512), lambda i: (0, 0)), pl.BlockSpec((1, 512), lambda i: (0, 0)),
                  pl.BlockSpec((D_MODEL, D_MODEL), lambda i: (0, 0)), pl.BlockSpec((tm, D_MODEL), lambda i: (i, 0))],
        out_specs=pl.BlockSpec((tm, D_MODEL), lambda i: (i, 0)),
        out_shape=jax.ShapeDtypeStruct((cfg.TP, D_MODEL), F32),
        compiler_params=_cp("parallel"),
    )(ya, yb, pf, pf, goa, gob, wo_p, h)


def _bias_build(table, bucket, maskadd, comm=None):
    def body(tab_ref, bk_ref, ma_ref, o_ref):
        def rows(g, carry):
            r = pl.ds(pl.multiple_of(g * 8, 8), 8)
            bk = bk_ref[0, r, :]
            accs = [jnp.zeros(bk.shape, F32)] * A_HEADS
            for b in range(N_BUCKETS):
                hit = bk == b
                accs = [jnp.where(hit, tab_ref[b, h], accs[h]) for h in range(A_HEADS)]
            ma = ma_ref[0, r, :]
            for h in range(A_HEADS):
                o_ref[0, h, r, :] = (accs[h] + ma) * LOG2E
            return carry

        lax.fori_loop(0, BLK // 8, rows, 0)

    return _call_with_comm(
        body, comm, grid=(4,), name="bias_build",
        in_specs=[pl.BlockSpec(memory_space=pltpu.SMEM), pl.BlockSpec((1, BLK, 512), lambda t: (t, 0, 0)),
                  pl.BlockSpec((1, BLK, 512), lambda t: (t, 0, 0))],
        out_specs=[pl.BlockSpec((1, A_HEADS, BLK, 512), lambda t: (t, 0, 0, 0))],
        out_shape=[jax.ShapeDtypeStruct((4, A_HEADS, BLK, 512), F32)],
        args=(table, bucket, maskadd))


def _bias_grad(s_accs, bucket, comm=None):
    depth = len(s_accs)

    def body(*refs):
        s_refs, bk_ref, o_ref, sum_ref, part_ref = refs[:depth], refs[depth], refs[depth + 1], refs[depth + 2], refs[depth + 3]
        t = pl.program_id(0)

        @pl.when(t == 0)
        def _():
            o_ref[...] = jnp.zeros_like(o_ref)

        total = s_refs[0][0]
        for extra in s_refs[1:]:
            total = total + extra[0]
        sum_ref[...] = total

        def step(b, carry):
            accs = [jnp.zeros((8, 512), F32) for _ in range(A_HEADS)]
            for g in range(BLK // 8):
                rows = pl.ds(g * 8, 8)
                hit = bk_ref[0, rows, :] == b
                for h in range(A_HEADS):
                    accs[h] = accs[h] + jnp.where(hit, sum_ref[h, rows, :], 0.0)
            rows8 = jnp.concatenate([jnp.sum(a, axis=0, keepdims=True) for a in accs], axis=0)
            part_ref[pl.ds(pl.multiple_of(b * A_HEADS, 8), A_HEADS), :] = rows8
            return carry

        lax.fori_loop(0, N_BUCKETS, step, 0)
        o_ref[...] += jnp.broadcast_to(jnp.sum(part_ref[...], axis=1, keepdims=True), o_ref.shape)

    s_spec = pl.BlockSpec((1, A_HEADS, BLK, 512), lambda t: (t, 0, 0, 0))
    return _call_with_comm(
        body, comm, grid=(4,), name="bias_grad",
        in_specs=[s_spec] * depth + [pl.BlockSpec((1, BLK, 512), lambda t: (t, 0, 0))],
        out_specs=[pl.BlockSpec((N_BUCKETS * A_HEADS, 128), lambda t: (0, 0))],
        out_shape=[jax.ShapeDtypeStruct((N_BUCKETS * A_HEADS, 128), F32)],
        scratch_shapes=[pltpu.VMEM((A_HEADS, BLK, 512), F32), pltpu.VMEM((N_BUCKETS * A_HEADS, 512), F32)],
        args=(*s_accs, bucket))


def _win_specs(cfg):
    nj = cfg.NJ
    row = lambda b, j: b * nj + j
    jt = lambda j: jnp.where(j == 0, 0, jnp.where(j == 1, 1, jnp.where(j == nj - 1, 3, 2)))
    slot_rows = [lambda b, j: row(b, 0), lambda b, j: row(b, jnp.maximum(j - 1, 0)), lambda b, j: row(b, j),
                 lambda b, j: row(b, jnp.minimum(j + 1, nj - 1))]
    k_specs = [pl.BlockSpec((BLK, 128), functools.partial(lambda b, j, f: (f(b, j), 4), f=f)) for f in slot_rows]
    v_specs = [pl.BlockSpec((BLK, 128), functools.partial(lambda b, j, f: (f(b, j), 5), f=f)) for f in slot_rows]
    q_spec = pl.BlockSpec((BLK, 512), lambda b, j: (row(b, j), 0))
    bias_spec = pl.BlockSpec((1, A_HEADS, BLK, 512), lambda b, j: (jt(j), 0, 0, 0))
    return row, jt, q_spec, k_specs, v_specs, bias_spec


def _stack4(ref):
    return jnp.concatenate([ref[:, c * 128:(c + 1) * 128] for c in range(4)], axis=0)


def _win_keys(k_refs, v_refs):
    k4 = jnp.concatenate([r[...] for r in k_refs], axis=0)
    v4 = jnp.concatenate([r[...] for r in v_refs], axis=0)
    lane_k = _lane(k4.shape)
    return (jnp.where(lane_k < 64, k4, jnp.zeros_like(k4)), jnp.where(lane_k >= 64, k4, jnp.zeros_like(k4))), v4


def _sink_col(sink_ref, hf):
    rowi = lax.broadcasted_iota(jnp.int32, (4 * BLK, 1), 0)
    col = jnp.full((4 * BLK, 1), sink_ref[4 * hf + 3], F32)
    for c in (2, 1, 0):
        col = jnp.where(rowi < (c + 1) * BLK, sink_ref[4 * hf + c], col)
    return col * LOG2E


def _win_fwd(cfg, pa, bias, sink):
    row, jt, q_spec, k_specs, v_specs, bias_spec = _win_specs(cfg)

    def body(sink_ref, q_ref, k0, k1, k2, k3, v0, v1, v2, v3, b_ref, o_ref, lse_ref):
        kk, v4 = _win_keys((k0, k1, k2, k3), (v0, v1, v2, v3))
        qs = _stack4(q_ref)
        lane_o = _lane((4 * BLK, 128))
        outs, lses = [], []
        for hf in range(2):
            s = _dot_nt(qs, kk[hf]) + b_ref[0, 4 * hf:4 * hf + 4].reshape(4 * BLK, 512)
            sink_col = _sink_col(sink_ref, hf)
            m = jnp.maximum(jnp.max(s, axis=1, keepdims=True), sink_col)
            e = jnp.exp2(s - m)
            den = jnp.sum(e, axis=1, keepdims=True) + jnp.exp2(sink_col - m)
            outs.append(_dot(e.astype(BF16), v4) / den)
            lses.append(m + jnp.log2(den))
        o = jnp.where(lane_o < 64, outs[0], outs[1])
        for c in range(4):
            o_ref[:, c * 128:(c + 1) * 128] = o[c * BLK:(c + 1) * BLK]
        lse_ref[...] = jnp.where(lane_o == 0, lses[0], jnp.where(lane_o == 1, lses[1], 0.0))

    return _pallas(
        body, grid=(cfg.B, cfg.NJ), name="win_fwd",
        in_specs=[pl.BlockSpec(memory_space=pltpu.SMEM), q_spec, *k_specs, *v_specs, bias_spec],
        out_specs=[pl.BlockSpec((BLK, 512), lambda b, j: (row(b, j), 0)), pl.BlockSpec((4 * BLK, 128), lambda b, j: (row(b, j), 0))],
        out_shape=[jax.ShapeDtypeStruct((cfg.TP, 512), F32), jax.ShapeDtypeStruct((4 * cfg.TP, 128), F32)],
        compiler_params=_cp("parallel", "parallel"),
    )(sink, pa, *([pa] * 8), bias)


def _win_bwd(cfg, pa, bias, sink, dya, ya, lse):
    row, jt, q_spec, k_specs, v_specs, bias_spec = _win_specs(cfg)
    nj = cfg.NJ

    def body(sink_ref, q_ref, k0, k1, k2, k3, v0, v1, v2, v3, b_ref, dy_ref, y_ref, lse_ref,
             dq_ref, dkp_ref, dvp_ref, dkm_ref, dvm_ref, s_ref, dsink_ref):
        j = pl.program_id(1)
        kind = jt(j)

        @pl.when((pl.program_id(0) == 0) & (j == 0))
        def _():
            s_ref[...] = jnp.zeros_like(s_ref)

        kk, v4 = _win_keys((k0, k1, k2, k3), (v0, v1, v2, v3))
        qs, dys, ys = _stack4(q_ref), _stack4(dy_ref), _stack4(y_ref)
        lane_o = _lane((4 * BLK, 128))
        half = (lane_o < 64, lane_o >= 64)
        lse_blk = lse_ref[...]
        dq = jnp.zeros((4 * BLK, 128), F32)
        dk4 = jnp.zeros((512, 128), F32)
        dv4 = jnp.zeros((512, 128), F32)
        dsink = jnp.zeros((8, 128), F32)
        lane_s = _lane((8, 128))
        row_s = lax.broadcasted_iota(jnp.int32, (8, 128), 0)
        for hf in range(2):
            lse_h = jnp.sum(jnp.where(lane_o == hf, lse_blk, 0.0), axis=1, keepdims=True)
            s = _dot_nt(qs, kk[hf]) + b_ref[0, 4 * hf:4 * hf + 4].reshape(4 * BLK, 512)
            p = jnp.exp2(s - lse_h)
            do_h = jnp.where(half[hf], dys, 0.0)
            delta = jnp.sum(do_h * ys, axis=1, keepdims=True)
            do_b = do_h.astype(BF16)
            ds = p * (_dot_nt(do_b, v4) - delta)
            s_ref[kind, 4 * hf:4 * hf + 4] += ds.reshape(4, BLK, 512)
            sink_grad = jnp.exp2(_sink_col(sink_ref, hf) - lse_h) * delta
            for c in range(4):
                tot = -jnp.sum(sink_grad[c * BLK:(c + 1) * BLK])
                dsink = jnp.where((row_s == 0) & (lane_s == 4 * hf + c), tot, dsink)
            dsb = (ds * LN2).astype(BF16)
            dq = dq + _dot(dsb, kk[hf])
            dk4 = dk4 + _dot_tn(dsb, jnp.where(half[hf], qs, jnp.zeros_like(qs)))
            dv4 = dv4 + _dot_tn(p.astype(BF16), do_b)
        for c in range(4):
            dq_ref[:, c * 128:(c + 1) * 128] = (dq[c * BLK:(c + 1) * BLK] * QA_SCALE).astype(BF16)
        dkp_ref[0] = dk4
        dvp_ref[0] = dv4

        @pl.when(j == 0)
        def _():
            dkm_ref[...] = dk4[:BLK]
            dvm_ref[...] = dv4[:BLK]

        @pl.when(j > 0)
        def _():
            dkm_ref[...] += dk4[:BLK]
            dvm_ref[...] += dv4[:BLK]

        @pl.when((pl.program_id(0) == 0) & (j == 0))
        def _():
            dsink_ref[...] = dsink

        @pl.when((pl.program_id(0) > 0) | (j > 0))
        def _():
            dsink_ref[...] += dsink

    blk_row = pl.BlockSpec((BLK, 512), lambda b, j: (row(b, j), 0))
    return _pallas(
        body, grid=(cfg.B, nj), name="win_bwd",
        in_specs=[pl.BlockSpec(memory_space=pltpu.SMEM), q_spec, *k_specs, *v_specs, bias_spec, blk_row, blk_row,
                  pl.BlockSpec((4 * BLK, 128), lambda b, j: (row(b, j), 0))],
        out_specs=[blk_row,
                   pl.BlockSpec((1, 512, 128), lambda b, j: (row(b, j), 0, 0)), pl.BlockSpec((1, 512, 128), lambda b, j: (row(b, j), 0, 0)),
                   pl.BlockSpec((BLK, 128), lambda b, j: (b, 0)), pl.BlockSpec((BLK, 128), lambda b, j: (b, 0)),
                   pl.BlockSpec((4, A_HEADS, BLK, 512), lambda b, j: (0, 0, 0, 0)),
                   pl.BlockSpec((8, 128), lambda b, j: (0, 0))],
        out_shape=[jax.ShapeDtypeStruct((cfg.TP, 512), BF16),
                   jax.ShapeDtypeStruct((cfg.B * nj, 512, 128), F32), jax.ShapeDtypeStruct((cfg.B * nj, 512, 128), F32),
                   jax.ShapeDtypeStruct((cfg.B * BLK, 128), F32), jax.ShapeDtypeStruct((cfg.B * BLK, 128), F32),
                   jax.ShapeDtypeStruct((4, A_HEADS, BLK, 512), F32),
                   jax.ShapeDtypeStruct((8, 128), F32)],
        compiler_params=_cp("arbitrary", "arbitrary"),
    )(sink, pa, *([pa] * 8), bias, dya, ya, lse)


def _win_dkv_combine(cfg, dkp, dvp, dkm, dvm):
    nj = cfg.NJ

    def body(kp, vp, km, vm, o_ref):
        o_ref[:BLK, :128] = km[...].astype(BF16)
        o_ref[:BLK, 128:] = vm[...].astype(BF16)
        for kb in range(1, nj):
            for col, part in ((0, kp), (128, vp)):
                tot = part[kb, 2 * BLK:3 * BLK] + part[kb - 1, 3 * BLK:4 * BLK]
                if kb + 1 < nj:
                    tot = tot + part[kb + 1, BLK:2 * BLK]
                o_ref[kb * BLK:(kb + 1) * BLK, col:col + 128] = tot.astype(BF16)

    return _pallas(
        body, grid=(cfg.B,), name="win_dkv_combine",
        in_specs=[pl.BlockSpec((nj, 512, 128), lambda b: (b, 0, 0)), pl.BlockSpec((nj, 512, 128), lambda b: (b, 0, 0)),
                  pl.BlockSpec((BLK, 128), lambda b: (b, 0)), pl.BlockSpec((BLK, 128), lambda b: (b, 0))],
        out_specs=pl.BlockSpec((cfg.LP, 256), lambda b: (b, 0)),
        out_shape=jax.ShapeDtypeStruct((cfg.TP, 256), BF16),
        compiler_params=_cp("parallel"),
    )(dkp, dvp, dkm, dvm)


def _pair_blockdiag(q):
    lane = _lane(q.shape)
    return jnp.concatenate([jnp.where(lane < 128, q, jnp.zeros_like(q)), jnp.where(lane >= 128, q, jnp.zeros_like(q))], axis=0)


def _mla_fwd(cfg, q, kt, v, comm=None):
    nj, lp = cfg.NJ, cfg.LP

    def body(q_ref, kt_ref, v_ref, o_ref, lse_ref, s_even, s_odd):
        i = pl.program_id(2)
        lane_o = _lane((BLK, 128))

        def logits(s_write):
            s_write[...] = _dot(_pair_blockdiag(q_ref[...]), kt_ref[...])

        def finish(s_read):
            s = s_read[...]
            m = jnp.max(s, axis=1, keepdims=True)
            e = jnp.exp2(s - m)
            den = jnp.sum(e, axis=1, keepdims=True)
            pv = _dot(e.astype(BF16), v_ref[...]) / den
            o_ref[...] = jnp.where(lane_o < 64, pv[:BLK], pv[BLK:])
            lse_ref[0] = jnp.broadcast_to(m + jnp.log2(den), (2 * BLK, 128))

        odd = i % 2 == 1

        @pl.when(i == 0)
        def _():
            logits(s_even)

        @pl.when(odd & (i < nj))
        def _():
            logits(s_odd)
            finish(s_even)

        @pl.when(jnp.logical_not(odd) & (i > 0) & (i < nj))
        def _():
            logits(s_even)
            finish(s_odd)

        @pl.when(i == nj)
        def _():
            finish(s_even if nj % 2 == 1 else s_odd)

    cur = lambda b, i: b * nj + jnp.minimum(i, nj - 1)
    prev = lambda b, i: b * nj + jnp.maximum(i - 1, 0)
    return _call_with_comm(
        body, comm, grid=(cfg.B, 4, nj + 1), name="mla_fwd",
        in_specs=[pl.BlockSpec((BLK, 256), lambda b, p, i: (cur(b, i), p)), pl.BlockSpec((256, lp), lambda b, p, i: (b * 4 + p, 0)),
                  pl.BlockSpec((lp, 128), lambda b, p, i: (b, p))],
        out_specs=[pl.BlockSpec((BLK, 128), lambda b, p, i: (prev(b, i), p)),
                   pl.BlockSpec((1, 2 * BLK, 128), lambda b, p, i: (p, prev(b, i), 0))],
        out_shape=[jax.ShapeDtypeStruct((cfg.TP, 512), F32), jax.ShapeDtypeStruct((4, 2 * cfg.TP, 128), F32)],
        scratch_shapes=[pltpu.VMEM((2 * BLK, lp), F32), pltpu.VMEM((2 * BLK, lp), F32)],
        args=(q, kt, v))


def _mla_bwd(cfg, q, k, kt, vt, dyb, yb, lse, comm=None):
    nj, lp = cfg.NJ, cfg.LP

    def body(q_ref, k_ref, kt_ref, vt_ref, dy_ref, y_ref, lse_ref, dq_ref, dk_ref, dv_ref):
        i = pl.program_id(2)

        @pl.when(i == 0)
        def _():
            dk_ref[...] = jnp.zeros_like(dk_ref)
            dv_ref[...] = jnp.zeros_like(dv_ref)

        lane_o = _lane((BLK, 128))
        qbd = _pair_blockdiag(q_ref[...])
        dy, y = dy_ref[...], y_ref[...]
        do_s = jnp.concatenate([jnp.where(lane_o < 64, dy, 0.0), jnp.where(lane_o >= 64, dy, 0.0)], axis=0)
        delta = jnp.sum(do_s * jnp.concatenate([y, y], axis=0), axis=1, keepdims=True)
        do_b = do_s.astype(BF16)
        p = jnp.exp2(_dot(qbd, kt_ref[...]) - lse_ref[0][:, :1])
        ds = p * (_dot(do_b, vt_ref[...]) - delta)
        dsb = (ds * LN2).astype(BF16)
        dq2 = _dot(dsb, k_ref[...])
        dq_ref[...] = jnp.where(_lane((BLK, 256)) < 128, dq2[:BLK], dq2[BLK:]) * Q_SCALE
        dk_ref[...] += _dot_tn(dsb, qbd)
        dv_ref[...] += _dot_tn(p.astype(BF16), do_b)

    return _call_with_comm(
        body, comm, grid=(cfg.B, 4, nj), name="mla_bwd",
        in_specs=[pl.BlockSpec((BLK, 256), lambda b, p, i: (b * nj + i, p)), pl.BlockSpec((lp, 256), lambda b, p, i: (b, p)),
                  pl.BlockSpec((256, lp), lambda b, p, i: (b * 4 + p, 0)), pl.BlockSpec((128, lp), lambda b, p, i: (b * 4 + p, 0)),
                  pl.BlockSpec((BLK, 128), lambda b, p, i: (b * nj + i, p)), pl.BlockSpec((BLK, 128), lambda b, p, i: (b * nj + i, p)),
                  pl.BlockSpec((1, 2 * BLK, 128), lambda b, p, i: (p, b * nj + i, 0))],
        out_specs=[pl.BlockSpec((BLK, 256), lambda b, p, i: (b * nj + i, p)), pl.BlockSpec((lp, 256), lambda b, p, i: (b, p)),
                   pl.BlockSpec((lp, 128), lambda b, p, i: (b, p))],
        out_shape=[jax.ShapeDtypeStruct((cfg.TP, 1024), F32), jax.ShapeDtypeStruct((cfg.TP, 1024), F32),
                   jax.ShapeDtypeStruct((cfg.TP, 512), F32)],
        args=(q, k, kt, vt, dyb, yb, lse))


def _loss_bwd(cfg, h, target, gf):
    nj, nb = cfg.NJ, cfg.NB

    def body(h_ref, t_ref, g_ref, dh_ref, loss_ref, dg_ref):
        b, j = pl.program_id(0), pl.program_id(1)

        @pl.when((b == 0) & (j == 0))
        def _():
            loss_ref[...] = jnp.zeros_like(loss_ref)
            dg_ref[...] = jnp.zeros_like(dg_ref)

        @pl.when(j == 0)
        def _():
            dh_ref[...] = jnp.zeros_like(dh_ref)

        @pl.when(j > 0)
        def _():
            g = g_ref[...]
            xh, r = _rms(h_ref[...])
            err = xh * g - t_ref[...]
            loss_ref[...] += jnp.where((lax.broadcasted_iota(jnp.int32, (8, 128), 0) == 0) & (_lane((8, 128)) == 0),
                                       (0.5 / D_MODEL) * jnp.sum(err * err), 0.0)
            dy = err * (1.0 / D_MODEL)
            dg_ref[...] += jnp.sum(dy * xh, axis=0, keepdims=True)
            dh_ref[...] = _rms_bwd(xh, r, dy * g)

    return _pallas(
        body, grid=(cfg.B, nj), name="loss_bwd",
        in_specs=[pl.BlockSpec((BLK, D_MODEL), lambda b, j: (b * nj + j, 0)),
                  pl.BlockSpec((BLK, D_MODEL), lambda b, j: (b * nb + jnp.maximum(j - 1, 0), 0)),
                  pl.BlockSpec((1, D_MODEL), lambda b, j: (0, 0))],
        out_specs=[pl.BlockSpec((BLK, D_MODEL), lambda b, j: (b * nj + j, 0)), pl.BlockSpec((8, 128), lambda b, j: (0, 0)),
                   pl.BlockSpec((1, D_MODEL), lambda b, j: (0, 0))],
        out_shape=[jax.ShapeDtypeStruct((cfg.TP, D_MODEL), F32), jax.ShapeDtypeStruct((8, 128), F32),
                   jax.ShapeDtypeStruct((1, D_MODEL), F32)],
        compiler_params=_cp("arbitrary", "arbitrary"),
    )(h, target, gf)


def _out_bwd(cfg, dh, ya, yb, pf, goa, gob, wo_p):
    tm = 256

    def body(dh_ref, ya_ref, yb_ref, ga_ref, gb_ref, goa_ref, gob_ref, w_ref,
             dya_ref, dyb_ref, dg_ref, dw_ref, dgoa_ref, dgob_ref):
        @pl.when(pl.program_id(0) == 0)
        def _():
            dw_ref[...] = jnp.zeros_like(dw_ref)
            dgoa_ref[...] = jnp.zeros_like(dgoa_ref)
            dgob_ref[...] = jnp.zeros_like(dgob_ref)

        ga, gb, goa, gob = ga_ref[...], gb_ref[...], goa_ref[...], gob_ref[...]
        xa, ra, xb, rb, sga, sgb, y_a, y_b = _gate_halves(ya_ref[...], yb_ref[...], ga, gb, goa, gob)
        dhb = dh_ref[...].astype(BF16)
        dw_ref[...] += _dot_tn(jnp.concatenate([y_a, y_b], axis=1).astype(BF16), dhb)
        dy = _dot_nt(dhb, w_ref[...])
        for (dyh, x, r, g, sg, go, dy_out, dgo_ref, col) in (
                (dy[:, :512], xa, ra, ga, sga, goa, dya_ref, dgoa_ref, 0), (dy[:, 512:], xb, rb, gb, sgb, gob, dyb_ref, dgob_ref, 512)):
            dn = dyh * (g * sg)
            dg_ref[:, col:col + 512] = (dyh * (x * go) * (sg * (1.0 + g * (1.0 - sg)))).astype(BF16)
            dgo_ref[...] += jnp.sum(dn * x, axis=0, keepdims=True)
            dy_out[...] = _rms_bwd(x, r, dn * go)

    half = lambda c: pl.BlockSpec((tm, 512), lambda i: (i, c))
    vec = pl.BlockSpec((1, 512), lambda i: (0, 0))
    return _pallas(
        body, grid=(cfg.TP // tm,), name="out_bwd",
        in_specs=[pl.BlockSpec((tm, D_MODEL), lambda i: (i, 0)), half(0), half(0), half(0), half(1), vec, vec,
                  pl.BlockSpec((D_MODEL, D_MODEL), lambda i: (0, 0))],
        out_specs=[half(0), half(0), pl.BlockSpec((tm, D_MODEL), lambda i: (i, 0)),
                   pl.BlockSpec((D_MODEL, D_MODEL), lambda i: (0, 0)), vec, vec],
        out_shape=[jax.ShapeDtypeStruct((cfg.TP, 512), F32), jax.ShapeDtypeStruct((cfg.TP, 512), F32),
                   jax.ShapeDtypeStruct((cfg.TP, D_MODEL), BF16), jax.ShapeDtypeStruct((D_MODEL, D_MODEL), F32),
                   jax.ShapeDtypeStruct((1, 512), F32), jax.ShapeDtypeStruct((1, 512), F32)],
        compiler_params=_cp("arbitrary"),
    )(dh, ya, yb, pf, pf, goa, gob, wo_p)


def _lat_bwd(cfg, dq, dk, dv, pf, gq, gkv, wq_p, wkv_p, c_tab, s_tab):
    nj = cfg.NJ

    def body(dq_ref, dk_ref, dv_ref, cq_ref, ckv_ref, gq_ref, gkv_ref, wq_ref, wkv_ref, c_ref, s_ref,
             dl_ref, dwq_ref, dwkv_ref, dgq_ref, dgkv_ref):
        @pl.when((pl.program_id(0) == 0) & (pl.program_id(1) == 0))
        def _():
            dwq_ref[...] = jnp.zeros_like(dwq_ref)
            dwkv_ref[...] = jnp.zeros_like(dwkv_ref)
            dgq_ref[...] = jnp.zeros_like(dgq_ref)
            dgkv_ref[...] = jnp.zeros_like(dgkv_ref)

        c1, s1 = c_ref[...], s_ref[...]
        c8, s8 = jnp.tile(c1, (1, 8)), jnp.tile(s1, (1, 8))
        dq_r = dq_ref[...]
        dqp = (dq_r * c8 + _swap_rope(dq_r * s8)).astype(BF16)
        gq = gq_ref[...]
        xq, rq = _rms(cq_ref[...])
        dwq_ref[...] += _dot_tn((xq * gq).astype(BF16), dqp)
        dn = _dot_nt(dqp, wq_ref[...])
        dgq_ref[...] += jnp.sum(dn * xq, axis=0, keepdims=True)
        dl_ref[:, :256] = _rms_bwd(xq, rq, dn * gq).astype(BF16)

        dk_r = dk_ref[...]
        dkr = dk_r[:, :128]
        for hd in range(1, 8):
            dkr = dkr + dk_r[:, hd * 128:(hd + 1) * 128]
        lane1 = _lane(dkr.shape)
        dkr = jnp.where((lane1 >= 64) & (lane1 < 96), dkr, 0.0)
        dl_ref[:, 384:] = (dkr * c1 + _swap_rope(dkr * s1)).astype(BF16)
        dkv = jnp.concatenate([dk_r, dv_ref[...]], axis=1).astype(BF16)
        gkv = gkv_ref[...]
        xk, rk = _rms(ckv_ref[...])
        dwkv_ref[...] += _dot_tn((xk * gkv).astype(BF16), dkv)
        dn2 = _dot_nt(dkv, wkv_ref[...])
        dgkv_ref[...] += jnp.sum(dn2 * xk, axis=0, keepdims=True)
        dl_ref[:, 256:384] = _rms_bwd(xk, rk, dn2 * gkv).astype(BF16)

    row = lambda b, j: b * nj + j
    const = lambda shape: pl.BlockSpec(shape, lambda b, j: (0, 0))
    return _pallas(
        body, grid=(cfg.B, nj), name="lat_bwd",
        in_specs=[pl.BlockSpec((BLK, 1024), lambda b, j: (row(b, j), 0)), pl.BlockSpec((BLK, 1024), lambda b, j: (row(b, j), 0)),
                  pl.BlockSpec((BLK, 512), lambda b, j: (row(b, j), 0)),
                  pl.BlockSpec((BLK, 256), lambda b, j: (row(b, j), 4)), pl.BlockSpec((BLK, 128), lambda b, j: (row(b, j), 10)),
                  const((1, 256)), const((1, 128)), const((256, 1024)), const((128, 1536)),
                  pl.BlockSpec((BLK, 128), lambda b, j: (j, 0)), pl.BlockSpec((BLK, 128), lambda b, j: (j, 0))],
        out_specs=[pl.BlockSpec((BLK, 512), lambda b, j: (row(b, j), 0)), const((256, 1024)), const((128, 1536)),
                   const((1, 256)), const((1, 128))],
        out_shape=[jax.ShapeDtypeStruct((cfg.TP, 512), BF16), jax.ShapeDtypeStruct((256, 1024), F32),
                   jax.ShapeDtypeStruct((128, 1536), F32), jax.ShapeDtypeStruct((1, 256), F32), jax.ShapeDtypeStruct((1, 128), F32)],
        compiler_params=_cp("arbitrary", "arbitrary"),
    )(dq, dk, dv, pf, pf, gq, gkv, wq_p, wkv_p, c_tab, s_tab)


def _inproj_bwd(cfg, h, g, w_p, dqa, dkva, dgate, dlat, dh, comm=None):
    tm = 256

    def body(h_ref, g_ref, w_ref, dqa_ref, dkva_ref, dg_ref, dl_ref, dh_ref, o_ref, dw_ref, dgn_ref):
        @pl.when(pl.program_id(0) == 0)
        def _():
            dw_ref[...] = jnp.zeros_like(dw_ref)
            dgn_ref[...] = jnp.zeros_like(dgn_ref)

        g = g_ref[...]
        xh, r = _rms(h_ref[...])
        dproj = jnp.concatenate([dqa_ref[...], dkva_ref[...], dg_ref[...], dl_ref[...]], axis=1)
        dw_ref[...] += _dot_tn((xh * g).astype(BF16), dproj)
        du = _dot_nt(dproj, w_ref[...])
        dgn_ref[...] += jnp.sum(du * xh, axis=0, keepdims=True)
        o_ref[...] = dh_ref[...] + _rms_bwd(xh, r, du * g)

    rows = lambda w: pl.BlockSpec((tm, w), lambda i: (i, 0))
    return _call_with_comm(
        body, comm, grid=(cfg.TP // tm,), name="inproj_bwd",
        in_specs=[rows(D_MODEL), pl.BlockSpec((1, D_MODEL), lambda i: (0, 0)), pl.BlockSpec((D_MODEL, W_IN_P), lambda i: (0, 0)),
                  rows(512), rows(256), rows(1024), rows(512), rows(D_MODEL)],
        out_specs=[rows(D_MODEL), pl.BlockSpec((D_MODEL, W_IN_P), lambda i: (0, 0)), pl.BlockSpec((1, D_MODEL), lambda i: (0, 0))],
        out_shape=[jax.ShapeDtypeStruct((cfg.TP, D_MODEL), F32), jax.ShapeDtypeStruct((D_MODEL, W_IN_P), F32),
                   jax.ShapeDtypeStruct((1, D_MODEL), F32)],
        args=(h, g, w_p, dqa, dkva, dgate, dlat, dh))


def _meta_grad(cfg, dh):
    def body(d_ref, o_ref):
        @pl.when(pl.program_id(0) == 0)
        def _():
            o_ref[...] = d_ref[...]

        @pl.when(pl.program_id(0) > 0)
        def _():
            o_ref[...] += d_ref[...]

    return _pallas(
        body, grid=(cfg.B,), name="meta_grad",
        in_specs=[pl.BlockSpec((BLK, D_MODEL), lambda b: (b * cfg.NJ, 0))],
        out_specs=pl.BlockSpec((BLK, D_MODEL), lambda b: (0, 0)),
        out_shape=jax.ShapeDtypeStruct((BLK, D_MODEL), F32),
        compiler_params=_cp("arbitrary"),
    )(dh)


MATRICES = ("w_in", "w_uq", "w_ukv", "w_out")


def _local_grads(cfg, x, target, meta_of, table, small, weight_of, rider=None):
    def ride(stage, i, mats):
        hook = rider(stage, i, mats) if rider else None
        return hook if hook else (None, lambda res: None)

    depth = small["norm_in"].shape[0]
    rel, vis = _window_structure(cfg.NJ)
    bucket = _t5_bucket(jnp.asarray(rel))
    maskadd = jnp.asarray(np.where(vis, 0.0, NEG).astype(np.float32))
    c_tab, s_tab = _rope_tables(cfg)
    comm, deliver = ride("bias_build", 0, {})
    bias, *travelled = _bias_build(table, bucket, maskadd, comm)
    deliver(travelled)

    meta_blk = jnp.concatenate([meta_of(), jnp.zeros((BLK - N_META, D_MODEL), F32)], axis=0)
    h = jnp.concatenate([jnp.broadcast_to(meta_blk[None], (cfg.B, BLK, D_MODEL)), x], axis=1).reshape(cfg.TP, D_MODEL)

    wp, saved = [], []
    for i in range(depth):
        w = dict(w_in=_w_in_to_p(weight_of(i, "w_in")),
                 g_in=small["norm_in"][i][None], gq=small["norm_q_lat"][i][None], gkv=small["norm_kv_lat"][i][None],
                 goa=_perm_heads64(small["norm_out_a"][i], 0)[None], gob=small["norm_out_b"][i][None], sink=small["sink_a"][i])
        wp.append(w)
        comm, deliver = ride("inproj_fwd", i, {})
        pa, pf, *travelled = _inproj_fwd(cfg, h, w["g_in"], w["w_in"], comm)
        deliver(travelled)
        w.update(w_uq=_w_uq_to_p(weight_of(i, "w_uq")), w_ukv=_w_ukv_to_p(weight_of(i, "w_ukv")), w_out=_w_out_to_p(weight_of(i, "w_out")))
        q, k, v, kt, vt = _lat_fwd(cfg, pf, w["gq"], w["gkv"], w["w_uq"], w["w_ukv"], c_tab, s_tab)
        ya, lse_a = _win_fwd(cfg, pa, bias, w["sink"])
        comm, deliver = ride("mla_fwd", i, {})
        yb, lse_b, *travelled = _mla_fwd(cfg, q, kt, v, comm)
        deliver(travelled)
        h_next = _out_fwd(cfg, ya, yb, pf, w["goa"], w["gob"], w["w_out"], h)
        saved.append(dict(h=h, pa=pa, pf=pf, q=q, k=k, kt=kt, vt=vt, ya=ya, lse_a=lse_a, yb=yb, lse_b=lse_b))
        h = h_next

    dh, loss_tile, d_norm_final = _loss_bwd(cfg, h, target.reshape(cfg.B * cfg.S, D_MODEL), small["norm_final"][None])

    grads = {k_: [] for k_ in ("norm_in", "sink_a", "norm_q_lat", "norm_kv_lat", "norm_out_a", "norm_out_b")}
    mats, s_accs = {}, []
    for i in reversed(range(depth)):
        w, sv = wp[i], saved[i]
        dya, dyb, dgate, dwo, dgoa, dgob = _out_bwd(cfg, dh, sv["ya"], sv["yb"], sv["pf"], w["goa"], w["gob"], w["w_out"])
        dqa, dkp, dvp, dkm, dvm, s_acc, dsink = _win_bwd(cfg, sv["pa"], bias, w["sink"], dya, sv["ya"], sv["lse_a"])
        dkva = _win_dkv_combine(cfg, dkp, dvp, dkm, dvm)
        comm, deliver = ride("mla_bwd", i, mats)
        dq, dk, dv, *travelled = _mla_bwd(cfg, sv["q"], sv["k"], sv["kt"], sv["vt"], dyb, sv["yb"], sv["lse_b"], comm)
        deliver(travelled)
        dlat, dwq, dwkv, dgq, dgkv = _lat_bwd(cfg, dq, dk, dv, sv["pf"], w["gq"], w["gkv"], w["w_uq"], w["w_ukv"], c_tab, s_tab)
        mats[i] = dict(w_uq=_w_uq_from_p(dwq), w_ukv=_w_ukv_from_p(dwkv), w_out=_w_out_from_p(dwo))
        comm, deliver = ride("inproj_bwd", i, mats)
        dh, dwin, dgin, *travelled = _inproj_bwd(cfg, sv["h"], w["g_in"], w["w_in"], dqa, dkva, dgate, dlat, dh, comm)
        deliver(travelled)
        s_accs.append(s_acc)
        mats[i]["w_in"] = _w_in_from_p(dwin)
        grads["norm_in"].append(dgin[0])
        grads["sink_a"].append(dsink[0, :A_HEADS])
        grads["norm_q_lat"].append(dgq[0])
        grads["norm_kv_lat"].append(dgkv[0])
        grads["norm_out_a"].append(_unperm_heads64(dgoa[0], 0))
        grads["norm_out_b"].append(dgob[0])

    out = {k_: jnp.stack(v_[::-1]) for k_, v_ in grads.items()}
    mats["meta_tokens"] = _meta_grad(cfg, dh)[:N_META]
    comm, deliver = ride("bias_grad", 0, mats)
    dtable, *travelled = _bias_grad(s_accs, bucket, comm)
    deliver(travelled)
    out["rel_bias_table"] = dtable[:, 0].reshape(N_BUCKETS, A_HEADS)
    out["norm_final"] = d_norm_final[0]
    return loss_tile[0, 0], dh.reshape(cfg.B, cfg.LP, D_MODEL)[:, BLK:], out, mats


MESH = pl.DeviceIdType.MESH
ANY = pl.BlockSpec(memory_space=pl.ANY)


def _place():
    x, y, c = lax.axis_index("x"), lax.axis_index("y"), lax.axis_index("c")
    others = [(1 - x, y), (x, 1 - y), (1 - x, 1 - y)]
    return x, y, c, others


Comm = collections.namedtuple("Comm", "inputs out_shapes scratch start wait")


def _gather_comm(shards):
    n = len(shards)

    def copies(ins, outs, sems, arriving):
        send_sems, recv_sems, local_sems = sems
        x, y, c, others = _place()
        k_me = 2 * x + y
        local = [pltpu.make_async_copy(ins[a], outs[a].at[k_me], local_sems.at[a]) for a in range(n)]
        remote = [pltpu.make_async_remote_copy(src_ref=ins[a], dst_ref=outs[a].at[2 * ox + oy if arriving else k_me],
                                               send_sem=send_sems.at[3 * a + j], recv_sem=recv_sems.at[3 * a + j],
                                               device_id=(ox, oy, c), device_id_type=MESH)
                  for a in range(n) for j, (ox, oy) in enumerate(others)]
        return local, remote

    def start(ins, outs, sems):
        local, sends = copies(ins, outs, sems, arriving=False)
        for cp in local + sends:
            cp.start()

    def wait(ins, outs, sems):
        local, recvs = copies(ins, outs, sems, arriving=True)
        for cp in recvs:
            cp.wait_recv()
        for cp in recvs:
            cp.wait_send()
        for cp in local:
            cp.wait()

    return Comm(list(shards), [jax.ShapeDtypeStruct((4, *s.shape), s.dtype) for s in shards],
                [pltpu.SemaphoreType.DMA((3 * n,)), pltpu.SemaphoreType.DMA((3 * n,)), pltpu.SemaphoreType.DMA((n,))], start, wait)


def _gather_halves_comm(shards):
    n = len(shards)

    def copies(ins, outs, sems, kind):
        send_sems, recv_sems, fwd_send_sems, fwd_recv_sems, local_sems = sems
        x, y, c, others = _place()
        k_me = 2 * x + y

        def half(ref, which):
            rows = ref.shape[0] // 2
            return ref.at[pl.ds(pl.multiple_of(which * rows, 8), rows)]

        if kind == "local":
            return [pltpu.make_async_copy(ins[a], outs[a].at[k_me], local_sems.at[a]) for a in range(n)]
        made = []
        for a in range(n):
            for j, (ox, oy) in enumerate(others):
                slot = outs[a].at[k_me if kind == "sent" else 2 * ox + oy]
                if kind in ("sent", "arrived"):
                    made.append(pltpu.make_async_remote_copy(
                        src_ref=half(ins[a], c), dst_ref=half(slot, c), send_sem=send_sems.at[3 * a + j],
                        recv_sem=recv_sems.at[3 * a + j], device_id=(ox, oy, c), device_id_type=MESH))
                else:
                    which = c if kind == "forward" else 1 - c
                    made.append(pltpu.make_async_remote_copy(
                        src_ref=half(slot, which), dst_ref=half(slot, which), send_sem=fwd_send_sems.at[3 * a + j],
                        recv_sem=fwd_recv_sems.at[3 * a + j], device_id=(x, y, 1 - c), device_id_type=MESH))
        return made

    def start(ins, outs, sems):
        for cp in copies(ins, outs, sems, "local") + copies(ins, outs, sems, "sent"):
            cp.start()

    def wait(ins, outs, sems):
        arrived, forward = copies(ins, outs, sems, "arrived"), copies(ins, outs, sems, "forward")
        for came, on in zip(arrived, forward):
            came.wait_recv()
            on.start()
        for cp in copies(ins, outs, sems, "forwarded"):
            cp.wait_recv()
        for cp in arrived + forward:
            cp.wait_send()
        for cp in copies(ins, outs, sems, "local"):
            cp.wait()

    return Comm(list(shards), [jax.ShapeDtypeStruct((4, *s.shape), s.dtype) for s in shards],
                [pltpu.SemaphoreType.DMA((3 * n,))] * 4 + [pltpu.SemaphoreType.DMA((n,))], start, wait)


def _scatter_comm(parts):
    n = len(parts)

    def copies(ins, outs, sems):
        send_sems, recv_sems = sems
        x, y, c, others = _place()
        return [pltpu.make_async_remote_copy(src_ref=ins[a].at[2 * ox + oy], dst_ref=outs[a].at[j], send_sem=send_sems.at[3 * a + j],
                                             recv_sem=recv_sems.at[3 * a + j], device_id=(ox, oy, c), device_id_type=MESH)
                for a in range(n) for j, (ox, oy) in enumerate(others)]

    def start(ins, outs, sems):
        for cp in copies(ins, outs, sems):
            cp.start()

    def wait(ins, outs, sems):
        cps = copies(ins, outs, sems)
        for cp in cps:
            cp.wait_recv()
        for cp in cps:
            cp.wait_send()

    return Comm(list(parts), [jax.ShapeDtypeStruct((3, *p.shape[1:]), p.dtype) for p in parts],
                [pltpu.SemaphoreType.DMA((3 * n,)), pltpu.SemaphoreType.DMA((3 * n,))], start, wait)


def _call_with_comm(body, comm, *, grid, name, in_specs, out_specs, out_shape, args, scratch_shapes=()):
    if comm is None:
        return _pallas(body, grid=grid, name=name, in_specs=in_specs, out_specs=out_specs, out_shape=out_shape,
                              scratch_shapes=list(scratch_shapes), compiler_params=_cp(*["arbitrary"] * len(grid)))(*args)
    n_in, n_out, ci, co, ns = len(in_specs), len(out_specs), len(comm.inputs), len(comm.out_shapes), len(scratch_shapes)

    def wrapped(*refs):
        ins, cins = refs[:n_in], refs[n_in:n_in + ci]
        outs, couts = refs[n_in + ci:n_in + ci + n_out], refs[n_in + ci + n_out:n_in + ci + n_out + co]
        scratch, sems = refs[n_in + ci + n_out + co:n_in + ci + n_out + co + ns], refs[n_in + ci + n_out + co + ns:]
        ids = [pl.program_id(a) for a in range(len(grid))]
        first = functools.reduce(jnp.logical_and, [i == 0 for i in ids])
        last = functools.reduce(jnp.logical_and, [i == g - 1 for i, g in zip(ids, grid)])

        @pl.when(first)
        def _():
            comm.start(cins, couts, sems)

        body(*ins, *outs, *scratch)

        @pl.when(last)
        def _():
            comm.wait(cins, couts, sems)

    return _pallas(
        wrapped, grid=grid, name=name + "_comm", in_specs=[*in_specs, *[ANY] * ci], out_specs=[*out_specs, *[ANY] * co],
        out_shape=[*out_shape, *comm.out_shapes], scratch_shapes=[*scratch_shapes, *comm.scratch],
        compiler_params=_cp(*["arbitrary"] * len(grid)))(*args, *comm.inputs)


def _swap_sibling(arrs):
    n = len(arrs)

    def body(*refs):
        ins, outs = refs[:n], refs[n:2 * n]
        send_sems, recv_sems = refs[2 * n:]
        x, y, c, _ = _place()
        copies = [pltpu.make_async_remote_copy(src_ref=ins[a], dst_ref=outs[a], send_sem=send_sems.at[a], recv_sem=recv_sems.at[a],
                                               device_id=(x, y, 1 - c), device_id_type=MESH) for a in range(n)]
        for cp in copies:
            cp.start()
        for cp in copies:
            cp.wait_recv()
        for cp in copies:
            cp.wait_send()

    return _pallas(
        body, name="swap_sibling", in_specs=[ANY] * n, out_specs=[ANY] * n,
        out_shape=[jax.ShapeDtypeStruct(a.shape, a.dtype) for a in arrs],
        scratch_shapes=[pltpu.SemaphoreType.DMA((n,)), pltpu.SemaphoreType.DMA((n,))],
    )(*arrs)


def _allreduce_small(v):
    def body(v_ref, o_ref, buf, send_sems, recv_sems):
        x, y, c, _ = _place()
        me = 4 * x + 2 * y + c
        buf[me] = v_ref[...]

        def copy(r):
            tx, ty, tc = (x + (r >> 2)) % 2, (y + ((r >> 1) & 1)) % 2, (c + (r & 1)) % 2
            return tx, ty, tc

        sends = []
        for r in range(1, 8):
            tx, ty, tc = copy(r)
            sends.append(pltpu.make_async_remote_copy(src_ref=v_ref, dst_ref=buf.at[me], send_sem=send_sems.at[r - 1],
                                                      recv_sem=recv_sems.at[r - 1], device_id=(tx, ty, tc), device_id_type=MESH))
        for cp in sends:
            cp.start()
        for r in range(1, 8):
            tx, ty, tc = copy(r)
            pltpu.make_async_remote_copy(src_ref=v_ref, dst_ref=buf.at[4 * tx + 2 * ty + tc], send_sem=send_sems.at[r - 1],
                                         recv_sem=recv_sems.at[r - 1], device_id=(tx, ty, tc), device_id_type=MESH).wait_recv()
        for cp in sends:
            cp.wait_send()
        acc = buf[0]
        for d in range(1, 8):
            acc = acc + buf[d]
        o_ref[...] = acc

    return pl.pallas_call(
        body, name="allreduce_small", in_specs=[pl.BlockSpec(memory_space=pltpu.VMEM)], out_specs=pl.BlockSpec(memory_space=pltpu.VMEM),
        out_shape=jax.ShapeDtypeStruct(v.shape, F32),
        scratch_shapes=[pltpu.VMEM((8, *v.shape), F32), pltpu.SemaphoreType.DMA((7,)), pltpu.SemaphoreType.DMA((7,))],
    )(v)


def _rows_view(a):
    return a.reshape(-1, a.shape[-1])


def _elementwise(name, fn, ins, n_out):
    rows, cols = ins[0].shape
    tm = min(rows, 256)
    spec = pl.BlockSpec((tm, cols), lambda i: (i, 0))

    def body(*refs):
        outs = fn(*[r[...] for r in refs[:len(ins)]])
        for o_ref, o in zip(refs[len(ins):], outs):
            o_ref[...] = o

    return _pallas(
        body, grid=(rows // tm,), name=name, in_specs=[spec] * len(ins), out_specs=[spec] * n_out,
        out_shape=[jax.ShapeDtypeStruct((rows, cols), F32)] * n_out, compiler_params=_cp("parallel"),
    )(*ins)


def _sum_parts(name, own, recv):
    def fn(o, r0, r1, r2):
        return (o + r0.astype(F32) + r1.astype(F32) + r2.astype(F32),)

    return _elementwise("sum_parts_" + name, fn, [own, recv[0], recv[1], recv[2]], 1)[0]


def _adamw(name, w, m, v, g_parts):
    def fn(w_, m_, v_, *gs):
        g = gs[0]
        for extra in gs[1:]:
            g = g + extra
        m_new = ADAM_B1 * m_ + (1.0 - ADAM_B1) * g
        v_new = ADAM_B2 * v_ + (1.0 - ADAM_B2) * (g * g)
        m_hat = m_new / (1.0 - ADAM_B1 ** ADAM_STEP)
        v_hat = v_new / (1.0 - ADAM_B2 ** ADAM_STEP)
        delta = -ADAM_LR * (m_hat / (jnp.sqrt(v_hat) + ADAM_EPS) + ADAM_WD * w_)
        return g, delta, m_new, v_new

    return _elementwise("adamw_" + name, fn, [w, m, v, *g_parts], 4)


MAT_AXIS = {"w_in": 1, "w_uq": 1, "w_ukv": 1, "w_out": 0}
SMALL = ("rel_bias_table", "norm_in", "sink_a", "norm_q_lat", "norm_kv_lat", "norm_out_a", "norm_out_b", "norm_final")
WEIGHTS = ("meta_tokens", "rel_bias_table", "norm_in", "w_in", "sink_a", "norm_q_lat", "w_uq", "norm_kv_lat", "w_ukv",
           "norm_out_a", "norm_out_b", "w_out", "norm_final")
SMALL_ROWS, SMALL_COLS = 8, 1024


def _pack_small(d, loss=None):
    flat = [d[n].reshape(-1) for n in SMALL]
    if loss is not None:
        flat.append(loss.reshape(1))
    used = sum(f.shape[0] for f in flat)
    flat.append(jnp.zeros((SMALL_ROWS * SMALL_COLS - used,), F32))
    return jnp.concatenate(flat).reshape(SMALL_ROWS, SMALL_COLS)


def _unpack_small(p, like):
    flat, out, off = p.reshape(-1), {}, 0
    for n in SMALL:
        size = int(np.prod(like[n].shape))
        out[n] = flat[off:off + size].reshape(like[n].shape)
        off += size
    return out, flat[off]


def _split4(a, axis):
    size = a.shape[axis] // 4
    return jnp.stack([lax.slice_in_dim(a, k * size, (k + 1) * size, axis=axis) for k in range(4)])


def _train_step(cfg, x, target, w, m, v):
    depth = w["w_in"].shape[0]
    rest = tuple(n for n in MATRICES if n != "w_in")
    weights, splits, received = {}, {}, {}

    def gather(i, names, also=(), build=_gather_comm):
        def deliver(res):
            for n, g in zip(names, res):
                weights[i, n] = jnp.concatenate([g[k] for k in range(4)], axis=MAT_AXIS[n])

        return build([w[n][i].astype(BF16) for n in names] + list(also)), deliver

    def scatter(i, names, mats, also=()):
        for n in names:
            splits[i, n] = _split4(mats[i][n], MAT_AXIS[n])

        def deliver(res):
            for n, r in zip(names, res):
                received[i, n] = r

        return _scatter_comm([splits[i, n].astype(BF16) for n in names] + list(also)), deliver

    def rider(stage, i, mats):
        if stage == "bias_build":
            comm, deliver = gather(0, ("w_in",), also=[w["meta_tokens"]], build=_gather_halves_comm)

            def deliver_first(res):
                deliver(res)
                weights["meta"] = jnp.concatenate([res[1][k] for k in range(4)], axis=1)

            return comm, deliver_first
        if stage == "inproj_fwd" and i == 0:
            return gather(0, rest)
        if stage == "mla_fwd" and i + 1 < depth:
            return gather(i + 1, MATRICES)
        if stage == "mla_bwd" and i + 1 < depth:
            return scatter(i + 1, MATRICES, mats)
        if stage == "inproj_bwd" and i == 0:
            return scatter(0, rest, mats)
        if stage == "bias_grad":
            splits["meta"] = _split4(mats["meta_tokens"], 1)
            comm, deliver = scatter(0, ("w_in",), mats, also=[splits["meta"].astype(BF16)])

            def deliver_last(res):
                deliver(res)
                received["meta"] = res[1]

            return comm, deliver_last
        return None

    loss_local, grad_x, g, mats = _local_grads(cfg, x, target, lambda: weights["meta"], w["rel_bias_table"], {n: w[n] for n in SMALL},
                                               lambda i, n: weights[i, n], rider)

    small_sum = _allreduce_small(_pack_small(g, loss_local))
    g_small, loss = _unpack_small(small_sum, {n: w[n] for n in SMALL})

    k_me = 2 * lax.axis_index("x") + lax.axis_index("y")

    def core_sum(name, split, recv):
        own = lax.dynamic_index_in_dim(split, k_me, 0, keepdims=False)
        return _sum_parts(name, _rows_view(own), recv.reshape(3, -1, recv.shape[-1]))

    partial = [core_sum("meta_tokens", splits["meta"], received["meta"])]
    for n in MATRICES:
        partial.append(jnp.concatenate([core_sum(f"{n}_{i}", splits[i, n], received[i, n]) for i in range(depth)], axis=0))
    sibling = _swap_sibling(partial)

    outs = {}
    for n, p_me, p_sib in zip(("meta_tokens", *MATRICES), partial, sibling):
        res = _adamw(n, _rows_view(w[n]), _rows_view(m[n]), _rows_view(v[n]), [p_me, p_sib])
        outs[n] = [r.reshape(w[n].shape) for r in res]
    res = _adamw("small", _pack_small(w), _pack_small(m), _pack_small(v), [_pack_small(g_small)])
    unpacked = [_unpack_small(r, {n: w[n] for n in SMALL})[0] for r in res]
    for n in SMALL:
        outs[n] = [u[n] for u in unpacked]

    result = [loss, grad_x]
    for field in range(4):
        result.extend(outs[n][field] for n in WEIGHTS)
    return tuple(result)


def kernel(x, meta_tokens, rel_bias_table, norm_in, w_in, sink_a, norm_q_lat, w_uq, norm_kv_lat, w_ukv, norm_out_a, norm_out_b, w_out, norm_final, loss_target, m_meta_tokens, m_rel_bias_table, m_norm_in, m_w_in, m_sink_a, m_norm_q_lat, m_w_uq, m_norm_kv_lat, m_w_ukv, m_norm_out_a, m_norm_out_b, m_w_out, m_norm_final, v_meta_tokens, v_rel_bias_table, v_norm_in, v_w_in, v_sink_a, v_norm_q_lat, v_w_uq, v_norm_kv_lat, v_w_ukv, v_norm_out_a, v_norm_out_b, v_w_out, v_norm_final):
    w = dict(zip(WEIGHTS, (meta_tokens, rel_bias_table, norm_in, w_in, sink_a, norm_q_lat, w_uq, norm_kv_lat, w_ukv, norm_out_a, norm_out_b, w_out, norm_final)))
    m = dict(zip(WEIGHTS, (m_meta_tokens, m_rel_bias_table, m_norm_in, m_w_in, m_sink_a, m_norm_q_lat, m_w_uq, m_norm_kv_lat, m_w_ukv, m_norm_out_a, m_norm_out_b, m_w_out, m_norm_final)))
    v = dict(zip(WEIGHTS, (v_meta_tokens, v_rel_bias_table, v_norm_in, v_w_in, v_sink_a, v_norm_q_lat, v_w_uq, v_norm_kv_lat, v_w_ukv, v_norm_out_a, v_norm_out_b, v_w_out, v_norm_final)))
    cfg = make_cfg(x.shape[0], x.shape[1])
    return _train_step(cfg, x, loss_target, w, m, v)
```

```python
import collections
import functools
import math

import jax
import jax.numpy as jnp
import numpy as np
from jax import lax
from jax.experimental import pallas as pl
from jax.experimental.pallas import tpu as pltpu

F32 = jnp.float32
BF16 = jnp.bfloat16

BLK = 128
N_META = 16
D_MODEL = 1024
A_HEADS, A_KV, A_DH = 8, 2, 64
B_HEADS, B_NOPE, B_ROPE, B_DV = 8, 64, 32, 64
Q_RANK, KV_RANK = 256, 128
N_BUCKETS, MAX_DIST = 32, 128
ROPE_THETA = 10000.0
EPS = 1e-6
IN_WIDTH = 2208
W_IN_P = 2304
NEG = -1e30
MASK_LANE = 96
LOG2E = math.log2(math.e)
Q_SCALE = (B_NOPE + B_ROPE) ** -0.5 * LOG2E
QA_SCALE = A_DH ** -0.5 * LOG2E
LN2 = math.log(2.0)
VMEM_LIMIT = 48 * 1024 * 1024

ADAM_LR, ADAM_B1, ADAM_B2, ADAM_EPS, ADAM_WD, ADAM_STEP = 0.001, 0.9, 0.999, 1e-08, 0.01, 10

Cfg = collections.namedtuple("Cfg", "B S NB NJ LP TP")


def make_cfg(batch, seq):
    nb = seq // BLK
    nj = nb + 1
    return Cfg(batch, seq, nb, nj, nj * BLK, batch * nj * BLK)


def _cp(*sem):
    return pltpu.CompilerParams(dimension_semantics=sem, vmem_limit_bytes=VMEM_LIMIT)


def _pallas(body, *, out_shape, **kw):
    pinned = jax.tree.map(lambda s: pltpu.HBM(s.shape, s.dtype), out_shape)
    call = pl.pallas_call(body, out_shape=pinned, **kw)
    return lambda *args: call(*[pltpu.with_memory_space_constraint(a, pltpu.HBM) for a in args])


def _dot(a, b):
    return jnp.dot(a, b, preferred_element_type=F32)


def _dot_nt(a, b):
    return lax.dot_general(a, b, (((1,), (1,)), ((), ())), preferred_element_type=F32)


def _dot_tn(a, b):
    return lax.dot_general(a, b, (((0,), (0,)), ((), ())), preferred_element_type=F32)


def _rms(x, width=None):
    n = x.shape[-1] if width is None else width
    r = lax.rsqrt(jnp.sum(x * x, axis=-1, keepdims=True) * (1.0 / n) + EPS)
    return x * r, r


def _rms_bwd(xhat, r, t):
    n = xhat.shape[-1]
    return r * (t - xhat * (jnp.sum(t * xhat, axis=-1, keepdims=True) * (1.0 / n)))


def _sigmoid(x):
    return 1.0 / (1.0 + jnp.exp(-x))


def _lane(shape):
    return lax.broadcasted_iota(jnp.int32, shape, len(shape) - 1)


def _swap_rope(x):
    n = x.shape[-1]
    lane = _lane(x.shape) % BLK
    up = pltpu.roll(x, n - 16, axis=x.ndim - 1)
    dn = pltpu.roll(x, 16, axis=x.ndim - 1)
    return jnp.where((lane >= 64) & (lane < 80), up, jnp.where((lane >= 80) & (lane < 96), dn, 0.0))


A_ORDER = (0, 4, 1, 5, 2, 6, 3, 7)


def _jtype(j, nj):
    return 0 if j == 0 else 1 if j == 1 else 3 if j == nj - 1 else 2


def _window_structure(nj):
    def pos(blk, r):
        return np.where(blk == 0, r, N_META + (blk - 1) * BLK + r)

    def valid(blk, r):
        return np.where(blk == 0, r < N_META, True)

    r = np.arange(BLK)
    rels, viss = [], []
    for j in (0, 1, 2, nj - 1):
        qpos = pos(j, r)[:, None]
        rel_t, vis_t = [], []
        for s, kb in enumerate((0, j - 1, j, j + 1)):
            slot_ok = (s == 0) or (1 <= kb <= nj - 1)
            kbc = min(max(kb, 0), nj - 1)
            kpos = pos(kbc, r)[None, :]
            rel = kpos - qpos
            v = valid(kbc, r)[None, :] & np.ones((BLK, 1), bool)
            if s > 0:
                v = v & (np.abs(rel) <= BLK)
            rel_t.append(rel)
            vis_t.append(v & slot_ok)
        rels.append(np.concatenate(rel_t, axis=1))
        viss.append(np.concatenate(vis_t, axis=1))
    return np.stack(rels).astype(np.int32), np.stack(viss)


def _t5_bucket(rel):
    nb = N_BUCKETS // 2
    max_exact = nb // 2
    ret = jnp.where(rel > 0, nb, 0)
    n = jnp.abs(rel)
    nf = jnp.maximum(n, 1).astype(jnp.float32)
    large = max_exact + (jnp.log(nf / max_exact) / math.log(MAX_DIST / max_exact) * (nb - max_exact)).astype(jnp.int32)
    large = jnp.minimum(large, nb - 1)
    return ret + jnp.where(n < max_exact, n, large)


def _perm_heads64(a, axis):
    parts = [lax.slice_in_dim(a, h * 64, (h + 1) * 64, axis=axis) for h in A_ORDER]
    return jnp.concatenate(parts, axis=axis)


def _unperm_heads64(a, axis):
    inv = [A_ORDER.index(h) for h in range(8)]
    parts = [lax.slice_in_dim(a, p * 64, (p + 1) * 64, axis=axis) for p in inv]
    return jnp.concatenate(parts, axis=axis)


def _w_in_to_p(w):
    sl = lambda a, b: lax.slice_in_dim(w, a, b, axis=1)
    z = lambda n: jnp.zeros((w.shape[0], n), w.dtype)
    return jnp.concatenate([_perm_heads64(sl(0, 512), 1), sl(512, 768), _perm_heads64(sl(768, 1280), 1), sl(1696, 2208),
                            sl(1280, 1536), sl(1536, 1664), z(64), sl(1664, 1696), z(32)], axis=1)


def _w_in_from_p(g):
    sl = lambda a, b: lax.slice_in_dim(g, a, b, axis=1)
    return jnp.concatenate([_unperm_heads64(sl(0, 512), 1), sl(512, 768), _unperm_heads64(sl(768, 1280), 1),
                            sl(1792, 2048), sl(2048, 2176), sl(2240, 2272), sl(1280, 1792)], axis=1)


def _w_uq_to_p(w):
    z = jnp.zeros((w.shape[0], 32), w.dtype)
    return jnp.concatenate([p for h in range(8) for p in (lax.slice_in_dim(w, h * 96, (h + 1) * 96, axis=1), z)], axis=1)


def _w_uq_from_p(g):
    return jnp.concatenate([lax.slice_in_dim(g, h * 128, h * 128 + 96, axis=1) for h in range(8)], axis=1)


def _w_ukv_to_p(w):
    z = jnp.zeros((w.shape[0], 64), w.dtype)
    ks = [p for h in range(8) for p in (lax.slice_in_dim(w, h * 128, h * 128 + 64, axis=1), z)]
    vs = [lax.slice_in_dim(w, h * 128 + 64, (h + 1) * 128, axis=1) for h in range(8)]
    return jnp.concatenate(ks + vs, axis=1)


def _w_ukv_from_p(g):
    parts = []
    for h in range(8):
        parts.append(lax.slice_in_dim(g, h * 128, h * 128 + 64, axis=1))
        parts.append(lax.slice_in_dim(g, 1024 + h * 64, 1024 + (h + 1) * 64, axis=1))
    return jnp.concatenate(parts, axis=1)


def _w_out_to_p(w):
    return jnp.concatenate([_perm_heads64(lax.slice_in_dim(w, 0, 512, axis=0), 0), lax.slice_in_dim(w, 512, 1024, axis=0)], axis=0)


def _w_out_from_p(g):
    return jnp.concatenate([_unperm_heads64(lax.slice_in_dim(g, 0, 512, axis=0), 0), lax.slice_in_dim(g, 512, 1024, axis=0)], axis=0)


def _rope_tables(cfg):
    half = B_ROPE // 2
    length = N_META + cfg.S
    freqs = ROPE_THETA ** (-jnp.arange(half, dtype=jnp.float32) / half)
    ang = jnp.arange(length, dtype=jnp.float32)[:, None] * freqs[None, :]
    cos, sin = jnp.cos(ang), jnp.sin(ang)

    def rows(t):
        return jnp.concatenate([t[:N_META], jnp.zeros((BLK - N_META, t.shape[1]), t.dtype), t[N_META:]], axis=0)

    ones = jnp.ones((length, 64), F32)
    zer = jnp.zeros((length, 32), F32)
    c_tab = rows(jnp.concatenate([ones, cos, cos, zer], axis=1))
    s_tab = rows(jnp.concatenate([zer, zer, -sin, sin, zer], axis=1))
    return c_tab, s_tab


def _inproj_fwd(cfg, h, g, w_p, comm=None):
    tm = 256

    def body(h_ref, g_ref, w_ref, pa_ref, pf_ref):
        xh, _ = _rms(h_ref[...])
        u = (xh * g_ref[...]).astype(BF16)
        acc = _dot(u, w_ref[...])
        pa_ref[:, :512] = (acc[:, :512] * QA_SCALE).astype(BF16)
        pa_ref[:, 512:] = acc[:, 512:768].astype(BF16)
        pf_ref[...] = acc[:, 768:]

    return _call_with_comm(
        body, comm, grid=(cfg.TP // tm,), name="inproj_fwd",
        in_specs=[pl.BlockSpec((tm, D_MODEL), lambda i: (i, 0)), pl.BlockSpec((1, D_MODEL), lambda i: (0, 0)),
                  pl.BlockSpec((D_MODEL, W_IN_P), lambda i: (0, 0))],
        out_specs=[pl.BlockSpec((tm, 768), lambda i: (i, 0)), pl.BlockSpec((tm, 1536), lambda i: (i, 0))],
        out_shape=[jax.ShapeDtypeStruct((cfg.TP, 768), BF16), jax.ShapeDtypeStruct((cfg.TP, 1536), F32)],
        args=(h, g, w_p))


def _lat_fwd(cfg, pf, gq, gkv, wq_p, wkv_p, c_tab, s_tab):
    nj = cfg.NJ

    def body(cq_ref, ckv_ref, kr_ref, gq_ref, gkv_ref, wq_ref, wkv_ref, c_ref, s_ref, q_ref, k_ref, v_ref, kt_ref, vt_ref):
        c1, s1 = c_ref[...], s_ref[...]
        c8, s8 = jnp.tile(c1, (1, 8)), jnp.tile(s1, (1, 8))
        mask_lane = _lane((BLK, 1024)) % BLK == MASK_LANE
        zero_row = (pl.program_id(1) == 0) & (lax.broadcasted_iota(jnp.int32, (BLK, 1024), 0) >= N_META)
        xq, _ = _rms(cq_ref[...])
        qp = _dot((xq * gq_ref[...]).astype(BF16), wq_ref[...])
        q_ref[...] = jnp.where(mask_lane, 1.0, (qp * c8 + _swap_rope(qp) * s8) * Q_SCALE).astype(BF16)
        xk, _ = _rms(ckv_ref[...])
        kvp = _dot((xk * gkv_ref[...]).astype(BF16), wkv_ref[...])
        kr = kr_ref[...]
        krr = kr * c1 + _swap_rope(kr) * s1
        k = jnp.where(mask_lane & zero_row, NEG, kvp[:, :1024] + jnp.tile(krr, (1, 8)))
        k_ref[...] = k.astype(BF16)
        v_ref[...] = kvp[:, 1024:].astype(BF16)
        kt_ref[...] = k.T.astype(BF16)
        vt_ref[...] = kvp[:, 1024:].T.astype(BF16)

    row = lambda b, j: b * nj + j
    return _pallas(
        body, grid=(cfg.B, nj), name="lat_fwd",
        in_specs=[pl.BlockSpec((BLK, 256), lambda b, j: (row(b, j), 4)), pl.BlockSpec((BLK, 128), lambda b, j: (row(b, j), 10)),
                  pl.BlockSpec((BLK, 128), lambda b, j: (row(b, j), 11)),
                  pl.BlockSpec((1, 256), lambda b, j: (0, 0)), pl.BlockSpec((1, 128), lambda b, j: (0, 0)),
                  pl.BlockSpec((256, 1024), lambda b, j: (0, 0)), pl.BlockSpec((128, 1536), lambda b, j: (0, 0)),
                  pl.BlockSpec((BLK, 128), lambda b, j: (j, 0)), pl.BlockSpec((BLK, 128), lambda b, j: (j, 0))],
        out_specs=[pl.BlockSpec((BLK, 1024), lambda b, j: (row(b, j), 0)), pl.BlockSpec((BLK, 1024), lambda b, j: (row(b, j), 0)),
                   pl.BlockSpec((BLK, 512), lambda b, j: (row(b, j), 0)),
                   pl.BlockSpec((1024, BLK), lambda b, j: (b, j)), pl.BlockSpec((512, BLK), lambda b, j: (b, j))],
        out_shape=[jax.ShapeDtypeStruct((cfg.TP, 1024), BF16), jax.ShapeDtypeStruct((cfg.TP, 1024), BF16),
                   jax.ShapeDtypeStruct((cfg.TP, 512), BF16),
                   jax.ShapeDtypeStruct((cfg.B * 1024, cfg.LP), BF16), jax.ShapeDtypeStruct((cfg.B * 512, cfg.LP), BF16)],
        compiler_params=_cp("parallel", "parallel"),
    )(pf, pf, pf, gq, gkv, wq_p, wkv_p, c_tab, s_tab)


def _gate_halves(ya, yb, ga, gb, goa, gob):
    xa, ra = _rms(ya)
    xb, rb = _rms(yb)
    sga, sgb = _sigmoid(ga), _sigmoid(gb)
    return xa, ra, xb, rb, sga, sgb, xa * goa * (ga * sga), xb * gob * (gb * sgb)


def _out_fwd(cfg, ya, yb, pf, goa, gob, wo_p, h):
    tm = 256

    def body(ya_ref, yb_ref, ga_ref, gb_ref, goa_ref, gob_ref, w_ref, h_ref, o_ref):
        *_, y_a, y_b = _gate_halves(ya_ref[...], yb_ref[...], ga_ref[...], gb_ref[...], goa_ref[...], gob_ref[...])
        y = jnp.concatenate([y_a, y_b], axis=1).astype(BF16)
        o_ref[...] = h_ref[...] + _dot(y, w_ref[...])

    return _pallas(
        body, grid=(cfg.TP // tm,), name="out_fwd",
        in_specs=[pl.BlockSpec((tm, 512), lambda i: (i, 0)), pl.BlockSpec((tm, 512), lambda i: (i, 0)),
                  pl.BlockSpec((tm, 512), lambda i: (i, 0)), pl.BlockSpec((tm, 512), lambda i: (i, 1)),
                  pl.BlockSpec((1, 512), lambda i: (0, 0)), pl.BlockSpec((1, 512), lambda i: (0, 0)),
                  pl.BlockSpec((D_MODEL, D_MODEL), lambda i: (0, 0)), pl.BlockSpec((tm, D_MODEL), lambda i: (i, 0))],
        out_specs=pl.BlockSpec((tm, D_MODEL), lambda i: (i, 0)),
        out_shape=jax.ShapeDtypeStruct((cfg.TP, D_MODEL), F32),
        compiler_params=_cp("parallel"),
    )(ya, yb, pf, pf, goa, gob, wo_p, h)


def _bias_build(table, bucket, maskadd, comm=None):
    def body(tab_ref, bk_ref, ma_ref, o_ref):
        def rows(g, carry):
            r = pl.ds(pl.multiple_of(g * 8, 8), 8)
            bk = bk_ref[0, r, :]
            accs = [jnp.zeros(bk.shape, F32)] * A_HEADS
            for b in range(N_BUCKETS):
                hit = bk == b
                accs = [jnp.where(hit, tab_ref[b, h], accs[h]) for h in range(A_HEADS)]
            ma = ma_ref[0, r, :]
            for h in range(A_HEADS):
                o_ref[0, h, r, :] = (accs[h] + ma) * LOG2E
            return carry

        lax.fori_loop(0, BLK // 8, rows, 0)

    return _call_with_comm(
        body, comm, grid=(4,), name="bias_build",
        in_specs=[pl.BlockSpec(memory_space=pltpu.SMEM), pl.BlockSpec((1, BLK, 512), lambda t: (t, 0, 0)),
                  pl.BlockSpec((1, BLK, 512), lambda t: (t, 0, 0))],
        out_specs=[pl.BlockSpec((1, A_HEADS, BLK, 512), lambda t: (t, 0, 0, 0))],
        out_shape=[jax.ShapeDtypeStruct((4, A_HEADS, BLK, 512), F32)],
        args=(table, bucket, maskadd))


def _bias_grad(s_accs, bucket, comm=None):
    depth = len(s_accs)

    def body(*refs):
        s_refs, bk_ref, o_ref, sum_ref, part_ref = refs[:depth], refs[depth], refs[depth + 1], refs[depth + 2], refs[depth + 3]
        t = pl.program_id(0)

        @pl.when(t == 0)
        def _():
            o_ref[...] = jnp.zeros_like(o_ref)

        total = s_refs[0][0]
        for extra in s_refs[1:]:
            total = total + extra[0]
        sum_ref[...] = total

        def step(b, carry):
            accs = [jnp.zeros((8, 512), F32) for _ in range(A_HEADS)]
            for g in range(BLK // 8):
                rows = pl.ds(g * 8, 8)
                hit = bk_ref[0, rows, :] == b
                for h in range(A_HEADS):
                    accs[h] = accs[h] + jnp.where(hit, sum_ref[h, rows, :], 0.0)
            rows8 = jnp.concatenate([jnp.sum(a, axis=0, keepdims=True) for a in accs], axis=0)
            part_ref[pl.ds(pl.multiple_of(b * A_HEADS, 8), A_HEADS), :] = rows8
            return carry

        lax.fori_loop(0, N_BUCKETS, step, 0)
        o_ref[...] += jnp.broadcast_to(jnp.sum(part_ref[...], axis=1, keepdims=True), o_ref.shape)

    s_spec = pl.BlockSpec((1, A_HEADS, BLK, 512), lambda t: (t, 0, 0, 0))
    return _call_with_comm(
        body, comm, grid=(4,), name="bias_grad",
        in_specs=[s_spec] * depth + [pl.BlockSpec((1, BLK, 512), lambda t: (t, 0, 0))],
        out_specs=[pl.BlockSpec((N_BUCKETS * A_HEADS, 128), lambda t: (0, 0))],
        out_shape=[jax.ShapeDtypeStruct((N_BUCKETS * A_HEADS, 128), F32)],
        scratch_shapes=[pltpu.VMEM((A_HEADS, BLK, 512), F32), pltpu.VMEM((N_BUCKETS * A_HEADS, 512), F32)],
        args=(*s_accs, bucket))


def _win_specs(cfg):
    nj = cfg.NJ
    row = lambda b, j: b * nj + j
    jt = lambda j: jnp.where(j == 0, 0, jnp.where(j == 1, 1, jnp.where(j == nj - 1, 3, 2)))
    slot_rows = [lambda b, j: row(b, 0), lambda b, j: row(b, jnp.maximum(j - 1, 0)), lambda b, j: row(b, j),
                 lambda b, j: row(b, jnp.minimum(j + 1, nj - 1))]
    k_specs = [pl.BlockSpec((BLK, 128), functools.partial(lambda b, j, f: (f(b, j), 4), f=f)) for f in slot_rows]
    v_specs = [pl.BlockSpec((BLK, 128), functools.partial(lambda b, j, f: (f(b, j), 5), f=f)) for f in slot_rows]
    q_spec = pl.BlockSpec((BLK, 512), lambda b, j: (row(b, j), 0))
    bias_spec = pl.BlockSpec((1, A_HEADS, BLK, 512), lambda b, j: (jt(j), 0, 0, 0))
    return row, jt, q_spec, k_specs, v_specs, bias_spec


def _stack4(ref):
    return jnp.concatenate([ref[:, c * 128:(c + 1) * 128] for c in range(4)], axis=0)


def _win_keys(k_refs, v_refs):
    k4 = jnp.concatenate([r[...] for r in k_refs], axis=0)
    v4 = jnp.concatenate([r[...] for r in v_refs], axis=0)
    lane_k = _lane(k4.shape)
    return (jnp.where(lane_k < 64, k4, jnp.zeros_like(k4)), jnp.where(lane_k >= 64, k4, jnp.zeros_like(k4))), v4


def _sink_col(sink_ref, hf):
    rowi = lax.broadcasted_iota(jnp.int32, (4 * BLK, 1), 0)
    col = jnp.full((4 * BLK, 1), sink_ref[4 * hf + 3], F32)
    for c in (2, 1, 0):
        col = jnp.where(rowi < (c + 1) * BLK, sink_ref[4 * hf + c], col)
    return col * LOG2E


def _win_fwd(cfg, pa, bias, sink):
    row, jt, q_spec, k_specs, v_specs, bias_spec = _win_specs(cfg)

    def body(sink_ref, q_ref, k0, k1, k2, k3, v0, v1, v2, v3, b_ref, o_ref, lse_ref):
        kk, v4 = _win_keys((k0, k1, k2, k3), (v0, v1, v2, v3))
        qs = _stack4(q_ref)
        lane_o = _lane((4 * BLK, 128))
        outs, lses = [], []
        for hf in range(2):
            s = _dot_nt(qs, kk[hf]) + b_ref[0, 4 * hf:4 * hf + 4].reshape(4 * BLK, 512)
            sink_col = _sink_col(sink_ref, hf)
            m = jnp.maximum(jnp.max(s, axis=1, keepdims=True), sink_col)
            e = jnp.exp2(s - m)
            den = jnp.sum(e, axis=1, keepdims=True) + jnp.exp2(sink_col - m)
            outs.append(_dot(e.astype(BF16), v4) / den)
            lses.append(m + jnp.log2(den))
        o = jnp.where(lane_o < 64, outs[0], outs[1])
        for c in range(4):
            o_ref[:, c * 128:(c + 1) * 128] = o[c * BLK:(c + 1) * BLK]
        lse_ref[...] = jnp.where(lane_o == 0, lses[0], jnp.where(lane_o == 1, lses[1], 0.0))

    return _pallas(
        body, grid=(cfg.B, cfg.NJ), name="win_fwd",
        in_specs=[pl.BlockSpec(memory_space=pltpu.SMEM), q_spec, *k_specs, *v_specs, bias_spec],
        out_specs=[pl.BlockSpec((BLK, 512), lambda b, j: (row(b, j), 0)), pl.BlockSpec((4 * BLK, 128), lambda b, j: (row(b, j), 0))],
        out_shape=[jax.ShapeDtypeStruct((cfg.TP, 512), F32), jax.ShapeDtypeStruct((4 * cfg.TP, 128), F32)],
        compiler_params=_cp("parallel", "parallel"),
    )(sink, pa, *([pa] * 8), bias)


def _win_bwd(cfg, pa, bias, sink, dya, ya, lse):
    row, jt, q_spec, k_specs, v_specs, bias_spec = _win_specs(cfg)
    nj = cfg.NJ

    def body(sink_ref, q_ref, k0, k1, k2, k3, v0, v1, v2, v3, b_ref, dy_ref, y_ref, lse_ref,
             dq_ref, dkp_ref, dvp_ref, dkm_ref, dvm_ref, s_ref, dsink_ref):
        j = pl.program_id(1)
        kind = jt(j)

        @pl.when((pl.program_id(0) == 0) & (j == 0))
        def _():
            s_ref[...] = jnp.zeros_like(s_ref)

        kk, v4 = _win_keys((k0, k1, k2, k3), (v0, v1, v2, v3))
        qs, dys, ys = _stack4(q_ref), _stack4(dy_ref), _stack4(y_ref)
        lane_o = _lane((4 * BLK, 128))
        half = (lane_o < 64, lane_o >= 64)
        lse_blk = lse_ref[...]
        dq = jnp.zeros((4 * BLK, 128), F32)
        dk4 = jnp.zeros((512, 128), F32)
        dv4 = jnp.zeros((512, 128), F32)
        dsink = jnp.zeros((8, 128), F32)
        lane_s = _lane((8, 128))
        row_s = lax.broadcasted_iota(jnp.int32, (8, 128), 0)
        for hf in range(2):
            lse_h = jnp.sum(jnp.where(lane_o == hf, lse_blk, 0.0), axis=1, keepdims=True)
            s = _dot_nt(qs, kk[hf]) + b_ref[0, 4 * hf:4 * hf + 4].reshape(4 * BLK, 512)
            p = jnp.exp2(s - lse_h)
            do_h = jnp.where(half[hf], dys, 0.0)
            delta = jnp.sum(do_h * ys, axis=1, keepdims=True)
            do_b = do_h.astype(BF16)
            ds = p * (_dot_nt(do_b, v4) - delta)
            s_ref[kind, 4 * hf:4 * hf + 4] += ds.reshape(4, BLK, 512)
            sink_grad = jnp.exp2(_sink_col(sink_ref, hf) - lse_h) * delta
            for c in range(4):
                tot = -jnp.sum(sink_grad[c * BLK:(c + 1) * BLK])
                dsink = jnp.where((row_s == 0) & (lane_s == 4 * hf + c), tot, dsink)
            dsb = (ds * LN2).astype(BF16)
            dq = dq + _dot(dsb, kk[hf])
            dk4 = dk4 + _dot_tn(dsb, jnp.where(half[hf], qs, jnp.zeros_like(qs)))
            dv4 = dv4 + _dot_tn(p.astype(BF16), do_b)
        for c in range(4):
            dq_ref[:, c * 128:(c + 1) * 128] = (dq[c * BLK:(c + 1) * BLK] * QA_SCALE).astype(BF16)
        dkp_ref[0] = dk4
        dvp_ref[0] = dv4

        @pl.when(j == 0)
        def _():
            dkm_ref[...] = dk4[:BLK]
            dvm_ref[...] = dv4[:BLK]

        @pl.when(j > 0)
        def _():
            dkm_ref[...] += dk4[:BLK]
            dvm_ref[...] += dv4[:BLK]

        @pl.when((pl.program_id(0) == 0) & (j == 0))
        def _():
            dsink_ref[...] = dsink

        @pl.when((pl.program_id(0) > 0) | (j > 0))
        def _():
            dsink_ref[...] += dsink

    blk_row = pl.BlockSpec((BLK, 512), lambda b, j: (row(b, j), 0))
    return _pallas(
        body, grid=(cfg.B, nj), name="win_bwd",
        in_specs=[pl.BlockSpec(memory_space=pltpu.SMEM), q_spec, *k_specs, *v_specs, bias_spec, blk_row, blk_row,
                  pl.BlockSpec((4 * BLK, 128), lambda b, j: (row(b, j), 0))],
        out_specs=[blk_row,
                   pl.BlockSpec((1, 512, 128), lambda b, j: (row(b, j), 0, 0)), pl.BlockSpec((1, 512, 128), lambda b, j: (row(b, j), 0, 0)),
                   pl.BlockSpec((BLK, 128), lambda b, j: (b, 0)), pl.BlockSpec((BLK, 128), lambda b, j: (b, 0)),
                   pl.BlockSpec((4, A_HEADS, BLK, 512), lambda b, j: (0, 0, 0, 0)),
                   pl.BlockSpec((8, 128), lambda b, j: (0, 0))],
        out_shape=[jax.ShapeDtypeStruct((cfg.TP, 512), BF16),
                   jax.ShapeDtypeStruct((cfg.B * nj, 512, 128), F32), jax.ShapeDtypeStruct((cfg.B * nj, 512, 128), F32),
                   jax.ShapeDtypeStruct((cfg.B * BLK, 128), F32), jax.ShapeDtypeStruct((cfg.B * BLK, 128), F32),
                   jax.ShapeDtypeStruct((4, A_HEADS, BLK, 512), F32),
                   jax.ShapeDtypeStruct((8, 128), F32)],
        compiler_params=_cp("arbitrary", "arbitrary"),
    )(sink, pa, *([pa] * 8), bias, dya, ya, lse)


def _win_dkv_combine(cfg, dkp, dvp, dkm, dvm):
    nj = cfg.NJ

    def body(kp, vp, km, vm, o_ref):
        o_ref[:BLK, :128] = km[...].astype(BF16)
        o_ref[:BLK, 128:] = vm[...].astype(BF16)
        for kb in range(1, nj):
            for col, part in ((0, kp), (128, vp)):
                tot = part[kb, 2 * BLK:3 * BLK] + part[kb - 1, 3 * BLK:4 * BLK]
                if kb + 1 < nj:
                    tot = tot + part[kb + 1, BLK:2 * BLK]
                o_ref[kb * BLK:(kb + 1) * BLK, col:col + 128] = tot.astype(BF16)

    return _pallas(
        body, grid=(cfg.B,), name="win_dkv_combine",
        in_specs=[pl.BlockSpec((nj, 512, 128), lambda b: (b, 0, 0)), pl.BlockSpec((nj, 512, 128), lambda b: (b, 0, 0)),
                  pl.BlockSpec((BLK, 128), lambda b: (b, 0)), pl.BlockSpec((BLK, 128), lambda b: (b, 0))],
        out_specs=pl.BlockSpec((cfg.LP, 256), lambda b: (b, 0)),
        out_shape=jax.ShapeDtypeStruct((cfg.TP, 256), BF16),
        compiler_params=_cp("parallel"),
    )(dkp, dvp, dkm, dvm)


def _pair_blockdiag(q):
    lane = _lane(q.shape)
    return jnp.concatenate([jnp.where(lane < 128, q, jnp.zeros_like(q)), jnp.where(lane >= 128, q, jnp.zeros_like(q))], axis=0)


def _mla_fwd(cfg, q, kt, v, comm=None):
    nj, lp = cfg.NJ, cfg.LP

    def body(q_ref, kt_ref, v_ref, o_ref, lse_ref, s_even, s_odd):
        i = pl.program_id(2)
        lane_o = _lane((BLK, 128))

        def logits(s_write):
            s_write[...] = _dot(_pair_blockdiag(q_ref[...]), kt_ref[...])

        def finish(s_read):
            s = s_read[...]
            m = jnp.max(s, axis=1, keepdims=True)
            e = jnp.exp2(s - m)
            den = jnp.sum(e, axis=1, keepdims=True)
            pv = _dot(e.astype(BF16), v_ref[...]) / den
            o_ref[...] = jnp.where(lane_o < 64, pv[:BLK], pv[BLK:])
            lse_ref[0] = jnp.broadcast_to(m + jnp.log2(den), (2 * BLK, 128))

        odd = i % 2 == 1

        @pl.when(i == 0)
        def _():
            logits(s_even)

        @pl.when(odd & (i < nj))
        def _():
            logits(s_odd)
            finish(s_even)

        @pl.when(jnp.logical_not(odd) & (i > 0) & (i < nj))
        def _():
            logits(s_even)
            finish(s_odd)

        @pl.when(i == nj)
        def _():
            finish(s_even if nj % 2 == 1 else s_odd)

    cur = lambda b, i: b * nj + jnp.minimum(i, nj - 1)
    prev = lambda b, i: b * nj + jnp.maximum(i - 1, 0)
    return _call_with_comm(
        body, comm, grid=(cfg.B, 4, nj + 1), name="mla_fwd",
        in_specs=[pl.BlockSpec((BLK, 256), lambda b, p, i: (cur(b, i), p)), pl.BlockSpec((256, lp), lambda b, p, i: (b * 4 + p, 0)),
                  pl.BlockSpec((lp, 128), lambda b, p, i: (b, p))],
        out_specs=[pl.BlockSpec((BLK, 128), lambda b, p, i: (prev(b, i), p)),
                   pl.BlockSpec((1, 2 * BLK, 128), lambda b, p, i: (p, prev(b, i), 0))],
        out_shape=[jax.ShapeDtypeStruct((cfg.TP, 512), F32), jax.ShapeDtypeStruct((4, 2 * cfg.TP, 128), F32)],
        scratch_shapes=[pltpu.VMEM((2 * BLK, lp), F32), pltpu.VMEM((2 * BLK, lp), F32)],
        args=(q, kt, v))


def _mla_bwd(cfg, q, k, kt, vt, dyb, yb, lse, comm=None):
    nj, lp = cfg.NJ, cfg.LP

    def body(q_ref, k_ref, kt_ref, vt_ref, dy_ref, y_ref, lse_ref, dq_ref, dk_ref, dv_ref):
        i = pl.program_id(2)

        @pl.when(i == 0)
        def _():
            dk_ref[...] = jnp.zeros_like(dk_ref)
            dv_ref[...] = jnp.zeros_like(dv_ref)

        lane_o = _lane((BLK, 128))
        qbd = _pair_blockdiag(q_ref[...])
        dy, y = dy_ref[...], y_ref[...]
        do_s = jnp.concatenate([jnp.where(lane_o < 64, dy, 0.0), jnp.where(lane_o >= 64, dy, 0.0)], axis=0)
        delta = jnp.sum(do_s * jnp.concatenate([y, y], axis=0), axis=1, keepdims=True)
        do_b = do_s.astype(BF16)
        p = jnp.exp2(_dot(qbd, kt_ref[...]) - lse_ref[0][:, :1])
        ds = p * (_dot(do_b, vt_ref[...]) - delta)
        dsb = (ds * LN2).astype(BF16)
        dq2 = _dot(dsb, k_ref[...])
        dq_ref[...] = jnp.where(_lane((BLK, 256)) < 128, dq2[:BLK], dq2[BLK:]) * Q_SCALE
        dk_ref[...] += _dot_tn(dsb, qbd)
        dv_ref[...] += _dot_tn(p.astype(BF16), do_b)

    return _call_with_comm(
        body, comm, grid=(cfg.B, 4, nj), name="mla_bwd",
        in_specs=[pl.BlockSpec((BLK, 256), lambda b, p, i: (b * nj + i, p)), pl.BlockSpec((lp, 256), lambda b, p, i: (b, p)),
                  pl.BlockSpec((256, lp), lambda b, p, i: (b * 4 + p, 0)), pl.BlockSpec((128, lp), lambda b, p, i: (b * 4 + p, 0)),
                  pl.BlockSpec((BLK, 128), lambda b, p, i: (b * nj + i, p)), pl.BlockSpec((BLK, 128), lambda b, p, i: (b * nj + i, p)),
                  pl.BlockSpec((1, 2 * BLK, 128), lambda b, p, i: (p, b * nj + i, 0))],
        out_specs=[pl.BlockSpec((BLK, 256), lambda b, p, i: (b * nj + i, p)), pl.BlockSpec((lp, 256), lambda b, p, i: (b, p)),
                   pl.BlockSpec((lp, 128), lambda b, p, i: (b, p))],
        out_shape=[jax.ShapeDtypeStruct((cfg.TP, 1024), F32), jax.ShapeDtypeStruct((cfg.TP, 1024), F32),
                   jax.ShapeDtypeStruct((cfg.TP, 512), F32)],
        args=(q, k, kt, vt, dyb, yb, lse))


def _loss_bwd(cfg, h, target, gf):
    nj, nb = cfg.NJ, cfg.NB

    def body(h_ref, t_ref, g_ref, dh_ref, loss_ref, dg_ref):
        b, j = pl.program_id(0), pl.program_id(1)

        @pl.when((b == 0) & (j == 0))
        def _():
            loss_ref[...] = jnp.zeros_like(loss_ref)
            dg_ref[...] = jnp.zeros_like(dg_ref)

        @pl.when(j == 0)
        def _():
            dh_ref[...] = jnp.zeros_like(dh_ref)

        @pl.when(j > 0)
        def _():
            g = g_ref[...]
            xh, r = _rms(h_ref[...])
            err = xh * g - t_ref[...]
            loss_ref[...] += jnp.where((lax.broadcasted_iota(jnp.int32, (8, 128), 0) == 0) & (_lane((8, 128)) == 0),
                                       (0.5 / D_MODEL) * jnp.sum(err * err), 0.0)
            dy = err * (1.0 / D_MODEL)
            dg_ref[...] += jnp.sum(dy * xh, axis=0, keepdims=True)
            dh_ref[...] = _rms_bwd(xh, r, dy * g)

    return _pallas(
        body, grid=(cfg.B, nj), name="loss_bwd",
        in_specs=[pl.BlockSpec((BLK, D_MODEL), lambda b, j: (b * nj + j, 0)),
                  pl.BlockSpec((BLK, D_MODEL), lambda b, j: (b * nb + jnp.maximum(j - 1, 0), 0)),
                  pl.BlockSpec((1, D_MODEL), lambda b, j: (0, 0))],
        out_specs=[pl.BlockSpec((BLK, D_MODEL), lambda b, j: (b * nj + j, 0)), pl.BlockSpec((8, 128), lambda b, j: (0, 0)),
                   pl.BlockSpec((1, D_MODEL), lambda b, j: (0, 0))],
        out_shape=[jax.ShapeDtypeStruct((cfg.TP, D_MODEL), F32), jax.ShapeDtypeStruct((8, 128), F32),
                   jax.ShapeDtypeStruct((1, D_MODEL), F32)],
        compiler_params=_cp("arbitrary", "arbitrary"),
    )(h, target, gf)


def _out_bwd(cfg, dh, ya, yb, pf, goa, gob, wo_p):
    tm = 256

    def body(dh_ref, ya_ref, yb_ref, ga_ref, gb_ref, goa_ref, gob_ref, w_ref,
             dya_ref, dyb_ref, dg_ref, dw_ref, dgoa_ref, dgob_ref):
        @pl.when(pl.program_id(0) == 0)
        def _():
            dw_ref[...] = jnp.zeros_like(dw_ref)
            dgoa_ref[...] = jnp.zeros_like(dgoa_ref)
            dgob_ref[...] = jnp.zeros_like(dgob_ref)

        ga, gb, goa, gob = ga_ref[...], gb_ref[...], goa_ref[...], gob_ref[...]
        xa, ra, xb, rb, sga, sgb, y_a, y_b = _gate_halves(ya_ref[...], yb_ref[...], ga, gb, goa, gob)
        dhb = dh_ref[...].astype(BF16)
        dw_ref[...] += _dot_tn(jnp.concatenate([y_a, y_b], axis=1).astype(BF16), dhb)
        dy = _dot_nt(dhb, w_ref[...])
        for (dyh, x, r, g, sg, go, dy_out, dgo_ref, col) in (
                (dy[:, :512], xa, ra, ga, sga, goa, dya_ref, dgoa_ref, 0), (dy[:, 512:], xb, rb, gb, sgb, gob, dyb_ref, dgob_ref, 512)):
            dn = dyh * (g * sg)
            dg_ref[:, col:col + 512] = (dyh * (x * go) * (sg * (1.0 + g * (1.0 - sg)))).astype(BF16)
            dgo_ref[...] += jnp.sum(dn * x, axis=0, keepdims=True)
            dy_out[...] = _rms_bwd(x, r, dn * go)

    half = lambda c: pl.BlockSpec((tm, 512), lambda i: (i, c))
    vec = pl.BlockSpec((1, 512), lambda i: (0, 0))
    return _pallas(
        body, grid=(cfg.TP // tm,), name="out_bwd",
        in_specs=[pl.BlockSpec((tm, D_MODEL), lambda i: (i, 0)), half(0), half(0), half(0), half(1), vec, vec,
                  pl.BlockSpec((D_MODEL, D_MODEL), lambda i: (0, 0))],
        out_specs=[half(0), half(0), pl.BlockSpec((tm, D_MODEL), lambda i: (i, 0)),
                   pl.BlockSpec((D_MODEL, D_MODEL), lambda i: (0, 0)), vec, vec],
        out_shape=[jax.ShapeDtypeStruct((cfg.TP, 512), F32), jax.ShapeDtypeStruct((cfg.TP, 512), F32),
                   jax.ShapeDtypeStruct((cfg.TP, D_MODEL), BF16), jax.ShapeDtypeStruct((D_MODEL, D_MODEL), F32),
                   jax.ShapeDtypeStruct((1, 512), F32), jax.ShapeDtypeStruct((1, 512), F32)],
        compiler_params=_cp("arbitrary"),
    )(dh, ya, yb, pf, pf, goa, gob, wo_p)


def _lat_bwd(cfg, dq, dk, dv, pf, gq, gkv, wq_p, wkv_p, c_tab, s_tab):
    nj = cfg.NJ

    def body(dq_ref, dk_ref, dv_ref, cq_ref, ckv_ref, gq_ref, gkv_ref, wq_ref, wkv_ref, c_ref, s_ref,
             dl_ref, dwq_ref, dwkv_ref, dgq_ref, dgkv_ref):
        @pl.when((pl.program_id(0) == 0) & (pl.program_id(1) == 0))
        def _():
            dwq_ref[...] = jnp.zeros_like(dwq_ref)
            dwkv_ref[...] = jnp.zeros_like(dwkv_ref)
            dgq_ref[...] = jnp.zeros_like(dgq_ref)
            dgkv_ref[...] = jnp.zeros_like(dgkv_ref)

        c1, s1 = c_ref[...], s_ref[...]
        c8, s8 = jnp.tile(c1, (1, 8)), jnp.tile(s1, (1, 8))
        dq_r = dq_ref[...]
        dqp = (dq_r * c8 + _swap_rope(dq_r * s8)).astype(BF16)
        gq = gq_ref[...]
        xq, rq = _rms(cq_ref[...])
        dwq_ref[...] += _dot_tn((xq * gq).astype(BF16), dqp)
        dn = _dot_nt(dqp, wq_ref[...])
        dgq_ref[...] += jnp.sum(dn * xq, axis=0, keepdims=True)
        dl_ref[:, :256] = _rms_bwd(xq, rq, dn * gq).astype(BF16)

        dk_r = dk_ref[...]
        dkr = dk_r[:, :128]
        for hd in range(1, 8):
            dkr = dkr + dk_r[:, hd * 128:(hd + 1) * 128]
        lane1 = _lane(dkr.shape)
        dkr = jnp.where((lane1 >= 64) & (lane1 < 96), dkr, 0.0)
        dl_ref[:, 384:] = (dkr * c1 + _swap_rope(dkr * s1)).astype(BF16)
        dkv = jnp.concatenate([dk_r, dv_ref[...]], axis=1).astype(BF16)
        gkv = gkv_ref[...]
        xk, rk = _rms(ckv_ref[...])
        dwkv_ref[...] += _dot_tn((xk * gkv).astype(BF16), dkv)
        dn2 = _dot_nt(dkv, wkv_ref[...])
        dgkv_ref[...] += jnp.sum(dn2 * xk, axis=0, keepdims=True)
        dl_ref[:, 256:384] = _rms_bwd(xk, rk, dn2 * gkv).astype(BF16)

    row = lambda b, j: b * nj + j
    const = lambda shape: pl.BlockSpec(shape, lambda b, j: (0, 0))
    return _pallas(
        body, grid=(cfg.B, nj), name="lat_bwd",
        in_specs=[pl.BlockSpec((BLK, 1024), lambda b, j: (row(b, j), 0)), pl.BlockSpec((BLK, 1024), lambda b, j: (row(b, j), 0)),
                  pl.BlockSpec((BLK, 512), lambda b, j: (row(b, j), 0)),
                  pl.BlockSpec((BLK, 256), lambda b, j: (row(b, j), 4)), pl.BlockSpec((BLK, 128), lambda b, j: (row(b, j), 10)),
                  const((1, 256)), const((1, 128)), const((256, 1024)), const((128, 1536)),
                  pl.BlockSpec((BLK, 128), lambda b, j: (j, 0)), pl.BlockSpec((BLK, 128), lambda b, j: (j, 0))],
        out_specs=[pl.BlockSpec((BLK, 512), lambda b, j: (row(b, j), 0)), const((256, 1024)), const((128, 1536)),
                   const((1, 256)), const((1, 128))],
        out_shape=[jax.ShapeDtypeStruct((cfg.TP, 512), BF16), jax.ShapeDtypeStruct((256, 1024), F32),
                   jax.ShapeDtypeStruct((128, 1536), F32), jax.ShapeDtypeStruct((1, 256), F32), jax.ShapeDtypeStruct((1, 128), F32)],
        compiler_params=_cp("arbitrary", "arbitrary"),
    )(dq, dk, dv, pf, pf, gq, gkv, wq_p, wkv_p, c_tab, s_tab)


def _inproj_bwd(cfg, h, g, w_p, dqa, dkva, dgate, dlat, dh, comm=None):
    tm = 256

    def body(h_ref, g_ref, w_ref, dqa_ref, dkva_ref, dg_ref, dl_ref, dh_ref, o_ref, dw_ref, dgn_ref):
        @pl.when(pl.program_id(0) == 0)
        def _():
            dw_ref[...] = jnp.zeros_like(dw_ref)
            dgn_ref[...] = jnp.zeros_like(dgn_ref)

        g = g_ref[...]
        xh, r = _rms(h_ref[...])
        dproj = jnp.concatenate([dqa_ref[...], dkva_ref[...], dg_ref[...], dl_ref[...]], axis=1)
        dw_ref[...] += _dot_tn((xh * g).astype(BF16), dproj)
        du = _dot_nt(dproj, w_ref[...])
        dgn_ref[...] += jnp.sum(du * xh, axis=0, keepdims=True)
        o_ref[...] = dh_ref[...] + _rms_bwd(xh, r, du * g)

    rows = lambda w: pl.BlockSpec((tm, w), lambda i: (i, 0))
    return _call_with_comm(
        body, comm, grid=(cfg.TP // tm,), name="inproj_bwd",
        in_specs=[rows(D_MODEL), pl.BlockSpec((1, D_MODEL), lambda i: (0, 0)), pl.BlockSpec((D_MODEL, W_IN_P), lambda i: (0, 0)),
                  rows(512), rows(256), rows(1024), rows(512), rows(D_MODEL)],
        out_specs=[rows(D_MODEL), pl.BlockSpec((D_MODEL, W_IN_P), lambda i: (0, 0)), pl.BlockSpec((1, D_MODEL), lambda i: (0, 0))],
        out_shape=[jax.ShapeDtypeStruct((cfg.TP, D_MODEL), F32), jax.ShapeDtypeStruct((D_MODEL, W_IN_P), F32),
                   jax.ShapeDtypeStruct((1, D_MODEL), F32)],
        args=(h, g, w_p, dqa, dkva, dgate, dlat, dh))


def _meta_grad(cfg, dh):
    def body(d_ref, o_ref):
        @pl.when(pl.program_id(0) == 0)
        def _():
            o_ref[...] = d_ref[...]

        @pl.when(pl.program_id(0) > 0)
        def _():
            o_ref[...] += d_ref[...]

    return _pallas(
        body, grid=(cfg.B,), name="meta_grad",
        in_specs=[pl.BlockSpec((BLK, D_MODEL), lambda b: (b * cfg.NJ, 0))],
        out_specs=pl.BlockSpec((BLK, D_MODEL), lambda b: (0, 0)),
        out_shape=jax.ShapeDtypeStruct((BLK, D_MODEL), F32),
        compiler_params=_cp("arbitrary"),
    )(dh)


MATRICES = ("w_in", "w_uq", "w_ukv", "w_out")


def _local_grads(cfg, x, target, meta_of, table, small, weight_of, rider=None):
    def ride(stage, i, mats):
        hook = rider(stage, i, mats) if rider else None
        return hook if hook else (None, lambda res: None)

    depth = small["norm_in"].shape[0]
    rel, vis = _window_structure(cfg.NJ)
    bucket = _t5_bucket(jnp.asarray(rel))
    maskadd = jnp.asarray(np.where(vis, 0.0, NEG).astype(np.float32))
    c_tab, s_tab = _rope_tables(cfg)
    comm, deliver = ride("bias_build", 0, {})
    place_x = _rows_copy_comm(x.reshape(cfg.B * cfg.S, D_MODEL), cfg.TP, [(b * cfg.S, b * cfg.LP + BLK, cfg.S) for b in range(cfg.B)])
    bias, h, *travelled = _bias_build(table, bucket, maskadd, _merge_comm(place_x, comm))
    deliver(travelled)
    meta_blk = jnp.concatenate([meta_of(), jnp.zeros((BLK - N_META, D_MODEL), F32)], axis=0)
    for b in range(cfg.B):
        h = lax.dynamic_update_slice_in_dim(h, meta_blk, b * cfg.LP, axis=0)

    wp, saved = [], []
    for i in range(depth):
        w = dict(w_in=_w_in_to_p(weight_of(i, "w_in")),
                 g_in=small["norm_in"][i][None], gq=small["norm_q_lat"][i][None], gkv=small["norm_kv_lat"][i][None],
                 goa=_perm_heads64(small["norm_out_a"][i], 0)[None], gob=small["norm_out_b"][i][None], sink=small["sink_a"][i])
        wp.append(w)
        comm, deliver = ride("inproj_fwd", i, {})
        pa, pf, *travelled = _inproj_fwd(cfg, h, w["g_in"], w["w_in"], comm)
        deliver(travelled)
        w.update(w_uq=_w_uq_to_p(weight_of(i, "w_uq")), w_ukv=_w_ukv_to_p(weight_of(i, "w_ukv")), w_out=_w_out_to_p(weight_of(i, "w_out")))
        q, k, v, kt, vt = _lat_fwd(cfg, pf, w["gq"], w["gkv"], w["w_uq"], w["w_ukv"], c_tab, s_tab)
        ya, lse_a = _win_fwd(cfg, pa, bias, w["sink"])
        comm, deliver = ride("mla_fwd", i, {})
        yb, lse_b, *travelled = _mla_fwd(cfg, q, kt, v, comm)
        deliver(travelled)
        h_next = _out_fwd(cfg, ya, yb, pf, w["goa"], w["gob"], w["w_out"], h)
        saved.append(dict(h=h, pa=pa, pf=pf, q=q, k=k, kt=kt, vt=vt, ya=ya, lse_a=lse_a, yb=yb, lse_b=lse_b))
        h = h_next

    dh, loss_tile, d_norm_final = _loss_bwd(cfg, h, target.reshape(cfg.B * cfg.S, D_MODEL), small["norm_final"][None])

    grads = {k_: [] for k_ in ("norm_in", "sink_a", "norm_q_lat", "norm_kv_lat", "norm_out_a", "norm_out_b")}
    mats, s_accs = {}, []
    for i in reversed(range(depth)):
        w, sv = wp[i], saved[i]
        dya, dyb, dgate, dwo, dgoa, dgob = _out_bwd(cfg, dh, sv["ya"], sv["yb"], sv["pf"], w["goa"], w["gob"], w["w_out"])
        dqa, dkp, dvp, dkm, dvm, s_acc, dsink = _win_bwd(cfg, sv["pa"], bias, w["sink"], dya, sv["ya"], sv["lse_a"])
        dkva = _win_dkv_combine(cfg, dkp, dvp, dkm, dvm)
        comm, deliver = ride("mla_bwd", i, mats)
        dq, dk, dv, *travelled = _mla_bwd(cfg, sv["q"], sv["k"], sv["kt"], sv["vt"], dyb, sv["yb"], sv["lse_b"], comm)
        deliver(travelled)
        dlat, dwq, dwkv, dgq, dgkv = _lat_bwd(cfg, dq, dk, dv, sv["pf"], w["gq"], w["gkv"], w["w_uq"], w["w_ukv"], c_tab, s_tab)
        mats[i] = dict(w_uq=_w_uq_from_p(dwq), w_ukv=_w_ukv_from_p(dwkv), w_out=_w_out_from_p(dwo))
        comm, deliver = ride("inproj_bwd", i, mats)
        dh, dwin, dgin, *travelled = _inproj_bwd(cfg, sv["h"], w["g_in"], w["w_in"], dqa, dkva, dgate, dlat, dh, comm)
        deliver(travelled)
        s_accs.append(s_acc)
        mats[i]["w_in"] = _w_in_from_p(dwin)
        grads["norm_in"].append(dgin[0])
        grads["sink_a"].append(dsink[0, :A_HEADS])
        grads["norm_q_lat"].append(dgq[0])
        grads["norm_kv_lat"].append(dgkv[0])
        grads["norm_out_a"].append(_unperm_heads64(dgoa[0], 0))
        grads["norm_out_b"].append(dgob[0])

    out = {k_: jnp.stack(v_[::-1]) for k_, v_ in grads.items()}
    mats["meta_tokens"] = _meta_grad(cfg, dh)[:N_META]
    comm, deliver = ride("bias_grad", 0, mats)
    dtable, *travelled = _bias_grad(s_accs, bucket, comm)
    deliver(travelled)
    out["rel_bias_table"] = dtable[:, 0].reshape(N_BUCKETS, A_HEADS)
    out["norm_final"] = d_norm_final[0]
    take_x = _rows_copy_comm(dh, cfg.B * cfg.S, [(b * cfg.LP + BLK, b * cfg.S, cfg.S) for b in range(cfg.B)])
    return loss_tile[0, 0], take_x, out, mats


MESH = pl.DeviceIdType.MESH
ANY = pl.BlockSpec(memory_space=pl.ANY)


def _place():
    x, y, c = lax.axis_index("x"), lax.axis_index("y"), lax.axis_index("c")
    others = [(1 - x, y), (x, 1 - y), (1 - x, 1 - y)]
    return x, y, c, others


Comm = collections.namedtuple("Comm", "inputs out_shapes scratch start wait")


def _rows_copy_comm(src, out_rows, moves):
    def copies(ins, outs, sems):
        return [pltpu.make_async_copy(ins[0].at[pl.ds(s, n)], outs[0].at[pl.ds(d, n)], sems[0].at[i]) for i, (s, d, n) in enumerate(moves)]

    def start(ins, outs, sems):
        for cp in copies(ins, outs, sems):
            cp.start()

    def wait(ins, outs, sems):
        for cp in copies(ins, outs, sems):
            cp.wait()

    return Comm([src], [jax.ShapeDtypeStruct((out_rows, src.shape[1]), src.dtype)], [pltpu.SemaphoreType.DMA((len(moves),))], start, wait)


def _merge_comm(a, b):
    if b is None:
        return a
    ni, no, ns = len(a.inputs), len(a.out_shapes), len(a.scratch)

    def start(ins, outs, sems):
        a.start(ins[:ni], outs[:no], sems[:ns])
        b.start(ins[ni:], outs[no:], sems[ns:])

    def wait(ins, outs, sems):
        a.wait(ins[:ni], outs[:no], sems[:ns])
        b.wait(ins[ni:], outs[no:], sems[ns:])

    return Comm(a.inputs + b.inputs, a.out_shapes + b.out_shapes, a.scratch + b.scratch, start, wait)


def _gather_comm(shards):
    n = len(shards)

    def copies(ins, outs, sems, arriving):
        send_sems, recv_sems, local_sems = sems
        x, y, c, others = _place()
        k_me = 2 * x + y
        local = [pltpu.make_async_copy(ins[a], outs[a].at[k_me], local_sems.at[a]) for a in range(n)]
        remote = [pltpu.make_async_remote_copy(src_ref=ins[a], dst_ref=outs[a].at[2 * ox + oy if arriving else k_me],
                                               send_sem=send_sems.at[3 * a + j], recv_sem=recv_sems.at[3 * a + j],
                                               device_id=(ox, oy, c), device_id_type=MESH)
                  for a in range(n) for j, (ox, oy) in enumerate(others)]
        return local, remote

    def start(ins, outs, sems):
        local, sends = copies(ins, outs, sems, arriving=False)
        for cp in local + sends:
            cp.start()

    def wait(ins, outs, sems):
        local, recvs = copies(ins, outs, sems, arriving=True)
        for cp in recvs:
            cp.wait_recv()
        for cp in recvs:
            cp.wait_send()
        for cp in local:
            cp.wait()

    return Comm(list(shards), [jax.ShapeDtypeStruct((4, *s.shape), s.dtype) for s in shards],
                [pltpu.SemaphoreType.DMA((3 * n,)), pltpu.SemaphoreType.DMA((3 * n,)), pltpu.SemaphoreType.DMA((n,))], start, wait)


def _gather_halves_comm(shards):
    n = len(shards)

    def copies(ins, outs, sems, kind):
        send_sems, recv_sems, fwd_send_sems, fwd_recv_sems, local_sems = sems
        x, y, c, others = _place()
        k_me = 2 * x + y

        def half(ref, which):
            rows = ref.shape[0] // 2
            return ref.at[pl.ds(pl.multiple_of(which * rows, 8), rows)]

        if kind == "local":
            return [pltpu.make_async_copy(ins[a], outs[a].at[k_me], local_sems.at[a]) for a in range(n)]
        made = []
        for a in range(n):
            for j, (ox, oy) in enumerate(others):
                slot = outs[a].at[k_me if kind == "sent" else 2 * ox + oy]
                if kind in ("sent", "arrived"):
                    made.append(pltpu.make_async_remote_copy(
                        src_ref=half(ins[a], c), dst_ref=half(slot, c), send_sem=send_sems.at[3 * a + j],
                        recv_sem=recv_sems.at[3 * a + j], device_id=(ox, oy, c), device_id_type=MESH))
                else:
                    which = c if kind == "forward" else 1 - c
                    made.append(pltpu.make_async_remote_copy(
                        src_ref=half(slot, which), dst_ref=half(slot, which), send_sem=fwd_send_sems.at[3 * a + j],
                        recv_sem=fwd_recv_sems.at[3 * a + j], device_id=(x, y, 1 - c), device_id_type=MESH))
        return made

    def start(ins, outs, sems):
        for cp in copies(ins, outs, sems, "local") + copies(ins, outs, sems, "sent"):
            cp.start()

    def wait(ins, outs, sems):
        arrived, forward = copies(ins, outs, sems, "arrived"), copies(ins, outs, sems, "forward")
        for came, on in zip(arrived, forward):
            came.wait_recv()
            on.start()
        for cp in copies(ins, outs, sems, "forwarded"):
            cp.wait_recv()
        for cp in arrived + forward:
            cp.wait_send()
        for cp in copies(ins, outs, sems, "local"):
            cp.wait()

    return Comm(list(shards), [jax.ShapeDtypeStruct((4, *s.shape), s.dtype) for s in shards],
                [pltpu.SemaphoreType.DMA((3 * n,))] * 4 + [pltpu.SemaphoreType.DMA((n,))], start, wait)


def _scatter_comm(parts):
    n = len(parts)

    def copies(ins, outs, sems):
        send_sems, recv_sems = sems
        x, y, c, others = _place()
        return [pltpu.make_async_remote_copy(src_ref=ins[a].at[2 * ox + oy], dst_ref=outs[a].at[j], send_sem=send_sems.at[3 * a + j],
                                             recv_sem=recv_sems.at[3 * a + j], device_id=(ox, oy, c), device_id_type=MESH)
                for a in range(n) for j, (ox, oy) in enumerate(others)]

    def start(ins, outs, sems):
        for cp in copies(ins, outs, sems):
            cp.start()

    def wait(ins, outs, sems):
        cps = copies(ins, outs, sems)
        for cp in cps:
            cp.wait_recv()
        for cp in cps:
            cp.wait_send()

    return Comm(list(parts), [jax.ShapeDtypeStruct((3, *p.shape[1:]), p.dtype) for p in parts],
                [pltpu.SemaphoreType.DMA((3 * n,)), pltpu.SemaphoreType.DMA((3 * n,))], start, wait)


def _call_with_comm(body, comm, *, grid, name, in_specs, out_specs, out_shape, args, scratch_shapes=()):
    if comm is None:
        return _pallas(body, grid=grid, name=name, in_specs=in_specs, out_specs=out_specs, out_shape=out_shape,
                              scratch_shapes=list(scratch_shapes), compiler_params=_cp(*["arbitrary"] * len(grid)))(*args)
    n_in, n_out, ci, co, ns = len(in_specs), len(out_specs), len(comm.inputs), len(comm.out_shapes), len(scratch_shapes)

    def wrapped(*refs):
        ins, cins = refs[:n_in], refs[n_in:n_in + ci]
        outs, couts = refs[n_in + ci:n_in + ci + n_out], refs[n_in + ci + n_out:n_in + ci + n_out + co]
        scratch, sems = refs[n_in + ci + n_out + co:n_in + ci + n_out + co + ns], refs[n_in + ci + n_out + co + ns:]
        ids = [pl.program_id(a) for a in range(len(grid))]
        first = functools.reduce(jnp.logical_and, [i == 0 for i in ids])
        last = functools.reduce(jnp.logical_and, [i == g - 1 for i, g in zip(ids, grid)])

        @pl.when(first)
        def _():
            comm.start(cins, couts, sems)

        body(*ins, *outs, *scratch)

        @pl.when(last)
        def _():
            comm.wait(cins, couts, sems)

    return _pallas(
        wrapped, grid=grid, name=name + "_comm", in_specs=[*in_specs, *[ANY] * ci], out_specs=[*out_specs, *[ANY] * co],
        out_shape=[*out_shape, *comm.out_shapes], scratch_shapes=[*scratch_shapes, *comm.scratch],
        compiler_params=_cp(*["arbitrary"] * len(grid)))(*args, *comm.inputs)


def _swap_sibling(arrs):
    n = len(arrs)

    def body(*refs):
        ins, outs = refs[:n], refs[n:2 * n]
        send_sems, recv_sems = refs[2 * n:]
        x, y, c, _ = _place()
        copies = [pltpu.make_async_remote_copy(src_ref=ins[a], dst_ref=outs[a], send_sem=send_sems.at[a], recv_sem=recv_sems.at[a],
                                               device_id=(x, y, 1 - c), device_id_type=MESH) for a in range(n)]
        for cp in copies:
            cp.start()
        for cp in copies:
            cp.wait_recv()
        for cp in copies:
            cp.wait_send()

    return _pallas(
        body, name="swap_sibling", in_specs=[ANY] * n, out_specs=[ANY] * n,
        out_shape=[jax.ShapeDtypeStruct(a.shape, a.dtype) for a in arrs],
        scratch_shapes=[pltpu.SemaphoreType.DMA((n,)), pltpu.SemaphoreType.DMA((n,))],
    )(*arrs)


def _allreduce_small(v):
    def body(v_ref, o_ref, buf, send_sems, recv_sems):
        x, y, c, _ = _place()
        me = 4 * x + 2 * y + c
        buf[me] = v_ref[...]

        def copy(r):
            tx, ty, tc = (x + (r >> 2)) % 2, (y + ((r >> 1) & 1)) % 2, (c + (r & 1)) % 2
            return tx, ty, tc

        sends = []
        for r in range(1, 8):
            tx, ty, tc = copy(r)
            sends.append(pltpu.make_async_remote_copy(src_ref=v_ref, dst_ref=buf.at[me], send_sem=send_sems.at[r - 1],
                                                      recv_sem=recv_sems.at[r - 1], device_id=(tx, ty, tc), device_id_type=MESH))
        for cp in sends:
            cp.start()
        for r in range(1, 8):
            tx, ty, tc = copy(r)
            pltpu.make_async_remote_copy(src_ref=v_ref, dst_ref=buf.at[4 * tx + 2 * ty + tc], send_sem=send_sems.at[r - 1],
                                         recv_sem=recv_sems.at[r - 1], device_id=(tx, ty, tc), device_id_type=MESH).wait_recv()
        for cp in sends:
            cp.wait_send()
        acc = buf[0]
        for d in range(1, 8):
            acc = acc + buf[d]
        o_ref[...] = acc

    return pl.pallas_call(
        body, name="allreduce_small", in_specs=[pl.BlockSpec(memory_space=pltpu.VMEM)], out_specs=pl.BlockSpec(memory_space=pltpu.VMEM),
        out_shape=jax.ShapeDtypeStruct(v.shape, F32),
        scratch_shapes=[pltpu.VMEM((8, *v.shape), F32), pltpu.SemaphoreType.DMA((7,)), pltpu.SemaphoreType.DMA((7,))],
    )(v)


def _rows_view(a):
    return a.reshape(-1, a.shape[-1])


def _elementwise(name, fn, ins, n_out, comm=None):
    rows, cols = ins[0].shape
    tm = min(rows, 256)
    spec = pl.BlockSpec((tm, cols), lambda i: (i, 0))

    def body(*refs):
        outs = fn(*[r[...] for r in refs[:len(ins)]])
        for o_ref, o in zip(refs[len(ins):], outs):
            o_ref[...] = o

    return _call_with_comm(
        body, comm, grid=(rows // tm,), name=name, in_specs=[spec] * len(ins), out_specs=[spec] * n_out,
        out_shape=[jax.ShapeDtypeStruct((rows, cols), F32)] * n_out, args=tuple(ins))


def _sum_parts(name, own, recv):
    def fn(o, r0, r1, r2):
        return (o + r0.astype(F32) + r1.astype(F32) + r2.astype(F32),)

    return _elementwise("sum_parts_" + name, fn, [own, recv[0], recv[1], recv[2]], 1)[0]


def _adamw(name, w, m, v, g_parts, comm=None):
    def fn(w_, m_, v_, *gs):
        g = gs[0]
        for extra in gs[1:]:
            g = g + extra
        m_new = ADAM_B1 * m_ + (1.0 - ADAM_B1) * g
        v_new = ADAM_B2 * v_ + (1.0 - ADAM_B2) * (g * g)
        m_hat = m_new / (1.0 - ADAM_B1 ** ADAM_STEP)
        v_hat = v_new / (1.0 - ADAM_B2 ** ADAM_STEP)
        delta = -ADAM_LR * (m_hat / (jnp.sqrt(v_hat) + ADAM_EPS) + ADAM_WD * w_)
        return g, delta, m_new, v_new

    return _elementwise("adamw_" + name, fn, [w, m, v, *g_parts], 4, comm)


MAT_AXIS = {"w_in": 1, "w_uq": 1, "w_ukv": 1, "w_out": 0}
SMALL = ("rel_bias_table", "norm_in", "sink_a", "norm_q_lat", "norm_kv_lat", "norm_out_a", "norm_out_b", "norm_final")
WEIGHTS = ("meta_tokens", "rel_bias_table", "norm_in", "w_in", "sink_a", "norm_q_lat", "w_uq", "norm_kv_lat", "w_ukv",
           "norm_out_a", "norm_out_b", "w_out", "norm_final")
SMALL_ROWS, SMALL_COLS = 8, 1024


def _pack_small(d, loss=None):
    flat = [d[n].reshape(-1) for n in SMALL]
    if loss is not None:
        flat.append(loss.reshape(1))
    used = sum(f.shape[0] for f in flat)
    flat.append(jnp.zeros((SMALL_ROWS * SMALL_COLS - used,), F32))
    return jnp.concatenate(flat).reshape(SMALL_ROWS, SMALL_COLS)


def _unpack_small(p, like):
    flat, out, off = p.reshape(-1), {}, 0
    for n in SMALL:
        size = int(np.prod(like[n].shape))
        out[n] = flat[off:off + size].reshape(like[n].shape)
        off += size
    return out, flat[off]


def _split4(a, axis):
    size = a.shape[axis] // 4
    return jnp.stack([lax.slice_in_dim(a, k * size, (k + 1) * size, axis=axis) for k in range(4)])


def _train_step(cfg, x, target, w, m, v):
    depth = w["w_in"].shape[0]
    rest = tuple(n for n in MATRICES if n != "w_in")
    weights, splits, received = {}, {}, {}

    def gather(i, names, also=(), build=_gather_comm):
        def deliver(res):
            for n, g in zip(names, res):
                weights[i, n] = jnp.concatenate([g[k] for k in range(4)], axis=MAT_AXIS[n])

        return build([w[n][i].astype(BF16) for n in names] + list(also)), deliver

    def scatter(i, names, mats, also=()):
        for n in names:
            splits[i, n] = _split4(mats[i][n], MAT_AXIS[n])

        def deliver(res):
            for n, r in zip(names, res):
                received[i, n] = r

        return _scatter_comm([splits[i, n].astype(BF16) for n in names] + list(also)), deliver

    def rider(stage, i, mats):
        if stage == "bias_build":
            comm, deliver = gather(0, ("w_in",), also=[w["meta_tokens"]], build=_gather_halves_comm)

            def deliver_first(res):
                deliver(res)
                weights["meta"] = jnp.concatenate([res[1][k] for k in range(4)], axis=1)

            return comm, deliver_first
        if stage == "inproj_fwd" and i == 0:
            return gather(0, rest)
        if stage == "mla_fwd" and i + 1 < depth:
            return gather(i + 1, MATRICES)
        if stage == "mla_bwd" and i + 1 < depth:
            return scatter(i + 1, MATRICES, mats)
        if stage == "inproj_bwd" and i == 0:
            return scatter(0, rest, mats)
        if stage == "bias_grad":
            splits["meta"] = _split4(mats["meta_tokens"], 1)
            comm, deliver = scatter(0, ("w_in",), mats, also=[splits["meta"].astype(BF16)])

            def deliver_last(res):
                deliver(res)
                received["meta"] = res[1]

            return comm, deliver_last
        return None

    loss_local, take_x, g, mats = _local_grads(cfg, x, target, lambda: weights["meta"], w["rel_bias_table"], {n: w[n] for n in SMALL},
                                               lambda i, n: weights[i, n], rider)

    small_sum = _allreduce_small(_pack_small(g, loss_local))
    g_small, loss = _unpack_small(small_sum, {n: w[n] for n in SMALL})

    k_me = 2 * lax.axis_index("x") + lax.axis_index("y")

    def core_sum(name, split, recv):
        own = lax.dynamic_index_in_dim(split, k_me, 0, keepdims=False)
        return _sum_parts(name, _rows_view(own), recv.reshape(3, -1, recv.shape[-1]))

    partial = [core_sum("meta_tokens", splits["meta"], received["meta"])]
    for n in MATRICES:
        partial.append(jnp.concatenate([core_sum(f"{n}_{i}", splits[i, n], received[i, n]) for i in range(depth)], axis=0))
    sibling = _swap_sibling(partial)

    outs = {}
    for n, p_me, p_sib in zip(("meta_tokens", *MATRICES), partial, sibling):
        res = _adamw(n, _rows_view(w[n]), _rows_view(m[n]), _rows_view(v[n]), [p_me, p_sib], take_x if n == "w_in" else None)
        outs[n] = [r.reshape(w[n].shape) for r in res[:4]]
        if n == "w_in":
            grad_x = res[4].reshape(x.shape)
    res = _adamw("small", _pack_small(w), _pack_small(m), _pack_small(v), [_pack_small(g_small)])
    unpacked = [_unpack_small(r, {n: w[n] for n in SMALL})[0] for r in res]
    for n in SMALL:
        outs[n] = [u[n] for u in unpacked]

    result = [loss, grad_x]
    for field in range(4):
        result.extend(outs[n][field] for n in WEIGHTS)
    return tuple(result)


def kernel(x, meta_tokens, rel_bias_table, norm_in, w_in, sink_a, norm_q_lat, w_uq, norm_kv_lat, w_ukv, norm_out_a, norm_out_b, w_out, norm_final, loss_target, m_meta_tokens, m_rel_bias_table, m_norm_in, m_w_in, m_sink_a, m_norm_q_lat, m_w_uq, m_norm_kv_lat, m_w_ukv, m_norm_out_a, m_norm_out_b, m_w_out, m_norm_final, v_meta_tokens, v_rel_bias_table, v_norm_in, v_w_in, v_sink_a, v_norm_q_lat, v_w_uq, v_norm_kv_lat, v_w_ukv, v_norm_out_a, v_norm_out_b, v_w_out, v_norm_final):
    w = dict(zip(WEIGHTS, (meta_tokens, rel_bias_table, norm_in, w_in, sink_a, norm_q_lat, w_uq, norm_kv_lat, w_ukv, norm_out_a, norm_out_b, w_out, norm_final)))
    m = dict(zip(WEIGHTS, (m_meta_tokens, m_rel_bias_table, m_norm_in, m_w_in, m_sink_a, m_norm_q_lat, m_w_uq, m_norm_kv_lat, m_w_ukv, m_norm_out_a, m_norm_out_b, m_w_out, m_norm_final)))
    v = dict(zip(WEIGHTS, (v_meta_tokens, v_rel_bias_table, v_norm_in, v_w_in, v_sink_a, v_norm_q_lat, v_w_uq, v_norm_kv_lat, v_w_ukv, v_norm_out_a, v_norm_out_b, v_w_out, v_norm_final)))
    cfg = make_cfg(x.shape[0], x.shape[1])
    return _train_step(cfg, x, loss_target, w, m, v)
```

```python
import collections
import functools
import math

import jax
import jax.numpy as jnp
import numpy as np
from jax import lax
from jax.experimental import pallas as pl
from jax.experimental.pallas import tpu as pltpu

F32 = jnp.float32
BF16 = jnp.bfloat16

BLK = 128
N_META = 16
D_MODEL = 1024
A_HEADS, A_KV, A_DH = 8, 2, 64
B_HEADS, B_NOPE, B_ROPE, B_DV = 8, 64, 32, 64
Q_RANK, KV_RANK = 256, 128
N_BUCKETS, MAX_DIST = 32, 128
ROPE_THETA = 10000.0
EPS = 1e-6
IN_WIDTH = 2208
W_IN_P = 2304
NEG = -1e30
MASK_LANE = 96
LOG2E = math.log2(math.e)
Q_SCALE = (B_NOPE + B_ROPE) ** -0.5 * LOG2E
QA_SCALE = A_DH ** -0.5 * LOG2E
LN2 = math.log(2.0)
VMEM_LIMIT = 48 * 1024 * 1024

ADAM_LR, ADAM_B1, ADAM_B2, ADAM_EPS, ADAM_WD, ADAM_STEP = 0.001, 0.9, 0.999, 1e-08, 0.01, 10

Cfg = collections.namedtuple("Cfg", "B S NB NJ LP TP")


def make_cfg(batch, seq):
    nb = seq // BLK
    nj = nb + 1
    return Cfg(batch, seq, nb, nj, nj * BLK, batch * nj * BLK)


def _cp(*sem):
    return pltpu.CompilerParams(dimension_semantics=sem, vmem_limit_bytes=VMEM_LIMIT)


def _pallas(body, *, out_shape, **kw):
    pinned = jax.tree.map(lambda s: pltpu.HBM(s.shape, s.dtype), out_shape)
    call = pl.pallas_call(body, out_shape=pinned, **kw)
    return lambda *args: call(*[pltpu.with_memory_space_constraint(a, pltpu.HBM) for a in args])


def _dot(a, b):
    return jnp.dot(a, b, preferred_element_type=F32)


def _dot_nt(a, b):
    return lax.dot_general(a, b, (((1,), (1,)), ((), ())), preferred_element_type=F32)


def _dot_tn(a, b):
    return lax.dot_general(a, b, (((0,), (0,)), ((), ())), preferred_element_type=F32)


def _rms(x, width=None):
    n = x.shape[-1] if width is None else width
    r = lax.rsqrt(jnp.sum(x * x, axis=-1, keepdims=True) * (1.0 / n) + EPS)
    return x * r, r


def _rms_bwd(xhat, r, t):
    n = xhat.shape[-1]
    return r * (t - xhat * (jnp.sum(t * xhat, axis=-1, keepdims=True) * (1.0 / n)))


def _sigmoid(x):
    return 1.0 / (1.0 + jnp.exp(-x))


def _lane(shape):
    return lax.broadcasted_iota(jnp.int32, shape, len(shape) - 1)


def _swap_rope(x):
    n = x.shape[-1]
    lane = _lane(x.shape) % BLK
    up = pltpu.roll(x, n - 16, axis=x.ndim - 1)
    dn = pltpu.roll(x, 16, axis=x.ndim - 1)
    return jnp.where((lane >= 64) & (lane < 80), up, jnp.where((lane >= 80) & (lane < 96), dn, 0.0))


A_ORDER = (0, 4, 1, 5, 2, 6, 3, 7)


def _jtype(j, nj):
    return 0 if j == 0 else 1 if j == 1 else 3 if j == nj - 1 else 2


def _window_structure(nj):
    def pos(blk, r):
        return np.where(blk == 0, r, N_META + (blk - 1) * BLK + r)

    def valid(blk, r):
        return np.where(blk == 0, r < N_META, True)

    r = np.arange(BLK)
    rels, viss = [], []
    for j in (0, 1, 2, nj - 1):
        qpos = pos(j, r)[:, None]
        rel_t, vis_t = [], []
        for s, kb in enumerate((0, j - 1, j, j + 1)):
            slot_ok = (s == 0) or (1 <= kb <= nj - 1)
            kbc = min(max(kb, 0), nj - 1)
            kpos = pos(kbc, r)[None, :]
            rel = kpos - qpos
            v = valid(kbc, r)[None, :] & np.ones((BLK, 1), bool)
            if s > 0:
                v = v & (np.abs(rel) <= BLK)
            rel_t.append(rel)
            vis_t.append(v & slot_ok)
        rels.append(np.concatenate(rel_t, axis=1))
        viss.append(np.concatenate(vis_t, axis=1))
    return np.stack(rels).astype(np.int32), np.stack(viss)


def _t5_bucket(rel):
    nb = N_BUCKETS // 2
    max_exact = nb // 2
    ret = jnp.where(rel > 0, nb, 0)
    n = jnp.abs(rel)
    nf = jnp.maximum(n, 1).astype(jnp.float32)
    large = max_exact + (jnp.log(nf / max_exact) / math.log(MAX_DIST / max_exact) * (nb - max_exact)).astype(jnp.int32)
    large = jnp.minimum(large, nb - 1)
    return ret + jnp.where(n < max_exact, n, large)


def _perm_heads64(a, axis):
    parts = [lax.slice_in_dim(a, h * 64, (h + 1) * 64, axis=axis) for h in A_ORDER]
    return jnp.concatenate(parts, axis=axis)


def _unperm_heads64(a, axis):
    inv = [A_ORDER.index(h) for h in range(8)]
    parts = [lax.slice_in_dim(a, p * 64, (p + 1) * 64, axis=axis) for p in inv]
    return jnp.concatenate(parts, axis=axis)


def _w_in_to_p(w):
    sl = lambda a, b: lax.slice_in_dim(w, a, b, axis=1)
    z = lambda n: jnp.zeros((w.shape[0], n), w.dtype)
    return jnp.concatenate([_perm_heads64(sl(0, 512), 1), sl(512, 768), _perm_heads64(sl(768, 1280), 1), sl(1696, 2208),
                            sl(1280, 1536), sl(1536, 1664), z(64), sl(1664, 1696), z(32)], axis=1)


def _w_in_from_p(g):
    sl = lambda a, b: lax.slice_in_dim(g, a, b, axis=1)
    return jnp.concatenate([_unperm_heads64(sl(0, 512), 1), sl(512, 768), _unperm_heads64(sl(768, 1280), 1),
                            sl(1792, 2048), sl(2048, 2176), sl(2240, 2272), sl(1280, 1792)], axis=1)


def _w_uq_to_p(w):
    z = jnp.zeros((w.shape[0], 32), w.dtype)
    return jnp.concatenate([p for h in range(8) for p in (lax.slice_in_dim(w, h * 96, (h + 1) * 96, axis=1), z)], axis=1)


def _w_uq_from_p(g):
    return jnp.concatenate([lax.slice_in_dim(g, h * 128, h * 128 + 96, axis=1) for h in range(8)], axis=1)


def _w_ukv_to_p(w):
    z = jnp.zeros((w.shape[0], 64), w.dtype)
    ks = [p for h in range(8) for p in (lax.slice_in_dim(w, h * 128, h * 128 + 64, axis=1), z)]
    vs = [lax.slice_in_dim(w, h * 128 + 64, (h + 1) * 128, axis=1) for h in range(8)]
    return jnp.concatenate(ks + vs, axis=1)


def _w_ukv_from_p(g):
    parts = []
    for h in range(8):
        parts.append(lax.slice_in_dim(g, h * 128, h * 128 + 64, axis=1))
        parts.append(lax.slice_in_dim(g, 1024 + h * 64, 1024 + (h + 1) * 64, axis=1))
    return jnp.concatenate(parts, axis=1)


def _w_out_to_p(w):
    return jnp.concatenate([_perm_heads64(lax.slice_in_dim(w, 0, 512, axis=0), 0), lax.slice_in_dim(w, 512, 1024, axis=0)], axis=0)


def _w_out_from_p(g):
    return jnp.concatenate([_unperm_heads64(lax.slice_in_dim(g, 0, 512, axis=0), 0), lax.slice_in_dim(g, 512, 1024, axis=0)], axis=0)


def _rope_tables(cfg):
    half = B_ROPE // 2
    length = N_META + cfg.S
    freqs = ROPE_THETA ** (-jnp.arange(half, dtype=jnp.float32) / half)
    ang = jnp.arange(length, dtype=jnp.float32)[:, None] * freqs[None, :]
    cos, sin = jnp.cos(ang), jnp.sin(ang)

    def rows(t):
        return jnp.concatenate([t[:N_META], jnp.zeros((BLK - N_META, t.shape[1]), t.dtype), t[N_META:]], axis=0)

    ones = jnp.ones((length, 64), F32)
    zer = jnp.zeros((length, 32), F32)
    c_tab = rows(jnp.concatenate([ones, cos, cos, zer], axis=1))
    s_tab = rows(jnp.concatenate([zer, zer, -sin, sin, zer], axis=1))
    return c_tab, s_tab


def _inproj_fwd(cfg, h, g, w_p, comm=None):
    tm = 256

    def body(h_ref, g_ref, w_ref, pa_ref, pf_ref):
        xh, _ = _rms(h_ref[...])
        u = (xh * g_ref[...]).astype(BF16)
        acc = _dot(u, w_ref[...])
        pa_ref[:, :512] = (acc[:, :512] * QA_SCALE).astype(BF16)
        pa_ref[:, 512:] = acc[:, 512:768].astype(BF16)
        pf_ref[...] = acc[:, 768:]

    return _call_with_comm(
        body, comm, grid=(cfg.TP // tm,), name="inproj_fwd",
        in_specs=[pl.BlockSpec((tm, D_MODEL), lambda i: (i, 0)), pl.BlockSpec((1, D_MODEL), lambda i: (0, 0)),
                  pl.BlockSpec((D_MODEL, W_IN_P), lambda i: (0, 0))],
        out_specs=[pl.BlockSpec((tm, 768), lambda i: (i, 0)), pl.BlockSpec((tm, 1536), lambda i: (i, 0))],
        out_shape=[jax.ShapeDtypeStruct((cfg.TP, 768), BF16), jax.ShapeDtypeStruct((cfg.TP, 1536), F32)],
        args=(h, g, w_p))


def _lat_fwd(cfg, pf, gq, gkv, wq_p, wkv_p, c_tab, s_tab):
    nj = cfg.NJ

    def body(cq_ref, ckv_ref, kr_ref, gq_ref, gkv_ref, wq_ref, wkv_ref, c_ref, s_ref, q_ref, k_ref, v_ref, kt_ref, vt_ref):
        c1, s1 = c_ref[...], s_ref[...]
        c8, s8 = jnp.tile(c1, (1, 8)), jnp.tile(s1, (1, 8))
        mask_lane = _lane((BLK, 1024)) % BLK == MASK_LANE
        zero_row = (pl.program_id(1) == 0) & (lax.broadcasted_iota(jnp.int32, (BLK, 1024), 0) >= N_META)
        xq, _ = _rms(cq_ref[...])
        qp = _dot((xq * gq_ref[...]).astype(BF16), wq_ref[...])
        q_ref[...] = jnp.where(mask_lane, 1.0, (qp * c8 + _swap_rope(qp) * s8) * Q_SCALE).astype(BF16)
        xk, _ = _rms(ckv_ref[...])
        kvp = _dot((xk * gkv_ref[...]).astype(BF16), wkv_ref[...])
        kr = kr_ref[...]
        krr = kr * c1 + _swap_rope(kr) * s1
        k = jnp.where(mask_lane & zero_row, NEG, kvp[:, :1024] + jnp.tile(krr, (1, 8)))
        k_ref[...] = k.astype(BF16)
        v_ref[...] = kvp[:, 1024:].astype(BF16)
        kt_ref[...] = k.T.astype(BF16)
        vt_ref[...] = kvp[:, 1024:].T.astype(BF16)

    row = lambda b, j: b * nj + j
    return _pallas(
        body, grid=(cfg.B, nj), name="lat_fwd",
        in_specs=[pl.BlockSpec((BLK, 256), lambda b, j: (row(b, j), 4)), pl.BlockSpec((BLK, 128), lambda b, j: (row(b, j), 10)),
                  pl.BlockSpec((BLK, 128), lambda b, j: (row(b, j), 11)),
                  pl.BlockSpec((1, 256), lambda b, j: (0, 0)), pl.BlockSpec((1, 128), lambda b, j: (0, 0)),
                  pl.BlockSpec((256, 1024), lambda b, j: (0, 0)), pl.BlockSpec((128, 1536), lambda b, j: (0, 0)),
                  pl.BlockSpec((BLK, 128), lambda b, j: (j, 0)), pl.BlockSpec((BLK, 128), lambda b, j: (j, 0))],
        out_specs=[pl.BlockSpec((BLK, 1024), lambda b, j: (row(b, j), 0)), pl.BlockSpec((BLK, 1024), lambda b, j: (row(b, j), 0)),
                   pl.BlockSpec((BLK, 512), lambda b, j: (row(b, j), 0)),
                   pl.BlockSpec((1024, BLK), lambda b, j: (b, j)), pl.BlockSpec((512, BLK), lambda b, j: (b, j))],
        out_shape=[jax.ShapeDtypeStruct((cfg.TP, 1024), BF16), jax.ShapeDtypeStruct((cfg.TP, 1024), BF16),
                   jax.ShapeDtypeStruct((cfg.TP, 512), BF16),
                   jax.ShapeDtypeStruct((cfg.B * 1024, cfg.LP), BF16), jax.ShapeDtypeStruct((cfg.B * 512, cfg.LP), BF16)],
        compiler_params=_cp("parallel", "parallel"),
    )(pf, pf, pf, gq, gkv, wq_p, wkv_p, c_tab, s_tab)


def _gate_halves(ya, yb, ga, gb, goa, gob):
    xa, ra = _rms(ya)
    xb, rb = _rms(yb)
    sga, sgb = _sigmoid(ga), _sigmoid(gb)
    return xa, ra, xb, rb, sga, sgb, xa * goa * (ga * sga), xb * gob * (gb * sgb)


def _out_fwd(cfg, ya, yb, pf, goa, gob, wo_p, h):
    tm = 256

    def body(ya_ref, yb_ref, ga_ref, gb_ref, goa_ref, gob_ref, w_ref, h_ref, o_ref):
        *_, y_a, y_b = _gate_halves(ya_ref[...], yb_ref[...], ga_ref[...], gb_ref[...], goa_ref[...], gob_ref[...])
        y = jnp.concatenate([y_a, y_b], axis=1).astype(BF16)
        o_ref[...] = h_ref[...] + _dot(y, w_ref[...])

    return _pallas(
        body, grid=(cfg.TP // tm,), name="out_fwd",
        in_specs=[pl.BlockSpec((tm, 512), lambda i: (i, 0)), pl.BlockSpec((tm, 512), lambda i: (i, 0)),
                  pl.BlockSpec((tm, 512), lambda i: (i, 0)), pl.BlockSpec((tm, 512), lambda i: (i, 1)),
                  pl.BlockSpec((1, 512), lambda i: (0, 0)), pl.BlockSpec((1, 512), lambda i: (0, 0)),
                  pl.BlockSpec((D_MODEL, D_MODEL), lambda i: (0, 0)), pl.BlockSpec((tm, D_MODEL), lambda i: (i, 0))],
        out_specs=pl.BlockSpec((tm, D_MODEL), lambda i: (i, 0)),
        out_shape=jax.ShapeDtypeStruct((cfg.TP, D_MODEL), F32),
        compiler_params=_cp("parallel"),
    )(ya, yb, pf, pf, goa, gob, wo_p, h)


def _bias_build(table, bucket, maskadd, comm=None):
    def body(tab_ref, bk_ref, ma_ref, o_ref):
        def rows(g, carry):
            r = pl.ds(pl.multiple_of(g * 8, 8), 8)
            bk = bk_ref[0, r, :]
            accs = [jnp.zeros(bk.shape, F32)] * A_HEADS
            for b in range(N_BUCKETS):
                hit = bk == b
                accs = [jnp.where(hit, tab_ref[b, h], accs[h]) for h in range(A_HEADS)]
            ma = ma_ref[0, r, :]
            for h in range(A_HEADS):
                o_ref[0, h, r, :] = (accs[h] + ma) * LOG2E
            return carry

        lax.fori_loop(0, BLK // 8, rows, 0)

    return _call_with_comm(
        body, comm, grid=(4,), name="bias_build",
        in_specs=[pl.BlockSpec(memory_space=pltpu.SMEM), pl.BlockSpec((1, BLK, 512), lambda t: (t, 0, 0)),
                  pl.BlockSpec((1, BLK, 512), lambda t: (t, 0, 0))],
        out_specs=[pl.BlockSpec((1, A_HEADS, BLK, 512), lambda t: (t, 0, 0, 0))],
        out_shape=[jax.ShapeDtypeStruct((4, A_HEADS, BLK, 512), F32)],
        args=(table, bucket, maskadd))


def _bias_grad(s_accs, bucket, comm=None):
    depth = len(s_accs)

    def body(*refs):
        s_refs, bk_ref, o_ref, sum_ref, part_ref = refs[:depth], refs[depth], refs[depth + 1], refs[depth + 2], refs[depth + 3]
        t = pl.program_id(0)

        @pl.when(t == 0)
        def _():
            o_ref[...] = jnp.zeros_like(o_ref)

        total = s_refs[0][0]
        for extra in s_refs[1:]:
            total = total + extra[0]
        sum_ref[...] = total

        def step(b, carry):
            accs = [jnp.zeros((8, 512), F32) for _ in range(A_HEADS)]
            for g in range(BLK // 8):
                rows = pl.ds(g * 8, 8)
                hit = bk_ref[0, rows, :] == b
                for h in range(A_HEADS):
                    accs[h] = accs[h] + jnp.where(hit, sum_ref[h, rows, :], 0.0)
            rows8 = jnp.concatenate([jnp.sum(a, axis=0, keepdims=True) for a in accs], axis=0)
            part_ref[pl.ds(pl.multiple_of(b * A_HEADS, 8), A_HEADS), :] = rows8
            return carry

        lax.fori_loop(0, N_BUCKETS, step, 0)
        o_ref[...] += jnp.broadcast_to(jnp.sum(part_ref[...], axis=1, keepdims=True), o_ref.shape)

    s_spec = pl.BlockSpec((1, A_HEADS, BLK, 512), lambda t: (t, 0, 0, 0))
    return _call_with_comm(
        body, comm, grid=(4,), name="bias_grad",
        in_specs=[s_spec] * depth + [pl.BlockSpec((1, BLK, 512), lambda t: (t, 0, 0))],
        out_specs=[pl.BlockSpec((N_BUCKETS * A_HEADS, 128), lambda t: (0, 0))],
        out_shape=[jax.ShapeDtypeStruct((N_BUCKETS * A_HEADS, 128), F32)],
        scratch_shapes=[pltpu.VMEM((A_HEADS, BLK, 512), F32), pltpu.VMEM((N_BUCKETS * A_HEADS, 512), F32)],
        args=(*s_accs, bucket))


def _win_specs(cfg):
    nj = cfg.NJ
    row = lambda b, j: b * nj + j
    jt = lambda j: jnp.where(j == 0, 0, jnp.where(j == 1, 1, jnp.where(j == nj - 1, 3, 2)))
    slot_rows = [lambda b, j: row(b, 0), lambda b, j: row(b, jnp.maximum(j - 1, 0)), lambda b, j: row(b, j),
                 lambda b, j: row(b, jnp.minimum(j + 1, nj - 1))]
    k_specs = [pl.BlockSpec((BLK, 128), functools.partial(lambda b, j, f: (f(b, j), 4), f=f)) for f in slot_rows]
    v_specs = [pl.BlockSpec((BLK, 128), functools.partial(lambda b, j, f: (f(b, j), 5), f=f)) for f in slot_rows]
    q_spec = pl.BlockSpec((BLK, 512), lambda b, j: (row(b, j), 0))
    bias_spec = pl.BlockSpec((1, A_HEADS, BLK, 512), lambda b, j: (jt(j), 0, 0, 0))
    return row, jt, q_spec, k_specs, v_specs, bias_spec


def _stack4(ref):
    return jnp.concatenate([ref[:, c * 128:(c + 1) * 128] for c in range(4)], axis=0)


def _win_keys(k_refs, v_refs):
    k4 = jnp.concatenate([r[...] for r in k_refs], axis=0)
    v4 = jnp.concatenate([r[...] for r in v_refs], axis=0)
    lane_k = _lane(k4.shape)
    return (jnp.where(lane_k < 64, k4, jnp.zeros_like(k4)), jnp.where(lane_k >= 64, k4, jnp.zeros_like(k4))), v4


def _sink_col(sink_ref, hf):
    rowi = lax.broadcasted_iota(jnp.int32, (4 * BLK, 1), 0)
    col = jnp.full((4 * BLK, 1), sink_ref[4 * hf + 3], F32)
    for c in (2, 1, 0):
        col = jnp.where(rowi < (c + 1) * BLK, sink_ref[4 * hf + c], col)
    return col * LOG2E


def _win_fwd(cfg, pa, bias, sink):
    row, jt, q_spec, k_specs, v_specs, bias_spec = _win_specs(cfg)

    def body(sink_ref, q_ref, k0, k1, k2, k3, v0, v1, v2, v3, b_ref, o_ref, lse_ref):
        kk, v4 = _win_keys((k0, k1, k2, k3), (v0, v1, v2, v3))
        qs = _stack4(q_ref)
        lane_o = _lane((4 * BLK, 128))
        outs, lses = [], []
        for hf in range(2):
            s = _dot_nt(qs, kk[hf]) + b_ref[0, 4 * hf:4 * hf + 4].reshape(4 * BLK, 512)
            sink_col = _sink_col(sink_ref, hf)
            m = jnp.maximum(jnp.max(s, axis=1, keepdims=True), sink_col)
            e = jnp.exp2(s - m)
            den = jnp.sum(e, axis=1, keepdims=True) + jnp.exp2(sink_col - m)
            outs.append(_dot(e.astype(BF16), v4) / den)
            lses.append(m + jnp.log2(den))
        o = jnp.where(lane_o < 64, outs[0], outs[1])
        for c in range(4):
            o_ref[:, c * 128:(c + 1) * 128] = o[c * BLK:(c + 1) * BLK]
        lse_ref[...] = jnp.where(lane_o == 0, lses[0], jnp.where(lane_o == 1, lses[1], 0.0))

    return _pallas(
        body, grid=(cfg.B, cfg.NJ), name="win_fwd",
        in_specs=[pl.BlockSpec(memory_space=pltpu.SMEM), q_spec, *k_specs, *v_specs, bias_spec],
        out_specs=[pl.BlockSpec((BLK, 512), lambda b, j: (row(b, j), 0)), pl.BlockSpec((4 * BLK, 128), lambda b, j: (row(b, j), 0))],
        out_shape=[jax.ShapeDtypeStruct((cfg.TP, 512), F32), jax.ShapeDtypeStruct((4 * cfg.TP, 128), F32)],
        compiler_params=_cp("parallel", "parallel"),
    )(sink, pa, *([pa] * 8), bias)


def _win_bwd(cfg, pa, bias, sink, dya, ya, lse):
    row, jt, q_spec, k_specs, v_specs, bias_spec = _win_specs(cfg)
    nj = cfg.NJ

    def body(sink_ref, q_ref, k0, k1, k2, k3, v0, v1, v2, v3, b_ref, dy_ref, y_ref, lse_ref,
             dq_ref, dkp_ref, dvp_ref, dkm_ref, dvm_ref, s_ref, dsink_ref):
        j = pl.program_id(1)
        kind = jt(j)

        @pl.when((pl.program_id(0) == 0) & (j == 0))
        def _():
            s_ref[...] = jnp.zeros_like(s_ref)

        kk, v4 = _win_keys((k0, k1, k2, k3), (v0, v1, v2, v3))
        qs, dys, ys = _stack4(q_ref), _stack4(dy_ref), _stack4(y_ref)
        lane_o = _lane((4 * BLK, 128))
        half = (lane_o < 64, lane_o >= 64)
        lse_blk = lse_ref[...]
        dq = jnp.zeros((4 * BLK, 128), F32)
        dk4 = jnp.zeros((512, 128), F32)
        dv4 = jnp.zeros((512, 128), F32)
        dsink = jnp.zeros((8, 128), F32)
        lane_s = _lane((8, 128))
        row_s = lax.broadcasted_iota(jnp.int32, (8, 128), 0)
        for hf in range(2):
            lse_h = jnp.sum(jnp.where(lane_o == hf, lse_blk, 0.0), axis=1, keepdims=True)
            s = _dot_nt(qs, kk[hf]) + b_ref[0, 4 * hf:4 * hf + 4].reshape(4 * BLK, 512)
            p = jnp.exp2(s - lse_h)
            do_h = jnp.where(half[hf], dys, 0.0)
            delta = jnp.sum(do_h * ys, axis=1, keepdims=True)
            do_b = do_h.astype(BF16)
            ds = p * (_dot_nt(do_b, v4) - delta)
            s_ref[kind, 4 * hf:4 * hf + 4] += ds.reshape(4, BLK, 512)
            sink_grad = jnp.exp2(_sink_col(sink_ref, hf) - lse_h) * delta
            for c in range(4):
                tot = -jnp.sum(sink_grad[c * BLK:(c + 1) * BLK])
                dsink = jnp.where((row_s == 0) & (lane_s == 4 * hf + c), tot, dsink)
            dsb = (ds * LN2).astype(BF16)
            dq = dq + _dot(dsb, kk[hf])
            dk4 = dk4 + _dot_tn(dsb, jnp.where(half[hf], qs, jnp.zeros_like(qs)))
            dv4 = dv4 + _dot_tn(p.astype(BF16), do_b)
        for c in range(4):
            dq_ref[:, c * 128:(c + 1) * 128] = (dq[c * BLK:(c + 1) * BLK] * QA_SCALE).astype(BF16)
        dkp_ref[0] = dk4
        dvp_ref[0] = dv4

        @pl.when(j == 0)
        def _():
            dkm_ref[...] = dk4[:BLK]
            dvm_ref[...] = dv4[:BLK]

        @pl.when(j > 0)
        def _():
            dkm_ref[...] += dk4[:BLK]
            dvm_ref[...] += dv4[:BLK]

        @pl.when((pl.program_id(0) == 0) & (j == 0))
        def _():
            dsink_ref[...] = dsink

        @pl.when((pl.program_id(0) > 0) | (j > 0))
        def _():
            dsink_ref[...] += dsink

    blk_row = pl.BlockSpec((BLK, 512), lambda b, j: (row(b, j), 0))
    return _pallas(
        body, grid=(cfg.B, nj), name="win_bwd",
        in_specs=[pl.BlockSpec(memory_space=pltpu.SMEM), q_spec, *k_specs, *v_specs, bias_spec, blk_row, blk_row,
                  pl.BlockSpec((4 * BLK, 128), lambda b, j: (row(b, j), 0))],
        out_specs=[blk_row,
                   pl.BlockSpec((1, 512, 128), lambda b, j: (row(b, j), 0, 0)), pl.BlockSpec((1, 512, 128), lambda b, j: (row(b, j), 0, 0)),
                   pl.BlockSpec((BLK, 128), lambda b, j: (b, 0)), pl.BlockSpec((BLK, 128), lambda b, j: (b, 0)),
                   pl.BlockSpec((4, A_HEADS, BLK, 512), lambda b, j: (0, 0, 0, 0)),
                   pl.BlockSpec((8, 128), lambda b, j: (0, 0))],
        out_shape=[jax.ShapeDtypeStruct((cfg.TP, 512), BF16),
                   jax.ShapeDtypeStruct((cfg.B * nj, 512, 128), F32), jax.ShapeDtypeStruct((cfg.B * nj, 512, 128), F32),
                   jax.ShapeDtypeStruct((cfg.B * BLK, 128), F32), jax.ShapeDtypeStruct((cfg.B * BLK, 128), F32),
                   jax.ShapeDtypeStruct((4, A_HEADS, BLK, 512), F32),
                   jax.ShapeDtypeStruct((8, 128), F32)],
        compiler_params=_cp("arbitrary", "arbitrary"),
    )(sink, pa, *([pa] * 8), bias, dya, ya, lse)


def _win_dkv_combine(cfg, dkp, dvp, dkm, dvm):
    nj = cfg.NJ

    def body(kp, vp, km, vm, o_ref):
        o_ref[:BLK, :128] = km[...].astype(BF16)
        o_ref[:BLK, 128:] = vm[...].astype(BF16)
        for kb in range(1, nj):
            for col, part in ((0, kp), (128, vp)):
                tot = part[kb, 2 * BLK:3 * BLK] + part[kb - 1, 3 * BLK:4 * BLK]
                if kb + 1 < nj:
                    tot = tot + part[kb + 1, BLK:2 * BLK]
                o_ref[kb * BLK:(kb + 1) * BLK, col:col + 128] = tot.astype(BF16)

    return _pallas(
        body, grid=(cfg.B,), name="win_dkv_combine",
        in_specs=[pl.BlockSpec((nj, 512, 128), lambda b: (b, 0, 0)), pl.BlockSpec((nj, 512, 128), lambda b: (b, 0, 0)),
                  pl.BlockSpec((BLK, 128), lambda b: (b, 0)), pl.BlockSpec((BLK, 128), lambda b: (b, 0))],
        out_specs=pl.BlockSpec((cfg.LP, 256), lambda b: (b, 0)),
        out_shape=jax.ShapeDtypeStruct((cfg.TP, 256), BF16),
        compiler_params=_cp("parallel"),
    )(dkp, dvp, dkm, dvm)


def _pair_blockdiag(q):
    lane = _lane(q.shape)
    return jnp.concatenate([jnp.where(lane < 128, q, jnp.zeros_like(q)), jnp.where(lane >= 128, q, jnp.zeros_like(q))], axis=0)


def _mla_fwd(cfg, q, kt, v, comm=None):
    nj, lp = cfg.NJ, cfg.LP

    def body(q_ref, kt_ref, v_ref, o_ref, lse_ref, s_even, s_odd):
        i = pl.program_id(2)
        lane_o = _lane((BLK, 128))

        def logits(s_write):
            s_write[...] = _dot(_pair_blockdiag(q_ref[...]), kt_ref[...])

        def finish(s_read):
            s = s_read[...]
            m = jnp.max(s, axis=1, keepdims=True)
            e = jnp.exp2(s - m)
            den = jnp.sum(e, axis=1, keepdims=True)
            pv = _dot(e.astype(BF16), v_ref[...]) / den
            o_ref[...] = jnp.where(lane_o < 64, pv[:BLK], pv[BLK:])
            lse_ref[0] = jnp.broadcast_to(m + jnp.log2(den), (2 * BLK, 128))

        odd = i % 2 == 1

        @pl.when(i == 0)
        def _():
            logits(s_even)

        @pl.when(odd & (i < nj))
        def _():
            logits(s_odd)
            finish(s_even)

        @pl.when(jnp.logical_not(odd) & (i > 0) & (i < nj))
        def _():
            logits(s_even)
            finish(s_odd)

        @pl.when(i == nj)
        def _():
            finish(s_even if nj % 2 == 1 else s_odd)

    cur = lambda b, i: b * nj + jnp.minimum(i, nj - 1)
    prev = lambda b, i: b * nj + jnp.maximum(i - 1, 0)
    return _call_with_comm(
        body, comm, grid=(cfg.B, 4, nj + 1), name="mla_fwd",
        in_specs=[pl.BlockSpec((BLK, 256), lambda b, p, i: (cur(b, i), p)), pl.BlockSpec((256, lp), lambda b, p, i: (b * 4 + p, 0)),
                  pl.BlockSpec((lp, 128), lambda b, p, i: (b, p))],
        out_specs=[pl.BlockSpec((BLK, 128), lambda b, p, i: (prev(b, i), p)),
                   pl.BlockSpec((1, 2 * BLK, 128), lambda b, p, i: (p, prev(b, i), 0))],
        out_shape=[jax.ShapeDtypeStruct((cfg.TP, 512), F32), jax.ShapeDtypeStruct((4, 2 * cfg.TP, 128), F32)],
        scratch_shapes=[pltpu.VMEM((2 * BLK, lp), F32), pltpu.VMEM((2 * BLK, lp), F32)],
        args=(q, kt, v))


def _mla_bwd(cfg, q, k, kt, vt, dyb, yb, lse, comm=None):
    nj, lp = cfg.NJ, cfg.LP

    def body(q_ref, k_ref, kt_ref, vt_ref, dy_ref, y_ref, lse_ref, dq_ref, dk_ref, dv_ref):
        i = pl.program_id(2)

        @pl.when(i == 0)
        def _():
            dk_ref[...] = jnp.zeros_like(dk_ref)
            dv_ref[...] = jnp.zeros_like(dv_ref)

        lane_o = _lane((BLK, 128))
        qbd = _pair_blockdiag(q_ref[...])
        dy, y = dy_ref[...], y_ref[...]
        do_s = jnp.concatenate([jnp.where(lane_o < 64, dy, 0.0), jnp.where(lane_o >= 64, dy, 0.0)], axis=0)
        delta = jnp.sum(do_s * jnp.concatenate([y, y], axis=0), axis=1, keepdims=True)
        do_b = do_s.astype(BF16)
        p = jnp.exp2(_dot(qbd, kt_ref[...]) - lse_ref[0][:, :1])
        ds = p * (_dot(do_b, vt_ref[...]) - delta)
        dsb = (ds * LN2).astype(BF16)
        dq2 = _dot(dsb, k_ref[...])
        dq_ref[...] = jnp.where(_lane((BLK, 256)) < 128, dq2[:BLK], dq2[BLK:]) * Q_SCALE
        dk_ref[...] += _dot_tn(dsb, qbd)
        dv_ref[...] += _dot_tn(p.astype(BF16), do_b)

    return _call_with_comm(
        body, comm, grid=(cfg.B, 4, nj), name="mla_bwd",
        in_specs=[pl.BlockSpec((BLK, 256), lambda b, p, i: (b * nj + i, p)), pl.BlockSpec((lp, 256), lambda b, p, i: (b, p)),
                  pl.BlockSpec((256, lp), lambda b, p, i: (b * 4 + p, 0)), pl.BlockSpec((128, lp), lambda b, p, i: (b * 4 + p, 0)),
                  pl.BlockSpec((BLK, 128), lambda b, p, i: (b * nj + i, p)), pl.BlockSpec((BLK, 128), lambda b, p, i: (b * nj + i, p)),
                  pl.BlockSpec((1, 2 * BLK, 128), lambda b, p, i: (p, b * nj + i, 0))],
        out_specs=[pl.BlockSpec((BLK, 256), lambda b, p, i: (b * nj + i, p)), pl.BlockSpec((lp, 256), lambda b, p, i: (b, p)),
                   pl.BlockSpec((lp, 128), lambda b, p, i: (b, p))],
        out_shape=[jax.ShapeDtypeStruct((cfg.TP, 1024), F32), jax.ShapeDtypeStruct((cfg.TP, 1024), F32),
                   jax.ShapeDtypeStruct((cfg.TP, 512), F32)],
        args=(q, k, kt, vt, dyb, yb, lse))


def _loss_bwd(cfg, h, target, gf):
    nj, nb = cfg.NJ, cfg.NB

    def body(h_ref, t_ref, g_ref, dh_ref, loss_ref, dg_ref):
        b, j = pl.program_id(0), pl.program_id(1)

        @pl.when((b == 0) & (j == 0))
        def _():
            loss_ref[...] = jnp.zeros_like(loss_ref)
            dg_ref[...] = jnp.zeros_like(dg_ref)

        @pl.when(j == 0)
        def _():
            dh_ref[...] = jnp.zeros_like(dh_ref)

        @pl.when(j > 0)
        def _():
            g = g_ref[...]
            xh, r = _rms(h_ref[...])
            err = xh * g - t_ref[...]
            loss_ref[...] += jnp.where((lax.broadcasted_iota(jnp.int32, (8, 128), 0) == 0) & (_lane((8, 128)) == 0),
                                       (0.5 / D_MODEL) * jnp.sum(err * err), 0.0)
            dy = err * (1.0 / D_MODEL)
            dg_ref[...] += jnp.sum(dy * xh, axis=0, keepdims=True)
            dh_ref[...] = _rms_bwd(xh, r, dy * g)

    return _pallas(
        body, grid=(cfg.B, nj), name="loss_bwd",
        in_specs=[pl.BlockSpec((BLK, D_MODEL), lambda b, j: (b * nj + j, 0)),
                  pl.BlockSpec((BLK, D_MODEL), lambda b, j: (b * nb + jnp.maximum(j - 1, 0), 0)),
                  pl.BlockSpec((1, D_MODEL), lambda b, j: (0, 0))],
        out_specs=[pl.BlockSpec((BLK, D_MODEL), lambda b, j: (b * nj + j, 0)), pl.BlockSpec((8, 128), lambda b, j: (0, 0)),
                   pl.BlockSpec((1, D_MODEL), lambda b, j: (0, 0))],
        out_shape=[jax.ShapeDtypeStruct((cfg.TP, D_MODEL), F32), jax.ShapeDtypeStruct((8, 128), F32),
                   jax.ShapeDtypeStruct((1, D_MODEL), F32)],
        compiler_params=_cp("arbitrary", "arbitrary"),
    )(h, target, gf)


def _out_bwd(cfg, dh, ya, yb, pf, goa, gob, wo_p):
    tm = 256

    def body(dh_ref, ya_ref, yb_ref, ga_ref, gb_ref, goa_ref, gob_ref, w_ref,
             dya_ref, dyb_ref, dg_ref, dw_ref, dgoa_ref, dgob_ref):
        @pl.when(pl.program_id(0) == 0)
        def _():
            dw_ref[...] = jnp.zeros_like(dw_ref)
            dgoa_ref[...] = jnp.zeros_like(dgoa_ref)
            dgob_ref[...] = jnp.zeros_like(dgob_ref)

        ga, gb, goa, gob = ga_ref[...], gb_ref[...], goa_ref[...], gob_ref[...]
        xa, ra, xb, rb, sga, sgb, y_a, y_b = _gate_halves(ya_ref[...], yb_ref[...], ga, gb, goa, gob)
        dhb = dh_ref[...].astype(BF16)
        dw_ref[...] += _dot_tn(jnp.concatenate([y_a, y_b], axis=1).astype(BF16), dhb)
        dy = _dot_nt(dhb, w_ref[...])
        for (dyh, x, r, g, sg, go, dy_out, dgo_ref, col) in (
                (dy[:, :512], xa, ra, ga, sga, goa, dya_ref, dgoa_ref, 0), (dy[:, 512:], xb, rb, gb, sgb, gob, dyb_ref, dgob_ref, 512)):
            dn = dyh * (g * sg)
            dg_ref[:, col:col + 512] = (dyh * (x * go) * (sg * (1.0 + g * (1.0 - sg)))).astype(BF16)
            dgo_ref[...] += jnp.sum(dn * x, axis=0, keepdims=True)
            dy_out[...] = _rms_bwd(x, r, dn * go)

    half = lambda c: pl.BlockSpec((tm, 512), lambda i: (i, c))
    vec = pl.BlockSpec((1, 512), lambda i: (0, 0))
    return _pallas(
        body, grid=(cfg.TP // tm,), name="out_bwd",
        in_specs=[pl.BlockSpec((tm, D_MODEL), lambda i: (i, 0)), half(0), half(0), half(0), half(1), vec, vec,
                  pl.BlockSpec((D_MODEL, D_MODEL), lambda i: (0, 0))],
        out_specs=[half(0), half(0), pl.BlockSpec((tm, D_MODEL), lambda i: (i, 0)),
                   pl.BlockSpec((D_MODEL, D_MODEL), lambda i: (0, 0)), vec, vec],
        out_shape=[jax.ShapeDtypeStruct((cfg.TP, 512), F32), jax.ShapeDtypeStruct((cfg.TP, 512), F32),
                   jax.ShapeDtypeStruct((cfg.TP, D_MODEL), BF16), jax.ShapeDtypeStruct((D_MODEL, D_MODEL), F32),
                   jax.ShapeDtypeStruct((1, 512), F32), jax.ShapeDtypeStruct((1, 512), F32)],
        compiler_params=_cp("arbitrary"),
    )(dh, ya, yb, pf, pf, goa, gob, wo_p)


def _lat_bwd(cfg, dq, dk, dv, pf, gq, gkv, wq_p, wkv_p, c_tab, s_tab):
    nj = cfg.NJ

    def body(dq_ref, dk_ref, dv_ref, cq_ref, ckv_ref, gq_ref, gkv_ref, wq_ref, wkv_ref, c_ref, s_ref,
             dl_ref, dwq_ref, dwkv_ref, dgq_ref, dgkv_ref):
        @pl.when((pl.program_id(0) == 0) & (pl.program_id(1) == 0))
        def _():
            dwq_ref[...] = jnp.zeros_like(dwq_ref)
            dwkv_ref[...] = jnp.zeros_like(dwkv_ref)
            dgq_ref[...] = jnp.zeros_like(dgq_ref)
            dgkv_ref[...] = jnp.zeros_like(dgkv_ref)

        c1, s1 = c_ref[...], s_ref[...]
        c8, s8 = jnp.tile(c1, (1, 8)), jnp.tile(s1, (1, 8))
        dq_r = dq_ref[...]
        dqp = (dq_r * c8 + _swap_rope(dq_r * s8)).astype(BF16)
        gq = gq_ref[...]
        xq, rq = _rms(cq_ref[...])
        dwq_ref[...] += _dot_tn((xq * gq).astype(BF16), dqp)
        dn = _dot_nt(dqp, wq_ref[...])
        dgq_ref[...] += jnp.sum(dn * xq, axis=0, keepdims=True)
        dl_ref[:, :256] = _rms_bwd(xq, rq, dn * gq).astype(BF16)

        dk_r = dk_ref[...]
        dkr = dk_r[:, :128]
        for hd in range(1, 8):
            dkr = dkr + dk_r[:, hd * 128:(hd + 1) * 128]
        lane1 = _lane(dkr.shape)
        dkr = jnp.where((lane1 >= 64) & (lane1 < 96), dkr, 0.0)
        dl_ref[:, 384:] = (dkr * c1 + _swap_rope(dkr * s1)).astype(BF16)
        dkv = jnp.concatenate([dk_r, dv_ref[...]], axis=1).astype(BF16)
        gkv = gkv_ref[...]
        xk, rk = _rms(ckv_ref[...])
        dwkv_ref[...] += _dot_tn((xk * gkv).astype(BF16), dkv)
        dn2 = _dot_nt(dkv, wkv_ref[...])
        dgkv_ref[...] += jnp.sum(dn2 * xk, axis=0, keepdims=True)
        dl_ref[:, 256:384] = _rms_bwd(xk, rk, dn2 * gkv).astype(BF16)

    row = lambda b, j: b * nj + j
    const = lambda shape: pl.BlockSpec(shape, lambda b, j: (0, 0))
    return _pallas(
        body, grid=(cfg.B, nj), name="lat_bwd",
        in_specs=[pl.BlockSpec((BLK, 1024), lambda b, j: (row(b, j), 0)), pl.BlockSpec((BLK, 1024), lambda b, j: (row(b, j), 0)),
                  pl.BlockSpec((BLK, 512), lambda b, j: (row(b, j), 0)),
                  pl.BlockSpec((BLK, 256), lambda b, j: (row(b, j), 4)), pl.BlockSpec((BLK, 128), lambda b, j: (row(b, j), 10)),
                  const((1, 256)), const((1, 128)), const((256, 1024)), const((128, 1536)),
                  pl.BlockSpec((BLK, 128), lambda b, j: (j, 0)), pl.BlockSpec((BLK, 128), lambda b, j: (j, 0))],
        out_specs=[pl.BlockSpec((BLK, 512), lambda b, j: (row(b, j), 0)), const((256, 1024)), const((128, 1536)),
                   const((1, 256)), const((1, 128))],
        out_shape=[jax.ShapeDtypeStruct((cfg.TP, 512), BF16), jax.ShapeDtypeStruct((256, 1024), F32),
                   jax.ShapeDtypeStruct((128, 1536), F32), jax.ShapeDtypeStruct((1, 256), F32), jax.ShapeDtypeStruct((1, 128), F32)],
        compiler_params=_cp("arbitrary", "arbitrary"),
    )(dq, dk, dv, pf, pf, gq, gkv, wq_p, wkv_p, c_tab, s_tab)


def _inproj_bwd(cfg, h, g, w_p, dqa, dkva, dgate, dlat, dh, comm=None):
    tm = 256

    def body(h_ref, g_ref, w_ref, dqa_ref, dkva_ref, dg_ref, dl_ref, dh_ref, o_ref, dw_ref, dgn_ref):
        @pl.when(pl.program_id(0) == 0)
        def _():
            dw_ref[...] = jnp.zeros_like(dw_ref)
            dgn_ref[...] = jnp.zeros_like(dgn_ref)

        g = g_ref[...]
        xh, r = _rms(h_ref[...])
        dproj = jnp.concatenate([dqa_ref[...], dkva_ref[...], dg_ref[...], dl_ref[...]], axis=1)
        dw_ref[...] += _dot_tn((xh * g).astype(BF16), dproj)
        du = _dot_nt(dproj, w_ref[...])
        dgn_ref[...] += jnp.sum(du * xh, axis=0, keepdims=True)
        o_ref[...] = dh_ref[...] + _rms_bwd(xh, r, du * g)

    rows = lambda w: pl.BlockSpec((tm, w), lambda i: (i, 0))
    return _call_with_comm(
        body, comm, grid=(cfg.TP // tm,), name="inproj_bwd",
        in_specs=[rows(D_MODEL), pl.BlockSpec((1, D_MODEL), lambda i: (0, 0)), pl.BlockSpec((D_MODEL, W_IN_P), lambda i: (0, 0)),
                  rows(512), rows(256), rows(1024), rows(512), rows(D_MODEL)],
        out_specs=[rows(D_MODEL), pl.BlockSpec((D_MODEL, W_IN_P), lambda i: (0, 0)), pl.BlockSpec((1, D_MODEL), lambda i: (0, 0))],
        out_shape=[jax.ShapeDtypeStruct((cfg.TP, D_MODEL), F32), jax.ShapeDtypeStruct((D_MODEL, W_IN_P), F32),
                   jax.ShapeDtypeStruct((1, D_MODEL), F32)],
        args=(h, g, w_p, dqa, dkva, dgate, dlat, dh))


def _meta_grad(cfg, dh):
    def body(d_ref, o_ref):
        @pl.when(pl.program_id(0) == 0)
        def _():
            o_ref[...] = d_ref[...]

        @pl.when(pl.program_id(0) > 0)
        def _():
            o_ref[...] += d_ref[...]

    return _pallas(
        body, grid=(cfg.B,), name="meta_grad",
        in_specs=[pl.BlockSpec((BLK, D_MODEL), lambda b: (b * cfg.NJ, 0))],
        out_specs=pl.BlockSpec((BLK, D_MODEL), lambda b: (0, 0)),
        out_shape=jax.ShapeDtypeStruct((BLK, D_MODEL), F32),
        compiler_params=_cp("arbitrary"),
    )(dh)


MATRICES = ("w_in", "w_uq", "w_ukv", "w_out")


def _local_grads(cfg, x, target, meta_of, table, small, weight_of, rider=None):
    def ride(stage, i, mats):
        hook = rider(stage, i, mats) if rider else None
        return hook if hook else (None, lambda res: None)

    depth = small["norm_in"].shape[0]
    rel, vis = _window_structure(cfg.NJ)
    bucket = _t5_bucket(jnp.asarray(rel))
    maskadd = jnp.asarray(np.where(vis, 0.0, NEG).astype(np.float32))
    c_tab, s_tab = _rope_tables(cfg)
    comm, deliver = ride("bias_build", 0, {})
    bias, *travelled = _bias_build(table, bucket, maskadd, comm)
    deliver(travelled)
    meta_blk = jnp.concatenate([meta_of(), jnp.zeros((BLK - N_META, D_MODEL), F32)], axis=0)
    h = jnp.concatenate([jnp.broadcast_to(meta_blk[None], (cfg.B, BLK, D_MODEL)), x], axis=1).reshape(cfg.TP, D_MODEL)

    wp, saved = [], []
    for i in range(depth):
        w = dict(w_in=_w_in_to_p(weight_of(i, "w_in")),
                 g_in=small["norm_in"][i][None], gq=small["norm_q_lat"][i][None], gkv=small["norm_kv_lat"][i][None],
                 goa=_perm_heads64(small["norm_out_a"][i], 0)[None], gob=small["norm_out_b"][i][None], sink=small["sink_a"][i])
        wp.append(w)
        comm, deliver = ride("inproj_fwd", i, {})
        pa, pf, *travelled = _inproj_fwd(cfg, h, w["g_in"], w["w_in"], comm)
        deliver(travelled)
        w.update(w_uq=_w_uq_to_p(weight_of(i, "w_uq")), w_ukv=_w_ukv_to_p(weight_of(i, "w_ukv")), w_out=_w_out_to_p(weight_of(i, "w_out")))
        q, k, v, kt, vt = _lat_fwd(cfg, pf, w["gq"], w["gkv"], w["w_uq"], w["w_ukv"], c_tab, s_tab)
        ya, lse_a = _win_fwd(cfg, pa, bias, w["sink"])
        comm, deliver = ride("mla_fwd", i, {})
        yb, lse_b, *travelled = _mla_fwd(cfg, q, kt, v, comm)
        deliver(travelled)
        h_next = _out_fwd(cfg, ya, yb, pf, w["goa"], w["gob"], w["w_out"], h)
        saved.append(dict(h=h, pa=pa, pf=pf, q=q, k=k, kt=kt, vt=vt, ya=ya, lse_a=lse_a, yb=yb, lse_b=lse_b))
        h = h_next

    dh, loss_tile, d_norm_final = _loss_bwd(cfg, h, target.reshape(cfg.B * cfg.S, D_MODEL), small["norm_final"][None])

    grads = {k_: [] for k_ in ("norm_in", "sink_a", "norm_q_lat", "norm_kv_lat", "norm_out_a", "norm_out_b")}
    mats, s_accs = {}, []
    for i in reversed(range(depth)):
        w, sv = wp[i], saved[i]
        dya, dyb, dgate, dwo, dgoa, dgob = _out_bwd(cfg, dh, sv["ya"], sv["yb"], sv["pf"], w["goa"], w["gob"], w["w_out"])
        dqa, dkp, dvp, dkm, dvm, s_acc, dsink = _win_bwd(cfg, sv["pa"], bias, w["sink"], dya, sv["ya"], sv["lse_a"])
        dkva = _win_dkv_combine(cfg, dkp, dvp, dkm, dvm)
        comm, deliver = ride("mla_bwd", i, mats)
        dq, dk, dv, *travelled = _mla_bwd(cfg, sv["q"], sv["k"], sv["kt"], sv["vt"], dyb, sv["yb"], sv["lse_b"], comm)
        deliver(travelled)
        dlat, dwq, dwkv, dgq, dgkv = _lat_bwd(cfg, dq, dk, dv, sv["pf"], w["gq"], w["gkv"], w["w_uq"], w["w_ukv"], c_tab, s_tab)
        mats[i] = dict(w_uq=_w_uq_from_p(dwq), w_ukv=_w_ukv_from_p(dwkv), w_out=_w_out_from_p(dwo))
        comm, deliver = ride("inproj_bwd", i, mats)
        dh, dwin, dgin, *travelled = _inproj_bwd(cfg, sv["h"], w["g_in"], w["w_in"], dqa, dkva, dgate, dlat, dh, comm)
        deliver(travelled)
        s_accs.append(s_acc)
        mats[i]["w_in"] = _w_in_from_p(dwin)
        grads["norm_in"].append(dgin[0])
        grads["sink_a"].append(dsink[0, :A_HEADS])
        grads["norm_q_lat"].append(dgq[0])
        grads["norm_kv_lat"].append(dgkv[0])
        grads["norm_out_a"].append(_unperm_heads64(dgoa[0], 0))
        grads["norm_out_b"].append(dgob[0])

    out = {k_: jnp.stack(v_[::-1]) for k_, v_ in grads.items()}
    mats["meta_tokens"] = _meta_grad(cfg, dh)[:N_META]
    comm, deliver = ride("bias_grad", 0, mats)
    dtable, *travelled = _bias_grad(s_accs, bucket, comm)
    deliver(travelled)
    out["rel_bias_table"] = dtable[:, 0].reshape(N_BUCKETS, A_HEADS)
    out["norm_final"] = d_norm_final[0]
    return loss_tile[0, 0], dh.reshape(cfg.B, cfg.LP, D_MODEL)[:, BLK:], out, mats


MESH = pl.DeviceIdType.MESH
ANY = pl.BlockSpec(memory_space=pl.ANY)


def _place():
    x, y, c = lax.axis_index("x"), lax.axis_index("y"), lax.axis_index("c")
    others = [(1 - x, y), (x, 1 - y), (1 - x, 1 - y)]
    return x, y, c, others


Comm = collections.namedtuple("Comm", "inputs out_shapes scratch start wait")


def _gather_comm(shards):
    n = len(shards)

    def copies(ins, outs, sems, arriving):
        send_sems, recv_sems = sems
        x, y, c, others = _place()
        k_me = 2 * x + y
        return [pltpu.make_async_remote_copy(src_ref=ins[a], dst_ref=outs[a].at[2 * ox + oy if arriving else k_me],
                                             send_sem=send_sems.at[3 * a + j], recv_sem=recv_sems.at[3 * a + j],
                                             device_id=(ox, oy, c), device_id_type=MESH)
                for a in range(n) for j, (ox, oy) in enumerate(others)]

    def start(ins, outs, sems):
        for cp in copies(ins, outs, sems, arriving=False):
            cp.start()

    def wait(ins, outs, sems):
        recvs = copies(ins, outs, sems, arriving=True)
        for cp in recvs:
            cp.wait_recv()
        for cp in recvs:
            cp.wait_send()

    return Comm(list(shards), [jax.ShapeDtypeStruct((4, *s.shape), s.dtype) for s in shards],
                [pltpu.SemaphoreType.DMA((3 * n,)), pltpu.SemaphoreType.DMA((3 * n,))], start, wait)


def _with_own(gathered, own):
    k_me = 2 * lax.axis_index("x") + lax.axis_index("y")
    return lax.dynamic_update_index_in_dim(gathered, own, k_me, 0)


def _gather_halves_comm(shards):
    n = len(shards)

    def copies(ins, outs, sems, kind):
        send_sems, recv_sems, fwd_send_sems, fwd_recv_sems = sems
        x, y, c, others = _place()
        k_me = 2 * x + y

        def half(ref, which):
            rows = ref.shape[0] // 2
            return ref.at[pl.ds(pl.multiple_of(which * rows, 8), rows)]

        made = []
        for a in range(n):
            for j, (ox, oy) in enumerate(others):
                slot = outs[a].at[k_me if kind == "sent" else 2 * ox + oy]
                if kind in ("sent", "arrived"):
                    made.append(pltpu.make_async_remote_copy(
                        src_ref=half(ins[a], c), dst_ref=half(slot, c), send_sem=send_sems.at[3 * a + j],
                        recv_sem=recv_sems.at[3 * a + j], device_id=(ox, oy, c), device_id_type=MESH))
                else:
                    which = c if kind == "forward" else 1 - c
                    made.append(pltpu.make_async_remote_copy(
                        src_ref=half(slot, which), dst_ref=half(slot, which), send_sem=fwd_send_sems.at[3 * a + j],
                        recv_sem=fwd_recv_sems.at[3 * a + j], device_id=(x, y, 1 - c), device_id_type=MESH))
        return made

    def start(ins, outs, sems):
        for cp in copies(ins, outs, sems, "sent"):
            cp.start()

    def wait(ins, outs, sems):
        arrived, forward = copies(ins, outs, sems, "arrived"), copies(ins, outs, sems, "forward")
        for came, on in zip(arrived, forward):
            came.wait_recv()
            on.start()
        for cp in copies(ins, outs, sems, "forwarded"):
            cp.wait_recv()
        for cp in arrived + forward:
            cp.wait_send()

    return Comm(list(shards), [jax.ShapeDtypeStruct((4, *s.shape), s.dtype) for s in shards],
                [pltpu.SemaphoreType.DMA((3 * n,))] * 4, start, wait)


def _scatter_comm(parts):
    n = len(parts)

    def copies(ins, outs, sems):
        send_sems, recv_sems = sems
        x, y, c, others = _place()
        return [pltpu.make_async_remote_copy(src_ref=ins[a].at[2 * ox + oy], dst_ref=outs[a].at[j], send_sem=send_sems.at[3 * a + j],
                                             recv_sem=recv_sems.at[3 * a + j], device_id=(ox, oy, c), device_id_type=MESH)
                for a in range(n) for j, (ox, oy) in enumerate(others)]

    def start(ins, outs, sems):
        for cp in copies(ins, outs, sems):
            cp.start()

    def wait(ins, outs, sems):
        cps = copies(ins, outs, sems)
        for cp in cps:
            cp.wait_recv()
        for cp in cps:
            cp.wait_send()

    return Comm(list(parts), [jax.ShapeDtypeStruct((3, *p.shape[1:]), p.dtype) for p in parts],
                [pltpu.SemaphoreType.DMA((3 * n,)), pltpu.SemaphoreType.DMA((3 * n,))], start, wait)


def _call_with_comm(body, comm, *, grid, name, in_specs, out_specs, out_shape, args, scratch_shapes=()):
    if comm is None:
        return _pallas(body, grid=grid, name=name, in_specs=in_specs, out_specs=out_specs, out_shape=out_shape,
                              scratch_shapes=list(scratch_shapes), compiler_params=_cp(*["arbitrary"] * len(grid)))(*args)
    n_in, n_out, ci, co, ns = len(in_specs), len(out_specs), len(comm.inputs), len(comm.out_shapes), len(scratch_shapes)

    def wrapped(*refs):
        ins, cins = refs[:n_in], refs[n_in:n_in + ci]
        outs, couts = refs[n_in + ci:n_in + ci + n_out], refs[n_in + ci + n_out:n_in + ci + n_out + co]
        scratch, sems = refs[n_in + ci + n_out + co:n_in + ci + n_out + co + ns], refs[n_in + ci + n_out + co + ns:]
        ids = [pl.program_id(a) for a in range(len(grid))]
        first = functools.reduce(jnp.logical_and, [i == 0 for i in ids])
        last = functools.reduce(jnp.logical_and, [i == g - 1 for i, g in zip(ids, grid)])

        @pl.when(first)
        def _():
            comm.start(cins, couts, sems)

        body(*ins, *outs, *scratch)

        @pl.when(last)
        def _():
            comm.wait(cins, couts, sems)

    return _pallas(
        wrapped, grid=grid, name=name + "_comm", in_specs=[*in_specs, *[ANY] * ci], out_specs=[*out_specs, *[ANY] * co],
        out_shape=[*out_shape, *comm.out_shapes], scratch_shapes=[*scratch_shapes, *comm.scratch],
        compiler_params=_cp(*["arbitrary"] * len(grid)))(*args, *comm.inputs)


def _swap_sibling(arrs):
    n = len(arrs)

    def body(*refs):
        ins, outs = refs[:n], refs[n:2 * n]
        send_sems, recv_sems = refs[2 * n:]
        x, y, c, _ = _place()
        copies = [pltpu.make_async_remote_copy(src_ref=ins[a], dst_ref=outs[a], send_sem=send_sems.at[a], recv_sem=recv_sems.at[a],
                                               device_id=(x, y, 1 - c), device_id_type=MESH) for a in range(n)]
        for cp in copies:
            cp.start()
        for cp in copies:
            cp.wait_recv()
        for cp in copies:
            cp.wait_send()

    return _pallas(
        body, name="swap_sibling", in_specs=[ANY] * n, out_specs=[ANY] * n,
        out_shape=[jax.ShapeDtypeStruct(a.shape, a.dtype) for a in arrs],
        scratch_shapes=[pltpu.SemaphoreType.DMA((n,)), pltpu.SemaphoreType.DMA((n,))],
    )(*arrs)


def _allreduce_small(v):
    def body(v_ref, o_ref, buf, send_sems, recv_sems):
        x, y, c, _ = _place()
        me = 4 * x + 2 * y + c
        buf[me] = v_ref[...]

        def copy(r):
            tx, ty, tc = (x + (r >> 2)) % 2, (y + ((r >> 1) & 1)) % 2, (c + (r & 1)) % 2
            return tx, ty, tc

        sends = []
        for r in range(1, 8):
            tx, ty, tc = copy(r)
            sends.append(pltpu.make_async_remote_copy(src_ref=v_ref, dst_ref=buf.at[me], send_sem=send_sems.at[r - 1],
                                                      recv_sem=recv_sems.at[r - 1], device_id=(tx, ty, tc), device_id_type=MESH))
        for cp in sends:
            cp.start()
        for r in range(1, 8):
            tx, ty, tc = copy(r)
            pltpu.make_async_remote_copy(src_ref=v_ref, dst_ref=buf.at[4 * tx + 2 * ty + tc], send_sem=send_sems.at[r - 1],
                                         recv_sem=recv_sems.at[r - 1], device_id=(tx, ty, tc), device_id_type=MESH).wait_recv()
        for cp in sends:
            cp.wait_send()
        acc = buf[0]
        for d in range(1, 8):
            acc = acc + buf[d]
        o_ref[...] = acc

    return pl.pallas_call(
        body, name="allreduce_small", in_specs=[pl.BlockSpec(memory_space=pltpu.VMEM)], out_specs=pl.BlockSpec(memory_space=pltpu.VMEM),
        out_shape=jax.ShapeDtypeStruct(v.shape, F32),
        scratch_shapes=[pltpu.VMEM((8, *v.shape), F32), pltpu.SemaphoreType.DMA((7,)), pltpu.SemaphoreType.DMA((7,))],
    )(v)


def _rows_view(a):
    return a.reshape(-1, a.shape[-1])


def _elementwise(name, fn, ins, n_out):
    rows, cols = ins[0].shape
    tm = min(rows, 256)
    spec = pl.BlockSpec((tm, cols), lambda i: (i, 0))

    def body(*refs):
        outs = fn(*[r[...] for r in refs[:len(ins)]])
        for o_ref, o in zip(refs[len(ins):], outs):
            o_ref[...] = o

    return _pallas(
        body, grid=(rows // tm,), name=name, in_specs=[spec] * len(ins), out_specs=[spec] * n_out,
        out_shape=[jax.ShapeDtypeStruct((rows, cols), F32)] * n_out, compiler_params=_cp("parallel"),
    )(*ins)


def _sum_parts(name, own, recv):
    def fn(o, r0, r1, r2):
        return (o + r0.astype(F32) + r1.astype(F32) + r2.astype(F32),)

    return _elementwise("sum_parts_" + name, fn, [own, recv[0], recv[1], recv[2]], 1)[0]


def _adamw(name, w, m, v, g_parts):
    def fn(w_, m_, v_, *gs):
        g = gs[0]
        for extra in gs[1:]:
            g = g + extra
        m_new = ADAM_B1 * m_ + (1.0 - ADAM_B1) * g
        v_new = ADAM_B2 * v_ + (1.0 - ADAM_B2) * (g * g)
        m_hat = m_new / (1.0 - ADAM_B1 ** ADAM_STEP)
        v_hat = v_new / (1.0 - ADAM_B2 ** ADAM_STEP)
        delta = -ADAM_LR * (m_hat / (jnp.sqrt(v_hat) + ADAM_EPS) + ADAM_WD * w_)
        return g, delta, m_new, v_new

    return _elementwise("adamw_" + name, fn, [w, m, v, *g_parts], 4)


MAT_AXIS = {"w_in": 1, "w_uq": 1, "w_ukv": 1, "w_out": 0}
SMALL = ("rel_bias_table", "norm_in", "sink_a", "norm_q_lat", "norm_kv_lat", "norm_out_a", "norm_out_b", "norm_final")
WEIGHTS = ("meta_tokens", "rel_bias_table", "norm_in", "w_in", "sink_a", "norm_q_lat", "w_uq", "norm_kv_lat", "w_ukv",
           "norm_out_a", "norm_out_b", "w_out", "norm_final")
SMALL_ROWS, SMALL_COLS = 8, 1024


def _pack_small(d, loss=None):
    flat = [d[n].reshape(-1) for n in SMALL]
    if loss is not None:
        flat.append(loss.reshape(1))
    used = sum(f.shape[0] for f in flat)
    flat.append(jnp.zeros((SMALL_ROWS * SMALL_COLS - used,), F32))
    return jnp.concatenate(flat).reshape(SMALL_ROWS, SMALL_COLS)


def _unpack_small(p, like):
    flat, out, off = p.reshape(-1), {}, 0
    for n in SMALL:
        size = int(np.prod(like[n].shape))
        out[n] = flat[off:off + size].reshape(like[n].shape)
        off += size
    return out, flat[off]


def _split4(a, axis):
    size = a.shape[axis] // 4
    return jnp.stack([lax.slice_in_dim(a, k * size, (k + 1) * size, axis=axis) for k in range(4)])


def _train_step(cfg, x, target, w, m, v):
    depth = w["w_in"].shape[0]
    rest = tuple(n for n in MATRICES if n != "w_in")
    weights, splits, received = {}, {}, {}

    def gather(i, names, also=(), build=_gather_comm):
        shards = [w[n][i].astype(BF16) for n in names]

        def deliver(res):
            for n, own, g in zip(names, shards, res):
                g = _with_own(g, own)
                weights[i, n] = jnp.concatenate([g[k] for k in range(4)], axis=MAT_AXIS[n])

        return build(shards + list(also)), deliver

    def scatter(i, names, mats, also=()):
        for n in names:
            splits[i, n] = _split4(mats[i][n], MAT_AXIS[n])

        def deliver(res):
            for n, r in zip(names, res):
                received[i, n] = r

        return _scatter_comm([splits[i, n].astype(BF16) for n in names] + list(also)), deliver

    def rider(stage, i, mats):
        if stage == "bias_build":
            comm, deliver = gather(0, ("w_in",), also=[w["meta_tokens"]], build=_gather_halves_comm)

            def deliver_first(res):
                deliver(res)
                metas = _with_own(res[1], w["meta_tokens"])
                weights["meta"] = jnp.concatenate([metas[k] for k in range(4)], axis=1)

            return comm, deliver_first
        if stage == "inproj_fwd" and i == 0:
            return gather(0, rest)
        if stage == "mla_fwd" and i + 1 < depth:
            return gather(i + 1, MATRICES)
        if stage == "mla_bwd" and i + 1 < depth:
            return scatter(i + 1, MATRICES, mats)
        if stage == "inproj_bwd" and i == 0:
            return scatter(0, rest, mats)
        if stage == "bias_grad":
            splits["meta"] = _split4(mats["meta_tokens"], 1)
            comm, deliver = scatter(0, ("w_in",), mats, also=[splits["meta"].astype(BF16)])

            def deliver_last(res):
                deliver(res)
                received["meta"] = res[1]

            return comm, deliver_last
        return None

    loss_local, grad_x, g, mats = _local_grads(cfg, x, target, lambda: weights["meta"], w["rel_bias_table"], {n: w[n] for n in SMALL},
                                               lambda i, n: weights[i, n], rider)

    small_sum = _allreduce_small(_pack_small(g, loss_local))
    g_small, loss = _unpack_small(small_sum, {n: w[n] for n in SMALL})

    k_me = 2 * lax.axis_index("x") + lax.axis_index("y")

    def core_sum(name, split, recv):
        own = lax.dynamic_index_in_dim(split, k_me, 0, keepdims=False)
        return _sum_parts(name, _rows_view(own), recv.reshape(3, -1, recv.shape[-1]))

    partial = [core_sum("meta_tokens", splits["meta"], received["meta"])]
    for n in MATRICES:
        partial.append(jnp.concatenate([core_sum(f"{n}_{i}", splits[i, n], received[i, n]) for i in range(depth)], axis=0))
    sibling = _swap_sibling(partial)

    outs = {}
    for n, p_me, p_sib in zip(("meta_tokens", *MATRICES), partial, sibling):
        res = _adamw(n, _rows_view(w[n]), _rows_view(m[n]), _rows_view(v[n]), [p_me, p_sib])
        outs[n] = [r.reshape(w[n].shape) for r in res]
    res = _adamw("small", _pack_small(w), _pack_small(m), _pack_small(v), [_pack_small(g_small)])
    unpacked = [_unpack_small(r, {n: w[n] for n in SMALL})[0] for r in res]
    for n in SMALL:
        outs[n] = [u[n] for u in unpacked]

    result = [loss, grad_x]
    for field in range(4):
        result.extend(outs[n][field] for n in WEIGHTS)
    return tuple(result)


def kernel(x, meta_tokens, rel_bias_table, norm_in, w_in, sink_a, norm_q_lat, w_uq, norm_kv_lat, w_ukv, norm_out_a, norm_out_b, w_out, norm_final, loss_target, m_meta_tokens, m_rel_bias_table, m_norm_in, m_w_in, m_sink_a, m_norm_q_lat, m_w_uq, m_norm_kv_lat, m_w_ukv, m_norm_out_a, m_norm_out_b, m_w_out, m_norm_final, v_meta_tokens, v_rel_bias_table, v_norm_in, v_w_in, v_sink_a, v_norm_q_lat, v_w_uq, v_norm_kv_lat, v_w_ukv, v_norm_out_a, v_norm_out_b, v_w_out, v_norm_final):
    w = dict(zip(WEIGHTS, (meta_tokens, rel_bias_table, norm_in, w_in, sink_a, norm_q_lat, w_uq, norm_kv_lat, w_ukv, norm_out_a, norm_out_b, w_out, norm_final)))
    m = dict(zip(WEIGHTS, (m_meta_tokens, m_rel_bias_table, m_norm_in, m_w_in, m_sink_a, m_norm_q_lat, m_w_uq, m_norm_kv_lat, m_w_ukv, m_norm_out_a, m_norm_out_b, m_w_out, m_norm_final)))
    v = dict(zip(WEIGHTS, (v_meta_tokens, v_rel_bias_table, v_norm_in, v_w_in, v_sink_a, v_norm_q_lat, v_w_uq, v_norm_kv_lat, v_w_ukv, v_norm_out_a, v_norm_out_b, v_w_out, v_norm_final)))
    cfg = make_cfg(x.shape[0], x.shape[1])
    return _train_step(cfg, x, loss_target, w, m, v)
```

```python
import collections
import functools
import math

import jax
import jax.numpy as jnp
import numpy as np
from jax import lax
from jax.experimental import pallas as pl
from jax.experimental.pallas import tpu as pltpu

F32 = jnp.float32
BF16 = jnp.bfloat16

BLK = 128
N_META = 16
D_MODEL = 1024
A_HEADS, A_KV, A_DH = 8, 2, 64
B_HEADS, B_NOPE, B_ROPE, B_DV = 8, 64, 32, 64
Q_RANK, KV_RANK = 256, 128
N_BUCKETS, MAX_DIST = 32, 128
ROPE_THETA = 10000.0
EPS = 1e-6
IN_WIDTH = 2208
W_IN_P = 2304
NEG = -1e30
MASK_LANE = 96
LOG2E = math.log2(math.e)
Q_SCALE = (B_NOPE + B_ROPE) ** -0.5 * LOG2E
QA_SCALE = A_DH ** -0.5 * LOG2E
LN2 = math.log(2.0)
VMEM_LIMIT = 48 * 1024 * 1024

ADAM_LR, ADAM_B1, ADAM_B2, ADAM_EPS, ADAM_WD, ADAM_STEP = 0.001, 0.9, 0.999, 1e-08, 0.01, 10

Cfg = collections.namedtuple("Cfg", "B S NB NJ LP TP")


def make_cfg(batch, seq):
    nb = seq // BLK
    nj = nb + 1
    return Cfg(batch, seq, nb, nj, nj * BLK, batch * nj * BLK)


def _cp(*sem):
    return pltpu.CompilerParams(dimension_semantics=sem, vmem_limit_bytes=VMEM_LIMIT)


def _pallas(body, *, out_shape, **kw):
    pinned = jax.tree.map(lambda s: pltpu.HBM(s.shape, s.dtype), out_shape)
    call = pl.pallas_call(body, out_shape=pinned, **kw)
    return lambda *args: call(*[pltpu.with_memory_space_constraint(a, pltpu.HBM) for a in args])


def _dot(a, b):
    return jnp.dot(a, b, preferred_element_type=F32)


def _dot_nt(a, b):
    return lax.dot_general(a, b, (((1,), (1,)), ((), ())), preferred_element_type=F32)


def _dot_tn(a, b):
    return lax.dot_general(a, b, (((0,), (0,)), ((), ())), preferred_element_type=F32)


def _rms(x, width=None):
    n = x.shape[-1] if width is None else width
    r = lax.rsqrt(jnp.sum(x * x, axis=-1, keepdims=True) * (1.0 / n) + EPS)
    return x * r, r


def _rms_bwd(xhat, r, t):
    n = xhat.shape[-1]
    return r * (t - xhat * (jnp.sum(t * xhat, axis=-1, keepdims=True) * (1.0 / n)))


def _sigmoid(x):
    return 1.0 / (1.0 + jnp.exp(-x))


def _lane(shape):
    return lax.broadcasted_iota(jnp.int32, shape, len(shape) - 1)


def _swap_rope(x):
    n = x.shape[-1]
    lane = _lane(x.shape) % BLK
    up = pltpu.roll(x, n - 16, axis=x.ndim - 1)
    dn = pltpu.roll(x, 16, axis=x.ndim - 1)
    return jnp.where((lane >= 64) & (lane < 80), up, jnp.where((lane >= 80) & (lane < 96), dn, 0.0))


A_ORDER = (0, 4, 1, 5, 2, 6, 3, 7)


def _jtype(j, nj):
    return 0 if j == 0 else 1 if j == 1 else 3 if j == nj - 1 else 2


def _window_structure(nj):
    def pos(blk, r):
        return np.where(blk == 0, r, N_META + (blk - 1) * BLK + r)

    def valid(blk, r):
        return np.where(blk == 0, r < N_META, True)

    r = np.arange(BLK)
    rels, viss = [], []
    for j in (0, 1, 2, nj - 1):
        qpos = pos(j, r)[:, None]
        rel_t, vis_t = [], []
        for s, kb in enumerate((0, j - 1, j, j + 1)):
            slot_ok = (s == 0) or (1 <= kb <= nj - 1)
            kbc = min(max(kb, 0), nj - 1)
            kpos = pos(kbc, r)[None, :]
            rel = kpos - qpos
            v = valid(kbc, r)[None, :] & np.ones((BLK, 1), bool)
            if s > 0:
                v = v & (np.abs(rel) <= BLK)
            rel_t.append(rel)
            vis_t.append(v & slot_ok)
        rels.append(np.concatenate(rel_t, axis=1))
        viss.append(np.concatenate(vis_t, axis=1))
    return np.stack(rels).astype(np.int32), np.stack(viss)


def _t5_bucket(rel):
    nb = N_BUCKETS // 2
    max_exact = nb // 2
    ret = jnp.where(rel > 0, nb, 0)
    n = jnp.abs(rel)
    nf = jnp.maximum(n, 1).astype(jnp.float32)
    large = max_exact + (jnp.log(nf / max_exact) / math.log(MAX_DIST / max_exact) * (nb - max_exact)).astype(jnp.int32)
    large = jnp.minimum(large, nb - 1)
    return ret + jnp.where(n < max_exact, n, large)


def _perm_heads64(a, axis):
    parts = [lax.slice_in_dim(a, h * 64, (h + 1) * 64, axis=axis) for h in A_ORDER]
    return jnp.concatenate(parts, axis=axis)


def _unperm_heads64(a, axis):
    inv = [A_ORDER.index(h) for h in range(8)]
    parts = [lax.slice_in_dim(a, p * 64, (p + 1) * 64, axis=axis) for p in inv]
    return jnp.concatenate(parts, axis=axis)


def _w_in_to_p(w):
    sl = lambda a, b: lax.slice_in_dim(w, a, b, axis=1)
    z = lambda n: jnp.zeros((w.shape[0], n), w.dtype)
    return jnp.concatenate([_perm_heads64(sl(0, 512), 1), sl(512, 768), _perm_heads64(sl(768, 1280), 1), sl(1696, 2208),
                            sl(1280, 1536), sl(1536, 1664), z(64), sl(1664, 1696), z(32)], axis=1)


def _w_in_from_p(g):
    sl = lambda a, b: lax.slice_in_dim(g, a, b, axis=1)
    return jnp.concatenate([_unperm_heads64(sl(0, 512), 1), sl(512, 768), _unperm_heads64(sl(768, 1280), 1),
                            sl(1792, 2048), sl(2048, 2176), sl(2240, 2272), sl(1280, 1792)], axis=1)


def _w_uq_to_p(w):
    z = jnp.zeros((w.shape[0], 32), w.dtype)
    return jnp.concatenate([p for h in range(8) for p in (lax.slice_in_dim(w, h * 96, (h + 1) * 96, axis=1), z)], axis=1)


def _w_uq_from_p(g):
    return jnp.concatenate([lax.slice_in_dim(g, h * 128, h * 128 + 96, axis=1) for h in range(8)], axis=1)


def _w_ukv_to_p(w):
    z = jnp.zeros((w.shape[0], 64), w.dtype)
    ks = [p for h in range(8) for p in (lax.slice_in_dim(w, h * 128, h * 128 + 64, axis=1), z)]
    vs = [lax.slice_in_dim(w, h * 128 + 64, (h + 1) * 128, axis=1) for h in range(8)]
    return jnp.concatenate(ks + vs, axis=1)


def _w_ukv_from_p(g):
    parts = []
    for h in range(8):
        parts.append(lax.slice_in_dim(g, h * 128, h * 128 + 64, axis=1))
        parts.append(lax.slice_in_dim(g, 1024 + h * 64, 1024 + (h + 1) * 64, axis=1))
    return jnp.concatenate(parts, axis=1)


def _w_out_to_p(w):
    return jnp.concatenate([_perm_heads64(lax.slice_in_dim(w, 0, 512, axis=0), 0), lax.slice_in_dim(w, 512, 1024, axis=0)], axis=0)


def _w_out_from_p(g):
    return jnp.concatenate([_unperm_heads64(lax.slice_in_dim(g, 0, 512, axis=0), 0), lax.slice_in_dim(g, 512, 1024, axis=0)], axis=0)


def _rope_tables(cfg):
    half = B_ROPE // 2
    length = N_META + cfg.S
    freqs = ROPE_THETA ** (-jnp.arange(half, dtype=jnp.float32) / half)
    ang = jnp.arange(length, dtype=jnp.float32)[:, None] * freqs[None, :]
    cos, sin = jnp.cos(ang), jnp.sin(ang)

    def rows(t):
        return jnp.concatenate([t[:N_META], jnp.zeros((BLK - N_META, t.shape[1]), t.dtype), t[N_META:]], axis=0)

    ones = jnp.ones((length, 64), F32)
    zer = jnp.zeros((length, 32), F32)
    c_tab = rows(jnp.concatenate([ones, cos, cos, zer], axis=1))
    s_tab = rows(jnp.concatenate([zer, zer, -sin, sin, zer], axis=1))
    return c_tab, s_tab


def _inproj_fwd(cfg, h, g, w_p, comm=None):
    tm = 256

    def body(h_ref, g_ref, w_ref, pa_ref, pf_ref):
        xh, _ = _rms(h_ref[...])
        u = (xh * g_ref[...]).astype(BF16)
        acc = _dot(u, w_ref[...])
        pa_ref[:, :512] = (acc[:, :512] * QA_SCALE).astype(BF16)
        pa_ref[:, 512:] = acc[:, 512:768].astype(BF16)
        pf_ref[...] = acc[:, 768:]

    return _call_with_comm(
        body, comm, grid=(cfg.TP // tm,), name="inproj_fwd",
        in_specs=[pl.BlockSpec((tm, D_MODEL), lambda i: (i, 0)), pl.BlockSpec((1, D_MODEL), lambda i: (0, 0)),
                  pl.BlockSpec((D_MODEL, W_IN_P), lambda i: (0, 0))],
        out_specs=[pl.BlockSpec((tm, 768), lambda i: (i, 0)), pl.BlockSpec((tm, 1536), lambda i: (i, 0))],
        out_shape=[jax.ShapeDtypeStruct((cfg.TP, 768), BF16), jax.ShapeDtypeStruct((cfg.TP, 1536), F32)],
        args=(h, g, w_p))


def _lat_fwd(cfg, pf, gq, gkv, wq_p, wkv_p, c_tab, s_tab):
    nj = cfg.NJ

    def body(cq_ref, ckv_ref, kr_ref, gq_ref, gkv_ref, wq_ref, wkv_ref, c_ref, s_ref, q_ref, k_ref, v_ref, kt_ref, vt_ref):
        c1, s1 = c_ref[...], s_ref[...]
        c8, s8 = jnp.tile(c1, (1, 8)), jnp.tile(s1, (1, 8))
        mask_lane = _lane((BLK, 1024)) % BLK == MASK_LANE
        zero_row = (pl.program_id(1) == 0) & (lax.broadcasted_iota(jnp.int32, (BLK, 1024), 0) >= N_META)
        xq, _ = _rms(cq_ref[...])
        qp = _dot((xq * gq_ref[...]).astype(BF16), wq_ref[...])
        q_ref[...] = jnp.where(mask_lane, 1.0, (qp * c8 + _swap_rope(qp) * s8) * Q_SCALE).astype(BF16)
        xk, _ = _rms(ckv_ref[...])
        kvp = _dot((xk * gkv_ref[...]).astype(BF16), wkv_ref[...])
        kr = kr_ref[...]
        krr = kr * c1 + _swap_rope(kr) * s1
        k = jnp.where(mask_lane & zero_row, NEG, kvp[:, :1024] + jnp.tile(krr, (1, 8)))
        k_ref[...] = k.astype(BF16)
        v_ref[...] = kvp[:, 1024:].astype(BF16)
        kt_ref[...] = k.T.astype(BF16)
        vt_ref[...] = kvp[:, 1024:].T.astype(BF16)

    row = lambda b, j: b * nj + j
    return _pallas(
        body, grid=(cfg.B, nj), name="lat_fwd",
        in_specs=[pl.BlockSpec((BLK, 256), lambda b, j: (row(b, j), 4)), pl.BlockSpec((BLK, 128), lambda b, j: (row(b, j), 10)),
                  pl.BlockSpec((BLK, 128), lambda b, j: (row(b, j), 11)),
                  pl.BlockSpec((1, 256), lambda b, j: (0, 0)), pl.BlockSpec((1, 128), lambda b, j: (0, 0)),
                  pl.BlockSpec((256, 1024), lambda b, j: (0, 0)), pl.BlockSpec((128, 1536), lambda b, j: (0, 0)),
                  pl.BlockSpec((BLK, 128), lambda b, j: (j, 0)), pl.BlockSpec((BLK, 128), lambda b, j: (j, 0))],
        out_specs=[pl.BlockSpec((BLK, 1024), lambda b, j: (row(b, j), 0)), pl.BlockSpec((BLK, 1024), lambda b, j: (row(b, j), 0)),
                   pl.BlockSpec((BLK, 512), lambda b, j: (row(b, j), 0)),
                   pl.BlockSpec((1024, BLK), lambda b, j: (b, j)), pl.BlockSpec((512, BLK), lambda b, j: (b, j))],
        out_shape=[jax.ShapeDtypeStruct((cfg.TP, 1024), BF16), jax.ShapeDtypeStruct((cfg.TP, 1024), BF16),
                   jax.ShapeDtypeStruct((cfg.TP, 512), BF16),
                   jax.ShapeDtypeStruct((cfg.B * 1024, cfg.LP), BF16), jax.ShapeDtypeStruct((cfg.B * 512, cfg.LP), BF16)],
        compiler_params=_cp("parallel", "parallel"),
    )(pf, pf, pf, gq, gkv, wq_p, wkv_p, c_tab, s_tab)


def _gate_halves(ya, yb, ga, gb, goa, gob):
    xa, ra = _rms(ya)
    xb, rb = _rms(yb)
    sga, sgb = _sigmoid(ga), _sigmoid(gb)
    return xa, ra, xb, rb, sga, sgb, xa * goa * (ga * sga), xb * gob * (gb * sgb)


def _out_fwd(cfg, ya, yb, pf, goa, gob, wo_p, h):
    tm = 256

    def body(ya_ref, yb_ref, ga_ref, gb_ref, goa_ref, gob_ref, w_ref, h_ref, o_ref):
        *_, y_a, y_b = _gate_halves(ya_ref[...], yb_ref[...], ga_ref[...], gb_ref[...], goa_ref[...], gob_ref[...])
        y = jnp.concatenate([y_a, y_b], axis=1).astype(BF16)
        o_ref[...] = h_ref[...] + _dot(y, w_ref[...])

    return _pallas(
        body, grid=(cfg.TP // tm,), name="out_fwd",
        in_specs=[pl.BlockSpec((tm, 512), lambda i: (i, 0)), pl.BlockSpec((tm, 512), lambda i: (i, 0)),
                  pl.BlockSpec((tm, 512), lambda i: (i, 0)), pl.BlockSpec((tm, 512), lambda i: (i, 1)),
                  pl.BlockSpec((1, 512), lambda i: (0, 0)), pl.BlockSpec((1, 512), lambda i: (0, 0)),
                  pl.BlockSpec((D_MODEL, D_MODEL), lambda i: (0, 0)), pl.BlockSpec((tm, D_MODEL), lambda i: (i, 0))],
        out_specs=pl.BlockSpec((tm, D_MODEL), lambda i: (i, 0)),
        out_shape=jax.ShapeDtypeStruct((cfg.TP, D_MODEL), F32),
        compiler_params=_cp("parallel"),
    )(ya, yb, pf, pf, goa, gob, wo_p, h)


def _bias_build(table, bucket, maskadd, comm=None):
    def body(tab_ref, bk_ref, ma_ref, o_ref):
        def rows(g, carry):
            r = pl.ds(pl.multiple_of(g * 8, 8), 8)
            bk = bk_ref[0, r, :]
            accs = [jnp.zeros(bk.shape, F32)] * A_HEADS
            for b in range(N_BUCKETS):
                hit = bk == b
                accs = [jnp.where(hit, tab_ref[b, h], accs[h]) for h in range(A_HEADS)]
            ma = ma_ref[0, r, :]
            for h in range(A_HEADS):
                o_ref[0, h, r, :] = (accs[h] + ma) * LOG2E
            return carry

        lax.fori_loop(0, BLK // 8, rows, 0)

    return _call_with_comm(
        body, comm, grid=(4,), name="bias_build",
        in_specs=[pl.BlockSpec(memory_space=pltpu.SMEM), pl.BlockSpec((1, BLK, 512), lambda t: (t, 0, 0)),
                  pl.BlockSpec((1, BLK, 512), lambda t: (t, 0, 0))],
        out_specs=[pl.BlockSpec((1, A_HEADS, BLK, 512), lambda t: (t, 0, 0, 0))],
        out_shape=[jax.ShapeDtypeStruct((4, A_HEADS, BLK, 512), F32)],
        args=(table, bucket, maskadd))


def _bias_grad(s_accs, bucket, comm=None):
    depth = len(s_accs)

    def body(*refs):
        s_refs, bk_ref, o_ref, sum_ref, part_ref = refs[:depth], refs[depth], refs[depth + 1], refs[depth + 2], refs[depth + 3]
        t = pl.program_id(0)

        @pl.when(t == 0)
        def _():
            o_ref[...] = jnp.zeros_like(o_ref)

        total = s_refs[0][0]
        for extra in s_refs[1:]:
            total = total + extra[0]
        sum_ref[...] = total

        def step(b, carry):
            accs = [jnp.zeros((8, 512), F32) for _ in range(A_HEADS)]
            for g in range(BLK // 8):
                rows = pl.ds(g * 8, 8)
                hit = bk_ref[0, rows, :] == b
                for h in range(A_HEADS):
                    accs[h] = accs[h] + jnp.where(hit, sum_ref[h, rows, :], 0.0)
            rows8 = jnp.concatenate([jnp.sum(a, axis=0, keepdims=True) for a in accs], axis=0)
            part_ref[pl.ds(pl.multiple_of(b * A_HEADS, 8), A_HEADS), :] = rows8
            return carry

        lax.fori_loop(0, N_BUCKETS, step, 0)
        o_ref[...] += jnp.broadcast_to(jnp.sum(part_ref[...], axis=1, keepdims=True), o_ref.shape)

    s_spec = pl.BlockSpec((1, A_HEADS, BLK, 512), lambda t: (t, 0, 0, 0))
    return _call_with_comm(
        body, comm, grid=(4,), name="bias_grad",
        in_specs=[s_spec] * depth + [pl.BlockSpec((1, BLK, 512), lambda t: (t, 0, 0))],
        out_specs=[pl.BlockSpec((N_BUCKETS * A_HEADS, 128), lambda t: (0, 0))],
        out_shape=[jax.ShapeDtypeStruct((N_BUCKETS * A_HEADS, 128), F32)],
        scratch_shapes=[pltpu.VMEM((A_HEADS, BLK, 512), F32), pltpu.VMEM((N_BUCKETS * A_HEADS, 512), F32)],
        args=(*s_accs, bucket))


def _win_specs(cfg):
    nj = cfg.NJ
    row = lambda b, j: b * nj + j
    jt = lambda j: jnp.where(j == 0, 0, jnp.where(j == 1, 1, jnp.where(j == nj - 1, 3, 2)))
    slot_rows = [lambda b, j: row(b, 0), lambda b, j: row(b, jnp.maximum(j - 1, 0)), lambda b, j: row(b, j),
                 lambda b, j: row(b, jnp.minimum(j + 1, nj - 1))]
    k_specs = [pl.BlockSpec((BLK, 128), functools.partial(lambda b, j, f: (f(b, j), 4), f=f)) for f in slot_rows]
    v_specs = [pl.BlockSpec((BLK, 128), functools.partial(lambda b, j, f: (f(b, j), 5), f=f)) for f in slot_rows]
    q_spec = pl.BlockSpec((BLK, 512), lambda b, j: (row(b, j), 0))
    bias_spec = pl.BlockSpec((1, A_HEADS, BLK, 512), lambda b, j: (jt(j), 0, 0, 0))
    return row, jt, q_spec, k_specs, v_specs, bias_spec


def _stack4(ref):
    return jnp.concatenate([ref[:, c * 128:(c + 1) * 128] for c in range(4)], axis=0)


def _win_keys(k_refs, v_refs):
    k4 = jnp.concatenate([r[...] for r in k_refs], axis=0)
    v4 = jnp.concatenate([r[...] for r in v_refs], axis=0)
    lane_k = _lane(k4.shape)
    return (jnp.where(lane_k < 64, k4, jnp.zeros_like(k4)), jnp.where(lane_k >= 64, k4, jnp.zeros_like(k4))), v4


def _sink_col(sink_ref, hf):
    rowi = lax.broadcasted_iota(jnp.int32, (4 * BLK, 1), 0)
    col = jnp.full((4 * BLK, 1), sink_ref[4 * hf + 3], F32)
    for c in (2, 1, 0):
        col = jnp.where(rowi < (c + 1) * BLK, sink_ref[4 * hf + c], col)
    return col * LOG2E


def _win_fwd(cfg, pa, bias, sink):
    row, jt, q_spec, k_specs, v_specs, bias_spec = _win_specs(cfg)

    def body(sink_ref, q_ref, k0, k1, k2, k3, v0, v1, v2, v3, b_ref, o_ref, lse_ref):
        kk, v4 = _win_keys((k0, k1, k2, k3), (v0, v1, v2, v3))
        qs = _stack4(q_ref)
        lane_o = _lane((4 * BLK, 128))
        outs, lses = [], []
        for hf in range(2):
            s = _dot_nt(qs, kk[hf]) + b_ref[0, 4 * hf:4 * hf + 4].reshape(4 * BLK, 512)
            sink_col = _sink_col(sink_ref, hf)
            m = jnp.maximum(jnp.max(s, axis=1, keepdims=True), sink_col)
            e = jnp.exp2(s - m)
            den = jnp.sum(e, axis=1, keepdims=True) + jnp.exp2(sink_col - m)
            outs.append(_dot(e.astype(BF16), v4) / den)
            lses.append(m + jnp.log2(den))
        o = jnp.where(lane_o < 64, outs[0], outs[1])
        for c in range(4):
            o_ref[:, c * 128:(c + 1) * 128] = o[c * BLK:(c + 1) * BLK]
        lse_ref[...] = jnp.where(lane_o == 0, lses[0], jnp.where(lane_o == 1, lses[1], 0.0))

    return _pallas(
        body, grid=(cfg.B, cfg.NJ), name="win_fwd",
        in_specs=[pl.BlockSpec(memory_space=pltpu.SMEM), q_spec, *k_specs, *v_specs, bias_spec],
        out_specs=[pl.BlockSpec((BLK, 512), lambda b, j: (row(b, j), 0)), pl.BlockSpec((4 * BLK, 128), lambda b, j: (row(b, j), 0))],
        out_shape=[jax.ShapeDtypeStruct((cfg.TP, 512), F32), jax.ShapeDtypeStruct((4 * cfg.TP, 128), F32)],
        compiler_params=_cp("parallel", "parallel"),
    )(sink, pa, *([pa] * 8), bias)


def _win_bwd(cfg, pa, bias, sink, dya, ya, lse):
    row, jt, q_spec, k_specs, v_specs, bias_spec = _win_specs(cfg)
    nj = cfg.NJ

    def body(sink_ref, q_ref, k0, k1, k2, k3, v0, v1, v2, v3, b_ref, dy_ref, y_ref, lse_ref,
             dq_ref, dkp_ref, dvp_ref, dkm_ref, dvm_ref, s_ref, dsink_ref):
        j = pl.program_id(1)
        kind = jt(j)

        @pl.when((pl.program_id(0) == 0) & (j == 0))
        def _():
            s_ref[...] = jnp.zeros_like(s_ref)

        kk, v4 = _win_keys((k0, k1, k2, k3), (v0, v1, v2, v3))
        qs, dys, ys = _stack4(q_ref), _stack4(dy_ref), _stack4(y_ref)
        lane_o = _lane((4 * BLK, 128))
        half = (lane_o < 64, lane_o >= 64)
        lse_blk = lse_ref[...]
        dq = jnp.zeros((4 * BLK, 128), F32)
        dk4 = jnp.zeros((512, 128), F32)
        dv4 = jnp.zeros((512, 128), F32)
        dsink = jnp.zeros((8, 128), F32)
        lane_s = _lane((8, 128))
        row_s = lax.broadcasted_iota(jnp.int32, (8, 128), 0)
        for hf in range(2):
            lse_h = jnp.sum(jnp.where(lane_o == hf, lse_blk, 0.0), axis=1, keepdims=True)
            s = _dot_nt(qs, kk[hf]) + b_ref[0, 4 * hf:4 * hf + 4].reshape(4 * BLK, 512)
            p = jnp.exp2(s - lse_h)
            do_h = jnp.where(half[hf], dys, 0.0)
            delta = jnp.sum(do_h * ys, axis=1, keepdims=True)
            do_b = do_h.astype(BF16)
            ds = p * (_dot_nt(do_b, v4) - delta)
            s_ref[kind, 4 * hf:4 * hf + 4] += ds.reshape(4, BLK, 512)
            sink_grad = jnp.exp2(_sink_col(sink_ref, hf) - lse_h) * delta
            for c in range(4):
                tot = -jnp.sum(sink_grad[c * BLK:(c + 1) * BLK])
                dsink = jnp.where((row_s == 0) & (lane_s == 4 * hf + c), tot, dsink)
            dsb = (ds * LN2).astype(BF16)
            dq = dq + _dot(dsb, kk[hf])
            dk4 = dk4 + _dot_tn(dsb, jnp.where(half[hf], qs, jnp.zeros_like(qs)))
            dv4 = dv4 + _dot_tn(p.astype(BF16), do_b)
        for c in range(4):
            dq_ref[:, c * 128:(c + 1) * 128] = (dq[c * BLK:(c + 1) * BLK] * QA_SCALE).astype(BF16)
        dkp_ref[0] = dk4
        dvp_ref[0] = dv4

        @pl.when(j == 0)
        def _():
            dkm_ref[...] = dk4[:BLK]
            dvm_ref[...] = dv4[:BLK]

        @pl.when(j > 0)
        def _():
            dkm_ref[...] += dk4[:BLK]
            dvm_ref[...] += dv4[:BLK]

        @pl.when((pl.program_id(0) == 0) & (j == 0))
        def _():
            dsink_ref[...] = dsink

        @pl.when((pl.program_id(0) > 0) | (j > 0))
        def _():
            dsink_ref[...] += dsink

    blk_row = pl.BlockSpec((BLK, 512), lambda b, j: (row(b, j), 0))
    return _pallas(
        body, grid=(cfg.B, nj), name="win_bwd",
        in_specs=[pl.BlockSpec(memory_space=pltpu.SMEM), q_spec, *k_specs, *v_specs, bias_spec, blk_row, blk_row,
                  pl.BlockSpec((4 * BLK, 128), lambda b, j: (row(b, j), 0))],
        out_specs=[blk_row,
                   pl.BlockSpec((1, 512, 128), lambda b, j: (row(b, j), 0, 0)), pl.BlockSpec((1, 512, 128), lambda b, j: (row(b, j), 0, 0)),
                   pl.BlockSpec((BLK, 128), lambda b, j: (b, 0)), pl.BlockSpec((BLK, 128), lambda b, j: (b, 0)),
                   pl.BlockSpec((4, A_HEADS, BLK, 512), lambda b, j: (0, 0, 0, 0)),
                   pl.BlockSpec((8, 128), lambda b, j: (0, 0))],
        out_shape=[jax.ShapeDtypeStruct((cfg.TP, 512), BF16),
                   jax.ShapeDtypeStruct((cfg.B * nj, 512, 128), F32), jax.ShapeDtypeStruct((cfg.B * nj, 512, 128), F32),
                   jax.ShapeDtypeStruct((cfg.B * BLK, 128), F32), jax.ShapeDtypeStruct((cfg.B * BLK, 128), F32),
                   jax.ShapeDtypeStruct((4, A_HEADS, BLK, 512), F32),
                   jax.ShapeDtypeStruct((8, 128), F32)],
        compiler_params=_cp("arbitrary", "arbitrary"),
    )(sink, pa, *([pa] * 8), bias, dya, ya, lse)


def _win_dkv_combine(cfg, dkp, dvp, dkm, dvm):
    nj = cfg.NJ

    def body(kp, vp, km, vm, o_ref):
        o_ref[:BLK, :128] = km[...].astype(BF16)
        o_ref[:BLK, 128:] = vm[...].astype(BF16)
        for kb in range(1, nj):
            for col, part in ((0, kp), (128, vp)):
                tot = part[kb, 2 * BLK:3 * BLK] + part[kb - 1, 3 * BLK:4 * BLK]
                if kb + 1 < nj:
                    tot = tot + part[kb + 1, BLK:2 * BLK]
                o_ref[kb * BLK:(kb + 1) * BLK, col:col + 128] = tot.astype(BF16)

    return _pallas(
        body, grid=(cfg.B,), name="win_dkv_combine",
        in_specs=[pl.BlockSpec((nj, 512, 128), lambda b: (b, 0, 0)), pl.BlockSpec((nj, 512, 128), lambda b: (b, 0, 0)),
                  pl.BlockSpec((BLK, 128), lambda b: (b, 0)), pl.BlockSpec((BLK, 128), lambda b: (b, 0))],
        out_specs=pl.BlockSpec((cfg.LP, 256), lambda b: (b, 0)),
        out_shape=jax.ShapeDtypeStruct((cfg.TP, 256), BF16),
        compiler_params=_cp("parallel"),
    )(dkp, dvp, dkm, dvm)


def _pair_blockdiag(q):
    lane = _lane(q.shape)
    return jnp.concatenate([jnp.where(lane < 128, q, jnp.zeros_like(q)), jnp.where(lane >= 128, q, jnp.zeros_like(q))], axis=0)


def _mla_fwd(cfg, q, kt, v, comm=None):
    nj, lp = cfg.NJ, cfg.LP

    def body(q_ref, kt_ref, v_ref, o_ref, lse_ref, s_even, s_odd):
        i = pl.program_id(2)
        lane_o = _lane((BLK, 128))

        def logits(s_write):
            s_write[...] = _dot(_pair_blockdiag(q_ref[...]), kt_ref[...])

        def finish(s_read):
            s = s_read[...]
            m = jnp.max(s, axis=1, keepdims=True)
            e = jnp.exp2(s - m)
            den = jnp.sum(e, axis=1, keepdims=True)
            pv = _dot(e.astype(BF16), v_ref[...]) / den
            o_ref[...] = jnp.where(lane_o < 64, pv[:BLK], pv[BLK:])
            lse_ref[0] = jnp.broadcast_to(m + jnp.log2(den), (2 * BLK, 128))

        odd = i % 2 == 1

        @pl.when(i == 0)
        def _():
            logits(s_even)

        @pl.when(odd & (i < nj))
        def _():
            logits(s_odd)
            finish(s_even)

        @pl.when(jnp.logical_not(odd) & (i > 0) & (i < nj))
        def _():
            logits(s_even)
            finish(s_odd)

        @pl.when(i == nj)
        def _():
            finish(s_even if nj % 2 == 1 else s_odd)

    cur = lambda b, i: b * nj + jnp.minimum(i, nj - 1)
    prev = lambda b, i: b * nj + jnp.maximum(i - 1, 0)
    return _call_with_comm(
        body, comm, grid=(cfg.B, 4, nj + 1), name="mla_fwd",
        in_specs=[pl.BlockSpec((BLK, 256), lambda b, p, i: (cur(b, i), p)), pl.BlockSpec((256, lp), lambda b, p, i: (b * 4 + p, 0)),
                  pl.BlockSpec((lp, 128), lambda b, p, i: (b, p))],
        out_specs=[pl.BlockSpec((BLK, 128), lambda b, p, i: (prev(b, i), p)),
                   pl.BlockSpec((1, 2 * BLK, 128), lambda b, p, i: (p, prev(b, i), 0))],
        out_shape=[jax.ShapeDtypeStruct((cfg.TP, 512), F32), jax.ShapeDtypeStruct((4, 2 * cfg.TP, 128), F32)],
        scratch_shapes=[pltpu.VMEM((2 * BLK, lp), F32), pltpu.VMEM((2 * BLK, lp), F32)],
        args=(q, kt, v))


def _mla_bwd(cfg, q, k, kt, vt, dyb, yb, lse, comm=None):
    nj, lp = cfg.NJ, cfg.LP

    def body(q_ref, k_ref, kt_ref, vt_ref, dy_ref, y_ref, lse_ref, dq_ref, dk_ref, dv_ref):
        i = pl.program_id(2)

        @pl.when(i == 0)
        def _():
            dk_ref[...] = jnp.zeros_like(dk_ref)
            dv_ref[...] = jnp.zeros_like(dv_ref)

        lane_o = _lane((BLK, 128))
        qbd = _pair_blockdiag(q_ref[...])
        dy, y = dy_ref[...], y_ref[...]
        do_s = jnp.concatenate([jnp.where(lane_o < 64, dy, 0.0), jnp.where(lane_o >= 64, dy, 0.0)], axis=0)
        delta = jnp.sum(do_s * jnp.concatenate([y, y], axis=0), axis=1, keepdims=True)
        do_b = do_s.astype(BF16)
        p = jnp.exp2(_dot(qbd, kt_ref[...]) - lse_ref[0][:, :1])
        ds = p * (_dot(do_b, vt_ref[...]) - delta)
        dsb = (ds * LN2).astype(BF16)
        dq2 = _dot(dsb, k_ref[...])
        dq_ref[...] = jnp.where(_lane((BLK, 256)) < 128, dq2[:BLK], dq2[BLK:]) * Q_SCALE
        dk_ref[...] += _dot_tn(dsb, qbd)
        dv_ref[...] += _dot_tn(p.astype(BF16), do_b)

    return _call_with_comm(
        body, comm, grid=(cfg.B, 4, nj), name="mla_bwd",
        in_specs=[pl.BlockSpec((BLK, 256), lambda b, p, i: (b * nj + i, p)), pl.BlockSpec((lp, 256), lambda b, p, i: (b, p)),
                  pl.BlockSpec((256, lp), lambda b, p, i: (b * 4 + p, 0)), pl.BlockSpec((128, lp), lambda b, p, i: (b * 4 + p, 0)),
                  pl.BlockSpec((BLK, 128), lambda b, p, i: (b * nj + i, p)), pl.BlockSpec((BLK, 128), lambda b, p, i: (b * nj + i, p)),
                  pl.BlockSpec((1, 2 * BLK, 128), lambda b, p, i: (p, b * nj + i, 0))],
        out_specs=[pl.BlockSpec((BLK, 256), lambda b, p, i: (b * nj + i, p)), pl.BlockSpec((lp, 256), lambda b, p, i: (b, p)),
                   pl.BlockSpec((lp, 128), lambda b, p, i: (b, p))],
        out_shape=[jax.ShapeDtypeStruct((cfg.TP, 1024), F32), jax.ShapeDtypeStruct((cfg.TP, 1024), F32),
                   jax.ShapeDtypeStruct((cfg.TP, 512), F32)],
        args=(q, k, kt, vt, dyb, yb, lse))


def _loss_bwd(cfg, h, target, gf):
    nj, nb = cfg.NJ, cfg.NB
    tm = 2 * BLK

    def target_block(g):
        return (g // nj) * nb + jnp.maximum(g % nj - 1, 0)

    def body(h_ref, ta_ref, tb_ref, g_ref, dh_ref, loss_ref, dg_ref):
        t = pl.program_id(0)

        @pl.when(t == 0)
        def _():
            loss_ref[...] = jnp.zeros_like(loss_ref)
            dg_ref[...] = jnp.zeros_like(dg_ref)

        g = g_ref[...]
        first = (lax.broadcasted_iota(jnp.int32, (8, 128), 0) == 0) & (_lane((8, 128)) == 0)
        for half, t_ref in enumerate((ta_ref, tb_ref)):
            rows = slice(half * BLK, (half + 1) * BLK)
            real = (2 * t + half) % nj > 0
            xh, r = _rms(h_ref[rows, :])
            err = jnp.where(real, xh * g - t_ref[...], 0.0)
            loss_ref[...] += jnp.where(first, (0.5 / D_MODEL) * jnp.sum(err * err), 0.0)
            dy = err * (1.0 / D_MODEL)
            dg_ref[...] += jnp.sum(dy * xh, axis=0, keepdims=True)
            dh_ref[rows, :] = _rms_bwd(xh, r, dy * g)

    return _pallas(
        body, grid=(cfg.TP // tm,), name="loss_bwd",
        in_specs=[pl.BlockSpec((tm, D_MODEL), lambda t: (t, 0)),
                  pl.BlockSpec((BLK, D_MODEL), lambda t: (target_block(2 * t), 0)),
                  pl.BlockSpec((BLK, D_MODEL), lambda t: (target_block(2 * t + 1), 0)),
                  pl.BlockSpec((1, D_MODEL), lambda t: (0, 0))],
        out_specs=[pl.BlockSpec((tm, D_MODEL), lambda t: (t, 0)), pl.BlockSpec((8, 128), lambda t: (0, 0)),
                   pl.BlockSpec((1, D_MODEL), lambda t: (0, 0))],
        out_shape=[jax.ShapeDtypeStruct((cfg.TP, D_MODEL), F32), jax.ShapeDtypeStruct((8, 128), F32),
                   jax.ShapeDtypeStruct((1, D_MODEL), F32)],
        compiler_params=_cp("arbitrary"),
    )(h, target, target, gf)


def _out_bwd(cfg, dh, ya, yb, pf, goa, gob, wo_p):
    tm = 256

    def body(dh_ref, ya_ref, yb_ref, ga_ref, gb_ref, goa_ref, gob_ref, w_ref,
             dya_ref, dyb_ref, dg_ref, dw_ref, dgoa_ref, dgob_ref):
        @pl.when(pl.program_id(0) == 0)
        def _():
            dw_ref[...] = jnp.zeros_like(dw_ref)
            dgoa_ref[...] = jnp.zeros_like(dgoa_ref)
            dgob_ref[...] = jnp.zeros_like(dgob_ref)

        ga, gb, goa, gob = ga_ref[...], gb_ref[...], goa_ref[...], gob_ref[...]
        xa, ra, xb, rb, sga, sgb, y_a, y_b = _gate_halves(ya_ref[...], yb_ref[...], ga, gb, goa, gob)
        dhb = dh_ref[...].astype(BF16)
        dw_ref[...] += _dot_tn(jnp.concatenate([y_a, y_b], axis=1).astype(BF16), dhb)
        dy = _dot_nt(dhb, w_ref[...])
        for (dyh, x, r, g, sg, go, dy_out, dgo_ref, col) in (
                (dy[:, :512], xa, ra, ga, sga, goa, dya_ref, dgoa_ref, 0), (dy[:, 512:], xb, rb, gb, sgb, gob, dyb_ref, dgob_ref, 512)):
            dn = dyh * (g * sg)
            dg_ref[:, col:col + 512] = (dyh * (x * go) * (sg * (1.0 + g * (1.0 - sg)))).astype(BF16)
            dgo_ref[...] += jnp.sum(dn * x, axis=0, keepdims=True)
            dy_out[...] = _rms_bwd(x, r, dn * go)

    half = lambda c: pl.BlockSpec((tm, 512), lambda i: (i, c))
    vec = pl.BlockSpec((1, 512), lambda i: (0, 0))
    return _pallas(
        body, grid=(cfg.TP // tm,), name="out_bwd",
        in_specs=[pl.BlockSpec((tm, D_MODEL), lambda i: (i, 0)), half(0), half(0), half(0), half(1), vec, vec,
                  pl.BlockSpec((D_MODEL, D_MODEL), lambda i: (0, 0))],
        out_specs=[half(0), half(0), pl.BlockSpec((tm, D_MODEL), lambda i: (i, 0)),
                   pl.BlockSpec((D_MODEL, D_MODEL), lambda i: (0, 0)), vec, vec],
        out_shape=[jax.ShapeDtypeStruct((cfg.TP, 512), F32), jax.ShapeDtypeStruct((cfg.TP, 512), F32),
                   jax.ShapeDtypeStruct((cfg.TP, D_MODEL), BF16), jax.ShapeDtypeStruct((D_MODEL, D_MODEL), F32),
                   jax.ShapeDtypeStruct((1, 512), F32), jax.ShapeDtypeStruct((1, 512), F32)],
        compiler_params=_cp("arbitrary"),
    )(dh, ya, yb, pf, pf, goa, gob, wo_p)


def _lat_bwd(cfg, dq, dk, dv, pf, gq, gkv, wq_p, wkv_p, c_tab, s_tab):
    nj = cfg.NJ

    def body(dq_ref, dk_ref, dv_ref, cq_ref, ckv_ref, gq_ref, gkv_ref, wq_ref, wkv_ref, c_ref, s_ref,
             dl_ref, dwq_ref, dwkv_ref, dgq_ref, dgkv_ref):
        @pl.when((pl.program_id(0) == 0) & (pl.program_id(1) == 0))
        def _():
            dwq_ref[...] = jnp.zeros_like(dwq_ref)
            dwkv_ref[...] = jnp.zeros_like(dwkv_ref)
            dgq_ref[...] = jnp.zeros_like(dgq_ref)
            dgkv_ref[...] = jnp.zeros_like(dgkv_ref)

        c1, s1 = c_ref[...], s_ref[...]
        c8, s8 = jnp.tile(c1, (1, 8)), jnp.tile(s1, (1, 8))
        dq_r = dq_ref[...]
        dqp = (dq_r * c8 + _swap_rope(dq_r * s8)).astype(BF16)
        gq = gq_ref[...]
        xq, rq = _rms(cq_ref[...])
        dwq_ref[...] += _dot_tn((xq * gq).astype(BF16), dqp)
        dn = _dot_nt(dqp, wq_ref[...])
        dgq_ref[...] += jnp.sum(dn * xq, axis=0, keepdims=True)
        dl_ref[:, :256] = _rms_bwd(xq, rq, dn * gq).astype(BF16)

        dk_r = dk_ref[...]
        dkr = dk_r[:, :128]
        for hd in range(1, 8):
            dkr = dkr + dk_r[:, hd * 128:(hd + 1) * 128]
        lane1 = _lane(dkr.shape)
        dkr = jnp.where((lane1 >= 64) & (lane1 < 96), dkr, 0.0)
        dl_ref[:, 384:] = (dkr * c1 + _swap_rope(dkr * s1)).astype(BF16)
        dkv = jnp.concatenate([dk_r, dv_ref[...]], axis=1).astype(BF16)
        gkv = gkv_ref[...]
        xk, rk = _rms(ckv_ref[...])
        dwkv_ref[...] += _dot_tn((xk * gkv).astype(BF16), dkv)
        dn2 = _dot_nt(dkv, wkv_ref[...])
        dgkv_ref[...] += jnp.sum(dn2 * xk, axis=0, keepdims=True)
        dl_ref[:, 256:384] = _rms_bwd(xk, rk, dn2 * gkv).astype(BF16)

    row = lambda b, j: b * nj + j
    const = lambda shape: pl.BlockSpec(shape, lambda b, j: (0, 0))
    return _pallas(
        body, grid=(cfg.B, nj), name="lat_bwd",
        in_specs=[pl.BlockSpec((BLK, 1024), lambda b, j: (row(b, j), 0)), pl.BlockSpec((BLK, 1024), lambda b, j: (row(b, j), 0)),
                  pl.BlockSpec((BLK, 512), lambda b, j: (row(b, j), 0)),
                  pl.BlockSpec((BLK, 256), lambda b, j: (row(b, j), 4)), pl.BlockSpec((BLK, 128), lambda b, j: (row(b, j), 10)),
                  const((1, 256)), const((1, 128)), const((256, 1024)), const((128, 1536)),
                  pl.BlockSpec((BLK, 128), lambda b, j: (j, 0)), pl.BlockSpec((BLK, 128), lambda b, j: (j, 0))],
        out_specs=[pl.BlockSpec((BLK, 512), lambda b, j: (row(b, j), 0)), const((256, 1024)), const((128, 1536)),
                   const((1, 256)), const((1, 128))],
        out_shape=[jax.ShapeDtypeStruct((cfg.TP, 512), BF16), jax.ShapeDtypeStruct((256, 1024), F32),
                   jax.ShapeDtypeStruct((128, 1536), F32), jax.ShapeDtypeStruct((1, 256), F32), jax.ShapeDtypeStruct((1, 128), F32)],
        compiler_params=_cp("arbitrary", "arbitrary"),
    )(dq, dk, dv, pf, pf, gq, gkv, wq_p, wkv_p, c_tab, s_tab)


def _inproj_bwd(cfg, h, g, w_p, dqa, dkva, dgate, dlat, dh, comm=None):
    tm = 256

    def body(h_ref, g_ref, w_ref, dqa_ref, dkva_ref, dg_ref, dl_ref, dh_ref, o_ref, dw_ref, dgn_ref):
        @pl.when(pl.program_id(0) == 0)
        def _():
            dw_ref[...] = jnp.zeros_like(dw_ref)
            dgn_ref[...] = jnp.zeros_like(dgn_ref)

        g = g_ref[...]
        xh, r = _rms(h_ref[...])
        dproj = jnp.concatenate([dqa_ref[...], dkva_ref[...], dg_ref[...], dl_ref[...]], axis=1)
        dw_ref[...] += _dot_tn((xh * g).astype(BF16), dproj)
        du = _dot_nt(dproj, w_ref[...])
        dgn_ref[...] += jnp.sum(du * xh, axis=0, keepdims=True)
        o_ref[...] = dh_ref[...] + _rms_bwd(xh, r, du * g)

    rows = lambda w: pl.BlockSpec((tm, w), lambda i: (i, 0))
    return _call_with_comm(
        body, comm, grid=(cfg.TP // tm,), name="inproj_bwd",
        in_specs=[rows(D_MODEL), pl.BlockSpec((1, D_MODEL), lambda i: (0, 0)), pl.BlockSpec((D_MODEL, W_IN_P), lambda i: (0, 0)),
                  rows(512), rows(256), rows(1024), rows(512), rows(D_MODEL)],
        out_specs=[rows(D_MODEL), pl.BlockSpec((D_MODEL, W_IN_P), lambda i: (0, 0)), pl.BlockSpec((1, D_MODEL), lambda i: (0, 0))],
        out_shape=[jax.ShapeDtypeStruct((cfg.TP, D_MODEL), F32), jax.ShapeDtypeStruct((D_MODEL, W_IN_P), F32),
                   jax.ShapeDtypeStruct((1, D_MODEL), F32)],
        args=(h, g, w_p, dqa, dkva, dgate, dlat, dh))


def _meta_grad(cfg, dh):
    def body(d_ref, o_ref):
        @pl.when(pl.program_id(0) == 0)
        def _():
            o_ref[...] = d_ref[...]

        @pl.when(pl.program_id(0) > 0)
        def _():
            o_ref[...] += d_ref[...]

    return _pallas(
        body, grid=(cfg.B,), name="meta_grad",
        in_specs=[pl.BlockSpec((BLK, D_MODEL), lambda b: (b * cfg.NJ, 0))],
        out_specs=pl.BlockSpec((BLK, D_MODEL), lambda b: (0, 0)),
        out_shape=jax.ShapeDtypeStruct((BLK, D_MODEL), F32),
        compiler_params=_cp("arbitrary"),
    )(dh)


MATRICES = ("w_in", "w_uq", "w_ukv", "w_out")


def _local_grads(cfg, x, target, meta_of, table, small, weight_of, rider=None):
    def ride(stage, i, mats):
        hook = rider(stage, i, mats) if rider else None
        return hook if hook else (None, lambda res: None)

    depth = small["norm_in"].shape[0]
    rel, vis = _window_structure(cfg.NJ)
    bucket = _t5_bucket(jnp.asarray(rel))
    maskadd = jnp.asarray(np.where(vis, 0.0, NEG).astype(np.float32))
    c_tab, s_tab = _rope_tables(cfg)
    comm, deliver = ride("bias_build", 0, {})
    bias, *travelled = _bias_build(table, bucket, maskadd, comm)
    deliver(travelled)

    meta_blk = jnp.concatenate([meta_of(), jnp.zeros((BLK - N_META, D_MODEL), F32)], axis=0)
    h = jnp.concatenate([jnp.broadcast_to(meta_blk[None], (cfg.B, BLK, D_MODEL)), x], axis=1).reshape(cfg.TP, D_MODEL)

    wp, saved = [], []
    for i in range(depth):
        w = dict(w_in=_w_in_to_p(weight_of(i, "w_in")),
                 g_in=small["norm_in"][i][None], gq=small["norm_q_lat"][i][None], gkv=small["norm_kv_lat"][i][None],
                 goa=_perm_heads64(small["norm_out_a"][i], 0)[None], gob=small["norm_out_b"][i][None], sink=small["sink_a"][i])
        wp.append(w)
        comm, deliver = ride("inproj_fwd", i, {})
        pa, pf, *travelled = _inproj_fwd(cfg, h, w["g_in"], w["w_in"], comm)
        deliver(travelled)
        w.update(w_uq=_w_uq_to_p(weight_of(i, "w_uq")), w_ukv=_w_ukv_to_p(weight_of(i, "w_ukv")), w_out=_w_out_to_p(weight_of(i, "w_out")))
        q, k, v, kt, vt = _lat_fwd(cfg, pf, w["gq"], w["gkv"], w["w_uq"], w["w_ukv"], c_tab, s_tab)
        ya, lse_a = _win_fwd(cfg, pa, bias, w["sink"])
        comm, deliver = ride("mla_fwd", i, {})
        yb, lse_b, *travelled = _mla_fwd(cfg, q, kt, v, comm)
        deliver(travelled)
        h_next = _out_fwd(cfg, ya, yb, pf, w["goa"], w["gob"], w["w_out"], h)
        saved.append(dict(h=h, pa=pa, pf=pf, q=q, k=k, kt=kt, vt=vt, ya=ya, lse_a=lse_a, yb=yb, lse_b=lse_b))
        h = h_next

    dh, loss_tile, d_norm_final = _loss_bwd(cfg, h, target.reshape(cfg.B * cfg.S, D_MODEL), small["norm_final"][None])

    grads = {k_: [] for k_ in ("norm_in", "sink_a", "norm_q_lat", "norm_kv_lat", "norm_out_a", "norm_out_b")}
    mats, s_accs = {}, []
    for i in reversed(range(depth)):
        w, sv = wp[i], saved[i]
        dya, dyb, dgate, dwo, dgoa, dgob = _out_bwd(cfg, dh, sv["ya"], sv["yb"], sv["pf"], w["goa"], w["gob"], w["w_out"])
        dqa, dkp, dvp, dkm, dvm, s_acc, dsink = _win_bwd(cfg, sv["pa"], bias, w["sink"], dya, sv["ya"], sv["lse_a"])
        dkva = _win_dkv_combine(cfg, dkp, dvp, dkm, dvm)
        comm, deliver = ride("mla_bwd", i, mats)
        dq, dk, dv, *travelled = _mla_bwd(cfg, sv["q"], sv["k"], sv["kt"], sv["vt"], dyb, sv["yb"], sv["lse_b"], comm)
        deliver(travelled)
        dlat, dwq, dwkv, dgq, dgkv = _lat_bwd(cfg, dq, dk, dv, sv["pf"], w["gq"], w["gkv"], w["w_uq"], w["w_ukv"], c_tab, s_tab)
        mats[i] = dict(w_uq=_w_uq_from_p(dwq), w_ukv=_w_ukv_from_p(dwkv), w_out=_w_out_from_p(dwo))
        comm, deliver = ride("inproj_bwd", i, mats)
        dh, dwin, dgin, *travelled = _inproj_bwd(cfg, sv["h"], w["g_in"], w["w_in"], dqa, dkva, dgate, dlat, dh, comm)
        deliver(travelled)
        s_accs.append(s_acc)
        mats[i]["w_in"] = _w_in_from_p(dwin)
        grads["norm_in"].append(dgin[0])
        grads["sink_a"].append(dsink[0, :A_HEADS])
        grads["norm_q_lat"].append(dgq[0])
        grads["norm_kv_lat"].append(dgkv[0])
        grads["norm_out_a"].append(_unperm_heads64(dgoa[0], 0))
        grads["norm_out_b"].append(dgob[0])

    out = {k_: jnp.stack(v_[::-1]) for k_, v_ in grads.items()}
    mats["meta_tokens"] = _meta_grad(cfg, dh)[:N_META]
    comm, deliver = ride("bias_grad", 0, mats)
    dtable, *travelled = _bias_grad(s_accs, bucket, comm)
    deliver(travelled)
    out["rel_bias_table"] = dtable[:, 0].reshape(N_BUCKETS, A_HEADS)
    out["norm_final"] = d_norm_final[0]
    return loss_tile[0, 0], dh.reshape(cfg.B, cfg.LP, D_MODEL)[:, BLK:], out, mats


MESH = pl.DeviceIdType.MESH
ANY = pl.BlockSpec(memory_space=pl.ANY)


def _place():
    x, y, c = lax.axis_index("x"), lax.axis_index("y"), lax.axis_index("c")
    others = [(1 - x, y), (x, 1 - y), (1 - x, 1 - y)]
    return x, y, c, others


Comm = collections.namedtuple("Comm", "inputs out_shapes scratch start wait")


def _gather_comm(shards):
    n = len(shards)

    def copies(ins, outs, sems, arriving):
        send_sems, recv_sems, local_sems = sems
        x, y, c, others = _place()
        k_me = 2 * x + y
        local = [pltpu.make_async_copy(ins[a], outs[a].at[k_me], local_sems.at[a]) for a in range(n)]
        remote = [pltpu.make_async_remote_copy(src_ref=ins[a], dst_ref=outs[a].at[2 * ox + oy if arriving else k_me],
                                               send_sem=send_sems.at[3 * a + j], recv_sem=recv_sems.at[3 * a + j],
                                               device_id=(ox, oy, c), device_id_type=MESH)
                  for a in range(n) for j, (ox, oy) in enumerate(others)]
        return local, remote

    def start(ins, outs, sems):
        local, sends = copies(ins, outs, sems, arriving=False)
        for cp in local + sends:
            cp.start()

    def wait(ins, outs, sems):
        local, recvs = copies(ins, outs, sems, arriving=True)
        for cp in recvs:
            cp.wait_recv()
        for cp in recvs:
            cp.wait_send()
        for cp in local:
            cp.wait()

    return Comm(list(shards), [jax.ShapeDtypeStruct((4, *s.shape), s.dtype) for s in shards],
                [pltpu.SemaphoreType.DMA((3 * n,)), pltpu.SemaphoreType.DMA((3 * n,)), pltpu.SemaphoreType.DMA((n,))], start, wait)


def _gather_halves_comm(shards):
    n = len(shards)

    def copies(ins, outs, sems, kind):
        send_sems, recv_sems, fwd_send_sems, fwd_recv_sems, local_sems = sems
        x, y, c, others = _place()
        k_me = 2 * x + y

        def half(ref, which):
            rows = ref.shape[0] // 2
            return ref.at[pl.ds(pl.multiple_of(which * rows, 8), rows)]

        if kind == "local":
            return [pltpu.make_async_copy(ins[a], outs[a].at[k_me], local_sems.at[a]) for a in range(n)]
        made = []
        for a in range(n):
            for j, (ox, oy) in enumerate(others):
                slot = outs[a].at[k_me if kind == "sent" else 2 * ox + oy]
                if kind in ("sent", "arrived"):
                    made.append(pltpu.make_async_remote_copy(
                        src_ref=half(ins[a], c), dst_ref=half(slot, c), send_sem=send_sems.at[3 * a + j],
                        recv_sem=recv_sems.at[3 * a + j], device_id=(ox, oy, c), device_id_type=MESH))
                else:
                    which = c if kind == "forward" else 1 - c
                    made.append(pltpu.make_async_remote_copy(
                        src_ref=half(slot, which), dst_ref=half(slot, which), send_sem=fwd_send_sems.at[3 * a + j],
                        recv_sem=fwd_recv_sems.at[3 * a + j], device_id=(x, y, 1 - c), device_id_type=MESH))
        return made

    def start(ins, outs, sems):
        for cp in copies(ins, outs, sems, "local") + copies(ins, outs, sems, "sent"):
            cp.start()

    def wait(ins, outs, sems):
        arrived, forward = copies(ins, outs, sems, "arrived"), copies(ins, outs, sems, "forward")
        for came, on in zip(arrived, forward):
            came.wait_recv()
            on.start()
        for cp in copies(ins, outs, sems, "forwarded"):
            cp.wait_recv()
        for cp in arrived + forward:
            cp.wait_send()
        for cp in copies(ins, outs, sems, "local"):
            cp.wait()

    return Comm(list(shards), [jax.ShapeDtypeStruct((4, *s.shape), s.dtype) for s in shards],
                [pltpu.SemaphoreType.DMA((3 * n,))] * 4 + [pltpu.SemaphoreType.DMA((n,))], start, wait)


def _scatter_comm(parts):
    n = len(parts)

    def copies(ins, outs, sems):
        send_sems, recv_sems = sems
        x, y, c, others = _place()
        return [pltpu.make_async_remote_copy(src_ref=ins[a].at[2 * ox + oy], dst_ref=outs[a].at[j], send_sem=send_sems.at[3 * a + j],
                                             recv_sem=recv_sems.at[3 * a + j], device_id=(ox, oy, c), device_id_type=MESH)
                for a in range(n) for j, (ox, oy) in enumerate(others)]

    def start(ins, outs, sems):
        for cp in copies(ins, outs, sems):
            cp.start()

    def wait(ins, outs, sems):
        cps = copies(ins, outs, sems)
        for cp in cps:
            cp.wait_recv()
        for cp in cps:
            cp.wait_send()

    return Comm(list(parts), [jax.ShapeDtypeStruct((3, *p.shape[1:]), p.dtype) for p in parts],
                [pltpu.SemaphoreType.DMA((3 * n,)), pltpu.SemaphoreType.DMA((3 * n,))], start, wait)


def _call_with_comm(body, comm, *, grid, name, in_specs, out_specs, out_shape, args, scratch_shapes=()):
    if comm is None:
        return _pallas(body, grid=grid, name=name, in_specs=in_specs, out_specs=out_specs, out_shape=out_shape,
                              scratch_shapes=list(scratch_shapes), compiler_params=_cp(*["arbitrary"] * len(grid)))(*args)
    n_in, n_out, ci, co, ns = len(in_specs), len(out_specs), len(comm.inputs), len(comm.out_shapes), len(scratch_shapes)

    def wrapped(*refs):
        ins, cins = refs[:n_in], refs[n_in:n_in + ci]
        outs, couts = refs[n_in + ci:n_in + ci + n_out], refs[n_in + ci + n_out:n_in + ci + n_out + co]
        scratch, sems = refs[n_in + ci + n_out + co:n_in + ci + n_out + co + ns], refs[n_in + ci + n_out + co + ns:]
        ids = [pl.program_id(a) for a in range(len(grid))]
        first = functools.reduce(jnp.logical_and, [i == 0 for i in ids])
        last = functools.reduce(jnp.logical_and, [i == g - 1 for i, g in zip(ids, grid)])

        @pl.when(first)
        def _():
            comm.start(cins, couts, sems)

        body(*ins, *outs, *scratch)

        @pl.when(last)
        def _():
            comm.wait(cins, couts, sems)

    return _pallas(
        wrapped, grid=grid, name=name + "_comm", in_specs=[*in_specs, *[ANY] * ci], out_specs=[*out_specs, *[ANY] * co],
        out_shape=[*out_shape, *comm.out_shapes], scratch_shapes=[*scratch_shapes, *comm.scratch],
        compiler_params=_cp(*["arbitrary"] * len(grid)))(*args, *comm.inputs)


def _swap_sibling(arrs):
    n = len(arrs)

    def body(*refs):
        ins, outs = refs[:n], refs[n:2 * n]
        send_sems, recv_sems = refs[2 * n:]
        x, y, c, _ = _place()
        copies = [pltpu.make_async_remote_copy(src_ref=ins[a], dst_ref=outs[a], send_sem=send_sems.at[a], recv_sem=recv_sems.at[a],
                                               device_id=(x, y, 1 - c), device_id_type=MESH) for a in range(n)]
        for cp in copies:
            cp.start()
        for cp in copies:
            cp.wait_recv()
        for cp in copies:
            cp.wait_send()

    return _pallas(
        body, name="swap_sibling", in_specs=[ANY] * n, out_specs=[ANY] * n,
        out_shape=[jax.ShapeDtypeStruct(a.shape, a.dtype) for a in arrs],
        scratch_shapes=[pltpu.SemaphoreType.DMA((n,)), pltpu.SemaphoreType.DMA((n,))],
    )(*arrs)


def _allreduce_small(v):
    def body(v_ref, o_ref, buf, send_sems, recv_sems):
        x, y, c, _ = _place()
        me = 4 * x + 2 * y + c
        buf[me] = v_ref[...]

        def copy(r):
            tx, ty, tc = (x + (r >> 2)) % 2, (y + ((r >> 1) & 1)) % 2, (c + (r & 1)) % 2
            return tx, ty, tc

        sends = []
        for r in range(1, 8):
            tx, ty, tc = copy(r)
            sends.append(pltpu.make_async_remote_copy(src_ref=v_ref, dst_ref=buf.at[me], send_sem=send_sems.at[r - 1],
                                                      recv_sem=recv_sems.at[r - 1], device_id=(tx, ty, tc), device_id_type=MESH))
        for cp in sends:
            cp.start()
        for r in range(1, 8):
            tx, ty, tc = copy(r)
            pltpu.make_async_remote_copy(src_ref=v_ref, dst_ref=buf.at[4 * tx + 2 * ty + tc], send_sem=send_sems.at[r - 1],
                                         recv_sem=recv_sems.at[r - 1], device_id=(tx, ty, tc), device_id_type=MESH).wait_recv()
        for cp in sends:
            cp.wait_send()
        acc = buf[0]
        for d in range(1, 8):
            acc = acc + buf[d]
        o_ref[...] = acc

    return pl.pallas_call(
        body, name="allreduce_small", in_specs=[pl.BlockSpec(memory_space=pltpu.VMEM)], out_specs=pl.BlockSpec(memory_space=pltpu.VMEM),
        out_shape=jax.ShapeDtypeStruct(v.shape, F32),
        scratch_shapes=[pltpu.VMEM((8, *v.shape), F32), pltpu.SemaphoreType.DMA((7,)), pltpu.SemaphoreType.DMA((7,))],
    )(v)


def _rows_view(a):
    return a.reshape(-1, a.shape[-1])


def _elementwise(name, fn, ins, n_out):
    rows, cols = ins[0].shape
    tm = min(rows, 256)
    spec = pl.BlockSpec((tm, cols), lambda i: (i, 0))

    def body(*refs):
        outs = fn(*[r[...] for r in refs[:len(ins)]])
        for o_ref, o in zip(refs[len(ins):], outs):
            o_ref[...] = o

    return _pallas(
        body, grid=(rows // tm,), name=name, in_specs=[spec] * len(ins), out_specs=[spec] * n_out,
        out_shape=[jax.ShapeDtypeStruct((rows, cols), F32)] * n_out, compiler_params=_cp("parallel"),
    )(*ins)


def _sum_parts(name, own, recv):
    def fn(o, r0, r1, r2):
        return (o + r0.astype(F32) + r1.astype(F32) + r2.astype(F32),)

    return _elementwise("sum_parts_" + name, fn, [own, recv[0], recv[1], recv[2]], 1)[0]


def _adamw(name, w, m, v, g_parts):
    def fn(w_, m_, v_, *gs):
        g = gs[0]
        for extra in gs[1:]:
            g = g + extra
        m_new = ADAM_B1 * m_ + (1.0 - ADAM_B1) * g
        v_new = ADAM_B2 * v_ + (1.0 - ADAM_B2) * (g * g)
        m_hat = m_new / (1.0 - ADAM_B1 ** ADAM_STEP)
        v_hat = v_new / (1.0 - ADAM_B2 ** ADAM_STEP)
        delta = -ADAM_LR * (m_hat / (jnp.sqrt(v_hat) + ADAM_EPS) + ADAM_WD * w_)
        return g, delta, m_new, v_new

    return _elementwise("adamw_" + name, fn, [w, m, v, *g_parts], 4)


MAT_AXIS = {"w_in": 1, "w_uq": 1, "w_ukv": 1, "w_out": 0}
SMALL = ("rel_bias_table", "norm_in", "sink_a", "norm_q_lat", "norm_kv_lat", "norm_out_a", "norm_out_b", "norm_final")
WEIGHTS = ("meta_tokens", "rel_bias_table", "norm_in", "w_in", "sink_a", "norm_q_lat", "w_uq", "norm_kv_lat", "w_ukv",
           "norm_out_a", "norm_out_b", "w_out", "norm_final")
SMALL_ROWS, SMALL_COLS = 8, 1024


def _pack_small(d, loss=None):
    flat = [d[n].reshape(-1) for n in SMALL]
    if loss is not None:
        flat.append(loss.reshape(1))
    used = sum(f.shape[0] for f in flat)
    flat.append(jnp.zeros((SMALL_ROWS * SMALL_COLS - used,), F32))
    return jnp.concatenate(flat).reshape(SMALL_ROWS, SMALL_COLS)


def _unpack_small(p, like):
    flat, out, off = p.reshape(-1), {}, 0
    for n in SMALL:
        size = int(np.prod(like[n].shape))
        out[n] = flat[off:off + size].reshape(like[n].shape)
        off += size
    return out, flat[off]


def _split4(a, axis):
    size = a.shape[axis] // 4
    return jnp.stack([lax.slice_in_dim(a, k * size, (k + 1) * size, axis=axis) for k in range(4)])


def _train_step(cfg, x, target, w, m, v):
    depth = w["w_in"].shape[0]
    rest = tuple(n for n in MATRICES if n != "w_in")
    weights, splits, received = {}, {}, {}

    def gather(i, names, also=(), build=_gather_comm):
        def deliver(res):
            for n, g in zip(names, res):
                weights[i, n] = jnp.concatenate([g[k] for k in range(4)], axis=MAT_AXIS[n])

        return build([w[n][i].astype(BF16) for n in names] + list(also)), deliver

    def scatter(i, names, mats, also=()):
        for n in names:
            splits[i, n] = _split4(mats[i][n], MAT_AXIS[n])

        def deliver(res):
            for n, r in zip(names, res):
                received[i, n] = r

        return _scatter_comm([splits[i, n].astype(BF16) for n in names] + list(also)), deliver

    def rider(stage, i, mats):
        if stage == "bias_build":
            comm, deliver = gather(0, ("w_in",), also=[w["meta_tokens"]], build=_gather_halves_comm)

            def deliver_first(res):
                deliver(res)
                weights["meta"] = jnp.concatenate([res[1][k] for k in range(4)], axis=1)

            return comm, deliver_first
        if stage == "inproj_fwd" and i == 0:
            return gather(0, rest)
        if stage == "mla_fwd" and i + 1 < depth:
            return gather(i + 1, MATRICES)
        if stage == "mla_bwd" and i + 1 < depth:
            return scatter(i + 1, MATRICES, mats)
        if stage == "inproj_bwd" and i == 0:
            return scatter(0, rest, mats)
        if stage == "bias_grad":
            splits["meta"] = _split4(mats["meta_tokens"], 1)
            comm, deliver = scatter(0, ("w_in",), mats, also=[splits["meta"].astype(BF16)])

            def deliver_last(res):
                deliver(res)
                received["meta"] = res[1]

            return comm, deliver_last
        return None

    loss_local, grad_x, g, mats = _local_grads(cfg, x, target, lambda: weights["meta"], w["rel_bias_table"], {n: w[n] for n in SMALL},
                                               lambda i, n: weights[i, n], rider)

    small_sum = _allreduce_small(_pack_small(g, loss_local))
    g_small, loss = _unpack_small(small_sum, {n: w[n] for n in SMALL})

    k_me = 2 * lax.axis_index("x") + lax.axis_index("y")

    def core_sum(name, split, recv):
        own = lax.dynamic_index_in_dim(split, k_me, 0, keepdims=False)
        return _sum_parts(name, _rows_view(own), recv.reshape(3, -1, recv.shape[-1]))

    partial = [core_sum("meta_tokens", splits["meta"], received["meta"])]
    for n in MATRICES:
        partial.append(jnp.concatenate([core_sum(f"{n}_{i}", splits[i, n], received[i, n]) for i in range(depth)], axis=0))
    sibling = _swap_sibling(partial)

    outs = {}
    for n, p_me, p_sib in zip(("meta_tokens", *MATRICES), partial, sibling):
        res = _adamw(n, _rows_view(w[n]), _rows_view(m[n]), _rows_view(v[n]), [p_me, p_sib])
        outs[n] = [r.reshape(w[n].shape) for r in res]
    res = _adamw("small", _pack_small(w), _pack_small(m), _pack_small(v), [_pack_small(g_small)])
    unpacked = [_unpack_small(r, {n: w[n] for n in SMALL})[0] for r in res]
    for n in SMALL:
        outs[n] = [u[n] for u in unpacked]

    result = [loss, grad_x]
    for field in range(4):
        result.extend(outs[n][field] for n in WEIGHTS)
    return tuple(result)


def kernel(x, meta_tokens, rel_bias_table, norm_in, w_in, sink_a, norm_q_lat, w_uq, norm_kv_lat, w_ukv, norm_out_a, norm_out_b, w_out, norm_final, loss_target, m_meta_tokens, m_rel_bias_table, m_norm_in, m_w_in, m_sink_a, m_norm_q_lat, m_w_uq, m_norm_kv_lat, m_w_ukv, m_norm_out_a, m_norm_out_b, m_w_out, m_norm_final, v_meta_tokens, v_rel_bias_table, v_norm_in, v_w_in, v_sink_a, v_norm_q_lat, v_w_uq, v_norm_kv_lat, v_w_ukv, v_norm_out_a, v_norm_out_b, v_w_out, v_norm_final):
    w = dict(zip(WEIGHTS, (meta_tokens, rel_bias_table, norm_in, w_in, sink_a, norm_q_lat, w_uq, norm_kv_lat, w_ukv, norm_out_a, norm_out_b, w_out, norm_final)))
    m = dict(zip(WEIGHTS, (m_meta_tokens, m_rel_bias_table, m_norm_in, m_w_in, m_sink_a, m_norm_q_lat, m_w_uq, m_norm_kv_lat, m_w_ukv, m_norm_out_a, m_norm_out_b, m_w_out, m_norm_final)))
    v = dict(zip(WEIGHTS, (v_meta_tokens, v_rel_bias_table, v_norm_in, v_w_in, v_sink_a, v_norm_q_lat, v_w_uq, v_norm_kv_lat, v_w_ukv, v_norm_out_a, v_norm_out_b, v_w_out, v_norm_final)))
    cfg = make_cfg(x.shape[0], x.shape[1])
    return _train_step(cfg, x, loss_target, w, m, v)
```

```python
import collections
import functools
import math

import jax
import jax.numpy as jnp
import numpy as np
from jax import lax
from jax.experimental import pallas as pl
from jax.experimental.pallas import tpu as pltpu

F32 = jnp.float32
BF16 = jnp.bfloat16

BLK = 128
N_META = 16
D_MODEL = 1024
A_HEADS, A_KV, A_DH = 8, 2, 64
B_HEADS, B_NOPE, B_ROPE, B_DV = 8, 64, 32, 64
Q_RANK, KV_RANK = 256, 128
N_BUCKETS, MAX_DIST = 32, 128
ROPE_THETA = 10000.0
EPS = 1e-6
IN_WIDTH = 2208
W_IN_P = 2304
NEG = -1e30
MASK_LANE = 96
LOG2E = math.log2(math.e)
Q_SCALE = (B_NOPE + B_ROPE) ** -0.5 * LOG2E
QA_SCALE = A_DH ** -0.5 * LOG2E
LN2 = math.log(2.0)
VMEM_LIMIT = 48 * 1024 * 1024

ADAM_LR, ADAM_B1, ADAM_B2, ADAM_EPS, ADAM_WD, ADAM_STEP = 0.001, 0.9, 0.999, 1e-08, 0.01, 10

Cfg = collections.namedtuple("Cfg", "B S NB NJ LP TP")


def make_cfg(batch, seq):
    nb = seq // BLK
    nj = nb + 1
    return Cfg(batch, seq, nb, nj, nj * BLK, batch * nj * BLK)


def _cp(*sem):
    return pltpu.CompilerParams(dimension_semantics=sem, vmem_limit_bytes=VMEM_LIMIT)


def _pallas(body, *, out_shape, **kw):
    pinned = jax.tree.map(lambda s: pltpu.HBM(s.shape, s.dtype), out_shape)
    call = pl.pallas_call(body, out_shape=pinned, **kw)
    return lambda *args: call(*[pltpu.with_memory_space_constraint(a, pltpu.HBM) for a in args])


def _dot(a, b):
    return jnp.dot(a, b, preferred_element_type=F32)


def _dot_nt(a, b):
    return lax.dot_general(a, b, (((1,), (1,)), ((), ())), preferred_element_type=F32)


def _dot_tn(a, b):
    return lax.dot_general(a, b, (((0,), (0,)), ((), ())), preferred_element_type=F32)


def _rms(x, width=None):
    n = x.shape[-1] if width is None else width
    r = lax.rsqrt(jnp.sum(x * x, axis=-1, keepdims=True) * (1.0 / n) + EPS)
    return x * r, r


def _rms_bwd(xhat, r, t):
    n = xhat.shape[-1]
    return r * (t - xhat * (jnp.sum(t * xhat, axis=-1, keepdims=True) * (1.0 / n)))


def _sigmoid(x):
    return 1.0 / (1.0 + jnp.exp(-x))


def _lane(shape):
    return lax.broadcasted_iota(jnp.int32, shape, len(shape) - 1)


def _swap_rope(x):
    n = x.shape[-1]
    lane = _lane(x.shape) % BLK
    up = pltpu.roll(x, n - 16, axis=x.ndim - 1)
    dn = pltpu.roll(x, 16, axis=x.ndim - 1)
    return jnp.where((lane >= 64) & (lane < 80), up, jnp.where((lane >= 80) & (lane < 96), dn, 0.0))


A_ORDER = (0, 4, 1, 5, 2, 6, 3, 7)


def _jtype(j, nj):
    return 0 if j == 0 else 1 if j == 1 else 3 if j == nj - 1 else 2


def _window_structure(nj):
    def pos(blk, r):
        return np.where(blk == 0, r, N_META + (blk - 1) * BLK + r)

    def valid(blk, r):
        return np.where(blk == 0, r < N_META, True)

    r = np.arange(BLK)
    rels, viss = [], []
    for j in (0, 1, 2, nj - 1):
        qpos = pos(j, r)[:, None]
        rel_t, vis_t = [], []
        for s, kb in enumerate((0, j - 1, j, j + 1)):
            slot_ok = (s == 0) or (1 <= kb <= nj - 1)
            kbc = min(max(kb, 0), nj - 1)
            kpos = pos(kbc, r)[None, :]
            rel = kpos - qpos
            v = valid(kbc, r)[None, :] & np.ones((BLK, 1), bool)
            if s > 0:
                v = v & (np.abs(rel) <= BLK)
            rel_t.append(rel)
            vis_t.append(v & slot_ok)
        rels.append(np.concatenate(rel_t, axis=1))
        viss.append(np.concatenate(vis_t, axis=1))
    return np.stack(rels).astype(np.int32), np.stack(viss)


def _t5_bucket(rel):
    nb = N_BUCKETS // 2
    max_exact = nb // 2
    ret = jnp.where(rel > 0, nb, 0)
    n = jnp.abs(rel)
    nf = jnp.maximum(n, 1).astype(jnp.float32)
    large = max_exact + (jnp.log(nf / max_exact) / math.log(MAX_DIST / max_exact) * (nb - max_exact)).astype(jnp.int32)
    large = jnp.minimum(large, nb - 1)
    return ret + jnp.where(n < max_exact, n, large)


def _perm_heads64(a, axis):
    parts = [lax.slice_in_dim(a, h * 64, (h + 1) * 64, axis=axis) for h in A_ORDER]
    return jnp.concatenate(parts, axis=axis)


def _unperm_heads64(a, axis):
    inv = [A_ORDER.index(h) for h in range(8)]
    parts = [lax.slice_in_dim(a, p * 64, (p + 1) * 64, axis=axis) for p in inv]
    return jnp.concatenate(parts, axis=axis)


def _w_in_to_p(w):
    sl = lambda a, b: lax.slice_in_dim(w, a, b, axis=1)
    z = lambda n: jnp.zeros((w.shape[0], n), w.dtype)
    return jnp.concatenate([_perm_heads64(sl(0, 512), 1), sl(512, 768), _perm_heads64(sl(768, 1280), 1), sl(1696, 2208),
                            sl(1280, 1536), sl(1536, 1664), z(64), sl(1664, 1696), z(32)], axis=1)


def _w_in_from_p(g):
    sl = lambda a, b: lax.slice_in_dim(g, a, b, axis=1)
    return jnp.concatenate([_unperm_heads64(sl(0, 512), 1), sl(512, 768), _unperm_heads64(sl(768, 1280), 1),
                            sl(1792, 2048), sl(2048, 2176), sl(2240, 2272), sl(1280, 1792)], axis=1)


def _w_uq_to_p(w):
    z = jnp.zeros((w.shape[0], 32), w.dtype)
    return jnp.concatenate([p for h in range(8) for p in (lax.slice_in_dim(w, h * 96, (h + 1) * 96, axis=1), z)], axis=1)


def _w_uq_from_p(g):
    return jnp.concatenate([lax.slice_in_dim(g, h * 128, h * 128 + 96, axis=1) for h in range(8)], axis=1)


def _w_ukv_to_p(w):
    z = jnp.zeros((w.shape[0], 64), w.dtype)
    ks = [p for h in range(8) for p in (lax.slice_in_dim(w, h * 128, h * 128 + 64, axis=1), z)]
    vs = [lax.slice_in_dim(w, h * 128 + 64, (h + 1) * 128, axis=1) for h in range(8)]
    return jnp.concatenate(ks + vs, axis=1)


def _w_ukv_from_p(g):
    parts = []
    for h in range(8):
        parts.append(lax.slice_in_dim(g, h * 128, h * 128 + 64, axis=1))
        parts.append(lax.slice_in_dim(g, 1024 + h * 64, 1024 + (h + 1) * 64, axis=1))
    return jnp.concatenate(parts, axis=1)


def _w_out_to_p(w):
    return jnp.concatenate([_perm_heads64(lax.slice_in_dim(w, 0, 512, axis=0), 0), lax.slice_in_dim(w, 512, 1024, axis=0)], axis=0)


def _w_out_from_p(g):
    return jnp.concatenate([_unperm_heads64(lax.slice_in_dim(g, 0, 512, axis=0), 0), lax.slice_in_dim(g, 512, 1024, axis=0)], axis=0)


def _rope_tables(cfg):
    half = B_ROPE // 2
    length = N_META + cfg.S
    freqs = ROPE_THETA ** (-jnp.arange(half, dtype=jnp.float32) / half)
    ang = jnp.arange(length, dtype=jnp.float32)[:, None] * freqs[None, :]
    cos, sin = jnp.cos(ang), jnp.sin(ang)

    def rows(t):
        return jnp.concatenate([t[:N_META], jnp.zeros((BLK - N_META, t.shape[1]), t.dtype), t[N_META:]], axis=0)

    ones = jnp.ones((length, 64), F32)
    zer = jnp.zeros((length, 32), F32)
    c_tab = rows(jnp.concatenate([ones, cos, cos, zer], axis=1))
    s_tab = rows(jnp.concatenate([zer, zer, -sin, sin, zer], axis=1))
    return c_tab, s_tab


def _inproj_fwd(cfg, h, g, w_p, comm=None):
    tm = 256

    def body(h_ref, g_ref, w_ref, pa_ref, pf_ref):
        xh, _ = _rms(h_ref[...])
        u = (xh * g_ref[...]).astype(BF16)
        acc = _dot(u, w_ref[...])
        pa_ref[:, :512] = (acc[:, :512] * QA_SCALE).astype(BF16)
        pa_ref[:, 512:] = acc[:, 512:768].astype(BF16)
        pf_ref[...] = acc[:, 768:]

    return _call_with_comm(
        body, comm, grid=(cfg.TP // tm,), name="inproj_fwd",
        in_specs=[pl.BlockSpec((tm, D_MODEL), lambda i: (i, 0)), pl.BlockSpec((1, D_MODEL), lambda i: (0, 0)),
                  pl.BlockSpec((D_MODEL, W_IN_P), lambda i: (0, 0))],
        out_specs=[pl.BlockSpec((tm, 768), lambda i: (i, 0)), pl.BlockSpec((tm, 1536), lambda i: (i, 0))],
        out_shape=[jax.ShapeDtypeStruct((cfg.TP, 768), BF16), jax.ShapeDtypeStruct((cfg.TP, 1536), F32)],
        args=(h, g, w_p))


def _lat_fwd(cfg, pf, gq, gkv, wq_p, wkv_p, c_tab, s_tab):
    nj = cfg.NJ

    def body(cq_ref, ckv_ref, kr_ref, gq_ref, gkv_ref, wq_ref, wkv_ref, c_ref, s_ref, q_ref, k_ref, v_ref, kt_ref, vt_ref):
        c1, s1 = c_ref[...], s_ref[...]
        c8, s8 = jnp.tile(c1, (1, 8)), jnp.tile(s1, (1, 8))
        mask_lane = _lane((BLK, 1024)) % BLK == MASK_LANE
        zero_row = (pl.program_id(1) == 0) & (lax.broadcasted_iota(jnp.int32, (BLK, 1024), 0) >= N_META)
        xq, _ = _rms(cq_ref[...])
        qp = _dot((xq * gq_ref[...]).astype(BF16), wq_ref[...])
        q_ref[...] = jnp.where(mask_lane, 1.0, (qp * c8 + _swap_rope(qp) * s8) * Q_SCALE).astype(BF16)
        xk, _ = _rms(ckv_ref[...])
        kvp = _dot((xk * gkv_ref[...]).astype(BF16), wkv_ref[...])
        kr = kr_ref[...]
        krr = kr * c1 + _swap_rope(kr) * s1
        k = jnp.where(mask_lane & zero_row, NEG, kvp[:, :1024] + jnp.tile(krr, (1, 8)))
        k_ref[...] = k.astype(BF16)
        v_ref[...] = kvp[:, 1024:].astype(BF16)
        kt_ref[...] = k.T.astype(BF16)
        vt_ref[...] = kvp[:, 1024:].T.astype(BF16)

    row = lambda b, j: b * nj + j
    return _pallas(
        body, grid=(cfg.B, nj), name="lat_fwd",
        in_specs=[pl.BlockSpec((BLK, 256), lambda b, j: (row(b, j), 4)), pl.BlockSpec((BLK, 128), lambda b, j: (row(b, j), 10)),
                  pl.BlockSpec((BLK, 128), lambda b, j: (row(b, j), 11)),
                  pl.BlockSpec((1, 256), lambda b, j: (0, 0)), pl.BlockSpec((1, 128), lambda b, j: (0, 0)),
                  pl.BlockSpec((256, 1024), lambda b, j: (0, 0)), pl.BlockSpec((128, 1536), lambda b, j: (0, 0)),
                  pl.BlockSpec((BLK, 128), lambda b, j: (j, 0)), pl.BlockSpec((BLK, 128), lambda b, j: (j, 0))],
        out_specs=[pl.BlockSpec((BLK, 1024), lambda b, j: (row(b, j), 0)), pl.BlockSpec((BLK, 1024), lambda b, j: (row(b, j), 0)),
                   pl.BlockSpec((BLK, 512), lambda b, j: (row(b, j), 0)),
                   pl.BlockSpec((1024, BLK), lambda b, j: (b, j)), pl.BlockSpec((512, BLK), lambda b, j: (b, j))],
        out_shape=[jax.ShapeDtypeStruct((cfg.TP, 1024), BF16), jax.ShapeDtypeStruct((cfg.TP, 1024), BF16),
                   jax.ShapeDtypeStruct((cfg.TP, 512), BF16),
                   jax.ShapeDtypeStruct((cfg.B * 1024, cfg.LP), BF16), jax.ShapeDtypeStruct((cfg.B * 512, cfg.LP), BF16)],
        compiler_params=_cp("parallel", "parallel"),
    )(pf, pf, pf, gq, gkv, wq_p, wkv_p, c_tab, s_tab)


def _gate_halves(ya, yb, ga, gb, goa, gob):
    xa, ra = _rms(ya)
    xb, rb = _rms(yb)
    sga, sgb = _sigmoid(ga), _sigmoid(gb)
    return xa, ra, xb, rb, sga, sgb, xa * goa * (ga * sga), xb * gob * (gb * sgb)


def _out_fwd(cfg, ya, yb, pf, goa, gob, wo_p, h):
    tm = 256

    def body(ya_ref, yb_ref, ga_ref, gb_ref, goa_ref, gob_ref, w_ref, h_ref, o_ref):
        *_, y_a, y_b = _gate_halves(ya_ref[...], yb_ref[...], ga_ref[...], gb_ref[...], goa_ref[...], gob_ref[...])
        y = jnp.concatenate([y_a, y_b], axis=1).astype(BF16)
        o_ref[...] = h_ref[...] + _dot(y, w_ref[...])

    return _pallas(
        body, grid=(cfg.TP // tm,), name="out_fwd",
        in_specs=[pl.BlockSpec((tm, 512), lambda i: (i, 0)), pl.BlockSpec((tm, 512), lambda i: (i, 0)),
                  pl.BlockSpec((tm, 512), lambda i: (i, 0)), pl.BlockSpec((tm, 512), lambda i: (i, 1)),
                  pl.BlockSpec((1, 512), lambda i: (0, 0)), pl.BlockSpec((1, 512), lambda i: (0, 0)),
                  pl.BlockSpec((D_MODEL, D_MODEL), lambda i: (0, 0)), pl.BlockSpec((tm, D_MODEL), lambda i: (i, 0))],
        out_specs=pl.BlockSpec((tm, D_MODEL), lambda i: (i, 0)),
        out_shape=jax.ShapeDtypeStruct((cfg.TP, D_MODEL), F32),
        compiler_params=_cp("parallel"),
    )(ya, yb, pf, pf, goa, gob, wo_p, h)


def _bias_build(table, bucket, maskadd, comm=None):
    def body(tab_ref, bk_ref, ma_ref, o_ref):
        def rows(g, carry):
            r = pl.ds(pl.multiple_of(g * 8, 8), 8)
            bk = bk_ref[0, r, :]
            accs = [jnp.zeros(bk.shape, F32)] * A_HEADS
            for b in range(N_BUCKETS):
                hit = bk == b
                accs = [jnp.where(hit, tab_ref[b, h], accs[h]) for h in range(A_HEADS)]
            ma = ma_ref[0, r, :]
            for h in range(A_HEADS):
                o_ref[0, h, r, :] = (accs[h] + ma) * LOG2E
            return carry

        lax.fori_loop(0, BLK // 8, rows, 0)

    return _call_with_comm(
        body, comm, grid=(4,), name="bias_build",
        in_specs=[pl.BlockSpec(memory_space=pltpu.SMEM), pl.BlockSpec((1, BLK, 512), lambda t: (t, 0, 0)),
                  pl.BlockSpec((1, BLK, 512), lambda t: (t, 0, 0))],
        out_specs=[pl.BlockSpec((1, A_HEADS, BLK, 512), lambda t: (t, 0, 0, 0))],
        out_shape=[jax.ShapeDtypeStruct((4, A_HEADS, BLK, 512), F32)],
        args=(table, bucket, maskadd))


def _bias_grad(s_accs, bucket, comm=None):
    depth = len(s_accs)

    def body(*refs):
        s_refs, bk_ref, o_ref, sum_ref, part_ref = refs[:depth], refs[depth], refs[depth + 1], refs[depth + 2], refs[depth + 3]
        t = pl.program_id(0)

        @pl.when(t == 0)
        def _():
            o_ref[...] = jnp.zeros_like(o_ref)

        total = s_refs[0][0]
        for extra in s_refs[1:]:
            total = total + extra[0]
        sum_ref[...] = total

        def step(b, carry):
            accs = [jnp.zeros((8, 512), F32) for _ in range(A_HEADS)]
            for g in range(BLK // 8):
                rows = pl.ds(g * 8, 8)
                hit = bk_ref[0, rows, :] == b
                for h in range(A_HEADS):
                    accs[h] = accs[h] + jnp.where(hit, sum_ref[h, rows, :], 0.0)
            rows8 = jnp.concatenate([jnp.sum(a, axis=0, keepdims=True) for a in accs], axis=0)
            part_ref[pl.ds(pl.multiple_of(b * A_HEADS, 8), A_HEADS), :] = rows8
            return carry

        lax.fori_loop(0, N_BUCKETS, step, 0)
        o_ref[...] += jnp.broadcast_to(jnp.sum(part_ref[...], axis=1, keepdims=True), o_ref.shape)

    s_spec = pl.BlockSpec((1, A_HEADS, BLK, 512), lambda t: (t, 0, 0, 0))
    return _call_with_comm(
        body, comm, grid=(4,), name="bias_grad",
        in_specs=[s_spec] * depth + [pl.BlockSpec((1, BLK, 512), lambda t: (t, 0, 0))],
        out_specs=[pl.BlockSpec((N_BUCKETS * A_HEADS, 128), lambda t: (0, 0))],
        out_shape=[jax.ShapeDtypeStruct((N_BUCKETS * A_HEADS, 128), F32)],
        scratch_shapes=[pltpu.VMEM((A_HEADS, BLK, 512), F32), pltpu.VMEM((N_BUCKETS * A_HEADS, 512), F32)],
        args=(*s_accs, bucket))


def _win_specs(cfg):
    nj = cfg.NJ
    row = lambda b, j: b * nj + j
    jt = lambda j: jnp.where(j == 0, 0, jnp.where(j == 1, 1, jnp.where(j == nj - 1, 3, 2)))
    slot_rows = [lambda b, j: row(b, 0), lambda b, j: row(b, jnp.maximum(j - 1, 0)), lambda b, j: row(b, j),
                 lambda b, j: row(b, jnp.minimum(j + 1, nj - 1))]
    k_specs = [pl.BlockSpec((BLK, 128), functools.partial(lambda b, j, f: (f(b, j), 4), f=f)) for f in slot_rows]
    v_specs = [pl.BlockSpec((BLK, 128), functools.partial(lambda b, j, f: (f(b, j), 5), f=f)) for f in slot_rows]
    q_spec = pl.BlockSpec((BLK, 512), lambda b, j: (row(b, j), 0))
    bias_spec = pl.BlockSpec((1, A_HEADS, BLK, 512), lambda b, j: (jt(j), 0, 0, 0))
    return row, jt, q_spec, k_specs, v_specs, bias_spec


def _stack4(ref):
    return jnp.concatenate([ref[:, c * 128:(c + 1) * 128] for c in range(4)], axis=0)


def _win_keys(k_refs, v_refs):
    k4 = jnp.concatenate([r[...] for r in k_refs], axis=0)
    v4 = jnp.concatenate([r[...] for r in v_refs], axis=0)
    lane_k = _lane(k4.shape)
    return (jnp.where(lane_k < 64, k4, jnp.zeros_like(k4)), jnp.where(lane_k >= 64, k4, jnp.zeros_like(k4))), v4


def _sink_col(sink_ref, hf):
    rowi = lax.broadcasted_iota(jnp.int32, (4 * BLK, 1), 0)
    col = jnp.full((4 * BLK, 1), sink_ref[4 * hf + 3], F32)
    for c in (2, 1, 0):
        col = jnp.where(rowi < (c + 1) * BLK, sink_ref[4 * hf + c], col)
    return col * LOG2E


def _win_fwd(cfg, pa, bias, sink):
    row, jt, q_spec, k_specs, v_specs, bias_spec = _win_specs(cfg)

    def body(sink_ref, q_ref, k0, k1, k2, k3, v0, v1, v2, v3, b_ref, o_ref, lse_ref):
        kk, v4 = _win_keys((k0, k1, k2, k3), (v0, v1, v2, v3))
        qs = _stack4(q_ref)
        lane_o = _lane((4 * BLK, 128))
        outs, lses = [], []
        for hf in range(2):
            s = _dot_nt(qs, kk[hf]) + b_ref[0, 4 * hf:4 * hf + 4].reshape(4 * BLK, 512)
            sink_col = _sink_col(sink_ref, hf)
            m = jnp.maximum(jnp.max(s, axis=1, keepdims=True), sink_col)
            e = jnp.exp2(s - m)
            den = jnp.sum(e, axis=1, keepdims=True) + jnp.exp2(sink_col - m)
            outs.append(_dot(e.astype(BF16), v4) / den)
            lses.append(m + jnp.log2(den))
        o = jnp.where(lane_o < 64, outs[0], outs[1])
        for c in range(4):
            o_ref[:, c * 128:(c + 1) * 128] = o[c * BLK:(c + 1) * BLK]
        lse_ref[...] = jnp.where(lane_o == 0, lses[0], jnp.where(lane_o == 1, lses[1], 0.0))

    return _pallas(
        body, grid=(cfg.B, cfg.NJ), name="win_fwd",
        in_specs=[pl.BlockSpec(memory_space=pltpu.SMEM), q_spec, *k_specs, *v_specs, bias_spec],
        out_specs=[pl.BlockSpec((BLK, 512), lambda b, j: (row(b, j), 0)), pl.BlockSpec((4 * BLK, 128), lambda b, j: (row(b, j), 0))],
        out_shape=[jax.ShapeDtypeStruct((cfg.TP, 512), F32), jax.ShapeDtypeStruct((4 * cfg.TP, 128), F32)],
        compiler_params=_cp("parallel", "parallel"),
    )(sink, pa, *([pa] * 8), bias)


def _win_bwd(cfg, pa, bias, sink, dya, ya, lse):
    row, jt, q_spec, k_specs, v_specs, bias_spec = _win_specs(cfg)
    nj = cfg.NJ

    def body(sink_ref, q_ref, k0, k1, k2, k3, v0, v1, v2, v3, b_ref, dy_ref, y_ref, lse_ref,
             dq_ref, dkp_ref, dvp_ref, dkm_ref, dvm_ref, s_ref, dsink_ref):
        j = pl.program_id(1)
        kind = jt(j)

        @pl.when((pl.program_id(0) == 0) & (j == 0))
        def _():
            s_ref[...] = jnp.zeros_like(s_ref)

        kk, v4 = _win_keys((k0, k1, k2, k3), (v0, v1, v2, v3))
        qs, dys, ys = _stack4(q_ref), _stack4(dy_ref), _stack4(y_ref)
        lane_o = _lane((4 * BLK, 128))
        half = (lane_o < 64, lane_o >= 64)
        lse_blk = lse_ref[...]
        dq = jnp.zeros((4 * BLK, 128), F32)
        dk4 = jnp.zeros((512, 128), F32)
        dv4 = jnp.zeros((512, 128), F32)
        dsink = jnp.zeros((8, 128), F32)
        lane_s = _lane((8, 128))
        row_s = lax.broadcasted_iota(jnp.int32, (8, 128), 0)
        for hf in range(2):
            lse_h = jnp.sum(jnp.where(lane_o == hf, lse_blk, 0.0), axis=1, keepdims=True)
            s = _dot_nt(qs, kk[hf]) + b_ref[0, 4 * hf:4 * hf + 4].reshape(4 * BLK, 512)
            p = jnp.exp2(s - lse_h)
            do_h = jnp.where(half[hf], dys, 0.0)
            delta = jnp.sum(do_h * ys, axis=1, keepdims=True)
            do_b = do_h.astype(BF16)
            ds = p * (_dot_nt(do_b, v4) - delta)
            s_ref[kind, 4 * hf:4 * hf + 4] += ds.reshape(4, BLK, 512)
            sink_grad = jnp.exp2(_sink_col(sink_ref, hf) - lse_h) * delta
            for c in range(4):
                tot = -jnp.sum(sink_grad[c * BLK:(c + 1) * BLK])
                dsink = jnp.where((row_s == 0) & (lane_s == 4 * hf + c), tot, dsink)
            dsb = (ds * LN2).astype(BF16)
            dq = dq + _dot(dsb, kk[hf])
            dk4 = dk4 + _dot_tn(dsb, jnp.where(half[hf], qs, jnp.zeros_like(qs)))
            dv4 = dv4 + _dot_tn(p.astype(BF16), do_b)
        for c in range(4):
            dq_ref[:, c * 128:(c + 1) * 128] = (dq[c * BLK:(c + 1) * BLK] * QA_SCALE).astype(BF16)
        dkp_ref[0] = dk4
        dvp_ref[0] = dv4

        @pl.when(j == 0)
        def _():
            dkm_ref[...] = dk4[:BLK]
            dvm_ref[...] = dv4[:BLK]

        @pl.when(j > 0)
        def _():
            dkm_ref[...] += dk4[:BLK]
            dvm_ref[...] += dv4[:BLK]

        @pl.when((pl.program_id(0) == 0) & (j == 0))
        def _():
            dsink_ref[...] = dsink

        @pl.when((pl.program_id(0) > 0) | (j > 0))
        def _():
            dsink_ref[...] += dsink

    blk_row = pl.BlockSpec((BLK, 512), lambda b, j: (row(b, j), 0))
    return _pallas(
        body, grid=(cfg.B, nj), name="win_bwd",
        in_specs=[pl.BlockSpec(memory_space=pltpu.SMEM), q_spec, *k_specs, *v_specs, bias_spec, blk_row, blk_row,
                  pl.BlockSpec((4 * BLK, 128), lambda b, j: (row(b, j), 0))],
        out_specs=[blk_row,
                   pl.BlockSpec((1, 512, 128), lambda b, j: (row(b, j), 0, 0)), pl.BlockSpec((1, 512, 128), lambda b, j: (row(b, j), 0, 0)),
                   pl.BlockSpec((BLK, 128), lambda b, j: (b, 0)), pl.BlockSpec((BLK, 128), lambda b, j: (b, 0)),
                   pl.BlockSpec((4, A_HEADS, BLK, 512), lambda b, j: (0, 0, 0, 0)),
                   pl.BlockSpec((8, 128), lambda b, j: (0, 0))],
        out_shape=[jax.ShapeDtypeStruct((cfg.TP, 512), BF16),
                   jax.ShapeDtypeStruct((cfg.B * nj, 512, 128), F32), jax.ShapeDtypeStruct((cfg.B * nj, 512, 128), F32),
                   jax.ShapeDtypeStruct((cfg.B * BLK, 128), F32), jax.ShapeDtypeStruct((cfg.B * BLK, 128), F32),
                   jax.ShapeDtypeStruct((4, A_HEADS, BLK, 512), F32),
                   jax.ShapeDtypeStruct((8, 128), F32)],
        compiler_params=_cp("arbitrary", "arbitrary"),
    )(sink, pa, *([pa] * 8), bias, dya, ya, lse)


def _win_dkv_combine(cfg, dkp, dvp, dkm, dvm):
    nj = cfg.NJ

    def body(kp, vp, km, vm, o_ref):
        o_ref[:BLK, :128] = km[...].astype(BF16)
        o_ref[:BLK, 128:] = vm[...].astype(BF16)
        for kb in range(1, nj):
            for col, part in ((0, kp), (128, vp)):
                tot = part[kb, 2 * BLK:3 * BLK] + part[kb - 1, 3 * BLK:4 * BLK]
                if kb + 1 < nj:
                    tot = tot + part[kb + 1, BLK:2 * BLK]
                o_ref[kb * BLK:(kb + 1) * BLK, col:col + 128] = tot.astype(BF16)

    return _pallas(
        body, grid=(cfg.B,), name="win_dkv_combine",
        in_specs=[pl.BlockSpec((nj, 512, 128), lambda b: (b, 0, 0)), pl.BlockSpec((nj, 512, 128), lambda b: (b, 0, 0)),
                  pl.BlockSpec((BLK, 128), lambda b: (b, 0)), pl.BlockSpec((BLK, 128), lambda b: (b, 0))],
        out_specs=pl.BlockSpec((cfg.LP, 256), lambda b: (b, 0)),
        out_shape=jax.ShapeDtypeStruct((cfg.TP, 256), BF16),
        compiler_params=_cp("parallel"),
    )(dkp, dvp, dkm, dvm)


def _pair_blockdiag(q):
    lane = _lane(q.shape)
    return jnp.concatenate([jnp.where(lane < 128, q, jnp.zeros_like(q)), jnp.where(lane >= 128, q, jnp.zeros_like(q))], axis=0)


def _mla_fwd(cfg, q, kt, v, comm=None):
    nj, lp = cfg.NJ, cfg.LP

    def body(q_ref, kt_ref, v_ref, o_ref, lse_ref, s_even, s_odd):
        i = pl.program_id(2)
        lane_o = _lane((BLK, 128))

        def logits(s_write):
            s_write[...] = _dot(_pair_blockdiag(q_ref[...]), kt_ref[...])

        def finish(s_read):
            s = s_read[...]
            m = jnp.max(s, axis=1, keepdims=True)
            e = jnp.exp2(s - m)
            den = jnp.sum(e, axis=1, keepdims=True)
            pv = _dot(e.astype(BF16), v_ref[...]) / den
            o_ref[...] = jnp.where(lane_o < 64, pv[:BLK], pv[BLK:])
            lse_ref[0] = jnp.broadcast_to(m + jnp.log2(den), (2 * BLK, 128))

        odd = i % 2 == 1

        @pl.when(i == 0)
        def _():
            logits(s_even)

        @pl.when(odd & (i < nj))
        def _():
            logits(s_odd)
            finish(s_even)

        @pl.when(jnp.logical_not(odd) & (i > 0) & (i < nj))
        def _():
            logits(s_even)
            finish(s_odd)

        @pl.when(i == nj)
        def _():
            finish(s_even if nj % 2 == 1 else s_odd)

    cur = lambda b, i: b * nj + jnp.minimum(i, nj - 1)
    prev = lambda b, i: b * nj + jnp.maximum(i - 1, 0)
    return _call_with_comm(
        body, comm, grid=(cfg.B, 4, nj + 1), name="mla_fwd",
        in_specs=[pl.BlockSpec((BLK, 256), lambda b, p, i: (cur(b, i), p)), pl.BlockSpec((256, lp), lambda b, p, i: (b * 4 + p, 0)),
                  pl.BlockSpec((lp, 128), lambda b, p, i: (b, p))],
        out_specs=[pl.BlockSpec((BLK, 128), lambda b, p, i: (prev(b, i), p)),
                   pl.BlockSpec((1, 2 * BLK, 128), lambda b, p, i: (p, prev(b, i), 0))],
        out_shape=[jax.ShapeDtypeStruct((cfg.TP, 512), F32), jax.ShapeDtypeStruct((4, 2 * cfg.TP, 128), F32)],
        scratch_shapes=[pltpu.VMEM((2 * BLK, lp), F32), pltpu.VMEM((2 * BLK, lp), F32)],
        args=(q, kt, v))


def _mla_bwd(cfg, q, k, kt, vt, dyb, yb, lse, comm=None):
    nj, lp = cfg.NJ, cfg.LP

    def body(q_ref, k_ref, kt_ref, vt_ref, dy_ref, y_ref, lse_ref, dq_ref, dk_ref, dv_ref):
        i = pl.program_id(2)

        @pl.when(i == 0)
        def _():
            dk_ref[...] = jnp.zeros_like(dk_ref)
            dv_ref[...] = jnp.zeros_like(dv_ref)

        lane_o = _lane((BLK, 128))
        qbd = _pair_blockdiag(q_ref[...])
        dy, y = dy_ref[...], y_ref[...]
        do_s = jnp.concatenate([jnp.where(lane_o < 64, dy, 0.0), jnp.where(lane_o >= 64, dy, 0.0)], axis=0)
        delta = jnp.sum(do_s * jnp.concatenate([y, y], axis=0), axis=1, keepdims=True)
        do_b = do_s.astype(BF16)
        p = jnp.exp2(_dot(qbd, kt_ref[...]) - lse_ref[0][:, :1])
        ds = p * (_dot(do_b, vt_ref[...]) - delta)
        dsb = (ds * LN2).astype(BF16)
        dq2 = _dot(dsb, k_ref[...])
        dq_ref[...] = jnp.where(_lane((BLK, 256)) < 128, dq2[:BLK], dq2[BLK:]) * Q_SCALE
        dk_ref[...] += _dot_tn(dsb, qbd)
        dv_ref[...] += _dot_tn(p.astype(BF16), do_b)

    return _call_with_comm(
        body, comm, grid=(cfg.B, 4, nj), name="mla_bwd",
        in_specs=[pl.BlockSpec((BLK, 256), lambda b, p, i: (b * nj + i, p)), pl.BlockSpec((lp, 256), lambda b, p, i: (b, p)),
                  pl.BlockSpec((256, lp), lambda b, p, i: (b * 4 + p, 0)), pl.BlockSpec((128, lp), lambda b, p, i: (b * 4 + p, 0)),
                  pl.BlockSpec((BLK, 128), lambda b, p, i: (b * nj + i, p)), pl.BlockSpec((BLK, 128), lambda b, p, i: (b * nj + i, p)),
                  pl.BlockSpec((1, 2 * BLK, 128), lambda b, p, i: (p, b * nj + i, 0))],
        out_specs=[pl.BlockSpec((BLK, 256), lambda b, p, i: (b * nj + i, p)), pl.BlockSpec((lp, 256), lambda b, p, i: (b, p)),
                   pl.BlockSpec((lp, 128), lambda b, p, i: (b, p))],
        out_shape=[jax.ShapeDtypeStruct((cfg.TP, 1024), F32), jax.ShapeDtypeStruct((cfg.TP, 1024), F32),
                   jax.ShapeDtypeStruct((cfg.TP, 512), F32)],
        args=(q, k, kt, vt, dyb, yb, lse))


def _loss_bwd(cfg, h, target, gf):
    nj, nb = cfg.NJ, cfg.NB
    tm = 2 * BLK

    def target_block(g):
        return (g // nj) * nb + jnp.maximum(g % nj - 1, 0)

    def body(h_ref, ta_ref, tb_ref, g_ref, dh_ref, loss_ref, dg_ref):
        t = pl.program_id(0)

        @pl.when(t == 0)
        def _():
            loss_ref[...] = jnp.zeros_like(loss_ref)
            dg_ref[...] = jnp.zeros_like(dg_ref)

        g = g_ref[...]
        first = (lax.broadcasted_iota(jnp.int32, (8, 128), 0) == 0) & (_lane((8, 128)) == 0)
        for half, t_ref in enumerate((ta_ref, tb_ref)):
            rows = slice(half * BLK, (half + 1) * BLK)
            real = (2 * t + half) % nj > 0
            xh, r = _rms(h_ref[rows, :])
            err = jnp.where(real, xh * g - t_ref[...], 0.0)
            loss_ref[...] += jnp.where(first, (0.5 / D_MODEL) * jnp.sum(err * err), 0.0)
            dy = err * (1.0 / D_MODEL)
            dg_ref[...] += jnp.sum(dy * xh, axis=0, keepdims=True)
            dh_ref[rows, :] = _rms_bwd(xh, r, dy * g)

    return _pallas(
        body, grid=(cfg.TP // tm,), name="loss_bwd",
        in_specs=[pl.BlockSpec((tm, D_MODEL), lambda t: (t, 0)),
                  pl.BlockSpec((BLK, D_MODEL), lambda t: (target_block(2 * t), 0)),
                  pl.BlockSpec((BLK, D_MODEL), lambda t: (target_block(2 * t + 1), 0)),
                  pl.BlockSpec((1, D_MODEL), lambda t: (0, 0))],
        out_specs=[pl.BlockSpec((tm, D_MODEL), lambda t: (t, 0)), pl.BlockSpec((8, 128), lambda t: (0, 0)),
                   pl.BlockSpec((1, D_MODEL), lambda t: (0, 0))],
        out_shape=[jax.ShapeDtypeStruct((cfg.TP, D_MODEL), F32), jax.ShapeDtypeStruct((8, 128), F32),
                   jax.ShapeDtypeStruct((1, D_MODEL), F32)],
        compiler_params=_cp("arbitrary"),
    )(h, target, target, gf)


def _out_bwd(cfg, dh, ya, yb, pf, goa, gob, wo_p):
    tm = 256

    def body(dh_ref, ya_ref, yb_ref, ga_ref, gb_ref, goa_ref, gob_ref, w_ref,
             dya_ref, dyb_ref, dg_ref, dw_ref, dgoa_ref, dgob_ref):
        @pl.when(pl.program_id(0) == 0)
        def _():
            dw_ref[...] = jnp.zeros_like(dw_ref)
            dgoa_ref[...] = jnp.zeros_like(dgoa_ref)
            dgob_ref[...] = jnp.zeros_like(dgob_ref)

        ga, gb, goa, gob = ga_ref[...], gb_ref[...], goa_ref[...], gob_ref[...]
        xa, ra, xb, rb, sga, sgb, y_a, y_b = _gate_halves(ya_ref[...], yb_ref[...], ga, gb, goa, gob)
        dhb = dh_ref[...].astype(BF16)
        dw_ref[...] += _dot_tn(jnp.concatenate([y_a, y_b], axis=1).astype(BF16), dhb)
        dy = _dot_nt(dhb, w_ref[...])
        for (dyh, x, r, g, sg, go, dy_out, dgo_ref, col) in (
                (dy[:, :512], xa, ra, ga, sga, goa, dya_ref, dgoa_ref, 0), (dy[:, 512:], xb, rb, gb, sgb, gob, dyb_ref, dgob_ref, 512)):
            dn = dyh * (g * sg)
            dg_ref[:, col:col + 512] = (dyh * (x * go) * (sg * (1.0 + g * (1.0 - sg)))).astype(BF16)
            dgo_ref[...] += jnp.sum(dn * x, axis=0, keepdims=True)
            dy_out[...] = _rms_bwd(x, r, dn * go)

    half = lambda c: pl.BlockSpec((tm, 512), lambda i: (i, c))
    vec = pl.BlockSpec((1, 512), lambda i: (0, 0))
    return _pallas(
        body, grid=(cfg.TP // tm,), name="out_bwd",
        in_specs=[pl.BlockSpec((tm, D_MODEL), lambda i: (i, 0)), half(0), half(0), half(0), half(1), vec, vec,
                  pl.BlockSpec((D_MODEL, D_MODEL), lambda i: (0, 0))],
        out_specs=[half(0), half(0), pl.BlockSpec((tm, D_MODEL), lambda i: (i, 0)),
                   pl.BlockSpec((D_MODEL, D_MODEL), lambda i: (0, 0)), vec, vec],
        out_shape=[jax.ShapeDtypeStruct((cfg.TP, 512), F32), jax.ShapeDtypeStruct((cfg.TP, 512), F32),
                   jax.ShapeDtypeStruct((cfg.TP, D_MODEL), BF16), jax.ShapeDtypeStruct((D_MODEL, D_MODEL), F32),
                   jax.ShapeDtypeStruct((1, 512), F32), jax.ShapeDtypeStruct((1, 512), F32)],
        compiler_params=_cp("arbitrary"),
    )(dh, ya, yb, pf, pf, goa, gob, wo_p)


def _lat_bwd(cfg, dq, dk, dv, pf, gq, gkv, wq_p, wkv_p, c_tab, s_tab):
    nj = cfg.NJ

    def body(dq_ref, dk_ref, dv_ref, cq_ref, ckv_ref, gq_ref, gkv_ref, wq_ref, wkv_ref, c_ref, s_ref,
             dl_ref, dwq_ref, dwkv_ref, dgq_ref, dgkv_ref):
        @pl.when((pl.program_id(0) == 0) & (pl.program_id(1) == 0))
        def _():
            dwq_ref[...] = jnp.zeros_like(dwq_ref)
            dwkv_ref[...] = jnp.zeros_like(dwkv_ref)
            dgq_ref[...] = jnp.zeros_like(dgq_ref)
            dgkv_ref[...] = jnp.zeros_like(dgkv_ref)

        c1, s1 = c_ref[...], s_ref[...]
        c8, s8 = jnp.tile(c1, (1, 8)), jnp.tile(s1, (1, 8))
        dq_r = dq_ref[...]
        dqp = (dq_r * c8 + _swap_rope(dq_r * s8)).astype(BF16)
        gq = gq_ref[...]
        xq, rq = _rms(cq_ref[...])
        dwq_ref[...] += _dot_tn((xq * gq).astype(BF16), dqp)
        dn = _dot_nt(dqp, wq_ref[...])
        dgq_ref[...] += jnp.sum(dn * xq, axis=0, keepdims=True)
        dl_ref[:, :256] = _rms_bwd(xq, rq, dn * gq).astype(BF16)

        dk_r = dk_ref[...]
        dkr = dk_r[:, :128]
        for hd in range(1, 8):
            dkr = dkr + dk_r[:, hd * 128:(hd + 1) * 128]
        lane1 = _lane(dkr.shape)
        dkr = jnp.where((lane1 >= 64) & (lane1 < 96), dkr, 0.0)
        dl_ref[:, 384:] = (dkr * c1 + _swap_rope(dkr * s1)).astype(BF16)
        dkv = jnp.concatenate([dk_r, dv_ref[...]], axis=1).astype(BF16)
        gkv = gkv_ref[...]
        xk, rk = _rms(ckv_ref[...])
        dwkv_ref[...] += _dot_tn((xk * gkv).astype(BF16), dkv)
        dn2 = _dot_nt(dkv, wkv_ref[...])
        dgkv_ref[...] += jnp.sum(dn2 * xk, axis=0, keepdims=True)
        dl_ref[:, 256:384] = _rms_bwd(xk, rk, dn2 * gkv).astype(BF16)

    row = lambda b, j: b * nj + j
    const = lambda shape: pl.BlockSpec(shape, lambda b, j: (0, 0))
    return _pallas(
        body, grid=(cfg.B, nj), name="lat_bwd",
        in_specs=[pl.BlockSpec((BLK, 1024), lambda b, j: (row(b, j), 0)), pl.BlockSpec((BLK, 1024), lambda b, j: (row(b, j), 0)),
                  pl.BlockSpec((BLK, 512), lambda b, j: (row(b, j), 0)),
                  pl.BlockSpec((BLK, 256), lambda b, j: (row(b, j), 4)), pl.BlockSpec((BLK, 128), lambda b, j: (row(b, j), 10)),
                  const((1, 256)), const((1, 128)), const((256, 1024)), const((128, 1536)),
                  pl.BlockSpec((BLK, 128), lambda b, j: (j, 0)), pl.BlockSpec((BLK, 128), lambda b, j: (j, 0))],
        out_specs=[pl.BlockSpec((BLK, 512), lambda b, j: (row(b, j), 0)), const((256, 1024)), const((128, 1536)),
                   const((1, 256)), const((1, 128))],
        out_shape=[jax.ShapeDtypeStruct((cfg.TP, 512), BF16), jax.ShapeDtypeStruct((256, 1024), F32),
                   jax.ShapeDtypeStruct((128, 1536), F32), jax.ShapeDtypeStruct((1, 256), F32), jax.ShapeDtypeStruct((1, 128), F32)],
        compiler_params=_cp("arbitrary", "arbitrary"),
    )(dq, dk, dv, pf, pf, gq, gkv, wq_p, wkv_p, c_tab, s_tab)


def _inproj_bwd(cfg, h, g, w_p, dqa, dkva, dgate, dlat, dh, comm=None):
    tm = 256

    def body(h_ref, g_ref, w_ref, dqa_ref, dkva_ref, dg_ref, dl_ref, dh_ref, o_ref, dw_ref, dgn_ref):
        @pl.when(pl.program_id(0) == 0)
        def _():
            dw_ref[...] = jnp.zeros_like(dw_ref)
            dgn_ref[...] = jnp.zeros_like(dgn_ref)

        g = g_ref[...]
        xh, r = _rms(h_ref[...])
        dproj = jnp.concatenate([dqa_ref[...], dkva_ref[...], dg_ref[...], dl_ref[...]], axis=1)
        dw_ref[...] += _dot_tn((xh * g).astype(BF16), dproj)
        du = _dot_nt(dproj, w_ref[...])
        dgn_ref[...] += jnp.sum(du * xh, axis=0, keepdims=True)
        o_ref[...] = dh_ref[...] + _rms_bwd(xh, r, du * g)

    rows = lambda w: pl.BlockSpec((tm, w), lambda i: (i, 0))
    return _call_with_comm(
        body, comm, grid=(cfg.TP // tm,), name="inproj_bwd",
        in_specs=[rows(D_MODEL), pl.BlockSpec((1, D_MODEL), lambda i: (0, 0)), pl.BlockSpec((D_MODEL, W_IN_P), lambda i: (0, 0)),
                  rows(512), rows(256), rows(1024), rows(512), rows(D_MODEL)],
        out_specs=[rows(D_MODEL), pl.BlockSpec((D_MODEL, W_IN_P), lambda i: (0, 0)), pl.BlockSpec((1, D_MODEL), lambda i: (0, 0))],
        out_shape=[jax.ShapeDtypeStruct((cfg.TP, D_MODEL), F32), jax.ShapeDtypeStruct((D_MODEL, W_IN_P), F32),
                   jax.ShapeDtypeStruct((1, D_MODEL), F32)],
        args=(h, g, w_p, dqa, dkva, dgate, dlat, dh))


def _meta_grad(cfg, dh):
    def body(d_ref, o_ref):
        @pl.when(pl.program_id(0) == 0)
        def _():
            o_ref[...] = d_ref[...]

        @pl.when(pl.program_id(0) > 0)
        def _():
            o_ref[...] += d_ref[...]

    return _pallas(
        body, grid=(cfg.B,), name="meta_grad",
        in_specs=[pl.BlockSpec((BLK, D_MODEL), lambda b: (b * cfg.NJ, 0))],
        out_specs=pl.BlockSpec((BLK, D_MODEL), lambda b: (0, 0)),
        out_shape=jax.ShapeDtypeStruct((BLK, D_MODEL), F32),
        compiler_params=_cp("arbitrary"),
    )(dh)


MATRICES = ("w_in", "w_uq", "w_ukv", "w_out")


def _local_grads(cfg, x, target, meta_of, table, small, weight_of, rider=None):
    def ride(stage, i, mats):
        hook = rider(stage, i, mats) if rider else None
        return hook if hook else (None, lambda res: None)

    depth = small["norm_in"].shape[0]
    rel, vis = _window_structure(cfg.NJ)
    bucket = _t5_bucket(jnp.asarray(rel))
    maskadd = jnp.asarray(np.where(vis, 0.0, NEG).astype(np.float32))
    c_tab, s_tab = _rope_tables(cfg)
    comm, deliver = ride("bias_build", 0, {})
    bias, *travelled = _bias_build(table, bucket, maskadd, comm)
    deliver(travelled)

    meta_blk = jnp.concatenate([meta_of(), jnp.zeros((BLK - N_META, D_MODEL), F32)], axis=0)
    h = jnp.concatenate([jnp.broadcast_to(meta_blk[None], (cfg.B, BLK, D_MODEL)), x], axis=1).reshape(cfg.TP, D_MODEL)

    wp, saved = [], []
    for i in range(depth):
        w = dict(w_in=_w_in_to_p(weight_of(i, "w_in")),
                 g_in=small["norm_in"][i][None], gq=small["norm_q_lat"][i][None], gkv=small["norm_kv_lat"][i][None],
                 goa=_perm_heads64(small["norm_out_a"][i], 0)[None], gob=small["norm_out_b"][i][None], sink=small["sink_a"][i])
        wp.append(w)
        comm, deliver = ride("inproj_fwd", i, {})
        pa, pf, *travelled = _inproj_fwd(cfg, h, w["g_in"], w["w_in"], comm)
        deliver(travelled)
        w.update(w_uq=_w_uq_to_p(weight_of(i, "w_uq")), w_ukv=_w_ukv_to_p(weight_of(i, "w_ukv")), w_out=_w_out_to_p(weight_of(i, "w_out")))
        q, k, v, kt, vt = _lat_fwd(cfg, pf, w["gq"], w["gkv"], w["w_uq"], w["w_ukv"], c_tab, s_tab)
        ya, lse_a = _win_fwd(cfg, pa, bias, w["sink"])
        comm, deliver = ride("mla_fwd", i, {})
        yb, lse_b, *travelled = _mla_fwd(cfg, q, kt, v, comm)
        deliver(travelled)
        h_next = _out_fwd(cfg, ya, yb, pf, w["goa"], w["gob"], w["w_out"], h)
        saved.append(dict(h=h, pa=pa, pf=pf, q=q, k=k, kt=kt, vt=vt, ya=ya, lse_a=lse_a, yb=yb, lse_b=lse_b))
        h = h_next

    dh, loss_tile, d_norm_final = _loss_bwd(cfg, h, target.reshape(cfg.B * cfg.S, D_MODEL), small["norm_final"][None])

    grads = {k_: [] for k_ in ("norm_in", "sink_a", "norm_q_lat", "norm_kv_lat", "norm_out_a", "norm_out_b")}
    mats, s_accs = {}, []
    for i in reversed(range(depth)):
        w, sv = wp[i], saved[i]
        dya, dyb, dgate, dwo, dgoa, dgob = _out_bwd(cfg, dh, sv["ya"], sv["yb"], sv["pf"], w["goa"], w["gob"], w["w_out"])
        dqa, dkp, dvp, dkm, dvm, s_acc, dsink = _win_bwd(cfg, sv["pa"], bias, w["sink"], dya, sv["ya"], sv["lse_a"])
        dkva = _win_dkv_combine(cfg, dkp, dvp, dkm, dvm)
        comm, deliver = ride("mla_bwd", i, mats)
        dq, dk, dv, *travelled = _mla_bwd(cfg, sv["q"], sv["k"], sv["kt"], sv["vt"], dyb, sv["yb"], sv["lse_b"], comm)
        deliver(travelled)
        dlat, dwq, dwkv, dgq, dgkv = _lat_bwd(cfg, dq, dk, dv, sv["pf"], w["gq"], w["gkv"], w["w_uq"], w["w_ukv"], c_tab, s_tab)
        mats[i] = dict(w_uq=_w_uq_from_p(dwq), w_ukv=_w_ukv_from_p(dwkv), w_out=_w_out_from_p(dwo))
        comm, deliver = ride("inproj_bwd", i, mats)
        dh, dwin, dgin, *travelled = _inproj_bwd(cfg, sv["h"], w["g_in"], w["w_in"], dqa, dkva, dgate, dlat, dh, comm)
        deliver(travelled)
        s_accs.append(s_acc)
        mats[i]["w_in"] = _w_in_from_p(dwin)
        grads["norm_in"].append(dgin[0])
        grads["sink_a"].append(dsink[0, :A_HEADS])
        grads["norm_q_lat"].append(dgq[0])
        grads["norm_kv_lat"].append(dgkv[0])
        grads["norm_out_a"].append(_unperm_heads64(dgoa[0], 0))
        grads["norm_out_b"].append(dgob[0])

    out = {k_: jnp.stack(v_[::-1]) for k_, v_ in grads.items()}
    mats["meta_tokens"] = _meta_grad(cfg, dh)[:N_META]
    comm, deliver = ride("bias_grad", 0, mats)
    dtable, *travelled = _bias_grad(s_accs, bucket, comm)
    deliver(travelled)
    out["rel_bias_table"] = dtable[:, 0].reshape(N_BUCKETS, A_HEADS)
    out["norm_final"] = d_norm_final[0]
    return loss_tile[0, 0], dh.reshape(cfg.B, cfg.LP, D_MODEL)[:, BLK:], out, mats


MESH = pl.DeviceIdType.MESH
ANY = pl.BlockSpec(memory_space=pl.ANY)


def _place():
    x, y, c = lax.axis_index("x"), lax.axis_index("y"), lax.axis_index("c")
    others = [(1 - x, y), (x, 1 - y), (1 - x, 1 - y)]
    return x, y, c, others


Comm = collections.namedtuple("Comm", "inputs out_shapes scratch start wait")


def _gather_comm(shards):
    n = len(shards)

    def copies(ins, outs, sems, arriving):
        send_sems, recv_sems, local_sems = sems
        x, y, c, others = _place()
        k_me = 2 * x + y
        local = [pltpu.make_async_copy(ins[a], outs[a].at[k_me], local_sems.at[a]) for a in range(n)]
        remote = [pltpu.make_async_remote_copy(src_ref=ins[a], dst_ref=outs[a].at[2 * ox + oy if arriving else k_me],
                                               send_sem=send_sems.at[3 * a + j], recv_sem=recv_sems.at[3 * a + j],
                                               device_id=(ox, oy, c), device_id_type=MESH)
                  for a in range(n) for j, (ox, oy) in enumerate(others)]
        return local, remote

    def start(ins, outs, sems):
        local, sends = copies(ins, outs, sems, arriving=False)
        for cp in local + sends:
            cp.start()

    def wait(ins, outs, sems):
        local, recvs = copies(ins, outs, sems, arriving=True)
        for cp in recvs:
            cp.wait_recv()
        for cp in recvs:
            cp.wait_send()
        for cp in local:
            cp.wait()

    return Comm(list(shards), [jax.ShapeDtypeStruct((4, *s.shape), s.dtype) for s in shards],
                [pltpu.SemaphoreType.DMA((3 * n,)), pltpu.SemaphoreType.DMA((3 * n,)), pltpu.SemaphoreType.DMA((n,))], start, wait)


def _gather_halves_comm(shards):
    n = len(shards)

    def copies(ins, outs, sems, kind):
        send_sems, recv_sems, fwd_send_sems, fwd_recv_sems, local_sems = sems
        x, y, c, others = _place()
        k_me = 2 * x + y

        def half(ref, which):
            rows = ref.shape[0] // 2
            return ref.at[pl.ds(pl.multiple_of(which * rows, 8), rows)]

        if kind == "local":
            return [pltpu.make_async_copy(ins[a], outs[a].at[k_me], local_sems.at[a]) for a in range(n)]
        made = []
        for a in range(n):
            for j, (ox, oy) in enumerate(others):
                slot = outs[a].at[k_me if kind == "sent" else 2 * ox + oy]
                if kind in ("sent", "arrived"):
                    made.append(pltpu.make_async_remote_copy(
                        src_ref=half(ins[a], c), dst_ref=half(slot, c), send_sem=send_sems.at[3 * a + j],
                        recv_sem=recv_sems.at[3 * a + j], device_id=(ox, oy, c), device_id_type=MESH))
                else:
                    which = c if kind == "forward" else 1 - c
                    made.append(pltpu.make_async_remote_copy(
                        src_ref=half(slot, which), dst_ref=half(slot, which), send_sem=fwd_send_sems.at[3 * a + j],
                        recv_sem=fwd_recv_sems.at[3 * a + j], device_id=(x, y, 1 - c), device_id_type=MESH))
        return made

    def start(ins, outs, sems):
        for cp in copies(ins, outs, sems, "local") + copies(ins, outs, sems, "sent"):
            cp.start()

    def wait(ins, outs, sems):
        arrived, forward = copies(ins, outs, sems, "arrived"), copies(ins, outs, sems, "forward")
        for came, on in zip(arrived, forward):
            came.wait_recv()
            on.start()
        for cp in copies(ins, outs, sems, "forwarded"):
            cp.wait_recv()
        for cp in arrived + forward:
            cp.wait_send()
        for cp in copies(ins, outs, sems, "local"):
            cp.wait()

    return Comm(list(shards), [jax.ShapeDtypeStruct((4, *s.shape), s.dtype) for s in shards],
                [pltpu.SemaphoreType.DMA((3 * n,))] * 4 + [pltpu.SemaphoreType.DMA((n,))], start, wait)


def _scatter_comm(parts):
    n = len(parts)

    def copies(ins, outs, sems):
        send_sems, recv_sems = sems
        x, y, c, others = _place()
        return [pltpu.make_async_remote_copy(src_ref=ins[a].at[2 * ox + oy], dst_ref=outs[a].at[j], send_sem=send_sems.at[3 * a + j],
                                             recv_sem=recv_sems.at[3 * a + j], device_id=(ox, oy, c), device_id_type=MESH)
                for a in range(n) for j, (ox, oy) in enumerate(others)]

    def start(ins, outs, sems):
        for cp in copies(ins, outs, sems):
            cp.start()

    def wait(ins, outs, sems):
        cps = copies(ins, outs, sems)
        for cp in cps:
            cp.wait_recv()
        for cp in cps:
            cp.wait_send()

    return Comm(list(parts), [jax.ShapeDtypeStruct((3, *p.shape[1:]), p.dtype) for p in parts],
                [pltpu.SemaphoreType.DMA((3 * n,)), pltpu.SemaphoreType.DMA((3 * n,))], start, wait)


def _call_with_comm(body, comm, *, grid, name, in_specs, out_specs, out_shape, args, scratch_shapes=()):
    if comm is None:
        return _pallas(body, grid=grid, name=name, in_specs=in_specs, out_specs=out_specs, out_shape=out_shape,
                              scratch_shapes=list(scratch_shapes), compiler_params=_cp(*["arbitrary"] * len(grid)))(*args)
    n_in, n_out, ci, co, ns = len(in_specs), len(out_specs), len(comm.inputs), len(comm.out_shapes), len(scratch_shapes)

    def wrapped(*refs):
        ins, cins = refs[:n_in], refs[n_in:n_in + ci]
        outs, couts = refs[n_in + ci:n_in + ci + n_out], refs[n_in + ci + n_out:n_in + ci + n_out + co]
        scratch, sems = refs[n_in + ci + n_out + co:n_in + ci + n_out + co + ns], refs[n_in + ci + n_out + co + ns:]
        ids = [pl.program_id(a) for a in range(len(grid))]
        first = functools.reduce(jnp.logical_and, [i == 0 for i in ids])
        last = functools.reduce(jnp.logical_and, [i == g - 1 for i, g in zip(ids, grid)])

        @pl.when(first)
        def _():
            comm.start(cins, couts, sems)

        body(*ins, *outs, *scratch)

        @pl.when(last)
        def _():
            comm.wait(cins, couts, sems)

    return _pallas(
        wrapped, grid=grid, name=name + "_comm", in_specs=[*in_specs, *[ANY] * ci], out_specs=[*out_specs, *[ANY] * co],
        out_shape=[*out_shape, *comm.out_shapes], scratch_shapes=[*scratch_shapes, *comm.scratch],
        compiler_params=_cp(*["arbitrary"] * len(grid)))(*args, *comm.inputs)


def _swap_sibling(arrs):
    n = len(arrs)

    def body(*refs):
        ins, outs = refs[:n], refs[n:2 * n]
        send_sems, recv_sems = refs[2 * n:]
        x, y, c, _ = _place()
        copies = [pltpu.make_async_remote_copy(src_ref=ins[a], dst_ref=outs[a], send_sem=send_sems.at[a], recv_sem=recv_sems.at[a],
                                               device_id=(x, y, 1 - c), device_id_type=MESH) for a in range(n)]
        for cp in copies:
            cp.start()
        for cp in copies:
            cp.wait_recv()
        for cp in copies:
            cp.wait_send()

    return _pallas(
        body, name="swap_sibling", in_specs=[ANY] * n, out_specs=[ANY] * n,
        out_shape=[jax.ShapeDtypeStruct(a.shape, a.dtype) for a in arrs],
        scratch_shapes=[pltpu.SemaphoreType.DMA((n,)), pltpu.SemaphoreType.DMA((n,))],
    )(*arrs)


def _allreduce_small(v):
    def body(v_ref, o_ref, buf, send_sems, recv_sems):
        x, y, c, _ = _place()
        me = 4 * x + 2 * y + c
        buf[me] = v_ref[...]

        def copy(r):
            tx, ty, tc = (x + (r >> 2)) % 2, (y + ((r >> 1) & 1)) % 2, (c + (r & 1)) % 2
            return tx, ty, tc

        sends = []
        for r in range(1, 8):
            tx, ty, tc = copy(r)
            sends.append(pltpu.make_async_remote_copy(src_ref=v_ref, dst_ref=buf.at[me], send_sem=send_sems.at[r - 1],
                                                      recv_sem=recv_sems.at[r - 1], device_id=(tx, ty, tc), device_id_type=MESH))
        for cp in sends:
            cp.start()
        for r in range(1, 8):
            tx, ty, tc = copy(r)
            pltpu.make_async_remote_copy(src_ref=v_ref, dst_ref=buf.at[4 * tx + 2 * ty + tc], send_sem=send_sems.at[r - 1],
                                         recv_sem=recv_sems.at[r - 1], device_id=(tx, ty, tc), device_id_type=MESH).wait_recv()
        for cp in sends:
            cp.wait_send()
        acc = buf[0]
        for d in range(1, 8):
            acc = acc + buf[d]
        o_ref[...] = acc

    return pl.pallas_call(
        body, name="allreduce_small", in_specs=[pl.BlockSpec(memory_space=pltpu.VMEM)], out_specs=pl.BlockSpec(memory_space=pltpu.VMEM),
        out_shape=jax.ShapeDtypeStruct(v.shape, F32),
        scratch_shapes=[pltpu.VMEM((8, *v.shape), F32), pltpu.SemaphoreType.DMA((7,)), pltpu.SemaphoreType.DMA((7,))],
    )(v)


def _rows_view(a):
    return a.reshape(-1, a.shape[-1])


def _elementwise(name, fn, ins, n_out):
    rows, cols = ins[0].shape
    tm = min(rows, 512)
    spec = pl.BlockSpec((tm, cols), lambda i: (i, 0))

    def body(*refs):
        outs = fn(*[r[...] for r in refs[:len(ins)]])
        for o_ref, o in zip(refs[len(ins):], outs):
            o_ref[...] = o

    return _pallas(
        body, grid=(rows // tm,), name=name, in_specs=[spec] * len(ins), out_specs=[spec] * n_out,
        out_shape=[jax.ShapeDtypeStruct((rows, cols), F32)] * n_out, compiler_params=_cp("parallel"),
    )(*ins)


def _sum_parts(name, own, recv):
    def fn(o, r0, r1, r2):
        return (o + r0.astype(F32) + r1.astype(F32) + r2.astype(F32),)

    return _elementwise("sum_parts_" + name, fn, [own, recv[0], recv[1], recv[2]], 1)[0]


def _adamw(name, w, m, v, g_parts):
    def fn(w_, m_, v_, *gs):
        g = gs[0]
        for extra in gs[1:]:
            g = g + extra
        m_new = ADAM_B1 * m_ + (1.0 - ADAM_B1) * g
        v_new = ADAM_B2 * v_ + (1.0 - ADAM_B2) * (g * g)
        m_hat = m_new / (1.0 - ADAM_B1 ** ADAM_STEP)
        v_hat = v_new / (1.0 - ADAM_B2 ** ADAM_STEP)
        delta = -ADAM_LR * (m_hat / (jnp.sqrt(v_hat) + ADAM_EPS) + ADAM_WD * w_)
        return g, delta, m_new, v_new

    return _elementwise("adamw_" + name, fn, [w, m, v, *g_parts], 4)


MAT_AXIS = {"w_in": 1, "w_uq": 1, "w_ukv": 1, "w_out": 0}
SMALL = ("rel_bias_table", "norm_in", "sink_a", "norm_q_lat", "norm_kv_lat", "norm_out_a", "norm_out_b", "norm_final")
WEIGHTS = ("meta_tokens", "rel_bias_table", "norm_in", "w_in", "sink_a", "norm_q_lat", "w_uq", "norm_kv_lat", "w_ukv",
           "norm_out_a", "norm_out_b", "w_out", "norm_final")
SMALL_ROWS, SMALL_COLS = 8, 1024


def _pack_small(d, loss=None):
    flat = [d[n].reshape(-1) for n in SMALL]
    if loss is not None:
        flat.append(loss.reshape(1))
    used = sum(f.shape[0] for f in flat)
    flat.append(jnp.zeros((SMALL_ROWS * SMALL_COLS - used,), F32))
    return jnp.concatenate(flat).reshape(SMALL_ROWS, SMALL_COLS)


def _unpack_small(p, like):
    flat, out, off = p.reshape(-1), {}, 0
    for n in SMALL:
        size = int(np.prod(like[n].shape))
        out[n] = flat[off:off + size].reshape(like[n].shape)
        off += size
    return out, flat[off]


def _split4(a, axis):
    size = a.shape[axis] // 4
    return jnp.stack([lax.slice_in_dim(a, k * size, (k + 1) * size, axis=axis) for k in range(4)])


def _train_step(cfg, x, target, w, m, v):
    depth = w["w_in"].shape[0]
    rest = tuple(n for n in MATRICES if n != "w_in")
    weights, splits, received = {}, {}, {}

    def gather(i, names, also=(), build=_gather_comm):
        def deliver(res):
            for n, g in zip(names, res):
                weights[i, n] = jnp.concatenate([g[k] for k in range(4)], axis=MAT_AXIS[n])

        return build([w[n][i].astype(BF16) for n in names] + list(also)), deliver

    def scatter(i, names, mats, also=()):
        for n in names:
            splits[i, n] = _split4(mats[i][n], MAT_AXIS[n])

        def deliver(res):
            for n, r in zip(names, res):
                received[i, n] = r

        return _scatter_comm([splits[i, n].astype(BF16) for n in names] + list(also)), deliver

    def rider(stage, i, mats):
        if stage == "bias_build":
            comm, deliver = gather(0, ("w_in",), also=[w["meta_tokens"]], build=_gather_halves_comm)

            def deliver_first(res):
                deliver(res)
                weights["meta"] = jnp.concatenate([res[1][k] for k in range(4)], axis=1)

            return comm, deliver_first
        if stage == "inproj_fwd" and i == 0:
            return gather(0, rest)
        if stage == "mla_fwd" and i + 1 < depth:
            return gather(i + 1, MATRICES)
        if stage == "mla_bwd" and i + 1 < depth:
            return scatter(i + 1, MATRICES, mats)
        if stage == "inproj_bwd" and i == 0:
            return scatter(0, rest, mats)
        if stage == "bias_grad":
            splits["meta"] = _split4(mats["meta_tokens"], 1)
            comm, deliver = scatter(0, ("w_in",), mats, also=[splits["meta"].astype(BF16)])

            def deliver_last(res):
                deliver(res)
                received["meta"] = res[1]

            return comm, deliver_last
        return None

    loss_local, grad_x, g, mats = _local_grads(cfg, x, target, lambda: weights["meta"], w["rel_bias_table"], {n: w[n] for n in SMALL},
                                               lambda i, n: weights[i, n], rider)

    small_sum = _allreduce_small(_pack_small(g, loss_local))
    g_small, loss = _unpack_small(small_sum, {n: w[n] for n in SMALL})

    k_me = 2 * lax.axis_index("x") + lax.axis_index("y")

    def core_sum(name, split, recv):
        own = lax.dynamic_index_in_dim(split, k_me, 0, keepdims=False)
        return _sum_parts(name, _rows_view(own), recv.reshape(3, -1, recv.shape[-1]))

    partial = [core_sum("meta_tokens", splits["meta"], received["meta"])]
    for n in MATRICES:
        partial.append(jnp.concatenate([core_sum(f"{n}_{i}", splits[i, n], received[i, n]) for i in range(depth)], axis=0))
    sibling = _swap_sibling(partial)

    outs = {}
    for n, p_me, p_sib in zip(("meta_tokens", *MATRICES), partial, sibling):
        res = _adamw(n, _rows_view(w[n]), _rows_view(m[n]), _rows_view(v[n]), [p_me, p_sib])
        outs[n] = [r.reshape(w[n].shape) for r in res]
    res = _adamw("small", _pack_small(w), _pack_small(m), _pack_small(v), [_pack_small(g_small)])
    unpacked = [_unpack_small(r, {n: w[n] for n in SMALL})[0] for r in res]
    for n in SMALL:
        outs[n] = [u[n] for u in unpacked]

    result = [loss, grad_x]
    for field in range(4):
        result.extend(outs[n][field] for n in WEIGHTS)
    return tuple(result)


def kernel(x, meta_tokens, rel_bias_table, norm_in, w_in, sink_a, norm_q_lat, w_uq, norm_kv_lat, w_ukv, norm_out_a, norm_out_b, w_out, norm_final, loss_target, m_meta_tokens, m_rel_bias_table, m_norm_in, m_w_in, m_sink_a, m_norm_q_lat, m_w_uq, m_norm_kv_lat, m_w_ukv, m_norm_out_a, m_norm_out_b, m_w_out, m_norm_final, v_meta_tokens, v_rel_bias_table, v_norm_in, v_w_in, v_sink_a, v_norm_q_lat, v_w_uq, v_norm_kv_lat, v_w_ukv, v_norm_out_a, v_norm_out_b, v_w_out, v_norm_final):
    w = dict(zip(WEIGHTS, (meta_tokens, rel_bias_table, norm_in, w_in, sink_a, norm_q_lat, w_uq, norm_kv_lat, w_ukv, norm_out_a, norm_out_b, w_out, norm_final)))
    m = dict(zip(WEIGHTS, (m_meta_tokens, m_rel_bias_table, m_norm_in, m_w_in, m_sink_a, m_norm_q_lat, m_w_uq, m_norm_kv_lat, m_w_ukv, m_norm_out_a, m_norm_out_b, m_w_out, m_norm_final)))
    v = dict(zip(WEIGHTS, (v_meta_tokens, v_rel_bias_table, v_norm_in, v_w_in, v_sink_a, v_norm_q_lat, v_w_uq, v_norm_kv_lat, v_w_ukv, v_norm_out_a, v_norm_out_b, v_w_out, v_norm_final)))
    cfg = make_cfg(x.shape[0], x.shape[1])
    return _train_step(cfg, x, loss_target, w, m, v)
```

```python
import collections
import functools
import math

import jax
import jax.numpy as jnp
import numpy as np
from jax import lax
from jax.experimental import pallas as pl
from jax.experimental.pallas import tpu as pltpu

F32 = jnp.float32
BF16 = jnp.bfloat16

BLK = 128
N_META = 16
D_MODEL = 1024
A_HEADS, A_KV, A_DH = 8, 2, 64
B_HEADS, B_NOPE, B_ROPE, B_DV = 8, 64, 32, 64
Q_RANK, KV_RANK = 256, 128
N_BUCKETS, MAX_DIST = 32, 128
ROPE_THETA = 10000.0
EPS = 1e-6
IN_WIDTH = 2208
W_IN_P = 2304
NEG = -1e30
MASK_LANE = 96
LOG2E = math.log2(math.e)
Q_SCALE = (B_NOPE + B_ROPE) ** -0.5 * LOG2E
QA_SCALE = A_DH ** -0.5 * LOG2E
LN2 = math.log(2.0)
VMEM_LIMIT = 48 * 1024 * 1024

ADAM_LR, ADAM_B1, ADAM_B2, ADAM_EPS, ADAM_WD, ADAM_STEP = 0.001, 0.9, 0.999, 1e-08, 0.01, 10

Cfg = collections.namedtuple("Cfg", "B S NB NJ LP TP")


def make_cfg(batch, seq):
    nb = seq // BLK
    nj = nb + 1
    return Cfg(batch, seq, nb, nj, nj * BLK, batch * nj * BLK)


def _cp(*sem):
    return pltpu.CompilerParams(dimension_semantics=sem, vmem_limit_bytes=VMEM_LIMIT)


def _pallas(body, *, out_shape, **kw):
    pinned = jax.tree.map(lambda s: pltpu.HBM(s.shape, s.dtype), out_shape)
    call = pl.pallas_call(body, out_shape=pinned, **kw)
    return lambda *args: call(*[pltpu.with_memory_space_constraint(a, pltpu.HBM) for a in args])


def _dot(a, b):
    return jnp.dot(a, b, preferred_element_type=F32)


def _dot_nt(a, b):
    return lax.dot_general(a, b, (((1,), (1,)), ((), ())), preferred_element_type=F32)


def _dot_tn(a, b):
    return lax.dot_general(a, b, (((0,), (0,)), ((), ())), preferred_element_type=F32)


def _rms(x, width=None):
    n = x.shape[-1] if width is None else width
    r = lax.rsqrt(jnp.sum(x * x, axis=-1, keepdims=True) * (1.0 / n) + EPS)
    return x * r, r


def _rms_bwd(xhat, r, t):
    n = xhat.shape[-1]
    return r * (t - xhat * (jnp.sum(t * xhat, axis=-1, keepdims=True) * (1.0 / n)))


def _sigmoid(x):
    return 1.0 / (1.0 + jnp.exp(-x))


def _lane(shape):
    return lax.broadcasted_iota(jnp.int32, shape, len(shape) - 1)


def _swap_rope(x):
    n = x.shape[-1]
    lane = _lane(x.shape) % BLK
    up = pltpu.roll(x, n - 16, axis=x.ndim - 1)
    dn = pltpu.roll(x, 16, axis=x.ndim - 1)
    return jnp.where((lane >= 64) & (lane < 80), up, jnp.where((lane >= 80) & (lane < 96), dn, 0.0))


A_ORDER = (0, 4, 1, 5, 2, 6, 3, 7)


def _jtype(j, nj):
    return 0 if j == 0 else 1 if j == 1 else 3 if j == nj - 1 else 2


def _window_structure(nj):
    def pos(blk, r):
        return np.where(blk == 0, r, N_META + (blk - 1) * BLK + r)

    def valid(blk, r):
        return np.where(blk == 0, r < N_META, True)

    r = np.arange(BLK)
    rels, viss = [], []
    for j in (0, 1, 2, nj - 1):
        qpos = pos(j, r)[:, None]
        rel_t, vis_t = [], []
        for s, kb in enumerate((0, j - 1, j, j + 1)):
            slot_ok = (s == 0) or (1 <= kb <= nj - 1)
            kbc = min(max(kb, 0), nj - 1)
            kpos = pos(kbc, r)[None, :]
            rel = kpos - qpos
            v = valid(kbc, r)[None, :] & np.ones((BLK, 1), bool)
            if s > 0:
                v = v & (np.abs(rel) <= BLK)
            rel_t.append(rel)
            vis_t.append(v & slot_ok)
        rels.append(np.concatenate(rel_t, axis=1))
        viss.append(np.concatenate(vis_t, axis=1))
    return np.stack(rels).astype(np.int32), np.stack(viss)


def _t5_bucket(rel):
    nb = N_BUCKETS // 2
    max_exact = nb // 2
    ret = jnp.where(rel > 0, nb, 0)
    n = jnp.abs(rel)
    nf = jnp.maximum(n, 1).astype(jnp.float32)
    large = max_exact + (jnp.log(nf / max_exact) / math.log(MAX_DIST / max_exact) * (nb - max_exact)).astype(jnp.int32)
    large = jnp.minimum(large, nb - 1)
    return ret + jnp.where(n < max_exact, n, large)


def _perm_heads64(a, axis):
    parts = [lax.slice_in_dim(a, h * 64, (h + 1) * 64, axis=axis) for h in A_ORDER]
    return jnp.concatenate(parts, axis=axis)


def _unperm_heads64(a, axis):
    inv = [A_ORDER.index(h) for h in range(8)]
    parts = [lax.slice_in_dim(a, p * 64, (p + 1) * 64, axis=axis) for p in inv]
    return jnp.concatenate(parts, axis=axis)


def _w_in_to_p(w):
    sl = lambda a, b: lax.slice_in_dim(w, a, b, axis=1)
    z = lambda n: jnp.zeros((w.shape[0], n), w.dtype)
    return jnp.concatenate([_perm_heads64(sl(0, 512), 1), sl(512, 768), _perm_heads64(sl(768, 1280), 1), sl(1696, 2208),
                            sl(1280, 1536), sl(1536, 1664), z(64), sl(1664, 1696), z(32)], axis=1)


def _w_in_from_p(g):
    sl = lambda a, b: lax.slice_in_dim(g, a, b, axis=1)
    return jnp.concatenate([_unperm_heads64(sl(0, 512), 1), sl(512, 768), _unperm_heads64(sl(768, 1280), 1),
                            sl(1792, 2048), sl(2048, 2176), sl(2240, 2272), sl(1280, 1792)], axis=1)


def _w_uq_to_p(w):
    z = jnp.zeros((w.shape[0], 32), w.dtype)
    return jnp.concatenate([p for h in range(8) for p in (lax.slice_in_dim(w, h * 96, (h + 1) * 96, axis=1), z)], axis=1)


def _w_uq_from_p(g):
    return jnp.concatenate([lax.slice_in_dim(g, h * 128, h * 128 + 96, axis=1) for h in range(8)], axis=1)


def _w_ukv_to_p(w):
    z = jnp.zeros((w.shape[0], 64), w.dtype)
    ks = [p for h in range(8) for p in (lax.slice_in_dim(w, h * 128, h * 128 + 64, axis=1), z)]
    vs = [lax.slice_in_dim(w, h * 128 + 64, (h + 1) * 128, axis=1) for h in range(8)]
    return jnp.concatenate(ks + vs, axis=1)


def _w_ukv_from_p(g):
    parts = []
    for h in range(8):
        parts.append(lax.slice_in_dim(g, h * 128, h * 128 + 64, axis=1))
        parts.append(lax.slice_in_dim(g, 1024 + h * 64, 1024 + (h + 1) * 64, axis=1))
    return jnp.concatenate(parts, axis=1)


def _w_out_to_p(w):
    return jnp.concatenate([_perm_heads64(lax.slice_in_dim(w, 0, 512, axis=0), 0), lax.slice_in_dim(w, 512, 1024, axis=0)], axis=0)


def _w_out_from_p(g):
    return jnp.concatenate([_unperm_heads64(lax.slice_in_dim(g, 0, 512, axis=0), 0), lax.slice_in_dim(g, 512, 1024, axis=0)], axis=0)


def _rope_tables(cfg):
    half = B_ROPE // 2
    length = N_META + cfg.S
    freqs = ROPE_THETA ** (-jnp.arange(half, dtype=jnp.float32) / half)
    ang = jnp.arange(length, dtype=jnp.float32)[:, None] * freqs[None, :]
    cos, sin = jnp.cos(ang), jnp.sin(ang)

    def rows(t):
        return jnp.concatenate([t[:N_META], jnp.zeros((BLK - N_META, t.shape[1]), t.dtype), t[N_META:]], axis=0)

    ones = jnp.ones((length, 64), F32)
    zer = jnp.zeros((length, 32), F32)
    c_tab = rows(jnp.concatenate([ones, cos, cos, zer], axis=1))
    s_tab = rows(jnp.concatenate([zer, zer, -sin, sin, zer], axis=1))
    return c_tab, s_tab


def _inproj_fwd(cfg, h, g, w_p, comm=None):
    tm = 256

    def body(h_ref, g_ref, w_ref, pa_ref, pf_ref):
        xh, _ = _rms(h_ref[...])
        u = (xh * g_ref[...]).astype(BF16)
        acc = _dot(u, w_ref[...])
        pa_ref[:, :512] = (acc[:, :512] * QA_SCALE).astype(BF16)
        pa_ref[:, 512:] = acc[:, 512:768].astype(BF16)
        pf_ref[...] = acc[:, 768:]

    return _call_with_comm(
        body, comm, grid=(cfg.TP // tm,), name="inproj_fwd",
        in_specs=[pl.BlockSpec((tm, D_MODEL), lambda i: (i, 0)), pl.BlockSpec((1, D_MODEL), lambda i: (0, 0)),
                  pl.BlockSpec((D_MODEL, W_IN_P), lambda i: (0, 0))],
        out_specs=[pl.BlockSpec((tm, 768), lambda i: (i, 0)), pl.BlockSpec((tm, 1536), lambda i: (i, 0))],
        out_shape=[jax.ShapeDtypeStruct((cfg.TP, 768), BF16), jax.ShapeDtypeStruct((cfg.TP, 1536), F32)],
        args=(h, g, w_p))


def _lat_fwd(cfg, pf, gq, gkv, wq_p, wkv_p, c_tab, s_tab):
    nj = cfg.NJ

    def body(cq_ref, ckv_ref, kr_ref, gq_ref, gkv_ref, wq_ref, wkv_ref, c_ref, s_ref, q_ref, k_ref, v_ref, kt_ref, vt_ref):
        c1, s1 = c_ref[...], s_ref[...]
        c8, s8 = jnp.tile(c1, (1, 8)), jnp.tile(s1, (1, 8))
        mask_lane = _lane((BLK, 1024)) % BLK == MASK_LANE
        zero_row = (pl.program_id(1) == 0) & (lax.broadcasted_iota(jnp.int32, (BLK, 1024), 0) >= N_META)
        xq, _ = _rms(cq_ref[...])
        qp = _dot((xq * gq_ref[...]).astype(BF16), wq_ref[...])
        q_ref[...] = jnp.where(mask_lane, 1.0, (qp * c8 + _swap_rope(qp) * s8) * Q_SCALE).astype(BF16)
        xk, _ = _rms(ckv_ref[...])
        kvp = _dot((xk * gkv_ref[...]).astype(BF16), wkv_ref[...])
        kr = kr_ref[...]
        krr = kr * c1 + _swap_rope(kr) * s1
        k = jnp.where(mask_lane & zero_row, NEG, kvp[:, :1024] + jnp.tile(krr, (1, 8)))
        k_ref[...] = k.astype(BF16)
        v_ref[...] = kvp[:, 1024:].astype(BF16)
        kt_ref[...] = k.T.astype(BF16)
        vt_ref[...] = kvp[:, 1024:].T.astype(BF16)

    row = lambda b, j: b * nj + j
    return _pallas(
        body, grid=(cfg.B, nj), name="lat_fwd",
        in_specs=[pl.BlockSpec((BLK, 256), lambda b, j: (row(b, j), 4)), pl.BlockSpec((BLK, 128), lambda b, j: (row(b, j), 10)),
                  pl.BlockSpec((BLK, 128), lambda b, j: (row(b, j), 11)),
                  pl.BlockSpec((1, 256), lambda b, j: (0, 0)), pl.BlockSpec((1, 128), lambda b, j: (0, 0)),
                  pl.BlockSpec((256, 1024), lambda b, j: (0, 0)), pl.BlockSpec((128, 1536), lambda b, j: (0, 0)),
                  pl.BlockSpec((BLK, 128), lambda b, j: (j, 0)), pl.BlockSpec((BLK, 128), lambda b, j: (j, 0))],
        out_specs=[pl.BlockSpec((BLK, 1024), lambda b, j: (row(b, j), 0)), pl.BlockSpec((BLK, 1024), lambda b, j: (row(b, j), 0)),
                   pl.BlockSpec((BLK, 512), lambda b, j: (row(b, j), 0)),
                   pl.BlockSpec((1024, BLK), lambda b, j: (b, j)), pl.BlockSpec((512, BLK), lambda b, j: (b, j))],
        out_shape=[jax.ShapeDtypeStruct((cfg.TP, 1024), BF16), jax.ShapeDtypeStruct((cfg.TP, 1024), BF16),
                   jax.ShapeDtypeStruct((cfg.TP, 512), BF16),
                   jax.ShapeDtypeStruct((cfg.B * 1024, cfg.LP), BF16), jax.ShapeDtypeStruct((cfg.B * 512, cfg.LP), BF16)],
        compiler_params=_cp("parallel", "parallel"),
    )(pf, pf, pf, gq, gkv, wq_p, wkv_p, c_tab, s_tab)


def _gate_halves(ya, yb, ga, gb, goa, gob):
    xa, ra = _rms(ya)
    xb, rb = _rms(yb)
    sga, sgb = _sigmoid(ga), _sigmoid(gb)
    return xa, ra, xb, rb, sga, sgb, xa * goa * (ga * sga), xb * gob * (gb * sgb)


def _out_fwd(cfg, ya, yb, pf, goa, gob, wo_p, h):
    tm = 256

    def body(ya_ref, yb_ref, ga_ref, gb_ref, goa_ref, gob_ref, w_ref, h_ref, o_ref):
        *_, y_a, y_b = _gate_halves(ya_ref[...], yb_ref[...], ga_ref[...], gb_ref[...], goa_ref[...], gob_ref[...])
        y = jnp.concatenate([y_a, y_b], axis=1).astype(BF16)
        o_ref[...] = h_ref[...] + _dot(y, w_ref[...])

    return _pallas(
        body, grid=(cfg.TP // tm,), name="out_fwd",
        in_specs=[pl.BlockSpec((tm, 512), lambda i: (i, 0)), pl.BlockSpec((tm, 512), lambda i: (i, 0)),
                  pl.BlockSpec((tm, 512), lambda i: (i, 0)), pl.BlockSpec((tm, 512), lambda i: (i, 1)),
                  pl.BlockSpec((1, 512), lambda i: (0, 0)), pl.BlockSpec((1, 512), lambda i: (0, 0)),
                  pl.BlockSpec((D_MODEL, D_MODEL), lambda i: (0, 0)), pl.BlockSpec((tm, D_MODEL), lambda i: (i, 0))],
        out_specs=pl.BlockSpec((tm, D_MODEL), lambda i: (i, 0)),
        out_shape=jax.ShapeDtypeStruct((cfg.TP, D_MODEL), F32),
        compiler_params=_cp("parallel"),
    )(ya, yb, pf, pf, goa, gob, wo_p, h)


def _bias_build(table, bucket, maskadd, comm=None):
    def body(tab_ref, bk_ref, ma_ref, o_ref):
        def rows(g, carry):
            r = pl.ds(pl.multiple_of(g * 8, 8), 8)
            bk = bk_ref[0, r, :]
            accs = [jnp.zeros(bk.shape, F32)] * A_HEADS
            for b in range(N_BUCKETS):
                hit = bk == b
                accs = [jnp.where(hit, tab_ref[b, h], accs[h]) for h in range(A_HEADS)]
            ma = ma_ref[0, r, :]
            for h in range(A_HEADS):
                o_ref[0, h, r, :] = (accs[h] + ma) * LOG2E
            return carry

        lax.fori_loop(0, BLK // 8, rows, 0)

    return _call_with_comm(
        body, comm, grid=(4,), name="bias_build",
        in_specs=[pl.BlockSpec(memory_space=pltpu.SMEM), pl.BlockSpec((1, BLK, 512), lambda t: (t, 0, 0)),
                  pl.BlockSpec((1, BLK, 512), lambda t: (t, 0, 0))],
        out_specs=[pl.BlockSpec((1, A_HEADS, BLK, 512), lambda t: (t, 0, 0, 0))],
        out_shape=[jax.ShapeDtypeStruct((4, A_HEADS, BLK, 512), F32)],
        args=(table, bucket, maskadd))


def _bias_grad(s_accs, bucket, comm=None):
    depth = len(s_accs)

    def body(*refs):
        s_refs, bk_ref, o_ref, sum_ref, part_ref = refs[:depth], refs[depth], refs[depth + 1], refs[depth + 2], refs[depth + 3]
        t = pl.program_id(0)

        @pl.when(t == 0)
        def _():
            o_ref[...] = jnp.zeros_like(o_ref)

        total = s_refs[0][0]
        for extra in s_refs[1:]:
            total = total + extra[0]
        sum_ref[...] = total

        def step(b, carry):
            accs = [jnp.zeros((8, 512), F32) for _ in range(A_HEADS)]
            for g in range(BLK // 8):
                rows = pl.ds(g * 8, 8)
                hit = bk_ref[0, rows, :] == b
                for h in range(A_HEADS):
                    accs[h] = accs[h] + jnp.where(hit, sum_ref[h, rows, :], 0.0)
            rows8 = jnp.concatenate([jnp.sum(a, axis=0, keepdims=True) for a in accs], axis=0)
            part_ref[pl.ds(pl.multiple_of(b * A_HEADS, 8), A_HEADS), :] = rows8
            return carry

        lax.fori_loop(0, N_BUCKETS, step, 0)
        o_ref[...] += jnp.broadcast_to(jnp.sum(part_ref[...], axis=1, keepdims=True), o_ref.shape)

    s_spec = pl.BlockSpec((1, A_HEADS, BLK, 512), lambda t: (t, 0, 0, 0))
    return _call_with_comm(
        body, comm, grid=(4,), name="bias_grad",
        in_specs=[s_spec] * depth + [pl.BlockSpec((1, BLK, 512), lambda t: (t, 0, 0))],
        out_specs=[pl.BlockSpec((N_BUCKETS * A_HEADS, 128), lambda t: (0, 0))],
        out_shape=[jax.ShapeDtypeStruct((N_BUCKETS * A_HEADS, 128), F32)],
        scratch_shapes=[pltpu.VMEM((A_HEADS, BLK, 512), F32), pltpu.VMEM((N_BUCKETS * A_HEADS, 512), F32)],
        args=(*s_accs, bucket))


def _win_specs(cfg):
    nj = cfg.NJ
    row = lambda b, j: b * nj + j
    jt = lambda j: jnp.where(j == 0, 0, jnp.where(j == 1, 1, jnp.where(j == nj - 1, 3, 2)))
    slot_rows = [lambda b, j: row(b, 0), lambda b, j: row(b, jnp.maximum(j - 1, 0)), lambda b, j: row(b, j),
                 lambda b, j: row(b, jnp.minimum(j + 1, nj - 1))]
    k_specs = [pl.BlockSpec((BLK, 128), functools.partial(lambda b, j, f: (f(b, j), 4), f=f)) for f in slot_rows]
    v_specs = [pl.BlockSpec((BLK, 128), functools.partial(lambda b, j, f: (f(b, j), 5), f=f)) for f in slot_rows]
    q_spec = pl.BlockSpec((BLK, 512), lambda b, j: (row(b, j), 0))
    bias_spec = pl.BlockSpec((1, A_HEADS, BLK, 512), lambda b, j: (jt(j), 0, 0, 0))
    return row, jt, q_spec, k_specs, v_specs, bias_spec


def _stack4(ref):
    return jnp.concatenate([ref[:, c * 128:(c + 1) * 128] for c in range(4)], axis=0)


def _win_keys(k_refs, v_refs):
    k4 = jnp.concatenate([r[...] for r in k_refs], axis=0)
    v4 = jnp.concatenate([r[...] for r in v_refs], axis=0)
    lane_k = _lane(k4.shape)
    return (jnp.where(lane_k < 64, k4, jnp.zeros_like(k4)), jnp.where(lane_k >= 64, k4, jnp.zeros_like(k4))), v4


def _sink_col(sink_ref, hf):
    rowi = lax.broadcasted_iota(jnp.int32, (4 * BLK, 1), 0)
    col = jnp.full((4 * BLK, 1), sink_ref[4 * hf + 3], F32)
    for c in (2, 1, 0):
        col = jnp.where(rowi < (c + 1) * BLK, sink_ref[4 * hf + c], col)
    return col * LOG2E


def _win_fwd(cfg, pa, bias, sink):
    row, jt, q_spec, k_specs, v_specs, bias_spec = _win_specs(cfg)

    def body(sink_ref, q_ref, k0, k1, k2, k3, v0, v1, v2, v3, b_ref, o_ref, lse_ref):
        kk, v4 = _win_keys((k0, k1, k2, k3), (v0, v1, v2, v3))
        qs = _stack4(q_ref)
        lane_o = _lane((4 * BLK, 128))
        outs, lses = [], []
        for hf in range(2):
            s = _dot_nt(qs, kk[hf]) + b_ref[0, 4 * hf:4 * hf + 4].reshape(4 * BLK, 512)
            sink_col = _sink_col(sink_ref, hf)
            m = jnp.maximum(jnp.max(s, axis=1, keepdims=True), sink_col)
            e = jnp.exp2(s - m)
            den = jnp.sum(e, axis=1, keepdims=True) + jnp.exp2(sink_col - m)
            outs.append(_dot(e.astype(BF16), v4) / den)
            lses.append(m + jnp.log2(den))
        o = jnp.where(lane_o < 64, outs[0], outs[1])
        for c in range(4):
            o_ref[:, c * 128:(c + 1) * 128] = o[c * BLK:(c + 1) * BLK]
        lse_ref[...] = jnp.where(lane_o == 0, lses[0], jnp.where(lane_o == 1, lses[1], 0.0))

    return _pallas(
        body, grid=(cfg.B, cfg.NJ), name="win_fwd",
        in_specs=[pl.BlockSpec(memory_space=pltpu.SMEM), q_spec, *k_specs, *v_specs, bias_spec],
        out_specs=[pl.BlockSpec((BLK, 512), lambda b, j: (row(b, j), 0)), pl.BlockSpec((4 * BLK, 128), lambda b, j: (row(b, j), 0))],
        out_shape=[jax.ShapeDtypeStruct((cfg.TP, 512), F32), jax.ShapeDtypeStruct((4 * cfg.TP, 128), F32)],
        compiler_params=_cp("parallel", "parallel"),
    )(sink, pa, *([pa] * 8), bias)


def _win_bwd(cfg, pa, bias, sink, dya, ya, lse):
    row, jt, q_spec, k_specs, v_specs, bias_spec = _win_specs(cfg)
    nj = cfg.NJ

    def body(sink_ref, q_ref, k0, k1, k2, k3, v0, v1, v2, v3, b_ref, dy_ref, y_ref, lse_ref,
             dq_ref, dkp_ref, dvp_ref, dkm_ref, dvm_ref, s_ref, dsink_ref):
        j = pl.program_id(1)
        kind = jt(j)

        @pl.when((pl.program_id(0) == 0) & (j == 0))
        def _():
            s_ref[...] = jnp.zeros_like(s_ref)

        kk, v4 = _win_keys((k0, k1, k2, k3), (v0, v1, v2, v3))
        qs, dys, ys = _stack4(q_ref), _stack4(dy_ref), _stack4(y_ref)
        lane_o = _lane((4 * BLK, 128))
        half = (lane_o < 64, lane_o >= 64)
        lse_blk = lse_ref[...]
        dq = jnp.zeros((4 * BLK, 128), F32)
        dk4 = jnp.zeros((512, 128), F32)
        dv4 = jnp.zeros((512, 128), F32)
        dsink = jnp.zeros((8, 128), F32)
        lane_s = _lane((8, 128))
        row_s = lax.broadcasted_iota(jnp.int32, (8, 128), 0)
        for hf in range(2):
            lse_h = jnp.sum(jnp.where(lane_o == hf, lse_blk, 0.0), axis=1, keepdims=True)
            s = _dot_nt(qs, kk[hf]) + b_ref[0, 4 * hf:4 * hf + 4].reshape(4 * BLK, 512)
            p = jnp.exp2(s - lse_h)
            do_h = jnp.where(half[hf], dys, 0.0)
            delta = jnp.sum(do_h * ys, axis=1, keepdims=True)
            do_b = do_h.astype(BF16)
            ds = p * (_dot_nt(do_b, v4) - delta)
            s_ref[kind, 4 * hf:4 * hf + 4] += ds.reshape(4, BLK, 512)
            sink_grad = jnp.exp2(_sink_col(sink_ref, hf) - lse_h) * delta
            for c in range(4):
                tot = -jnp.sum(sink_grad[c * BLK:(c + 1) * BLK])
                dsink = jnp.where((row_s == 0) & (lane_s == 4 * hf + c), tot, dsink)
            dsb = (ds * LN2).astype(BF16)
            dq = dq + _dot(dsb, kk[hf])
            dk4 = dk4 + _dot_tn(dsb, jnp.where(half[hf], qs, jnp.zeros_like(qs)))
            dv4 = dv4 + _dot_tn(p.astype(BF16), do_b)
        for c in range(4):
            dq_ref[:, c * 128:(c + 1) * 128] = (dq[c * BLK:(c + 1) * BLK] * QA_SCALE).astype(BF16)
        dkp_ref[0] = dk4
        dvp_ref[0] = dv4

        @pl.when(j == 0)
        def _():
            dkm_ref[...] = dk4[:BLK]
            dvm_ref[...] = dv4[:BLK]

        @pl.when(j > 0)
        def _():
            dkm_ref[...] += dk4[:BLK]
            dvm_ref[...] += dv4[:BLK]

        @pl.when((pl.program_id(0) == 0) & (j == 0))
        def _():
            dsink_ref[...] = dsink

        @pl.when((pl.program_id(0) > 0) | (j > 0))
        def _():
            dsink_ref[...] += dsink

    blk_row = pl.BlockSpec((BLK, 512), lambda b, j: (row(b, j), 0))
    return _pallas(
        body, grid=(cfg.B, nj), name="win_bwd",
        in_specs=[pl.BlockSpec(memory_space=pltpu.SMEM), q_spec, *k_specs, *v_specs, bias_spec, blk_row, blk_row,
                  pl.BlockSpec((4 * BLK, 128), lambda b, j: (row(b, j), 0))],
        out_specs=[blk_row,
                   pl.BlockSpec((1, 512, 128), lambda b, j: (row(b, j), 0, 0)), pl.BlockSpec((1, 512, 128), lambda b, j: (row(b, j), 0, 0)),
                   pl.BlockSpec((BLK, 128), lambda b, j: (b, 0)), pl.BlockSpec((BLK, 128), lambda b, j: (b, 0)),
                   pl.BlockSpec((4, A_HEADS, BLK, 512), lambda b, j: (0, 0, 0, 0)),
                   pl.BlockSpec((8, 128), lambda b, j: (0, 0))],
        out_shape=[jax.ShapeDtypeStruct((cfg.TP, 512), BF16),
                   jax.ShapeDtypeStruct((cfg.B * nj, 512, 128), F32), jax.ShapeDtypeStruct((cfg.B * nj, 512, 128), F32),
                   jax.ShapeDtypeStruct((cfg.B * BLK, 128), F32), jax.ShapeDtypeStruct((cfg.B * BLK, 128), F32),
                   jax.ShapeDtypeStruct((4, A_HEADS, BLK, 512), F32),
                   jax.ShapeDtypeStruct((8, 128), F32)],
        compiler_params=_cp("arbitrary", "arbitrary"),
    )(sink, pa, *([pa] * 8), bias, dya, ya, lse)


def _win_dkv_combine(cfg, dkp, dvp, dkm, dvm):
    nj = cfg.NJ

    def body(kp, vp, km, vm, o_ref):
        o_ref[:BLK, :128] = km[...].astype(BF16)
        o_ref[:BLK, 128:] = vm[...].astype(BF16)
        for kb in range(1, nj):
            for col, part in ((0, kp), (128, vp)):
                tot = part[kb, 2 * BLK:3 * BLK] + part[kb - 1, 3 * BLK:4 * BLK]
                if kb + 1 < nj:
                    tot = tot + part[kb + 1, BLK:2 * BLK]
                o_ref[kb * BLK:(kb + 1) * BLK, col:col + 128] = tot.astype(BF16)

    return _pallas(
        body, grid=(cfg.B,), name="win_dkv_combine",
        in_specs=[pl.BlockSpec((nj, 512, 128), lambda b: (b, 0, 0)), pl.BlockSpec((nj, 512, 128), lambda b: (b, 0, 0)),
                  pl.BlockSpec((BLK, 128), lambda b: (b, 0)), pl.BlockSpec((BLK, 128), lambda b: (b, 0))],
        out_specs=pl.BlockSpec((cfg.LP, 256), lambda b: (b, 0)),
        out_shape=jax.ShapeDtypeStruct((cfg.TP, 256), BF16),
        compiler_params=_cp("parallel"),
    )(dkp, dvp, dkm, dvm)


def _pair_blockdiag(q):
    lane = _lane(q.shape)
    return jnp.concatenate([jnp.where(lane < 128, q, jnp.zeros_like(q)), jnp.where(lane >= 128, q, jnp.zeros_like(q))], axis=0)


def _mla_fwd(cfg, q, kt, v, comm=None):
    nj, lp = cfg.NJ, cfg.LP

    def body(q_ref, kt_ref, v_ref, o_ref, lse_ref, s_even, s_odd):
        i = pl.program_id(2)
        lane_o = _lane((BLK, 128))

        def logits(s_write):
            s_write[...] = _dot(_pair_blockdiag(q_ref[...]), kt_ref[...])

        def finish(s_read):
            s = s_read[...]
            m = jnp.max(s, axis=1, keepdims=True)
            e = jnp.exp2(s - m)
            den = jnp.sum(e, axis=1, keepdims=True)
            pv = _dot(e.astype(BF16), v_ref[...]) / den
            o_ref[...] = jnp.where(lane_o < 64, pv[:BLK], pv[BLK:])
            lse_ref[0] = jnp.broadcast_to(m + jnp.log2(den), (2 * BLK, 128))

        odd = i % 2 == 1

        @pl.when(i == 0)
        def _():
            logits(s_even)

        @pl.when(odd & (i < nj))
        def _():
            logits(s_odd)
            finish(s_even)

        @pl.when(jnp.logical_not(odd) & (i > 0) & (i < nj))
        def _():
            logits(s_even)
            finish(s_odd)

        @pl.when(i == nj)
        def _():
            finish(s_even if nj % 2 == 1 else s_odd)

    cur = lambda b, i: b * nj + jnp.minimum(i, nj - 1)
    prev = lambda b, i: b * nj + jnp.maximum(i - 1, 0)
    return _call_with_comm(
        body, comm, grid=(cfg.B, 4, nj + 1), name="mla_fwd",
        in_specs=[pl.BlockSpec((BLK, 256), lambda b, p, i: (cur(b, i), p)), pl.BlockSpec((256, lp), lambda b, p, i: (b * 4 + p, 0)),
                  pl.BlockSpec((lp, 128), lambda b, p, i: (b, p))],
        out_specs=[pl.BlockSpec((BLK, 128), lambda b, p, i: (prev(b, i), p)),
                   pl.BlockSpec((1, 2 * BLK, 128), lambda b, p, i: (p, prev(b, i), 0))],
        out_shape=[jax.ShapeDtypeStruct((cfg.TP, 512), F32), jax.ShapeDtypeStruct((4, 2 * cfg.TP, 128), F32)],
        scratch_shapes=[pltpu.VMEM((2 * BLK, lp), F32), pltpu.VMEM((2 * BLK, lp), F32)],
        args=(q, kt, v))


def _mla_bwd(cfg, q, k, kt, vt, dyb, yb, lse, comm=None):
    nj, lp = cfg.NJ, cfg.LP

    def body(q_ref, k_ref, kt_ref, vt_ref, dy_ref, y_ref, lse_ref, dq_ref, dk_ref, dv_ref):
        i = pl.program_id(2)

        @pl.when(i == 0)
        def _():
            dk_ref[...] = jnp.zeros_like(dk_ref)
            dv_ref[...] = jnp.zeros_like(dv_ref)

        lane_o = _lane((BLK, 128))
        qbd = _pair_blockdiag(q_ref[...])
        dy, y = dy_ref[...], y_ref[...]
        do_s = jnp.concatenate([jnp.where(lane_o < 64, dy, 0.0), jnp.where(lane_o >= 64, dy, 0.0)], axis=0)
        delta = jnp.sum(do_s * jnp.concatenate([y, y], axis=0), axis=1, keepdims=True)
        do_b = do_s.astype(BF16)
        p = jnp.exp2(_dot(qbd, kt_ref[...]) - lse_ref[0][:, :1])
        ds = p * (_dot(do_b, vt_ref[...]) - delta)
        dsb = (ds * LN2).astype(BF16)
        dq2 = _dot(dsb, k_ref[...])
        dq_ref[...] = jnp.where(_lane((BLK, 256)) < 128, dq2[:BLK], dq2[BLK:]) * Q_SCALE
        dk_ref[...] += _dot_tn(dsb, qbd)
        dv_ref[...] += _dot_tn(p.astype(BF16), do_b)

    return _call_with_comm(
        body, comm, grid=(cfg.B, 4, nj), name="mla_bwd",
        in_specs=[pl.BlockSpec((BLK, 256), lambda b, p, i: (b * nj + i, p)), pl.BlockSpec((lp, 256), lambda b, p, i: (b, p)),
                  pl.BlockSpec((256, lp), lambda b, p, i: (b * 4 + p, 0)), pl.BlockSpec((128, lp), lambda b, p, i: (b * 4 + p, 0)),
                  pl.BlockSpec((BLK, 128), lambda b, p, i: (b * nj + i, p)), pl.BlockSpec((BLK, 128), lambda b, p, i: (b * nj + i, p)),
                  pl.BlockSpec((1, 2 * BLK, 128), lambda b, p, i: (p, b * nj + i, 0))],
        out_specs=[pl.BlockSpec((BLK, 256), lambda b, p, i: (b * nj + i, p)), pl.BlockSpec((lp, 256), lambda b, p, i: (b, p)),
                   pl.BlockSpec((lp, 128), lambda b, p, i: (b, p))],
        out_shape=[jax.ShapeDtypeStruct((cfg.TP, 1024), F32), jax.ShapeDtypeStruct((cfg.TP, 1024), F32),
                   jax.ShapeDtypeStruct((cfg.TP, 512), F32)],
        args=(q, k, kt, vt, dyb, yb, lse))


def _loss_bwd(cfg, h, target, gf):
    nj, nb = cfg.NJ, cfg.NB
    tm = 2 * BLK

    def target_block(g):
        return (g // nj) * nb + jnp.maximum(g % nj - 1, 0)

    def body(h_ref, ta_ref, tb_ref, g_ref, dh_ref, loss_ref, dg_ref):
        t = pl.program_id(0)

        @pl.when(t == 0)
        def _():
            loss_ref[...] = jnp.zeros_like(loss_ref)
            dg_ref[...] = jnp.zeros_like(dg_ref)

        g = g_ref[...]
        first = (lax.broadcasted_iota(jnp.int32, (8, 128), 0) == 0) & (_lane((8, 128)) == 0)
        for half, t_ref in enumerate((ta_ref, tb_ref)):
            rows = slice(half * BLK, (half + 1) * BLK)
            real = (2 * t + half) % nj > 0
            xh, r = _rms(h_ref[rows, :])
            err = jnp.where(real, xh * g - t_ref[...], 0.0)
            loss_ref[...] += jnp.where(first, (0.5 / D_MODEL) * jnp.sum(err * err), 0.0)
            dy = err * (1.0 / D_MODEL)
            dg_ref[...] += jnp.sum(dy * xh, axis=0, keepdims=True)
            dh_ref[rows, :] = _rms_bwd(xh, r, dy * g)

    return _pallas(
        body, grid=(cfg.TP // tm,), name="loss_bwd",
        in_specs=[pl.BlockSpec((tm, D_MODEL), lambda t: (t, 0)),
                  pl.BlockSpec((BLK, D_MODEL), lambda t: (target_block(2 * t), 0)),
                  pl.BlockSpec((BLK, D_MODEL), lambda t: (target_block(2 * t + 1), 0)),
                  pl.BlockSpec((1, D_MODEL), lambda t: (0, 0))],
        out_specs=[pl.BlockSpec((tm, D_MODEL), lambda t: (t, 0)), pl.BlockSpec((8, 128), lambda t: (0, 0)),
                   pl.BlockSpec((1, D_MODEL), lambda t: (0, 0))],
        out_shape=[jax.ShapeDtypeStruct((cfg.TP, D_MODEL), F32), jax.ShapeDtypeStruct((8, 128), F32),
                   jax.ShapeDtypeStruct((1, D_MODEL), F32)],
        compiler_params=_cp("arbitrary"),
    )(h, target, target, gf)


def _out_bwd(cfg, dh, ya, yb, pf, goa, gob, wo_p):
    tm = 256

    def body(dh_ref, ya_ref, yb_ref, ga_ref, gb_ref, goa_ref, gob_ref, w_ref,
             dya_ref, dyb_ref, dg_ref, dw_ref, dgoa_ref, dgob_ref):
        @pl.when(pl.program_id(0) == 0)
        def _():
            dw_ref[...] = jnp.zeros_like(dw_ref)
            dgoa_ref[...] = jnp.zeros_like(dgoa_ref)
            dgob_ref[...] = jnp.zeros_like(dgob_ref)

        ga, gb, goa, gob = ga_ref[...], gb_ref[...], goa_ref[...], gob_ref[...]
        xa, ra, xb, rb, sga, sgb, y_a, y_b = _gate_halves(ya_ref[...], yb_ref[...], ga, gb, goa, gob)
        dhb = dh_ref[...].astype(BF16)
        dw_ref[...] += _dot_tn(jnp.concatenate([y_a, y_b], axis=1).astype(BF16), dhb)
        dy = _dot_nt(dhb, w_ref[...])
        for (dyh, x, r, g, sg, go, dy_out, dgo_ref, col) in (
                (dy[:, :512], xa, ra, ga, sga, goa, dya_ref, dgoa_ref, 0), (dy[:, 512:], xb, rb, gb, sgb, gob, dyb_ref, dgob_ref, 512)):
            dn = dyh * (g * sg)
            dg_ref[:, col:col + 512] = (dyh * (x * go) * (sg * (1.0 + g * (1.0 - sg)))).astype(BF16)
            dgo_ref[...] += jnp.sum(dn * x, axis=0, keepdims=True)
            dy_out[...] = _rms_bwd(x, r, dn * go)

    half = lambda c: pl.BlockSpec((tm, 512), lambda i: (i, c))
    vec = pl.BlockSpec((1, 512), lambda i: (0, 0))
    return _pallas(
        body, grid=(cfg.TP // tm,), name="out_bwd",
        in_specs=[pl.BlockSpec((tm, D_MODEL), lambda i: (i, 0)), half(0), half(0), half(0), half(1), vec, vec,
                  pl.BlockSpec((D_MODEL, D_MODEL), lambda i: (0, 0))],
        out_specs=[half(0), half(0), pl.BlockSpec((tm, D_MODEL), lambda i: (i, 0)),
                   pl.BlockSpec((D_MODEL, D_MODEL), lambda i: (0, 0)), vec, vec],
        out_shape=[jax.ShapeDtypeStruct((cfg.TP, 512), F32), jax.ShapeDtypeStruct((cfg.TP, 512), F32),
                   jax.ShapeDtypeStruct((cfg.TP, D_MODEL), BF16), jax.ShapeDtypeStruct((D_MODEL, D_MODEL), F32),
                   jax.ShapeDtypeStruct((1, 512), F32), jax.ShapeDtypeStruct((1, 512), F32)],
        compiler_params=_cp("arbitrary"),
    )(dh, ya, yb, pf, pf, goa, gob, wo_p)


def _lat_bwd(cfg, dq, dk, dv, pf, gq, gkv, wq_p, wkv_p, c_tab, s_tab):
    nj = cfg.NJ

    def body(dq_ref, dk_ref, dv_ref, cq_ref, ckv_ref, gq_ref, gkv_ref, wq_ref, wkv_ref, c_ref, s_ref,
             dl_ref, dwq_ref, dwkv_ref, dgq_ref, dgkv_ref):
        @pl.when((pl.program_id(0) == 0) & (pl.program_id(1) == 0))
        def _():
            dwq_ref[...] = jnp.zeros_like(dwq_ref)
            dwkv_ref[...] = jnp.zeros_like(dwkv_ref)
            dgq_ref[...] = jnp.zeros_like(dgq_ref)
            dgkv_ref[...] = jnp.zeros_like(dgkv_ref)

        c1, s1 = c_ref[...], s_ref[...]
        c8, s8 = jnp.tile(c1, (1, 8)), jnp.tile(s1, (1, 8))
        dq_r = dq_ref[...]
        dqp = (dq_r * c8 + _swap_rope(dq_r * s8)).astype(BF16)
        gq = gq_ref[...]
        xq, rq = _rms(cq_ref[...])
        dwq_ref[...] += _dot_tn((xq * gq).astype(BF16), dqp)
        dn = _dot_nt(dqp, wq_ref[...])
        dgq_ref[...] += jnp.sum(dn * xq, axis=0, keepdims=True)
        dl_ref[:, :256] = _rms_bwd(xq, rq, dn * gq).astype(BF16)

        dk_r = dk_ref[...]
        dkr = dk_r[:, :128]
        for hd in range(1, 8):
            dkr = dkr + dk_r[:, hd * 128:(hd + 1) * 128]
        lane1 = _lane(dkr.shape)
        dkr = jnp.where((lane1 >= 64) & (lane1 < 96), dkr, 0.0)
        dl_ref[:, 384:] = (dkr * c1 + _swap_rope(dkr * s1)).astype(BF16)
        dkv = jnp.concatenate([dk_r, dv_ref[...]], axis=1).astype(BF16)
        gkv = gkv_ref[...]
        xk, rk = _rms(ckv_ref[...])
        dwkv_ref[...] += _dot_tn((xk * gkv).astype(BF16), dkv)
        dn2 = _dot_nt(dkv, wkv_ref[...])
        dgkv_ref[...] += jnp.sum(dn2 * xk, axis=0, keepdims=True)
        dl_ref[:, 256:384] = _rms_bwd(xk, rk, dn2 * gkv).astype(BF16)

    row = lambda b, j: b * nj + j
    const = lambda shape: pl.BlockSpec(shape, lambda b, j: (0, 0))
    return _pallas(
        body, grid=(cfg.B, nj), name="lat_bwd",
        in_specs=[pl.BlockSpec((BLK, 1024), lambda b, j: (row(b, j), 0)), pl.BlockSpec((BLK, 1024), lambda b, j: (row(b, j), 0)),
                  pl.BlockSpec((BLK, 512), lambda b, j: (row(b, j), 0)),
                  pl.BlockSpec((BLK, 256), lambda b, j: (row(b, j), 4)), pl.BlockSpec((BLK, 128), lambda b, j: (row(b, j), 10)),
                  const((1, 256)), const((1, 128)), const((256, 1024)), const((128, 1536)),
                  pl.BlockSpec((BLK, 128), lambda b, j: (j, 0)), pl.BlockSpec((BLK, 128), lambda b, j: (j, 0))],
        out_specs=[pl.BlockSpec((BLK, 512), lambda b, j: (row(b, j), 0)), const((256, 1024)), const((128, 1536)),
                   const((1, 256)), const((1, 128))],
        out_shape=[jax.ShapeDtypeStruct((cfg.TP, 512), BF16), jax.ShapeDtypeStruct((256, 1024), F32),
                   jax.ShapeDtypeStruct((128, 1536), F32), jax.ShapeDtypeStruct((1, 256), F32), jax.ShapeDtypeStruct((1, 128), F32)],
        compiler_params=_cp("arbitrary", "arbitrary"),
    )(dq, dk, dv, pf, pf, gq, gkv, wq_p, wkv_p, c_tab, s_tab)


def _inproj_bwd(cfg, h, g, w_p, dqa, dkva, dgate, dlat, dh, comm=None):
    tm = 256

    def body(h_ref, g_ref, w_ref, dqa_ref, dkva_ref, dg_ref, dl_ref, dh_ref, o_ref, dw_ref, dgn_ref):
        @pl.when(pl.program_id(0) == 0)
        def _():
            dw_ref[...] = jnp.zeros_like(dw_ref)
            dgn_ref[...] = jnp.zeros_like(dgn_ref)

        g = g_ref[...]
        xh, r = _rms(h_ref[...])
        dproj = jnp.concatenate([dqa_ref[...], dkva_ref[...], dg_ref[...], dl_ref[...]], axis=1)
        dw_ref[...] += _dot_tn((xh * g).astype(BF16), dproj)
        du = _dot_nt(dproj, w_ref[...])
        dgn_ref[...] += jnp.sum(du * xh, axis=0, keepdims=True)
        o_ref[...] = dh_ref[...] + _rms_bwd(xh, r, du * g)

    rows = lambda w: pl.BlockSpec((tm, w), lambda i: (i, 0))
    return _call_with_comm(
        body, comm, grid=(cfg.TP // tm,), name="inproj_bwd",
        in_specs=[rows(D_MODEL), pl.BlockSpec((1, D_MODEL), lambda i: (0, 0)), pl.BlockSpec((D_MODEL, W_IN_P), lambda i: (0, 0)),
                  rows(512), rows(256), rows(1024), rows(512), rows(D_MODEL)],
        out_specs=[rows(D_MODEL), pl.BlockSpec((D_MODEL, W_IN_P), lambda i: (0, 0)), pl.BlockSpec((1, D_MODEL), lambda i: (0, 0))],
        out_shape=[jax.ShapeDtypeStruct((cfg.TP, D_MODEL), F32), jax.ShapeDtypeStruct((D_MODEL, W_IN_P), F32),
                   jax.ShapeDtypeStruct((1, D_MODEL), F32)],
        args=(h, g, w_p, dqa, dkva, dgate, dlat, dh))


def _meta_grad(cfg, dh):
    def body(d_ref, o_ref):
        @pl.when(pl.program_id(0) == 0)
        def _():
            o_ref[...] = d_ref[...]

        @pl.when(pl.program_id(0) > 0)
        def _():
            o_ref[...] += d_ref[...]

    return _pallas(
        body, grid=(cfg.B,), name="meta_grad",
        in_specs=[pl.BlockSpec((BLK, D_MODEL), lambda b: (b * cfg.NJ, 0))],
        out_specs=pl.BlockSpec((BLK, D_MODEL), lambda b: (0, 0)),
        out_shape=jax.ShapeDtypeStruct((BLK, D_MODEL), F32),
        compiler_params=_cp("arbitrary"),
    )(dh)


MATRICES = ("w_in", "w_uq", "w_ukv", "w_out")


def _local_grads(cfg, x, target, meta_of, table, small, weight_of, rider=None):
    def ride(stage, i, mats):
        hook = rider(stage, i, mats) if rider else None
        return hook if hook else (None, lambda res: None)

    depth = small["norm_in"].shape[0]
    rel, vis = _window_structure(cfg.NJ)
    bucket = _t5_bucket(jnp.asarray(rel))
    maskadd = jnp.asarray(np.where(vis, 0.0, NEG).astype(np.float32))
    c_tab, s_tab = _rope_tables(cfg)
    comm, deliver = ride("bias_build", 0, {})
    bias, *travelled = _bias_build(table, bucket, maskadd, comm)
    deliver(travelled)

    meta_blk = jnp.concatenate([meta_of(), jnp.zeros((BLK - N_META, D_MODEL), F32)], axis=0)
    h = jnp.concatenate([jnp.broadcast_to(meta_blk[None], (cfg.B, BLK, D_MODEL)), x], axis=1).reshape(cfg.TP, D_MODEL)

    wp, saved = [], []
    for i in range(depth):
        w = dict(w_in=_w_in_to_p(weight_of(i, "w_in")),
                 g_in=small["norm_in"][i][None], gq=small["norm_q_lat"][i][None], gkv=small["norm_kv_lat"][i][None],
                 goa=_perm_heads64(small["norm_out_a"][i], 0)[None], gob=small["norm_out_b"][i][None], sink=small["sink_a"][i])
        wp.append(w)
        comm, deliver = ride("inproj_fwd", i, {})
        pa, pf, *travelled = _inproj_fwd(cfg, h, w["g_in"], w["w_in"], comm)
        deliver(travelled)
        w.update(w_uq=_w_uq_to_p(weight_of(i, "w_uq")), w_ukv=_w_ukv_to_p(weight_of(i, "w_ukv")), w_out=_w_out_to_p(weight_of(i, "w_out")))
        q, k, v, kt, vt = _lat_fwd(cfg, pf, w["gq"], w["gkv"], w["w_uq"], w["w_ukv"], c_tab, s_tab)
        ya, lse_a = _win_fwd(cfg, pa, bias, w["sink"])
        comm, deliver = ride("mla_fwd", i, {})
        yb, lse_b, *travelled = _mla_fwd(cfg, q, kt, v, comm)
        deliver(travelled)
        h_next = _out_fwd(cfg, ya, yb, pf, w["goa"], w["gob"], w["w_out"], h)
        saved.append(dict(h=h, pa=pa, pf=pf, q=q, k=k, kt=kt, vt=vt, ya=ya, lse_a=lse_a, yb=yb, lse_b=lse_b))
        h = h_next

    dh, loss_tile, d_norm_final = _loss_bwd(cfg, h, target.reshape(cfg.B * cfg.S, D_MODEL), small["norm_final"][None])

    grads = {k_: [] for k_ in ("norm_in", "sink_a", "norm_q_lat", "norm_kv_lat", "norm_out_a", "norm_out_b")}
    mats, s_accs = {}, []
    for i in reversed(range(depth)):
        w, sv = wp[i], saved[i]
        dya, dyb, dgate, dwo, dgoa, dgob = _out_bwd(cfg, dh, sv["ya"], sv["yb"], sv["pf"], w["goa"], w["gob"], w["w_out"])
        dqa, dkp, dvp, dkm, dvm, s_acc, dsink = _win_bwd(cfg, sv["pa"], bias, w["sink"], dya, sv["ya"], sv["lse_a"])
        dkva = _win_dkv_combine(cfg, dkp, dvp, dkm, dvm)
        comm, deliver = ride("mla_bwd", i, mats)
        dq, dk, dv, *travelled = _mla_bwd(cfg, sv["q"], sv["k"], sv["kt"], sv["vt"], dyb, sv["yb"], sv["lse_b"], comm)
        deliver(travelled)
        dlat, dwq, dwkv, dgq, dgkv = _lat_bwd(cfg, dq, dk, dv, sv["pf"], w["gq"], w["gkv"], w["w_uq"], w["w_ukv"], c_tab, s_tab)
        mats[i] = dict(w_uq=_w_uq_from_p(dwq), w_ukv=_w_ukv_from_p(dwkv), w_out=_w_out_from_p(dwo))
        comm, deliver = ride("inproj_bwd", i, mats)
        dh, dwin, dgin, *travelled = _inproj_bwd(cfg, sv["h"], w["g_in"], w["w_in"], dqa, dkva, dgate, dlat, dh, comm)
        deliver(travelled)
        s_accs.append(s_acc)
        mats[i]["w_in"] = _w_in_from_p(dwin)
        grads["norm_in"].append(dgin[0])
        grads["sink_a"].append(dsink[0, :A_HEADS])
        grads["norm_q_lat"].append(dgq[0])
        grads["norm_kv_lat"].append(dgkv[0])
        grads["norm_out_a"].append(_unperm_heads64(dgoa[0], 0))
        grads["norm_out_b"].append(dgob[0])

    out = {k_: jnp.stack(v_[::-1]) for k_, v_ in grads.items()}
    mats["meta_tokens"] = _meta_grad(cfg, dh)[:N_META]
    comm, deliver = ride("bias_grad", 0, mats)
    dtable, *travelled = _bias_grad(s_accs, bucket, comm)
    deliver(travelled)
    out["rel_bias_table"] = dtable[:, 0].reshape(N_BUCKETS, A_HEADS)
    out["norm_final"] = d_norm_final[0]
    return loss_tile[0, 0], dh.reshape(cfg.B, cfg.LP, D_MODEL)[:, BLK:], out, mats


MESH = pl.DeviceIdType.MESH
ANY = pl.BlockSpec(memory_space=pl.ANY)


def _place():
    x, y, c = lax.axis_index("x"), lax.axis_index("y"), lax.axis_index("c")
    others = [(1 - x, y), (x, 1 - y), (1 - x, 1 - y)]
    return x, y, c, others


Comm = collections.namedtuple("Comm", "inputs out_shapes scratch start wait")


def _gather_comm(shards):
    n = len(shards)

    def copies(ins, outs, sems, arriving):
        send_sems, recv_sems, local_sems = sems
        x, y, c, others = _place()
        k_me = 2 * x + y
        local = [pltpu.make_async_copy(ins[a], outs[a].at[k_me], local_sems.at[a]) for a in range(n)]
        remote = [pltpu.make_async_remote_copy(src_ref=ins[a], dst_ref=outs[a].at[2 * ox + oy if arriving else k_me],
                                               send_sem=send_sems.at[3 * a + j], recv_sem=recv_sems.at[3 * a + j],
                                               device_id=(ox, oy, c), device_id_type=MESH)
                  for a in range(n) for j, (ox, oy) in enumerate(others)]
        return local, remote

    def start(ins, outs, sems):
        local, sends = copies(ins, outs, sems, arriving=False)
        for cp in local + sends:
            cp.start()

    def wait(ins, outs, sems):
        local, recvs = copies(ins, outs, sems, arriving=True)
        for cp in recvs:
            cp.wait_recv()
        for cp in recvs:
            cp.wait_send()
        for cp in local:
            cp.wait()

    return Comm(list(shards), [jax.ShapeDtypeStruct((4, *s.shape), s.dtype) for s in shards],
                [pltpu.SemaphoreType.DMA((3 * n,)), pltpu.SemaphoreType.DMA((3 * n,)), pltpu.SemaphoreType.DMA((n,))], start, wait)


def _gather_halves_comm(shards):
    n = len(shards)

    def copies(ins, outs, sems, kind):
        send_sems, recv_sems, fwd_send_sems, fwd_recv_sems, local_sems = sems
        x, y, c, others = _place()
        k_me = 2 * x + y

        def half(ref, which):
            rows = ref.shape[0] // 2
            return ref.at[pl.ds(pl.multiple_of(which * rows, 8), rows)]

        if kind == "local":
            return [pltpu.make_async_copy(ins[a], outs[a].at[k_me], local_sems.at[a]) for a in range(n)]
        made = []
        for a in range(n):
            for j, (ox, oy) in enumerate(others):
                slot = outs[a].at[k_me if kind == "sent" else 2 * ox + oy]
                if kind in ("sent", "arrived"):
                    made.append(pltpu.make_async_remote_copy(
                        src_ref=half(ins[a], c), dst_ref=half(slot, c), send_sem=send_sems.at[3 * a + j],
                        recv_sem=recv_sems.at[3 * a + j], device_id=(ox, oy, c), device_id_type=MESH))
                else:
                    which = c if kind == "forward" else 1 - c
                    made.append(pltpu.make_async_remote_copy(
                        src_ref=half(slot, which), dst_ref=half(slot, which), send_sem=fwd_send_sems.at[3 * a + j],
                        recv_sem=fwd_recv_sems.at[3 * a + j], device_id=(x, y, 1 - c), device_id_type=MESH))
        return made

    def start(ins, outs, sems):
        for cp in copies(ins, outs, sems, "local") + copies(ins, outs, sems, "sent"):
            cp.start()

    def wait(ins, outs, sems):
        arrived, forward = copies(ins, outs, sems, "arrived"), copies(ins, outs, sems, "forward")
        for came, on in zip(arrived, forward):
            came.wait_recv()
            on.start()
        for cp in copies(ins, outs, sems, "forwarded"):
            cp.wait_recv()
        for cp in arrived + forward:
            cp.wait_send()
        for cp in copies(ins, outs, sems, "local"):
            cp.wait()

    return Comm(list(shards), [jax.ShapeDtypeStruct((4, *s.shape), s.dtype) for s in shards],
                [pltpu.SemaphoreType.DMA((3 * n,))] * 4 + [pltpu.SemaphoreType.DMA((n,))], start, wait)


def _scatter_comm(parts):
    n = len(parts)

    def copies(ins, outs, sems):
        send_sems, recv_sems = sems
        x, y, c, others = _place()
        return [pltpu.make_async_remote_copy(src_ref=ins[a].at[2 * ox + oy], dst_ref=outs[a].at[j], send_sem=send_sems.at[3 * a + j],
                                             recv_sem=recv_sems.at[3 * a + j], device_id=(ox, oy, c), device_id_type=MESH)
                for a in range(n) for j, (ox, oy) in enumerate(others)]

    def start(ins, outs, sems):
        for cp in copies(ins, outs, sems):
            cp.start()

    def wait(ins, outs, sems):
        cps = copies(ins, outs, sems)
        for cp in cps:
            cp.wait_recv()
        for cp in cps:
            cp.wait_send()

    return Comm(list(parts), [jax.ShapeDtypeStruct((3, *p.shape[1:]), p.dtype) for p in parts],
                [pltpu.SemaphoreType.DMA((3 * n,)), pltpu.SemaphoreType.DMA((3 * n,))], start, wait)


def _call_with_comm(body, comm, *, grid, name, in_specs, out_specs, out_shape, args, scratch_shapes=()):
    if comm is None:
        return _pallas(body, grid=grid, name=name, in_specs=in_specs, out_specs=out_specs, out_shape=out_shape,
                              scratch_shapes=list(scratch_shapes), compiler_params=_cp(*["arbitrary"] * len(grid)))(*args)
    n_in, n_out, ci, co, ns = len(in_specs), len(out_specs), len(comm.inputs), len(comm.out_shapes), len(scratch_shapes)

    def wrapped(*refs):
        ins, cins = refs[:n_in], refs[n_in:n_in + ci]
        outs, couts = refs[n_in + ci:n_in + ci + n_out], refs[n_in + ci + n_out:n_in + ci + n_out + co]
        scratch, sems = refs[n_in + ci + n_out + co:n_in + ci + n_out + co + ns], refs[n_in + ci + n_out + co + ns:]
        ids = [pl.program_id(a) for a in range(len(grid))]
        first = functools.reduce(jnp.logical_and, [i == 0 for i in ids])
        last = functools.reduce(jnp.logical_and, [i == g - 1 for i, g in zip(ids, grid)])

        @pl.when(first)
        def _():
            comm.start(cins, couts, sems)

        body(*ins, *outs, *scratch)

        @pl.when(last)
        def _():
            comm.wait(cins, couts, sems)

    return _pallas(
        wrapped, grid=grid, name=name + "_comm", in_specs=[*in_specs, *[ANY] * ci], out_specs=[*out_specs, *[ANY] * co],
        out_shape=[*out_shape, *comm.out_shapes], scratch_shapes=[*scratch_shapes, *comm.scratch],
        compiler_params=_cp(*["arbitrary"] * len(grid)))(*args, *comm.inputs)


def _allreduce_and_swap(v, arrs):
    n = len(arrs)

    def body(*refs):
        v_ref, ins = refs[0], refs[1:1 + n]
        o_ref, outs = refs[1 + n], refs[2 + n:2 + 2 * n]
        buf, send_sems, recv_sems, sib_send_sems, sib_recv_sems = refs[2 + 2 * n:]
        x, y, c, _ = _place()
        swaps = [pltpu.make_async_remote_copy(src_ref=ins[a], dst_ref=outs[a], send_sem=sib_send_sems.at[a], recv_sem=sib_recv_sems.at[a],
                                              device_id=(x, y, 1 - c), device_id_type=MESH) for a in range(n)]
        for cp in swaps:
            cp.start()
        me = 4 * x + 2 * y + c
        buf[me] = v_ref[...]

        def copy(r):
            tx, ty, tc = (x + (r >> 2)) % 2, (y + ((r >> 1) & 1)) % 2, (c + (r & 1)) % 2
            return tx, ty, tc

        sends = []
        for r in range(1, 8):
            tx, ty, tc = copy(r)
            sends.append(pltpu.make_async_remote_copy(src_ref=v_ref, dst_ref=buf.at[me], send_sem=send_sems.at[r - 1],
                                                      recv_sem=recv_sems.at[r - 1], device_id=(tx, ty, tc), device_id_type=MESH))
        for cp in sends:
            cp.start()
        for r in range(1, 8):
            tx, ty, tc = copy(r)
            pltpu.make_async_remote_copy(src_ref=v_ref, dst_ref=buf.at[4 * tx + 2 * ty + tc], send_sem=send_sems.at[r - 1],
                                         recv_sem=recv_sems.at[r - 1], device_id=(tx, ty, tc), device_id_type=MESH).wait_recv()
        for cp in sends:
            cp.wait_send()
        acc = buf[0]
        for d in range(1, 8):
            acc = acc + buf[d]
        o_ref[...] = acc
        for cp in swaps:
            cp.wait_recv()
        for cp in swaps:
            cp.wait_send()

    vmem = pl.BlockSpec(memory_space=pltpu.VMEM)
    res = pl.pallas_call(
        body, name="allreduce_and_swap", in_specs=[vmem] + [ANY] * n, out_specs=[vmem] + [ANY] * n,
        out_shape=[jax.ShapeDtypeStruct(v.shape, F32)] + [jax.ShapeDtypeStruct(a.shape, a.dtype) for a in arrs],
        scratch_shapes=[pltpu.VMEM((8, *v.shape), F32), pltpu.SemaphoreType.DMA((7,)), pltpu.SemaphoreType.DMA((7,)),
                        pltpu.SemaphoreType.DMA((n,)), pltpu.SemaphoreType.DMA((n,))],
    )(v, *arrs)
    return res[0], res[1:]


def _rows_view(a):
    return a.reshape(-1, a.shape[-1])


def _elementwise(name, fn, ins, n_out):
    rows, cols = ins[0].shape
    tm = min(rows, 256)
    spec = pl.BlockSpec((tm, cols), lambda i: (i, 0))

    def body(*refs):
        outs = fn(*[r[...] for r in refs[:len(ins)]])
        for o_ref, o in zip(refs[len(ins):], outs):
            o_ref[...] = o

    return _pallas(
        body, grid=(rows // tm,), name=name, in_specs=[spec] * len(ins), out_specs=[spec] * n_out,
        out_shape=[jax.ShapeDtypeStruct((rows, cols), F32)] * n_out, compiler_params=_cp("parallel"),
    )(*ins)


def _sum_parts(name, own, recv):
    def fn(o, r0, r1, r2):
        return (o + r0.astype(F32) + r1.astype(F32) + r2.astype(F32),)

    return _elementwise("sum_parts_" + name, fn, [own, recv[0], recv[1], recv[2]], 1)[0]


def _adamw(name, w, m, v, g_parts):
    def fn(w_, m_, v_, *gs):
        g = gs[0]
        for extra in gs[1:]:
            g = g + extra
        m_new = ADAM_B1 * m_ + (1.0 - ADAM_B1) * g
        v_new = ADAM_B2 * v_ + (1.0 - ADAM_B2) * (g * g)
        m_hat = m_new / (1.0 - ADAM_B1 ** ADAM_STEP)
        v_hat = v_new / (1.0 - ADAM_B2 ** ADAM_STEP)
        delta = -ADAM_LR * (m_hat / (jnp.sqrt(v_hat) + ADAM_EPS) + ADAM_WD * w_)
        return g, delta, m_new, v_new

    return _elementwise("adamw_" + name, fn, [w, m, v, *g_parts], 4)


MAT_AXIS = {"w_in": 1, "w_uq": 1, "w_ukv": 1, "w_out": 0}
SMALL = ("rel_bias_table", "norm_in", "sink_a", "norm_q_lat", "norm_kv_lat", "norm_out_a", "norm_out_b", "norm_final")
WEIGHTS = ("meta_tokens", "rel_bias_table", "norm_in", "w_in", "sink_a", "norm_q_lat", "w_uq", "norm_kv_lat", "w_ukv",
           "norm_out_a", "norm_out_b", "w_out", "norm_final")
SMALL_ROWS, SMALL_COLS = 8, 1024


def _pack_small(d, loss=None):
    flat = [d[n].reshape(-1) for n in SMALL]
    if loss is not None:
        flat.append(loss.reshape(1))
    used = sum(f.shape[0] for f in flat)
    flat.append(jnp.zeros((SMALL_ROWS * SMALL_COLS - used,), F32))
    return jnp.concatenate(flat).reshape(SMALL_ROWS, SMALL_COLS)


def _unpack_small(p, like):
    flat, out, off = p.reshape(-1), {}, 0
    for n in SMALL:
        size = int(np.prod(like[n].shape))
        out[n] = flat[off:off + size].reshape(like[n].shape)
        off += size
    return out, flat[off]


def _split4(a, axis):
    size = a.shape[axis] // 4
    return jnp.stack([lax.slice_in_dim(a, k * size, (k + 1) * size, axis=axis) for k in range(4)])


def _train_step(cfg, x, target, w, m, v):
    depth = w["w_in"].shape[0]
    rest = tuple(n for n in MATRICES if n != "w_in")
    weights, splits, received = {}, {}, {}

    def gather(i, names, also=(), build=_gather_comm):
        def deliver(res):
            for n, g in zip(names, res):
                weights[i, n] = jnp.concatenate([g[k] for k in range(4)], axis=MAT_AXIS[n])

        return build([w[n][i].astype(BF16) for n in names] + list(also)), deliver

    def scatter(i, names, mats, also=()):
        for n in names:
            splits[i, n] = _split4(mats[i][n], MAT_AXIS[n])

        def deliver(res):
            for n, r in zip(names, res):
                received[i, n] = r

        return _scatter_comm([splits[i, n].astype(BF16) for n in names] + list(also)), deliver

    def rider(stage, i, mats):
        if stage == "bias_build":
            comm, deliver = gather(0, ("w_in",), also=[w["meta_tokens"]], build=_gather_halves_comm)

            def deliver_first(res):
                deliver(res)
                weights["meta"] = jnp.concatenate([res[1][k] for k in range(4)], axis=1)

            return comm, deliver_first
        if stage == "inproj_fwd" and i == 0:
            return gather(0, rest)
        if stage == "mla_fwd" and i + 1 < depth:
            return gather(i + 1, MATRICES)
        if stage == "mla_bwd" and i + 1 < depth:
            return scatter(i + 1, MATRICES, mats)
        if stage == "inproj_bwd" and i == 0:
            return scatter(0, rest, mats)
        if stage == "bias_grad":
            splits["meta"] = _split4(mats["meta_tokens"], 1)
            comm, deliver = scatter(0, ("w_in",), mats, also=[splits["meta"].astype(BF16)])

            def deliver_last(res):
                deliver(res)
                received["meta"] = res[1]

            return comm, deliver_last
        return None

    loss_local, grad_x, g, mats = _local_grads(cfg, x, target, lambda: weights["meta"], w["rel_bias_table"], {n: w[n] for n in SMALL},
                                               lambda i, n: weights[i, n], rider)

    k_me = 2 * lax.axis_index("x") + lax.axis_index("y")

    def core_sum(name, split, recv):
        own = lax.dynamic_index_in_dim(split, k_me, 0, keepdims=False)
        return _sum_parts(name, _rows_view(own), recv.reshape(3, -1, recv.shape[-1]))

    partial = [core_sum("meta_tokens", splits["meta"], received["meta"])]
    for n in MATRICES:
        partial.append(jnp.concatenate([core_sum(f"{n}_{i}", splits[i, n], received[i, n]) for i in range(depth)], axis=0))
    small_sum, sibling = _allreduce_and_swap(_pack_small(g, loss_local), partial)
    g_small, loss = _unpack_small(small_sum, {n: w[n] for n in SMALL})

    outs = {}
    for n, p_me, p_sib in zip(("meta_tokens", *MATRICES), partial, sibling):
        res = _adamw(n, _rows_view(w[n]), _rows_view(m[n]), _rows_view(v[n]), [p_me, p_sib])
        outs[n] = [r.reshape(w[n].shape) for r in res]
    res = _adamw("small", _pack_small(w), _pack_small(m), _pack_small(v), [_pack_small(g_small)])
    unpacked = [_unpack_small(r, {n: w[n] for n in SMALL})[0] for r in res]
    for n in SMALL:
        outs[n] = [u[n] for u in unpacked]

    result = [loss, grad_x]
    for field in range(4):
        result.extend(outs[n][field] for n in WEIGHTS)
    return tuple(result)


def kernel(x, meta_tokens, rel_bias_table, norm_in, w_in, sink_a, norm_q_lat, w_uq, norm_kv_lat, w_ukv, norm_out_a, norm_out_b, w_out, norm_final, loss_target, m_meta_tokens, m_rel_bias_table, m_norm_in, m_w_in, m_sink_a, m_norm_q_lat, m_w_uq, m_norm_kv_lat, m_w_ukv, m_norm_out_a, m_norm_out_b, m_w_out, m_norm_final, v_meta_tokens, v_rel_bias_table, v_norm_in, v_w_in, v_sink_a, v_norm_q_lat, v_w_uq, v_norm_kv_lat, v_w_ukv, v_norm_out_a, v_norm_out_b, v_w_out, v_norm_final):
    w = dict(zip(WEIGHTS, (meta_tokens, rel_bias_table, norm_in, w_in, sink_a, norm_q_lat, w_uq, norm_kv_lat, w_ukv, norm_out_a, norm_out_b, w_out, norm_final)))
    m = dict(zip(WEIGHTS, (m_meta_tokens, m_rel_bias_table, m_norm_in, m_w_in, m_sink_a, m_norm_q_lat, m_w_uq, m_norm_kv_lat, m_w_ukv, m_norm_out_a, m_norm_out_b, m_w_out, m_norm_final)))
    v = dict(zip(WEIGHTS, (v_meta_tokens, v_rel_bias_table, v_norm_in, v_w_in, v_sink_a, v_norm_q_lat, v_w_uq, v_norm_kv_lat, v_w_ukv, v_norm_out_a, v_norm_out_b, v_w_out, v_norm_final)))
    cfg = make_cfg(x.shape[0], x.shape[1])
    return _train_step(cfg, x, loss_target, w, m, v)
```

```python
import collections
import functools
import math

import jax
import jax.numpy as jnp
import numpy as np
from jax import lax
from jax.experimental import pallas as pl
from jax.experimental.pallas import tpu as pltpu

F32 = jnp.float32
BF16 = jnp.bfloat16

BLK = 128
N_META = 16
D_MODEL = 1024
A_HEADS, A_KV, A_DH = 8, 2, 64
B_HEADS, B_NOPE, B_ROPE, B_DV = 8, 64, 32, 64
Q_RANK, KV_RANK = 256, 128
N_BUCKETS, MAX_DIST = 32, 128
ROPE_THETA = 10000.0
EPS = 1e-6
IN_WIDTH = 2208
W_IN_P = 2304
NEG = -1e30
MASK_LANE = 96
LOG2E = math.log2(math.e)
Q_SCALE = (B_NOPE + B_ROPE) ** -0.5 * LOG2E
QA_SCALE = A_DH ** -0.5 * LOG2E
LN2 = math.log(2.0)
VMEM_LIMIT = 48 * 1024 * 1024

ADAM_LR, ADAM_B1, ADAM_B2, ADAM_EPS, ADAM_WD, ADAM_STEP = 0.001, 0.9, 0.999, 1e-08, 0.01, 10

Cfg = collections.namedtuple("Cfg", "B S NB NJ LP TP")


def make_cfg(batch, seq):
    nb = seq // BLK
    nj = nb + 1
    return Cfg(batch, seq, nb, nj, nj * BLK, batch * nj * BLK)


def _cp(*sem):
    return pltpu.CompilerParams(dimension_semantics=sem, vmem_limit_bytes=VMEM_LIMIT)


def _pallas(body, *, out_shape, **kw):
    pinned = jax.tree.map(lambda s: pltpu.HBM(s.shape, s.dtype), out_shape)
    call = pl.pallas_call(body, out_shape=pinned, **kw)
    return lambda *args: call(*[pltpu.with_memory_space_constraint(a, pltpu.HBM) for a in args])


def _dot(a, b):
    return jnp.dot(a, b, preferred_element_type=F32)


def _dot_nt(a, b):
    return lax.dot_general(a, b, (((1,), (1,)), ((), ())), preferred_element_type=F32)


def _dot_tn(a, b):
    return lax.dot_general(a, b, (((0,), (0,)), ((), ())), preferred_element_type=F32)


def _rms(x, width=None):
    n = x.shape[-1] if width is None else width
    r = lax.rsqrt(jnp.sum(x * x, axis=-1, keepdims=True) * (1.0 / n) + EPS)
    return x * r, r


def _rms_bwd(xhat, r, t):
    n = xhat.shape[-1]
    return r * (t - xhat * (jnp.sum(t * xhat, axis=-1, keepdims=True) * (1.0 / n)))


def _sigmoid(x):
    return 1.0 / (1.0 + jnp.exp(-x))


def _lane(shape):
    return lax.broadcasted_iota(jnp.int32, shape, len(shape) - 1)


def _swap_rope(x):
    n = x.shape[-1]
    lane = _lane(x.shape) % BLK
    up = pltpu.roll(x, n - 16, axis=x.ndim - 1)
    dn = pltpu.roll(x, 16, axis=x.ndim - 1)
    return jnp.where((lane >= 64) & (lane < 80), up, jnp.where((lane >= 80) & (lane < 96), dn, 0.0))


A_ORDER = (0, 4, 1, 5, 2, 6, 3, 7)


def _jtype(j, nj):
    return 0 if j == 0 else 1 if j == 1 else 3 if j == nj - 1 else 2


def _window_structure(nj):
    def pos(blk, r):
        return np.where(blk == 0, r, N_META + (blk - 1) * BLK + r)

    def valid(blk, r):
        return np.where(blk == 0, r < N_META, True)

    r = np.arange(BLK)
    rels, viss = [], []
    for j in (0, 1, 2, nj - 1):
        qpos = pos(j, r)[:, None]
        rel_t, vis_t = [], []
        for s, kb in enumerate((0, j - 1, j, j + 1)):
            slot_ok = (s == 0) or (1 <= kb <= nj - 1)
            kbc = min(max(kb, 0), nj - 1)
            kpos = pos(kbc, r)[None, :]
            rel = kpos - qpos
            v = valid(kbc, r)[None, :] & np.ones((BLK, 1), bool)
            if s > 0:
                v = v & (np.abs(rel) <= BLK)
            rel_t.append(rel)
            vis_t.append(v & slot_ok)
        rels.append(np.concatenate(rel_t, axis=1))
        viss.append(np.concatenate(vis_t, axis=1))
    return np.stack(rels).astype(np.int32), np.stack(viss)


def _t5_bucket(rel):
    nb = N_BUCKETS // 2
    max_exact = nb // 2
    ret = jnp.where(rel > 0, nb, 0)
    n = jnp.abs(rel)
    nf = jnp.maximum(n, 1).astype(jnp.float32)
    large = max_exact + (jnp.log(nf / max_exact) / math.log(MAX_DIST / max_exact) * (nb - max_exact)).astype(jnp.int32)
    large = jnp.minimum(large, nb - 1)
    return ret + jnp.where(n < max_exact, n, large)


def _perm_heads64(a, axis):
    parts = [lax.slice_in_dim(a, h * 64, (h + 1) * 64, axis=axis) for h in A_ORDER]
    return jnp.concatenate(parts, axis=axis)


def _unperm_heads64(a, axis):
    inv = [A_ORDER.index(h) for h in range(8)]
    parts = [lax.slice_in_dim(a, p * 64, (p + 1) * 64, axis=axis) for p in inv]
    return jnp.concatenate(parts, axis=axis)


def _w_in_to_p(w):
    sl = lambda a, b: lax.slice_in_dim(w, a, b, axis=1)
    z = lambda n: jnp.zeros((w.shape[0], n), w.dtype)
    return jnp.concatenate([_perm_heads64(sl(0, 512), 1), sl(512, 768), _perm_heads64(sl(768, 1280), 1), sl(1696, 2208),
                            sl(1280, 1536), sl(1536, 1664), z(64), sl(1664, 1696), z(32)], axis=1)


def _w_in_from_p(g):
    sl = lambda a, b: lax.slice_in_dim(g, a, b, axis=1)
    return jnp.concatenate([_unperm_heads64(sl(0, 512), 1), sl(512, 768), _unperm_heads64(sl(768, 1280), 1),
                            sl(1792, 2048), sl(2048, 2176), sl(2240, 2272), sl(1280, 1792)], axis=1)


def _w_uq_to_p(w):
    z = jnp.zeros((w.shape[0], 32), w.dtype)
    return jnp.concatenate([p for h in range(8) for p in (lax.slice_in_dim(w, h * 96, (h + 1) * 96, axis=1), z)], axis=1)


def _w_uq_from_p(g):
    return jnp.concatenate([lax.slice_in_dim(g, h * 128, h * 128 + 96, axis=1) for h in range(8)], axis=1)


def _w_ukv_to_p(w):
    z = jnp.zeros((w.shape[0], 64), w.dtype)
    ks = [p for h in range(8) for p in (lax.slice_in_dim(w, h * 128, h * 128 + 64, axis=1), z)]
    vs = [lax.slice_in_dim(w, h * 128 + 64, (h + 1) * 128, axis=1) for h in range(8)]
    return jnp.concatenate(ks + vs, axis=1)


def _w_ukv_from_p(g):
    parts = []
    for h in range(8):
        parts.append(lax.slice_in_dim(g, h * 128, h * 128 + 64, axis=1))
        parts.append(lax.slice_in_dim(g, 1024 + h * 64, 1024 + (h + 1) * 64, axis=1))
    return jnp.concatenate(parts, axis=1)


def _w_out_to_p(w):
    return jnp.concatenate([_perm_heads64(lax.slice_in_dim(w, 0, 512, axis=0), 0), lax.slice_in_dim(w, 512, 1024, axis=0)], axis=0)


def _w_out_from_p(g):
    return jnp.concatenate([_unperm_heads64(lax.slice_in_dim(g, 0, 512, axis=0), 0), lax.slice_in_dim(g, 512, 1024, axis=0)], axis=0)


def _rope_tables(cfg):
    half = B_ROPE // 2
    length = N_META + cfg.S
    freqs = ROPE_THETA ** (-jnp.arange(half, dtype=jnp.float32) / half)
    ang = jnp.arange(length, dtype=jnp.float32)[:, None] * freqs[None, :]
    cos, sin = jnp.cos(ang), jnp.sin(ang)

    def rows(t):
        return jnp.concatenate([t[:N_META], jnp.zeros((BLK - N_META, t.shape[1]), t.dtype), t[N_META:]], axis=0)

    ones = jnp.ones((length, 64), F32)
    zer = jnp.zeros((length, 32), F32)
    c_tab = rows(jnp.concatenate([ones, cos, cos, zer], axis=1))
    s_tab = rows(jnp.concatenate([zer, zer, -sin, sin, zer], axis=1))
    return c_tab, s_tab


def _inproj_fwd(cfg, h, g, w_p, comm=None):
    tm = 256

    def body(h_ref, g_ref, w_ref, pa_ref, pf_ref):
        xh, _ = _rms(h_ref[...])
        u = (xh * g_ref[...]).astype(BF16)
        acc = _dot(u, w_ref[...])
        pa_ref[:, :512] = (acc[:, :512] * QA_SCALE).astype(BF16)
        pa_ref[:, 512:] = acc[:, 512:768].astype(BF16)
        pf_ref[...] = acc[:, 768:]

    return _call_with_comm(
        body, comm, grid=(cfg.TP // tm,), name="inproj_fwd",
        in_specs=[pl.BlockSpec((tm, D_MODEL), lambda i: (i, 0)), pl.BlockSpec((1, D_MODEL), lambda i: (0, 0)),
                  pl.BlockSpec((D_MODEL, W_IN_P), lambda i: (0, 0))],
        out_specs=[pl.BlockSpec((tm, 768), lambda i: (i, 0)), pl.BlockSpec((tm, 1536), lambda i: (i, 0))],
        out_shape=[jax.ShapeDtypeStruct((cfg.TP, 768), BF16), jax.ShapeDtypeStruct((cfg.TP, 1536), F32)],
        args=(h, g, w_p))


def _lat_fwd(cfg, pf, gq, gkv, wq_p, wkv_p, c_tab, s_tab):
    nj = cfg.NJ

    def body(cq_ref, ckv_ref, kr_ref, gq_ref, gkv_ref, wq_ref, wkv_ref, c_ref, s_ref, q_ref, k_ref, v_ref, kt_ref, vt_ref):
        c1, s1 = c_ref[...], s_ref[...]
        c8, s8 = jnp.tile(c1, (1, 8)), jnp.tile(s1, (1, 8))
        mask_lane = _lane((BLK, 1024)) % BLK == MASK_LANE
        zero_row = (pl.program_id(1) == 0) & (lax.broadcasted_iota(jnp.int32, (BLK, 1024), 0) >= N_META)
        xq, _ = _rms(cq_ref[...])
        qp = _dot((xq * gq_ref[...]).astype(BF16), wq_ref[...])
        q_ref[...] = jnp.where(mask_lane, 1.0, (qp * c8 + _swap_rope(qp) * s8) * Q_SCALE).astype(BF16)
        xk, _ = _rms(ckv_ref[...])
        kvp = _dot((xk * gkv_ref[...]).astype(BF16), wkv_ref[...])
        kr = kr_ref[...]
        krr = kr * c1 + _swap_rope(kr) * s1
        k = jnp.where(mask_lane & zero_row, NEG, kvp[:, :1024] + jnp.tile(krr, (1, 8)))
        k_ref[...] = k.astype(BF16)
        v_ref[...] = kvp[:, 1024:].astype(BF16)
        kt_ref[...] = k.T.astype(BF16)
        vt_ref[...] = kvp[:, 1024:].T.astype(BF16)

    row = lambda b, j: b * nj + j
    return _pallas(
        body, grid=(cfg.B, nj), name="lat_fwd",
        in_specs=[pl.BlockSpec((BLK, 256), lambda b, j: (row(b, j), 4)), pl.BlockSpec((BLK, 128), lambda b, j: (row(b, j), 10)),
                  pl.BlockSpec((BLK, 128), lambda b, j: (row(b, j), 11)),
                  pl.BlockSpec((1, 256), lambda b, j: (0, 0)), pl.BlockSpec((1, 128), lambda b, j: (0, 0)),
                  pl.BlockSpec((256, 1024), lambda b, j: (0, 0)), pl.BlockSpec((128, 1536), lambda b, j: (0, 0)),
                  pl.BlockSpec((BLK, 128), lambda b, j: (j, 0)), pl.BlockSpec((BLK, 128), lambda b, j: (j, 0))],
        out_specs=[pl.BlockSpec((BLK, 1024), lambda b, j: (row(b, j), 0)), pl.BlockSpec((BLK, 1024), lambda b, j: (row(b, j), 0)),
                   pl.BlockSpec((BLK, 512), lambda b, j: (row(b, j), 0)),
                   pl.BlockSpec((1024, BLK), lambda b, j: (b, j)), pl.BlockSpec((512, BLK), lambda b, j: (b, j))],
        out_shape=[jax.ShapeDtypeStruct((cfg.TP, 1024), BF16), jax.ShapeDtypeStruct((cfg.TP, 1024), BF16),
                   jax.ShapeDtypeStruct((cfg.TP, 512), BF16),
                   jax.ShapeDtypeStruct((cfg.B * 1024, cfg.LP), BF16), jax.ShapeDtypeStruct((cfg.B * 512, cfg.LP), BF16)],
        compiler_params=_cp("parallel", "parallel"),
    )(pf, pf, pf, gq, gkv, wq_p, wkv_p, c_tab, s_tab)


def _gate_halves(ya, yb, ga, gb, goa, gob):
    xa, ra = _rms(ya)
    xb, rb = _rms(yb)
    sga, sgb = _sigmoid(ga), _sigmoid(gb)
    return xa, ra, xb, rb, sga, sgb, xa * goa * (ga * sga), xb * gob * (gb * sgb)


def _out_fwd(cfg, ya, yb, pf, goa, gob, wo_p, h):
    tm = 256

    def body(ya_ref, yb_ref, ga_ref, gb_ref, goa_ref, gob_ref, w_ref, h_ref, o_ref):
        *_, y_a, y_b = _gate_halves(ya_ref[...], yb_ref[...], ga_ref[...], gb_ref[...], goa_ref[...], gob_ref[...])
        y = jnp.concatenate([y_a, y_b], axis=1).astype(BF16)
        o_ref[...] = h_ref[...] + _dot(y, w_ref[...])

    return _pallas(
        body, grid=(cfg.TP // tm,), name="out_fwd",
        in_specs=[pl.BlockSpec((tm, 512), lambda i: (i, 0)), pl.BlockSpec((tm, 512), lambda i: (i, 0)),
                  pl.BlockSpec((tm, 512), lambda i: (i, 0)), pl.BlockSpec((tm, 512), lambda i: (i, 1)),
                  pl.BlockSpec((1, 512), lambda i: (0, 0)), pl.BlockSpec((1, 512), lambda i: (0, 0)),
                  pl.BlockSpec((D_MODEL, D_MODEL), lambda i: (0, 0)), pl.BlockSpec((tm, D_MODEL), lambda i: (i, 0))],
        out_specs=pl.BlockSpec((tm, D_MODEL), lambda i: (i, 0)),
        out_shape=jax.ShapeDtypeStruct((cfg.TP, D_MODEL), F32),
        compiler_params=_cp("parallel"),
    )(ya, yb, pf, pf, goa, gob, wo_p, h)


def _bias_build(table, bucket, maskadd, comm=None):
    def body(tab_ref, bk_ref, ma_ref, o_ref):
        def rows(g, carry):
            r = pl.ds(pl.multiple_of(g * 8, 8), 8)
            bk = bk_ref[0, r, :]
            accs = [jnp.zeros(bk.shape, F32)] * A_HEADS
            for b in range(N_BUCKETS):
                hit = bk == b
                accs = [jnp.where(hit, tab_ref[b, h], accs[h]) for h in range(A_HEADS)]
            ma = ma_ref[0, r, :]
            for h in range(A_HEADS):
                o_ref[0, h, r, :] = (accs[h] + ma) * LOG2E
            return carry

        lax.fori_loop(0, BLK // 8, rows, 0)

    return _call_with_comm(
        body, comm, grid=(4,), name="bias_build",
        in_specs=[pl.BlockSpec(memory_space=pltpu.SMEM), pl.BlockSpec((1, BLK, 512), lambda t: (t, 0, 0)),
                  pl.BlockSpec((1, BLK, 512), lambda t: (t, 0, 0))],
        out_specs=[pl.BlockSpec((1, A_HEADS, BLK, 512), lambda t: (t, 0, 0, 0))],
        out_shape=[jax.ShapeDtypeStruct((4, A_HEADS, BLK, 512), F32)],
        args=(table, bucket, maskadd))


def _bias_grad(s_accs, bucket, comm=None):
    depth = len(s_accs)

    def body(*refs):
        s_refs, bk_ref, o_ref, sum_ref, part_ref = refs[:depth], refs[depth], refs[depth + 1], refs[depth + 2], refs[depth + 3]
        t = pl.program_id(0)

        @pl.when(t == 0)
        def _():
            o_ref[...] = jnp.zeros_like(o_ref)

        total = s_refs[0][0]
        for extra in s_refs[1:]:
            total = total + extra[0]
        sum_ref[...] = total

        def step(b, carry):
            accs = [jnp.zeros((8, 512), F32) for _ in range(A_HEADS)]
            for g in range(BLK // 8):
                rows = pl.ds(g * 8, 8)
                hit = bk_ref[0, rows, :] == b
                for h in range(A_HEADS):
                    accs[h] = accs[h] + jnp.where(hit, sum_ref[h, rows, :], 0.0)
            rows8 = jnp.concatenate([jnp.sum(a, axis=0, keepdims=True) for a in accs], axis=0)
            part_ref[pl.ds(pl.multiple_of(b * A_HEADS, 8), A_HEADS), :] = rows8
            return carry

        lax.fori_loop(0, N_BUCKETS, step, 0)
        o_ref[...] += jnp.broadcast_to(jnp.sum(part_ref[...], axis=1, keepdims=True), o_ref.shape)

    s_spec = pl.BlockSpec((1, A_HEADS, BLK, 512), lambda t: (t, 0, 0, 0))
    return _call_with_comm(
        body, comm, grid=(4,), name="bias_grad",
        in_specs=[s_spec] * depth + [pl.BlockSpec((1, BLK, 512), lambda t: (t, 0, 0))],
        out_specs=[pl.BlockSpec((N_BUCKETS * A_HEADS, 128), lambda t: (0, 0))],
        out_shape=[jax.ShapeDtypeStruct((N_BUCKETS * A_HEADS, 128), F32)],
        scratch_shapes=[pltpu.VMEM((A_HEADS, BLK, 512), F32), pltpu.VMEM((N_BUCKETS * A_HEADS, 512), F32)],
        args=(*s_accs, bucket))


def _win_specs(cfg):
    nj = cfg.NJ
    row = lambda b, j: b * nj + j
    jt = lambda j: jnp.where(j == 0, 0, jnp.where(j == 1, 1, jnp.where(j == nj - 1, 3, 2)))
    slot_rows = [lambda b, j: row(b, 0), lambda b, j: row(b, jnp.maximum(j - 1, 0)), lambda b, j: row(b, j),
                 lambda b, j: row(b, jnp.minimum(j + 1, nj - 1))]
    k_specs = [pl.BlockSpec((BLK, 128), functools.partial(lambda b, j, f: (f(b, j), 4), f=f)) for f in slot_rows]
    v_specs = [pl.BlockSpec((BLK, 128), functools.partial(lambda b, j, f: (f(b, j), 5), f=f)) for f in slot_rows]
    q_spec = pl.BlockSpec((BLK, 512), lambda b, j: (row(b, j), 0))
    bias_spec = pl.BlockSpec((1, A_HEADS, BLK, 512), lambda b, j: (jt(j), 0, 0, 0))
    return row, jt, q_spec, k_specs, v_specs, bias_spec


def _stack4(ref):
    return jnp.concatenate([ref[:, c * 128:(c + 1) * 128] for c in range(4)], axis=0)


def _win_keys(k_refs, v_refs):
    k4 = jnp.concatenate([r[...] for r in k_refs], axis=0)
    v4 = jnp.concatenate([r[...] for r in v_refs], axis=0)
    lane_k = _lane(k4.shape)
    return (jnp.where(lane_k < 64, k4, jnp.zeros_like(k4)), jnp.where(lane_k >= 64, k4, jnp.zeros_like(k4))), v4


def _sink_col(sink_ref, hf):
    rowi = lax.broadcasted_iota(jnp.int32, (4 * BLK, 1), 0)
    col = jnp.full((4 * BLK, 1), sink_ref[4 * hf + 3], F32)
    for c in (2, 1, 0):
        col = jnp.where(rowi < (c + 1) * BLK, sink_ref[4 * hf + c], col)
    return col * LOG2E


def _win_fwd(cfg, pa, bias, sink):
    row, jt, q_spec, k_specs, v_specs, bias_spec = _win_specs(cfg)

    def body(sink_ref, q_ref, k0, k1, k2, k3, v0, v1, v2, v3, b_ref, o_ref, lse_ref):
        kk, v4 = _win_keys((k0, k1, k2, k3), (v0, v1, v2, v3))
        qs = _stack4(q_ref)
        lane_o = _lane((4 * BLK, 128))
        outs, lses = [], []
        for hf in range(2):
            s = _dot_nt(qs, kk[hf]) + b_ref[0, 4 * hf:4 * hf + 4].reshape(4 * BLK, 512)
            sink_col = _sink_col(sink_ref, hf)
            m = jnp.maximum(jnp.max(s, axis=1, keepdims=True), sink_col)
            e = jnp.exp2(s - m)
            den = jnp.sum(e, axis=1, keepdims=True) + jnp.exp2(sink_col - m)
            outs.append(_dot(e.astype(BF16), v4) / den)
            lses.append(m + jnp.log2(den))
        o = jnp.where(lane_o < 64, outs[0], outs[1])
        for c in range(4):
            o_ref[:, c * 128:(c + 1) * 128] = o[c * BLK:(c + 1) * BLK]
        lse_ref[...] = jnp.where(lane_o == 0, lses[0], jnp.where(lane_o == 1, lses[1], 0.0))

    return _pallas(
        body, grid=(cfg.B, cfg.NJ), name="win_fwd",
        in_specs=[pl.BlockSpec(memory_space=pltpu.SMEM), q_spec, *k_specs, *v_specs, bias_spec],
        out_specs=[pl.BlockSpec((BLK, 512), lambda b, j: (row(b, j), 0)), pl.BlockSpec((4 * BLK, 128), lambda b, j: (row(b, j), 0))],
        out_shape=[jax.ShapeDtypeStruct((cfg.TP, 512), F32), jax.ShapeDtypeStruct((4 * cfg.TP, 128), F32)],
        compiler_params=_cp("parallel", "parallel"),
    )(sink, pa, *([pa] * 8), bias)


def _win_bwd(cfg, pa, bias, sink, dya, ya, lse):
    row, jt, q_spec, k_specs, v_specs, bias_spec = _win_specs(cfg)
    nj = cfg.NJ

    def body(sink_ref, q_ref, k0, k1, k2, k3, v0, v1, v2, v3, b_ref, dy_ref, y_ref, lse_ref,
             dq_ref, dkp_ref, dvp_ref, dkm_ref, dvm_ref, s_ref, dsink_ref):
        j = pl.program_id(1)
        kind = jt(j)

        @pl.when((pl.program_id(0) == 0) & (j == 0))
        def _():
            s_ref[...] = jnp.zeros_like(s_ref)

        kk, v4 = _win_keys((k0, k1, k2, k3), (v0, v1, v2, v3))
        qs, dys, ys = _stack4(q_ref), _stack4(dy_ref), _stack4(y_ref)
        lane_o = _lane((4 * BLK, 128))
        half = (lane_o < 64, lane_o >= 64)
        lse_blk = lse_ref[...]
        dq = jnp.zeros((4 * BLK, 128), F32)
        dk4 = jnp.zeros((512, 128), F32)
        dv4 = jnp.zeros((512, 128), F32)
        dsink = jnp.zeros((8, 128), F32)
        lane_s = _lane((8, 128))
        row_s = lax.broadcasted_iota(jnp.int32, (8, 128), 0)
        for hf in range(2):
            lse_h = jnp.sum(jnp.where(lane_o == hf, lse_blk, 0.0), axis=1, keepdims=True)
            s = _dot_nt(qs, kk[hf]) + b_ref[0, 4 * hf:4 * hf + 4].reshape(4 * BLK, 512)
            p = jnp.exp2(s - lse_h)
            do_h = jnp.where(half[hf], dys, 0.0)
            delta = jnp.sum(do_h * ys, axis=1, keepdims=True)
            do_b = do_h.astype(BF16)
            ds = p * (_dot_nt(do_b, v4) - delta)
            s_ref[kind, 4 * hf:4 * hf + 4] += ds.reshape(4, BLK, 512)
            sink_grad = jnp.exp2(_sink_col(sink_ref, hf) - lse_h) * delta
            for c in range(4):
                tot = -jnp.sum(sink_grad[c * BLK:(c + 1) * BLK])
                dsink = jnp.where((row_s == 0) & (lane_s == 4 * hf + c), tot, dsink)
            dsb = (ds * LN2).astype(BF16)
            dq = dq + _dot(dsb, kk[hf])
            dk4 = dk4 + _dot_tn(dsb, jnp.where(half[hf], qs, jnp.zeros_like(qs)))
            dv4 = dv4 + _dot_tn(p.astype(BF16), do_b)
        for c in range(4):
            dq_ref[:, c * 128:(c + 1) * 128] = (dq[c * BLK:(c + 1) * BLK] * QA_SCALE).astype(BF16)
        dkp_ref[0] = dk4
        dvp_ref[0] = dv4

        @pl.when(j == 0)
        def _():
            dkm_ref[...] = dk4[:BLK]
            dvm_ref[...] = dv4[:BLK]

        @pl.when(j > 0)
        def _():
            dkm_ref[...] += dk4[:BLK]
            dvm_ref[...] += dv4[:BLK]

        @pl.when((pl.program_id(0) == 0) & (j == 0))
        def _():
            dsink_ref[...] = dsink

        @pl.when((pl.program_id(0) > 0) | (j > 0))
        def _():
            dsink_ref[...] += dsink

    blk_row = pl.BlockSpec((BLK, 512), lambda b, j: (row(b, j), 0))
    return _pallas(
        body, grid=(cfg.B, nj), name="win_bwd",
        in_specs=[pl.BlockSpec(memory_space=pltpu.SMEM), q_spec, *k_specs, *v_specs, bias_spec, blk_row, blk_row,
                  pl.BlockSpec((4 * BLK, 128), lambda b, j: (row(b, j), 0))],
        out_specs=[blk_row,
                   pl.BlockSpec((1, 512, 128), lambda b, j: (row(b, j), 0, 0)), pl.BlockSpec((1, 512, 128), lambda b, j: (row(b, j), 0, 0)),
                   pl.BlockSpec((BLK, 128), lambda b, j: (b, 0)), pl.BlockSpec((BLK, 128), lambda b, j: (b, 0)),
                   pl.BlockSpec((4, A_HEADS, BLK, 512), lambda b, j: (0, 0, 0, 0)),
                   pl.BlockSpec((8, 128), lambda b, j: (0, 0))],
        out_shape=[jax.ShapeDtypeStruct((cfg.TP, 512), BF16),
                   jax.ShapeDtypeStruct((cfg.B * nj, 512, 128), F32), jax.ShapeDtypeStruct((cfg.B * nj, 512, 128), F32),
                   jax.ShapeDtypeStruct((cfg.B * BLK, 128), F32), jax.ShapeDtypeStruct((cfg.B * BLK, 128), F32),
                   jax.ShapeDtypeStruct((4, A_HEADS, BLK, 512), F32),
                   jax.ShapeDtypeStruct((8, 128), F32)],
        compiler_params=_cp("arbitrary", "arbitrary"),
    )(sink, pa, *([pa] * 8), bias, dya, ya, lse)


def _win_dkv_combine(cfg, dkp, dvp, dkm, dvm):
    nj = cfg.NJ

    def body(kp, vp, km, vm, o_ref):
        o_ref[:BLK, :128] = km[...].astype(BF16)
        o_ref[:BLK, 128:] = vm[...].astype(BF16)
        for kb in range(1, nj):
            for col, part in ((0, kp), (128, vp)):
                tot = part[kb, 2 * BLK:3 * BLK] + part[kb - 1, 3 * BLK:4 * BLK]
                if kb + 1 < nj:
                    tot = tot + part[kb + 1, BLK:2 * BLK]
                o_ref[kb * BLK:(kb + 1) * BLK, col:col + 128] = tot.astype(BF16)

    return _pallas(
        body, grid=(cfg.B,), name="win_dkv_combine",
        in_specs=[pl.BlockSpec((nj, 512, 128), lambda b: (b, 0, 0)), pl.BlockSpec((nj, 512, 128), lambda b: (b, 0, 0)),
                  pl.BlockSpec((BLK, 128), lambda b: (b, 0)), pl.BlockSpec((BLK, 128), lambda b: (b, 0))],
        out_specs=pl.BlockSpec((cfg.LP, 256), lambda b: (b, 0)),
        out_shape=jax.ShapeDtypeStruct((cfg.TP, 256), BF16),
        compiler_params=_cp("parallel"),
    )(dkp, dvp, dkm, dvm)


def _pair_blockdiag(q):
    lane = _lane(q.shape)
    return jnp.concatenate([jnp.where(lane < 128, q, jnp.zeros_like(q)), jnp.where(lane >= 128, q, jnp.zeros_like(q))], axis=0)


def _mla_fwd(cfg, q, kt, v, comm=None):
    nj, lp = cfg.NJ, cfg.LP

    def body(q_ref, kt_ref, v_ref, o_ref, lse_ref, s_even, s_odd):
        i = pl.program_id(2)
        lane_o = _lane((BLK, 128))

        def logits(s_write):
            s_write[...] = _dot(_pair_blockdiag(q_ref[...]), kt_ref[...])

        def finish(s_read):
            s = s_read[...]
            m = jnp.max(s, axis=1, keepdims=True)
            e = jnp.exp2(s - m)
            den = jnp.sum(e, axis=1, keepdims=True)
            pv = _dot(e.astype(BF16), v_ref[...]) / den
            o_ref[...] = jnp.where(lane_o < 64, pv[:BLK], pv[BLK:])
            lse_ref[0] = jnp.broadcast_to(m + jnp.log2(den), (2 * BLK, 128))

        odd = i % 2 == 1

        @pl.when(i == 0)
        def _():
            logits(s_even)

        @pl.when(odd & (i < nj))
        def _():
            logits(s_odd)
            finish(s_even)

        @pl.when(jnp.logical_not(odd) & (i > 0) & (i < nj))
        def _():
            logits(s_even)
            finish(s_odd)

        @pl.when(i == nj)
        def _():
            finish(s_even if nj % 2 == 1 else s_odd)

    cur = lambda b, i: b * nj + jnp.minimum(i, nj - 1)
    prev = lambda b, i: b * nj + jnp.maximum(i - 1, 0)
    return _call_with_comm(
        body, comm, grid=(cfg.B, 4, nj + 1), name="mla_fwd",
        in_specs=[pl.BlockSpec((BLK, 256), lambda b, p, i: (cur(b, i), p)), pl.BlockSpec((256, lp), lambda b, p, i: (b * 4 + p, 0)),
                  pl.BlockSpec((lp, 128), lambda b, p, i: (b, p))],
        out_specs=[pl.BlockSpec((BLK, 128), lambda b, p, i: (prev(b, i), p)),
                   pl.BlockSpec((1, 2 * BLK, 128), lambda b, p, i: (p, prev(b, i), 0))],
        out_shape=[jax.ShapeDtypeStruct((cfg.TP, 512), F32), jax.ShapeDtypeStruct((4, 2 * cfg.TP, 128), F32)],
        scratch_shapes=[pltpu.VMEM((2 * BLK, lp), F32), pltpu.VMEM((2 * BLK, lp), F32)],
        args=(q, kt, v))


def _mla_bwd(cfg, q, k, kt, vt, dyb, yb, lse, comm=None):
    nj, lp = cfg.NJ, cfg.LP

    def body(q_ref, k_ref, kt_ref, vt_ref, dy_ref, y_ref, lse_ref, dq_ref, dk_ref, dv_ref):
        i = pl.program_id(2)

        @pl.when(i == 0)
        def _():
            dk_ref[...] = jnp.zeros_like(dk_ref)
            dv_ref[...] = jnp.zeros_like(dv_ref)

        lane_o = _lane((BLK, 128))
        qbd = _pair_blockdiag(q_ref[...])
        dy, y = dy_ref[...], y_ref[...]
        do_s = jnp.concatenate([jnp.where(lane_o < 64, dy, 0.0), jnp.where(lane_o >= 64, dy, 0.0)], axis=0)
        delta = jnp.sum(do_s * jnp.concatenate([y, y], axis=0), axis=1, keepdims=True)
        do_b = do_s.astype(BF16)
        p = jnp.exp2(_dot(qbd, kt_ref[...]) - lse_ref[0][:, :1])
        ds = p * (_dot(do_b, vt_ref[...]) - delta)
        dsb = (ds * LN2).astype(BF16)
        dq2 = _dot(dsb, k_ref[...])
        dq_ref[...] = jnp.where(_lane((BLK, 256)) < 128, dq2[:BLK], dq2[BLK:]) * Q_SCALE
        dk_ref[...] += _dot_tn(dsb, qbd)
        dv_ref[...] += _dot_tn(p.astype(BF16), do_b)

    return _call_with_comm(
        body, comm, grid=(cfg.B, 4, nj), name="mla_bwd",
        in_specs=[pl.BlockSpec((BLK, 256), lambda b, p, i: (b * nj + i, p)), pl.BlockSpec((lp, 256), lambda b, p, i: (b, p)),
                  pl.BlockSpec((256, lp), lambda b, p, i: (b * 4 + p, 0)), pl.BlockSpec((128, lp), lambda b, p, i: (b * 4 + p, 0)),
                  pl.BlockSpec((BLK, 128), lambda b, p, i: (b * nj + i, p)), pl.BlockSpec((BLK, 128), lambda b, p, i: (b * nj + i, p)),
                  pl.BlockSpec((1, 2 * BLK, 128), lambda b, p, i: (p, b * nj + i, 0))],
        out_specs=[pl.BlockSpec((BLK, 256), lambda b, p, i: (b * nj + i, p)), pl.BlockSpec((lp, 256), lambda b, p, i: (b, p)),
                   pl.BlockSpec((lp, 128), lambda b, p, i: (b, p))],
        out_shape=[jax.ShapeDtypeStruct((cfg.TP, 1024), F32), jax.ShapeDtypeStruct((cfg.TP, 1024), F32),
                   jax.ShapeDtypeStruct((cfg.TP, 512), F32)],
        args=(q, k, kt, vt, dyb, yb, lse))


def _loss_bwd(cfg, h, target, gf):
    nj, nb = cfg.NJ, cfg.NB
    tm = 2 * BLK

    def target_block(g):
        return (g // nj) * nb + jnp.maximum(g % nj - 1, 0)

    def body(h_ref, ta_ref, tb_ref, g_ref, dh_ref, loss_ref, dg_ref):
        t = pl.program_id(0)

        @pl.when(t == 0)
        def _():
            loss_ref[...] = jnp.zeros_like(loss_ref)
            dg_ref[...] = jnp.zeros_like(dg_ref)

        g = g_ref[...]
        first = (lax.broadcasted_iota(jnp.int32, (8, 128), 0) == 0) & (_lane((8, 128)) == 0)
        for half, t_ref in enumerate((ta_ref, tb_ref)):
            rows = slice(half * BLK, (half + 1) * BLK)
            real = (2 * t + half) % nj > 0
            xh, r = _rms(h_ref[rows, :])
            err = jnp.where(real, xh * g - t_ref[...], 0.0)
            loss_ref[...] += jnp.where(first, (0.5 / D_MODEL) * jnp.sum(err * err), 0.0)
            dy = err * (1.0 / D_MODEL)
            dg_ref[...] += jnp.sum(dy * xh, axis=0, keepdims=True)
            dh_ref[rows, :] = _rms_bwd(xh, r, dy * g)

    return _pallas(
        body, grid=(cfg.TP // tm,), name="loss_bwd",
        in_specs=[pl.BlockSpec((tm, D_MODEL), lambda t: (t, 0)),
                  pl.BlockSpec((BLK, D_MODEL), lambda t: (target_block(2 * t), 0)),
                  pl.BlockSpec((BLK, D_MODEL), lambda t: (target_block(2 * t + 1), 0)),
                  pl.BlockSpec((1, D_MODEL), lambda t: (0, 0))],
        out_specs=[pl.BlockSpec((tm, D_MODEL), lambda t: (t, 0)), pl.BlockSpec((8, 128), lambda t: (0, 0)),
                   pl.BlockSpec((1, D_MODEL), lambda t: (0, 0))],
        out_shape=[jax.ShapeDtypeStruct((cfg.TP, D_MODEL), F32), jax.ShapeDtypeStruct((8, 128), F32),
                   jax.ShapeDtypeStruct((1, D_MODEL), F32)],
        compiler_params=_cp("arbitrary"),
    )(h, target, target, gf)


def _out_bwd(cfg, dh, ya, yb, pf, goa, gob, wo_p):
    tm = 256

    def body(dh_ref, ya_ref, yb_ref, ga_ref, gb_ref, goa_ref, gob_ref, w_ref,
             dya_ref, dyb_ref, dg_ref, dw_ref, dgoa_ref, dgob_ref):
        @pl.when(pl.program_id(0) == 0)
        def _():
            dw_ref[...] = jnp.zeros_like(dw_ref)
            dgoa_ref[...] = jnp.zeros_like(dgoa_ref)
            dgob_ref[...] = jnp.zeros_like(dgob_ref)

        ga, gb, goa, gob = ga_ref[...], gb_ref[...], goa_ref[...], gob_ref[...]
        xa, ra, xb, rb, sga, sgb, y_a, y_b = _gate_halves(ya_ref[...], yb_ref[...], ga, gb, goa, gob)
        dhb = dh_ref[...].astype(BF16)
        dw_ref[...] += _dot_tn(jnp.concatenate([y_a, y_b], axis=1).astype(BF16), dhb)
        dy = _dot_nt(dhb, w_ref[...])
        for (dyh, x, r, g, sg, go, dy_out, dgo_ref, col) in (
                (dy[:, :512], xa, ra, ga, sga, goa, dya_ref, dgoa_ref, 0), (dy[:, 512:], xb, rb, gb, sgb, gob, dyb_ref, dgob_ref, 512)):
            dn = dyh * (g * sg)
            dg_ref[:, col:col + 512] = (dyh * (x * go) * (sg * (1.0 + g * (1.0 - sg)))).astype(BF16)
            dgo_ref[...] += jnp.sum(dn * x, axis=0, keepdims=True)
            dy_out[...] = _rms_bwd(x, r, dn * go)

    half = lambda c: pl.BlockSpec((tm, 512), lambda i: (i, c))
    vec = pl.BlockSpec((1, 512), lambda i: (0, 0))
    return _pallas(
        body, grid=(cfg.TP // tm,), name="out_bwd",
        in_specs=[pl.BlockSpec((tm, D_MODEL), lambda i: (i, 0)), half(0), half(0), half(0), half(1), vec, vec,
                  pl.BlockSpec((D_MODEL, D_MODEL), lambda i: (0, 0))],
        out_specs=[half(0), half(0), pl.BlockSpec((tm, D_MODEL), lambda i: (i, 0)),
                   pl.BlockSpec((D_MODEL, D_MODEL), lambda i: (0, 0)), vec, vec],
        out_shape=[jax.ShapeDtypeStruct((cfg.TP, 512), F32), jax.ShapeDtypeStruct((cfg.TP, 512), F32),
                   jax.ShapeDtypeStruct((cfg.TP, D_MODEL), BF16), jax.ShapeDtypeStruct((D_MODEL, D_MODEL), F32),
                   jax.ShapeDtypeStruct((1, 512), F32), jax.ShapeDtypeStruct((1, 512), F32)],
        compiler_params=_cp("arbitrary"),
    )(dh, ya, yb, pf, pf, goa, gob, wo_p)


def _lat_bwd(cfg, dq, dk, dv, pf, gq, gkv, wq_p, wkv_p, c_tab, s_tab):
    nj = cfg.NJ

    def body(dq_ref, dk_ref, dv_ref, cq_ref, ckv_ref, gq_ref, gkv_ref, wq_ref, wkv_ref, c_ref, s_ref,
             dl_ref, dwq_ref, dwkv_ref, dgq_ref, dgkv_ref):
        @pl.when((pl.program_id(0) == 0) & (pl.program_id(1) == 0))
        def _():
            dwq_ref[...] = jnp.zeros_like(dwq_ref)
            dwkv_ref[...] = jnp.zeros_like(dwkv_ref)
            dgq_ref[...] = jnp.zeros_like(dgq_ref)
            dgkv_ref[...] = jnp.zeros_like(dgkv_ref)

        c1, s1 = c_ref[...], s_ref[...]
        c8, s8 = jnp.tile(c1, (1, 8)), jnp.tile(s1, (1, 8))
        dq_r = dq_ref[...]
        dqp = (dq_r * c8 + _swap_rope(dq_r * s8)).astype(BF16)
        gq = gq_ref[...]
        xq, rq = _rms(cq_ref[...])
        dwq_ref[...] += _dot_tn((xq * gq).astype(BF16), dqp)
        dn = _dot_nt(dqp, wq_ref[...])
        dgq_ref[...] += jnp.sum(dn * xq, axis=0, keepdims=True)
        dl_ref[:, :256] = _rms_bwd(xq, rq, dn * gq).astype(BF16)

        dk_r = dk_ref[...]
        dkr = dk_r[:, :128]
        for hd in range(1, 8):
            dkr = dkr + dk_r[:, hd * 128:(hd + 1) * 128]
        lane1 = _lane(dkr.shape)
        dkr = jnp.where((lane1 >= 64) & (lane1 < 96), dkr, 0.0)
        dl_ref[:, 384:] = (dkr * c1 + _swap_rope(dkr * s1)).astype(BF16)
        dkv = jnp.concatenate([dk_r, dv_ref[...]], axis=1).astype(BF16)
        gkv = gkv_ref[...]
        xk, rk = _rms(ckv_ref[...])
        dwkv_ref[...] += _dot_tn((xk * gkv).astype(BF16), dkv)
        dn2 = _dot_nt(dkv, wkv_ref[...])
        dgkv_ref[...] += jnp.sum(dn2 * xk, axis=0, keepdims=True)
        dl_ref[:, 256:384] = _rms_bwd(xk, rk, dn2 * gkv).astype(BF16)

    row = lambda b, j: b * nj + j
    const = lambda shape: pl.BlockSpec(shape, lambda b, j: (0, 0))
    return _pallas(
        body, grid=(cfg.B, nj), name="lat_bwd",
        in_specs=[pl.BlockSpec((BLK, 1024), lambda b, j: (row(b, j), 0)), pl.BlockSpec((BLK, 1024), lambda b, j: (row(b, j), 0)),
                  pl.BlockSpec((BLK, 512), lambda b, j: (row(b, j), 0)),
                  pl.BlockSpec((BLK, 256), lambda b, j: (row(b, j), 4)), pl.BlockSpec((BLK, 128), lambda b, j: (row(b, j), 10)),
                  const((1, 256)), const((1, 128)), const((256, 1024)), const((128, 1536)),
                  pl.BlockSpec((BLK, 128), lambda b, j: (j, 0)), pl.BlockSpec((BLK, 128), lambda b, j: (j, 0))],
        out_specs=[pl.BlockSpec((BLK, 512), lambda b, j: (row(b, j), 0)), const((256, 1024)), const((128, 1536)),
                   const((1, 256)), const((1, 128))],
        out_shape=[jax.ShapeDtypeStruct((cfg.TP, 512), BF16), jax.ShapeDtypeStruct((256, 1024), F32),
                   jax.ShapeDtypeStruct((128, 1536), F32), jax.ShapeDtypeStruct((1, 256), F32), jax.ShapeDtypeStruct((1, 128), F32)],
        compiler_params=_cp("arbitrary", "arbitrary"),
    )(dq, dk, dv, pf, pf, gq, gkv, wq_p, wkv_p, c_tab, s_tab)


def _inproj_bwd(cfg, h, g, w_p, dqa, dkva, dgate, dlat, dh, comm=None):
    tm = 256

    def body(h_ref, g_ref, w_ref, dqa_ref, dkva_ref, dg_ref, dl_ref, dh_ref, o_ref, dw_ref, dgn_ref):
        @pl.when(pl.program_id(0) == 0)
        def _():
            dw_ref[...] = jnp.zeros_like(dw_ref)
            dgn_ref[...] = jnp.zeros_like(dgn_ref)

        g = g_ref[...]
        xh, r = _rms(h_ref[...])
        dproj = jnp.concatenate([dqa_ref[...], dkva_ref[...], dg_ref[...], dl_ref[...]], axis=1)
        dw_ref[...] += _dot_tn((xh * g).astype(BF16), dproj)
        du = _dot_nt(dproj, w_ref[...])
        dgn_ref[...] += jnp.sum(du * xh, axis=0, keepdims=True)
        o_ref[...] = dh_ref[...] + _rms_bwd(xh, r, du * g)

    rows = lambda w: pl.BlockSpec((tm, w), lambda i: (i, 0))
    return _call_with_comm(
        body, comm, grid=(cfg.TP // tm,), name="inproj_bwd",
        in_specs=[rows(D_MODEL), pl.BlockSpec((1, D_MODEL), lambda i: (0, 0)), pl.BlockSpec((D_MODEL, W_IN_P), lambda i: (0, 0)),
                  rows(512), rows(256), rows(1024), rows(512), rows(D_MODEL)],
        out_specs=[rows(D_MODEL), pl.BlockSpec((D_MODEL, W_IN_P), lambda i: (0, 0)), pl.BlockSpec((1, D_MODEL), lambda i: (0, 0))],
        out_shape=[jax.ShapeDtypeStruct((cfg.TP, D_MODEL), F32), jax.ShapeDtypeStruct((D_MODEL, W_IN_P), F32),
                   jax.ShapeDtypeStruct((1, D_MODEL), F32)],
        args=(h, g, w_p, dqa, dkva, dgate, dlat, dh))


def _meta_grad(cfg, dh):
    def body(d_ref, o_ref):
        @pl.when(pl.program_id(0) == 0)
        def _():
            o_ref[...] = d_ref[...]

        @pl.when(pl.program_id(0) > 0)
        def _():
            o_ref[...] += d_ref[...]

    return _pallas(
        body, grid=(cfg.B,), name="meta_grad",
        in_specs=[pl.BlockSpec((BLK, D_MODEL), lambda b: (b * cfg.NJ, 0))],
        out_specs=pl.BlockSpec((BLK, D_MODEL), lambda b: (0, 0)),
        out_shape=jax.ShapeDtypeStruct((BLK, D_MODEL), F32),
        compiler_params=_cp("arbitrary"),
    )(dh)


MATRICES = ("w_in", "w_uq", "w_ukv", "w_out")


def _local_grads(cfg, x, target, meta_of, table, small, weight_of, rider=None):
    def ride(stage, i, mats):
        hook = rider(stage, i, mats) if rider else None
        return hook if hook else (None, lambda res: None)

    depth = small["norm_in"].shape[0]
    rel, vis = _window_structure(cfg.NJ)
    bucket = _t5_bucket(jnp.asarray(rel))
    maskadd = jnp.asarray(np.where(vis, 0.0, NEG).astype(np.float32))
    c_tab, s_tab = _rope_tables(cfg)
    comm, deliver = ride("bias_build", 0, {})
    bias, *travelled = _bias_build(table, bucket, maskadd, comm)
    deliver(travelled)

    meta_blk = jnp.concatenate([meta_of(), jnp.zeros((BLK - N_META, D_MODEL), F32)], axis=0)
    h = jnp.concatenate([jnp.broadcast_to(meta_blk[None], (cfg.B, BLK, D_MODEL)), x], axis=1).reshape(cfg.TP, D_MODEL)

    wp, saved = [], []
    for i in range(depth):
        w = dict(w_in=_w_in_to_p(weight_of(i, "w_in")),
                 g_in=small["norm_in"][i][None], gq=small["norm_q_lat"][i][None], gkv=small["norm_kv_lat"][i][None],
                 goa=_perm_heads64(small["norm_out_a"][i], 0)[None], gob=small["norm_out_b"][i][None], sink=small["sink_a"][i])
        wp.append(w)
        comm, deliver = ride("inproj_fwd", i, {})
        pa, pf, *travelled = _inproj_fwd(cfg, h, w["g_in"], w["w_in"], comm)
        deliver(travelled)
        w.update(w_uq=_w_uq_to_p(weight_of(i, "w_uq")), w_ukv=_w_ukv_to_p(weight_of(i, "w_ukv")), w_out=_w_out_to_p(weight_of(i, "w_out")))
        q, k, v, kt, vt = _lat_fwd(cfg, pf, w["gq"], w["gkv"], w["w_uq"], w["w_ukv"], c_tab, s_tab)
        ya, lse_a = _win_fwd(cfg, pa, bias, w["sink"])
        comm, deliver = ride("mla_fwd", i, {})
        yb, lse_b, *travelled = _mla_fwd(cfg, q, kt, v, comm)
        deliver(travelled)
        h_next = _out_fwd(cfg, ya, yb, pf, w["goa"], w["gob"], w["w_out"], h)
        saved.append(dict(h=h, pa=pa, pf=pf, q=q, k=k, kt=kt, vt=vt, ya=ya, lse_a=lse_a, yb=yb, lse_b=lse_b))
        h = h_next

    dh, loss_tile, d_norm_final = _loss_bwd(cfg, h, target.reshape(cfg.B * cfg.S, D_MODEL), small["norm_final"][None])

    grads = {k_: [] for k_ in ("norm_in", "sink_a", "norm_q_lat", "norm_kv_lat", "norm_out_a", "norm_out_b")}
    mats, s_accs = {}, []
    for i in reversed(range(depth)):
        w, sv = wp[i], saved[i]
        dya, dyb, dgate, dwo, dgoa, dgob = _out_bwd(cfg, dh, sv["ya"], sv["yb"], sv["pf"], w["goa"], w["gob"], w["w_out"])
        dqa, dkp, dvp, dkm, dvm, s_acc, dsink = _win_bwd(cfg, sv["pa"], bias, w["sink"], dya, sv["ya"], sv["lse_a"])
        dkva = _win_dkv_combine(cfg, dkp, dvp, dkm, dvm)
        comm, deliver = ride("mla_bwd", i, mats)
        dq, dk, dv, *travelled = _mla_bwd(cfg, sv["q"], sv["k"], sv["kt"], sv["vt"], dyb, sv["yb"], sv["lse_b"], comm)
        deliver(travelled)
        dlat, dwq, dwkv, dgq, dgkv = _lat_bwd(cfg, dq, dk, dv, sv["pf"], w["gq"], w["gkv"], w["w_uq"], w["w_ukv"], c_tab, s_tab)
        mats[i] = dict(w_uq=_w_uq_from_p(dwq), w_ukv=_w_ukv_from_p(dwkv), w_out=_w_out_from_p(dwo))
        comm, deliver = ride("inproj_bwd", i, mats)
        dh, dwin, dgin, *travelled = _inproj_bwd(cfg, sv["h"], w["g_in"], w["w_in"], dqa, dkva, dgate, dlat, dh, comm)
        deliver(travelled)
        s_accs.append(s_acc)
        mats[i]["w_in"] = _w_in_from_p(dwin)
        grads["norm_in"].append(dgin[0])
        grads["sink_a"].append(dsink[0, :A_HEADS])
        grads["norm_q_lat"].append(dgq[0])
        grads["norm_kv_lat"].append(dgkv[0])
        grads["norm_out_a"].append(_unperm_heads64(dgoa[0], 0))
        grads["norm_out_b"].append(dgob[0])

    out = {k_: jnp.stack(v_[::-1]) for k_, v_ in grads.items()}
    mats["meta_tokens"] = _meta_grad(cfg, dh)[:N_META]
    comm, deliver = ride("bias_grad", 0, mats)
    dtable, *travelled = _bias_grad(s_accs, bucket, comm)
    deliver(travelled)
    out["rel_bias_table"] = dtable[:, 0].reshape(N_BUCKETS, A_HEADS)
    out["norm_final"] = d_norm_final[0]
    return loss_tile[0, 0], dh.reshape(cfg.B, cfg.LP, D_MODEL)[:, BLK:], out, mats


MESH = pl.DeviceIdType.MESH
ANY = pl.BlockSpec(memory_space=pl.ANY)


def _place():
    x, y, c = lax.axis_index("x"), lax.axis_index("y"), lax.axis_index("c")
    others = [(1 - x, y), (x, 1 - y), (1 - x, 1 - y)]
    return x, y, c, others


Comm = collections.namedtuple("Comm", "inputs out_shapes scratch start wait")


def _gather_comm(shards):
    n = len(shards)

    def copies(ins, outs, sems, arriving):
        send_sems, recv_sems, local_sems = sems
        x, y, c, others = _place()
        k_me = 2 * x + y
        local = [pltpu.make_async_copy(ins[a], outs[a].at[k_me], local_sems.at[a]) for a in range(n)]
        remote = [pltpu.make_async_remote_copy(src_ref=ins[a], dst_ref=outs[a].at[2 * ox + oy if arriving else k_me],
                                               send_sem=send_sems.at[3 * a + j], recv_sem=recv_sems.at[3 * a + j],
                                               device_id=(ox, oy, c), device_id_type=MESH)
                  for a in range(n) for j, (ox, oy) in enumerate(others)]
        return local, remote

    def start(ins, outs, sems):
        local, sends = copies(ins, outs, sems, arriving=False)
        for cp in local + sends:
            cp.start()

    def wait(ins, outs, sems):
        local, recvs = copies(ins, outs, sems, arriving=True)
        for cp in recvs:
            cp.wait_recv()
        for cp in recvs:
            cp.wait_send()
        for cp in local:
            cp.wait()

    return Comm(list(shards), [jax.ShapeDtypeStruct((4, *s.shape), s.dtype) for s in shards],
                [pltpu.SemaphoreType.DMA((3 * n,)), pltpu.SemaphoreType.DMA((3 * n,)), pltpu.SemaphoreType.DMA((n,))], start, wait)


def _gather_halves_comm(shards):
    n = len(shards)

    def copies(ins, outs, sems, kind):
        send_sems, recv_sems, fwd_send_sems, fwd_recv_sems, local_sems = sems
        x, y, c, others = _place()
        k_me = 2 * x + y

        def half(ref, which):
            rows = ref.shape[0] // 2
            return ref.at[pl.ds(pl.multiple_of(which * rows, 8), rows)]

        if kind == "local":
            return [pltpu.make_async_copy(ins[a], outs[a].at[k_me], local_sems.at[a]) for a in range(n)]
        made = []
        for a in range(n):
            for j, (ox, oy) in enumerate(others):
                slot = outs[a].at[k_me if kind == "sent" else 2 * ox + oy]
                if kind in ("sent", "arrived"):
                    made.append(pltpu.make_async_remote_copy(
                        src_ref=half(ins[a], c), dst_ref=half(slot, c), send_sem=send_sems.at[3 * a + j],
                        recv_sem=recv_sems.at[3 * a + j], device_id=(ox, oy, c), device_id_type=MESH))
                else:
                    which = c if kind == "forward" else 1 - c
                    made.append(pltpu.make_async_remote_copy(
                        src_ref=half(slot, which), dst_ref=half(slot, which), send_sem=fwd_send_sems.at[3 * a + j],
                        recv_sem=fwd_recv_sems.at[3 * a + j], device_id=(x, y, 1 - c), device_id_type=MESH))
        return made

    def start(ins, outs, sems):
        for cp in copies(ins, outs, sems, "local") + copies(ins, outs, sems, "sent"):
            cp.start()

    def wait(ins, outs, sems):
        arrived, forward = copies(ins, outs, sems, "arrived"), copies(ins, outs, sems, "forward")
        for came, on in zip(arrived, forward):
            came.wait_recv()
            on.start()
        for cp in copies(ins, outs, sems, "forwarded"):
            cp.wait_recv()
        for cp in arrived + forward:
            cp.wait_send()
        for cp in copies(ins, outs, sems, "local"):
            cp.wait()

    return Comm(list(shards), [jax.ShapeDtypeStruct((4, *s.shape), s.dtype) for s in shards],
                [pltpu.SemaphoreType.DMA((3 * n,))] * 4 + [pltpu.SemaphoreType.DMA((n,))], start, wait)


def _scatter_comm(parts):
    n = len(parts)

    def copies(ins, outs, sems):
        send_sems, recv_sems = sems
        x, y, c, others = _place()
        return [pltpu.make_async_remote_copy(src_ref=ins[a].at[2 * ox + oy], dst_ref=outs[a].at[j], send_sem=send_sems.at[3 * a + j],
                                             recv_sem=recv_sems.at[3 * a + j], device_id=(ox, oy, c), device_id_type=MESH)
                for a in range(n) for j, (ox, oy) in enumerate(others)]

    def start(ins, outs, sems):
        for cp in copies(ins, outs, sems):
            cp.start()

    def wait(ins, outs, sems):
        cps = copies(ins, outs, sems)
        for cp in cps:
            cp.wait_recv()
        for cp in cps:
            cp.wait_send()

    return Comm(list(parts), [jax.ShapeDtypeStruct((3, *p.shape[1:]), p.dtype) for p in parts],
                [pltpu.SemaphoreType.DMA((3 * n,)), pltpu.SemaphoreType.DMA((3 * n,))], start, wait)


def _call_with_comm(body, comm, *, grid, name, in_specs, out_specs, out_shape, args, scratch_shapes=()):
    if comm is None:
        return _pallas(body, grid=grid, name=name, in_specs=in_specs, out_specs=out_specs, out_shape=out_shape,
                              scratch_shapes=list(scratch_shapes), compiler_params=_cp(*["arbitrary"] * len(grid)))(*args)
    n_in, n_out, ci, co, ns = len(in_specs), len(out_specs), len(comm.inputs), len(comm.out_shapes), len(scratch_shapes)

    def wrapped(*refs):
        ins, cins = refs[:n_in], refs[n_in:n_in + ci]
        outs, couts = refs[n_in + ci:n_in + ci + n_out], refs[n_in + ci + n_out:n_in + ci + n_out + co]
        scratch, sems = refs[n_in + ci + n_out + co:n_in + ci + n_out + co + ns], refs[n_in + ci + n_out + co + ns:]
        ids = [pl.program_id(a) for a in range(len(grid))]
        first = functools.reduce(jnp.logical_and, [i == 0 for i in ids])
        last = functools.reduce(jnp.logical_and, [i == g - 1 for i, g in zip(ids, grid)])

        @pl.when(first)
        def _():
            comm.start(cins, couts, sems)

        body(*ins, *outs, *scratch)

        @pl.when(last)
        def _():
            comm.wait(cins, couts, sems)

    return _pallas(
        wrapped, grid=grid, name=name + "_comm", in_specs=[*in_specs, *[ANY] * ci], out_specs=[*out_specs, *[ANY] * co],
        out_shape=[*out_shape, *comm.out_shapes], scratch_shapes=[*scratch_shapes, *comm.scratch],
        compiler_params=_cp(*["arbitrary"] * len(grid)))(*args, *comm.inputs)


def _allreduce_and_swap(v, arrs):
    n = len(arrs)

    def body(*refs):
        v_ref, ins = refs[0], refs[1:1 + n]
        o_ref, outs = refs[1 + n], refs[2 + n:2 + 2 * n]
        buf, send_sems, recv_sems, sib_send_sems, sib_recv_sems = refs[2 + 2 * n:]
        x, y, c, _ = _place()
        swaps = [pltpu.make_async_remote_copy(src_ref=ins[a], dst_ref=outs[a], send_sem=sib_send_sems.at[a], recv_sem=sib_recv_sems.at[a],
                                              device_id=(x, y, 1 - c), device_id_type=MESH) for a in range(n)]
        for cp in swaps:
            cp.start()
        me = 4 * x + 2 * y + c
        buf[me] = v_ref[...]

        def copy(r):
            tx, ty, tc = (x + (r >> 2)) % 2, (y + ((r >> 1) & 1)) % 2, (c + (r & 1)) % 2
            return tx, ty, tc

        sends = []
        for r in range(1, 8):
            tx, ty, tc = copy(r)
            sends.append(pltpu.make_async_remote_copy(src_ref=v_ref, dst_ref=buf.at[me], send_sem=send_sems.at[r - 1],
                                                      recv_sem=recv_sems.at[r - 1], device_id=(tx, ty, tc), device_id_type=MESH))
        for cp in sends:
            cp.start()
        for r in range(1, 8):
            tx, ty, tc = copy(r)
            pltpu.make_async_remote_copy(src_ref=v_ref, dst_ref=buf.at[4 * tx + 2 * ty + tc], send_sem=send_sems.at[r - 1],
                                         recv_sem=recv_sems.at[r - 1], device_id=(tx, ty, tc), device_id_type=MESH).wait_recv()
        for cp in sends:
            cp.wait_send()
        acc = buf[0]
        for d in range(1, 8):
            acc = acc + buf[d]
        o_ref[...] = acc
        for cp in swaps:
            cp.wait_recv()
        for cp in swaps:
            cp.wait_send()

    vmem = pl.BlockSpec(memory_space=pltpu.VMEM)
    res = pl.pallas_call(
        body, name="allreduce_and_swap", in_specs=[vmem] + [ANY] * n, out_specs=[vmem] + [ANY] * n,
        out_shape=[jax.ShapeDtypeStruct(v.shape, F32)] + [jax.ShapeDtypeStruct(a.shape, a.dtype) for a in arrs],
        scratch_shapes=[pltpu.VMEM((8, *v.shape), F32), pltpu.SemaphoreType.DMA((7,)), pltpu.SemaphoreType.DMA((7,)),
                        pltpu.SemaphoreType.DMA((n,)), pltpu.SemaphoreType.DMA((n,))],
    )(v, *arrs)
    return res[0], res[1:]


def _rows_view(a):
    return a.reshape(-1, a.shape[-1])


def _elementwise(name, fn, ins, n_out):
    rows, cols = ins[0].shape
    tm = min(rows, 256)
    spec = pl.BlockSpec((tm, cols), lambda i: (i, 0))

    def body(*refs):
        outs = fn(*[r[...] for r in refs[:len(ins)]])
        for o_ref, o in zip(refs[len(ins):], outs):
            o_ref[...] = o

    return _pallas(
        body, grid=(rows // tm,), name=name, in_specs=[spec] * len(ins), out_specs=[spec] * n_out,
        out_shape=[jax.ShapeDtypeStruct((rows, cols), F32)] * n_out, compiler_params=_cp("parallel"),
    )(*ins)


def _sum_parts(name, own, recv):
    def fn(o, r0, r1, r2):
        return (o + r0.astype(F32) + r1.astype(F32) + r2.astype(F32),)

    return _elementwise("sum_parts_" + name, fn, [own, recv[0], recv[1], recv[2]], 1)[0]


def _adamw(name, w, m, v, g_parts):
    def fn(w_, m_, v_, *gs):
        g = gs[0]
        for extra in gs[1:]:
            g = g + extra
        m_new = ADAM_B1 * m_ + (1.0 - ADAM_B1) * g
        v_new = ADAM_B2 * v_ + (1.0 - ADAM_B2) * (g * g)
        m_hat = m_new / (1.0 - ADAM_B1 ** ADAM_STEP)
        v_hat = v_new / (1.0 - ADAM_B2 ** ADAM_STEP)
        delta = -ADAM_LR * (m_hat / (jnp.sqrt(v_hat) + ADAM_EPS) + ADAM_WD * w_)
        return g, delta, m_new, v_new

    return _elementwise("adamw_" + name, fn, [w, m, v, *g_parts], 4)


MAT_AXIS = {"w_in": 1, "w_uq": 1, "w_ukv": 1, "w_out": 0}
SMALL = ("rel_bias_table", "norm_in", "sink_a", "norm_q_lat", "norm_kv_lat", "norm_out_a", "norm_out_b", "norm_final")
WEIGHTS = ("meta_tokens", "rel_bias_table", "norm_in", "w_in", "sink_a", "norm_q_lat", "w_uq", "norm_kv_lat", "w_ukv",
           "norm_out_a", "norm_out_b", "w_out", "norm_final")
SMALL_ROWS, SMALL_COLS = 8, 1024


def _pack_small(d, loss=None):
    flat = [d[n].reshape(-1) for n in SMALL]
    if loss is not None:
        flat.append(loss.reshape(1))
    used = sum(f.shape[0] for f in flat)
    flat.append(jnp.zeros((SMALL_ROWS * SMALL_COLS - used,), F32))
    return jnp.concatenate(flat).reshape(SMALL_ROWS, SMALL_COLS)


def _unpack_small(p, like):
    flat, out, off = p.reshape(-1), {}, 0
    for n in SMALL:
        size = int(np.prod(like[n].shape))
        out[n] = flat[off:off + size].reshape(like[n].shape)
        off += size
    return out, flat[off]


def _split4(a, axis):
    size = a.shape[axis] // 4
    return jnp.stack([lax.slice_in_dim(a, k * size, (k + 1) * size, axis=axis) for k in range(4)])


def _train_step(cfg, x, target, w, m, v):
    depth = w["w_in"].shape[0]
    rest = tuple(n for n in MATRICES if n != "w_in")
    weights, splits, received = {}, {}, {}

    def gather(i, names, also=(), build=_gather_comm):
        def deliver(res):
            for n, g in zip(names, res):
                weights[i, n] = jnp.concatenate([g[k] for k in range(4)], axis=MAT_AXIS[n])

        return build([w[n][i].astype(BF16) for n in names] + list(also)), deliver

    def scatter(i, names, mats, also=()):
        for n in names:
            splits[i, n] = _split4(mats[i][n], MAT_AXIS[n])

        def deliver(res):
            for n, r in zip(names, res):
                received[i, n] = r

        return _scatter_comm([splits[i, n].astype(BF16) for n in names] + list(also)), deliver

    def rider(stage, i, mats):
        if stage == "bias_build":
            comm, deliver = gather(0, ("w_in",), also=[w["meta_tokens"]], build=_gather_halves_comm)

            def deliver_first(res):
                deliver(res)
                weights["meta"] = jnp.concatenate([res[1][k] for k in range(4)], axis=1)

            return comm, deliver_first
        if stage == "inproj_fwd" and i == 0:
            return gather(0, rest, build=_gather_halves_comm)
        if stage == "mla_fwd" and i + 1 < depth:
            return gather(i + 1, MATRICES)
        if stage == "mla_bwd" and i + 1 < depth:
            return scatter(i + 1, MATRICES, mats)
        if stage == "inproj_bwd" and i == 0:
            return scatter(0, rest, mats)
        if stage == "bias_grad":
            splits["meta"] = _split4(mats["meta_tokens"], 1)
            comm, deliver = scatter(0, ("w_in",), mats, also=[splits["meta"].astype(BF16)])

            def deliver_last(res):
                deliver(res)
                received["meta"] = res[1]

            return comm, deliver_last
        return None

    loss_local, grad_x, g, mats = _local_grads(cfg, x, target, lambda: weights["meta"], w["rel_bias_table"], {n: w[n] for n in SMALL},
                                               lambda i, n: weights[i, n], rider)

    k_me = 2 * lax.axis_index("x") + lax.axis_index("y")

    def core_sum(name, split, recv):
        own = lax.dynamic_index_in_dim(split, k_me, 0, keepdims=False)
        return _sum_parts(name, _rows_view(own), recv.reshape(3, -1, recv.shape[-1]))

    partial = [core_sum("meta_tokens", splits["meta"], received["meta"])]
    for n in MATRICES:
        partial.append(jnp.concatenate([core_sum(f"{n}_{i}", splits[i, n], received[i, n]) for i in range(depth)], axis=0))
    small_sum, sibling = _allreduce_and_swap(_pack_small(g, loss_local), partial)
    g_small, loss = _unpack_small(small_sum, {n: w[n] for n in SMALL})

    outs = {}
    for n, p_me, p_sib in zip(("meta_tokens", *MATRICES), partial, sibling):
        res = _adamw(n, _rows_view(w[n]), _rows_view(m[n]), _rows_view(v[n]), [p_me, p_sib])
        outs[n] = [r.reshape(w[n].shape) for r in res]
    res = _adamw("small", _pack_small(w), _pack_small(m), _pack_small(v), [_pack_small(g_small)])
    unpacked = [_unpack_small(r, {n: w[n] for n in SMALL})[0] for r in res]
    for n in SMALL:
        outs[n] = [u[n] for u in unpacked]

    result = [loss, grad_x]
    for field in range(4):
        result.extend(outs[n][field] for n in WEIGHTS)
    return tuple(result)


def kernel(x, meta_tokens, rel_bias_table, norm_in, w_in, sink_a, norm_q_lat, w_uq, norm_kv_lat, w_ukv, norm_out_a, norm_out_b, w_out, norm_final, loss_target, m_meta_tokens, m_rel_bias_table, m_norm_in, m_w_in, m_sink_a, m_norm_q_lat, m_w_uq, m_norm_kv_lat, m_w_ukv, m_norm_out_a, m_norm_out_b, m_w_out, m_norm_final, v_meta_tokens, v_rel_bias_table, v_norm_in, v_w_in, v_sink_a, v_norm_q_lat, v_w_uq, v_norm_kv_lat, v_w_ukv, v_norm_out_a, v_norm_out_b, v_w_out, v_norm_final):
    w = dict(zip(WEIGHTS, (meta_tokens, rel_bias_table, norm_in, w_in, sink_a, norm_q_lat, w_uq, norm_kv_lat, w_ukv, norm_out_a, norm_out_b, w_out, norm_final)))
    m = dict(zip(WEIGHTS, (m_meta_tokens, m_rel_bias_table, m_norm_in, m_w_in, m_sink_a, m_norm_q_lat, m_w_uq, m_norm_kv_lat, m_w_ukv, m_norm_out_a, m_norm_out_b, m_w_out, m_norm_final)))
    v = dict(zip(WEIGHTS, (v_meta_tokens, v_rel_bias_table, v_norm_in, v_w_in, v_sink_a, v_norm_q_lat, v_w_uq, v_norm_kv_lat, v_w_ukv, v_norm_out_a, v_norm_out_b, v_w_out, v_norm_final)))
    cfg = make_cfg(x.shape[0], x.shape[1])
    return _train_step(cfg, x, loss_target, w, m, v)
```
